```python
import math
import jax
import jax.numpy as jnp
from jax import lax
import numpy as np

D_MODEL = 1024
BATCH = 8
SEQ = 8192
DEPTH = 2

EPS = 1e-6
N_BRANCHES = 3
D_SSM = 3 * D_MODEL // 4
SSM_GROUP = 16
SSM_GROUPS = D_SSM // SSM_GROUP
SSM_STATE = 64
DT_MIN = 0.001
DT_MAX = 0.1
ATTN_HEAD_DIM = 64
ATTN_HEADS_PER_GROUP = 4
ATTN_CONFIGS = ((128, 1), (512, 4), (2048, 16))
N_ATTN_HEADS = ATTN_HEADS_PER_GROUP * len(ATTN_CONFIGS)
D_ATTN = N_ATTN_HEADS * ATTN_HEAD_DIM
ATTN_BLOCK = 128
NUM_BUCKETS = 32
REL_MAX_DISTANCE = 2048
NEG_INF = -1e30
MEM_LEN = 256
MEM_HEADS = 4
MEM_HEAD_DIM = D_MODEL // 8
D_MEM = MEM_HEADS * MEM_HEAD_DIM
D_IN = 2 * D_SSM + 4 * D_ATTN + 2 * D_MEM + N_BRANCHES * D_MODEL

kernel_name = "hybrid_s5_dilated_attn_memxattn_gated"


def rms_norm(x, g):
    x32 = x.astype(jnp.float32)
    y = x32 * lax.rsqrt(jnp.mean(x32 * x32, axis=-1, keepdims=True) + EPS)
    return (y * g.astype(jnp.float32)).astype(x.dtype)


def rel_bucket(dist):
    n = jnp.maximum(dist, 0)
    max_exact = NUM_BUCKETS // 2
    n_f = jnp.maximum(n, 1).astype(jnp.float32)
    large = max_exact + (jnp.log(n_f / max_exact) / math.log(REL_MAX_DISTANCE / max_exact)
                         * (NUM_BUCKETS - max_exact)).astype(jnp.int32)
    large = jnp.minimum(large, NUM_BUCKETS - 1)
    return jnp.where(n < max_exact, n, large)


def s5_ssm(u, lam_re, lam_im, log_dt, b_re, b_im, c_re, c_im, d):
    B, L, _ = u.shape
    f32 = jnp.float32
    u32 = u.astype(f32).reshape(B, L, SSM_GROUPS, SSM_GROUP)
    lre, lim = lam_re.astype(f32), lam_im.astype(f32)
    dt = jnp.exp(log_dt.astype(f32))[:, None]
    mag = jnp.exp(lre * dt)
    abar_re, abar_im = mag * jnp.cos(lim * dt), mag * jnp.sin(lim * dt)
    den = lre * lre + lim * lim
    nr, ni = abar_re - 1.0, abar_im
    f_re = (nr * lre + ni * lim) / den
    f_im = (ni * lre - nr * lim) / den
    br, bi = b_re.astype(f32), b_im.astype(f32)
    bbar_re = f_re[..., None] * br - f_im[..., None] * bi
    bbar_im = f_re[..., None] * bi + f_im[..., None] * br
    bu_re = jnp.einsum('blgh,gph->blgp', u32, bbar_re)
    bu_im = jnp.einsum('blgh,gph->blgp', u32, bbar_im)
    a_re = jnp.broadcast_to(abar_re, (L, SSM_GROUPS, SSM_STATE))
    a_im = jnp.broadcast_to(abar_im, (L, SSM_GROUPS, SSM_STATE))

    def combine(e1, e2):
        a1r, a1i, b1r, b1i = e1
        a2r, a2i, b2r, b2i = e2
        return (a1r * a2r - a1i * a2i,
                a1r * a2i + a1i * a2r,
                a2r * b1r - a2i * b1i + b2r,
                a2r * b1i + a2i * b1r + b2i)

    def scan_one(br_seq, bi_seq):
        _, _, xr, xi = lax.associative_scan(combine, (a_re, a_im, br_seq, bi_seq), axis=0)
        return xr, xi

    xr, xi = jax.vmap(scan_one)(bu_re, bu_im)
    y = (jnp.einsum('blgp,ghp->blgh', xr, c_re.astype(f32))
         - jnp.einsum('blgp,ghp->blgh', xi, c_im.astype(f32))
         + d.astype(f32).reshape(SSM_GROUPS, SSM_GROUP) * u32)
    return y.reshape(B, L, D_SSM).astype(u.dtype)


def dilated_window_attention(q, k, v, bias_tab, window, dilation):
    B, L, H, hd = q.shape
    r = dilation
    span = window // dilation
    unit = r * ATTN_BLOCK
    Lp = -(-L // unit) * unit
    M = Lp // r
    nb = M // ATTN_BLOCK
    pad = ((0, 0), (0, Lp - L), (0, 0), (0, 0))

    def to_blocks(a):
        a = jnp.pad(a, pad).reshape(B, M, r, H, hd).transpose(0, 2, 1, 3, 4)
        return a.reshape(B, r, nb, ATTN_BLOCK, H, hd)

    def with_prev(a):
        prev = jnp.pad(a[:, :, :-1], ((0, 0), (0, 0), (1, 0), (0, 0), (0, 0), (0, 0)))
        return jnp.concatenate([prev, a], axis=3)

    qb = to_blocks(q)
    kw = with_prev(to_blocks(k))
    vw = with_prev(to_blocks(v))
    s = jnp.einsum('brnqhd,brnkhd->brnhqk', qb, kw).astype(jnp.float32) * (hd ** -0.5)
    qi = jnp.arange(ATTN_BLOCK)[:, None]
    kj = jnp.arange(2 * ATTN_BLOCK)[None, :]
    delta = ATTN_BLOCK + qi - kj
    band = (delta >= 0) & (delta <= span)
    has_prev = (jnp.arange(nb) > 0)[:, None, None] | (kj >= ATTN_BLOCK)[None]
    valid = band[None] & has_prev
    bias = bias_tab[rel_bucket(jnp.maximum(delta, 0) * r)]
    s = s + bias.transpose(2, 0, 1).astype(jnp.float32)
    s = jnp.where(valid[:, None], s, NEG_INF)
    m = jnp.max(s, axis=-1, keepdims=True)
    p = jnp.exp(s - m)
    l = jnp.sum(p, axis=-1, keepdims=True)
    o = jnp.einsum('brnhqk,brnkhd->brnqhd', (p / l).astype(v.dtype), vw)
    lse = (m + jnp.log(l))[..., 0]
    o = o.reshape(B, r, M, H, hd).transpose(0, 2, 1, 3, 4).reshape(B, Lp, H, hd)[:, :L]
    lse = lse.transpose(0, 1, 2, 4, 3).reshape(B, r, M, H).transpose(0, 2, 1, 3).reshape(B, Lp, H)[:, :L]
    return o, lse


def _fwd_setup_inputs(seed: int = 0) -> dict:
    key = jax.random.key(seed)
    ks = jax.random.split(key, 32)
    f32 = jnp.float32

    def nrm(k, shape, scale):
        return jax.random.normal(k, shape, f32) * scale

    G, P, Hg = SSM_GROUPS, SSM_STATE, SSM_GROUP
    return {
        "x": nrm(ks[0], (BATCH, SEQ, D_MODEL), 1.0),
        "mem": nrm(ks[1], (BATCH, MEM_LEN, D_MODEL), 1.0),
        "norm_g": 1.0 + nrm(ks[2], (DEPTH, D_MODEL), 0.1),
        "mem_norm_g": 1.0 + nrm(ks[3], (DEPTH, D_MODEL), 0.1),
        "w_in": nrm(ks[4], (DEPTH, D_MODEL, D_IN), D_MODEL ** -0.5),
        "b_gate": nrm(ks[5], (DEPTH, N_BRANCHES * D_MODEL), 0.1),
        "ssm_lambda_re": -0.5 + nrm(ks[6], (DEPTH, G, P), 0.01),
        "ssm_lambda_im": math.pi * jnp.arange(P, dtype=f32) + nrm(ks[7], (DEPTH, G, P), 0.01),
        "ssm_log_dt": jax.random.uniform(ks[8], (DEPTH, G), f32, math.log(DT_MIN), math.log(DT_MAX)),
        "ssm_b_re": nrm(ks[9], (DEPTH, G, P, Hg), (0.5 / Hg) ** 0.5),
        "ssm_b_im": nrm(ks[10], (DEPTH, G, P, Hg), (0.5 / Hg) ** 0.5),
        "ssm_c_re": nrm(ks[11], (DEPTH, G, Hg, P), (0.5 / P) ** 0.5),
        "ssm_c_im": nrm(ks[12], (DEPTH, G, Hg, P), (0.5 / P) ** 0.5),
        "ssm_d": nrm(ks[13], (DEPTH, D_SSM), 0.5),
        "w_glu": nrm(ks[14], (DEPTH, D_SSM, D_SSM), D_SSM ** -0.5),
        "b_glu": nrm(ks[15], (DEPTH, D_SSM), 0.1),
        "w_mem_kv": nrm(ks[16], (DEPTH, D_MODEL, 2 * D_MEM), D_MODEL ** -0.5),
        "w_br_ssm": nrm(ks[17], (DEPTH, D_SSM, D_MODEL), D_SSM ** -0.5),
        "w_br_attn": nrm(ks[18], (DEPTH, D_ATTN, D_MODEL), D_ATTN ** -0.5),
        "w_br_mem": nrm(ks[19], (DEPTH, D_MEM, D_MODEL), D_MEM ** -0.5),
        "w_out": nrm(ks[20], (DEPTH, D_MODEL, D_MODEL), D_MODEL ** -0.5),
        "rel_bias": nrm(ks[21], (NUM_BUCKETS, N_ATTN_HEADS), 0.5),
        "final_norm_g": 1.0 + nrm(ks[22], (D_MODEL,), 0.1),
    }


def _fwd_reference(x, mem, norm_g, mem_norm_g, w_in, b_gate, ssm_lambda_re, ssm_lambda_im, ssm_log_dt,
              ssm_b_re, ssm_b_im, ssm_c_re, ssm_c_im, ssm_d, w_glu, b_glu, w_mem_kv, w_br_ssm,
              w_br_attn, w_br_mem, w_out, rel_bias, final_norm_g):
    B, L, _ = x.shape
    sizes = (D_SSM, D_SSM, D_ATTN, D_ATTN, D_ATTN, D_ATTN, D_MEM, D_MEM, N_BRANCHES * D_MODEL)
    split_at = np.cumsum(sizes)[:-1].tolist()
    for layer in range(DEPTH):
        h = rms_norm(x, norm_g[layer])
        proj = h @ w_in[layer]
        u_ssm, z_ssm, q, k, v, z_attn, q_mem, z_mem, gate_logits = jnp.split(proj, split_at, axis=-1)

        y = s5_ssm(u_ssm, ssm_lambda_re[layer], ssm_lambda_im[layer], ssm_log_dt[layer],
                   ssm_b_re[layer], ssm_b_im[layer], ssm_c_re[layer], ssm_c_im[layer], ssm_d[layer])
        y = jax.nn.gelu(y)
        y = y * jax.nn.sigmoid(y @ w_glu[layer] + b_glu[layer])
        o_ssm = y * jax.nn.silu(z_ssm)

        qh = q.reshape(B, L, N_ATTN_HEADS, ATTN_HEAD_DIM)
        kh = k.reshape(B, L, N_ATTN_HEADS, ATTN_HEAD_DIM)
        vh = v.reshape(B, L, N_ATTN_HEADS, ATTN_HEAD_DIM)
        outs, lses = [], []
        for g, (window, dilation) in enumerate(ATTN_CONFIGS):
            sl = slice(g * ATTN_HEADS_PER_GROUP, (g + 1) * ATTN_HEADS_PER_GROUP)
            o_g, lse_g = dilated_window_attention(qh[:, :, sl], kh[:, :, sl], vh[:, :, sl],
                                                  rel_bias[:, sl], window, dilation)
            outs.append(o_g)
            lses.append(lse_g)
        alpha = jax.nn.softmax(jnp.stack(lses, axis=0), axis=0)
        o_attn = jnp.concatenate([o_g * alpha[g][..., None].astype(o_g.dtype) for g, o_g in enumerate(outs)],
                                 axis=2).reshape(B, L, D_ATTN)
        o_attn = o_attn * jax.nn.silu(z_attn)

        kv_mem = rms_norm(mem, mem_norm_g[layer]) @ w_mem_kv[layer]
        k_mem, v_mem = jnp.split(kv_mem, 2, axis=-1)
        k_mem = k_mem.reshape(B, MEM_LEN, MEM_HEADS, MEM_HEAD_DIM)
        v_mem = v_mem.reshape(B, MEM_LEN, MEM_HEADS, MEM_HEAD_DIM)
        qm = q_mem.reshape(B, L, MEM_HEADS, MEM_HEAD_DIM)
        s_mem = jnp.einsum('blhd,bmhd->bhlm', qm, k_mem).astype(jnp.float32) * (MEM_HEAD_DIM ** -0.5)
        p_mem = jax.nn.softmax(s_mem, axis=-1).astype(v_mem.dtype)
        o_mem = jnp.einsum('bhlm,bmhd->blhd', p_mem, v_mem).reshape(B, L, D_MEM)
        o_mem = o_mem * jax.nn.silu(z_mem)

        gates = jax.nn.sigmoid((gate_logits + b_gate[layer]).astype(jnp.float32))
        gates = gates.reshape(B, L, N_BRANCHES, D_MODEL).astype(x.dtype)
        merged = (gates[:, :, 0] * (o_ssm @ w_br_ssm[layer])
                  + gates[:, :, 1] * (o_attn @ w_br_attn[layer])
                  + gates[:, :, 2] * (o_mem @ w_br_mem[layer]))
        x = x + merged @ w_out[layer]
    return rms_norm(x, final_norm_g)


import jax as _jax
import jax.numpy as _jnp

TWIN_FORMAT = 'train_step'
FWD_PARAMS = ['x', 'mem', 'norm_g', 'mem_norm_g', 'w_in', 'b_gate', 'ssm_lambda_re', 'ssm_lambda_im', 'ssm_log_dt', 'ssm_b_re', 'ssm_b_im', 'ssm_c_re', 'ssm_c_im', 'ssm_d', 'w_glu', 'b_glu', 'w_mem_kv', 'w_br_ssm', 'w_br_attn', 'w_br_mem', 'w_out', 'rel_bias', 'final_norm_g']
TWIN_WEIGHTS = ['norm_g', 'mem_norm_g', 'w_in', 'b_gate', 'ssm_lambda_re', 'ssm_lambda_im', 'ssm_log_dt', 'ssm_b_re', 'ssm_b_im', 'ssm_c_re', 'ssm_c_im', 'ssm_d', 'w_glu', 'b_glu', 'w_mem_kv', 'w_br_ssm', 'w_br_attn', 'w_br_mem', 'w_out', 'rel_bias', 'final_norm_g']
TWIN_DIFF_INPUT = 'x'
TWIN_INPUTS = ['x', 'mem', 'norm_g', 'mem_norm_g', 'w_in', 'b_gate', 'ssm_lambda_re', 'ssm_lambda_im', 'ssm_log_dt', 'ssm_b_re', 'ssm_b_im', 'ssm_c_re', 'ssm_c_im', 'ssm_d', 'w_glu', 'b_glu', 'w_mem_kv', 'w_br_ssm', 'w_br_attn', 'w_br_mem', 'w_out', 'rel_bias', 'final_norm_g', 'loss_target', 'm_norm_g', 'm_mem_norm_g', 'm_w_in', 'm_b_gate', 'm_ssm_lambda_re', 'm_ssm_lambda_im', 'm_ssm_log_dt', 'm_ssm_b_re', 'm_ssm_b_im', 'm_ssm_c_re', 'm_ssm_c_im', 'm_ssm_d', 'm_w_glu', 'm_b_glu', 'm_w_mem_kv', 'm_w_br_ssm', 'm_w_br_attn', 'm_w_br_mem', 'm_w_out', 'm_rel_bias', 'm_final_norm_g', 'v_norm_g', 'v_mem_norm_g', 'v_w_in', 'v_b_gate', 'v_ssm_lambda_re', 'v_ssm_lambda_im', 'v_ssm_log_dt', 'v_ssm_b_re', 'v_ssm_b_im', 'v_ssm_c_re', 'v_ssm_c_im', 'v_ssm_d', 'v_w_glu', 'v_b_glu', 'v_w_mem_kv', 'v_w_br_ssm', 'v_w_br_attn', 'v_w_br_mem', 'v_w_out', 'v_rel_bias', 'v_final_norm_g']
TWIN_OUTPUTS = ['loss', 'grad_x', 'grad_norm_g', 'grad_mem_norm_g', 'grad_w_in', 'grad_b_gate', 'grad_ssm_lambda_re', 'grad_ssm_lambda_im', 'grad_ssm_log_dt', 'grad_ssm_b_re', 'grad_ssm_b_im', 'grad_ssm_c_re', 'grad_ssm_c_im', 'grad_ssm_d', 'grad_w_glu', 'grad_b_glu', 'grad_w_mem_kv', 'grad_w_br_ssm', 'grad_w_br_attn', 'grad_w_br_mem', 'grad_w_out', 'grad_rel_bias', 'grad_final_norm_g', 'delta_norm_g', 'delta_mem_norm_g', 'delta_w_in', 'delta_b_gate', 'delta_ssm_lambda_re', 'delta_ssm_lambda_im', 'delta_ssm_log_dt', 'delta_ssm_b_re', 'delta_ssm_b_im', 'delta_ssm_c_re', 'delta_ssm_c_im', 'delta_ssm_d', 'delta_w_glu', 'delta_b_glu', 'delta_w_mem_kv', 'delta_w_br_ssm', 'delta_w_br_attn', 'delta_w_br_mem', 'delta_w_out', 'delta_rel_bias', 'delta_final_norm_g', 'new_m_norm_g', 'new_m_mem_norm_g', 'new_m_w_in', 'new_m_b_gate', 'new_m_ssm_lambda_re', 'new_m_ssm_lambda_im', 'new_m_ssm_log_dt', 'new_m_ssm_b_re', 'new_m_ssm_b_im', 'new_m_ssm_c_re', 'new_m_ssm_c_im', 'new_m_ssm_d', 'new_m_w_glu', 'new_m_b_glu', 'new_m_w_mem_kv', 'new_m_w_br_ssm', 'new_m_w_br_attn', 'new_m_w_br_mem', 'new_m_w_out', 'new_m_rel_bias', 'new_m_final_norm_g', 'new_v_norm_g', 'new_v_mem_norm_g', 'new_v_w_in', 'new_v_b_gate', 'new_v_ssm_lambda_re', 'new_v_ssm_lambda_im', 'new_v_ssm_log_dt', 'new_v_ssm_b_re', 'new_v_ssm_b_im', 'new_v_ssm_c_re', 'new_v_ssm_c_im', 'new_v_ssm_d', 'new_v_w_glu', 'new_v_b_glu', 'new_v_w_mem_kv', 'new_v_w_br_ssm', 'new_v_w_br_attn', 'new_v_w_br_mem', 'new_v_w_out', 'new_v_rel_bias', 'new_v_final_norm_g']
TWIN_LEAF_KINDS = {'loss': 'loss', 'grad_x': 'grad_x', 'grad_norm_g': 'grad_w', 'grad_mem_norm_g': 'grad_w', 'grad_w_in': 'grad_w', 'grad_b_gate': 'grad_w', 'grad_ssm_lambda_re': 'grad_w', 'grad_ssm_lambda_im': 'grad_w', 'grad_ssm_log_dt': 'grad_w', 'grad_ssm_b_re': 'grad_w', 'grad_ssm_b_im': 'grad_w', 'grad_ssm_c_re': 'grad_w', 'grad_ssm_c_im': 'grad_w', 'grad_ssm_d': 'grad_w', 'grad_w_glu': 'grad_w', 'grad_b_glu': 'grad_w', 'grad_w_mem_kv': 'grad_w', 'grad_w_br_ssm': 'grad_w', 'grad_w_br_attn': 'grad_w', 'grad_w_br_mem': 'grad_w', 'grad_w_out': 'grad_w', 'grad_rel_bias': 'grad_w', 'grad_final_norm_g': 'grad_w', 'delta_norm_g': 'delta_w', 'delta_mem_norm_g': 'delta_w', 'delta_w_in': 'delta_w', 'delta_b_gate': 'delta_w', 'delta_ssm_lambda_re': 'delta_w', 'delta_ssm_lambda_im': 'delta_w', 'delta_ssm_log_dt': 'delta_w', 'delta_ssm_b_re': 'delta_w', 'delta_ssm_b_im': 'delta_w', 'delta_ssm_c_re': 'delta_w', 'delta_ssm_c_im': 'delta_w', 'delta_ssm_d': 'delta_w', 'delta_w_glu': 'delta_w', 'delta_b_glu': 'delta_w', 'delta_w_mem_kv': 'delta_w', 'delta_w_br_ssm': 'delta_w', 'delta_w_br_attn': 'delta_w', 'delta_w_br_mem': 'delta_w', 'delta_w_out': 'delta_w', 'delta_rel_bias': 'delta_w', 'delta_final_norm_g': 'delta_w', 'new_m_norm_g': 'new_m', 'new_m_mem_norm_g': 'new_m', 'new_m_w_in': 'new_m', 'new_m_b_gate': 'new_m', 'new_m_ssm_lambda_re': 'new_m', 'new_m_ssm_lambda_im': 'new_m', 'new_m_ssm_log_dt': 'new_m', 'new_m_ssm_b_re': 'new_m', 'new_m_ssm_b_im': 'new_m', 'new_m_ssm_c_re': 'new_m', 'new_m_ssm_c_im': 'new_m', 'new_m_ssm_d': 'new_m', 'new_m_w_glu': 'new_m', 'new_m_b_glu': 'new_m', 'new_m_w_mem_kv': 'new_m', 'new_m_w_br_ssm': 'new_m', 'new_m_w_br_attn': 'new_m', 'new_m_w_br_mem': 'new_m', 'new_m_w_out': 'new_m', 'new_m_rel_bias': 'new_m', 'new_m_final_norm_g': 'new_m', 'new_v_norm_g': 'new_v', 'new_v_mem_norm_g': 'new_v', 'new_v_w_in': 'new_v', 'new_v_b_gate': 'new_v', 'new_v_ssm_lambda_re': 'new_v', 'new_v_ssm_lambda_im': 'new_v', 'new_v_ssm_log_dt': 'new_v', 'new_v_ssm_b_re': 'new_v', 'new_v_ssm_b_im': 'new_v', 'new_v_ssm_c_re': 'new_v', 'new_v_ssm_c_im': 'new_v', 'new_v_ssm_d': 'new_v', 'new_v_w_glu': 'new_v', 'new_v_b_glu': 'new_v', 'new_v_w_mem_kv': 'new_v', 'new_v_w_br_ssm': 'new_v', 'new_v_w_br_attn': 'new_v', 'new_v_w_br_mem': 'new_v', 'new_v_w_out': 'new_v', 'new_v_rel_bias': 'new_v', 'new_v_final_norm_g': 'new_v'}


def _forward(args):
    return _fwd_reference(*[args[k] for k in FWD_PARAMS])


def _output_shape():
    def fwd():
        inp = _fwd_setup_inputs(0)
        return _fwd_reference(*[inp[k] for k in FWD_PARAMS])
    out = _jax.eval_shape(fwd)
    return out.shape, out.dtype

N_MICROBATCH = 1
ADAM_LR = 0.001
ADAM_B1 = 0.9
ADAM_B2 = 0.999
ADAM_EPS = 1e-08
ADAM_WD = 0.01
ADAM_STEP = 10
PER_EXAMPLE_BATCH_AXIS = {'x': 0, 'mem': 0, 'loss_target': 0}
SHARED_INPUTS = []
_WEIGHT_DTYPES = {'norm_g': _jnp.float32, 'mem_norm_g': _jnp.float32, 'w_in': _jnp.float32, 'b_gate': _jnp.float32, 'ssm_lambda_re': _jnp.float32, 'ssm_lambda_im': _jnp.float32, 'ssm_log_dt': _jnp.float32, 'ssm_b_re': _jnp.float32, 'ssm_b_im': _jnp.float32, 'ssm_c_re': _jnp.float32, 'ssm_c_im': _jnp.float32, 'ssm_d': _jnp.float32, 'w_glu': _jnp.float32, 'b_glu': _jnp.float32, 'w_mem_kv': _jnp.float32, 'w_br_ssm': _jnp.float32, 'w_br_attn': _jnp.float32, 'w_br_mem': _jnp.float32, 'w_out': _jnp.float32, 'rel_bias': _jnp.float32, 'final_norm_g': _jnp.float32}
MOMENT_SCALE = {'norm_g': 3.116804e-02, 'mem_norm_g': 1.404945e-02, 'w_in': 1.071316e-02, 'b_gate': 4.279159e-03, 'ssm_lambda_re': 1.799777e-03, 'ssm_lambda_im': 1.591974e-03, 'ssm_log_dt': 1.363342e+00, 'ssm_b_re': 1.067948e-03, 'ssm_b_im': 1.064160e-03, 'ssm_c_re': 2.148908e-03, 'ssm_c_im': 2.132537e-03, 'ssm_d': 3.218802e-02, 'w_glu': 2.469924e-03, 'b_glu': 7.740896e-03, 'w_mem_kv': 1.268752e-02, 'w_br_ssm': 1.419533e-02, 'w_br_attn': 9.993476e-03, 'w_br_mem': 9.091136e-03, 'w_out': 1.912052e-02, 'rel_bias': 1.652371e-02, 'final_norm_g': 6.434778e+01}


def _to_microbatches(a, axis):
    t = _jnp.moveaxis(a, axis, 0)
    t = t.reshape((N_MICROBATCH, t.shape[0] // N_MICROBATCH) + t.shape[1:])
    return _jnp.moveaxis(t, 1, axis + 1)


def setup_inputs(seed: int = 0) -> dict:
    inp = _fwd_setup_inputs(seed)
    key = _jax.random.fold_in(_jax.random.key(seed), 7919)
    shape, _ = _output_shape()
    out = dict(inp)
    out["loss_target"] = _jax.random.normal(_jax.random.fold_in(key, 0), shape, _jnp.float32)
    for i, name in enumerate(TWIN_WEIGHTS):
        w = inp[name].astype(_jnp.float32)
        if MOMENT_SCALE is None:
            s = _jnp.sqrt(_jnp.mean(_jnp.square(w)) + 1e-30)
        else:
            s = MOMENT_SCALE[name]
        km, kv = _jax.random.split(_jax.random.fold_in(key, i + 1))
        out[name] = w
        out["m_" + name] = s * _jax.random.normal(km, w.shape, _jnp.float32)
        out["v_" + name] = (s * s) * _jax.random.uniform(kv, w.shape, _jnp.float32, 0.5, 1.5)
    if N_MICROBATCH > 1:
        for name, axis in PER_EXAMPLE_BATCH_AXIS.items():
            out[name] = _to_microbatches(out[name], axis)
    return {'x': out['x'], 'mem': out['mem'], 'norm_g': out['norm_g'], 'mem_norm_g': out['mem_norm_g'], 'w_in': out['w_in'], 'b_gate': out['b_gate'], 'ssm_lambda_re': out['ssm_lambda_re'], 'ssm_lambda_im': out['ssm_lambda_im'], 'ssm_log_dt': out['ssm_log_dt'], 'ssm_b_re': out['ssm_b_re'], 'ssm_b_im': out['ssm_b_im'], 'ssm_c_re': out['ssm_c_re'], 'ssm_c_im': out['ssm_c_im'], 'ssm_d': out['ssm_d'], 'w_glu': out['w_glu'], 'b_glu': out['b_glu'], 'w_mem_kv': out['w_mem_kv'], 'w_br_ssm': out['w_br_ssm'], 'w_br_attn': out['w_br_attn'], 'w_br_mem': out['w_br_mem'], 'w_out': out['w_out'], 'rel_bias': out['rel_bias'], 'final_norm_g': out['final_norm_g'], 'loss_target': out['loss_target'], 'm_norm_g': out['m_norm_g'], 'm_mem_norm_g': out['m_mem_norm_g'], 'm_w_in': out['m_w_in'], 'm_b_gate': out['m_b_gate'], 'm_ssm_lambda_re': out['m_ssm_lambda_re'], 'm_ssm_lambda_im': out['m_ssm_lambda_im'], 'm_ssm_log_dt': out['m_ssm_log_dt'], 'm_ssm_b_re': out['m_ssm_b_re'], 'm_ssm_b_im': out['m_ssm_b_im'], 'm_ssm_c_re': out['m_ssm_c_re'], 'm_ssm_c_im': out['m_ssm_c_im'], 'm_ssm_d': out['m_ssm_d'], 'm_w_glu': out['m_w_glu'], 'm_b_glu': out['m_b_glu'], 'm_w_mem_kv': out['m_w_mem_kv'], 'm_w_br_ssm': out['m_w_br_ssm'], 'm_w_br_attn': out['m_w_br_attn'], 'm_w_br_mem': out['m_w_br_mem'], 'm_w_out': out['m_w_out'], 'm_rel_bias': out['m_rel_bias'], 'm_final_norm_g': out['m_final_norm_g'], 'v_norm_g': out['v_norm_g'], 'v_mem_norm_g': out['v_mem_norm_g'], 'v_w_in': out['v_w_in'], 'v_b_gate': out['v_b_gate'], 'v_ssm_lambda_re': out['v_ssm_lambda_re'], 'v_ssm_lambda_im': out['v_ssm_lambda_im'], 'v_ssm_log_dt': out['v_ssm_log_dt'], 'v_ssm_b_re': out['v_ssm_b_re'], 'v_ssm_b_im': out['v_ssm_b_im'], 'v_ssm_c_re': out['v_ssm_c_re'], 'v_ssm_c_im': out['v_ssm_c_im'], 'v_ssm_d': out['v_ssm_d'], 'v_w_glu': out['v_w_glu'], 'v_b_glu': out['v_b_glu'], 'v_w_mem_kv': out['v_w_mem_kv'], 'v_w_br_ssm': out['v_w_br_ssm'], 'v_w_br_attn': out['v_w_br_attn'], 'v_w_br_mem': out['v_w_br_mem'], 'v_w_out': out['v_w_out'], 'v_rel_bias': out['v_rel_bias'], 'v_final_norm_g': out['v_final_norm_g']}


def _loss(weights, diff, rest, loss_target):
    with _jax.named_scope("forward"):
        args = {**rest, TWIN_DIFF_INPUT: diff, **{k: w.astype(_WEIGHT_DTYPES[k]) for k, w in weights.items()}}
        y = _forward(args)
    with _jax.named_scope("loss_head"):
        err = _jnp.square(y.astype(_jnp.float32) - loss_target)
        return 0.5 * _jnp.sum(_jnp.mean(err, axis=-1)) if err.ndim else 0.5 * err


def _adamw(w, g, m, v):
    m = ADAM_B1 * m + (1.0 - ADAM_B1) * g
    v = ADAM_B2 * v + (1.0 - ADAM_B2) * _jnp.square(g)
    m_hat = m / (1.0 - ADAM_B1 ** ADAM_STEP)
    v_hat = v / (1.0 - ADAM_B2 ** ADAM_STEP)
    delta = -ADAM_LR * (m_hat / (_jnp.sqrt(v_hat) + ADAM_EPS) + ADAM_WD * w)
    return delta, m, v


def reference(x, mem, norm_g, mem_norm_g, w_in, b_gate, ssm_lambda_re, ssm_lambda_im, ssm_log_dt, ssm_b_re, ssm_b_im, ssm_c_re, ssm_c_im, ssm_d, w_glu, b_glu, w_mem_kv, w_br_ssm, w_br_attn, w_br_mem, w_out, rel_bias, final_norm_g, loss_target, m_norm_g, m_mem_norm_g, m_w_in, m_b_gate, m_ssm_lambda_re, m_ssm_lambda_im, m_ssm_log_dt, m_ssm_b_re, m_ssm_b_im, m_ssm_c_re, m_ssm_c_im, m_ssm_d, m_w_glu, m_b_glu, m_w_mem_kv, m_w_br_ssm, m_w_br_attn, m_w_br_mem, m_w_out, m_rel_bias, m_final_norm_g, v_norm_g, v_mem_norm_g, v_w_in, v_b_gate, v_ssm_lambda_re, v_ssm_lambda_im, v_ssm_log_dt, v_ssm_b_re, v_ssm_b_im, v_ssm_c_re, v_ssm_c_im, v_ssm_d, v_w_glu, v_b_glu, v_w_mem_kv, v_w_br_ssm, v_w_br_attn, v_w_br_mem, v_w_out, v_rel_bias, v_final_norm_g):
    given = dict(x=x, mem=mem, norm_g=norm_g, mem_norm_g=mem_norm_g, w_in=w_in, b_gate=b_gate, ssm_lambda_re=ssm_lambda_re, ssm_lambda_im=ssm_lambda_im, ssm_log_dt=ssm_log_dt, ssm_b_re=ssm_b_re, ssm_b_im=ssm_b_im, ssm_c_re=ssm_c_re, ssm_c_im=ssm_c_im, ssm_d=ssm_d, w_glu=w_glu, b_glu=b_glu, w_mem_kv=w_mem_kv, w_br_ssm=w_br_ssm, w_br_attn=w_br_attn, w_br_mem=w_br_mem, w_out=w_out, rel_bias=rel_bias, final_norm_g=final_norm_g, loss_target=loss_target, m_norm_g=m_norm_g, m_mem_norm_g=m_mem_norm_g, m_w_in=m_w_in, m_b_gate=m_b_gate, m_ssm_lambda_re=m_ssm_lambda_re, m_ssm_lambda_im=m_ssm_lambda_im, m_ssm_log_dt=m_ssm_log_dt, m_ssm_b_re=m_ssm_b_re, m_ssm_b_im=m_ssm_b_im, m_ssm_c_re=m_ssm_c_re, m_ssm_c_im=m_ssm_c_im, m_ssm_d=m_ssm_d, m_w_glu=m_w_glu, m_b_glu=m_b_glu, m_w_mem_kv=m_w_mem_kv, m_w_br_ssm=m_w_br_ssm, m_w_br_attn=m_w_br_attn, m_w_br_mem=m_w_br_mem, m_w_out=m_w_out, m_rel_bias=m_rel_bias, m_final_norm_g=m_final_norm_g, v_norm_g=v_norm_g, v_mem_norm_g=v_mem_norm_g, v_w_in=v_w_in, v_b_gate=v_b_gate, v_ssm_lambda_re=v_ssm_lambda_re, v_ssm_lambda_im=v_ssm_lambda_im, v_ssm_log_dt=v_ssm_log_dt, v_ssm_b_re=v_ssm_b_re, v_ssm_b_im=v_ssm_b_im, v_ssm_c_re=v_ssm_c_re, v_ssm_c_im=v_ssm_c_im, v_ssm_d=v_ssm_d, v_w_glu=v_w_glu, v_b_glu=v_b_glu, v_w_mem_kv=v_w_mem_kv, v_w_br_ssm=v_w_br_ssm, v_w_br_attn=v_w_br_attn, v_w_br_mem=v_w_br_mem, v_w_out=v_w_out, v_rel_bias=v_rel_bias, v_final_norm_g=v_final_norm_g)
    weights = {n: given[n] for n in TWIN_WEIGHTS}
    shared = {n: given[n] for n in SHARED_INPUTS}
    per_example = {n: given[n] for n in ['x', 'mem']}
    grad_fn = _jax.value_and_grad(_loss, argnums=(0, 1))

    def one_microbatch(ex, loss_target):
        ex = dict(ex)
        diff = ex.pop(TWIN_DIFF_INPUT)
        return grad_fn(weights, diff, {**shared, **ex}, loss_target)

    if N_MICROBATCH == 1:
        loss, (grad_w, grad_x) = one_microbatch(per_example, given["loss_target"])
    else:
        def body(carry, xs):
            loss_sum, grad_sum = carry
            l_k, (gw_k, gx_k) = one_microbatch(xs[0], xs[1])
            with _jax.named_scope("update"):
                return (loss_sum + l_k, _jax.tree.map(_jnp.add, grad_sum, gw_k)), gx_k

        init = (_jnp.zeros((), _jnp.float32), _jax.tree.map(_jnp.zeros_like, weights))
        (loss, grad_w), grad_x = _jax.lax.scan(body, init, (per_example, given["loss_target"]))
    with _jax.named_scope("update"):
        delta_w, new_m, new_v = {}, {}, {}
        for n in TWIN_WEIGHTS:
            delta_w[n], new_m[n], new_v[n] = _adamw(weights[n], grad_w[n], given["m_" + n], given["v_" + n])
    return (loss, grad_x, *[grad_w[n] for n in TWIN_WEIGHTS], *[delta_w[n] for n in TWIN_WEIGHTS],
            *[new_m[n] for n in TWIN_WEIGHTS], *[new_v[n] for n in TWIN_WEIGHTS])
```

```python
import math

import jax
import jax.numpy as jnp
import numpy as np
from jax import lax
from jax.experimental import pallas as pl
from jax.experimental.pallas import tpu as pltpu

F32 = jnp.float32
BF16 = jnp.bfloat16
MESH = pl.DeviceIdType.MESH
HBM = pl.BlockSpec(memory_space=pltpu.HBM)

EPS = 1e-6
SSM_GROUP = 16
SSM_STATE = 64
ATTN_HEAD_DIM = 64
HEADS_PER_GROUP = 4
ATTN_CONFIGS = ((128, 1), (512, 4), (2048, 16))
ATTN_BLOCK = 128
NUM_BUCKETS = 32
REL_MAX_DISTANCE = 2048
NEG_INF = -1e30
MEM_HEADS = 4
ADAM_LR = 0.001
ADAM_B1 = 0.9
ADAM_B2 = 0.999
ADAM_EPS = 1e-08
ADAM_WD = 0.01
ADAM_STEP = 10

LANES = 128
SUBLANES = 8
VMEM_LIMIT_BYTES = 48 * 1024 * 1024
SSM_BLOCK_CH = 128
SSM_SEGMENTS = SUBLANES
SSM_CHUNK_STEPS = 128

N_CHIPS = 4
BIG = (("w_in", 2), ("w_glu", 1), ("w_mem_kv", 1), ("w_br_ssm", 2), ("w_br_attn", 2), ("w_br_mem", 2), ("w_out", 1))
SMALL = ("norm_g", "mem_norm_g", "b_gate", "ssm_lambda_re", "ssm_lambda_im", "ssm_log_dt", "ssm_b_re", "ssm_b_im",
         "ssm_c_re", "ssm_c_im", "ssm_d", "b_glu", "rel_bias", "final_norm_g")
WEIGHTS = ("norm_g", "mem_norm_g", "w_in", "b_gate", "ssm_lambda_re", "ssm_lambda_im", "ssm_log_dt", "ssm_b_re",
           "ssm_b_im", "ssm_c_re", "ssm_c_im", "ssm_d", "w_glu", "b_glu", "w_mem_kv", "w_br_ssm", "w_br_attn",
           "w_br_mem", "w_out", "rel_bias", "final_norm_g")
INPUTS = ("x", "mem") + WEIGHTS + ("loss_target",) + tuple("m_" + n for n in WEIGHTS) + tuple("v_" + n for n in WEIGHTS)


def _params(*sem):
    return pltpu.CompilerParams(dimension_semantics=sem, vmem_limit_bytes=VMEM_LIMIT_BYTES)


def _pick(dim, pref, align):
    if dim <= pref:
        return dim
    t = pref - pref % align
    while t >= align:
        if dim % t == 0:
            return t
        t -= align
    return dim


def _sigmoid(v):
    return 1.0 / (1.0 + jnp.exp(-v))


def _silu_and_grad(z):
    s = _sigmoid(z)
    return z * s, s * (1.0 + z * (1.0 - s))


_GELU_C = math.sqrt(2.0 / math.pi)


def _gelu_and_grad(y):
    inner = _GELU_C * (y + 0.044715 * y * y * y)
    t = jnp.tanh(inner)
    g = 0.5 * y * (1.0 + t)
    dg = 0.5 * (1.0 + t) + 0.5 * y * (1.0 - t * t) * _GELU_C * (1.0 + 3.0 * 0.044715 * y * y)
    return g, dg


def _dot(a, b, dims):
    return lax.dot_general(a, b, (dims, ((), ())), preferred_element_type=F32)


NN = ((1,), (0,))
NT = ((1,), (1,))
TN = ((0,), (0,))


def _matmul(a, b, *, mode, name, out_dtype=F32, add=None, split_a=False, tm=1024, tn=512, tk=2304):
    if mode == "tn":
        K, M = a.shape
    else:
        M, K = a.shape
    N = b.shape[0] if mode == "nt" else b.shape[1]
    tm = _pick(M, tm, LANES if mode == "tn" else SUBLANES)
    tn = _pick(N, tn, LANES)
    tk = _pick(K, tk, LANES)
    nk = K // tk
    dims = {"nn": NN, "nt": NT, "tn": TN}[mode]

    def body(*refs):
        if add is None:
            a_ref, b_ref, o_ref, acc_ref = refs
        else:
            a_ref, b_ref, add_ref, o_ref, acc_ref = refs
        k = pl.program_id(2)
        bv = b_ref[...].astype(BF16)
        if split_a:
            af = a_ref[...].astype(F32)
            hi = af.astype(BF16)
            lo = (af - hi.astype(F32)).astype(BF16)
            part = _dot(hi, bv, dims) + _dot(lo, bv, dims)
        else:
            part = _dot(a_ref[...].astype(BF16), bv, dims)

        @pl.when(k == 0)
        def _():
            acc_ref[...] = part

        @pl.when(k > 0)
        def _():
            acc_ref[...] += part

        @pl.when(k == nk - 1)
        def _():
            r = acc_ref[...]
            if add is not None:
                r = r + add_ref[...]
            o_ref[...] = r.astype(out_dtype)

    a_spec = pl.BlockSpec((tk, tm), lambda i, j, k: (k, i)) if mode == "tn" else pl.BlockSpec((tm, tk), lambda i, j, k: (i, k))
    b_spec = pl.BlockSpec((tn, tk), lambda i, j, k: (j, k)) if mode == "nt" else pl.BlockSpec((tk, tn), lambda i, j, k: (k, j))
    in_specs = [a_spec, b_spec]
    args = [a, b]
    if add is not None:
        in_specs.append(pl.BlockSpec((tm, tn), lambda i, j, k: (i, j)))
        args.append(add)
    return pl.pallas_call(
        body, name=name, grid=(M // tm, N // tn, nk), in_specs=in_specs,
        out_specs=pl.BlockSpec((tm, tn), lambda i, j, k: (i, j)),
        out_shape=jax.ShapeDtypeStruct((M, N), out_dtype),
        scratch_shapes=[pltpu.VMEM((tm, tn), F32)],
        compiler_params=_params("parallel", "parallel", "arbitrary"),
    )(*args)


def _rmsnorm(x, g, name):
    T, D = x.shape
    tm = _pick(T, 512, SUBLANES)

    def body(x_ref, g_ref, h_ref):
        xv = x_ref[...]
        r = lax.rsqrt(jnp.mean(xv * xv, axis=-1, keepdims=True) + EPS)
        h_ref[...] = (xv * r * g_ref[...]).astype(BF16)

    return pl.pallas_call(
        body, name=name, grid=(T // tm,),
        in_specs=[pl.BlockSpec((tm, D), lambda i: (i, 0)), pl.BlockSpec((1, D), lambda i: (0, 0))],
        out_specs=pl.BlockSpec((tm, D), lambda i: (i, 0)),
        out_shape=jax.ShapeDtypeStruct((T, D), BF16), compiler_params=_params("parallel"),
    )(x, g.reshape(1, D))


def _rmsnorm_bwd(x, g, dh, dres, name):
    T, D = x.shape
    tm = _pick(T, 512, SUBLANES)
    with_res = dres is not None

    def body(*refs):
        if with_res:
            x_ref, g_ref, dh_ref, dres_ref, dx_ref, dg_ref = refs
        else:
            x_ref, g_ref, dh_ref, dx_ref, dg_ref = refs
        xv = x_ref[...]
        dhv = dh_ref[...]
        r = lax.rsqrt(jnp.mean(xv * xv, axis=-1, keepdims=True) + EPS)
        dyg = dhv * g_ref[...]
        c = jnp.mean(dyg * xv, axis=-1, keepdims=True)
        dx = r * dyg - xv * (r * r * r) * c
        if with_res:
            dx = dx + dres_ref[...]
        dx_ref[...] = dx

        @pl.when(pl.program_id(0) == 0)
        def _():
            dg_ref[...] = jnp.zeros_like(dg_ref)

        dg_ref[...] += jnp.sum(dhv * xv * r, axis=0, keepdims=True)

    row = pl.BlockSpec((tm, D), lambda i: (i, 0))
    vec = pl.BlockSpec((1, D), lambda i: (0, 0))
    ins = [x, g.reshape(1, D), dh] + ([dres] if with_res else [])
    return pl.pallas_call(
        body, name=name, grid=(T // tm,), in_specs=[row, vec, row] + ([row] if with_res else []),
        out_specs=[row, vec],
        out_shape=[jax.ShapeDtypeStruct((T, D), F32), jax.ShapeDtypeStruct((1, D), F32)],
        compiler_params=_params("arbitrary"),
    )(*ins)


def _loss_head(x, g, target, name):
    T, D = x.shape
    tm = _pick(T, 512, SUBLANES)

    def body(x_ref, g_ref, t_ref, loss_ref, dx_ref, dg_ref):
        xv = x_ref[...]
        gv = g_ref[...]
        r = lax.rsqrt(jnp.mean(xv * xv, axis=-1, keepdims=True) + EPS)
        e = xv * r * gv - t_ref[...]
        dy = e * (1.0 / D)
        dyg = dy * gv
        c = jnp.mean(dyg * xv, axis=-1, keepdims=True)
        dx_ref[...] = r * dyg - xv * (r * r * r) * c

        @pl.when(pl.program_id(0) == 0)
        def _():
            loss_ref[...] = jnp.zeros_like(loss_ref)
            dg_ref[...] = jnp.zeros_like(dg_ref)

        loss_ref[...] += jnp.sum(e * e, axis=0, keepdims=True) * (0.5 / D)
        dg_ref[...] += jnp.sum(dy * xv * r, axis=0, keepdims=True)

    row = pl.BlockSpec((tm, D), lambda i: (i, 0))
    vec = pl.BlockSpec((1, D), lambda i: (0, 0))
    return pl.pallas_call(
        body, name=name, grid=(T // tm,), in_specs=[row, vec, row], out_specs=[vec, row, vec],
        out_shape=[jax.ShapeDtypeStruct((1, D), F32), jax.ShapeDtypeStruct((T, D), F32), jax.ShapeDtypeStruct((1, D), F32)],
        compiler_params=_params("arbitrary"),
    )(x, g.reshape(1, D), target)


def _ssm_disc_math(lre, lim, logdt, br, bi):
    dt = jnp.exp(logdt)
    mag = jnp.exp(lre * dt)
    ar = mag * jnp.cos(lim * dt)
    ai = mag * jnp.sin(lim * dt)
    den = lre * lre + lim * lim
    nr = ar - 1.0
    fr = (nr * lre + ai * lim) / den
    fi = (ai * lre - nr * lim) / den
    return ar, ai, fr[None] * br - fi[None] * bi, fr[None] * bi + fi[None] * br


def _ssm_disc(lre, lim, logdt, br, bi, name):
    def body(lre_ref, lim_ref, dt_ref, br_ref, bi_ref, ar_ref, ai_ref, bbr_ref, bbi_ref):
        ar, ai, bbr, bbi = _ssm_disc_math(lre_ref[...], lim_ref[...], dt_ref[...], br_ref[...], bi_ref[...])
        ar_ref[...] = ar
        ai_ref[...] = ai
        bbr_ref[...] = bbr
        bbi_ref[...] = bbi

    sd = jax.ShapeDtypeStruct
    return pl.pallas_call(
        body, name=name, out_shape=[sd(lre.shape, F32), sd(lre.shape, F32), sd(br.shape, F32), sd(br.shape, F32)],
    )(lre, lim, logdt, br, bi)


def _ssm_disc_bwd(lre, lim, logdt, br, bi, dar, dai, dbbr, dbbi, name):
    def body(lre_ref, lim_ref, dt_ref, br_ref, bi_ref, dar_ref, dai_ref, dbbr_ref, dbbi_ref,
             glre_ref, glim_ref, gdt_ref, gbr_ref, gbi_ref):
        _, vjp = jax.vjp(_ssm_disc_math, lre_ref[...], lim_ref[...], dt_ref[...], br_ref[...], bi_ref[...])
        glre, glim, gdt, gbr, gbi = vjp((dar_ref[...], dai_ref[...], dbbr_ref[...], dbbi_ref[...]))
        glre_ref[...] = glre
        glim_ref[...] = glim
        gdt_ref[...] = gdt
        gbr_ref[...] = gbr
        gbi_ref[...] = gbi

    sd = jax.ShapeDtypeStruct
    return pl.pallas_call(
        body, name=name,
        out_shape=[sd(lre.shape, F32), sd(lre.shape, F32), sd(logdt.shape, F32), sd(br.shape, F32), sd(br.shape, F32)],
    )(lre, lim, logdt, br, bi, dar, dai, dbbr, dbbi)


def _shift_segments(v, down):
    n = v.shape[0]
    rows = lax.broadcasted_iota(jnp.int32, v.shape, 0)
    if down:
        return jnp.where(rows >= 1, pltpu.roll(v, 1, 0), 0.0)
    return jnp.where(rows < n - 1, pltpu.roll(v, n - 1, 0), 0.0)


def _cpow(ar, ai, n):
    rr, ri = None, None
    pr, pi = ar, ai
    while n:
        if n & 1:
            if rr is None:
                rr, ri = pr, pi
            else:
                rr, ri = rr * pr - ri * pi, rr * pi + ri * pr
        n >>= 1
        if n:
            pr, pi = pr * pr - pi * pi, 2.0 * pr * pi
    return rr, ri


def _ssm_geometry(T, C):
    seg_steps = T // SSM_SEGMENTS
    kc = min(SSM_CHUNK_STEPS, seg_steps)
    return C // SSM_BLOCK_CH, seg_steps, kc, seg_steps // kc, SSM_SEGMENTS * kc


def _ssm_carries(u, dy, bmat, cmat, amat, *, reverse, name):
    src = dy if reverse else u
    T, C = src.shape
    nblk, seg_steps, kc, nchunk, rc = _ssm_geometry(T, C)
    half = bmat.shape[2] // 2

    def body(src_ref, w_ref, a_ref, out_ref, buf_ref, st_ref):
        c = pl.program_id(1)

        @pl.when(c == 0)
        def _():
            st_ref[...] = jnp.zeros_like(st_ref)

        if reverse:
            buf_ref[...] = _dot(src_ref[...].astype(BF16), w_ref[0], NT)
        else:
            buf_ref[...] = _dot(src_ref[...].astype(BF16), w_ref[0], NN)
        ar = a_ref[0, :, :half]
        ai = a_ref[0, :, half:]
        if reverse:
            ai = -ai

        def step(i, carry):
            xr, xi = carry
            k = (kc - 1 - i) if reverse else i
            row = pl.multiple_of(k * SUBLANES, SUBLANES)
            br = buf_ref[pl.ds(row, SUBLANES), :half]
            bi = buf_ref[pl.ds(row, SUBLANES), half:]
            return ar * xr - ai * xi + br, ar * xi + ai * xr + bi

        xr, xi = lax.fori_loop(0, kc, step, (st_ref[:, :half], st_ref[:, half:]), unroll=8)
        st_ref[:, :half] = xr
        st_ref[:, half:] = xi

        @pl.when(c == nchunk - 1)
        def _():
            pr, pi = _cpow(ar, ai, seg_steps)
            sr = jnp.zeros_like(xr)
            si = jnp.zeros_like(xi)
            for _ in range(SSM_SEGMENTS - 1):
                nr = xr + pr * sr - pi * si
                ni = xi + pr * si + pi * sr
                sr = _shift_segments(nr, not reverse)
                si = _shift_segments(ni, not reverse)
            out_ref[0, :, :half] = sr
            out_ref[0, :, half:] = si

    cidx = (lambda b, c: (nchunk - 1 - c, b)) if reverse else (lambda b, c: (c, b))
    w = cmat if reverse else bmat
    return pl.pallas_call(
        body, name=name, grid=(nblk, nchunk),
        in_specs=[pl.BlockSpec((rc, SSM_BLOCK_CH), cidx),
                  pl.BlockSpec((1,) + w.shape[1:], lambda b, c: (b, 0, 0)),
                  pl.BlockSpec((1, SUBLANES, 2 * half), lambda b, c: (b, 0, 0))],
        out_specs=pl.BlockSpec((1, SUBLANES, 2 * half), lambda b, c: (b, 0, 0)),
        out_shape=jax.ShapeDtypeStruct((nblk, SUBLANES, 2 * half), F32),
        scratch_shapes=[pltpu.VMEM((rc, 2 * half), F32), pltpu.VMEM((SUBLANES, 2 * half), F32)],
        compiler_params=_params("parallel", "arbitrary"),
    )(src, w, amat)


def _ssm_scan(u, bmat, cmat, amat, carries, dvec, name):
    T, C = u.shape
    nblk, seg_steps, kc, nchunk, rc = _ssm_geometry(T, C)
    half = bmat.shape[2] // 2

    def body(u_ref, b_ref, c_ref, a_ref, s_ref, d_ref, y_ref, x_ref, st_ref):
        c = pl.program_id(1)

        @pl.when(c == 0)
        def _():
            st_ref[...] = s_ref[0]

        uv = u_ref[...]
        x_ref[...] = _dot(uv.astype(BF16), b_ref[0], NN)
        ar = a_ref[0, :, :half]
        ai = a_ref[0, :, half:]

        def step(k, carry):
            xr, xi = carry
            row = pl.multiple_of(k * SUBLANES, SUBLANES)
            nr = ar * xr - ai * xi + x_ref[pl.ds(row, SUBLANES), :half]
            ni = ar * xi + ai * xr + x_ref[pl.ds(row, SUBLANES), half:]
            x_ref[pl.ds(row, SUBLANES), :half] = nr
            x_ref[pl.ds(row, SUBLANES), half:] = ni
            return nr, ni

        xr, xi = lax.fori_loop(0, kc, step, (st_ref[:, :half], st_ref[:, half:]), unroll=8)
        st_ref[:, :half] = xr
        st_ref[:, half:] = xi
        y_ref[...] = _dot(x_ref[...].astype(BF16), c_ref[0], NN) + d_ref[...] * uv

    return pl.pallas_call(
        body, name=name, grid=(nblk, nchunk),
        in_specs=[pl.BlockSpec((rc, SSM_BLOCK_CH), lambda b, c: (c, b)),
                  pl.BlockSpec((1,) + bmat.shape[1:], lambda b, c: (b, 0, 0)),
                  pl.BlockSpec((1,) + cmat.shape[1:], lambda b, c: (b, 0, 0)),
                  pl.BlockSpec((1, SUBLANES, 2 * half), lambda b, c: (b, 0, 0)),
                  pl.BlockSpec((1, SUBLANES, 2 * half), lambda b, c: (b, 0, 0)),
                  pl.BlockSpec((1, SSM_BLOCK_CH), lambda b, c: (0, b))],
        out_specs=[pl.BlockSpec((rc, SSM_BLOCK_CH), lambda b, c: (c, b)),
                   pl.BlockSpec((rc, 2 * half), lambda b, c: (c, b))],
        out_shape=[jax.ShapeDtypeStruct((T, C), F32), jax.ShapeDtypeStruct((T, nblk * 2 * half), F32)],
        scratch_shapes=[pltpu.VMEM((SUBLANES, 2 * half), F32)],
        compiler_params=_params("parallel", "arbitrary"),
    )(u, bmat, cmat, amat, carries, dvec)


def _ssm_scan_bwd(dy, u, xs, bmat, cmat, amat, carries, dvec, name):
    T, C = u.shape
    nblk, seg_steps, kc, nchunk, rc = _ssm_geometry(T, C)
    half = bmat.shape[2] // 2
    width = 2 * half

    def body(dy_ref, u_ref, x_ref, xp_ref, b_ref, c_ref, a_ref, s_ref, d_ref,
             du_ref, db_ref, dc_ref, da_ref, dd_ref, g_ref, st_ref, acc_ref):
        c = pl.program_id(1)

        @pl.when(c == 0)
        def _():
            st_ref[...] = s_ref[0]
            acc_ref[...] = jnp.zeros_like(acc_ref)
            db_ref[...] = jnp.zeros_like(db_ref)
            dc_ref[...] = jnp.zeros_like(dc_ref)
            dd_ref[...] = jnp.zeros_like(dd_ref)

        dyv = dy_ref[...]
        uv = u_ref[...]
        dyb = dyv.astype(BF16)
        g_ref[...] = _dot(dyb, c_ref[0], NT)
        ar = a_ref[0, :, :half]
        ai = a_ref[0, :, half:]

        def step(i, carry):
            gr, gi, sr, si = carry
            k = kc - 1 - i
            row = pl.multiple_of(k * SUBLANES, SUBLANES)
            nr = ar * gr + ai * gi + g_ref[pl.ds(row, SUBLANES), :half]
            ni = ar * gi - ai * gr + g_ref[pl.ds(row, SUBLANES), half:]
            g_ref[pl.ds(row, SUBLANES), :half] = nr
            g_ref[pl.ds(row, SUBLANES), half:] = ni
            prow = pl.multiple_of(jnp.maximum(k - 1, 0) * SUBLANES, SUBLANES)
            live = (k >= 1).astype(F32)
            xr = x_ref[pl.ds(prow, SUBLANES), :half] * live
            xi = x_ref[pl.ds(prow, SUBLANES), half:] * live
            return nr, ni, sr + xr * nr + xi * ni, si + xr * ni - xi * nr

        init = (st_ref[:, :half], st_ref[:, half:], acc_ref[:, :half], acc_ref[:, half:])
        gr, gi, sr, si = lax.fori_loop(0, kc, step, init, unroll=8)
        st_ref[:, :half] = gr
        st_ref[:, half:] = gi
        xpr = xp_ref[:, :half]
        xpi = xp_ref[:, half:]
        first = (c == nchunk - 1)
        xpr = jnp.where(first, _shift_segments(xpr, True), xpr)
        xpi = jnp.where(first, _shift_segments(xpi, True), xpi)
        acc_ref[:, :half] = sr + xpr * gr + xpi * gi
        acc_ref[:, half:] = si + xpr * gi - xpi * gr

        gb = g_ref[...].astype(BF16)
        du_ref[...] = _dot(gb, b_ref[0], NT) + dyv * d_ref[...]
        db_ref[0] += _dot(uv.astype(BF16), gb, TN)
        dc_ref[0] += _dot(dyb, x_ref[...].astype(BF16), TN)
        dd_ref[...] += jnp.sum(dyv * uv, axis=0, keepdims=True)

        @pl.when(c == nchunk - 1)
        def _():
            tot = jnp.sum(acc_ref[...], axis=0, keepdims=True)
            da_ref[0] = jnp.broadcast_to(tot, (SUBLANES, width))

    rev = lambda b, c: (nchunk - 1 - c, b)
    blk3 = lambda b, c: (b, 0, 0)
    prev_group = lambda b, c: (((nchunk - 1 - c) * kc - 1 + seg_steps) % seg_steps, b)
    sd = jax.ShapeDtypeStruct
    return pl.pallas_call(
        body, name=name, grid=(nblk, nchunk),
        in_specs=[pl.BlockSpec((rc, SSM_BLOCK_CH), rev), pl.BlockSpec((rc, SSM_BLOCK_CH), rev),
                  pl.BlockSpec((rc, width), rev), pl.BlockSpec((SUBLANES, width), prev_group),
                  pl.BlockSpec((1,) + bmat.shape[1:], blk3), pl.BlockSpec((1,) + cmat.shape[1:], blk3),
                  pl.BlockSpec((1, SUBLANES, width), blk3), pl.BlockSpec((1, SUBLANES, width), blk3),
                  pl.BlockSpec((1, SSM_BLOCK_CH), lambda b, c: (0, b))],
        out_specs=[pl.BlockSpec((rc, SSM_BLOCK_CH), rev), pl.BlockSpec((1, SSM_BLOCK_CH, width), blk3),
                   pl.BlockSpec((1, SSM_BLOCK_CH, width), blk3), pl.BlockSpec((1, SUBLANES, width), blk3),
                   pl.BlockSpec((1, SSM_BLOCK_CH), lambda b, c: (0, b))],
        out_shape=[sd((T, C), F32), sd((nblk, SSM_BLOCK_CH, width), F32), sd((nblk, SSM_BLOCK_CH, width), F32),
                   sd((nblk, SUBLANES, width), F32), sd((1, C), F32)],
        scratch_shapes=[pltpu.VMEM((rc, width), F32), pltpu.VMEM((SUBLANES, width), F32), pltpu.VMEM((SUBLANES, width), F32)],
        compiler_params=_params("parallel", "arbitrary"),
    )(dy, u, xs, xs, bmat, cmat, amat, carries, dvec)


def _ssm_post(y, uz, w_glu, b_glu, name):
    T, C = y.shape
    tm = _pick(T, 512, SUBLANES)

    def body(y_ref, z_ref, w_ref, b_ref, o_ref, a_ref):
        a, _ = _gelu_and_grad(y_ref[...])
        ab = a.astype(BF16)
        sg = _sigmoid(_dot(ab, w_ref[...], NN) + b_ref[...])
        sz, _ = _silu_and_grad(z_ref[...])
        o_ref[...] = (a * sg * sz).astype(BF16)
        a_ref[...] = ab

    row = pl.BlockSpec((tm, C), lambda i: (i, 0))
    return pl.pallas_call(
        body, name=name, grid=(T // tm,),
        in_specs=[row, pl.BlockSpec((tm, C), lambda i: (i, 1)), pl.BlockSpec((C, C), lambda i: (0, 0)),
                  pl.BlockSpec((1, C), lambda i: (0, 0))],
        out_specs=[row, row], out_shape=[jax.ShapeDtypeStruct((T, C), BF16)] * 2, compiler_params=_params("parallel"),
    )(y, uz, w_glu, b_glu.reshape(1, C))


def _ssm_post_bwd(do, y, uz, w_glu, b_glu, name):
    T, C = y.shape
    tm = _pick(T, 512, SUBLANES)

    def body(do_ref, y_ref, z_ref, w_ref, b_ref, dy_ref, dz_ref, ds_ref, db_ref):
        dov = do_ref[...]
        a, da_dy = _gelu_and_grad(y_ref[...])
        sg = _sigmoid(_dot(a.astype(BF16), w_ref[...], NN) + b_ref[...])
        sz, dsz = _silu_and_grad(z_ref[...])
        yg = a * sg
        dz_ref[...] = (dov * yg * dsz).astype(BF16)
        dyg = dov * sz
        ds = dyg * a * sg * (1.0 - sg)
        dsb = ds.astype(BF16)
        ds_ref[...] = dsb
        da = dyg * sg + _dot(dsb, w_ref[...], NT)
        dy_ref[...] = da * da_dy

        @pl.when(pl.program_id(0) == 0)
        def _():
            db_ref[...] = jnp.zeros_like(db_ref)

        db_ref[...] += jnp.sum(ds, axis=0, keepdims=True)

    row = pl.BlockSpec((tm, C), lambda i: (i, 0))
    vec = pl.BlockSpec((1, C), lambda i: (0, 0))
    sd = jax.ShapeDtypeStruct
    return pl.pallas_call(
        body, name=name, grid=(T // tm,),
        in_specs=[row, row, pl.BlockSpec((tm, C), lambda i: (i, 1)), pl.BlockSpec((C, C), lambda i: (0, 0)), vec],
        out_specs=[row, row, row, vec],
        out_shape=[sd((T, C), F32), sd((T, C), BF16), sd((T, C), BF16), sd((1, C), F32)],
        compiler_params=_params("arbitrary"),
    )(do, y, uz, w_glu, b_glu.reshape(1, C))


def _rel_bucket(dist):
    n = jnp.maximum(dist, 0)
    max_exact = NUM_BUCKETS // 2
    n_f = jnp.maximum(n, 1).astype(F32)
    large = max_exact + (jnp.log(n_f / max_exact) / math.log(REL_MAX_DISTANCE / max_exact)
                         * (NUM_BUCKETS - max_exact)).astype(jnp.int32)
    large = jnp.minimum(large, NUM_BUCKETS - 1)
    return jnp.where(n < max_exact, n, large)


def _band_tables():
    qi = jnp.arange(ATTN_BLOCK)[:, None]
    kj = jnp.arange(2 * ATTN_BLOCK)[None, :]
    delta = ATTN_BLOCK + qi - kj
    buckets, bands = [], []
    for window, dilation in ATTN_CONFIGS:
        bands.append((delta >= 0) & (delta <= window // dilation))
        buckets.append(_rel_bucket(jnp.maximum(delta, 0) * dilation))
    return jnp.stack(buckets), jnp.stack(bands)


def _attn_blocks_per_residue(T):
    return [T // (ATTN_BLOCK * d) for _, d in ATTN_CONFIGS]


def _attn_fwd(q, k, v, biasm, name):
    ng, nh, T, hd = q.shape
    nblk = T // ATTN_BLOCK
    nbs = _attn_blocks_per_residue(T)
    scale = hd ** -0.5
    B = ATTN_BLOCK

    def body(q_ref, kc_ref, kp_ref, vc_ref, vp_ref, bias_ref, o_ref, lse_ref):
        g = pl.program_id(0)
        b = pl.program_id(1)
        nb = jnp.where(g == 0, nbs[0], jnp.where(g == 1, nbs[1], nbs[2]))
        has_prev = (b % nb) != 0
        for h in range(nh):
            qh = q_ref[0, h]
            sp = _dot(qh, kp_ref[0, h], NT) * scale + bias_ref[0, h, :, :B]
            sp = jnp.where(has_prev, sp, NEG_INF)
            sc = _dot(qh, kc_ref[0, h], NT) * scale + bias_ref[0, h, :, B:]
            m = jnp.maximum(jnp.max(sp, axis=-1, keepdims=True), jnp.max(sc, axis=-1, keepdims=True))
            pp = jnp.exp(sp - m)
            pc = jnp.exp(sc - m)
            l = jnp.sum(pp, axis=-1, keepdims=True) + jnp.sum(pc, axis=-1, keepdims=True)
            o_ref[0, h] = (_dot((pp / l).astype(BF16), vp_ref[0, h], NN)
                           + _dot((pc / l).astype(BF16), vc_ref[0, h], NN))
            lse_ref[0, h] = jnp.broadcast_to(m + jnp.log(l), (B, hd))

    cur = pl.BlockSpec((1, nh, B, hd), lambda g, b: (g, 0, b, 0))
    prev = pl.BlockSpec((1, nh, B, hd), lambda g, b: (g, 0, jnp.maximum(b - 1, 0), 0))
    return pl.pallas_call(
        body, name=name, grid=(ng, nblk),
        in_specs=[cur, cur, prev, cur, prev, pl.BlockSpec((1, nh, B, 2 * B), lambda g, b: (g, 0, 0, 0))],
        out_specs=[cur, cur], out_shape=[jax.ShapeDtypeStruct(q.shape, F32)] * 2,
        compiler_params=_params("parallel", "parallel"),
    )(q, k, k, v, v, biasm)


def _attn_bwd(q, k, v, do, dvec, lse, biasm, name):
    ng, nh, T, hd = q.shape
    nblk = T // ATTN_BLOCK
    nbs = _attn_blocks_per_residue(T)
    scale = hd ** -0.5
    B = ATTN_BLOCK

    def body(q_ref, kc_ref, kp_ref, vc_ref, vp_ref, do_ref, dv_ref, lse_ref, bias_ref,
             dq_ref, dk_ref, dvo_ref, dbias_ref, ck_ref, cv_ref):
        g = pl.program_id(0)
        b = pl.program_id(1)
        nb = jnp.where(g == 0, nbs[0], jnp.where(g == 1, nbs[1], nbs[2]))
        has_prev = (b % nb) != 0

        @pl.when(b == 0)
        def _():
            dbias_ref[...] = jnp.zeros_like(dbias_ref)
            ck_ref[...] = jnp.zeros_like(ck_ref)
            cv_ref[...] = jnp.zeros_like(cv_ref)

        @pl.when(b < nblk)
        def _():
            for h in range(nh):
                qh = q_ref[0, h]
                doh = do_ref[0, h]
                lse_col = lse_ref[0, h, :, :1]
                d_col = dv_ref[0, h, :, :1]
                sp = _dot(qh, kp_ref[0, h], NT) * scale + bias_ref[0, h, :, :B]
                sp = jnp.where(has_prev, sp, NEG_INF)
                sc = _dot(qh, kc_ref[0, h], NT) * scale + bias_ref[0, h, :, B:]
                pp = jnp.exp(sp - lse_col)
                pc = jnp.exp(sc - lse_col)
                dsp = pp * (_dot(doh, vp_ref[0, h], NT) + d_col)
                dsc = pc * (_dot(doh, vc_ref[0, h], NT) + d_col)
                dbias_ref[0, h, :, :B] += dsp
                dbias_ref[0, h, :, B:] += dsc
                dspb = dsp.astype(BF16)
                dscb = dsc.astype(BF16)
                dq_ref[0, h] = ((_dot(dspb, kp_ref[0, h], NN) + _dot(dscb, kc_ref[0, h], NN)) * scale).astype(BF16)
                dk_ref[0, h] = (ck_ref[h] + _dot(dspb, qh, TN) * scale).astype(BF16)
                dvo_ref[0, h] = (cv_ref[h] + _dot(pp.astype(BF16), doh, TN)).astype(BF16)
                ck_ref[h] = _dot(dscb, qh, TN) * scale
                cv_ref[h] = _dot(pc.astype(BF16), doh, TN)

        @pl.when(b == nblk)
        def _():
            dk_ref[0] = ck_ref[...].astype(BF16)
            dvo_ref[0] = cv_ref[...].astype(BF16)

    last = nblk - 1
    cur = pl.BlockSpec((1, nh, B, hd), lambda g, b: (g, 0, jnp.minimum(b, last), 0))
    prev = pl.BlockSpec((1, nh, B, hd), lambda g, b: (g, 0, jnp.clip(b - 1, 0, last), 0))
    tab = pl.BlockSpec((1, nh, B, 2 * B), lambda g, b: (g, 0, 0, 0))
    sd = jax.ShapeDtypeStruct
    return pl.pallas_call(
        body, name=name, grid=(ng, nblk + 1),
        in_specs=[cur, cur, prev, cur, prev, cur, cur, cur, tab],
        out_specs=[cur, prev, prev, tab],
        out_shape=[sd(q.shape, BF16), sd(q.shape, BF16), sd(q.shape, BF16), sd(biasm.shape, F32)],
        scratch_shapes=[pltpu.VMEM((nh, B, hd), F32), pltpu.VMEM((nh, B, hd), F32)],
        compiler_params=_params("parallel", "arbitrary"),
    )(q, k, k, v, v, do, dvec, lse, biasm)


def _attn_mix(o, lse, z, name):
    T, C = o.shape
    gw = C // len(ATTN_CONFIGS)
    tm = _pick(T, 512, SUBLANES)

    def body(o_ref, lse_ref, z_ref, out_ref):
        ls = [lse_ref[:, i * gw:(i + 1) * gw] for i in range(3)]
        mx = jnp.maximum(jnp.maximum(ls[0], ls[1]), ls[2])
        es = [jnp.exp(l - mx) for l in ls]
        den = es[0] + es[1] + es[2]
        for i in range(3):
            sz, _ = _silu_and_grad(z_ref[:, i * gw:(i + 1) * gw])
            out_ref[:, i * gw:(i + 1) * gw] = (o_ref[:, i * gw:(i + 1) * gw] * (es[i] / den) * sz).astype(BF16)

    row = pl.BlockSpec((tm, C), lambda i: (i, 0))
    return pl.pallas_call(
        body, name=name, grid=(T // tm,), in_specs=[row, row, row], out_specs=row,
        out_shape=jax.ShapeDtypeStruct((T, C), BF16), compiler_params=_params("parallel"),
    )(o, lse, z)


def _attn_mix_bwd(dout, o, lse, z, name):
    T, C = o.shape
    gw = C // len(ATTN_CONFIGS)
    tm = _pick(T, 512, SUBLANES)
    head_of = np.arange(gw) // ATTN_HEAD_DIM
    ones = jnp.asarray(head_of[:, None] == head_of[None, :], BF16)

    def body(dout_ref, o_ref, lse_ref, z_ref, ones_ref, dz_ref, do_ref, dv_ref):
        ls = [lse_ref[:, i * gw:(i + 1) * gw] for i in range(3)]
        mx = jnp.maximum(jnp.maximum(ls[0], ls[1]), ls[2])
        es = [jnp.exp(l - mx) for l in ls]
        den = es[0] + es[1] + es[2]
        alphas, ebar = [], 0.0
        for i in range(3):
            sl = slice(i * gw, (i + 1) * gw)
            alpha = es[i] / den
            ov = o_ref[:, sl]
            dv = dout_ref[:, sl]
            sz, dsz = _silu_and_grad(z_ref[:, sl])
            dz_ref[:, sl] = (dv * ov * alpha * dsz).astype(BF16)
            da = dv * sz
            do_ref[:, sl] = (da * alpha).astype(BF16)
            t = da * ov
            t1 = t.astype(BF16)
            r1 = t - t1.astype(F32)
            t2 = r1.astype(BF16)
            t3 = (r1 - t2.astype(F32)).astype(BF16)
            e = _dot(t1, ones_ref[...], NN) + _dot(t2, ones_ref[...], NN) + _dot(t3, ones_ref[...], NN)
            ebar = ebar + alpha * e
            alphas.append(alpha)
        for i in range(3):
            dv_ref[:, i * gw:(i + 1) * gw] = -alphas[i] * ebar

    row = pl.BlockSpec((tm, C), lambda i: (i, 0))
    sd = jax.ShapeDtypeStruct
    return pl.pallas_call(
        body, name=name, grid=(T // tm,),
        in_specs=[row, row, row, row, pl.BlockSpec((gw, gw), lambda i: (0, 0))], out_specs=[row, row, row],
        out_shape=[sd((T, C), BF16), sd((T, C), BF16), sd((T, C), F32)], compiler_params=_params("parallel"),
    )(dout, o, lse, z, ones)


def _mem_attn(qz, kv, name):
    T = qz.shape[0]
    dm = qz.shape[1] // 2
    M = kv.shape[0]
    hd = dm // MEM_HEADS
    scale = hd ** -0.5
    tm = _pick(T, 512, SUBLANES)

    def body(q_ref, z_ref, k_ref, v_ref, o_ref):
        for h in range(MEM_HEADS):
            sl = slice(h * hd, (h + 1) * hd)
            s = _dot(q_ref[:, sl].astype(BF16), k_ref[:, sl], NT) * scale
            p = jnp.exp(s - jnp.max(s, axis=-1, keepdims=True))
            pn = p / jnp.sum(p, axis=-1, keepdims=True)
            sz, _ = _silu_and_grad(z_ref[:, sl])
            o_ref[:, sl] = (_dot(pn.astype(BF16), v_ref[:, sl], NN) * sz).astype(BF16)

    return pl.pallas_call(
        body, name=name, grid=(T // tm,),
        in_specs=[pl.BlockSpec((tm, dm), lambda i: (i, 0)), pl.BlockSpec((tm, dm), lambda i: (i, 1)),
                  pl.BlockSpec((M, dm), lambda i: (0, 0)), pl.BlockSpec((M, dm), lambda i: (0, 1))],
        out_specs=pl.BlockSpec((tm, dm), lambda i: (i, 0)),
        out_shape=jax.ShapeDtypeStruct((T, dm), BF16), compiler_params=_params("parallel"),
    )(qz, qz, kv, kv)


def _mem_attn_bwd(do, qz, kv, name):
    T = qz.shape[0]
    dm = qz.shape[1] // 2
    M = kv.shape[0]
    hd = dm // MEM_HEADS
    scale = hd ** -0.5
    tm = _pick(T, 512, SUBLANES)

    def body(do_ref, q_ref, z_ref, k_ref, v_ref, dq_ref, dz_ref, dk_ref, dv_ref):
        @pl.when(pl.program_id(0) == 0)
        def _():
            dk_ref[...] = jnp.zeros_like(dk_ref)
            dv_ref[...] = jnp.zeros_like(dv_ref)

        for h in range(MEM_HEADS):
            sl = slice(h * hd, (h + 1) * hd)
            qb = q_ref[:, sl].astype(BF16)
            s = _dot(qb, k_ref[:, sl], NT) * scale
            p = jnp.exp(s - jnp.max(s, axis=-1, keepdims=True))
            pn = p / jnp.sum(p, axis=-1, keepdims=True)
            pnb = pn.astype(BF16)
            o = _dot(pnb, v_ref[:, sl], NN)
            sz, dsz = _silu_and_grad(z_ref[:, sl])
            dov = do_ref[:, sl]
            dz_ref[:, sl] = (dov * o * dsz).astype(BF16)
            dob = (dov * sz).astype(BF16)
            dp = _dot(dob, v_ref[:, sl], NT)
            ds = pn * (dp - jnp.sum(dp * pn, axis=-1, keepdims=True))
            dsb = ds.astype(BF16)
            dq_ref[:, sl] = (_dot(dsb, k_ref[:, sl], NN) * scale).astype(BF16)
            dk_ref[:, sl] += _dot(dsb, qb, TN) * scale
            dv_ref[:, sl] += _dot(pnb, dob, TN)

    rowq = pl.BlockSpec((tm, dm), lambda i: (i, 0))
    rowz = pl.BlockSpec((tm, dm), lambda i: (i, 1))
    kb = pl.BlockSpec((M, dm), lambda i: (0, 0))
    vb = pl.BlockSpec((M, dm), lambda i: (0, 1))
    sd = jax.ShapeDtypeStruct
    dq, dz, dk, dv = pl.pallas_call(
        body, name=name, grid=(T // tm,), in_specs=[rowq, rowq, rowz, kb, vb],
        out_specs=[rowq, rowq, kb, kb],
        out_shape=[sd((T, dm), BF16), sd((T, dm), BF16), sd((M, dm), F32), sd((M, dm), F32)],
        compiler_params=_params("arbitrary"),
    )(do, qz, qz, kv, kv)
    return dq, dz, dk, dv


def _merge(bps, logits, b_gate, name):
    T, D = bps[0].shape
    tm = _pick(T, 512, SUBLANES)

    def body(p0_ref, p1_ref, p2_ref, l_ref, b_ref, o_ref):
        acc = 0.0
        for i, p_ref in enumerate((p0_ref, p1_ref, p2_ref)):
            sl = slice(i * D, (i + 1) * D)
            acc = acc + _sigmoid(l_ref[:, sl] + b_ref[:, sl]) * p_ref[...]
        o_ref[...] = acc.astype(BF16)

    row = pl.BlockSpec((tm, D), lambda i: (i, 0))
    return pl.pallas_call(
        body, name=name, grid=(T // tm,),
        in_specs=[row, row, row, pl.BlockSpec((tm, 3 * D), lambda i: (i, 0)), pl.BlockSpec((1, 3 * D), lambda i: (0, 0))],
        out_specs=row, out_shape=jax.ShapeDtypeStruct((T, D), BF16), compiler_params=_params("parallel"),
    )(*bps, logits, b_gate.reshape(1, 3 * D))


def _merge_bwd(dmerged, bps, logits, b_gate, name):
    T, D = bps[0].shape
    tm = _pick(T, 512, SUBLANES)

    def body(dm_ref, p0_ref, p1_ref, p2_ref, l_ref, b_ref, d0_ref, d1_ref, d2_ref, dl_ref, db_ref):
        @pl.when(pl.program_id(0) == 0)
        def _():
            db_ref[...] = jnp.zeros_like(db_ref)

        dmv = dm_ref[...]
        for i, (p_ref, d_ref) in enumerate(((p0_ref, d0_ref), (p1_ref, d1_ref), (p2_ref, d2_ref))):
            sl = slice(i * D, (i + 1) * D)
            gt = _sigmoid(l_ref[:, sl] + b_ref[:, sl])
            d_ref[...] = (dmv * gt).astype(BF16)
            dl = dmv * p_ref[...] * gt * (1.0 - gt)
            dl_ref[:, sl] = dl.astype(BF16)
            db_ref[:, sl] += jnp.sum(dl, axis=0, keepdims=True)

    row = pl.BlockSpec((tm, D), lambda i: (i, 0))
    wide = pl.BlockSpec((tm, 3 * D), lambda i: (i, 0))
    vec = pl.BlockSpec((1, 3 * D), lambda i: (0, 0))
    sd = jax.ShapeDtypeStruct
    return pl.pallas_call(
        body, name=name, grid=(T // tm,), in_specs=[row, row, row, row, wide, vec],
        out_specs=[row, row, row, wide, vec],
        out_shape=[sd((T, D), BF16)] * 3 + [sd((T, 3 * D), BF16), sd((1, 3 * D), F32)],
        compiler_params=_params("arbitrary"),
    )(dmerged, *bps, logits, b_gate.reshape(1, 3 * D))


def _to_segments(a):
    T, C = a.shape
    return a.reshape(SSM_SEGMENTS, T // SSM_SEGMENTS, C).transpose(1, 0, 2).reshape(T, C)


def _from_segments(a):
    T, C = a.shape
    return a.reshape(T // SSM_SEGMENTS, SSM_SEGMENTS, C).transpose(1, 0, 2).reshape(T, C)


def _to_residues(a):
    T = a.shape[0]
    out = []
    for g, (_, r) in enumerate(ATTN_CONFIGS):
        ag = a[:, g * 256:(g + 1) * 256].reshape(T // r, r, HEADS_PER_GROUP, ATTN_HEAD_DIM)
        out.append(ag.transpose(2, 1, 0, 3).reshape(HEADS_PER_GROUP, T, ATTN_HEAD_DIM))
    return jnp.stack(out)


def _from_residues(a):
    T = a.shape[2]
    out = []
    for g, (_, r) in enumerate(ATTN_CONFIGS):
        ag = a[g].reshape(HEADS_PER_GROUP, r, T // r, ATTN_HEAD_DIM).transpose(2, 1, 0, 3)
        out.append(ag.reshape(T, HEADS_PER_GROUP * ATTN_HEAD_DIM))
    return jnp.concatenate(out, axis=1)


def _block_diag(w):
    nblk, ng, a, b = w.shape
    eye = jnp.eye(ng, dtype=w.dtype)
    return (w[:, :, :, None, :] * eye[None, :, None, :, None]).reshape(nblk, ng * a, ng * b)


def _block_diag_part(m, a, b):
    nblk = m.shape[0]
    ng = m.shape[1] // a
    m5 = m.reshape(nblk, ng, a, ng, b)
    eye = jnp.eye(ng, dtype=m.dtype)
    return jnp.sum(m5 * eye[None, :, None, :, None], axis=3)


def _ssm_matrices(p, L, tag):
    G, P = p["ssm_lambda_re"].shape[1:]
    Hg = SSM_GROUP
    gpb = SSM_BLOCK_CH // Hg
    nblk = G // gpb
    br = p["ssm_b_re"][L].transpose(2, 0, 1)
    bi = p["ssm_b_im"][L].transpose(2, 0, 1)
    disc_in = (p["ssm_lambda_re"][L], p["ssm_lambda_im"][L], p["ssm_log_dt"][L].reshape(G, 1), br, bi)
    ar, ai, bbr, bbi = _ssm_disc(*disc_in, name=f"ssm_disc_{tag}")
    amat = jnp.concatenate([ar.reshape(nblk, gpb * P), ai.reshape(nblk, gpb * P)], axis=1)
    amat = jnp.broadcast_to(amat[:, None, :], (nblk, SUBLANES, 2 * gpb * P))
    bbr_g = bbr.transpose(1, 0, 2).reshape(nblk, gpb, Hg, P)
    bbi_g = bbi.transpose(1, 0, 2).reshape(nblk, gpb, Hg, P)
    bmat = jnp.concatenate([_block_diag(bbr_g), _block_diag(bbi_g)], axis=2).astype(BF16)
    cre = p["ssm_c_re"][L].reshape(nblk, gpb, Hg, P).transpose(0, 1, 3, 2)
    cim = p["ssm_c_im"][L].reshape(nblk, gpb, Hg, P).transpose(0, 1, 3, 2)
    cmat = jnp.concatenate([_block_diag(cre), -_block_diag(cim)], axis=1).astype(BF16)
    return disc_in, amat, bmat, cmat


def _layer_fwd(x, mem, p, wb, L, biasm):
    T, D = x.shape
    C = p["ssm_d"].shape[1]
    dm = wb["w_br_mem"].shape[1]
    tag = f"l{L}"
    s = {"x": x}
    h = _rmsnorm(x, p["norm_g"][L], f"norm_{tag}")
    w_in = wb["w_in"][L]
    offs = np.cumsum([0, 2 * C, 3 * 768, 768, 2 * dm, 3 * D])
    names = ("uz", "qkv", "z_attn", "qz_mem", "logits")
    dts = (F32, BF16, F32, F32, F32)
    for i, (nm, dt) in enumerate(zip(names, dts)):
        s[nm] = _matmul(h, w_in[:, offs[i]:offs[i + 1]], mode="nn", name=f"in_{nm}_{tag}", out_dtype=dt)
    s["h"] = h

    disc_in, amat, bmat, cmat = _ssm_matrices(p, L, tag)
    u_seg = _to_segments(s["uz"][:, :C])
    dvec = p["ssm_d"][L].reshape(1, C)
    car = _ssm_carries(u_seg, None, bmat, cmat, amat, reverse=False, name=f"ssm_carry_{tag}")
    y_seg, xs = _ssm_scan(u_seg, bmat, cmat, amat, car, dvec, f"ssm_scan_{tag}")
    y = _from_segments(y_seg)
    o_ssm, a_glu = _ssm_post(y, s["uz"], wb["w_glu"][L], p["b_glu"][L], f"ssm_post_{tag}")
    s.update(disc_in=disc_in, amat=amat, bmat=bmat, cmat=cmat, u_seg=u_seg, xs=xs, y=y, a_glu=a_glu, o_ssm=o_ssm)

    q, k, v = (_to_residues(s["qkv"][:, i * 768:(i + 1) * 768]) for i in range(3))
    o_r, lse_r = _attn_fwd(q, k, v, biasm, f"attn_{tag}")
    o_nat, lse_nat = _from_residues(o_r), _from_residues(lse_r)
    o_attn = _attn_mix(o_nat, lse_nat, s["z_attn"], f"attn_mix_{tag}")
    s.update(q=q, k=k, v=v, lse_r=lse_r, o_nat=o_nat, lse_nat=lse_nat, o_attn=o_attn)

    mn = _rmsnorm(mem, p["mem_norm_g"][L], f"mem_norm_{tag}")
    kv = _matmul(mn, wb["w_mem_kv"][L], mode="nn", name=f"mem_kv_{tag}", out_dtype=BF16)
    o_mem = _mem_attn(s["qz_mem"], kv, f"mem_attn_{tag}")
    s.update(mn=mn, kv=kv, o_mem=o_mem)

    bps = [_matmul(o, wb[n][L], mode="nn", name=f"br_{n}_{tag}")
           for o, n in ((o_ssm, "w_br_ssm"), (o_attn, "w_br_attn"), (o_mem, "w_br_mem"))]
    merged = _merge(bps, s["logits"], p["b_gate"][L], f"merge_{tag}")
    s.update(bps=bps, merged=merged)
    x_new = _matmul(merged, wb["w_out"][L], mode="nn", name=f"out_{tag}", add=x)
    return x_new, s


def _layer_bwd(dx, mem, p, wb, L, s, biasm):
    T, D = dx.shape
    C = p["ssm_d"].shape[1]
    tag = f"l{L}"
    g = {}
    dmerged = _matmul(dx, wb["w_out"][L], mode="nt", name=f"d_merged_{tag}")
    g["w_out"] = _matmul(s["merged"], dx, mode="tn", name=f"dw_out_{tag}", out_dtype=BF16)
    dbp0, dbp1, dbp2, dlogits, g["b_gate"] = _merge_bwd(dmerged, s["bps"], s["logits"], p["b_gate"][L], f"merge_bwd_{tag}")
    dos = []
    for dbp, o, n in ((dbp0, s["o_ssm"], "w_br_ssm"), (dbp1, s["o_attn"], "w_br_attn"), (dbp2, s["o_mem"], "w_br_mem")):
        dos.append(_matmul(dbp, wb[n][L], mode="nt", name=f"d_o_{n}_{tag}"))
        g[n] = _matmul(o, dbp, mode="tn", name=f"d{n}_{tag}", out_dtype=BF16)

    dy, dz_ssm, ds_glu, g["b_glu"] = _ssm_post_bwd(dos[0], s["y"], s["uz"], wb["w_glu"][L], p["b_glu"][L], f"ssm_post_bwd_{tag}")
    g["w_glu"] = _matmul(s["a_glu"], ds_glu, mode="tn", name=f"dw_glu_{tag}", out_dtype=BF16)
    dy_seg = _to_segments(dy)
    dvec = p["ssm_d"][L].reshape(1, C)
    rcar = _ssm_carries(None, dy_seg, s["bmat"], s["cmat"], s["amat"], reverse=True, name=f"ssm_rcarry_{tag}")
    du_seg, dbm, dct, dam, g["ssm_d"] = _ssm_scan_bwd(dy_seg, s["u_seg"], s["xs"], s["bmat"], s["cmat"], s["amat"], rcar, dvec,
                                                      f"ssm_scan_bwd_{tag}")
    du = _from_segments(du_seg)
    G, P = p["ssm_lambda_re"].shape[1:]
    Hg = SSM_GROUP
    half = dbm.shape[2] // 2
    dbbr = _block_diag_part(dbm[:, :, :half], Hg, P).reshape(G, Hg, P).transpose(1, 0, 2)
    dbbi = _block_diag_part(dbm[:, :, half:], Hg, P).reshape(G, Hg, P).transpose(1, 0, 2)
    g["ssm_c_re"] = _block_diag_part(dct[:, :, :half], Hg, P).reshape(G, Hg, P)
    g["ssm_c_im"] = -_block_diag_part(dct[:, :, half:], Hg, P).reshape(G, Hg, P)
    dar = dam[:, 0, :half].reshape(G, P)
    dai = dam[:, 0, half:].reshape(G, P)
    glre, glim, gdt, gbr, gbi = _ssm_disc_bwd(*s["disc_in"], dar, dai, dbbr, dbbi, name=f"ssm_disc_bwd_{tag}")
    g["ssm_lambda_re"], g["ssm_lambda_im"], g["ssm_log_dt"] = glre, glim, gdt.reshape(G)
    g["ssm_b_re"] = gbr.transpose(1, 2, 0)
    g["ssm_b_im"] = gbi.transpose(1, 2, 0)

    dz_attn, do_nat, dvec_nat = _attn_mix_bwd(dos[1], s["o_nat"], s["lse_nat"], s["z_attn"], f"attn_mix_bwd_{tag}")
    dq_r, dk_r, dv_r, dbias = _attn_bwd(s["q"], s["k"], s["v"], _to_residues(do_nat), _to_residues(dvec_nat), s["lse_r"], biasm,
                                        f"attn_bwd_{tag}")
    dqkv = [_from_residues(a) for a in (dq_r, dk_r, dv_r)]

    dq_mem, dz_mem, dk_mem, dv_mem = _mem_attn_bwd(dos[2], s["qz_mem"], s["kv"], f"mem_attn_bwd_{tag}")
    dkv = jnp.concatenate([dk_mem, dv_mem], axis=1)
    g["w_mem_kv"] = _matmul(s["mn"], dkv, mode="tn", name=f"dw_mem_kv_{tag}", out_dtype=BF16)
    dmn = _matmul(dkv, wb["w_mem_kv"][L], mode="nt", name=f"d_mn_{tag}")
    _, g["mem_norm_g"] = _rmsnorm_bwd(mem, p["mem_norm_g"][L], dmn, None, f"mem_norm_bwd_{tag}")

    dproj = jnp.concatenate([du.astype(BF16), dz_ssm] + dqkv + [dz_attn, dq_mem, dz_mem, dlogits], axis=1)
    dh = _matmul(dproj, wb["w_in"][L], mode="nt", name=f"d_h_{tag}")
    g["w_in"] = _matmul(s["h"], dproj, mode="tn", name=f"dw_in_{tag}", out_dtype=BF16)
    dx_in, g["norm_g"] = _rmsnorm_bwd(s["x"], p["norm_g"][L], dh, dx, f"norm_bwd_{tag}")
    return dx_in, g, dbias


def _rel_bias_grad(dbias_sum, name):
    buckets, bands = _band_tables()
    cols = []
    for gi in range(len(ATTN_CONFIGS)):
        onehot = ((buckets[gi].reshape(-1, 1) == jnp.arange(NUM_BUCKETS)[None, :]) & bands[gi].reshape(-1, 1)).astype(BF16)
        flat = dbias_sum[gi].reshape(HEADS_PER_GROUP, -1)
        cols.append(_matmul(flat, onehot, mode="nn", name=f"{name}_{gi}", split_a=True, tk=4096).T)
    return jnp.concatenate(cols, axis=1)


def _local_step(x, mem, target, p, wb):
    depth = p["norm_g"].shape[0]
    buckets, bands = _band_tables()
    biasm = []
    for gi in range(len(ATTN_CONFIGS)):
        tab = p["rel_bias"][:, gi * HEADS_PER_GROUP:(gi + 1) * HEADS_PER_GROUP][buckets[gi]]
        biasm.append(jnp.where(bands[gi][None], tab.transpose(2, 0, 1), NEG_INF))
    biasm = jnp.stack(biasm).astype(F32)
    saved = []
    for L in range(depth):
        x, s = _layer_fwd(x, mem, p, wb, L, biasm)
        saved.append(s)
    loss_vec, dx, dgf = _loss_head(x, p["final_norm_g"], target, "loss_head")
    grads = {"final_norm_g": dgf.reshape(-1)}
    per_layer = [None] * depth
    dbias_sum = 0.0
    for L in reversed(range(depth)):
        dx, per_layer[L], dbias = _layer_bwd(dx, mem, p, wb, L, saved[L], biasm)
        dbias_sum = dbias_sum + dbias
    for n in per_layer[0]:
        grads[n] = jnp.stack([per_layer[L][n].reshape(p[n].shape[1:]) if n not in dict(BIG) else per_layer[L][n]
                              for L in range(depth)])
    grads["rel_bias"] = _rel_bias_grad(dbias_sum, "d_rel_bias")
    return jnp.sum(loss_vec), dx, grads


def _chip_coords(j):
    return j // 2, j % 2


def _gather_shards(shards, name):
    n = len(shards)

    def body(*refs):
        ins, outs = refs[:n], refs[n:2 * n]
        send_sems, recv_sems, loc_sems = refs[2 * n:]
        x, y, c = lax.axis_index("x"), lax.axis_index("y"), lax.axis_index("c")
        mine = 2 * x + y
        for t in range(n):
            pltpu.make_async_copy(ins[t], outs[t].at[mine], loc_sems.at[t]).start()
            for j in range(N_CHIPS):
                @pl.when(j != mine)
                def _():
                    pltpu.make_async_remote_copy(
                        src_ref=ins[t], dst_ref=outs[t].at[mine], send_sem=send_sems.at[t, j], recv_sem=recv_sems.at[t, mine],
                        device_id=(*_chip_coords(j), c), device_id_type=MESH).start()
        for t in range(n):
            for j in range(N_CHIPS):
                @pl.when(j != mine)
                def _():
                    cp = pltpu.make_async_remote_copy(
                        src_ref=ins[t], dst_ref=outs[t].at[j], send_sem=send_sems.at[t, j], recv_sem=recv_sems.at[t, j],
                        device_id=(*_chip_coords(j), c), device_id_type=MESH)
                    cp.wait_send()
                    cp.wait_recv()
            pltpu.make_async_copy(ins[t], outs[t].at[mine], loc_sems.at[t]).wait()

    return pl.pallas_call(
        body, name=name, in_specs=[HBM] * n, out_specs=[HBM] * n,
        out_shape=[jax.ShapeDtypeStruct((N_CHIPS,) + a.shape, a.dtype) for a in shards],
        scratch_shapes=[pltpu.SemaphoreType.DMA((n, N_CHIPS)), pltpu.SemaphoreType.DMA((n, N_CHIPS)), pltpu.SemaphoreType.DMA((n,))],
    )(*shards)


def _scatter_slices(arrays, axes, name):
    n = len(arrays)

    def piece(a, ax):
        if ax is None:
            return a.shape, None
        w = a.shape[ax] // N_CHIPS
        return a.shape[:ax] + (w,) + a.shape[ax + 1:], w

    shapes = [piece(a, ax) for a, ax in zip(arrays, axes)]

    def body(*refs):
        ins, outs = refs[:n], refs[n:2 * n]
        send_sems, recv_sems, loc_sems = refs[2 * n:]
        x, y, c = lax.axis_index("x"), lax.axis_index("y"), lax.axis_index("c")
        mine = 2 * x + y

        def src(t, j):
            ax, w = axes[t], shapes[t][1]
            if ax is None:
                return ins[t]
            idx = tuple(pl.ds(j * w, w) if d == ax else slice(None) for d in range(len(arrays[t].shape)))
            return ins[t].at[idx]

        for t in range(n):
            for j in range(N_CHIPS):
                @pl.when(j == mine)
                def _():
                    pltpu.make_async_copy(src(t, j), outs[t].at[j], loc_sems.at[t]).start()

                @pl.when(j != mine)
                def _():
                    pltpu.make_async_remote_copy(
                        src_ref=src(t, j), dst_ref=outs[t].at[mine], send_sem=send_sems.at[t, j], recv_sem=recv_sems.at[t, mine],
                        device_id=(*_chip_coords(j), c), device_id_type=MESH).start()
        for t in range(n):
            for j in range(N_CHIPS):
                @pl.when(j == mine)
                def _():
                    pltpu.make_async_copy(src(t, j), outs[t].at[j], loc_sems.at[t]).wait()

                @pl.when(j != mine)
                def _():
                    cp = pltpu.make_async_remote_copy(
                        src_ref=src(t, j), dst_ref=outs[t].at[j], send_sem=send_sems.at[t, j], recv_sem=recv_sems.at[t, j],
                        device_id=(*_chip_coords(j), c), device_id_type=MESH)
                    cp.wait_send()
                    cp.wait_recv()

    return pl.pallas_call(
        body, name=name, in_specs=[HBM] * n, out_specs=[HBM] * n,
        out_shape=[jax.ShapeDtypeStruct((N_CHIPS,) + sh, a.dtype) for a, (sh, _) in zip(arrays, shapes)],
        scratch_shapes=[pltpu.SemaphoreType.DMA((n, N_CHIPS)), pltpu.SemaphoreType.DMA((n, N_CHIPS)), pltpu.SemaphoreType.DMA((n,))],
    )(*arrays)


def _swap_with_sibling(arrays, name):
    n = len(arrays)

    def body(*refs):
        ins, outs = refs[:n], refs[n:2 * n]
        send_sems, recv_sems = refs[2 * n:]
        peer = (lax.axis_index("x"), lax.axis_index("y"), 1 - lax.axis_index("c"))
        cps = [pltpu.make_async_remote_copy(src_ref=ins[t], dst_ref=outs[t], send_sem=send_sems.at[t], recv_sem=recv_sems.at[t],
                                            device_id=peer, device_id_type=MESH) for t in range(n)]
        for cp in cps:
            cp.start()
        for cp in cps:
            cp.wait_send()
            cp.wait_recv()

    return pl.pallas_call(
        body, name=name, in_specs=[HBM] * n, out_specs=[HBM] * n,
        out_shape=[jax.ShapeDtypeStruct(a.shape, a.dtype) for a in arrays],
        scratch_shapes=[pltpu.SemaphoreType.DMA((n,)), pltpu.SemaphoreType.DMA((n,))],
    )(*arrays)


def _sum_chips(landed, name):
    _, R, C = landed.shape
    tr = _pick(R, max(SUBLANES, (1 << 19) // C // 16 * 16), 16)

    def body(l_ref, o_ref):
        acc = l_ref[0].astype(F32) + l_ref[1].astype(F32)
        acc = acc + l_ref[2].astype(F32)
        o_ref[...] = acc + l_ref[3].astype(F32)

    return pl.pallas_call(
        body, name=name, grid=(R // tr,), in_specs=[pl.BlockSpec((N_CHIPS, tr, C), lambda i: (0, i, 0))],
        out_specs=pl.BlockSpec((tr, C), lambda i: (i, 0)), out_shape=jax.ShapeDtypeStruct((R, C), F32),
        compiler_params=_params("parallel"),
    )(landed)


def _adamw(w, pa, pb, m, v, name):
    R, C = w.shape
    tr = _pick(R, max(SUBLANES, (1 << 18) // C // 8 * 8), SUBLANES)
    c1 = 1.0 / (1.0 - ADAM_B1 ** ADAM_STEP)
    c2 = 1.0 / (1.0 - ADAM_B2 ** ADAM_STEP)

    def body(w_ref, pa_ref, pb_ref, m_ref, v_ref, g_ref, d_ref, nm_ref, nv_ref):
        g = pa_ref[...] + pb_ref[...]
        nm = ADAM_B1 * m_ref[...] + (1.0 - ADAM_B1) * g
        nv = ADAM_B2 * v_ref[...] + (1.0 - ADAM_B2) * (g * g)
        g_ref[...] = g
        nm_ref[...] = nm
        nv_ref[...] = nv
        d_ref[...] = -ADAM_LR * ((nm * c1) / (jnp.sqrt(nv * c2) + ADAM_EPS) + ADAM_WD * w_ref[...])

    blk = pl.BlockSpec((tr, C), lambda i: (i, 0))
    return pl.pallas_call(
        body, name=name, grid=(R // tr,), in_specs=[blk] * 5, out_specs=[blk] * 4,
        out_shape=[jax.ShapeDtypeStruct((R, C), F32)] * 4, compiler_params=_params("parallel"),
    )(w, pa, pb, m, v)


def _pack_small(d, prefix=""):
    flat = jnp.concatenate([d[prefix + n].astype(F32).reshape(-1) for n in SMALL])
    pad = (-flat.shape[0]) % (16 * LANES)
    return jnp.pad(flat, (0, pad)).reshape(-1, LANES)


def _unpack_small(packed, shapes):
    flat = packed.reshape(-1)
    out, off = {}, 0
    for n in SMALL:
        size = int(np.prod(shapes[n]))
        out[n] = flat[off:off + size].reshape(shapes[n])
        off += size
    return out


def kernel(*args):
    p = dict(zip(INPUTS, args))
    x, mem, target = p["x"][0], p["mem"][0], p["loss_target"][0]

    gathered = _gather_shards([p[n].astype(BF16) for n, _ in BIG], "gather_weights")
    wb = {n: jnp.concatenate([gw[j] for j in range(N_CHIPS)], axis=ax) for (n, ax), gw in zip(BIG, gathered)}

    loss_part, dx, grads = _local_step(x, mem, target, p, wb)
    loss = lax.psum(loss_part, ("x", "y", "c"))

    send = [grads[n] for n, _ in BIG] + [_pack_small(grads)]
    landed = _scatter_slices(send, [ax for _, ax in BIG] + [None], "scatter_grads")
    partial = []
    for (n, _), ld in list(zip(BIG, landed)) + [(("small", None), landed[-1])]:
        partial.append(_sum_chips(ld.reshape(N_CHIPS, -1, ld.shape[-1]), f"sum_chips_{n}"))
    other = _swap_with_sibling(partial, "swap_partials")

    out = {}
    for (n, _), pa, pb in zip(BIG, partial, other):
        sh = p[n].shape
        two_d = lambda a: a.reshape(-1, sh[-1])
        res = _adamw(two_d(p[n]), pa, pb, two_d(p["m_" + n]), two_d(p["v_" + n]), f"adamw_{n}")
        for key, r in zip(("grad_", "delta_", "new_m_", "new_v_"), res):
            out[key + n] = r.reshape(sh)
    res = _adamw(_pack_small(p), partial[-1], other[-1], _pack_small(p, "m_"), _pack_small(p, "v_"), "adamw_small")
    shapes = {n: p[n].shape for n in SMALL}
    for key, r in zip(("grad_", "delta_", "new_m_", "new_v_"), res):
        for n, a in _unpack_small(r, shapes).items():
            out[key + n] = a

    result = [loss, dx.reshape(p["x"].shape)]
    for key in ("grad_", "delta_", "new_m_", "new_v_"):
        result += [out[key + n] for n in WEIGHTS]
    return tuple(result)
```

```python
import math

import jax
import jax.numpy as jnp
import numpy as np
from jax import lax
from jax.experimental import pallas as pl
from jax.experimental.pallas import tpu as pltpu

F32 = jnp.float32
BF16 = jnp.bfloat16
MESH = pl.DeviceIdType.MESH
HBM = pl.BlockSpec(memory_space=pltpu.HBM)

EPS = 1e-6
SSM_GROUP = 16
SSM_STATE = 64
ATTN_HEAD_DIM = 64
HEADS_PER_GROUP = 4
ATTN_CONFIGS = ((128, 1), (512, 4), (2048, 16))
ATTN_BLOCK = 128
NUM_BUCKETS = 32
REL_MAX_DISTANCE = 2048
NEG_INF = -1e30
MEM_HEADS = 4
ADAM_LR = 0.001
ADAM_B1 = 0.9
ADAM_B2 = 0.999
ADAM_EPS = 1e-08
ADAM_WD = 0.01
ADAM_STEP = 10

LANES = 128
SUBLANES = 8
VMEM_LIMIT_BYTES = 48 * 1024 * 1024
SSM_BLOCK_CH = 128
SSM_SEGMENTS = SUBLANES
SSM_CHUNK_STEPS = 128

N_CHIPS = 4
BIG = (("w_in", 2), ("w_glu", 1), ("w_mem_kv", 1), ("w_br_ssm", 2), ("w_br_attn", 2), ("w_br_mem", 2), ("w_out", 1))
SMALL = ("norm_g", "mem_norm_g", "b_gate", "ssm_lambda_re", "ssm_lambda_im", "ssm_log_dt", "ssm_b_re", "ssm_b_im",
         "ssm_c_re", "ssm_c_im", "ssm_d", "b_glu", "rel_bias", "final_norm_g")
WEIGHTS = ("norm_g", "mem_norm_g", "w_in", "b_gate", "ssm_lambda_re", "ssm_lambda_im", "ssm_log_dt", "ssm_b_re",
           "ssm_b_im", "ssm_c_re", "ssm_c_im", "ssm_d", "w_glu", "b_glu", "w_mem_kv", "w_br_ssm", "w_br_attn",
           "w_br_mem", "w_out", "rel_bias", "final_norm_g")
INPUTS = ("x", "mem") + WEIGHTS + ("loss_target",) + tuple("m_" + n for n in WEIGHTS) + tuple("v_" + n for n in WEIGHTS)


def _params(*sem):
    return pltpu.CompilerParams(dimension_semantics=sem, vmem_limit_bytes=VMEM_LIMIT_BYTES)


def _pick(dim, pref, align):
    if dim <= pref:
        return dim
    t = pref - pref % align
    while t >= align:
        if dim % t == 0:
            return t
        t -= align
    return dim


def _sigmoid(v):
    return 1.0 / (1.0 + jnp.exp(-v))


def _silu_and_grad(z):
    s = _sigmoid(z)
    return z * s, s * (1.0 + z * (1.0 - s))


_GELU_C = math.sqrt(2.0 / math.pi)


def _gelu_and_grad(y):
    inner = _GELU_C * (y + 0.044715 * y * y * y)
    t = jnp.tanh(inner)
    g = 0.5 * y * (1.0 + t)
    dg = 0.5 * (1.0 + t) + 0.5 * y * (1.0 - t * t) * _GELU_C * (1.0 + 3.0 * 0.044715 * y * y)
    return g, dg


def _dot(a, b, dims):
    return lax.dot_general(a, b, (dims, ((), ())), preferred_element_type=F32)


NN = ((1,), (0,))
NT = ((1,), (1,))
TN = ((0,), (0,))


def _matmul(a, b, *, mode, name, out_dtype=F32, add=None, split_a=1, tm=1024, tn=768, tk=2304,
            b_lead=None, b_off=0, n_cols=None, stack=None):
    if mode == "tn":
        K, M = a.shape
    else:
        M, K = a.shape
    bshape = b.shape if b_lead is None else b.shape[1:]
    N = n_cols or (bshape[0] if mode == "nt" else bshape[1])
    tm = _pick(M, tm, LANES if mode == "tn" else SUBLANES)
    tn = _pick(math.gcd(N, b_off) if b_off else N, tn, LANES)
    tk = _pick(K, tk, LANES)
    nk = K // tk
    joff = b_off // tn
    dims = {"nn": NN, "nt": NT, "tn": TN}[mode]
    has_add = add is not None
    has_prev = stack is not None and stack[2] is not None

    def body(*refs):
        a_ref, b_ref = refs[:2]
        add_ref = refs[2] if has_add else None
        o_ref, acc_ref = refs[-2:]
        k = pl.program_id(2)
        bv = b_ref[...].astype(BF16)
        if split_a > 1:
            rest = a_ref[...].astype(F32)
            part = 0.0
            for _ in range(split_a):
                piece = rest.astype(BF16)
                part = part + _dot(piece, bv, dims)
                rest = rest - piece.astype(F32)
        else:
            part = _dot(a_ref[...].astype(BF16), bv, dims)

        @pl.when(k == 0)
        def _():
            acc_ref[...] = part

        @pl.when(k > 0)
        def _():
            acc_ref[...] += part

        @pl.when(k == nk - 1)
        def _():
            r = acc_ref[...]
            if has_add:
                r = r + add_ref[...]
            o_ref[...] = r.astype(out_dtype)

    a_spec = pl.BlockSpec((tk, tm), lambda i, j, k: (k, i)) if mode == "tn" else pl.BlockSpec((tm, tk), lambda i, j, k: (i, k))
    lead = () if b_lead is None else (b_lead,)
    lead_blk = () if b_lead is None else (None,)
    if mode == "nt":
        b_spec = pl.BlockSpec(lead_blk + (tn, tk), lambda i, j, k: lead + (j + joff, k))
    else:
        b_spec = pl.BlockSpec(lead_blk + (tk, tn), lambda i, j, k: lead + (k, j + joff))
    in_specs = [a_spec, b_spec]
    args = [a, b]
    if has_add:
        in_specs.append(pl.BlockSpec((tm, tn), lambda i, j, k: (i, j)))
        args.append(add)
    aliases = {}
    if stack is None:
        out_spec = pl.BlockSpec((tm, tn), lambda i, j, k: (i, j))
        out_shape = jax.ShapeDtypeStruct((M, N), out_dtype)
    else:
        layer, depth, prev = stack
        out_spec = pl.BlockSpec((None, tm, tn), lambda i, j, k: (layer, i, j))
        out_shape = jax.ShapeDtypeStruct((depth, M, N), out_dtype)
        if has_prev:
            in_specs.append(pl.BlockSpec(memory_space=pl.ANY))
            args.append(prev)
            aliases = {len(args) - 1: 0}
    return pl.pallas_call(
        body, name=name, grid=(M // tm, N // tn, nk), in_specs=in_specs, out_specs=out_spec, out_shape=out_shape,
        scratch_shapes=[pltpu.VMEM((tm, tn), F32)], input_output_aliases=aliases,
        compiler_params=_params("parallel", "parallel", "arbitrary"),
    )(*args)


def _rmsnorm(x, g, name):
    T, D = x.shape
    tm = _pick(T, 512, SUBLANES)

    def body(x_ref, g_ref, h_ref):
        xv = x_ref[...]
        r = lax.rsqrt(jnp.mean(xv * xv, axis=-1, keepdims=True) + EPS)
        h_ref[...] = (xv * r * g_ref[...]).astype(BF16)

    return pl.pallas_call(
        body, name=name, grid=(T // tm,),
        in_specs=[pl.BlockSpec((tm, D), lambda i: (i, 0)), pl.BlockSpec((1, D), lambda i: (0, 0))],
        out_specs=pl.BlockSpec((tm, D), lambda i: (i, 0)),
        out_shape=jax.ShapeDtypeStruct((T, D), BF16), compiler_params=_params("parallel"),
    )(x, g.reshape(1, D))


def _rmsnorm_bwd(x, g, dh, dres, name):
    T, D = x.shape
    tm = _pick(T, 512, SUBLANES)
    with_res = dres is not None

    def body(*refs):
        if with_res:
            x_ref, g_ref, dh_ref, dres_ref, dx_ref, dg_ref = refs
        else:
            x_ref, g_ref, dh_ref, dx_ref, dg_ref = refs
        xv = x_ref[...]
        dhv = dh_ref[...]
        r = lax.rsqrt(jnp.mean(xv * xv, axis=-1, keepdims=True) + EPS)
        dyg = dhv * g_ref[...]
        c = jnp.mean(dyg * xv, axis=-1, keepdims=True)
        dx = r * dyg - xv * (r * r * r) * c
        if with_res:
            dx = dx + dres_ref[...]
        dx_ref[...] = dx

        @pl.when(pl.program_id(0) == 0)
        def _():
            dg_ref[...] = jnp.zeros_like(dg_ref)

        dg_ref[...] += jnp.sum(dhv * xv * r, axis=0, keepdims=True)

    row = pl.BlockSpec((tm, D), lambda i: (i, 0))
    vec = pl.BlockSpec((1, D), lambda i: (0, 0))
    ins = [x, g.reshape(1, D), dh] + ([dres] if with_res else [])
    return pl.pallas_call(
        body, name=name, grid=(T // tm,), in_specs=[row, vec, row] + ([row] if with_res else []),
        out_specs=[row, vec],
        out_shape=[jax.ShapeDtypeStruct((T, D), F32), jax.ShapeDtypeStruct((1, D), F32)],
        compiler_params=_params("arbitrary"),
    )(*ins)


def _loss_head(x, g, target, name):
    T, D = x.shape
    tm = _pick(T, 512, SUBLANES)

    def body(x_ref, g_ref, t_ref, loss_ref, dx_ref, dg_ref):
        xv = x_ref[...]
        gv = g_ref[...]
        r = lax.rsqrt(jnp.mean(xv * xv, axis=-1, keepdims=True) + EPS)
        e = xv * r * gv - t_ref[...]
        dy = e * (1.0 / D)
        dyg = dy * gv
        c = jnp.mean(dyg * xv, axis=-1, keepdims=True)
        dx_ref[...] = r * dyg - xv * (r * r * r) * c

        @pl.when(pl.program_id(0) == 0)
        def _():
            loss_ref[...] = jnp.zeros_like(loss_ref)
            dg_ref[...] = jnp.zeros_like(dg_ref)

        loss_ref[...] += jnp.sum(e * e, axis=0, keepdims=True) * (0.5 / D)
        dg_ref[...] += jnp.sum(dy * xv * r, axis=0, keepdims=True)

    row = pl.BlockSpec((tm, D), lambda i: (i, 0))
    vec = pl.BlockSpec((1, D), lambda i: (0, 0))
    return pl.pallas_call(
        body, name=name, grid=(T // tm,), in_specs=[row, vec, row], out_specs=[vec, row, vec],
        out_shape=[jax.ShapeDtypeStruct((1, D), F32), jax.ShapeDtypeStruct((T, D), F32), jax.ShapeDtypeStruct((1, D), F32)],
        compiler_params=_params("arbitrary"),
    )(x, g.reshape(1, D), target)


def _ssm_disc_math(lre, lim, logdt, br, bi):
    dt = jnp.exp(logdt)
    mag = jnp.exp(lre * dt)
    ar = mag * jnp.cos(lim * dt)
    ai = mag * jnp.sin(lim * dt)
    den = lre * lre + lim * lim
    nr = ar - 1.0
    fr = (nr * lre + ai * lim) / den
    fi = (ai * lre - nr * lim) / den
    return ar, ai, fr[None] * br - fi[None] * bi, fr[None] * bi + fi[None] * br


def _ssm_disc(lre, lim, logdt, br, bi, name):
    def body(lre_ref, lim_ref, dt_ref, br_ref, bi_ref, ar_ref, ai_ref, bbr_ref, bbi_ref):
        ar, ai, bbr, bbi = _ssm_disc_math(lre_ref[...], lim_ref[...], dt_ref[...], br_ref[...], bi_ref[...])
        ar_ref[...] = ar
        ai_ref[...] = ai
        bbr_ref[...] = bbr
        bbi_ref[...] = bbi

    sd = jax.ShapeDtypeStruct
    return pl.pallas_call(
        body, name=name, out_shape=[sd(lre.shape, F32), sd(lre.shape, F32), sd(br.shape, F32), sd(br.shape, F32)],
    )(lre, lim, logdt, br, bi)


def _ssm_disc_bwd(lre, lim, logdt, br, bi, dar, dai, dbbr, dbbi, name):
    def body(lre_ref, lim_ref, dt_ref, br_ref, bi_ref, dar_ref, dai_ref, dbbr_ref, dbbi_ref,
             glre_ref, glim_ref, gdt_ref, gbr_ref, gbi_ref):
        _, vjp = jax.vjp(_ssm_disc_math, lre_ref[...], lim_ref[...], dt_ref[...], br_ref[...], bi_ref[...])
        glre, glim, gdt, gbr, gbi = vjp((dar_ref[...], dai_ref[...], dbbr_ref[...], dbbi_ref[...]))
        glre_ref[...] = glre
        glim_ref[...] = glim
        gdt_ref[...] = gdt
        gbr_ref[...] = gbr
        gbi_ref[...] = gbi

    sd = jax.ShapeDtypeStruct
    return pl.pallas_call(
        body, name=name,
        out_shape=[sd(lre.shape, F32), sd(lre.shape, F32), sd(logdt.shape, F32), sd(br.shape, F32), sd(br.shape, F32)],
    )(lre, lim, logdt, br, bi, dar, dai, dbbr, dbbi)


def _shift_segments(v, down):
    n = v.shape[0]
    rows = lax.broadcasted_iota(jnp.int32, v.shape, 0)
    if down:
        return jnp.where(rows >= 1, pltpu.roll(v, 1, 0), 0.0)
    return jnp.where(rows < n - 1, pltpu.roll(v, n - 1, 0), 0.0)


def _cpow(ar, ai, n):
    rr, ri = None, None
    pr, pi = ar, ai
    while n:
        if n & 1:
            if rr is None:
                rr, ri = pr, pi
            else:
                rr, ri = rr * pr - ri * pi, rr * pi + ri * pr
        n >>= 1
        if n:
            pr, pi = pr * pr - pi * pi, 2.0 * pr * pi
    return rr, ri


def _ssm_geometry(T, C):
    seg_steps = T // SSM_SEGMENTS
    kc = min(SSM_CHUNK_STEPS, seg_steps)
    return C // SSM_BLOCK_CH, seg_steps, kc, seg_steps // kc, SSM_SEGMENTS * kc


def _ssm_carries(u, dy, bmat, cmat, amat, *, reverse, name):
    src = dy if reverse else u
    T, C = src.shape
    nblk, seg_steps, kc, nchunk, rc = _ssm_geometry(T, C)
    half = bmat.shape[2] // 2

    def body(src_ref, w_ref, a_ref, out_ref, buf_ref, st_ref):
        c = pl.program_id(1)

        @pl.when(c == 0)
        def _():
            st_ref[...] = jnp.zeros_like(st_ref)

        if reverse:
            buf_ref[...] = _dot(src_ref[...].astype(BF16), w_ref[0], NT)
        else:
            buf_ref[...] = _dot(src_ref[...].astype(BF16), w_ref[0], NN)
        ar = a_ref[0, :, :half]
        ai = a_ref[0, :, half:]
        if reverse:
            ai = -ai

        def step(i, carry):
            xr, xi = carry
            k = (kc - 1 - i) if reverse else i
            row = pl.multiple_of(k * SUBLANES, SUBLANES)
            br = buf_ref[pl.ds(row, SUBLANES), :half]
            bi = buf_ref[pl.ds(row, SUBLANES), half:]
            return ar * xr - ai * xi + br, ar * xi + ai * xr + bi

        xr, xi = lax.fori_loop(0, kc, step, (st_ref[:, :half], st_ref[:, half:]), unroll=8)
        st_ref[:, :half] = xr
        st_ref[:, half:] = xi

        @pl.when(c == nchunk - 1)
        def _():
            pr, pi = _cpow(ar, ai, seg_steps)
            sr = jnp.zeros_like(xr)
            si = jnp.zeros_like(xi)
            for _ in range(SSM_SEGMENTS - 1):
                nr = xr + pr * sr - pi * si
                ni = xi + pr * si + pi * sr
                sr = _shift_segments(nr, not reverse)
                si = _shift_segments(ni, not reverse)
            out_ref[0, :, :half] = sr
            out_ref[0, :, half:] = si

    cidx = (lambda b, c: (nchunk - 1 - c, b)) if reverse else (lambda b, c: (c, b))
    w = cmat if reverse else bmat
    return pl.pallas_call(
        body, name=name, grid=(nblk, nchunk),
        in_specs=[pl.BlockSpec((rc, SSM_BLOCK_CH), cidx),
                  pl.BlockSpec((1,) + w.shape[1:], lambda b, c: (b, 0, 0)),
                  pl.BlockSpec((1, SUBLANES, 2 * half), lambda b, c: (b, 0, 0))],
        out_specs=pl.BlockSpec((1, SUBLANES, 2 * half), lambda b, c: (b, 0, 0)),
        out_shape=jax.ShapeDtypeStruct((nblk, SUBLANES, 2 * half), F32),
        scratch_shapes=[pltpu.VMEM((rc, 2 * half), F32), pltpu.VMEM((SUBLANES, 2 * half), F32)],
        compiler_params=_params("parallel", "arbitrary"),
    )(src, w, amat)


def _ssm_scan(u, bmat, cmat, amat, carries, dvec, name):
    T, C = u.shape
    nblk, seg_steps, kc, nchunk, rc = _ssm_geometry(T, C)
    half = bmat.shape[2] // 2

    def body(u_ref, b_ref, c_ref, a_ref, s_ref, d_ref, y_ref, x_ref, st_ref):
        c = pl.program_id(1)

        @pl.when(c == 0)
        def _():
            st_ref[...] = s_ref[0]

        uv = u_ref[...]
        x_ref[...] = _dot(uv.astype(BF16), b_ref[0], NN)
        ar = a_ref[0, :, :half]
        ai = a_ref[0, :, half:]

        def step(k, carry):
            xr, xi = carry
            row = pl.multiple_of(k * SUBLANES, SUBLANES)
            nr = ar * xr - ai * xi + x_ref[pl.ds(row, SUBLANES), :half]
            ni = ar * xi + ai * xr + x_ref[pl.ds(row, SUBLANES), half:]
            x_ref[pl.ds(row, SUBLANES), :half] = nr
            x_ref[pl.ds(row, SUBLANES), half:] = ni
            return nr, ni

        xr, xi = lax.fori_loop(0, kc, step, (st_ref[:, :half], st_ref[:, half:]), unroll=8)
        st_ref[:, :half] = xr
        st_ref[:, half:] = xi
        y_ref[...] = _dot(x_ref[...].astype(BF16), c_ref[0], NN) + d_ref[...] * uv

    return pl.pallas_call(
        body, name=name, grid=(nblk, nchunk),
        in_specs=[pl.BlockSpec((rc, SSM_BLOCK_CH), lambda b, c: (c, b)),
                  pl.BlockSpec((1,) + bmat.shape[1:], lambda b, c: (b, 0, 0)),
                  pl.BlockSpec((1,) + cmat.shape[1:], lambda b, c: (b, 0, 0)),
                  pl.BlockSpec((1, SUBLANES, 2 * half), lambda b, c: (b, 0, 0)),
                  pl.BlockSpec((1, SUBLANES, 2 * half), lambda b, c: (b, 0, 0)),
                  pl.BlockSpec((1, SSM_BLOCK_CH), lambda b, c: (0, b))],
        out_specs=[pl.BlockSpec((rc, SSM_BLOCK_CH), lambda b, c: (c, b)),
                   pl.BlockSpec((rc, 2 * half), lambda b, c: (c, b))],
        out_shape=[jax.ShapeDtypeStruct((T, C), F32), jax.ShapeDtypeStruct((T, nblk * 2 * half), F32)],
        scratch_shapes=[pltpu.VMEM((SUBLANES, 2 * half), F32)],
        compiler_params=_params("parallel", "arbitrary"),
    )(u, bmat, cmat, amat, carries, dvec)


def _ssm_scan_bwd(dy, u, xs, bmat, cmat, amat, carries, dvec, name):
    T, C = u.shape
    nblk, seg_steps, kc, nchunk, rc = _ssm_geometry(T, C)
    half = bmat.shape[2] // 2
    width = 2 * half

    def body(dy_ref, u_ref, x_ref, xp_ref, b_ref, c_ref, a_ref, s_ref, d_ref,
             du_ref, db_ref, dc_ref, da_ref, dd_ref, g_ref, st_ref, acc_ref):
        c = pl.program_id(1)

        @pl.when(c == 0)
        def _():
            st_ref[...] = s_ref[0]
            acc_ref[...] = jnp.zeros_like(acc_ref)
            db_ref[...] = jnp.zeros_like(db_ref)
            dc_ref[...] = jnp.zeros_like(dc_ref)
            dd_ref[...] = jnp.zeros_like(dd_ref)

        dyv = dy_ref[...]
        uv = u_ref[...]
        dyb = dyv.astype(BF16)
        g_ref[...] = _dot(dyb, c_ref[0], NT)
        ar = a_ref[0, :, :half]
        ai = a_ref[0, :, half:]

        def step(i, carry):
            gr, gi, sr, si = carry
            k = kc - 1 - i
            row = pl.multiple_of(k * SUBLANES, SUBLANES)
            nr = ar * gr + ai * gi + g_ref[pl.ds(row, SUBLANES), :half]
            ni = ar * gi - ai * gr + g_ref[pl.ds(row, SUBLANES), half:]
            g_ref[pl.ds(row, SUBLANES), :half] = nr
            g_ref[pl.ds(row, SUBLANES), half:] = ni
            prow = pl.multiple_of(jnp.maximum(k - 1, 0) * SUBLANES, SUBLANES)
            live = (k >= 1).astype(F32)
            xr = x_ref[pl.ds(prow, SUBLANES), :half] * live
            xi = x_ref[pl.ds(prow, SUBLANES), half:] * live
            return nr, ni, sr + xr * nr + xi * ni, si + xr * ni - xi * nr

        init = (st_ref[:, :half], st_ref[:, half:], acc_ref[:, :half], acc_ref[:, half:])
        gr, gi, sr, si = lax.fori_loop(0, kc, step, init, unroll=8)
        st_ref[:, :half] = gr
        st_ref[:, half:] = gi
        xpr = xp_ref[:, :half]
        xpi = xp_ref[:, half:]
        first = (c == nchunk - 1)
        xpr = jnp.where(first, _shift_segments(xpr, True), xpr)
        xpi = jnp.where(first, _shift_segments(xpi, True), xpi)
        acc_ref[:, :half] = sr + xpr * gr + xpi * gi
        acc_ref[:, half:] = si + xpr * gi - xpi * gr

        gb = g_ref[...].astype(BF16)
        du_ref[...] = _dot(gb, b_ref[0], NT) + dyv * d_ref[...]
        db_ref[0] += _dot(uv.astype(BF16), gb, TN)
        dc_ref[0] += _dot(dyb, x_ref[...].astype(BF16), TN)
        dd_ref[...] += jnp.sum(dyv * uv, axis=0, keepdims=True)

        @pl.when(c == nchunk - 1)
        def _():
            tot = jnp.sum(acc_ref[...], axis=0, keepdims=True)
            da_ref[0] = jnp.broadcast_to(tot, (SUBLANES, width))

    rev = lambda b, c: (nchunk - 1 - c, b)
    blk3 = lambda b, c: (b, 0, 0)
    prev_group = lambda b, c: (((nchunk - 1 - c) * kc - 1 + seg_steps) % seg_steps, b)
    sd = jax.ShapeDtypeStruct
    return pl.pallas_call(
        body, name=name, grid=(nblk, nchunk),
        in_specs=[pl.BlockSpec((rc, SSM_BLOCK_CH), rev), pl.BlockSpec((rc, SSM_BLOCK_CH), rev),
                  pl.BlockSpec((rc, width), rev), pl.BlockSpec((SUBLANES, width), prev_group),
                  pl.BlockSpec((1,) + bmat.shape[1:], blk3), pl.BlockSpec((1,) + cmat.shape[1:], blk3),
                  pl.BlockSpec((1, SUBLANES, width), blk3), pl.BlockSpec((1, SUBLANES, width), blk3),
                  pl.BlockSpec((1, SSM_BLOCK_CH), lambda b, c: (0, b))],
        out_specs=[pl.BlockSpec((rc, SSM_BLOCK_CH), rev), pl.BlockSpec((1, SSM_BLOCK_CH, width), blk3),
                   pl.BlockSpec((1, SSM_BLOCK_CH, width), blk3), pl.BlockSpec((1, SUBLANES, width), blk3),
                   pl.BlockSpec((1, SSM_BLOCK_CH), lambda b, c: (0, b))],
        out_shape=[sd((T, C), F32), sd((nblk, SSM_BLOCK_CH, width), F32), sd((nblk, SSM_BLOCK_CH, width), F32),
                   sd((nblk, SUBLANES, width), F32), sd((1, C), F32)],
        scratch_shapes=[pltpu.VMEM((rc, width), F32), pltpu.VMEM((SUBLANES, width), F32), pltpu.VMEM((SUBLANES, width), F32)],
        compiler_params=_params("parallel", "arbitrary"),
    )(dy, u, xs, xs, bmat, cmat, amat, carries, dvec)


def _ssm_post(y, uz, w_glu, b_glu, name):
    T, C = y.shape
    tm = _pick(T, 512, SUBLANES)

    def body(y_ref, z_ref, w_ref, b_ref, o_ref, a_ref):
        a, _ = _gelu_and_grad(y_ref[...])
        ab = a.astype(BF16)
        sg = _sigmoid(_dot(ab, w_ref[...], NN) + b_ref[...])
        sz, _ = _silu_and_grad(z_ref[...])
        o_ref[...] = (a * sg * sz).astype(BF16)
        a_ref[...] = ab

    row = pl.BlockSpec((tm, C), lambda i: (i, 0))
    return pl.pallas_call(
        body, name=name, grid=(T // tm,),
        in_specs=[row, pl.BlockSpec((tm, C), lambda i: (i, 1)), pl.BlockSpec((C, C), lambda i: (0, 0)),
                  pl.BlockSpec((1, C), lambda i: (0, 0))],
        out_specs=[row, row], out_shape=[jax.ShapeDtypeStruct((T, C), BF16)] * 2, compiler_params=_params("parallel"),
    )(y, uz, w_glu, b_glu.reshape(1, C))


def _ssm_post_bwd(do, y, uz, w_glu, b_glu, name):
    T, C = y.shape
    tm = _pick(T, 512, SUBLANES)

    def body(do_ref, y_ref, z_ref, w_ref, b_ref, dy_ref, dz_ref, ds_ref, db_ref):
        dov = do_ref[...]
        a, da_dy = _gelu_and_grad(y_ref[...])
        sg = _sigmoid(_dot(a.astype(BF16), w_ref[...], NN) + b_ref[...])
        sz, dsz = _silu_and_grad(z_ref[...])
        yg = a * sg
        dz_ref[...] = (dov * yg * dsz).astype(BF16)
        dyg = dov * sz
        ds = dyg * a * sg * (1.0 - sg)
        dsb = ds.astype(BF16)
        ds_ref[...] = dsb
        da = dyg * sg + _dot(dsb, w_ref[...], NT)
        dy_ref[...] = da * da_dy

        @pl.when(pl.program_id(0) == 0)
        def _():
            db_ref[...] = jnp.zeros_like(db_ref)

        db_ref[...] += jnp.sum(ds, axis=0, keepdims=True)

    row = pl.BlockSpec((tm, C), lambda i: (i, 0))
    vec = pl.BlockSpec((1, C), lambda i: (0, 0))
    sd = jax.ShapeDtypeStruct
    return pl.pallas_call(
        body, name=name, grid=(T // tm,),
        in_specs=[row, row, pl.BlockSpec((tm, C), lambda i: (i, 1)), pl.BlockSpec((C, C), lambda i: (0, 0)), vec],
        out_specs=[row, row, row, vec],
        out_shape=[sd((T, C), F32), sd((T, C), BF16), sd((T, C), BF16), sd((1, C), F32)],
        compiler_params=_params("arbitrary"),
    )(do, y, uz, w_glu, b_glu.reshape(1, C))


def _rel_bucket(dist):
    n = jnp.maximum(dist, 0)
    max_exact = NUM_BUCKETS // 2
    n_f = jnp.maximum(n, 1).astype(F32)
    large = max_exact + (jnp.log(n_f / max_exact) / math.log(REL_MAX_DISTANCE / max_exact)
                         * (NUM_BUCKETS - max_exact)).astype(jnp.int32)
    large = jnp.minimum(large, NUM_BUCKETS - 1)
    return jnp.where(n < max_exact, n, large)


def _band_tables():
    qi = jnp.arange(ATTN_BLOCK)[:, None]
    kj = jnp.arange(2 * ATTN_BLOCK)[None, :]
    delta = ATTN_BLOCK + qi - kj
    buckets, bands = [], []
    for window, dilation in ATTN_CONFIGS:
        bands.append((delta >= 0) & (delta <= window // dilation))
        buckets.append(_rel_bucket(jnp.maximum(delta, 0) * dilation))
    return jnp.stack(buckets), jnp.stack(bands)


def _attn_blocks_per_residue(T):
    return [T // (ATTN_BLOCK * d) for _, d in ATTN_CONFIGS]


def _attn_fwd(q, k, v, biasm, name):
    ng, T, gw = q.shape
    hd = ATTN_HEAD_DIM
    nh = gw // hd
    nblk = T // ATTN_BLOCK
    nbs = _attn_blocks_per_residue(T)
    scale = hd ** -0.5
    B = ATTN_BLOCK

    def body(q_ref, kc_ref, kp_ref, vc_ref, vp_ref, bias_ref, o_ref, lse_ref):
        g = pl.program_id(0)
        b = pl.program_id(1)
        nb = jnp.where(g == 0, nbs[0], jnp.where(g == 1, nbs[1], nbs[2]))
        has_prev = (b % nb) != 0
        for h in range(nh):
            hs = slice(h * hd, (h + 1) * hd)
            qh = q_ref[0, :, hs]
            sp = _dot(qh, kp_ref[0, :, hs], NT) * scale + bias_ref[0, h, :, :B]
            sp = jnp.where(has_prev, sp, NEG_INF)
            sc = _dot(qh, kc_ref[0, :, hs], NT) * scale + bias_ref[0, h, :, B:]
            m = jnp.maximum(jnp.max(sp, axis=-1, keepdims=True), jnp.max(sc, axis=-1, keepdims=True))
            pp = jnp.exp(sp - m)
            pc = jnp.exp(sc - m)
            l = jnp.sum(pp, axis=-1, keepdims=True) + jnp.sum(pc, axis=-1, keepdims=True)
            o_ref[0, :, hs] = (_dot((pp / l).astype(BF16), vp_ref[0, :, hs], NN)
                               + _dot((pc / l).astype(BF16), vc_ref[0, :, hs], NN))
            lse_ref[0, :, hs] = jnp.broadcast_to(m + jnp.log(l), (B, hd))

    cur = pl.BlockSpec((1, B, gw), lambda g, b: (g, b, 0))
    prev = pl.BlockSpec((1, B, gw), lambda g, b: (g, jnp.maximum(b - 1, 0), 0))
    return pl.pallas_call(
        body, name=name, grid=(ng, nblk),
        in_specs=[cur, cur, prev, cur, prev, pl.BlockSpec((1, nh, B, 2 * B), lambda g, b: (g, 0, 0, 0))],
        out_specs=[cur, cur], out_shape=[jax.ShapeDtypeStruct(q.shape, F32)] * 2,
        compiler_params=_params("parallel", "parallel"),
    )(q, k, k, v, v, biasm)


def _attn_bwd(q, k, v, do, dvec, lse, biasm, name):
    ng, T, gw = q.shape
    hd = ATTN_HEAD_DIM
    nh = gw // hd
    nblk = T // ATTN_BLOCK
    nbs = _attn_blocks_per_residue(T)
    scale = hd ** -0.5
    B = ATTN_BLOCK

    def body(q_ref, kc_ref, kp_ref, vc_ref, vp_ref, do_ref, dv_ref, lse_ref, bias_ref,
             dq_ref, dk_ref, dvo_ref, dbias_ref, ck_ref, cv_ref):
        g = pl.program_id(0)
        b = pl.program_id(1)
        nb = jnp.where(g == 0, nbs[0], jnp.where(g == 1, nbs[1], nbs[2]))
        has_prev = (b % nb) != 0

        @pl.when(b == 0)
        def _():
            dbias_ref[...] = jnp.zeros_like(dbias_ref)
            ck_ref[...] = jnp.zeros_like(ck_ref)
            cv_ref[...] = jnp.zeros_like(cv_ref)

        @pl.when(b < nblk)
        def _():
            for h in range(nh):
                hs = slice(h * hd, (h + 1) * hd)
                qh = q_ref[0, :, hs]
                doh = do_ref[0, :, hs]
                lse_col = lse_ref[0, :, h * hd:h * hd + 1]
                d_col = dv_ref[0, :, h * hd:h * hd + 1]
                sp = _dot(qh, kp_ref[0, :, hs], NT) * scale + bias_ref[0, h, :, :B]
                sp = jnp.where(has_prev, sp, NEG_INF)
                sc = _dot(qh, kc_ref[0, :, hs], NT) * scale + bias_ref[0, h, :, B:]
                pp = jnp.exp(sp - lse_col)
                pc = jnp.exp(sc - lse_col)
                dsp = pp * (_dot(doh, vp_ref[0, :, hs], NT) + d_col)
                dsc = pc * (_dot(doh, vc_ref[0, :, hs], NT) + d_col)
                dbias_ref[0, h, :, :B] += dsp
                dbias_ref[0, h, :, B:] += dsc
                dspb = dsp.astype(BF16)
                dscb = dsc.astype(BF16)
                dq_ref[0, :, hs] = ((_dot(dspb, kp_ref[0, :, hs], NN) + _dot(dscb, kc_ref[0, :, hs], NN)) * scale).astype(BF16)
                dk_ref[0, :, hs] = (ck_ref[:, hs] + _dot(dspb, qh, TN) * scale).astype(BF16)
                dvo_ref[0, :, hs] = (cv_ref[:, hs] + _dot(pp.astype(BF16), doh, TN)).astype(BF16)
                ck_ref[:, hs] = _dot(dscb, qh, TN) * scale
                cv_ref[:, hs] = _dot(pc.astype(BF16), doh, TN)

        @pl.when(b == nblk)
        def _():
            dk_ref[0] = ck_ref[...].astype(BF16)
            dvo_ref[0] = cv_ref[...].astype(BF16)

    last = nblk - 1
    cur = pl.BlockSpec((1, B, gw), lambda g, b: (g, jnp.minimum(b, last), 0))
    prev = pl.BlockSpec((1, B, gw), lambda g, b: (g, jnp.clip(b - 1, 0, last), 0))
    tab = pl.BlockSpec((1, nh, B, 2 * B), lambda g, b: (g, 0, 0, 0))
    sd = jax.ShapeDtypeStruct
    return pl.pallas_call(
        body, name=name, grid=(ng, nblk + 1),
        in_specs=[cur, cur, prev, cur, prev, cur, cur, cur, tab],
        out_specs=[cur, prev, prev, tab],
        out_shape=[sd(q.shape, BF16), sd(q.shape, BF16), sd(q.shape, BF16), sd(biasm.shape, F32)],
        scratch_shapes=[pltpu.VMEM((B, gw), F32), pltpu.VMEM((B, gw), F32)],
        compiler_params=_params("parallel", "arbitrary"),
    )(q, k, k, v, v, do, dvec, lse, biasm)


def _attn_mix(o, lse, z, name):
    T, C = o.shape
    gw = C // len(ATTN_CONFIGS)
    tm = _pick(T, 512, SUBLANES)

    def body(o_ref, lse_ref, z_ref, out_ref):
        ls = [lse_ref[:, i * gw:(i + 1) * gw] for i in range(3)]
        mx = jnp.maximum(jnp.maximum(ls[0], ls[1]), ls[2])
        es = [jnp.exp(l - mx) for l in ls]
        den = es[0] + es[1] + es[2]
        for i in range(3):
            sz, _ = _silu_and_grad(z_ref[:, i * gw:(i + 1) * gw])
            out_ref[:, i * gw:(i + 1) * gw] = (o_ref[:, i * gw:(i + 1) * gw] * (es[i] / den) * sz).astype(BF16)

    row = pl.BlockSpec((tm, C), lambda i: (i, 0))
    return pl.pallas_call(
        body, name=name, grid=(T // tm,), in_specs=[row, row, row], out_specs=row,
        out_shape=jax.ShapeDtypeStruct((T, C), BF16), compiler_params=_params("parallel"),
    )(o, lse, z)


def _attn_mix_bwd(dout, o, lse, z, name):
    T, C = o.shape
    gw = C // len(ATTN_CONFIGS)
    tm = _pick(T, 512, SUBLANES)
    head_of = np.arange(gw) // ATTN_HEAD_DIM
    ones = jnp.asarray(head_of[:, None] == head_of[None, :], BF16)

    def body(dout_ref, o_ref, lse_ref, z_ref, ones_ref, dz_ref, do_ref, dv_ref):
        ls = [lse_ref[:, i * gw:(i + 1) * gw] for i in range(3)]
        mx = jnp.maximum(jnp.maximum(ls[0], ls[1]), ls[2])
        es = [jnp.exp(l - mx) for l in ls]
        den = es[0] + es[1] + es[2]
        alphas, ebar = [], 0.0
        for i in range(3):
            sl = slice(i * gw, (i + 1) * gw)
            alpha = es[i] / den
            ov = o_ref[:, sl]
            dv = dout_ref[:, sl]
            sz, dsz = _silu_and_grad(z_ref[:, sl])
            dz_ref[:, sl] = (dv * ov * alpha * dsz).astype(BF16)
            da = dv * sz
            do_ref[:, sl] = (da * alpha).astype(BF16)
            t = da * ov
            t1 = t.astype(BF16)
            r1 = t - t1.astype(F32)
            t2 = r1.astype(BF16)
            t3 = (r1 - t2.astype(F32)).astype(BF16)
            e = _dot(t1, ones_ref[...], NN) + _dot(t2, ones_ref[...], NN) + _dot(t3, ones_ref[...], NN)
            ebar = ebar + alpha * e
            alphas.append(alpha)
        for i in range(3):
            dv_ref[:, i * gw:(i + 1) * gw] = -alphas[i] * ebar

    row = pl.BlockSpec((tm, C), lambda i: (i, 0))
    sd = jax.ShapeDtypeStruct
    return pl.pallas_call(
        body, name=name, grid=(T // tm,),
        in_specs=[row, row, row, row, pl.BlockSpec((gw, gw), lambda i: (0, 0))], out_specs=[row, row, row],
        out_shape=[sd((T, C), BF16), sd((T, C), BF16), sd((T, C), F32)], compiler_params=_params("parallel"),
    )(dout, o, lse, z, ones)


def _mem_attn(qz, kv, name):
    T = qz.shape[0]
    dm = qz.shape[1] // 2
    M = kv.shape[0]
    hd = dm // MEM_HEADS
    scale = hd ** -0.5
    tm = _pick(T, 512, SUBLANES)

    def body(q_ref, z_ref, k_ref, v_ref, o_ref):
        for h in range(MEM_HEADS):
            sl = slice(h * hd, (h + 1) * hd)
            s = _dot(q_ref[:, sl].astype(BF16), k_ref[:, sl], NT) * scale
            p = jnp.exp(s - jnp.max(s, axis=-1, keepdims=True))
            pn = p / jnp.sum(p, axis=-1, keepdims=True)
            sz, _ = _silu_and_grad(z_ref[:, sl])
            o_ref[:, sl] = (_dot(pn.astype(BF16), v_ref[:, sl], NN) * sz).astype(BF16)

    return pl.pallas_call(
        body, name=name, grid=(T // tm,),
        in_specs=[pl.BlockSpec((tm, dm), lambda i: (i, 0)), pl.BlockSpec((tm, dm), lambda i: (i, 1)),
                  pl.BlockSpec((M, dm), lambda i: (0, 0)), pl.BlockSpec((M, dm), lambda i: (0, 1))],
        out_specs=pl.BlockSpec((tm, dm), lambda i: (i, 0)),
        out_shape=jax.ShapeDtypeStruct((T, dm), BF16), compiler_params=_params("parallel"),
    )(qz, qz, kv, kv)


def _mem_attn_bwd(do, qz, kv, name):
    T = qz.shape[0]
    dm = qz.shape[1] // 2
    M = kv.shape[0]
    hd = dm // MEM_HEADS
    scale = hd ** -0.5
    tm = _pick(T, 512, SUBLANES)

    def body(do_ref, q_ref, z_ref, k_ref, v_ref, dq_ref, dz_ref, dk_ref, dv_ref):
        @pl.when(pl.program_id(0) == 0)
        def _():
            dk_ref[...] = jnp.zeros_like(dk_ref)
            dv_ref[...] = jnp.zeros_like(dv_ref)

        for h in range(MEM_HEADS):
            sl = slice(h * hd, (h + 1) * hd)
            qb = q_ref[:, sl].astype(BF16)
            s = _dot(qb, k_ref[:, sl], NT) * scale
            p = jnp.exp(s - jnp.max(s, axis=-1, keepdims=True))
            pn = p / jnp.sum(p, axis=-1, keepdims=True)
            pnb = pn.astype(BF16)
            o = _dot(pnb, v_ref[:, sl], NN)
            sz, dsz = _silu_and_grad(z_ref[:, sl])
            dov = do_ref[:, sl]
            dz_ref[:, sl] = (dov * o * dsz).astype(BF16)
            dob = (dov * sz).astype(BF16)
            dp = _dot(dob, v_ref[:, sl], NT)
            ds = pn * (dp - jnp.sum(dp * pn, axis=-1, keepdims=True))
            dsb = ds.astype(BF16)
            dq_ref[:, sl] = (_dot(dsb, k_ref[:, sl], NN) * scale).astype(BF16)
            dk_ref[:, sl] += _dot(dsb, qb, TN) * scale
            dv_ref[:, sl] += _dot(pnb, dob, TN)

    rowq = pl.BlockSpec((tm, dm), lambda i: (i, 0))
    rowz = pl.BlockSpec((tm, dm), lambda i: (i, 1))
    kb = pl.BlockSpec((M, dm), lambda i: (0, 0))
    vb = pl.BlockSpec((M, dm), lambda i: (0, 1))
    sd = jax.ShapeDtypeStruct
    dq, dz, dk, dv = pl.pallas_call(
        body, name=name, grid=(T // tm,), in_specs=[rowq, rowq, rowz, kb, vb],
        out_specs=[rowq, rowq, kb, kb],
        out_shape=[sd((T, dm), BF16), sd((T, dm), BF16), sd((M, dm), F32), sd((M, dm), F32)],
        compiler_params=_params("arbitrary"),
    )(do, qz, qz, kv, kv)
    return dq, dz, dk, dv


def _merge(bps, logits, b_gate, name):
    T, D = bps[0].shape
    tm = _pick(T, 512, SUBLANES)

    def body(p0_ref, p1_ref, p2_ref, l_ref, b_ref, o_ref):
        acc = 0.0
        for i, p_ref in enumerate((p0_ref, p1_ref, p2_ref)):
            sl = slice(i * D, (i + 1) * D)
            acc = acc + _sigmoid(l_ref[:, sl] + b_ref[:, sl]) * p_ref[...]
        o_ref[...] = acc.astype(BF16)

    row = pl.BlockSpec((tm, D), lambda i: (i, 0))
    return pl.pallas_call(
        body, name=name, grid=(T // tm,),
        in_specs=[row, row, row, pl.BlockSpec((tm, 3 * D), lambda i: (i, 0)), pl.BlockSpec((1, 3 * D), lambda i: (0, 0))],
        out_specs=row, out_shape=jax.ShapeDtypeStruct((T, D), BF16), compiler_params=_params("parallel"),
    )(*bps, logits, b_gate.reshape(1, 3 * D))


def _merge_bwd(dmerged, bps, logits, b_gate, name):
    T, D = bps[0].shape
    tm = _pick(T, 512, SUBLANES)

    def body(dm_ref, p0_ref, p1_ref, p2_ref, l_ref, b_ref, d0_ref, d1_ref, d2_ref, dl_ref, db_ref):
        @pl.when(pl.program_id(0) == 0)
        def _():
            db_ref[...] = jnp.zeros_like(db_ref)

        dmv = dm_ref[...]
        for i, (p_ref, d_ref) in enumerate(((p0_ref, d0_ref), (p1_ref, d1_ref), (p2_ref, d2_ref))):
            sl = slice(i * D, (i + 1) * D)
            gt = _sigmoid(l_ref[:, sl] + b_ref[:, sl])
            d_ref[...] = (dmv * gt).astype(BF16)
            dl = dmv * p_ref[...] * gt * (1.0 - gt)
            dl_ref[:, sl] = dl.astype(BF16)
            db_ref[:, sl] += jnp.sum(dl, axis=0, keepdims=True)

    row = pl.BlockSpec((tm, D), lambda i: (i, 0))
    wide = pl.BlockSpec((tm, 3 * D), lambda i: (i, 0))
    vec = pl.BlockSpec((1, 3 * D), lambda i: (0, 0))
    sd = jax.ShapeDtypeStruct
    return pl.pallas_call(
        body, name=name, grid=(T // tm,), in_specs=[row, row, row, row, wide, vec],
        out_specs=[row, row, row, wide, vec],
        out_shape=[sd((T, D), BF16)] * 3 + [sd((T, 3 * D), BF16), sd((1, 3 * D), F32)],
        compiler_params=_params("arbitrary"),
    )(dmerged, *bps, logits, b_gate.reshape(1, 3 * D))


def _to_segments(a):
    T, C = a.shape
    return a.reshape(SSM_SEGMENTS, T // SSM_SEGMENTS, C).transpose(1, 0, 2).reshape(T, C)


def _from_segments(a):
    T, C = a.shape
    return a.reshape(T // SSM_SEGMENTS, SSM_SEGMENTS, C).transpose(1, 0, 2).reshape(T, C)


def _to_residues(a):
    T = a.shape[0]
    gw = HEADS_PER_GROUP * ATTN_HEAD_DIM
    out = []
    for g, (_, r) in enumerate(ATTN_CONFIGS):
        ag = a[:, g * gw:(g + 1) * gw].reshape(T // r, r, gw)
        out.append(ag.transpose(1, 0, 2).reshape(T, gw))
    return jnp.stack(out)


def _from_residues(a):
    _, T, gw = a.shape
    out = []
    for g, (_, r) in enumerate(ATTN_CONFIGS):
        out.append(a[g].reshape(r, T // r, gw).transpose(1, 0, 2).reshape(T, gw))
    return jnp.concatenate(out, axis=1)


def _block_diag(w):
    nblk, ng, a, b = w.shape
    eye = jnp.eye(ng, dtype=w.dtype)
    return (w[:, :, :, None, :] * eye[None, :, None, :, None]).reshape(nblk, ng * a, ng * b)


def _block_diag_part(m, a, b):
    nblk = m.shape[0]
    ng = m.shape[1] // a
    m5 = m.reshape(nblk, ng, a, ng, b)
    eye = jnp.eye(ng, dtype=m.dtype)
    return jnp.sum(m5 * eye[None, :, None, :, None], axis=3)


def _ssm_matrices(p, L, tag):
    G, P = p["ssm_lambda_re"].shape[1:]
    Hg = SSM_GROUP
    gpb = SSM_BLOCK_CH // Hg
    nblk = G // gpb
    br = p["ssm_b_re"][L].transpose(2, 0, 1)
    bi = p["ssm_b_im"][L].transpose(2, 0, 1)
    disc_in = (p["ssm_lambda_re"][L], p["ssm_lambda_im"][L], p["ssm_log_dt"][L].reshape(G, 1), br, bi)
    ar, ai, bbr, bbi = _ssm_disc(*disc_in, name=f"ssm_disc_{tag}")
    amat = jnp.concatenate([ar.reshape(nblk, gpb * P), ai.reshape(nblk, gpb * P)], axis=1)
    amat = jnp.broadcast_to(amat[:, None, :], (nblk, SUBLANES, 2 * gpb * P))
    bbr_g = bbr.transpose(1, 0, 2).reshape(nblk, gpb, Hg, P)
    bbi_g = bbi.transpose(1, 0, 2).reshape(nblk, gpb, Hg, P)
    bmat = jnp.concatenate([_block_diag(bbr_g), _block_diag(bbi_g)], axis=2).astype(BF16)
    cre = p["ssm_c_re"][L].reshape(nblk, gpb, Hg, P).transpose(0, 1, 3, 2)
    cim = p["ssm_c_im"][L].reshape(nblk, gpb, Hg, P).transpose(0, 1, 3, 2)
    cmat = jnp.concatenate([_block_diag(cre), -_block_diag(cim)], axis=1).astype(BF16)
    return disc_in, amat, bmat, cmat


def _layer_fwd(x, mem, p, wb, L, biasm):
    T, D = x.shape
    C = p["ssm_d"].shape[1]
    dm = wb["w_br_mem"].shape[1]
    tag = f"l{L}"
    s = {"x": x}
    h = _rmsnorm(x, p["norm_g"][L], f"norm_{tag}")
    offs = [int(o) for o in np.cumsum([0, 2 * C, 3 * 768, 768, 2 * dm, 3 * D])]
    names = ("uz", "qkv", "z_attn", "qz_mem", "logits")
    dts = (F32, BF16, F32, F32, F32)
    for i, (nm, dt) in enumerate(zip(names, dts)):
        s[nm] = _matmul(h, wb["w_in"], mode="nn", name=f"in_{nm}_{tag}", out_dtype=dt, b_lead=L, b_off=offs[i],
                        n_cols=offs[i + 1] - offs[i])
    s["h"] = h

    disc_in, amat, bmat, cmat = _ssm_matrices(p, L, tag)
    u_seg = _to_segments(s["uz"][:, :C])
    dvec = p["ssm_d"][L].reshape(1, C)
    car = _ssm_carries(u_seg, None, bmat, cmat, amat, reverse=False, name=f"ssm_carry_{tag}")
    y_seg, xs = _ssm_scan(u_seg, bmat, cmat, amat, car, dvec, f"ssm_scan_{tag}")
    y = _from_segments(y_seg)
    o_ssm, a_glu = _ssm_post(y, s["uz"], wb["w_glu"][L], p["b_glu"][L], f"ssm_post_{tag}")
    s.update(disc_in=disc_in, amat=amat, bmat=bmat, cmat=cmat, u_seg=u_seg, xs=xs, y=y, a_glu=a_glu, o_ssm=o_ssm)

    q, k, v = (_to_residues(s["qkv"][:, i * 768:(i + 1) * 768]) for i in range(3))
    o_r, lse_r = _attn_fwd(q, k, v, biasm, f"attn_{tag}")
    o_nat, lse_nat = _from_residues(o_r), _from_residues(lse_r)
    o_attn = _attn_mix(o_nat, lse_nat, s["z_attn"], f"attn_mix_{tag}")
    s.update(q=q, k=k, v=v, lse_r=lse_r, o_nat=o_nat, lse_nat=lse_nat, o_attn=o_attn)

    mn = _rmsnorm(mem, p["mem_norm_g"][L], f"mem_norm_{tag}")
    kv = _matmul(mn, wb["w_mem_kv"], mode="nn", name=f"mem_kv_{tag}", out_dtype=BF16, b_lead=L)
    o_mem = _mem_attn(s["qz_mem"], kv, f"mem_attn_{tag}")
    s.update(mn=mn, kv=kv, o_mem=o_mem)

    bps = [_matmul(o, wb[n], mode="nn", name=f"br_{n}_{tag}", b_lead=L)
           for o, n in ((o_ssm, "w_br_ssm"), (o_attn, "w_br_attn"), (o_mem, "w_br_mem"))]
    merged = _merge(bps, s["logits"], p["b_gate"][L], f"merge_{tag}")
    s.update(bps=bps, merged=merged)
    x_new = _matmul(merged, wb["w_out"], mode="nn", name=f"out_{tag}", add=x, b_lead=L)
    return x_new, s


def _layer_bwd(dx, mem, p, wb, L, s, biasm, gprev):
    T, D = dx.shape
    C = p["ssm_d"].shape[1]
    depth = p["norm_g"].shape[0]
    tag = f"l{L}"
    g = {}

    def wgrad(n, a, b):
        g[n] = _matmul(a, b, mode="tn", name=f"d{n}_{tag}", out_dtype=BF16, stack=(L, depth, gprev.get(n)))

    dmerged = _matmul(dx, wb["w_out"], mode="nt", name=f"d_merged_{tag}", b_lead=L)
    wgrad("w_out", s["merged"], dx)
    dbp0, dbp1, dbp2, dlogits, g["b_gate"] = _merge_bwd(dmerged, s["bps"], s["logits"], p["b_gate"][L], f"merge_bwd_{tag}")
    dos = []
    for dbp, o, n in ((dbp0, s["o_ssm"], "w_br_ssm"), (dbp1, s["o_attn"], "w_br_attn"), (dbp2, s["o_mem"], "w_br_mem")):
        dos.append(_matmul(dbp, wb[n], mode="nt", name=f"d_o_{n}_{tag}", b_lead=L))
        wgrad(n, o, dbp)

    dy, dz_ssm, ds_glu, g["b_glu"] = _ssm_post_bwd(dos[0], s["y"], s["uz"], wb["w_glu"][L], p["b_glu"][L], f"ssm_post_bwd_{tag}")
    wgrad("w_glu", s["a_glu"], ds_glu)
    dy_seg = _to_segments(dy)
    dvec = p["ssm_d"][L].reshape(1, C)
    rcar = _ssm_carries(None, dy_seg, s["bmat"], s["cmat"], s["amat"], reverse=True, name=f"ssm_rcarry_{tag}")
    du_seg, dbm, dct, dam, g["ssm_d"] = _ssm_scan_bwd(dy_seg, s["u_seg"], s["xs"], s["bmat"], s["cmat"], s["amat"], rcar, dvec,
                                                      f"ssm_scan_bwd_{tag}")
    du = _from_segments(du_seg)
    G, P = p["ssm_lambda_re"].shape[1:]
    Hg = SSM_GROUP
    half = dbm.shape[2] // 2
    dbbr = _block_diag_part(dbm[:, :, :half], Hg, P).reshape(G, Hg, P).transpose(1, 0, 2)
    dbbi = _block_diag_part(dbm[:, :, half:], Hg, P).reshape(G, Hg, P).transpose(1, 0, 2)
    g["ssm_c_re"] = _block_diag_part(dct[:, :, :half], Hg, P).reshape(G, Hg, P)
    g["ssm_c_im"] = -_block_diag_part(dct[:, :, half:], Hg, P).reshape(G, Hg, P)
    dar = dam[:, 0, :half].reshape(G, P)
    dai = dam[:, 0, half:].reshape(G, P)
    glre, glim, gdt, gbr, gbi = _ssm_disc_bwd(*s["disc_in"], dar, dai, dbbr, dbbi, name=f"ssm_disc_bwd_{tag}")
    g["ssm_lambda_re"], g["ssm_lambda_im"], g["ssm_log_dt"] = glre, glim, gdt.reshape(G)
    g["ssm_b_re"] = gbr.transpose(1, 2, 0)
    g["ssm_b_im"] = gbi.transpose(1, 2, 0)

    dz_attn, do_nat, dvec_nat = _attn_mix_bwd(dos[1], s["o_nat"], s["lse_nat"], s["z_attn"], f"attn_mix_bwd_{tag}")
    dq_r, dk_r, dv_r, dbias = _attn_bwd(s["q"], s["k"], s["v"], _to_residues(do_nat), _to_residues(dvec_nat), s["lse_r"], biasm,
                                        f"attn_bwd_{tag}")
    dqkv = [_from_residues(a) for a in (dq_r, dk_r, dv_r)]

    dq_mem, dz_mem, dk_mem, dv_mem = _mem_attn_bwd(dos[2], s["qz_mem"], s["kv"], f"mem_attn_bwd_{tag}")
    dkv = jnp.concatenate([dk_mem, dv_mem], axis=1)
    wgrad("w_mem_kv", s["mn"], dkv)
    dmn = _matmul(dkv, wb["w_mem_kv"], mode="nt", name=f"d_mn_{tag}", b_lead=L)
    _, g["mem_norm_g"] = _rmsnorm_bwd(mem, p["mem_norm_g"][L], dmn, None, f"mem_norm_bwd_{tag}")

    dproj = jnp.concatenate([du.astype(BF16), dz_ssm] + dqkv + [dz_attn, dq_mem, dz_mem, dlogits], axis=1)
    dh = _matmul(dproj, wb["w_in"], mode="nt", name=f"d_h_{tag}", b_lead=L)
    wgrad("w_in", s["h"], dproj)
    dx_in, g["norm_g"] = _rmsnorm_bwd(s["x"], p["norm_g"][L], dh, dx, f"norm_bwd_{tag}")
    return dx_in, g, dbias


def _bucket_onehot(gi):
    buckets, bands = _band_tables()
    hit = (buckets[gi].reshape(1, -1) == jnp.arange(NUM_BUCKETS)[:, None]) & bands[gi].reshape(1, -1)
    return hit.astype(BF16)


def _bias_tables(rel_bias, name):
    _, bands = _band_tables()
    out = []
    for gi in range(len(ATTN_CONFIGS)):
        tab = rel_bias[:, gi * HEADS_PER_GROUP:(gi + 1) * HEADS_PER_GROUP].T
        flat = _matmul(tab, _bucket_onehot(gi), mode="nn", name=f"{name}_{gi}", split_a=3, tn=4096)
        out.append(jnp.where(bands[gi][None], flat.reshape(HEADS_PER_GROUP, ATTN_BLOCK, 2 * ATTN_BLOCK), NEG_INF))
    return jnp.stack(out)


def _rel_bias_grad(dbias_sum, name):
    cols = []
    for gi in range(len(ATTN_CONFIGS)):
        flat = dbias_sum[gi].reshape(HEADS_PER_GROUP, -1)
        cols.append(_matmul(flat, _bucket_onehot(gi), mode="nt", name=f"{name}_{gi}", split_a=2, tk=4096).T)
    return jnp.concatenate(cols, axis=1)


def _local_step(x, mem, target, p, wb):
    depth = p["norm_g"].shape[0]
    biasm = _bias_tables(p["rel_bias"], "bias_table")
    saved = []
    for L in range(depth):
        x, s = _layer_fwd(x, mem, p, wb, L, biasm)
        saved.append(s)
    loss_vec, dx, dgf = _loss_head(x, p["final_norm_g"], target, "loss_head")
    grads = {"final_norm_g": dgf.reshape(-1)}
    per_layer = [None] * depth
    dbias_sum = 0.0
    stacked = {}
    for L in reversed(range(depth)):
        dx, per_layer[L], dbias = _layer_bwd(dx, mem, p, wb, L, saved[L], biasm, stacked)
        stacked = {n: per_layer[L][n] for n, _ in BIG}
        dbias_sum = dbias_sum + dbias
    grads.update(stacked)
    for n in per_layer[0]:
        if n not in stacked:
            grads[n] = jnp.stack([per_layer[L][n].reshape(p[n].shape[1:]) for L in range(depth)])
    grads["rel_bias"] = _rel_bias_grad(dbias_sum, "d_rel_bias")
    return jnp.sum(loss_vec), dx, grads


def _chip_coords(j):
    return j // 2, j % 2


def _gather_shards(shards, axes, name):
    n = len(shards)
    widths = [a.shape[ax] for a, ax in zip(shards, axes)]
    aligns = [LANES if ax == 2 else 16 for ax in axes]

    def body(*refs):
        ins, outs = refs[:n], refs[n:2 * n]
        send_sems, recv_sems, fsend_sems, frecv_sems, loc_sems = refs[2 * n:]
        x, y, c = lax.axis_index("x"), lax.axis_index("y"), lax.axis_index("c")
        mine = 2 * x + y
        sibling = (x, y, 1 - c)

        def window(t, layer, j):
            start = pl.ds(pl.multiple_of(j * widths[t], aligns[t]), widths[t])
            return outs[t].at[(layer, start, slice(None)) if axes[t] == 1 else (layer, slice(None), start)]

        def local(t, layer):
            return pltpu.make_async_copy(ins[t].at[layer], window(t, layer, mine), loc_sems.at[t, layer])

        def over_ici(t, j, block):
            return pltpu.make_async_remote_copy(
                src_ref=ins[t].at[c], dst_ref=window(t, c, block), send_sem=send_sems.at[t, j], recv_sem=recv_sems.at[t, block],
                device_id=(*_chip_coords(j), c), device_id_type=MESH)

        def over_d2d(t, j, layer):
            return pltpu.make_async_remote_copy(
                src_ref=window(t, layer, j), dst_ref=window(t, layer, j), send_sem=fsend_sems.at[t, j],
                recv_sem=frecv_sems.at[t, j], device_id=sibling, device_id_type=MESH)

        for t in range(n):
            for layer in range(2):
                local(t, layer).start()
            for j in range(N_CHIPS):
                @pl.when(j != mine)
                def _():
                    over_ici(t, j, mine).start()
        for t in range(n):
            for j in range(N_CHIPS):
                @pl.when(j != mine)
                def _():
                    over_ici(t, j, j).wait_recv()
                    over_d2d(t, j, c).start()
        for t in range(n):
            for j in range(N_CHIPS):
                @pl.when(j != mine)
                def _():
                    over_ici(t, j, mine).wait_send()
                    over_d2d(t, j, c).wait_send()
                    over_d2d(t, j, 1 - c).wait_recv()
            for layer in range(2):
                local(t, layer).wait()

    def full_shape(a, ax):
        return a.shape[:ax] + (N_CHIPS * a.shape[ax],) + a.shape[ax + 1:]

    sem = pltpu.SemaphoreType.DMA
    return pl.pallas_call(
        body, name=name, in_specs=[HBM] * n, out_specs=[HBM] * n,
        out_shape=[jax.ShapeDtypeStruct(full_shape(a, ax), a.dtype) for a, ax in zip(shards, axes)],
        scratch_shapes=[sem((n, N_CHIPS)), sem((n, N_CHIPS)), sem((n, N_CHIPS)), sem((n, N_CHIPS)), sem((n, 2))],
    )(*shards)


def _scatter_slices(arrays, axes, name):
    n = len(arrays)

    def piece(a, ax):
        if ax is None:
            return a.shape, None
        w = a.shape[ax] // N_CHIPS
        return a.shape[:ax] + (w,) + a.shape[ax + 1:], w

    shapes = [piece(a, ax) for a, ax in zip(arrays, axes)]

    def body(*refs):
        ins, outs = refs[:n], refs[n:2 * n]
        send_sems, recv_sems, loc_sems = refs[2 * n:]
        x, y, c = lax.axis_index("x"), lax.axis_index("y"), lax.axis_index("c")
        mine = 2 * x + y

        def src(t, j):
            ax, w = axes[t], shapes[t][1]
            if ax is None:
                return ins[t]
            idx = tuple(pl.ds(j * w, w) if d == ax else slice(None) for d in range(len(arrays[t].shape)))
            return ins[t].at[idx]

        for t in range(n):
            for j in range(N_CHIPS):
                @pl.when(j == mine)
                def _():
                    pltpu.make_async_copy(src(t, j), outs[t].at[j], loc_sems.at[t]).start()

                @pl.when(j != mine)
                def _():
                    pltpu.make_async_remote_copy(
                        src_ref=src(t, j), dst_ref=outs[t].at[mine], send_sem=send_sems.at[t, j], recv_sem=recv_sems.at[t, mine],
                        device_id=(*_chip_coords(j), c), device_id_type=MESH).start()
        for t in range(n):
            for j in range(N_CHIPS):
                @pl.when(j == mine)
                def _():
                    pltpu.make_async_copy(src(t, j), outs[t].at[j], loc_sems.at[t]).wait()

                @pl.when(j != mine)
                def _():
                    cp = pltpu.make_async_remote_copy(
                        src_ref=src(t, j), dst_ref=outs[t].at[j], send_sem=send_sems.at[t, j], recv_sem=recv_sems.at[t, j],
                        device_id=(*_chip_coords(j), c), device_id_type=MESH)
                    cp.wait_send()
                    cp.wait_recv()

    return pl.pallas_call(
        body, name=name, in_specs=[HBM] * n, out_specs=[HBM] * n,
        out_shape=[jax.ShapeDtypeStruct((N_CHIPS,) + sh, a.dtype) for a, (sh, _) in zip(arrays, shapes)],
        scratch_shapes=[pltpu.SemaphoreType.DMA((n, N_CHIPS)), pltpu.SemaphoreType.DMA((n, N_CHIPS)), pltpu.SemaphoreType.DMA((n,))],
    )(*arrays)


def _swap_layers(stacked, name):
    n = len(stacked)

    def body(*refs):
        ins, outs = refs[:n], refs[n:2 * n]
        send_sems, recv_sems = refs[2 * n:]
        c = lax.axis_index("c")
        peer = (lax.axis_index("x"), lax.axis_index("y"), 1 - c)
        cps = [pltpu.make_async_remote_copy(src_ref=ins[t].at[1 - c], dst_ref=outs[t], send_sem=send_sems.at[t],
                                            recv_sem=recv_sems.at[t], device_id=peer, device_id_type=MESH) for t in range(n)]
        for cp in cps:
            cp.start()
        for cp in cps:
            cp.wait_send()
            cp.wait_recv()

    return pl.pallas_call(
        body, name=name, in_specs=[HBM] * n, out_specs=[HBM] * n,
        out_shape=[jax.ShapeDtypeStruct(a.shape[1:], a.dtype) for a in stacked],
        scratch_shapes=[pltpu.SemaphoreType.DMA((n,)), pltpu.SemaphoreType.DMA((n,))],
    )(*stacked)


def _merge_layers(halves, name):
    n = len(halves)

    def body(*refs):
        ins, outs = refs[:n], refs[n:2 * n]
        send_sems, recv_sems, loc_sems = refs[2 * n:]
        c = lax.axis_index("c")
        peer = (lax.axis_index("x"), lax.axis_index("y"), 1 - c)
        for t in range(n):
            pltpu.make_async_copy(ins[t], outs[t].at[c], loc_sems.at[t]).start()
            pltpu.make_async_remote_copy(src_ref=ins[t], dst_ref=outs[t].at[c], send_sem=send_sems.at[t], recv_sem=recv_sems.at[t],
                                         device_id=peer, device_id_type=MESH).start()
        for t in range(n):
            cp = pltpu.make_async_remote_copy(src_ref=ins[t], dst_ref=outs[t].at[1 - c], send_sem=send_sems.at[t],
                                              recv_sem=recv_sems.at[t], device_id=peer, device_id_type=MESH)
            cp.wait_send()
            cp.wait_recv()
            pltpu.make_async_copy(ins[t], outs[t].at[c], loc_sems.at[t]).wait()

    sem = pltpu.SemaphoreType.DMA
    return pl.pallas_call(
        body, name=name, in_specs=[HBM] * n, out_specs=[HBM] * n,
        out_shape=[jax.ShapeDtypeStruct((2,) + a.shape, a.dtype) for a in halves],
        scratch_shapes=[sem((n,)), sem((n,)), sem((n,))],
    )(*halves)


def _pair_sum(stacked, landed, core, name):
    _, K, N = stacked.shape
    tr = _pick(K, max(16, (1 << 19) // N // 16 * 16), 16)

    def body(c_ref, s_ref, l_ref, o_ref):
        o_ref[...] = (s_ref[...].astype(F32) + l_ref[...].astype(F32)).astype(o_ref.dtype)

    return pl.pallas_call(
        body, name=name,
        grid_spec=pltpu.PrefetchScalarGridSpec(
            num_scalar_prefetch=1, grid=(K // tr,),
            in_specs=[pl.BlockSpec((None, tr, N), lambda i, c: (c[0], i, 0)), pl.BlockSpec((tr, N), lambda i, c: (i, 0))],
            out_specs=pl.BlockSpec((tr, N), lambda i, c: (i, 0))),
        out_shape=jax.ShapeDtypeStruct((K, N), stacked.dtype), compiler_params=_params("parallel"),
    )(core, stacked, landed)


def _sum_chips(landed, name):
    _, R, C = landed.shape
    tr = _pick(R, max(SUBLANES, (1 << 19) // C // 16 * 16), 16)

    def body(l_ref, o_ref):
        acc = l_ref[0].astype(F32) + l_ref[1].astype(F32)
        acc = acc + l_ref[2].astype(F32)
        o_ref[...] = acc + l_ref[3].astype(F32)

    return pl.pallas_call(
        body, name=name, grid=(R // tr,), in_specs=[pl.BlockSpec((N_CHIPS, tr, C), lambda i: (0, i, 0))],
        out_specs=pl.BlockSpec((tr, C), lambda i: (i, 0)), out_shape=jax.ShapeDtypeStruct((R, C), F32),
        compiler_params=_params("parallel"),
    )(landed)


def _adamw(w, g, m, v, name):
    R, C = w.shape
    tr = _pick(R, max(SUBLANES, (1 << 18) // C // 8 * 8), SUBLANES)
    c1 = 1.0 / (1.0 - ADAM_B1 ** ADAM_STEP)
    c2 = 1.0 / (1.0 - ADAM_B2 ** ADAM_STEP)

    def body(w_ref, g_ref, m_ref, v_ref, d_ref, nm_ref, nv_ref):
        g = g_ref[...]
        nm = ADAM_B1 * m_ref[...] + (1.0 - ADAM_B1) * g
        nv = ADAM_B2 * v_ref[...] + (1.0 - ADAM_B2) * (g * g)
        nm_ref[...] = nm
        nv_ref[...] = nv
        d_ref[...] = -ADAM_LR * ((nm * c1) / (jnp.sqrt(nv * c2) + ADAM_EPS) + ADAM_WD * w_ref[...])

    blk = pl.BlockSpec((tr, C), lambda i: (i, 0))
    return pl.pallas_call(
        body, name=name, grid=(R // tr,), in_specs=[blk] * 4, out_specs=[blk] * 3,
        out_shape=[jax.ShapeDtypeStruct((R, C), F32)] * 3, compiler_params=_params("parallel"),
    )(w, g, m, v)


def _pack_small(d, prefix=""):
    flat = jnp.concatenate([d[prefix + n].astype(F32).reshape(-1) for n in SMALL])
    pad = (-flat.shape[0]) % (2 * 16 * LANES)
    return jnp.pad(flat, (0, pad)).reshape(-1, LANES)


def _unpack_small(packed, shapes):
    flat = packed.reshape(-1)
    out, off = {}, 0
    for n in SMALL:
        size = int(np.prod(shapes[n]))
        out[n] = flat[off:off + size].reshape(shapes[n])
        off += size
    return out


def kernel(*args):
    p = dict(zip(INPUTS, args))
    x, mem, target = p["x"][0], p["mem"][0], p["loss_target"][0]

    names = [n for n, _ in BIG] + ["small"]
    gathered = _gather_shards([p[n].astype(BF16) for n, _ in BIG], [ax for _, ax in BIG], "gather_weights")
    wb = dict(zip(names, gathered))

    loss_part, dx, grads = _local_step(x, mem, target, p, wb)
    loss = lax.psum(loss_part, ("x", "y", "c"))

    core = lax.axis_index("c").astype(jnp.int32).reshape(1)
    stacked = [grads[n] for n, _ in BIG] + [_pack_small(grads).reshape(2, -1, LANES)]
    theirs = _swap_layers(stacked, "swap_layers")
    pair = [_pair_sum(s, o, core, f"pair_sum_{n}") for n, s, o in zip(names, stacked, theirs)]
    landed = _scatter_slices(pair, [ax - 1 for _, ax in BIG] + [None], "scatter_grads")
    reduced = [_sum_chips(ld.reshape(N_CHIPS, -1, ld.shape[-1]), f"sum_chips_{n}") for n, ld in zip(names, landed)]
    total = _merge_layers(reduced, "merge_layers")

    out = {}
    for (n, _), g in zip(BIG, total):
        sh = p[n].shape
        two_d = lambda a: a.reshape(-1, sh[-1])
        res = (g,) + tuple(_adamw(two_d(p[n]), two_d(g), two_d(p["m_" + n]), two_d(p["v_" + n]), f"adamw_{n}"))
        for key, r in zip(("grad_", "delta_", "new_m_", "new_v_"), res):
            out[key + n] = r.reshape(sh)
    g = total[-1].reshape(-1, LANES)
    res = (g,) + tuple(_adamw(_pack_small(p), g, _pack_small(p, "m_"), _pack_small(p, "v_"), "adamw_small"))
    shapes = {n: p[n].shape for n in SMALL}
    for key, r in zip(("grad_", "delta_", "new_m_", "new_v_"), res):
        for n, a in _unpack_small(r, shapes).items():
            out[key + n] = a

    result = [loss, dx.reshape(p["x"].shape)]
    for key in ("grad_", "delta_", "new_m_", "new_v_"):
        result += [out[key + n] for n in WEIGHTS]
    return tuple(result)
```

```python
import math

import jax
import jax.numpy as jnp
import numpy as np
from jax import lax
from jax.experimental import pallas as pl
from jax.experimental.pallas import tpu as pltpu

F32 = jnp.float32
BF16 = jnp.bfloat16
MESH = pl.DeviceIdType.MESH
HBM = pl.BlockSpec(memory_space=pltpu.HBM)

EPS = 1e-6
SSM_GROUP = 16
SSM_STATE = 64
ATTN_HEAD_DIM = 64
HEADS_PER_GROUP = 4
ATTN_CONFIGS = ((128, 1), (512, 4), (2048, 16))
ATTN_BLOCK = 128
NUM_BUCKETS = 32
REL_MAX_DISTANCE = 2048
NEG_INF = -1e30
MEM_HEADS = 4
ADAM_LR = 0.001
ADAM_B1 = 0.9
ADAM_B2 = 0.999
ADAM_EPS = 1e-08
ADAM_WD = 0.01
ADAM_STEP = 10

LANES = 128
SUBLANES = 8
VMEM_LIMIT_BYTES = 48 * 1024 * 1024
SSM_BLOCK_CH = 128
SSM_SEGMENTS = SUBLANES
SSM_CHUNK_STEPS = 128

N_CHIPS = 4
BIG = (("w_in", 2), ("w_glu", 1), ("w_mem_kv", 1), ("w_br_ssm", 2), ("w_br_attn", 2), ("w_br_mem", 2), ("w_out", 1))
SMALL = ("norm_g", "mem_norm_g", "b_gate", "ssm_lambda_re", "ssm_lambda_im", "ssm_log_dt", "ssm_b_re", "ssm_b_im",
         "ssm_c_re", "ssm_c_im", "ssm_d", "b_glu", "rel_bias", "final_norm_g")
WEIGHTS = ("norm_g", "mem_norm_g", "w_in", "b_gate", "ssm_lambda_re", "ssm_lambda_im", "ssm_log_dt", "ssm_b_re",
           "ssm_b_im", "ssm_c_re", "ssm_c_im", "ssm_d", "w_glu", "b_glu", "w_mem_kv", "w_br_ssm", "w_br_attn",
           "w_br_mem", "w_out", "rel_bias", "final_norm_g")
INPUTS = ("x", "mem") + WEIGHTS + ("loss_target",) + tuple("m_" + n for n in WEIGHTS) + tuple("v_" + n for n in WEIGHTS)


def _params(*sem):
    return pltpu.CompilerParams(dimension_semantics=sem, vmem_limit_bytes=VMEM_LIMIT_BYTES)


def _pick(dim, pref, align):
    if dim <= pref:
        return dim
    t = pref - pref % align
    while t >= align:
        if dim % t == 0:
            return t
        t -= align
    return dim


def _sigmoid(v):
    return 1.0 / (1.0 + jnp.exp(-v))


def _silu_and_grad(z):
    s = _sigmoid(z)
    return z * s, s * (1.0 + z * (1.0 - s))


_GELU_C = math.sqrt(2.0 / math.pi)


def _gelu_and_grad(y):
    inner = _GELU_C * (y + 0.044715 * y * y * y)
    t = jnp.tanh(inner)
    g = 0.5 * y * (1.0 + t)
    dg = 0.5 * (1.0 + t) + 0.5 * y * (1.0 - t * t) * _GELU_C * (1.0 + 3.0 * 0.044715 * y * y)
    return g, dg


def _dot(a, b, dims):
    return lax.dot_general(a, b, (dims, ((), ())), preferred_element_type=F32)


NN = ((1,), (0,))
NT = ((1,), (1,))
TN = ((0,), (0,))


def _matmul(a, b, *, mode, name, out_dtype=F32, add=None, split_a=1, tm=1024, tn=768, tk=2304,
            b_lead=None, b_off=0, n_cols=None, stack=None):
    if mode == "tn":
        K, M = a.shape
    else:
        M, K = a.shape
    bshape = b.shape if b_lead is None else b.shape[1:]
    N = n_cols or (bshape[0] if mode == "nt" else bshape[1])
    tm = _pick(M, tm, LANES if mode == "tn" else SUBLANES)
    tn = _pick(math.gcd(N, b_off) if b_off else N, tn, LANES)
    tk = _pick(K, tk, LANES)
    nk = K // tk
    joff = b_off // tn
    dims = {"nn": NN, "nt": NT, "tn": TN}[mode]
    has_add = add is not None
    has_prev = stack is not None and stack[2] is not None

    def body(*refs):
        a_ref, b_ref = refs[:2]
        add_ref = refs[2] if has_add else None
        o_ref, acc_ref = refs[-2:]
        k = pl.program_id(2)
        bv = b_ref[...].astype(BF16)
        if split_a > 1:
            rest = a_ref[...].astype(F32)
            part = 0.0
            for _ in range(split_a):
                piece = rest.astype(BF16)
                part = part + _dot(piece, bv, dims)
                rest = rest - piece.astype(F32)
        else:
            part = _dot(a_ref[...].astype(BF16), bv, dims)

        @pl.when(k == 0)
        def _():
            acc_ref[...] = part

        @pl.when(k > 0)
        def _():
            acc_ref[...] += part

        @pl.when(k == nk - 1)
        def _():
            r = acc_ref[...]
            if has_add:
                r = r + add_ref[...]
            o_ref[...] = r.astype(out_dtype)

    a_spec = pl.BlockSpec((tk, tm), lambda i, j, k: (k, i)) if mode == "tn" else pl.BlockSpec((tm, tk), lambda i, j, k: (i, k))
    lead = () if b_lead is None else (b_lead,)
    lead_blk = () if b_lead is None else (None,)
    if mode == "nt":
        b_spec = pl.BlockSpec(lead_blk + (tn, tk), lambda i, j, k: lead + (j + joff, k))
    else:
        b_spec = pl.BlockSpec(lead_blk + (tk, tn), lambda i, j, k: lead + (k, j + joff))
    in_specs = [a_spec, b_spec]
    args = [a, b]
    if has_add:
        in_specs.append(pl.BlockSpec((tm, tn), lambda i, j, k: (i, j)))
        args.append(add)
    aliases = {}
    if stack is None:
        out_spec = pl.BlockSpec((tm, tn), lambda i, j, k: (i, j))
        out_shape = jax.ShapeDtypeStruct((M, N), out_dtype)
    else:
        layer, depth, prev = stack
        out_spec = pl.BlockSpec((None, tm, tn), lambda i, j, k: (layer, i, j))
        out_shape = jax.ShapeDtypeStruct((depth, M, N), out_dtype)
        if has_prev:
            in_specs.append(pl.BlockSpec(memory_space=pl.ANY))
            args.append(prev)
            aliases = {len(args) - 1: 0}
    return pl.pallas_call(
        body, name=name, grid=(M // tm, N // tn, nk), in_specs=in_specs, out_specs=out_spec, out_shape=out_shape,
        scratch_shapes=[pltpu.VMEM((tm, tn), F32)], input_output_aliases=aliases,
        compiler_params=_params("parallel", "parallel", "arbitrary"),
    )(*args)


def _rmsnorm(x, g, name):
    T, D = x.shape
    tm = _pick(T, 512, SUBLANES)

    def body(x_ref, g_ref, h_ref):
        xv = x_ref[...]
        r = lax.rsqrt(jnp.mean(xv * xv, axis=-1, keepdims=True) + EPS)
        h_ref[...] = (xv * r * g_ref[...]).astype(BF16)

    return pl.pallas_call(
        body, name=name, grid=(T // tm,),
        in_specs=[pl.BlockSpec((tm, D), lambda i: (i, 0)), pl.BlockSpec((1, D), lambda i: (0, 0))],
        out_specs=pl.BlockSpec((tm, D), lambda i: (i, 0)),
        out_shape=jax.ShapeDtypeStruct((T, D), BF16), compiler_params=_params("parallel"),
    )(x, g.reshape(1, D))


def _rmsnorm_bwd(x, g, dh, dres, name):
    T, D = x.shape
    tm = _pick(T, 512, SUBLANES)
    with_res = dres is not None

    def body(*refs):
        if with_res:
            x_ref, g_ref, dh_ref, dres_ref, dx_ref, dg_ref = refs
        else:
            x_ref, g_ref, dh_ref, dx_ref, dg_ref = refs
        xv = x_ref[...]
        dhv = dh_ref[...]
        r = lax.rsqrt(jnp.mean(xv * xv, axis=-1, keepdims=True) + EPS)
        dyg = dhv * g_ref[...]
        c = jnp.mean(dyg * xv, axis=-1, keepdims=True)
        dx = r * dyg - xv * (r * r * r) * c
        if with_res:
            dx = dx + dres_ref[...]
        dx_ref[...] = dx

        @pl.when(pl.program_id(0) == 0)
        def _():
            dg_ref[...] = jnp.zeros_like(dg_ref)

        dg_ref[...] += jnp.sum(dhv * xv * r, axis=0, keepdims=True)

    row = pl.BlockSpec((tm, D), lambda i: (i, 0))
    vec = pl.BlockSpec((1, D), lambda i: (0, 0))
    ins = [x, g.reshape(1, D), dh] + ([dres] if with_res else [])
    return pl.pallas_call(
        body, name=name, grid=(T // tm,), in_specs=[row, vec, row] + ([row] if with_res else []),
        out_specs=[row, vec],
        out_shape=[jax.ShapeDtypeStruct((T, D), F32), jax.ShapeDtypeStruct((1, D), F32)],
        compiler_params=_params("arbitrary"),
    )(*ins)


def _loss_head(x, g, target, name):
    T, D = x.shape
    tm = _pick(T, 512, SUBLANES)

    def body(x_ref, g_ref, t_ref, loss_ref, dx_ref, dg_ref):
        xv = x_ref[...]
        gv = g_ref[...]
        r = lax.rsqrt(jnp.mean(xv * xv, axis=-1, keepdims=True) + EPS)
        e = xv * r * gv - t_ref[...]
        dy = e * (1.0 / D)
        dyg = dy * gv
        c = jnp.mean(dyg * xv, axis=-1, keepdims=True)
        dx_ref[...] = r * dyg - xv * (r * r * r) * c

        @pl.when(pl.program_id(0) == 0)
        def _():
            loss_ref[...] = jnp.zeros_like(loss_ref)
            dg_ref[...] = jnp.zeros_like(dg_ref)

        loss_ref[...] += jnp.sum(e * e, axis=0, keepdims=True) * (0.5 / D)
        dg_ref[...] += jnp.sum(dy * xv * r, axis=0, keepdims=True)

    row = pl.BlockSpec((tm, D), lambda i: (i, 0))
    vec = pl.BlockSpec((1, D), lambda i: (0, 0))
    return pl.pallas_call(
        body, name=name, grid=(T // tm,), in_specs=[row, vec, row], out_specs=[vec, row, vec],
        out_shape=[jax.ShapeDtypeStruct((1, D), F32), jax.ShapeDtypeStruct((T, D), F32), jax.ShapeDtypeStruct((1, D), F32)],
        compiler_params=_params("arbitrary"),
    )(x, g.reshape(1, D), target)


def _ssm_disc_math(lre, lim, logdt, br, bi):
    dt = jnp.exp(logdt)
    mag = jnp.exp(lre * dt)
    ar = mag * jnp.cos(lim * dt)
    ai = mag * jnp.sin(lim * dt)
    den = lre * lre + lim * lim
    nr = ar - 1.0
    fr = (nr * lre + ai * lim) / den
    fi = (ai * lre - nr * lim) / den
    return ar, ai, fr[None] * br - fi[None] * bi, fr[None] * bi + fi[None] * br


def _ssm_disc(lre, lim, logdt, br, bi, name):
    def body(lre_ref, lim_ref, dt_ref, br_ref, bi_ref, ar_ref, ai_ref, bbr_ref, bbi_ref):
        ar, ai, bbr, bbi = _ssm_disc_math(lre_ref[...], lim_ref[...], dt_ref[...], br_ref[...], bi_ref[...])
        ar_ref[...] = ar
        ai_ref[...] = ai
        bbr_ref[...] = bbr
        bbi_ref[...] = bbi

    sd = jax.ShapeDtypeStruct
    return pl.pallas_call(
        body, name=name, out_shape=[sd(lre.shape, F32), sd(lre.shape, F32), sd(br.shape, F32), sd(br.shape, F32)],
    )(lre, lim, logdt, br, bi)


def _ssm_disc_bwd(lre, lim, logdt, br, bi, dar, dai, dbbr, dbbi, name):
    def body(lre_ref, lim_ref, dt_ref, br_ref, bi_ref, dar_ref, dai_ref, dbbr_ref, dbbi_ref,
             glre_ref, glim_ref, gdt_ref, gbr_ref, gbi_ref):
        _, vjp = jax.vjp(_ssm_disc_math, lre_ref[...], lim_ref[...], dt_ref[...], br_ref[...], bi_ref[...])
        glre, glim, gdt, gbr, gbi = vjp((dar_ref[...], dai_ref[...], dbbr_ref[...], dbbi_ref[...]))
        glre_ref[...] = glre
        glim_ref[...] = glim
        gdt_ref[...] = gdt
        gbr_ref[...] = gbr
        gbi_ref[...] = gbi

    sd = jax.ShapeDtypeStruct
    return pl.pallas_call(
        body, name=name,
        out_shape=[sd(lre.shape, F32), sd(lre.shape, F32), sd(logdt.shape, F32), sd(br.shape, F32), sd(br.shape, F32)],
    )(lre, lim, logdt, br, bi, dar, dai, dbbr, dbbi)


def _shift_segments(v, down):
    n = v.shape[0]
    rows = lax.broadcasted_iota(jnp.int32, v.shape, 0)
    if down:
        return jnp.where(rows >= 1, pltpu.roll(v, 1, 0), 0.0)
    return jnp.where(rows < n - 1, pltpu.roll(v, n - 1, 0), 0.0)


def _cpow(ar, ai, n):
    rr, ri = None, None
    pr, pi = ar, ai
    while n:
        if n & 1:
            if rr is None:
                rr, ri = pr, pi
            else:
                rr, ri = rr * pr - ri * pi, rr * pi + ri * pr
        n >>= 1
        if n:
            pr, pi = pr * pr - pi * pi, 2.0 * pr * pi
    return rr, ri


def _ssm_geometry(T, C):
    seg_steps = T // SSM_SEGMENTS
    kc = min(SSM_CHUNK_STEPS, seg_steps)
    return C // SSM_BLOCK_CH, seg_steps, kc, seg_steps // kc, SSM_SEGMENTS * kc


def _ssm_carries(u, dy, bmat, cmat, amat, *, reverse, name):
    src = dy if reverse else u
    T, C = src.shape
    nblk, seg_steps, kc, nchunk, rc = _ssm_geometry(T, C)
    half = bmat.shape[2] // 2

    def body(src_ref, w_ref, a_ref, out_ref, buf_ref, st_ref):
        c = pl.program_id(1)

        @pl.when(c == 0)
        def _():
            st_ref[...] = jnp.zeros_like(st_ref)

        if reverse:
            buf_ref[...] = _dot(src_ref[...].astype(BF16), w_ref[0], NT)
        else:
            buf_ref[...] = _dot(src_ref[...].astype(BF16), w_ref[0], NN)
        ar = a_ref[0, :, :half]
        ai = a_ref[0, :, half:]
        if reverse:
            ai = -ai

        def step(i, carry):
            xr, xi = carry
            k = (kc - 1 - i) if reverse else i
            row = pl.multiple_of(k * SUBLANES, SUBLANES)
            br = buf_ref[pl.ds(row, SUBLANES), :half]
            bi = buf_ref[pl.ds(row, SUBLANES), half:]
            return ar * xr - ai * xi + br, ar * xi + ai * xr + bi

        xr, xi = lax.fori_loop(0, kc, step, (st_ref[:, :half], st_ref[:, half:]), unroll=8)
        st_ref[:, :half] = xr
        st_ref[:, half:] = xi

        @pl.when(c == nchunk - 1)
        def _():
            pr, pi = _cpow(ar, ai, seg_steps)
            sr = jnp.zeros_like(xr)
            si = jnp.zeros_like(xi)
            for _ in range(SSM_SEGMENTS - 1):
                nr = xr + pr * sr - pi * si
                ni = xi + pr * si + pi * sr
                sr = _shift_segments(nr, not reverse)
                si = _shift_segments(ni, not reverse)
            out_ref[0, :, :half] = sr
            out_ref[0, :, half:] = si

    cidx = (lambda b, c: (nchunk - 1 - c, b)) if reverse else (lambda b, c: (c, b))
    w = cmat if reverse else bmat
    return pl.pallas_call(
        body, name=name, grid=(nblk, nchunk),
        in_specs=[pl.BlockSpec((rc, SSM_BLOCK_CH), cidx),
                  pl.BlockSpec((1,) + w.shape[1:], lambda b, c: (b, 0, 0)),
                  pl.BlockSpec((1, SUBLANES, 2 * half), lambda b, c: (b, 0, 0))],
        out_specs=pl.BlockSpec((1, SUBLANES, 2 * half), lambda b, c: (b, 0, 0)),
        out_shape=jax.ShapeDtypeStruct((nblk, SUBLANES, 2 * half), F32),
        scratch_shapes=[pltpu.VMEM((rc, 2 * half), F32), pltpu.VMEM((SUBLANES, 2 * half), F32)],
        compiler_params=_params("parallel", "arbitrary"),
    )(src, w, amat)


def _ssm_scan(u, bmat, cmat, amat, carries, dvec, name):
    T, C = u.shape
    nblk, seg_steps, kc, nchunk, rc = _ssm_geometry(T, C)
    half = bmat.shape[2] // 2

    def body(u_ref, b_ref, c_ref, a_ref, s_ref, d_ref, y_ref, x_ref, st_ref):
        c = pl.program_id(1)

        @pl.when(c == 0)
        def _():
            st_ref[...] = s_ref[0]

        uv = u_ref[...]
        x_ref[...] = _dot(uv.astype(BF16), b_ref[0], NN)
        ar = a_ref[0, :, :half]
        ai = a_ref[0, :, half:]

        def step(k, carry):
            xr, xi = carry
            row = pl.multiple_of(k * SUBLANES, SUBLANES)
            nr = ar * xr - ai * xi + x_ref[pl.ds(row, SUBLANES), :half]
            ni = ar * xi + ai * xr + x_ref[pl.ds(row, SUBLANES), half:]
            x_ref[pl.ds(row, SUBLANES), :half] = nr
            x_ref[pl.ds(row, SUBLANES), half:] = ni
            return nr, ni

        xr, xi = lax.fori_loop(0, kc, step, (st_ref[:, :half], st_ref[:, half:]), unroll=8)
        st_ref[:, :half] = xr
        st_ref[:, half:] = xi
        y_ref[...] = _dot(x_ref[...].astype(BF16), c_ref[0], NN) + d_ref[...] * uv

    return pl.pallas_call(
        body, name=name, grid=(nblk, nchunk),
        in_specs=[pl.BlockSpec((rc, SSM_BLOCK_CH), lambda b, c: (c, b)),
                  pl.BlockSpec((1,) + bmat.shape[1:], lambda b, c: (b, 0, 0)),
                  pl.BlockSpec((1,) + cmat.shape[1:], lambda b, c: (b, 0, 0)),
                  pl.BlockSpec((1, SUBLANES, 2 * half), lambda b, c: (b, 0, 0)),
                  pl.BlockSpec((1, SUBLANES, 2 * half), lambda b, c: (b, 0, 0)),
                  pl.BlockSpec((1, SSM_BLOCK_CH), lambda b, c: (0, b))],
        out_specs=[pl.BlockSpec((rc, SSM_BLOCK_CH), lambda b, c: (c, b)),
                   pl.BlockSpec((rc, 2 * half), lambda b, c: (c, b))],
        out_shape=[jax.ShapeDtypeStruct((T, C), F32), jax.ShapeDtypeStruct((T, nblk * 2 * half), F32)],
        scratch_shapes=[pltpu.VMEM((SUBLANES, 2 * half), F32)],
        compiler_params=_params("parallel", "arbitrary"),
    )(u, bmat, cmat, amat, carries, dvec)


def _ssm_scan_bwd(dy, u, xs, bmat, cmat, amat, carries, dvec, name):
    T, C = u.shape
    nblk, seg_steps, kc, nchunk, rc = _ssm_geometry(T, C)
    half = bmat.shape[2] // 2
    width = 2 * half

    def body(dy_ref, u_ref, x_ref, xp_ref, b_ref, c_ref, a_ref, s_ref, d_ref,
             du_ref, db_ref, dc_ref, da_ref, dd_ref, g_ref, st_ref, acc_ref):
        c = pl.program_id(1)

        @pl.when(c == 0)
        def _():
            st_ref[...] = s_ref[0]
            acc_ref[...] = jnp.zeros_like(acc_ref)
            db_ref[...] = jnp.zeros_like(db_ref)
            dc_ref[...] = jnp.zeros_like(dc_ref)
            dd_ref[...] = jnp.zeros_like(dd_ref)

        dyv = dy_ref[...]
        uv = u_ref[...]
        dyb = dyv.astype(BF16)
        g_ref[...] = _dot(dyb, c_ref[0], NT)
        ar = a_ref[0, :, :half]
        ai = a_ref[0, :, half:]

        def step(i, carry):
            gr, gi, sr, si = carry
            k = kc - 1 - i
            row = pl.multiple_of(k * SUBLANES, SUBLANES)
            nr = ar * gr + ai * gi + g_ref[pl.ds(row, SUBLANES), :half]
            ni = ar * gi - ai * gr + g_ref[pl.ds(row, SUBLANES), half:]
            g_ref[pl.ds(row, SUBLANES), :half] = nr
            g_ref[pl.ds(row, SUBLANES), half:] = ni
            prow = pl.multiple_of(jnp.maximum(k - 1, 0) * SUBLANES, SUBLANES)
            live = (k >= 1).astype(F32)
            xr = x_ref[pl.ds(prow, SUBLANES), :half] * live
            xi = x_ref[pl.ds(prow, SUBLANES), half:] * live
            return nr, ni, sr + xr * nr + xi * ni, si + xr * ni - xi * nr

        init = (st_ref[:, :half], st_ref[:, half:], acc_ref[:, :half], acc_ref[:, half:])
        gr, gi, sr, si = lax.fori_loop(0, kc, step, init, unroll=8)
        st_ref[:, :half] = gr
        st_ref[:, half:] = gi
        xpr = xp_ref[:, :half]
        xpi = xp_ref[:, half:]
        first = (c == nchunk - 1)
        xpr = jnp.where(first, _shift_segments(xpr, True), xpr)
        xpi = jnp.where(first, _shift_segments(xpi, True), xpi)
        acc_ref[:, :half] = sr + xpr * gr + xpi * gi
        acc_ref[:, half:] = si + xpr * gi - xpi * gr

        gb = g_ref[...].astype(BF16)
        du_ref[...] = _dot(gb, b_ref[0], NT) + dyv * d_ref[...]
        db_ref[0] += _dot(uv.astype(BF16), gb, TN)
        dc_ref[0] += _dot(dyb, x_ref[...].astype(BF16), TN)
        dd_ref[...] += jnp.sum(dyv * uv, axis=0, keepdims=True)

        @pl.when(c == nchunk - 1)
        def _():
            tot = jnp.sum(acc_ref[...], axis=0, keepdims=True)
            da_ref[0] = jnp.broadcast_to(tot, (SUBLANES, width))

    rev = lambda b, c: (nchunk - 1 - c, b)
    blk3 = lambda b, c: (b, 0, 0)
    prev_group = lambda b, c: (((nchunk - 1 - c) * kc - 1 + seg_steps) % seg_steps, b)
    sd = jax.ShapeDtypeStruct
    return pl.pallas_call(
        body, name=name, grid=(nblk, nchunk),
        in_specs=[pl.BlockSpec((rc, SSM_BLOCK_CH), rev), pl.BlockSpec((rc, SSM_BLOCK_CH), rev),
                  pl.BlockSpec((rc, width), rev), pl.BlockSpec((SUBLANES, width), prev_group),
                  pl.BlockSpec((1,) + bmat.shape[1:], blk3), pl.BlockSpec((1,) + cmat.shape[1:], blk3),
                  pl.BlockSpec((1, SUBLANES, width), blk3), pl.BlockSpec((1, SUBLANES, width), blk3),
                  pl.BlockSpec((1, SSM_BLOCK_CH), lambda b, c: (0, b))],
        out_specs=[pl.BlockSpec((rc, SSM_BLOCK_CH), rev), pl.BlockSpec((1, SSM_BLOCK_CH, width), blk3),
                   pl.BlockSpec((1, SSM_BLOCK_CH, width), blk3), pl.BlockSpec((1, SUBLANES, width), blk3),
                   pl.BlockSpec((1, SSM_BLOCK_CH), lambda b, c: (0, b))],
        out_shape=[sd((T, C), F32), sd((nblk, SSM_BLOCK_CH, width), F32), sd((nblk, SSM_BLOCK_CH, width), F32),
                   sd((nblk, SUBLANES, width), F32), sd((1, C), F32)],
        scratch_shapes=[pltpu.VMEM((rc, width), F32), pltpu.VMEM((SUBLANES, width), F32), pltpu.VMEM((SUBLANES, width), F32)],
        compiler_params=_params("parallel", "arbitrary"),
    )(dy, u, xs, xs, bmat, cmat, amat, carries, dvec)


def _ssm_post(y, uz, w_glu, b_glu, name):
    T, C = y.shape
    tm = _pick(T, 512, SUBLANES)

    def body(y_ref, z_ref, w_ref, b_ref, o_ref, a_ref):
        a, _ = _gelu_and_grad(y_ref[...])
        ab = a.astype(BF16)
        sg = _sigmoid(_dot(ab, w_ref[...], NN) + b_ref[...])
        sz, _ = _silu_and_grad(z_ref[...])
        o_ref[...] = (a * sg * sz).astype(BF16)
        a_ref[...] = ab

    row = pl.BlockSpec((tm, C), lambda i: (i, 0))
    return pl.pallas_call(
        body, name=name, grid=(T // tm,),
        in_specs=[row, pl.BlockSpec((tm, C), lambda i: (i, 1)), pl.BlockSpec((C, C), lambda i: (0, 0)),
                  pl.BlockSpec((1, C), lambda i: (0, 0))],
        out_specs=[row, row], out_shape=[jax.ShapeDtypeStruct((T, C), BF16)] * 2, compiler_params=_params("parallel"),
    )(y, uz, w_glu, b_glu.reshape(1, C))


def _ssm_post_bwd(do, y, uz, w_glu, b_glu, name):
    T, C = y.shape
    tm = _pick(T, 512, SUBLANES)

    def body(do_ref, y_ref, z_ref, w_ref, b_ref, dy_ref, dz_ref, ds_ref, db_ref):
        dov = do_ref[...]
        a, da_dy = _gelu_and_grad(y_ref[...])
        sg = _sigmoid(_dot(a.astype(BF16), w_ref[...], NN) + b_ref[...])
        sz, dsz = _silu_and_grad(z_ref[...])
        yg = a * sg
        dz_ref[...] = (dov * yg * dsz).astype(BF16)
        dyg = dov * sz
        ds = dyg * a * sg * (1.0 - sg)
        dsb = ds.astype(BF16)
        ds_ref[...] = dsb
        da = dyg * sg + _dot(dsb, w_ref[...], NT)
        dy_ref[...] = da * da_dy

        @pl.when(pl.program_id(0) == 0)
        def _():
            db_ref[...] = jnp.zeros_like(db_ref)

        db_ref[...] += jnp.sum(ds, axis=0, keepdims=True)

    row = pl.BlockSpec((tm, C), lambda i: (i, 0))
    vec = pl.BlockSpec((1, C), lambda i: (0, 0))
    sd = jax.ShapeDtypeStruct
    return pl.pallas_call(
        body, name=name, grid=(T // tm,),
        in_specs=[row, row, pl.BlockSpec((tm, C), lambda i: (i, 1)), pl.BlockSpec((C, C), lambda i: (0, 0)), vec],
        out_specs=[row, row, row, vec],
        out_shape=[sd((T, C), F32), sd((T, C), BF16), sd((T, C), BF16), sd((1, C), F32)],
        compiler_params=_params("arbitrary"),
    )(do, y, uz, w_glu, b_glu.reshape(1, C))


def _rel_bucket(dist):
    n = jnp.maximum(dist, 0)
    max_exact = NUM_BUCKETS // 2
    n_f = jnp.maximum(n, 1).astype(F32)
    large = max_exact + (jnp.log(n_f / max_exact) / math.log(REL_MAX_DISTANCE / max_exact)
                         * (NUM_BUCKETS - max_exact)).astype(jnp.int32)
    large = jnp.minimum(large, NUM_BUCKETS - 1)
    return jnp.where(n < max_exact, n, large)


def _band_tables():
    qi = jnp.arange(ATTN_BLOCK)[:, None]
    kj = jnp.arange(2 * ATTN_BLOCK)[None, :]
    delta = ATTN_BLOCK + qi - kj
    buckets, bands = [], []
    for window, dilation in ATTN_CONFIGS:
        bands.append((delta >= 0) & (delta <= window // dilation))
        buckets.append(_rel_bucket(jnp.maximum(delta, 0) * dilation))
    return jnp.stack(buckets), jnp.stack(bands)


def _attn_blocks_per_residue(T):
    return [T // (ATTN_BLOCK * d) for _, d in ATTN_CONFIGS]


def _attn_fwd(q, k, v, biasm, name):
    ng, T, gw = q.shape
    hd = ATTN_HEAD_DIM
    nh = gw // hd
    nblk = T // ATTN_BLOCK
    nbs = _attn_blocks_per_residue(T)
    scale = hd ** -0.5
    B = ATTN_BLOCK

    def body(q_ref, kc_ref, kp_ref, vc_ref, vp_ref, bias_ref, o_ref, lse_ref, s_ref, p_ref):
        g = pl.program_id(0)
        b = pl.program_id(1)
        nb = jnp.where(g == 0, nbs[0], jnp.where(g == 1, nbs[1], nbs[2]))
        no_prev = (b % nb) == 0
        col = lax.broadcasted_iota(jnp.int32, (1, 2 * B), 1)
        pen = jnp.where((col < B) & no_prev, NEG_INF, 0.0)
        heads = [slice(h * hd, (h + 1) * hd) for h in range(nh)]
        for h, hs in enumerate(heads):
            kw = jnp.concatenate([kp_ref[0, :, hs], kc_ref[0, :, hs]], axis=0)
            s_ref[h] = _dot(q_ref[0, :, hs], kw, NT)
        for h, hs in enumerate(heads):
            s = s_ref[h] * scale + bias_ref[0, h] + pen
            m = jnp.max(s, axis=-1, keepdims=True)
            p = jnp.exp(s - m)
            l = jnp.sum(p, axis=-1, keepdims=True)
            p_ref[h] = (p / l).astype(BF16)
            lse_ref[0, :, hs] = jnp.broadcast_to(m + jnp.log(l), (B, hd))
        for h, hs in enumerate(heads):
            vw = jnp.concatenate([vp_ref[0, :, hs], vc_ref[0, :, hs]], axis=0)
            o_ref[0, :, hs] = _dot(p_ref[h], vw, NN)

    cur = pl.BlockSpec((1, B, gw), lambda g, b: (g, b, 0))
    prev = pl.BlockSpec((1, B, gw), lambda g, b: (g, jnp.maximum(b - 1, 0), 0))
    return pl.pallas_call(
        body, name=name, grid=(ng, nblk),
        in_specs=[cur, cur, prev, cur, prev, pl.BlockSpec((1, nh, B, 2 * B), lambda g, b: (g, 0, 0, 0))],
        out_specs=[cur, cur], out_shape=[jax.ShapeDtypeStruct(q.shape, F32)] * 2,
        scratch_shapes=[pltpu.VMEM((nh, B, 2 * B), F32), pltpu.VMEM((nh, B, 2 * B), BF16)],
        compiler_params=_params("parallel", "parallel"),
    )(q, k, k, v, v, biasm)


def _attn_bwd(q, k, v, do, dvec, lse, biasm, name):
    ng, T, gw = q.shape
    hd = ATTN_HEAD_DIM
    nh = gw // hd
    nblk = T // ATTN_BLOCK
    nbs = _attn_blocks_per_residue(T)
    scale = hd ** -0.5
    B = ATTN_BLOCK

    def body(q_ref, kc_ref, kp_ref, vc_ref, vp_ref, do_ref, dv_ref, lse_ref, bias_ref,
             dq_ref, dk_ref, dvo_ref, dbias_ref, ck_ref, cv_ref, s_ref, dp_ref, p_ref, ds_ref):
        g = pl.program_id(0)
        b = pl.program_id(1)
        nb = jnp.where(g == 0, nbs[0], jnp.where(g == 1, nbs[1], nbs[2]))
        no_prev = (b % nb) == 0

        @pl.when(b == 0)
        def _():
            dbias_ref[...] = jnp.zeros_like(dbias_ref)
            ck_ref[...] = jnp.zeros_like(ck_ref)
            cv_ref[...] = jnp.zeros_like(cv_ref)

        @pl.when(b < nblk)
        def _():
            col = lax.broadcasted_iota(jnp.int32, (1, 2 * B), 1)
            pen = jnp.where((col < B) & no_prev, NEG_INF, 0.0)
            heads = [slice(h * hd, (h + 1) * hd) for h in range(nh)]
            for h, hs in enumerate(heads):
                kw = jnp.concatenate([kp_ref[0, :, hs], kc_ref[0, :, hs]], axis=0)
                vw = jnp.concatenate([vp_ref[0, :, hs], vc_ref[0, :, hs]], axis=0)
                s_ref[h] = _dot(q_ref[0, :, hs], kw, NT)
                dp_ref[h] = _dot(do_ref[0, :, hs], vw, NT)
            for h, hs in enumerate(heads):
                lse_col = lse_ref[0, :, h * hd:h * hd + 1]
                d_col = dv_ref[0, :, h * hd:h * hd + 1]
                p = jnp.exp(s_ref[h] * scale + bias_ref[0, h] + pen - lse_col)
                ds = p * (dp_ref[h] + d_col)
                dbias_ref[0, h] += ds
                p_ref[h] = p.astype(BF16)
                ds_ref[h] = ds.astype(BF16)
            for h, hs in enumerate(heads):
                qh = q_ref[0, :, hs]
                kw = jnp.concatenate([kp_ref[0, :, hs], kc_ref[0, :, hs]], axis=0)
                dq_ref[0, :, hs] = (_dot(ds_ref[h], kw, NN) * scale).astype(BF16)
                dkw = _dot(ds_ref[h], qh, TN) * scale
                dvw = _dot(p_ref[h], do_ref[0, :, hs], TN)
                dk_ref[0, :, hs] = (ck_ref[:, hs] + dkw[:B]).astype(BF16)
                dvo_ref[0, :, hs] = (cv_ref[:, hs] + dvw[:B]).astype(BF16)
                ck_ref[:, hs] = dkw[B:]
                cv_ref[:, hs] = dvw[B:]

        @pl.when(b == nblk)
        def _():
            dk_ref[0] = ck_ref[...].astype(BF16)
            dvo_ref[0] = cv_ref[...].astype(BF16)

    last = nblk - 1
    cur = pl.BlockSpec((1, B, gw), lambda g, b: (g, jnp.minimum(b, last), 0))
    prev = pl.BlockSpec((1, B, gw), lambda g, b: (g, jnp.clip(b - 1, 0, last), 0))
    tab = pl.BlockSpec((1, nh, B, 2 * B), lambda g, b: (g, 0, 0, 0))
    sd = jax.ShapeDtypeStruct
    return pl.pallas_call(
        body, name=name, grid=(ng, nblk + 1),
        in_specs=[cur, cur, prev, cur, prev, cur, cur, cur, tab],
        out_specs=[cur, prev, prev, tab],
        out_shape=[sd(q.shape, BF16), sd(q.shape, BF16), sd(q.shape, BF16), sd(biasm.shape, F32)],
        scratch_shapes=[pltpu.VMEM((B, gw), F32), pltpu.VMEM((B, gw), F32), pltpu.VMEM((nh, B, 2 * B), F32),
                        pltpu.VMEM((nh, B, 2 * B), F32), pltpu.VMEM((nh, B, 2 * B), BF16), pltpu.VMEM((nh, B, 2 * B), BF16)],
        compiler_params=_params("parallel", "arbitrary"),
    )(q, k, k, v, v, do, dvec, lse, biasm)


def _attn_mix(o, lse, z, name):
    T, C = o.shape
    gw = C // len(ATTN_CONFIGS)
    tm = _pick(T, 512, SUBLANES)

    def body(o_ref, lse_ref, z_ref, out_ref):
        ls = [lse_ref[:, i * gw:(i + 1) * gw] for i in range(3)]
        mx = jnp.maximum(jnp.maximum(ls[0], ls[1]), ls[2])
        es = [jnp.exp(l - mx) for l in ls]
        den = es[0] + es[1] + es[2]
        for i in range(3):
            sz, _ = _silu_and_grad(z_ref[:, i * gw:(i + 1) * gw])
            out_ref[:, i * gw:(i + 1) * gw] = (o_ref[:, i * gw:(i + 1) * gw] * (es[i] / den) * sz).astype(BF16)

    row = pl.BlockSpec((tm, C), lambda i: (i, 0))
    return pl.pallas_call(
        body, name=name, grid=(T // tm,), in_specs=[row, row, row], out_specs=row,
        out_shape=jax.ShapeDtypeStruct((T, C), BF16), compiler_params=_params("parallel"),
    )(o, lse, z)


def _attn_mix_bwd(dout, o, lse, z, name):
    T, C = o.shape
    gw = C // len(ATTN_CONFIGS)
    tm = _pick(T, 512, SUBLANES)
    head_of = np.arange(gw) // ATTN_HEAD_DIM
    ones = jnp.asarray(head_of[:, None] == head_of[None, :], BF16)

    def body(dout_ref, o_ref, lse_ref, z_ref, ones_ref, dz_ref, do_ref, dv_ref):
        ls = [lse_ref[:, i * gw:(i + 1) * gw] for i in range(3)]
        mx = jnp.maximum(jnp.maximum(ls[0], ls[1]), ls[2])
        es = [jnp.exp(l - mx) for l in ls]
        den = es[0] + es[1] + es[2]
        alphas, ebar = [], 0.0
        for i in range(3):
            sl = slice(i * gw, (i + 1) * gw)
            alpha = es[i] / den
            ov = o_ref[:, sl]
            dv = dout_ref[:, sl]
            sz, dsz = _silu_and_grad(z_ref[:, sl])
            dz_ref[:, sl] = (dv * ov * alpha * dsz).astype(BF16)
            da = dv * sz
            do_ref[:, sl] = (da * alpha).astype(BF16)
            t = da * ov
            t1 = t.astype(BF16)
            r1 = t - t1.astype(F32)
            t2 = r1.astype(BF16)
            t3 = (r1 - t2.astype(F32)).astype(BF16)
            e = _dot(t1, ones_ref[...], NN) + _dot(t2, ones_ref[...], NN) + _dot(t3, ones_ref[...], NN)
            ebar = ebar + alpha * e
            alphas.append(alpha)
        for i in range(3):
            dv_ref[:, i * gw:(i + 1) * gw] = -alphas[i] * ebar

    row = pl.BlockSpec((tm, C), lambda i: (i, 0))
    sd = jax.ShapeDtypeStruct
    return pl.pallas_call(
        body, name=name, grid=(T // tm,),
        in_specs=[row, row, row, row, pl.BlockSpec((gw, gw), lambda i: (0, 0))], out_specs=[row, row, row],
        out_shape=[sd((T, C), BF16), sd((T, C), BF16), sd((T, C), F32)], compiler_params=_params("parallel"),
    )(dout, o, lse, z, ones)


def _mem_attn(qz, kv, name):
    T = qz.shape[0]
    dm = qz.shape[1] // 2
    M = kv.shape[0]
    hd = dm // MEM_HEADS
    scale = hd ** -0.5
    tm = _pick(T, 512, SUBLANES)

    def body(q_ref, z_ref, k_ref, v_ref, o_ref, s_ref, p_ref):
        heads = [slice(h * hd, (h + 1) * hd) for h in range(MEM_HEADS)]
        for h, sl in enumerate(heads):
            s_ref[h] = _dot(q_ref[:, sl].astype(BF16), k_ref[:, sl], NT)
        for h, sl in enumerate(heads):
            s = s_ref[h] * scale
            p = jnp.exp(s - jnp.max(s, axis=-1, keepdims=True))
            p_ref[h] = (p / jnp.sum(p, axis=-1, keepdims=True)).astype(BF16)
        for h, sl in enumerate(heads):
            sz, _ = _silu_and_grad(z_ref[:, sl])
            o_ref[:, sl] = (_dot(p_ref[h], v_ref[:, sl], NN) * sz).astype(BF16)

    return pl.pallas_call(
        body, name=name, grid=(T // tm,),
        in_specs=[pl.BlockSpec((tm, dm), lambda i: (i, 0)), pl.BlockSpec((tm, dm), lambda i: (i, 1)),
                  pl.BlockSpec((M, dm), lambda i: (0, 0)), pl.BlockSpec((M, dm), lambda i: (0, 1))],
        out_specs=pl.BlockSpec((tm, dm), lambda i: (i, 0)),
        out_shape=jax.ShapeDtypeStruct((T, dm), BF16),
        scratch_shapes=[pltpu.VMEM((MEM_HEADS, tm, M), F32), pltpu.VMEM((MEM_HEADS, tm, M), BF16)],
        compiler_params=_params("parallel"),
    )(qz, qz, kv, kv)


def _mem_attn_bwd(do, qz, kv, name):
    T = qz.shape[0]
    dm = qz.shape[1] // 2
    M = kv.shape[0]
    hd = dm // MEM_HEADS
    scale = hd ** -0.5
    tm = _pick(T, 512, SUBLANES)

    def body(do_ref, q_ref, z_ref, k_ref, v_ref, dq_ref, dz_ref, dk_ref, dv_ref, s_ref, dp_ref, p_ref, ds_ref, dob_ref):
        @pl.when(pl.program_id(0) == 0)
        def _():
            dk_ref[...] = jnp.zeros_like(dk_ref)
            dv_ref[...] = jnp.zeros_like(dv_ref)

        heads = [slice(h * hd, (h + 1) * hd) for h in range(MEM_HEADS)]
        for h, sl in enumerate(heads):
            sz, _ = _silu_and_grad(z_ref[:, sl])
            dob = (do_ref[:, sl] * sz).astype(BF16)
            dob_ref[:, sl] = dob
            s_ref[h] = _dot(q_ref[:, sl].astype(BF16), k_ref[:, sl], NT)
            dp_ref[h] = _dot(dob, v_ref[:, sl], NT)
        for h, sl in enumerate(heads):
            s = s_ref[h] * scale
            p = jnp.exp(s - jnp.max(s, axis=-1, keepdims=True))
            pn = p / jnp.sum(p, axis=-1, keepdims=True)
            dp = dp_ref[h]
            p_ref[h] = pn.astype(BF16)
            ds_ref[h] = (pn * (dp - jnp.sum(dp * pn, axis=-1, keepdims=True))).astype(BF16)
        for h, sl in enumerate(heads):
            _, dsz = _silu_and_grad(z_ref[:, sl])
            dz_ref[:, sl] = (do_ref[:, sl] * _dot(p_ref[h], v_ref[:, sl], NN) * dsz).astype(BF16)
            dq_ref[:, sl] = (_dot(ds_ref[h], k_ref[:, sl], NN) * scale).astype(BF16)
            dk_ref[:, sl] += _dot(ds_ref[h], q_ref[:, sl].astype(BF16), TN) * scale
            dv_ref[:, sl] += _dot(p_ref[h], dob_ref[:, sl], TN)

    rowq = pl.BlockSpec((tm, dm), lambda i: (i, 0))
    rowz = pl.BlockSpec((tm, dm), lambda i: (i, 1))
    kb = pl.BlockSpec((M, dm), lambda i: (0, 0))
    vb = pl.BlockSpec((M, dm), lambda i: (0, 1))
    sd = jax.ShapeDtypeStruct
    dq, dz, dk, dv = pl.pallas_call(
        body, name=name, grid=(T // tm,), in_specs=[rowq, rowq, rowz, kb, vb],
        out_specs=[rowq, rowq, kb, kb],
        out_shape=[sd((T, dm), BF16), sd((T, dm), BF16), sd((M, dm), F32), sd((M, dm), F32)],
        scratch_shapes=[pltpu.VMEM((MEM_HEADS, tm, M), F32), pltpu.VMEM((MEM_HEADS, tm, M), F32),
                        pltpu.VMEM((MEM_HEADS, tm, M), BF16), pltpu.VMEM((MEM_HEADS, tm, M), BF16), pltpu.VMEM((tm, dm), BF16)],
        compiler_params=_params("arbitrary"),
    )(do, qz, qz, kv, kv)
    return dq, dz, dk, dv


def _merge(bps, logits, b_gate, name):
    T, D = bps[0].shape
    tm = _pick(T, 512, SUBLANES)

    def body(p0_ref, p1_ref, p2_ref, l_ref, b_ref, o_ref):
        acc = 0.0
        for i, p_ref in enumerate((p0_ref, p1_ref, p2_ref)):
            sl = slice(i * D, (i + 1) * D)
            acc = acc + _sigmoid(l_ref[:, sl] + b_ref[:, sl]) * p_ref[...]
        o_ref[...] = acc.astype(BF16)

    row = pl.BlockSpec((tm, D), lambda i: (i, 0))
    return pl.pallas_call(
        body, name=name, grid=(T // tm,),
        in_specs=[row, row, row, pl.BlockSpec((tm, 3 * D), lambda i: (i, 0)), pl.BlockSpec((1, 3 * D), lambda i: (0, 0))],
        out_specs=row, out_shape=jax.ShapeDtypeStruct((T, D), BF16), compiler_params=_params("parallel"),
    )(*bps, logits, b_gate.reshape(1, 3 * D))


def _merge_bwd(dmerged, bps, logits, b_gate, name):
    T, D = bps[0].shape
    tm = _pick(T, 512, SUBLANES)

    def body(dm_ref, p0_ref, p1_ref, p2_ref, l_ref, b_ref, d0_ref, d1_ref, d2_ref, dl_ref, db_ref):
        @pl.when(pl.program_id(0) == 0)
        def _():
            db_ref[...] = jnp.zeros_like(db_ref)

        dmv = dm_ref[...]
        for i, (p_ref, d_ref) in enumerate(((p0_ref, d0_ref), (p1_ref, d1_ref), (p2_ref, d2_ref))):
            sl = slice(i * D, (i + 1) * D)
            gt = _sigmoid(l_ref[:, sl] + b_ref[:, sl])
            d_ref[...] = (dmv * gt).astype(BF16)
            dl = dmv * p_ref[...] * gt * (1.0 - gt)
            dl_ref[:, sl] = dl.astype(BF16)
            db_ref[:, sl] += jnp.sum(dl, axis=0, keepdims=True)

    row = pl.BlockSpec((tm, D), lambda i: (i, 0))
    wide = pl.BlockSpec((tm, 3 * D), lambda i: (i, 0))
    vec = pl.BlockSpec((1, 3 * D), lambda i: (0, 0))
    sd = jax.ShapeDtypeStruct
    return pl.pallas_call(
        body, name=name, grid=(T // tm,), in_specs=[row, row, row, row, wide, vec],
        out_specs=[row, row, row, wide, vec],
        out_shape=[sd((T, D), BF16)] * 3 + [sd((T, 3 * D), BF16), sd((1, 3 * D), F32)],
        compiler_params=_params("arbitrary"),
    )(dmerged, *bps, logits, b_gate.reshape(1, 3 * D))


def _to_segments(a):
    T, C = a.shape
    return a.reshape(SSM_SEGMENTS, T // SSM_SEGMENTS, C).transpose(1, 0, 2).reshape(T, C)


def _from_segments(a):
    T, C = a.shape
    return a.reshape(T // SSM_SEGMENTS, SSM_SEGMENTS, C).transpose(1, 0, 2).reshape(T, C)


def _to_residues(a):
    T = a.shape[0]
    gw = HEADS_PER_GROUP * ATTN_HEAD_DIM
    out = []
    for g, (_, r) in enumerate(ATTN_CONFIGS):
        ag = a[:, g * gw:(g + 1) * gw].reshape(T // r, r, gw)
        out.append(ag.transpose(1, 0, 2).reshape(T, gw))
    return jnp.stack(out)


def _from_residues(a):
    _, T, gw = a.shape
    out = []
    for g, (_, r) in enumerate(ATTN_CONFIGS):
        out.append(a[g].reshape(r, T // r, gw).transpose(1, 0, 2).reshape(T, gw))
    return jnp.concatenate(out, axis=1)


def _block_diag(w):
    nblk, ng, a, b = w.shape
    eye = jnp.eye(ng, dtype=w.dtype)
    return (w[:, :, :, None, :] * eye[None, :, None, :, None]).reshape(nblk, ng * a, ng * b)


def _block_diag_part(m, a, b):
    nblk = m.shape[0]
    ng = m.shape[1] // a
    m5 = m.reshape(nblk, ng, a, ng, b)
    eye = jnp.eye(ng, dtype=m.dtype)
    return jnp.sum(m5 * eye[None, :, None, :, None], axis=3)


def _ssm_matrices(p, L, tag):
    G, P = p["ssm_lambda_re"].shape[1:]
    Hg = SSM_GROUP
    gpb = SSM_BLOCK_CH // Hg
    nblk = G // gpb
    br = p["ssm_b_re"][L].transpose(2, 0, 1)
    bi = p["ssm_b_im"][L].transpose(2, 0, 1)
    disc_in = (p["ssm_lambda_re"][L], p["ssm_lambda_im"][L], p["ssm_log_dt"][L].reshape(G, 1), br, bi)
    ar, ai, bbr, bbi = _ssm_disc(*disc_in, name=f"ssm_disc_{tag}")
    amat = jnp.concatenate([ar.reshape(nblk, gpb * P), ai.reshape(nblk, gpb * P)], axis=1)
    amat = jnp.broadcast_to(amat[:, None, :], (nblk, SUBLANES, 2 * gpb * P))
    bbr_g = bbr.transpose(1, 0, 2).reshape(nblk, gpb, Hg, P)
    bbi_g = bbi.transpose(1, 0, 2).reshape(nblk, gpb, Hg, P)
    bmat = jnp.concatenate([_block_diag(bbr_g), _block_diag(bbi_g)], axis=2).astype(BF16)
    cre = p["ssm_c_re"][L].reshape(nblk, gpb, Hg, P).transpose(0, 1, 3, 2)
    cim = p["ssm_c_im"][L].reshape(nblk, gpb, Hg, P).transpose(0, 1, 3, 2)
    cmat = jnp.concatenate([_block_diag(cre), -_block_diag(cim)], axis=1).astype(BF16)
    return disc_in, amat, bmat, cmat


def _layer_fwd(x, mem, p, wb, L, biasm):
    T, D = x.shape
    C = p["ssm_d"].shape[1]
    dm = wb["w_br_mem"].shape[1]
    tag = f"l{L}"
    s = {"x": x}
    h = _rmsnorm(x, p["norm_g"][L], f"norm_{tag}")
    offs = [int(o) for o in np.cumsum([0, 2 * C, 3 * 768, 768, 2 * dm, 3 * D])]
    names = ("uz", "qkv", "z_attn", "qz_mem", "logits")
    dts = (F32, BF16, F32, F32, F32)
    for i, (nm, dt) in enumerate(zip(names, dts)):
        s[nm] = _matmul(h, wb["w_in"], mode="nn", name=f"in_{nm}_{tag}", out_dtype=dt, b_lead=L, b_off=offs[i],
                        n_cols=offs[i + 1] - offs[i])
    s["h"] = h

    disc_in, amat, bmat, cmat = _ssm_matrices(p, L, tag)
    u_seg = _to_segments(s["uz"][:, :C])
    dvec = p["ssm_d"][L].reshape(1, C)
    car = _ssm_carries(u_seg, None, bmat, cmat, amat, reverse=False, name=f"ssm_carry_{tag}")
    y_seg, xs = _ssm_scan(u_seg, bmat, cmat, amat, car, dvec, f"ssm_scan_{tag}")
    y = _from_segments(y_seg)
    o_ssm, a_glu = _ssm_post(y, s["uz"], wb["w_glu"][L], p["b_glu"][L], f"ssm_post_{tag}")
    s.update(disc_in=disc_in, amat=amat, bmat=bmat, cmat=cmat, u_seg=u_seg, xs=xs, y=y, a_glu=a_glu, o_ssm=o_ssm)

    q, k, v = (_to_residues(s["qkv"][:, i * 768:(i + 1) * 768]) for i in range(3))
    o_r, lse_r = _attn_fwd(q, k, v, biasm, f"attn_{tag}")
    o_nat, lse_nat = _from_residues(o_r), _from_residues(lse_r)
    o_attn = _attn_mix(o_nat, lse_nat, s["z_attn"], f"attn_mix_{tag}")
    s.update(q=q, k=k, v=v, lse_r=lse_r, o_nat=o_nat, lse_nat=lse_nat, o_attn=o_attn)

    mn = _rmsnorm(mem, p["mem_norm_g"][L], f"mem_norm_{tag}")
    kv = _matmul(mn, wb["w_mem_kv"], mode="nn", name=f"mem_kv_{tag}", out_dtype=BF16, b_lead=L)
    o_mem = _mem_attn(s["qz_mem"], kv, f"mem_attn_{tag}")
    s.update(mn=mn, kv=kv, o_mem=o_mem)

    bps = [_matmul(o, wb[n], mode="nn", name=f"br_{n}_{tag}", b_lead=L)
           for o, n in ((o_ssm, "w_br_ssm"), (o_attn, "w_br_attn"), (o_mem, "w_br_mem"))]
    merged = _merge(bps, s["logits"], p["b_gate"][L], f"merge_{tag}")
    s.update(bps=bps, merged=merged)
    x_new = _matmul(merged, wb["w_out"], mode="nn", name=f"out_{tag}", add=x, b_lead=L)
    return x_new, s


def _layer_bwd(dx, mem, p, wb, L, s, biasm, gprev):
    T, D = dx.shape
    C = p["ssm_d"].shape[1]
    depth = p["norm_g"].shape[0]
    tag = f"l{L}"
    g = {}

    def wgrad(n, a, b):
        g[n] = _matmul(a, b, mode="tn", name=f"d{n}_{tag}", out_dtype=BF16, stack=(L, depth, gprev.get(n)))

    dmerged = _matmul(dx, wb["w_out"], mode="nt", name=f"d_merged_{tag}", b_lead=L)
    wgrad("w_out", s["merged"], dx)
    dbp0, dbp1, dbp2, dlogits, g["b_gate"] = _merge_bwd(dmerged, s["bps"], s["logits"], p["b_gate"][L], f"merge_bwd_{tag}")
    dos = []
    for dbp, o, n in ((dbp0, s["o_ssm"], "w_br_ssm"), (dbp1, s["o_attn"], "w_br_attn"), (dbp2, s["o_mem"], "w_br_mem")):
        dos.append(_matmul(dbp, wb[n], mode="nt", name=f"d_o_{n}_{tag}", b_lead=L))
        wgrad(n, o, dbp)

    dy, dz_ssm, ds_glu, g["b_glu"] = _ssm_post_bwd(dos[0], s["y"], s["uz"], wb["w_glu"][L], p["b_glu"][L], f"ssm_post_bwd_{tag}")
    wgrad("w_glu", s["a_glu"], ds_glu)
    dy_seg = _to_segments(dy)
    dvec = p["ssm_d"][L].reshape(1, C)
    rcar = _ssm_carries(None, dy_seg, s["bmat"], s["cmat"], s["amat"], reverse=True, name=f"ssm_rcarry_{tag}")
    du_seg, dbm, dct, dam, g["ssm_d"] = _ssm_scan_bwd(dy_seg, s["u_seg"], s["xs"], s["bmat"], s["cmat"], s["amat"], rcar, dvec,
                                                      f"ssm_scan_bwd_{tag}")
    du = _from_segments(du_seg)
    G, P = p["ssm_lambda_re"].shape[1:]
    Hg = SSM_GROUP
    half = dbm.shape[2] // 2
    dbbr = _block_diag_part(dbm[:, :, :half], Hg, P).reshape(G, Hg, P).transpose(1, 0, 2)
    dbbi = _block_diag_part(dbm[:, :, half:], Hg, P).reshape(G, Hg, P).transpose(1, 0, 2)
    g["ssm_c_re"] = _block_diag_part(dct[:, :, :half], Hg, P).reshape(G, Hg, P)
    g["ssm_c_im"] = -_block_diag_part(dct[:, :, half:], Hg, P).reshape(G, Hg, P)
    dar = dam[:, 0, :half].reshape(G, P)
    dai = dam[:, 0, half:].reshape(G, P)
    glre, glim, gdt, gbr, gbi = _ssm_disc_bwd(*s["disc_in"], dar, dai, dbbr, dbbi, name=f"ssm_disc_bwd_{tag}")
    g["ssm_lambda_re"], g["ssm_lambda_im"], g["ssm_log_dt"] = glre, glim, gdt.reshape(G)
    g["ssm_b_re"] = gbr.transpose(1, 2, 0)
    g["ssm_b_im"] = gbi.transpose(1, 2, 0)

    dz_attn, do_nat, dvec_nat = _attn_mix_bwd(dos[1], s["o_nat"], s["lse_nat"], s["z_attn"], f"attn_mix_bwd_{tag}")
    dq_r, dk_r, dv_r, dbias = _attn_bwd(s["q"], s["k"], s["v"], _to_residues(do_nat), _to_residues(dvec_nat), s["lse_r"], biasm,
                                        f"attn_bwd_{tag}")
    dqkv = [_from_residues(a) for a in (dq_r, dk_r, dv_r)]

    dq_mem, dz_mem, dk_mem, dv_mem = _mem_attn_bwd(dos[2], s["qz_mem"], s["kv"], f"mem_attn_bwd_{tag}")
    dkv = jnp.concatenate([dk_mem, dv_mem], axis=1)
    wgrad("w_mem_kv", s["mn"], dkv)
    dmn = _matmul(dkv, wb["w_mem_kv"], mode="nt", name=f"d_mn_{tag}", b_lead=L)
    _, g["mem_norm_g"] = _rmsnorm_bwd(mem, p["mem_norm_g"][L], dmn, None, f"mem_norm_bwd_{tag}")

    dproj = jnp.concatenate([du.astype(BF16), dz_ssm] + dqkv + [dz_attn, dq_mem, dz_mem, dlogits], axis=1)
    dh = _matmul(dproj, wb["w_in"], mode="nt", name=f"d_h_{tag}", b_lead=L)
    wgrad("w_in", s["h"], dproj)
    dx_in, g["norm_g"] = _rmsnorm_bwd(s["x"], p["norm_g"][L], dh, dx, f"norm_bwd_{tag}")
    return dx_in, g, dbias


def _bucket_onehot(gi):
    buckets, bands = _band_tables()
    hit = (buckets[gi].reshape(1, -1) == jnp.arange(NUM_BUCKETS)[:, None]) & bands[gi].reshape(1, -1)
    return hit.astype(BF16)


def _bias_tables(rel_bias, name):
    _, bands = _band_tables()
    out = []
    for gi in range(len(ATTN_CONFIGS)):
        tab = rel_bias[:, gi * HEADS_PER_GROUP:(gi + 1) * HEADS_PER_GROUP].T
        flat = _matmul(tab, _bucket_onehot(gi), mode="nn", name=f"{name}_{gi}", split_a=3, tn=4096)
        out.append(jnp.where(bands[gi][None], flat.reshape(HEADS_PER_GROUP, ATTN_BLOCK, 2 * ATTN_BLOCK), NEG_INF))
    return jnp.stack(out)


def _rel_bias_grad(dbias_sum, name):
    cols = []
    for gi in range(len(ATTN_CONFIGS)):
        flat = dbias_sum[gi].reshape(HEADS_PER_GROUP, -1)
        cols.append(_matmul(flat, _bucket_onehot(gi), mode="nt", name=f"{name}_{gi}", split_a=2, tk=4096).T)
    return jnp.concatenate(cols, axis=1)


def _local_step(x, mem, target, p, wb):
    depth = p["norm_g"].shape[0]
    biasm = _bias_tables(p["rel_bias"], "bias_table")
    saved = []
    for L in range(depth):
        x, s = _layer_fwd(x, mem, p, wb, L, biasm)
        saved.append(s)
    loss_vec, dx, dgf = _loss_head(x, p["final_norm_g"], target, "loss_head")
    grads = {"final_norm_g": dgf.reshape(-1)}
    per_layer = [None] * depth
    dbias_sum = 0.0
    stacked = {}
    for L in reversed(range(depth)):
        dx, per_layer[L], dbias = _layer_bwd(dx, mem, p, wb, L, saved[L], biasm, stacked)
        stacked = {n: per_layer[L][n] for n, _ in BIG}
        dbias_sum = dbias_sum + dbias
    grads.update(stacked)
    for n in per_layer[0]:
        if n not in stacked:
            grads[n] = jnp.stack([per_layer[L][n].reshape(p[n].shape[1:]) for L in range(depth)])
    grads["rel_bias"] = _rel_bias_grad(dbias_sum, "d_rel_bias")
    return jnp.sum(loss_vec), dx, grads


def _chip_coords(j):
    return j // 2, j % 2


def _place_shard(shard, ax, chip, name):
    _, a, b = shard.shape
    ra = _pick(a, 256, 16)
    full = (2, a * N_CHIPS, b) if ax == 1 else (2, a, b * N_CHIPS)
    per = a // ra

    def body(j_ref, s_ref, o_ref):
        o_ref[...] = s_ref[...].astype(BF16)

    out_idx = (lambda l, i, j: (l, j[0] * per + i, 0)) if ax == 1 else (lambda l, i, j: (l, i, j[0]))
    return pl.pallas_call(
        body, name=name,
        grid_spec=pltpu.PrefetchScalarGridSpec(
            num_scalar_prefetch=1, grid=(2, per),
            in_specs=[pl.BlockSpec((None, ra, b), lambda l, i, j: (l, i, 0))],
            out_specs=pl.BlockSpec((None, ra, b), out_idx)),
        out_shape=jax.ShapeDtypeStruct(full, BF16), compiler_params=_params("parallel", "parallel"),
    )(chip, shard)


def _gather_shards(fulls, axes, name):
    n = len(fulls)
    widths = [a.shape[ax] // N_CHIPS for a, ax in zip(fulls, axes)]
    aligns = [LANES if ax == 2 else 16 for ax in axes]

    def body(*refs):
        outs = refs[n:2 * n]
        send_sems, recv_sems, fsend_sems, frecv_sems = refs[2 * n:]
        x, y, c = lax.axis_index("x"), lax.axis_index("y"), lax.axis_index("c")
        mine = 2 * x + y
        sibling = (x, y, 1 - c)

        def window(t, layer, j):
            start = pl.ds(pl.multiple_of(j * widths[t], aligns[t]), widths[t])
            return outs[t].at[(layer, start, slice(None)) if axes[t] == 1 else (layer, slice(None), start)]

        def over_ici(t, j, block):
            return pltpu.make_async_remote_copy(
                src_ref=window(t, c, mine), dst_ref=window(t, c, block), send_sem=send_sems.at[t, j],
                recv_sem=recv_sems.at[t, block], device_id=(*_chip_coords(j), c), device_id_type=MESH)

        def over_d2d(t, j, layer):
            return pltpu.make_async_remote_copy(
                src_ref=window(t, layer, j), dst_ref=window(t, layer, j), send_sem=fsend_sems.at[t, j],
                recv_sem=frecv_sems.at[t, j], device_id=sibling, device_id_type=MESH)

        for t in range(n):
            for j in range(N_CHIPS):
                @pl.when(j != mine)
                def _():
                    over_ici(t, j, mine).start()
        for t in range(n):
            for j in range(N_CHIPS):
                @pl.when(j != mine)
                def _():
                    over_ici(t, j, j).wait_recv()
                    over_d2d(t, j, c).start()
        for t in range(n):
            for j in range(N_CHIPS):
                @pl.when(j != mine)
                def _():
                    over_ici(t, j, mine).wait_send()
                    over_d2d(t, j, c).wait_send()
                    over_d2d(t, j, 1 - c).wait_recv()

    sem = pltpu.SemaphoreType.DMA
    return pl.pallas_call(
        body, name=name, in_specs=[HBM] * n, out_specs=[HBM] * n,
        out_shape=[jax.ShapeDtypeStruct(a.shape, a.dtype) for a in fulls],
        input_output_aliases={t: t for t in range(n)},
        scratch_shapes=[sem((n, N_CHIPS)), sem((n, N_CHIPS)), sem((n, N_CHIPS)), sem((n, N_CHIPS))],
    )(*fulls)


def _scatter_slices(arrays, axes, name):
    n = len(arrays)

    def piece(a, ax):
        if ax is None:
            return a.shape, None
        w = a.shape[ax] // N_CHIPS
        return a.shape[:ax] + (w,) + a.shape[ax + 1:], w

    shapes = [piece(a, ax) for a, ax in zip(arrays, axes)]

    def body(*refs):
        ins, outs = refs[:n], refs[n:2 * n]
        send_sems, recv_sems, loc_sems = refs[2 * n:]
        x, y, c = lax.axis_index("x"), lax.axis_index("y"), lax.axis_index("c")
        mine = 2 * x + y

        def src(t, j):
            ax, w = axes[t], shapes[t][1]
            if ax is None:
                return ins[t]
            idx = tuple(pl.ds(j * w, w) if d == ax else slice(None) for d in range(len(arrays[t].shape)))
            return ins[t].at[idx]

        for t in range(n):
            for j in range(N_CHIPS):
                @pl.when(j == mine)
                def _():
                    pltpu.make_async_copy(src(t, j), outs[t].at[j], loc_sems.at[t]).start()

                @pl.when(j != mine)
                def _():
                    pltpu.make_async_remote_copy(
                        src_ref=src(t, j), dst_ref=outs[t].at[mine], send_sem=send_sems.at[t, j], recv_sem=recv_sems.at[t, mine],
                        device_id=(*_chip_coords(j), c), device_id_type=MESH).start()
        for t in range(n):
            for j in range(N_CHIPS):
                @pl.when(j == mine)
                def _():
                    pltpu.make_async_copy(src(t, j), outs[t].at[j], loc_sems.at[t]).wait()

                @pl.when(j != mine)
                def _():
                    cp = pltpu.make_async_remote_copy(
                        src_ref=src(t, j), dst_ref=outs[t].at[j], send_sem=send_sems.at[t, j], recv_sem=recv_sems.at[t, j],
                        device_id=(*_chip_coords(j), c), device_id_type=MESH)
                    cp.wait_send()
                    cp.wait_recv()

    return pl.pallas_call(
        body, name=name, in_specs=[HBM] * n, out_specs=[HBM] * n,
        out_shape=[jax.ShapeDtypeStruct((N_CHIPS,) + sh, a.dtype) for a, (sh, _) in zip(arrays, shapes)],
        scratch_shapes=[pltpu.SemaphoreType.DMA((n, N_CHIPS)), pltpu.SemaphoreType.DMA((n, N_CHIPS)), pltpu.SemaphoreType.DMA((n,))],
    )(*arrays)


def _swap_layers(stacked, name):
    n = len(stacked)

    def body(*refs):
        ins, outs = refs[:n], refs[n:2 * n]
        send_sems, recv_sems = refs[2 * n:]
        c = lax.axis_index("c")
        peer = (lax.axis_index("x"), lax.axis_index("y"), 1 - c)
        cps = [pltpu.make_async_remote_copy(src_ref=ins[t].at[1 - c], dst_ref=outs[t], send_sem=send_sems.at[t],
                                            recv_sem=recv_sems.at[t], device_id=peer, device_id_type=MESH) for t in range(n)]
        for cp in cps:
            cp.start()
        for cp in cps:
            cp.wait_send()
            cp.wait_recv()

    return pl.pallas_call(
        body, name=name, in_specs=[HBM] * n, out_specs=[HBM] * n,
        out_shape=[jax.ShapeDtypeStruct(a.shape[1:], a.dtype) for a in stacked],
        scratch_shapes=[pltpu.SemaphoreType.DMA((n,)), pltpu.SemaphoreType.DMA((n,))],
    )(*stacked)


def _merge_layers(stacked, name):
    n = len(stacked)

    def body(*refs):
        outs = refs[n:2 * n]
        send_sems, recv_sems = refs[2 * n:]
        c = lax.axis_index("c")
        peer = (lax.axis_index("x"), lax.axis_index("y"), 1 - c)
        for t in range(n):
            pltpu.make_async_remote_copy(src_ref=outs[t].at[c], dst_ref=outs[t].at[c], send_sem=send_sems.at[t],
                                         recv_sem=recv_sems.at[t], device_id=peer, device_id_type=MESH).start()
        for t in range(n):
            cp = pltpu.make_async_remote_copy(src_ref=outs[t].at[c], dst_ref=outs[t].at[1 - c], send_sem=send_sems.at[t],
                                              recv_sem=recv_sems.at[t], device_id=peer, device_id_type=MESH)
            cp.wait_send()
            cp.wait_recv()

    sem = pltpu.SemaphoreType.DMA
    return pl.pallas_call(
        body, name=name, in_specs=[HBM] * n, out_specs=[HBM] * n,
        out_shape=[jax.ShapeDtypeStruct(a.shape, a.dtype) for a in stacked],
        input_output_aliases={t: t for t in range(n)}, scratch_shapes=[sem((n,)), sem((n,))],
    )(*stacked)


def _pair_sum(stacked, landed, core, name):
    _, K, N = stacked.shape
    tr = _pick(K, max(16, (1 << 19) // N // 16 * 16), 16)

    def body(c_ref, s_ref, l_ref, o_ref):
        o_ref[...] = (s_ref[...].astype(F32) + l_ref[...].astype(F32)).astype(o_ref.dtype)

    return pl.pallas_call(
        body, name=name,
        grid_spec=pltpu.PrefetchScalarGridSpec(
            num_scalar_prefetch=1, grid=(K // tr,),
            in_specs=[pl.BlockSpec((None, tr, N), lambda i, c: (c[0], i, 0)), pl.BlockSpec((tr, N), lambda i, c: (i, 0))],
            out_specs=pl.BlockSpec((tr, N), lambda i, c: (i, 0))),
        out_shape=jax.ShapeDtypeStruct((K, N), stacked.dtype), compiler_params=_params("parallel"),
    )(core, stacked, landed)


def _sum_chips(landed, core, name):
    _, R, C = landed.shape
    tr = _pick(R, max(SUBLANES, (1 << 19) // C // 16 * 16), 16)

    def body(c_ref, l_ref, o_ref):
        acc = l_ref[0].astype(F32) + l_ref[1].astype(F32)
        acc = acc + l_ref[2].astype(F32)
        o_ref[...] = acc + l_ref[3].astype(F32)

    return pl.pallas_call(
        body, name=name,
        grid_spec=pltpu.PrefetchScalarGridSpec(
            num_scalar_prefetch=1, grid=(R // tr,),
            in_specs=[pl.BlockSpec((N_CHIPS, tr, C), lambda i, c: (0, i, 0))],
            out_specs=pl.BlockSpec((None, tr, C), lambda i, c: (c[0], i, 0))),
        out_shape=jax.ShapeDtypeStruct((2, R, C), F32), compiler_params=_params("parallel"),
    )(core, landed)


def _adamw(w, g, m, v, name):
    R, C = w.shape
    tr = _pick(R, max(SUBLANES, (1 << 18) // C // 8 * 8), SUBLANES)
    c1 = 1.0 / (1.0 - ADAM_B1 ** ADAM_STEP)
    c2 = 1.0 / (1.0 - ADAM_B2 ** ADAM_STEP)

    def body(w_ref, g_ref, m_ref, v_ref, d_ref, nm_ref, nv_ref):
        g = g_ref[...]
        nm = ADAM_B1 * m_ref[...] + (1.0 - ADAM_B1) * g
        nv = ADAM_B2 * v_ref[...] + (1.0 - ADAM_B2) * (g * g)
        nm_ref[...] = nm
        nv_ref[...] = nv
        d_ref[...] = -ADAM_LR * ((nm * c1) / (jnp.sqrt(nv * c2) + ADAM_EPS) + ADAM_WD * w_ref[...])

    blk = pl.BlockSpec((tr, C), lambda i: (i, 0))
    return pl.pallas_call(
        body, name=name, grid=(R // tr,), in_specs=[blk] * 4, out_specs=[blk] * 3,
        out_shape=[jax.ShapeDtypeStruct((R, C), F32)] * 3, compiler_params=_params("parallel"),
    )(w, g, m, v)


def _pack_small(d, prefix=""):
    flat = jnp.concatenate([d[prefix + n].astype(F32).reshape(-1) for n in SMALL])
    pad = (-flat.shape[0]) % (2 * 16 * LANES)
    return jnp.pad(flat, (0, pad)).reshape(-1, LANES)


def _unpack_small(packed, shapes):
    flat = packed.reshape(-1)
    out, off = {}, 0
    for n in SMALL:
        size = int(np.prod(shapes[n]))
        out[n] = flat[off:off + size].reshape(shapes[n])
        off += size
    return out


def kernel(*args):
    p = dict(zip(INPUTS, args))
    x, mem, target = p["x"][0], p["mem"][0], p["loss_target"][0]

    names = [n for n, _ in BIG] + ["small"]
    core = lax.axis_index("c").astype(jnp.int32).reshape(1)
    chip = (2 * lax.axis_index("x") + lax.axis_index("y")).astype(jnp.int32).reshape(1)
    placed = [_place_shard(p[n], ax, chip, f"place_{n}") for n, ax in BIG]
    wb = dict(zip(names, _gather_shards(placed, [ax for _, ax in BIG], "gather_weights")))

    loss_part, dx, grads = _local_step(x, mem, target, p, wb)
    loss = lax.psum(loss_part, ("x", "y", "c"))

    stacked = [grads[n] for n, _ in BIG] + [_pack_small(grads).reshape(2, -1, LANES)]
    theirs = _swap_layers(stacked, "swap_layers")
    pair = [_pair_sum(s, o, core, f"pair_sum_{n}") for n, s, o in zip(names, stacked, theirs)]
    landed = _scatter_slices(pair, [ax - 1 for _, ax in BIG] + [None], "scatter_grads")
    reduced = [_sum_chips(ld.reshape(N_CHIPS, -1, ld.shape[-1]), core, f"sum_chips_{n}") for n, ld in zip(names, landed)]
    total = _merge_layers(reduced, "merge_layers")

    out = {}
    for (n, _), g in zip(BIG, total):
        sh = p[n].shape
        two_d = lambda a: a.reshape(-1, sh[-1])
        res = (g,) + tuple(_adamw(two_d(p[n]), two_d(g), two_d(p["m_" + n]), two_d(p["v_" + n]), f"adamw_{n}"))
        for key, r in zip(("grad_", "delta_", "new_m_", "new_v_"), res):
            out[key + n] = r.reshape(sh)
    g = total[-1].reshape(-1, LANES)
    res = (g,) + tuple(_adamw(_pack_small(p), g, _pack_small(p, "m_"), _pack_small(p, "v_"), "adamw_small"))
    shapes = {n: p[n].shape for n in SMALL}
    for key, r in zip(("grad_", "delta_", "new_m_", "new_v_"), res):
        for n, a in _unpack_small(r, shapes).items():
            out[key + n] = a

    result = [loss, dx.reshape(p["x"].shape)]
    for key in ("grad_", "delta_", "new_m_", "new_v_"):
        result += [out[key + n] for n in WEIGHTS]
    return tuple(result)
```

```python
import math

import jax
import jax.numpy as jnp
import numpy as np
from jax import lax
from jax.experimental import pallas as pl
from jax.experimental.pallas import tpu as pltpu

F32 = jnp.float32
BF16 = jnp.bfloat16
MESH = pl.DeviceIdType.MESH
HBM = pl.BlockSpec(memory_space=pltpu.HBM)

EPS = 1e-6
SSM_GROUP = 16
SSM_STATE = 64
ATTN_HEAD_DIM = 64
HEADS_PER_GROUP = 4
ATTN_CONFIGS = ((128, 1), (512, 4), (2048, 16))
ATTN_BLOCK = 128
NUM_BUCKETS = 32
REL_MAX_DISTANCE = 2048
NEG_INF = -1e30
MEM_HEADS = 4
ADAM_LR = 0.001
ADAM_B1 = 0.9
ADAM_B2 = 0.999
ADAM_EPS = 1e-08
ADAM_WD = 0.01
ADAM_STEP = 10

LANES = 128
SUBLANES = 8
VMEM_LIMIT_BYTES = 48 * 1024 * 1024
SSM_BLOCK_CH = 128
SSM_SEGMENTS = SUBLANES
SSM_CHUNK_STEPS = 128

N_CHIPS = 4
BIG = (("w_in", 2), ("w_glu", 1), ("w_mem_kv", 1), ("w_br_ssm", 2), ("w_br_attn", 2), ("w_br_mem", 2), ("w_out", 1))
SMALL = ("norm_g", "mem_norm_g", "b_gate", "ssm_lambda_re", "ssm_lambda_im", "ssm_log_dt", "ssm_b_re", "ssm_b_im",
         "ssm_c_re", "ssm_c_im", "ssm_d", "b_glu", "rel_bias", "final_norm_g")
WEIGHTS = ("norm_g", "mem_norm_g", "w_in", "b_gate", "ssm_lambda_re", "ssm_lambda_im", "ssm_log_dt", "ssm_b_re",
           "ssm_b_im", "ssm_c_re", "ssm_c_im", "ssm_d", "w_glu", "b_glu", "w_mem_kv", "w_br_ssm", "w_br_attn",
           "w_br_mem", "w_out", "rel_bias", "final_norm_g")
INPUTS = ("x", "mem") + WEIGHTS + ("loss_target",) + tuple("m_" + n for n in WEIGHTS) + tuple("v_" + n for n in WEIGHTS)


def _params(*sem):
    return pltpu.CompilerParams(dimension_semantics=sem, vmem_limit_bytes=VMEM_LIMIT_BYTES)


def _pick(dim, pref, align):
    if dim <= pref:
        return dim
    t = pref - pref % align
    while t >= align:
        if dim % t == 0:
            return t
        t -= align
    return dim


def _sigmoid(v):
    return 1.0 / (1.0 + jnp.exp(-v))


def _silu_and_grad(z):
    s = _sigmoid(z)
    return z * s, s * (1.0 + z * (1.0 - s))


_GELU_C = math.sqrt(2.0 / math.pi)


def _gelu_and_grad(y):
    inner = _GELU_C * (y + 0.044715 * y * y * y)
    t = jnp.tanh(inner)
    g = 0.5 * y * (1.0 + t)
    dg = 0.5 * (1.0 + t) + 0.5 * y * (1.0 - t * t) * _GELU_C * (1.0 + 3.0 * 0.044715 * y * y)
    return g, dg


def _dot(a, b, dims):
    return lax.dot_general(a, b, (dims, ((), ())), preferred_element_type=F32)


NN = ((1,), (0,))
NT = ((1,), (1,))
TN = ((0,), (0,))


def _matmul(a, b, *, mode, name, out_dtype=F32, add=None, split_a=1, tm=1024, tn=768, tk=2304,
            b_lead=None, b_off=0, n_cols=None, stack=None):
    if mode == "tn":
        K, M = a.shape
    else:
        M, K = a.shape
    bshape = b.shape if b_lead is None else b.shape[1:]
    N = n_cols or (bshape[0] if mode == "nt" else bshape[1])
    if mode != "tn" and M >= 4 * tm:
        tm = 2 * tm
    tm = _pick(M, tm, LANES if mode == "tn" else SUBLANES)
    tn = _pick(math.gcd(N, b_off) if b_off else N, tn, LANES)
    tk = _pick(K, tk, LANES)
    nk = K // tk
    joff = b_off // tn
    dims = {"nn": NN, "nt": NT, "tn": TN}[mode]
    has_add = add is not None
    has_prev = stack is not None and stack[2] is not None

    def body(*refs):
        a_ref, b_ref = refs[:2]
        add_ref = refs[2] if has_add else None
        o_ref = refs[-2] if nk > 1 else refs[-1]
        k = pl.program_id(2)
        bv = b_ref[...].astype(BF16)
        if split_a > 1:
            rest = a_ref[...].astype(F32)
            part = 0.0
            for _ in range(split_a):
                piece = rest.astype(BF16)
                part = part + _dot(piece, bv, dims)
                rest = rest - piece.astype(F32)
        else:
            part = _dot(a_ref[...].astype(BF16), bv, dims)

        def finish(r):
            if has_add:
                r = r + add_ref[...]
            o_ref[...] = r.astype(out_dtype)

        if nk == 1:
            finish(part)
            return
        acc_ref = refs[-1]

        @pl.when(k == 0)
        def _():
            acc_ref[...] = part

        @pl.when((k > 0) & (k < nk - 1))
        def _():
            acc_ref[...] += part

        @pl.when(k == nk - 1)
        def _():
            finish(acc_ref[...] + part)

    a_spec = pl.BlockSpec((tk, tm), lambda i, j, k: (k, i)) if mode == "tn" else pl.BlockSpec((tm, tk), lambda i, j, k: (i, k))
    lead = () if b_lead is None else (b_lead,)
    lead_blk = () if b_lead is None else (None,)
    if mode == "nt":
        b_spec = pl.BlockSpec(lead_blk + (tn, tk), lambda i, j, k: lead + (j + joff, k))
    else:
        b_spec = pl.BlockSpec(lead_blk + (tk, tn), lambda i, j, k: lead + (k, j + joff))
    in_specs = [a_spec, b_spec]
    args = [a, b]
    if has_add:
        in_specs.append(pl.BlockSpec((tm, tn), lambda i, j, k: (i, j)))
        args.append(add)
    aliases = {}
    if stack is None:
        out_spec = pl.BlockSpec((tm, tn), lambda i, j, k: (i, j))
        out_shape = jax.ShapeDtypeStruct((M, N), out_dtype)
    else:
        layer, depth, prev = stack
        out_spec = pl.BlockSpec((None, tm, tn), lambda i, j, k: (layer, i, j))
        out_shape = jax.ShapeDtypeStruct((depth, M, N), out_dtype)
        if has_prev:
            in_specs.append(pl.BlockSpec(memory_space=pl.ANY))
            args.append(prev)
            aliases = {len(args) - 1: 0}
    return pl.pallas_call(
        body, name=name, grid=(M // tm, N // tn, nk), in_specs=in_specs, out_specs=out_spec, out_shape=out_shape,
        scratch_shapes=[pltpu.VMEM((tm, tn), F32)] if nk > 1 else [], input_output_aliases=aliases,
        compiler_params=_params("parallel", "parallel", "arbitrary"),
    )(*args)


def _rmsnorm(x, g, name):
    T, D = x.shape
    tm = _pick(T, 512, SUBLANES)

    def body(x_ref, g_ref, h_ref):
        xv = x_ref[...]
        r = lax.rsqrt(jnp.mean(xv * xv, axis=-1, keepdims=True) + EPS)
        h_ref[...] = (xv * r * g_ref[...]).astype(BF16)

    return pl.pallas_call(
        body, name=name, grid=(T // tm,),
        in_specs=[pl.BlockSpec((tm, D), lambda i: (i, 0)), pl.BlockSpec((1, D), lambda i: (0, 0))],
        out_specs=pl.BlockSpec((tm, D), lambda i: (i, 0)),
        out_shape=jax.ShapeDtypeStruct((T, D), BF16), compiler_params=_params("parallel"),
    )(x, g.reshape(1, D))


def _rmsnorm_bwd(x, g, dh, dres, name):
    T, D = x.shape
    tm = _pick(T, 512, SUBLANES)
    with_res = dres is not None

    def body(*refs):
        if with_res:
            x_ref, g_ref, dh_ref, dres_ref, dx_ref, dg_ref = refs
        else:
            x_ref, g_ref, dh_ref, dx_ref, dg_ref = refs
        xv = x_ref[...]
        dhv = dh_ref[...]
        r = lax.rsqrt(jnp.mean(xv * xv, axis=-1, keepdims=True) + EPS)
        dyg = dhv * g_ref[...]
        c = jnp.mean(dyg * xv, axis=-1, keepdims=True)
        dx = r * dyg - xv * (r * r * r) * c
        if with_res:
            dx = dx + dres_ref[...]
        dx_ref[...] = dx

        @pl.when(pl.program_id(0) == 0)
        def _():
            dg_ref[...] = jnp.zeros_like(dg_ref)

        dg_ref[...] += jnp.sum(dhv * xv * r, axis=0, keepdims=True)

    row = pl.BlockSpec((tm, D), lambda i: (i, 0))
    vec = pl.BlockSpec((1, D), lambda i: (0, 0))
    ins = [x, g.reshape(1, D), dh] + ([dres] if with_res else [])
    return pl.pallas_call(
        body, name=name, grid=(T // tm,), in_specs=[row, vec, row] + ([row] if with_res else []),
        out_specs=[row, vec],
        out_shape=[jax.ShapeDtypeStruct((T, D), F32), jax.ShapeDtypeStruct((1, D), F32)],
        compiler_params=_params("arbitrary"),
    )(*ins)


def _loss_head(x, g, target, name):
    T, D = x.shape
    tm = _pick(T, 512, SUBLANES)

    def body(x_ref, g_ref, t_ref, loss_ref, dx_ref, dg_ref):
        xv = x_ref[...]
        gv = g_ref[...]
        r = lax.rsqrt(jnp.mean(xv * xv, axis=-1, keepdims=True) + EPS)
        e = xv * r * gv - t_ref[...]
        dy = e * (1.0 / D)
        dyg = dy * gv
        c = jnp.mean(dyg * xv, axis=-1, keepdims=True)
        dx_ref[...] = r * dyg - xv * (r * r * r) * c

        @pl.when(pl.program_id(0) == 0)
        def _():
            loss_ref[...] = jnp.zeros_like(loss_ref)
            dg_ref[...] = jnp.zeros_like(dg_ref)

        loss_ref[...] += jnp.sum(e * e, axis=0, keepdims=True) * (0.5 / D)
        dg_ref[...] += jnp.sum(dy * xv * r, axis=0, keepdims=True)

    row = pl.BlockSpec((tm, D), lambda i: (i, 0))
    vec = pl.BlockSpec((1, D), lambda i: (0, 0))
    return pl.pallas_call(
        body, name=name, grid=(T // tm,), in_specs=[row, vec, row], out_specs=[vec, row, vec],
        out_shape=[jax.ShapeDtypeStruct((1, D), F32), jax.ShapeDtypeStruct((T, D), F32), jax.ShapeDtypeStruct((1, D), F32)],
        compiler_params=_params("arbitrary"),
    )(x, g.reshape(1, D), target)


def _ssm_disc_math(lre, lim, logdt, br, bi):
    dt = jnp.exp(logdt)
    mag = jnp.exp(lre * dt)
    ar = mag * jnp.cos(lim * dt)
    ai = mag * jnp.sin(lim * dt)
    den = lre * lre + lim * lim
    nr = ar - 1.0
    fr = (nr * lre + ai * lim) / den
    fi = (ai * lre - nr * lim) / den
    return ar, ai, fr[None] * br - fi[None] * bi, fr[None] * bi + fi[None] * br


def _ssm_disc(lre, lim, logdt, br, bi, name):
    def body(lre_ref, lim_ref, dt_ref, br_ref, bi_ref, ar_ref, ai_ref, bbr_ref, bbi_ref):
        ar, ai, bbr, bbi = _ssm_disc_math(lre_ref[...], lim_ref[...], dt_ref[...], br_ref[...], bi_ref[...])
        ar_ref[...] = ar
        ai_ref[...] = ai
        bbr_ref[...] = bbr
        bbi_ref[...] = bbi

    sd = jax.ShapeDtypeStruct
    return pl.pallas_call(
        body, name=name, out_shape=[sd(lre.shape, F32), sd(lre.shape, F32), sd(br.shape, F32), sd(br.shape, F32)],
    )(lre, lim, logdt, br, bi)


def _ssm_disc_bwd(lre, lim, logdt, br, bi, dar, dai, dbbr, dbbi, name):
    def body(lre_ref, lim_ref, dt_ref, br_ref, bi_ref, dar_ref, dai_ref, dbbr_ref, dbbi_ref,
             glre_ref, glim_ref, gdt_ref, gbr_ref, gbi_ref):
        _, vjp = jax.vjp(_ssm_disc_math, lre_ref[...], lim_ref[...], dt_ref[...], br_ref[...], bi_ref[...])
        glre, glim, gdt, gbr, gbi = vjp((dar_ref[...], dai_ref[...], dbbr_ref[...], dbbi_ref[...]))
        glre_ref[...] = glre
        glim_ref[...] = glim
        gdt_ref[...] = gdt
        gbr_ref[...] = gbr
        gbi_ref[...] = gbi

    sd = jax.ShapeDtypeStruct
    return pl.pallas_call(
        body, name=name,
        out_shape=[sd(lre.shape, F32), sd(lre.shape, F32), sd(logdt.shape, F32), sd(br.shape, F32), sd(br.shape, F32)],
    )(lre, lim, logdt, br, bi, dar, dai, dbbr, dbbi)


def _shift_segments(v, down):
    n = v.shape[0]
    rows = lax.broadcasted_iota(jnp.int32, v.shape, 0)
    if down:
        return jnp.where(rows >= 1, pltpu.roll(v, 1, 0), 0.0)
    return jnp.where(rows < n - 1, pltpu.roll(v, n - 1, 0), 0.0)


def _cpow(ar, ai, n):
    rr, ri = None, None
    pr, pi = ar, ai
    while n:
        if n & 1:
            if rr is None:
                rr, ri = pr, pi
            else:
                rr, ri = rr * pr - ri * pi, rr * pi + ri * pr
        n >>= 1
        if n:
            pr, pi = pr * pr - pi * pi, 2.0 * pr * pi
    return rr, ri


def _ssm_geometry(T, C):
    seg_steps = T // SSM_SEGMENTS
    kc = min(SSM_CHUNK_STEPS, seg_steps)
    return C // SSM_BLOCK_CH, seg_steps, kc, seg_steps // kc, SSM_SEGMENTS * kc


def _ssm_carries(u, dy, bmat, cmat, amat, *, reverse, name):
    src = dy if reverse else u
    T, C = src.shape
    nblk, seg_steps, kc, nchunk, rc = _ssm_geometry(T, C)
    half = bmat.shape[2] // 2

    def body(src_ref, w_ref, a_ref, out_ref, buf_ref, st_ref):
        c = pl.program_id(1)

        @pl.when(c == 0)
        def _():
            st_ref[...] = jnp.zeros_like(st_ref)

        if reverse:
            buf_ref[...] = _dot(src_ref[...].astype(BF16), w_ref[0], NT)
        else:
            buf_ref[...] = _dot(src_ref[...].astype(BF16), w_ref[0], NN)
        ar = a_ref[0, :, :half]
        ai = a_ref[0, :, half:]
        if reverse:
            ai = -ai

        def step(i, carry):
            xr, xi = carry
            k = (kc - 1 - i) if reverse else i
            row = pl.multiple_of(k * SUBLANES, SUBLANES)
            br = buf_ref[pl.ds(row, SUBLANES), :half]
            bi = buf_ref[pl.ds(row, SUBLANES), half:]
            return ar * xr - ai * xi + br, ar * xi + ai * xr + bi

        xr, xi = lax.fori_loop(0, kc, step, (st_ref[:, :half], st_ref[:, half:]), unroll=8)
        st_ref[:, :half] = xr
        st_ref[:, half:] = xi

        @pl.when(c == nchunk - 1)
        def _():
            pr, pi = _cpow(ar, ai, seg_steps)
            sr = jnp.zeros_like(xr)
            si = jnp.zeros_like(xi)
            for _ in range(SSM_SEGMENTS - 1):
                nr = xr + pr * sr - pi * si
                ni = xi + pr * si + pi * sr
                sr = _shift_segments(nr, not reverse)
                si = _shift_segments(ni, not reverse)
            out_ref[0, :, :half] = sr
            out_ref[0, :, half:] = si

    cidx = (lambda b, c: (nchunk - 1 - c, b)) if reverse else (lambda b, c: (c, b))
    w = cmat if reverse else bmat
    return pl.pallas_call(
        body, name=name, grid=(nblk, nchunk),
        in_specs=[pl.BlockSpec((rc, SSM_BLOCK_CH), cidx),
                  pl.BlockSpec((1,) + w.shape[1:], lambda b, c: (b, 0, 0)),
                  pl.BlockSpec((1, SUBLANES, 2 * half), lambda b, c: (b, 0, 0))],
        out_specs=pl.BlockSpec((1, SUBLANES, 2 * half), lambda b, c: (b, 0, 0)),
        out_shape=jax.ShapeDtypeStruct((nblk, SUBLANES, 2 * half), F32),
        scratch_shapes=[pltpu.VMEM((rc, 2 * half), F32), pltpu.VMEM((SUBLANES, 2 * half), F32)],
        compiler_params=_params("parallel", "arbitrary"),
    )(src, w, amat)


def _ssm_scan(u, bmat, cmat, amat, carries, dvec, name):
    T, C = u.shape
    nblk, seg_steps, kc, nchunk, rc = _ssm_geometry(T, C)
    half = bmat.shape[2] // 2

    def body(u_ref, b_ref, c_ref, a_ref, s_ref, d_ref, y_ref, x_ref, st_ref):
        c = pl.program_id(1)

        @pl.when(c == 0)
        def _():
            st_ref[...] = s_ref[0]

        uv = u_ref[...]
        x_ref[...] = _dot(uv.astype(BF16), b_ref[0], NN)
        ar = a_ref[0, :, :half]
        ai = a_ref[0, :, half:]

        def step(k, carry):
            xr, xi = carry
            row = pl.multiple_of(k * SUBLANES, SUBLANES)
            nr = ar * xr - ai * xi + x_ref[pl.ds(row, SUBLANES), :half]
            ni = ar * xi + ai * xr + x_ref[pl.ds(row, SUBLANES), half:]
            x_ref[pl.ds(row, SUBLANES), :half] = nr
            x_ref[pl.ds(row, SUBLANES), half:] = ni
            return nr, ni

        xr, xi = lax.fori_loop(0, kc, step, (st_ref[:, :half], st_ref[:, half:]), unroll=8)
        st_ref[:, :half] = xr
        st_ref[:, half:] = xi
        y_ref[...] = _dot(x_ref[...].astype(BF16), c_ref[0], NN) + d_ref[...] * uv

    return pl.pallas_call(
        body, name=name, grid=(nblk, nchunk),
        in_specs=[pl.BlockSpec((rc, SSM_BLOCK_CH), lambda b, c: (c, b)),
                  pl.BlockSpec((1,) + bmat.shape[1:], lambda b, c: (b, 0, 0)),
                  pl.BlockSpec((1,) + cmat.shape[1:], lambda b, c: (b, 0, 0)),
                  pl.BlockSpec((1, SUBLANES, 2 * half), lambda b, c: (b, 0, 0)),
                  pl.BlockSpec((1, SUBLANES, 2 * half), lambda b, c: (b, 0, 0)),
                  pl.BlockSpec((1, SSM_BLOCK_CH), lambda b, c: (0, b))],
        out_specs=[pl.BlockSpec((rc, SSM_BLOCK_CH), lambda b, c: (c, b)),
                   pl.BlockSpec((rc, 2 * half), lambda b, c: (c, b))],
        out_shape=[jax.ShapeDtypeStruct((T, C), F32), jax.ShapeDtypeStruct((T, nblk * 2 * half), F32)],
        scratch_shapes=[pltpu.VMEM((SUBLANES, 2 * half), F32)],
        compiler_params=_params("parallel", "arbitrary"),
    )(u, bmat, cmat, amat, carries, dvec)


def _ssm_scan_bwd(dy, u, xs, bmat, cmat, amat, carries, dvec, name):
    T, C = u.shape
    nblk, seg_steps, kc, nchunk, rc = _ssm_geometry(T, C)
    half = bmat.shape[2] // 2
    width = 2 * half

    def body(dy_ref, u_ref, x_ref, xp_ref, b_ref, c_ref, a_ref, s_ref, d_ref,
             du_ref, db_ref, dc_ref, da_ref, dd_ref, g_ref, st_ref, acc_ref):
        c = pl.program_id(1)

        @pl.when(c == 0)
        def _():
            st_ref[...] = s_ref[0]
            acc_ref[...] = jnp.zeros_like(acc_ref)
            db_ref[...] = jnp.zeros_like(db_ref)
            dc_ref[...] = jnp.zeros_like(dc_ref)
            dd_ref[...] = jnp.zeros_like(dd_ref)

        dyv = dy_ref[...]
        uv = u_ref[...]
        dyb = dyv.astype(BF16)
        g_ref[...] = _dot(dyb, c_ref[0], NT)
        ar = a_ref[0, :, :half]
        ai = a_ref[0, :, half:]

        def step(i, carry):
            gr, gi, sr, si = carry
            k = kc - 1 - i
            row = pl.multiple_of(k * SUBLANES, SUBLANES)
            nr = ar * gr + ai * gi + g_ref[pl.ds(row, SUBLANES), :half]
            ni = ar * gi - ai * gr + g_ref[pl.ds(row, SUBLANES), half:]
            g_ref[pl.ds(row, SUBLANES), :half] = nr
            g_ref[pl.ds(row, SUBLANES), half:] = ni
            prow = pl.multiple_of(jnp.maximum(k - 1, 0) * SUBLANES, SUBLANES)
            live = (k >= 1).astype(F32)
            xr = x_ref[pl.ds(prow, SUBLANES), :half] * live
            xi = x_ref[pl.ds(prow, SUBLANES), half:] * live
            return nr, ni, sr + xr * nr + xi * ni, si + xr * ni - xi * nr

        init = (st_ref[:, :half], st_ref[:, half:], acc_ref[:, :half], acc_ref[:, half:])
        gr, gi, sr, si = lax.fori_loop(0, kc, step, init, unroll=8)
        st_ref[:, :half] = gr
        st_ref[:, half:] = gi
        xpr = xp_ref[:, :half]
        xpi = xp_ref[:, half:]
        first = (c == nchunk - 1)
        xpr = jnp.where(first, _shift_segments(xpr, True), xpr)
        xpi = jnp.where(first, _shift_segments(xpi, True), xpi)
        acc_ref[:, :half] = sr + xpr * gr + xpi * gi
        acc_ref[:, half:] = si + xpr * gi - xpi * gr

        gb = g_ref[...].astype(BF16)
        du_ref[...] = _dot(gb, b_ref[0], NT) + dyv * d_ref[...]
        db_ref[0] += _dot(uv.astype(BF16), gb, TN)
        dc_ref[0] += _dot(dyb, x_ref[...].astype(BF16), TN)
        dd_ref[...] += jnp.sum(dyv * uv, axis=0, keepdims=True)

        @pl.when(c == nchunk - 1)
        def _():
            tot = jnp.sum(acc_ref[...], axis=0, keepdims=True)
            da_ref[0] = jnp.broadcast_to(tot, (SUBLANES, width))

    rev = lambda b, c: (nchunk - 1 - c, b)
    blk3 = lambda b, c: (b, 0, 0)
    prev_group = lambda b, c: (((nchunk - 1 - c) * kc - 1 + seg_steps) % seg_steps, b)
    sd = jax.ShapeDtypeStruct
    return pl.pallas_call(
        body, name=name, grid=(nblk, nchunk),
        in_specs=[pl.BlockSpec((rc, SSM_BLOCK_CH), rev), pl.BlockSpec((rc, SSM_BLOCK_CH), rev),
                  pl.BlockSpec((rc, width), rev), pl.BlockSpec((SUBLANES, width), prev_group),
                  pl.BlockSpec((1,) + bmat.shape[1:], blk3), pl.BlockSpec((1,) + cmat.shape[1:], blk3),
                  pl.BlockSpec((1, SUBLANES, width), blk3), pl.BlockSpec((1, SUBLANES, width), blk3),
                  pl.BlockSpec((1, SSM_BLOCK_CH), lambda b, c: (0, b))],
        out_specs=[pl.BlockSpec((rc, SSM_BLOCK_CH), rev), pl.BlockSpec((1, SSM_BLOCK_CH, width), blk3),
                   pl.BlockSpec((1, SSM_BLOCK_CH, width), blk3), pl.BlockSpec((1, SUBLANES, width), blk3),
                   pl.BlockSpec((1, SSM_BLOCK_CH), lambda b, c: (0, b))],
        out_shape=[sd((T, C), F32), sd((nblk, SSM_BLOCK_CH, width), F32), sd((nblk, SSM_BLOCK_CH, width), F32),
                   sd((nblk, SUBLANES, width), F32), sd((1, C), F32)],
        scratch_shapes=[pltpu.VMEM((rc, width), F32), pltpu.VMEM((SUBLANES, width), F32), pltpu.VMEM((SUBLANES, width), F32)],
        compiler_params=_params("parallel", "arbitrary"),
    )(dy, u, xs, xs, bmat, cmat, amat, carries, dvec)


def _ssm_post(y, uz, w_glu, b_glu, name):
    T, C = y.shape
    tm = _pick(T, 512, SUBLANES)

    def body(y_ref, z_ref, w_ref, b_ref, o_ref, a_ref):
        a, _ = _gelu_and_grad(y_ref[...])
        ab = a.astype(BF16)
        sg = _sigmoid(_dot(ab, w_ref[...], NN) + b_ref[...])
        sz, _ = _silu_and_grad(z_ref[...])
        o_ref[...] = (a * sg * sz).astype(BF16)
        a_ref[...] = ab

    row = pl.BlockSpec((tm, C), lambda i: (i, 0))
    return pl.pallas_call(
        body, name=name, grid=(T // tm,),
        in_specs=[row, pl.BlockSpec((tm, C), lambda i: (i, 1)), pl.BlockSpec((C, C), lambda i: (0, 0)),
                  pl.BlockSpec((1, C), lambda i: (0, 0))],
        out_specs=[row, row], out_shape=[jax.ShapeDtypeStruct((T, C), BF16)] * 2, compiler_params=_params("parallel"),
    )(y, uz, w_glu, b_glu.reshape(1, C))


def _ssm_post_bwd(do, y, uz, w_glu, b_glu, name):
    T, C = y.shape
    tm = _pick(T, 512, SUBLANES)

    def body(do_ref, y_ref, z_ref, w_ref, b_ref, dy_ref, dz_ref, ds_ref, db_ref):
        dov = do_ref[...]
        a, da_dy = _gelu_and_grad(y_ref[...])
        sg = _sigmoid(_dot(a.astype(BF16), w_ref[...], NN) + b_ref[...])
        sz, dsz = _silu_and_grad(z_ref[...])
        yg = a * sg
        dz_ref[...] = (dov * yg * dsz).astype(BF16)
        dyg = dov * sz
        ds = dyg * a * sg * (1.0 - sg)
        dsb = ds.astype(BF16)
        ds_ref[...] = dsb
        da = dyg * sg + _dot(dsb, w_ref[...], NT)
        dy_ref[...] = da * da_dy

        @pl.when(pl.program_id(0) == 0)
        def _():
            db_ref[...] = jnp.zeros_like(db_ref)

        db_ref[...] += jnp.sum(ds, axis=0, keepdims=True)

    row = pl.BlockSpec((tm, C), lambda i: (i, 0))
    vec = pl.BlockSpec((1, C), lambda i: (0, 0))
    sd = jax.ShapeDtypeStruct
    return pl.pallas_call(
        body, name=name, grid=(T // tm,),
        in_specs=[row, row, pl.BlockSpec((tm, C), lambda i: (i, 1)), pl.BlockSpec((C, C), lambda i: (0, 0)), vec],
        out_specs=[row, row, row, vec],
        out_shape=[sd((T, C), F32), sd((T, C), BF16), sd((T, C), BF16), sd((1, C), F32)],
        compiler_params=_params("arbitrary"),
    )(do, y, uz, w_glu, b_glu.reshape(1, C))


def _rel_bucket(dist):
    n = jnp.maximum(dist, 0)
    max_exact = NUM_BUCKETS // 2
    n_f = jnp.maximum(n, 1).astype(F32)
    large = max_exact + (jnp.log(n_f / max_exact) / math.log(REL_MAX_DISTANCE / max_exact)
                         * (NUM_BUCKETS - max_exact)).astype(jnp.int32)
    large = jnp.minimum(large, NUM_BUCKETS - 1)
    return jnp.where(n < max_exact, n, large)


def _band_tables():
    qi = jnp.arange(ATTN_BLOCK)[:, None]
    kj = jnp.arange(2 * ATTN_BLOCK)[None, :]
    delta = ATTN_BLOCK + qi - kj
    buckets, bands = [], []
    for window, dilation in ATTN_CONFIGS:
        bands.append((delta >= 0) & (delta <= window // dilation))
        buckets.append(_rel_bucket(jnp.maximum(delta, 0) * dilation))
    return jnp.stack(buckets), jnp.stack(bands)


def _attn_blocks_per_residue(T):
    return [T // (ATTN_BLOCK * d) for _, d in ATTN_CONFIGS]


ATTN_UNITS = 4


def _attn_tile(T, r):
    nq = max(1, ATTN_UNITS // r)
    rows = ATTN_BLOCK * r * nq
    return nq, rows, T // rows


def _attn_units(r, nq, chunk):
    if r >= ATTN_UNITS:
        return [(chunk * ATTN_UNITS + i, None) for i in range(ATTN_UNITS)]
    units = []
    for j in range(nq):
        for s in range(r):
            units.append((ATTN_BLOCK * j * r + s, ATTN_BLOCK * (j - 1) * r + s if j else None))
    return units


def _rows(start, r):
    return pl.ds(start, ATTN_BLOCK, stride=r) if r > 1 else pl.ds(start, ATTN_BLOCK)


def _attn_group_fwd(qkv, biasm, g, name):
    T = qkv.shape[0]
    r = ATTN_CONFIGS[g][1]
    B, hd = ATTN_BLOCK, ATTN_HEAD_DIM
    nq, rows, ntiles = _attn_tile(T, r)
    nchunks = max(1, r // ATTN_UNITS)
    last_prev = B * (nq - 1) * r
    scale = hd ** -0.5
    tiles_per_tensor = 3 * HEADS_PER_GROUP * hd // LANES

    def body(q_ref, kc_ref, kp_ref, vc_ref, vp_ref, bias_ref, o_ref, lse_ref, s_ref, p_ref):
        n = pl.program_id(1)
        lane = lax.broadcasted_iota(jnp.int32, (1, LANES), 1)
        col = lax.broadcasted_iota(jnp.int32, (1, 2 * B), 1)
        masks = [lane < hd, lane >= hd]
        first_pen = jnp.where((col < B) & (n == 0), NEG_INF, 0.0)

        def chunk_body(chunk):
            units = _attn_units(r, nq, chunk)

            def keys(cur_ref, prev_ref, cs, ps):
                prev = prev_ref[_rows(last_prev + (cs if r >= ATTN_UNITS else cs % r), r), :] if ps is None else cur_ref[_rows(ps, r), :]
                return jnp.concatenate([prev, cur_ref[_rows(cs, r), :]], axis=0).astype(BF16)

            for u, (cs, ps) in enumerate(units):
                qv = q_ref[_rows(cs, r), :]
                kw = keys(kc_ref, kp_ref, cs, ps)
                for hh in range(2):
                    s_ref[2 * u + hh] = _dot(jnp.where(masks[hh], qv, 0.0).astype(BF16), kw, NT)
            for u, (cs, ps) in enumerate(units):
                lses = []
                for hh in range(2):
                    s = s_ref[2 * u + hh] * scale + bias_ref[hh]
                    if ps is None:
                        s = s + first_pen
                    m = jnp.max(s, axis=-1, keepdims=True)
                    p = jnp.exp(s - m)
                    l = jnp.sum(p, axis=-1, keepdims=True)
                    p_ref[2 * u + hh] = (p / l).astype(BF16)
                    lses.append(m + jnp.log(l))
                lse_ref[_rows(cs, r), :] = jnp.where(masks[0], lses[0], lses[1])
            for u, (cs, ps) in enumerate(units):
                vw = keys(vc_ref, vp_ref, cs, ps)
                o_ref[_rows(cs, r), :] = (_dot(p_ref[2 * u], jnp.where(masks[0], vw, 0), NN)
                                          + _dot(p_ref[2 * u + 1], jnp.where(masks[1], vw, 0), NN))

        if nchunks == 1:
            chunk_body(0)
        else:
            pl.loop(0, nchunks)(chunk_body)

    def cur(t):
        return pl.BlockSpec((rows, LANES), lambda hf, n: (n, t * tiles_per_tensor + 2 * g + hf))

    def prev(t):
        return pl.BlockSpec((rows, LANES), lambda hf, n: (jnp.maximum(n - 1, 0), t * tiles_per_tensor + 2 * g + hf))

    out = pl.BlockSpec((rows, LANES), lambda hf, n: (n, hf))
    sd = jax.ShapeDtypeStruct((T, 2 * LANES), F32)
    return pl.pallas_call(
        body, name=name, grid=(2, ntiles),
        in_specs=[cur(0), cur(1), prev(1), cur(2), prev(2), pl.BlockSpec((None, 2, B, 2 * B), lambda hf, n: (g, hf, 0, 0))],
        out_specs=[out, out], out_shape=[sd, sd],
        scratch_shapes=[pltpu.VMEM((2 * ATTN_UNITS, B, 2 * B), F32), pltpu.VMEM((2 * ATTN_UNITS, B, 2 * B), BF16)],
        compiler_params=_params("parallel", "parallel"),
    )(qkv, qkv, qkv, qkv, qkv, biasm)


def _attn_group_bwd(qkv, do, dvec, lse, biasm, g, name):
    T = qkv.shape[0]
    r = ATTN_CONFIGS[g][1]
    B, hd = ATTN_BLOCK, ATTN_HEAD_DIM
    nq, rows, ntiles = _attn_tile(T, r)
    nchunks = max(1, r // ATTN_UNITS)
    last_prev = B * (nq - 1) * r
    scale = hd ** -0.5
    tiles_per_tensor = 3 * HEADS_PER_GROUP * hd // LANES

    def body(q_ref, kc_ref, kp_ref, vc_ref, vp_ref, do_ref, dv_ref, lse_ref, bias_ref,
             dq_ref, dk_ref, dvo_ref, dbias_ref, ck_ref, cv_ref, ak_ref, av_ref, s_ref, dp_ref, p_ref, ds_ref):
        n = pl.program_id(1)
        lane = lax.broadcasted_iota(jnp.int32, (1, LANES), 1)
        col = lax.broadcasted_iota(jnp.int32, (1, 2 * B), 1)
        masks = [lane < hd, lane >= hd]
        first_pen = jnp.where((col < B) & (n == 0), NEG_INF, 0.0)

        @pl.when(n == 0)
        def _():
            dbias_ref[...] = jnp.zeros_like(dbias_ref)
            ck_ref[...] = jnp.zeros_like(ck_ref)
            cv_ref[...] = jnp.zeros_like(cv_ref)

        def chunk_body(chunk):
            units = _attn_units(r, nq, chunk)

            def prev_rows(cs):
                return _rows(last_prev + (cs if r >= ATTN_UNITS else cs % r), r)

            def keys(cur_ref, prev_ref, cs, ps):
                prev = prev_ref[prev_rows(cs), :] if ps is None else cur_ref[_rows(ps, r), :]
                return jnp.concatenate([prev, cur_ref[_rows(cs, r), :]], axis=0).astype(BF16)

            for u, (cs, ps) in enumerate(units):
                qv = q_ref[_rows(cs, r), :]
                dov = do_ref[_rows(cs, r), :]
                kw = keys(kc_ref, kp_ref, cs, ps)
                vw = keys(vc_ref, vp_ref, cs, ps)
                for hh in range(2):
                    s_ref[2 * u + hh] = _dot(jnp.where(masks[hh], qv, 0.0).astype(BF16), kw, NT)
                    dp_ref[2 * u + hh] = _dot(jnp.where(masks[hh], dov, 0.0).astype(BF16), vw, NT)
            for u, (cs, ps) in enumerate(units):
                lse_t = lse_ref[_rows(cs, r), :]
                dv_t = dv_ref[_rows(cs, r), :]
                for hh in range(2):
                    lo = hh * hd
                    s = s_ref[2 * u + hh] * scale + bias_ref[hh]
                    if ps is None:
                        s = s + first_pen
                    p = jnp.exp(s - lse_t[:, lo:lo + 1])
                    ds = p * (dp_ref[2 * u + hh] + dv_t[:, lo:lo + 1])
                    dbias_ref[hh] += ds
                    p_ref[2 * u + hh] = p.astype(BF16)
                    ds_ref[2 * u + hh] = ds.astype(BF16)
            for u, (cs, ps) in enumerate(units):
                qv = q_ref[_rows(cs, r), :]
                dov = do_ref[_rows(cs, r), :]
                kw = keys(kc_ref, kp_ref, cs, ps)
                dq, dkw, dvw = 0.0, 0.0, 0.0
                for hh in range(2):
                    dsb = ds_ref[2 * u + hh]
                    dq = dq + _dot(dsb, jnp.where(masks[hh], kw, 0), NN)
                    dkw = dkw + _dot(dsb, jnp.where(masks[hh], qv, 0.0).astype(BF16), TN)
                    dvw = dvw + _dot(p_ref[2 * u + hh], jnp.where(masks[hh], dov, 0.0).astype(BF16), TN)
                dq_ref[_rows(cs, r), :] = dq * scale
                ak_ref[_rows(cs, r), :] = dkw[B:] * scale
                av_ref[_rows(cs, r), :] = dvw[B:]
                if ps is None:
                    ck_ref[prev_rows(cs), :] += dkw[:B] * scale
                    cv_ref[prev_rows(cs), :] += dvw[:B]
                else:
                    ak_ref[_rows(ps, r), :] += dkw[:B] * scale
                    av_ref[_rows(ps, r), :] += dvw[:B]

        @pl.when(n < ntiles)
        def _():
            if nchunks == 1:
                chunk_body(0)
            else:
                pl.loop(0, nchunks)(chunk_body)

        dk_ref[...] = ck_ref[...].astype(BF16)
        dvo_ref[...] = cv_ref[...].astype(BF16)
        ck_ref[...] = ak_ref[...]
        cv_ref[...] = av_ref[...]

    last = ntiles - 1

    def cur(t):
        return pl.BlockSpec((rows, LANES), lambda hf, n: (jnp.minimum(n, last), t * tiles_per_tensor + 2 * g + hf))

    def prev(t):
        return pl.BlockSpec((rows, LANES), lambda hf, n: (jnp.clip(n - 1, 0, last), t * tiles_per_tensor + 2 * g + hf))

    nat = pl.BlockSpec((rows, LANES), lambda hf, n: (jnp.minimum(n, last), hf))
    nat_prev = pl.BlockSpec((rows, LANES), lambda hf, n: (jnp.clip(n - 1, 0, last), hf))
    tab = pl.BlockSpec((None, 2, B, 2 * B), lambda hf, n: (g, hf, 0, 0))
    dtab = pl.BlockSpec((2, B, 2 * B), lambda hf, n: (hf, 0, 0))
    sd = jax.ShapeDtypeStruct
    vm = pltpu.VMEM
    return pl.pallas_call(
        body, name=name, grid=(2, ntiles + 1),
        in_specs=[cur(0), cur(1), prev(1), cur(2), prev(2), nat, nat, nat, tab],
        out_specs=[nat, nat_prev, nat_prev, dtab],
        out_shape=[sd((T, 2 * LANES), F32), sd((T, 2 * LANES), BF16), sd((T, 2 * LANES), BF16),
                   sd((HEADS_PER_GROUP, B, 2 * B), F32)],
        scratch_shapes=[vm((rows, LANES), F32), vm((rows, LANES), F32), vm((rows, LANES), F32), vm((rows, LANES), F32),
                        vm((2 * ATTN_UNITS, B, 2 * B), F32), vm((2 * ATTN_UNITS, B, 2 * B), F32),
                        vm((2 * ATTN_UNITS, B, 2 * B), BF16), vm((2 * ATTN_UNITS, B, 2 * B), BF16)],
        compiler_params=_params("parallel", "arbitrary"),
    )(qkv, qkv, qkv, qkv, qkv, do, dvec, lse, biasm)


def _attn_fwd(q, k, v, biasm, name):
    ng, T, gw = q.shape
    hd = ATTN_HEAD_DIM
    nh = gw // hd
    nblk = T // ATTN_BLOCK
    nbs = _attn_blocks_per_residue(T)
    scale = hd ** -0.5
    B = ATTN_BLOCK

    def body(q_ref, kc_ref, kp_ref, vc_ref, vp_ref, bias_ref, o_ref, lse_ref, s_ref, p_ref):
        g = pl.program_id(0)
        b = pl.program_id(1)
        nb = jnp.where(g == 0, nbs[0], jnp.where(g == 1, nbs[1], nbs[2]))
        no_prev = (b % nb) == 0
        col = lax.broadcasted_iota(jnp.int32, (1, 2 * B), 1)
        pen = jnp.where((col < B) & no_prev, NEG_INF, 0.0)
        heads = [slice(h * hd, (h + 1) * hd) for h in range(nh)]
        for h, hs in enumerate(heads):
            kw = jnp.concatenate([kp_ref[0, :, hs], kc_ref[0, :, hs]], axis=0)
            s_ref[h] = _dot(q_ref[0, :, hs], kw, NT)
        for h, hs in enumerate(heads):
            s = s_ref[h] * scale + bias_ref[0, h] + pen
            m = jnp.max(s, axis=-1, keepdims=True)
            p = jnp.exp(s - m)
            l = jnp.sum(p, axis=-1, keepdims=True)
            p_ref[h] = (p / l).astype(BF16)
            lse_ref[0, :, hs] = jnp.broadcast_to(m + jnp.log(l), (B, hd))
        for h, hs in enumerate(heads):
            vw = jnp.concatenate([vp_ref[0, :, hs], vc_ref[0, :, hs]], axis=0)
            o_ref[0, :, hs] = _dot(p_ref[h], vw, NN)

    cur = pl.BlockSpec((1, B, gw), lambda g, b: (g, b, 0))
    prev = pl.BlockSpec((1, B, gw), lambda g, b: (g, jnp.maximum(b - 1, 0), 0))
    return pl.pallas_call(
        body, name=name, grid=(ng, nblk),
        in_specs=[cur, cur, prev, cur, prev, pl.BlockSpec((1, nh, B, 2 * B), lambda g, b: (g, 0, 0, 0))],
        out_specs=[cur, cur], out_shape=[jax.ShapeDtypeStruct(q.shape, F32)] * 2,
        scratch_shapes=[pltpu.VMEM((nh, B, 2 * B), F32), pltpu.VMEM((nh, B, 2 * B), BF16)],
        compiler_params=_params("parallel", "parallel"),
    )(q, k, k, v, v, biasm)


def _attn_bwd(q, k, v, do, dvec, lse, biasm, name):
    ng, T, gw = q.shape
    hd = ATTN_HEAD_DIM
    nh = gw // hd
    nblk = T // ATTN_BLOCK
    nbs = _attn_blocks_per_residue(T)
    scale = hd ** -0.5
    B = ATTN_BLOCK

    def body(q_ref, kc_ref, kp_ref, vc_ref, vp_ref, do_ref, dv_ref, lse_ref, bias_ref,
             dq_ref, dk_ref, dvo_ref, dbias_ref, ck_ref, cv_ref, s_ref, dp_ref, p_ref, ds_ref):
        g = pl.program_id(0)
        b = pl.program_id(1)
        nb = jnp.where(g == 0, nbs[0], jnp.where(g == 1, nbs[1], nbs[2]))
        no_prev = (b % nb) == 0

        @pl.when(b == 0)
        def _():
            dbias_ref[...] = jnp.zeros_like(dbias_ref)
            ck_ref[...] = jnp.zeros_like(ck_ref)
            cv_ref[...] = jnp.zeros_like(cv_ref)

        @pl.when(b < nblk)
        def _():
            col = lax.broadcasted_iota(jnp.int32, (1, 2 * B), 1)
            pen = jnp.where((col < B) & no_prev, NEG_INF, 0.0)
            heads = [slice(h * hd, (h + 1) * hd) for h in range(nh)]
            for h, hs in enumerate(heads):
                kw = jnp.concatenate([kp_ref[0, :, hs], kc_ref[0, :, hs]], axis=0)
                vw = jnp.concatenate([vp_ref[0, :, hs], vc_ref[0, :, hs]], axis=0)
                s_ref[h] = _dot(q_ref[0, :, hs], kw, NT)
                dp_ref[h] = _dot(do_ref[0, :, hs], vw, NT)
            for h, hs in enumerate(heads):
                lse_col = lse_ref[0, :, h * hd:h * hd + 1]
                d_col = dv_ref[0, :, h * hd:h * hd + 1]
                p = jnp.exp(s_ref[h] * scale + bias_ref[0, h] + pen - lse_col)
                ds = p * (dp_ref[h] + d_col)
                dbias_ref[0, h] += ds
                p_ref[h] = p.astype(BF16)
                ds_ref[h] = ds.astype(BF16)
            for h, hs in enumerate(heads):
                qh = q_ref[0, :, hs]
                kw = jnp.concatenate([kp_ref[0, :, hs], kc_ref[0, :, hs]], axis=0)
                dq_ref[0, :, hs] = (_dot(ds_ref[h], kw, NN) * scale).astype(BF16)
                dkw = _dot(ds_ref[h], qh, TN) * scale
                dvw = _dot(p_ref[h], do_ref[0, :, hs], TN)
                dk_ref[0, :, hs] = (ck_ref[:, hs] + dkw[:B]).astype(BF16)
                dvo_ref[0, :, hs] = (cv_ref[:, hs] + dvw[:B]).astype(BF16)
                ck_ref[:, hs] = dkw[B:]
                cv_ref[:, hs] = dvw[B:]

        @pl.when(b == nblk)
        def _():
            dk_ref[0] = ck_ref[...].astype(BF16)
            dvo_ref[0] = cv_ref[...].astype(BF16)

    last = nblk - 1
    cur = pl.BlockSpec((1, B, gw), lambda g, b: (g, jnp.minimum(b, last), 0))
    prev = pl.BlockSpec((1, B, gw), lambda g, b: (g, jnp.clip(b - 1, 0, last), 0))
    tab = pl.BlockSpec((1, nh, B, 2 * B), lambda g, b: (g, 0, 0, 0))
    sd = jax.ShapeDtypeStruct
    return pl.pallas_call(
        body, name=name, grid=(ng, nblk + 1),
        in_specs=[cur, cur, prev, cur, prev, cur, cur, cur, tab],
        out_specs=[cur, prev, prev, tab],
        out_shape=[sd(q.shape, BF16), sd(q.shape, BF16), sd(q.shape, BF16), sd(biasm.shape, F32)],
        scratch_shapes=[pltpu.VMEM((B, gw), F32), pltpu.VMEM((B, gw), F32), pltpu.VMEM((nh, B, 2 * B), F32),
                        pltpu.VMEM((nh, B, 2 * B), F32), pltpu.VMEM((nh, B, 2 * B), BF16), pltpu.VMEM((nh, B, 2 * B), BF16)],
        compiler_params=_params("parallel", "arbitrary"),
    )(q, k, k, v, v, do, dvec, lse, biasm)


def _attn_mix(os, lses, z, name):
    T, gw = os[0].shape
    C = z.shape[1]
    tm = _pick(T, 512, SUBLANES)

    def body(o0_ref, o1_ref, o2_ref, l0_ref, l1_ref, l2_ref, z_ref, out_ref):
        ls = [l0_ref[...], l1_ref[...], l2_ref[...]]
        mx = jnp.maximum(jnp.maximum(ls[0], ls[1]), ls[2])
        es = [jnp.exp(l - mx) for l in ls]
        den = es[0] + es[1] + es[2]
        for i, o_ref in enumerate((o0_ref, o1_ref, o2_ref)):
            sz, _ = _silu_and_grad(z_ref[:, i * gw:(i + 1) * gw])
            out_ref[:, i * gw:(i + 1) * gw] = (o_ref[...] * (es[i] / den) * sz).astype(BF16)

    row = pl.BlockSpec((tm, C), lambda i: (i, 0))
    grp = pl.BlockSpec((tm, gw), lambda i: (i, 0))
    return pl.pallas_call(
        body, name=name, grid=(T // tm,), in_specs=[grp] * 6 + [row], out_specs=row,
        out_shape=jax.ShapeDtypeStruct((T, C), BF16), compiler_params=_params("parallel"),
    )(*os, *lses, z)


def _attn_mix_bwd(dout, os, lses, z, name):
    T, gw = os[0].shape
    C = z.shape[1]
    tm = _pick(T, 512, SUBLANES)
    head_of = np.arange(gw) // ATTN_HEAD_DIM
    ones = jnp.asarray(head_of[:, None] == head_of[None, :], BF16)

    def body(dout_ref, o0_ref, o1_ref, o2_ref, l0_ref, l1_ref, l2_ref, z_ref, ones_ref,
             dz_ref, do0_ref, do1_ref, do2_ref, dv0_ref, dv1_ref, dv2_ref):
        ls = [l0_ref[...], l1_ref[...], l2_ref[...]]
        mx = jnp.maximum(jnp.maximum(ls[0], ls[1]), ls[2])
        es = [jnp.exp(l - mx) for l in ls]
        den = es[0] + es[1] + es[2]
        alphas, ebar = [], 0.0
        for i, (o_ref, do_ref) in enumerate(((o0_ref, do0_ref), (o1_ref, do1_ref), (o2_ref, do2_ref))):
            sl = slice(i * gw, (i + 1) * gw)
            alpha = es[i] / den
            ov = o_ref[...]
            dv = dout_ref[:, sl]
            sz, dsz = _silu_and_grad(z_ref[:, sl])
            dz_ref[:, sl] = (dv * ov * alpha * dsz).astype(BF16)
            da = dv * sz
            do_ref[...] = da * alpha
            t = da * ov
            t1 = t.astype(BF16)
            r1 = t - t1.astype(F32)
            t2 = r1.astype(BF16)
            t3 = (r1 - t2.astype(F32)).astype(BF16)
            e = _dot(t1, ones_ref[...], NN) + _dot(t2, ones_ref[...], NN) + _dot(t3, ones_ref[...], NN)
            ebar = ebar + alpha * e
            alphas.append(alpha)
        for alpha, dv_ref in zip(alphas, (dv0_ref, dv1_ref, dv2_ref)):
            dv_ref[...] = -alpha * ebar

    row = pl.BlockSpec((tm, C), lambda i: (i, 0))
    grp = pl.BlockSpec((tm, gw), lambda i: (i, 0))
    sd = jax.ShapeDtypeStruct
    res = pl.pallas_call(
        body, name=name, grid=(T // tm,),
        in_specs=[row] + [grp] * 6 + [row, pl.BlockSpec((gw, gw), lambda i: (0, 0))], out_specs=[row] + [grp] * 6,
        out_shape=[sd((T, C), BF16)] + [sd((T, gw), F32)] * 6, compiler_params=_params("parallel"),
    )(dout, *os, *lses, z, ones)
    return res[0], res[1:4], res[4:7]


def _mem_attn(qz, kv, name):
    T = qz.shape[0]
    dm = qz.shape[1] // 2
    M = kv.shape[0]
    hd = dm // MEM_HEADS
    scale = hd ** -0.5
    tm = _pick(T, 512, SUBLANES)

    def body(q_ref, z_ref, k_ref, v_ref, o_ref, s_ref, p_ref):
        heads = [slice(h * hd, (h + 1) * hd) for h in range(MEM_HEADS)]
        for h, sl in enumerate(heads):
            s_ref[h] = _dot(q_ref[:, sl].astype(BF16), k_ref[:, sl], NT)
        for h, sl in enumerate(heads):
            s = s_ref[h] * scale
            p = jnp.exp(s - jnp.max(s, axis=-1, keepdims=True))
            p_ref[h] = (p / jnp.sum(p, axis=-1, keepdims=True)).astype(BF16)
        for h, sl in enumerate(heads):
            sz, _ = _silu_and_grad(z_ref[:, sl])
            o_ref[:, sl] = (_dot(p_ref[h], v_ref[:, sl], NN) * sz).astype(BF16)

    return pl.pallas_call(
        body, name=name, grid=(T // tm,),
        in_specs=[pl.BlockSpec((tm, dm), lambda i: (i, 0)), pl.BlockSpec((tm, dm), lambda i: (i, 1)),
                  pl.BlockSpec((M, dm), lambda i: (0, 0)), pl.BlockSpec((M, dm), lambda i: (0, 1))],
        out_specs=pl.BlockSpec((tm, dm), lambda i: (i, 0)),
        out_shape=jax.ShapeDtypeStruct((T, dm), BF16),
        scratch_shapes=[pltpu.VMEM((MEM_HEADS, tm, M), F32), pltpu.VMEM((MEM_HEADS, tm, M), BF16)],
        compiler_params=_params("parallel"),
    )(qz, qz, kv, kv)


def _mem_attn_bwd(do, qz, kv, name):
    T = qz.shape[0]
    dm = qz.shape[1] // 2
    M = kv.shape[0]
    hd = dm // MEM_HEADS
    scale = hd ** -0.5
    tm = _pick(T, 512, SUBLANES)

    def body(do_ref, q_ref, z_ref, k_ref, v_ref, dq_ref, dz_ref, dk_ref, dv_ref, s_ref, dp_ref, p_ref, ds_ref, dob_ref):
        @pl.when(pl.program_id(0) == 0)
        def _():
            dk_ref[...] = jnp.zeros_like(dk_ref)
            dv_ref[...] = jnp.zeros_like(dv_ref)

        heads = [slice(h * hd, (h + 1) * hd) for h in range(MEM_HEADS)]
        for h, sl in enumerate(heads):
            sz, _ = _silu_and_grad(z_ref[:, sl])
            dob = (do_ref[:, sl] * sz).astype(BF16)
            dob_ref[:, sl] = dob
            s_ref[h] = _dot(q_ref[:, sl].astype(BF16), k_ref[:, sl], NT)
            dp_ref[h] = _dot(dob, v_ref[:, sl], NT)
        for h, sl in enumerate(heads):
            s = s_ref[h] * scale
            p = jnp.exp(s - jnp.max(s, axis=-1, keepdims=True))
            pn = p / jnp.sum(p, axis=-1, keepdims=True)
            dp = dp_ref[h]
            p_ref[h] = pn.astype(BF16)
            ds_ref[h] = (pn * (dp - jnp.sum(dp * pn, axis=-1, keepdims=True))).astype(BF16)
        for h, sl in enumerate(heads):
            _, dsz = _silu_and_grad(z_ref[:, sl])
            dz_ref[:, sl] = (do_ref[:, sl] * _dot(p_ref[h], v_ref[:, sl], NN) * dsz).astype(BF16)
            dq_ref[:, sl] = (_dot(ds_ref[h], k_ref[:, sl], NN) * scale).astype(BF16)
            dk_ref[:, sl] += _dot(ds_ref[h], q_ref[:, sl].astype(BF16), TN) * scale
            dv_ref[:, sl] += _dot(p_ref[h], dob_ref[:, sl], TN)

    rowq = pl.BlockSpec((tm, dm), lambda i: (i, 0))
    rowz = pl.BlockSpec((tm, dm), lambda i: (i, 1))
    kb = pl.BlockSpec((M, dm), lambda i: (0, 0))
    vb = pl.BlockSpec((M, dm), lambda i: (0, 1))
    sd = jax.ShapeDtypeStruct
    dq, dz, dk, dv = pl.pallas_call(
        body, name=name, grid=(T // tm,), in_specs=[rowq, rowq, rowz, kb, vb],
        out_specs=[rowq, rowq, kb, kb],
        out_shape=[sd((T, dm), BF16), sd((T, dm), BF16), sd((M, dm), F32), sd((M, dm), F32)],
        scratch_shapes=[pltpu.VMEM((MEM_HEADS, tm, M), F32), pltpu.VMEM((MEM_HEADS, tm, M), F32),
                        pltpu.VMEM((MEM_HEADS, tm, M), BF16), pltpu.VMEM((MEM_HEADS, tm, M), BF16), pltpu.VMEM((tm, dm), BF16)],
        compiler_params=_params("arbitrary"),
    )(do, qz, qz, kv, kv)
    return dq, dz, dk, dv


def _merge(bps, logits, b_gate, name):
    T, D = bps[0].shape
    tm = _pick(T, 512, SUBLANES)

    def body(p0_ref, p1_ref, p2_ref, l_ref, b_ref, o_ref):
        acc = 0.0
        for i, p_ref in enumerate((p0_ref, p1_ref, p2_ref)):
            sl = slice(i * D, (i + 1) * D)
            acc = acc + _sigmoid(l_ref[:, sl] + b_ref[:, sl]) * p_ref[...]
        o_ref[...] = acc.astype(BF16)

    row = pl.BlockSpec((tm, D), lambda i: (i, 0))
    return pl.pallas_call(
        body, name=name, grid=(T // tm,),
        in_specs=[row, row, row, pl.BlockSpec((tm, 3 * D), lambda i: (i, 0)), pl.BlockSpec((1, 3 * D), lambda i: (0, 0))],
        out_specs=row, out_shape=jax.ShapeDtypeStruct((T, D), BF16), compiler_params=_params("parallel"),
    )(*bps, logits, b_gate.reshape(1, 3 * D))


def _merge_bwd(dmerged, bps, logits, b_gate, name):
    T, D = bps[0].shape
    tm = _pick(T, 512, SUBLANES)

    def body(dm_ref, p0_ref, p1_ref, p2_ref, l_ref, b_ref, d0_ref, d1_ref, d2_ref, dl_ref, db_ref):
        @pl.when(pl.program_id(0) == 0)
        def _():
            db_ref[...] = jnp.zeros_like(db_ref)

        dmv = dm_ref[...]
        for i, (p_ref, d_ref) in enumerate(((p0_ref, d0_ref), (p1_ref, d1_ref), (p2_ref, d2_ref))):
            sl = slice(i * D, (i + 1) * D)
            gt = _sigmoid(l_ref[:, sl] + b_ref[:, sl])
            d_ref[...] = (dmv * gt).astype(BF16)
            dl = dmv * p_ref[...] * gt * (1.0 - gt)
            dl_ref[:, sl] = dl.astype(BF16)
            db_ref[:, sl] += jnp.sum(dl, axis=0, keepdims=True)

    row = pl.BlockSpec((tm, D), lambda i: (i, 0))
    wide = pl.BlockSpec((tm, 3 * D), lambda i: (i, 0))
    vec = pl.BlockSpec((1, 3 * D), lambda i: (0, 0))
    sd = jax.ShapeDtypeStruct
    return pl.pallas_call(
        body, name=name, grid=(T // tm,), in_specs=[row, row, row, row, wide, vec],
        out_specs=[row, row, row, wide, vec],
        out_shape=[sd((T, D), BF16)] * 3 + [sd((T, 3 * D), BF16), sd((1, 3 * D), F32)],
        compiler_params=_params("arbitrary"),
    )(dmerged, *bps, logits, b_gate.reshape(1, 3 * D))


def _to_segments(a):
    T, C = a.shape
    return a.reshape(SSM_SEGMENTS, T // SSM_SEGMENTS, C).transpose(1, 0, 2).reshape(T, C)


def _from_segments(a):
    T, C = a.shape
    return a.reshape(T // SSM_SEGMENTS, SSM_SEGMENTS, C).transpose(1, 0, 2).reshape(T, C)


def _to_residues(a):
    T = a.shape[0]
    gw = HEADS_PER_GROUP * ATTN_HEAD_DIM
    out = []
    for g, (_, r) in enumerate(ATTN_CONFIGS):
        ag = a[:, g * gw:(g + 1) * gw].reshape(T // r, r, gw)
        out.append(ag.transpose(1, 0, 2).reshape(T, gw))
    return jnp.stack(out)


def _from_residues(a):
    _, T, gw = a.shape
    out = []
    for g, (_, r) in enumerate(ATTN_CONFIGS):
        out.append(a[g].reshape(r, T // r, gw).transpose(1, 0, 2).reshape(T, gw))
    return jnp.concatenate(out, axis=1)


def _block_diag(w):
    nblk, ng, a, b = w.shape
    eye = jnp.eye(ng, dtype=w.dtype)
    return (w[:, :, :, None, :] * eye[None, :, None, :, None]).reshape(nblk, ng * a, ng * b)


def _block_diag_part(m, a, b):
    nblk = m.shape[0]
    ng = m.shape[1] // a
    m5 = m.reshape(nblk, ng, a, ng, b)
    eye = jnp.eye(ng, dtype=m.dtype)
    return jnp.sum(m5 * eye[None, :, None, :, None], axis=3)


def _ssm_matrices(p, L, tag):
    G, P = p["ssm_lambda_re"].shape[1:]
    Hg = SSM_GROUP
    gpb = SSM_BLOCK_CH // Hg
    nblk = G // gpb
    br = p["ssm_b_re"][L].transpose(2, 0, 1)
    bi = p["ssm_b_im"][L].transpose(2, 0, 1)
    disc_in = (p["ssm_lambda_re"][L], p["ssm_lambda_im"][L], p["ssm_log_dt"][L].reshape(G, 1), br, bi)
    ar, ai, bbr, bbi = _ssm_disc(*disc_in, name=f"ssm_disc_{tag}")
    amat = jnp.concatenate([ar.reshape(nblk, gpb * P), ai.reshape(nblk, gpb * P)], axis=1)
    amat = jnp.broadcast_to(amat[:, None, :], (nblk, SUBLANES, 2 * gpb * P))
    bbr_g = bbr.transpose(1, 0, 2).reshape(nblk, gpb, Hg, P)
    bbi_g = bbi.transpose(1, 0, 2).reshape(nblk, gpb, Hg, P)
    bmat = jnp.concatenate([_block_diag(bbr_g), _block_diag(bbi_g)], axis=2).astype(BF16)
    cre = p["ssm_c_re"][L].reshape(nblk, gpb, Hg, P).transpose(0, 1, 3, 2)
    cim = p["ssm_c_im"][L].reshape(nblk, gpb, Hg, P).transpose(0, 1, 3, 2)
    cmat = jnp.concatenate([_block_diag(cre), -_block_diag(cim)], axis=1).astype(BF16)
    return disc_in, amat, bmat, cmat


def _layer_fwd(x, mem, p, wb, L, biasm):
    T, D = x.shape
    C = p["ssm_d"].shape[1]
    dm = wb["w_br_mem"].shape[1]
    tag = f"l{L}"
    s = {"x": x}
    h = _rmsnorm(x, p["norm_g"][L], f"norm_{tag}")
    offs = [int(o) for o in np.cumsum([0, 2 * C, 3 * 768, 768, 2 * dm, 3 * D])]
    names = ("uz", "qkv", "z_attn", "qz_mem", "logits")
    dts = (F32, F32, F32, F32, F32)
    for i, (nm, dt) in enumerate(zip(names, dts)):
        s[nm] = _matmul(h, wb["w_in"], mode="nn", name=f"in_{nm}_{tag}", out_dtype=dt, b_lead=L, b_off=offs[i],
                        n_cols=offs[i + 1] - offs[i])
    s["h"] = h

    disc_in, amat, bmat, cmat = _ssm_matrices(p, L, tag)
    u_seg = _to_segments(s["uz"][:, :C])
    dvec = p["ssm_d"][L].reshape(1, C)
    car = _ssm_carries(u_seg, None, bmat, cmat, amat, reverse=False, name=f"ssm_carry_{tag}")
    y_seg, xs = _ssm_scan(u_seg, bmat, cmat, amat, car, dvec, f"ssm_scan_{tag}")
    y = _from_segments(y_seg)
    o_ssm, a_glu = _ssm_post(y, s["uz"], wb["w_glu"][L], p["b_glu"][L], f"ssm_post_{tag}")
    s.update(disc_in=disc_in, amat=amat, bmat=bmat, cmat=cmat, u_seg=u_seg, xs=xs, y=y, a_glu=a_glu, o_ssm=o_ssm)

    groups = [_attn_group_fwd(s["qkv"], biasm, g, f"attn_g{g}_{tag}") for g in range(len(ATTN_CONFIGS))]
    os, lses = [o for o, _ in groups], [l for _, l in groups]
    o_attn = _attn_mix(os, lses, s["z_attn"], f"attn_mix_{tag}")
    s.update(os=os, lses=lses, o_attn=o_attn)

    mn = _rmsnorm(mem, p["mem_norm_g"][L], f"mem_norm_{tag}")
    kv = _matmul(mn, wb["w_mem_kv"], mode="nn", name=f"mem_kv_{tag}", out_dtype=BF16, b_lead=L)
    o_mem = _mem_attn(s["qz_mem"], kv, f"mem_attn_{tag}")
    s.update(mn=mn, kv=kv, o_mem=o_mem)

    bps = [_matmul(o, wb[n], mode="nn", name=f"br_{n}_{tag}", b_lead=L)
           for o, n in ((o_ssm, "w_br_ssm"), (o_attn, "w_br_attn"), (o_mem, "w_br_mem"))]
    merged = _merge(bps, s["logits"], p["b_gate"][L], f"merge_{tag}")
    s.update(bps=bps, merged=merged)
    x_new = _matmul(merged, wb["w_out"], mode="nn", name=f"out_{tag}", add=x, b_lead=L)
    return x_new, s


def _layer_bwd(dx, mem, p, wb, L, s, biasm, gprev):
    T, D = dx.shape
    C = p["ssm_d"].shape[1]
    depth = p["norm_g"].shape[0]
    tag = f"l{L}"
    g = {}

    def wgrad(n, a, b):
        g[n] = _matmul(a, b, mode="tn", name=f"d{n}_{tag}", out_dtype=BF16, stack=(L, depth, gprev.get(n)))

    dmerged = _matmul(dx, wb["w_out"], mode="nt", name=f"d_merged_{tag}", b_lead=L)
    wgrad("w_out", s["merged"], dx)
    dbp0, dbp1, dbp2, dlogits, g["b_gate"] = _merge_bwd(dmerged, s["bps"], s["logits"], p["b_gate"][L], f"merge_bwd_{tag}")
    dos = []
    for dbp, o, n in ((dbp0, s["o_ssm"], "w_br_ssm"), (dbp1, s["o_attn"], "w_br_attn"), (dbp2, s["o_mem"], "w_br_mem")):
        dos.append(_matmul(dbp, wb[n], mode="nt", name=f"d_o_{n}_{tag}", b_lead=L))
        wgrad(n, o, dbp)

    dy, dz_ssm, ds_glu, g["b_glu"] = _ssm_post_bwd(dos[0], s["y"], s["uz"], wb["w_glu"][L], p["b_glu"][L], f"ssm_post_bwd_{tag}")
    wgrad("w_glu", s["a_glu"], ds_glu)
    dy_seg = _to_segments(dy)
    dvec = p["ssm_d"][L].reshape(1, C)
    rcar = _ssm_carries(None, dy_seg, s["bmat"], s["cmat"], s["amat"], reverse=True, name=f"ssm_rcarry_{tag}")
    du_seg, dbm, dct, dam, g["ssm_d"] = _ssm_scan_bwd(dy_seg, s["u_seg"], s["xs"], s["bmat"], s["cmat"], s["amat"], rcar, dvec,
                                                      f"ssm_scan_bwd_{tag}")
    du = _from_segments(du_seg)
    G, P = p["ssm_lambda_re"].shape[1:]
    Hg = SSM_GROUP
    half = dbm.shape[2] // 2
    dbbr = _block_diag_part(dbm[:, :, :half], Hg, P).reshape(G, Hg, P).transpose(1, 0, 2)
    dbbi = _block_diag_part(dbm[:, :, half:], Hg, P).reshape(G, Hg, P).transpose(1, 0, 2)
    g["ssm_c_re"] = _block_diag_part(dct[:, :, :half], Hg, P).reshape(G, Hg, P)
    g["ssm_c_im"] = -_block_diag_part(dct[:, :, half:], Hg, P).reshape(G, Hg, P)
    dar = dam[:, 0, :half].reshape(G, P)
    dai = dam[:, 0, half:].reshape(G, P)
    glre, glim, gdt, gbr, gbi = _ssm_disc_bwd(*s["disc_in"], dar, dai, dbbr, dbbi, name=f"ssm_disc_bwd_{tag}")
    g["ssm_lambda_re"], g["ssm_lambda_im"], g["ssm_log_dt"] = glre, glim, gdt.reshape(G)
    g["ssm_b_re"] = gbr.transpose(1, 2, 0)
    g["ssm_b_im"] = gbi.transpose(1, 2, 0)

    dz_attn, do_g, dvec_g = _attn_mix_bwd(dos[1], s["os"], s["lses"], s["z_attn"], f"attn_mix_bwd_{tag}")
    back = [_attn_group_bwd(s["qkv"], do_g[g], dvec_g[g], s["lses"][g], biasm, g, f"attn_bwd_g{g}_{tag}")
            for g in range(len(ATTN_CONFIGS))]
    dqkv = [b[i].astype(BF16) for i in range(3) for b in back]
    dbias = jnp.stack([b[3] for b in back])

    dq_mem, dz_mem, dk_mem, dv_mem = _mem_attn_bwd(dos[2], s["qz_mem"], s["kv"], f"mem_attn_bwd_{tag}")
    dkv = jnp.concatenate([dk_mem, dv_mem], axis=1)
    wgrad("w_mem_kv", s["mn"], dkv)
    dmn = _matmul(dkv, wb["w_mem_kv"], mode="nt", name=f"d_mn_{tag}", b_lead=L)
    _, g["mem_norm_g"] = _rmsnorm_bwd(mem, p["mem_norm_g"][L], dmn, None, f"mem_norm_bwd_{tag}")

    dproj = jnp.concatenate([du.astype(BF16), dz_ssm] + dqkv + [dz_attn, dq_mem, dz_mem, dlogits], axis=1)
    dh = _matmul(dproj, wb["w_in"], mode="nt", name=f"d_h_{tag}", b_lead=L)
    wgrad("w_in", s["h"], dproj)
    dx_in, g["norm_g"] = _rmsnorm_bwd(s["x"], p["norm_g"][L], dh, dx, f"norm_bwd_{tag}")
    return dx_in, g, dbias


def _bucket_onehot(gi):
    buckets, bands = _band_tables()
    hit = (buckets[gi].reshape(1, -1) == jnp.arange(NUM_BUCKETS)[:, None]) & bands[gi].reshape(1, -1)
    return hit.astype(BF16)


def _bias_tables(rel_bias, name):
    _, bands = _band_tables()
    out = []
    for gi in range(len(ATTN_CONFIGS)):
        tab = rel_bias[:, gi * HEADS_PER_GROUP:(gi + 1) * HEADS_PER_GROUP].T
        flat = _matmul(tab, _bucket_onehot(gi), mode="nn", name=f"{name}_{gi}", split_a=3, tn=4096)
        out.append(jnp.where(bands[gi][None], flat.reshape(HEADS_PER_GROUP, ATTN_BLOCK, 2 * ATTN_BLOCK), NEG_INF))
    return jnp.stack(out)


def _rel_bias_grad(dbias_sum, name):
    cols = []
    for gi in range(len(ATTN_CONFIGS)):
        flat = dbias_sum[gi].reshape(HEADS_PER_GROUP, -1)
        cols.append(_matmul(flat, _bucket_onehot(gi), mode="nt", name=f"{name}_{gi}", split_a=2, tk=4096).T)
    return jnp.concatenate(cols, axis=1)


def _local_step(x, mem, target, p, wb):
    depth = p["norm_g"].shape[0]
    biasm = _bias_tables(p["rel_bias"], "bias_table")
    saved = []
    for L in range(depth):
        x, s = _layer_fwd(x, mem, p, wb, L, biasm)
        saved.append(s)
    loss_vec, dx, dgf = _loss_head(x, p["final_norm_g"], target, "loss_head")
    grads = {"final_norm_g": dgf.reshape(-1)}
    per_layer = [None] * depth
    dbias_sum = 0.0
    stacked = {}
    for L in reversed(range(depth)):
        dx, per_layer[L], dbias = _layer_bwd(dx, mem, p, wb, L, saved[L], biasm, stacked)
        stacked = {n: per_layer[L][n] for n, _ in BIG}
        dbias_sum = dbias_sum + dbias
    grads.update(stacked)
    for n in per_layer[0]:
        if n not in stacked:
            grads[n] = jnp.stack([per_layer[L][n].reshape(p[n].shape[1:]) for L in range(depth)])
    grads["rel_bias"] = _rel_bias_grad(dbias_sum, "d_rel_bias")
    return jnp.sum(loss_vec), dx, grads


def _chip_coords(j):
    return j // 2, j % 2


def _place_shard(shard, ax, chip, name):
    _, a, b = shard.shape
    ra = _pick(a, 256, 16)
    full = (2, a * N_CHIPS, b) if ax == 1 else (2, a, b * N_CHIPS)
    per = a // ra

    def body(j_ref, s_ref, o_ref):
        o_ref[...] = s_ref[...].astype(BF16)

    out_idx = (lambda l, i, j: (l, j[0] * per + i, 0)) if ax == 1 else (lambda l, i, j: (l, i, j[0]))
    return pl.pallas_call(
        body, name=name,
        grid_spec=pltpu.PrefetchScalarGridSpec(
            num_scalar_prefetch=1, grid=(2, per),
            in_specs=[pl.BlockSpec((None, ra, b), lambda l, i, j: (l, i, 0))],
            out_specs=pl.BlockSpec((None, ra, b), out_idx)),
        out_shape=jax.ShapeDtypeStruct(full, BF16), compiler_params=_params("parallel", "parallel"),
    )(chip, shard)


def _gather_shards(fulls, axes, name):
    n = len(fulls)
    widths = [a.shape[ax] // N_CHIPS for a, ax in zip(fulls, axes)]
    aligns = [LANES if ax == 2 else 16 for ax in axes]

    def body(*refs):
        outs = refs[n:2 * n]
        send_sems, recv_sems, fsend_sems, frecv_sems = refs[2 * n:]
        x, y, c = lax.axis_index("x"), lax.axis_index("y"), lax.axis_index("c")
        mine = 2 * x + y
        sibling = (x, y, 1 - c)

        def window(t, layer, j):
            start = pl.ds(pl.multiple_of(j * widths[t], aligns[t]), widths[t])
            return outs[t].at[(layer, start, slice(None)) if axes[t] == 1 else (layer, slice(None), start)]

        def over_ici(t, j, block):
            return pltpu.make_async_remote_copy(
                src_ref=window(t, c, mine), dst_ref=window(t, c, block), send_sem=send_sems.at[t, j],
                recv_sem=recv_sems.at[t, block], device_id=(*_chip_coords(j), c), device_id_type=MESH)

        def over_d2d(t, j, layer):
            return pltpu.make_async_remote_copy(
                src_ref=window(t, layer, j), dst_ref=window(t, layer, j), send_sem=fsend_sems.at[t, j],
                recv_sem=frecv_sems.at[t, j], device_id=sibling, device_id_type=MESH)

        for t in range(n):
            for j in range(N_CHIPS):
                @pl.when(j != mine)
                def _():
                    over_ici(t, j, mine).start()
        for t in range(n):
            for j in range(N_CHIPS):
                @pl.when(j != mine)
                def _():
                    over_ici(t, j, j).wait_recv()
                    over_d2d(t, j, c).start()
        for t in range(n):
            for j in range(N_CHIPS):
                @pl.when(j != mine)
                def _():
                    over_ici(t, j, mine).wait_send()
                    over_d2d(t, j, c).wait_send()
                    over_d2d(t, j, 1 - c).wait_recv()

    sem = pltpu.SemaphoreType.DMA
    return pl.pallas_call(
        body, name=name, in_specs=[HBM] * n, out_specs=[HBM] * n,
        out_shape=[jax.ShapeDtypeStruct(a.shape, a.dtype) for a in fulls],
        input_output_aliases={t: t for t in range(n)},
        scratch_shapes=[sem((n, N_CHIPS)), sem((n, N_CHIPS)), sem((n, N_CHIPS)), sem((n, N_CHIPS))],
    )(*fulls)


def _scatter_slices(arrays, axes, name):
    n = len(arrays)

    def piece(a, ax):
        if ax is None:
            return a.shape, None
        w = a.shape[ax] // N_CHIPS
        return a.shape[:ax] + (w,) + a.shape[ax + 1:], w

    shapes = [piece(a, ax) for a, ax in zip(arrays, axes)]

    def body(*refs):
        ins, outs = refs[:n], refs[n:2 * n]
        send_sems, recv_sems, loc_sems = refs[2 * n:]
        x, y, c = lax.axis_index("x"), lax.axis_index("y"), lax.axis_index("c")
        mine = 2 * x + y

        def src(t, j):
            ax, w = axes[t], shapes[t][1]
            if ax is None:
                return ins[t]
            idx = tuple(pl.ds(j * w, w) if d == ax else slice(None) for d in range(len(arrays[t].shape)))
            return ins[t].at[idx]

        for t in range(n):
            for j in range(N_CHIPS):
                @pl.when(j == mine)
                def _():
                    pltpu.make_async_copy(src(t, j), outs[t].at[j], loc_sems.at[t]).start()

                @pl.when(j != mine)
                def _():
                    pltpu.make_async_remote_copy(
                        src_ref=src(t, j), dst_ref=outs[t].at[mine], send_sem=send_sems.at[t, j], recv_sem=recv_sems.at[t, mine],
                        device_id=(*_chip_coords(j), c), device_id_type=MESH).start()
        for t in range(n):
            for j in range(N_CHIPS):
                @pl.when(j == mine)
                def _():
                    pltpu.make_async_copy(src(t, j), outs[t].at[j], loc_sems.at[t]).wait()

                @pl.when(j != mine)
                def _():
                    cp = pltpu.make_async_remote_copy(
                        src_ref=src(t, j), dst_ref=outs[t].at[j], send_sem=send_sems.at[t, j], recv_sem=recv_sems.at[t, j],
                        device_id=(*_chip_coords(j), c), device_id_type=MESH)
                    cp.wait_send()
                    cp.wait_recv()

    return pl.pallas_call(
        body, name=name, in_specs=[HBM] * n, out_specs=[HBM] * n,
        out_shape=[jax.ShapeDtypeStruct((N_CHIPS,) + sh, a.dtype) for a, (sh, _) in zip(arrays, shapes)],
        scratch_shapes=[pltpu.SemaphoreType.DMA((n, N_CHIPS)), pltpu.SemaphoreType.DMA((n, N_CHIPS)), pltpu.SemaphoreType.DMA((n,))],
    )(*arrays)


def _swap_layers(stacked, name):
    n = len(stacked)

    def body(*refs):
        ins, outs = refs[:n], refs[n:2 * n]
        send_sems, recv_sems = refs[2 * n:]
        c = lax.axis_index("c")
        peer = (lax.axis_index("x"), lax.axis_index("y"), 1 - c)
        cps = [pltpu.make_async_remote_copy(src_ref=ins[t].at[1 - c], dst_ref=outs[t], send_sem=send_sems.at[t],
                                            recv_sem=recv_sems.at[t], device_id=peer, device_id_type=MESH) for t in range(n)]
        for cp in cps:
            cp.start()
        for cp in cps:
            cp.wait_send()
            cp.wait_recv()

    return pl.pallas_call(
        body, name=name, in_specs=[HBM] * n, out_specs=[HBM] * n,
        out_shape=[jax.ShapeDtypeStruct(a.shape[1:], a.dtype) for a in stacked],
        scratch_shapes=[pltpu.SemaphoreType.DMA((n,)), pltpu.SemaphoreType.DMA((n,))],
    )(*stacked)


def _merge_layers(stacked, name):
    n = len(stacked)

    def body(*refs):
        outs = refs[n:2 * n]
        send_sems, recv_sems = refs[2 * n:]
        c = lax.axis_index("c")
        peer = (lax.axis_index("x"), lax.axis_index("y"), 1 - c)
        for t in range(n):
            pltpu.make_async_remote_copy(src_ref=outs[t].at[c], dst_ref=outs[t].at[c], send_sem=send_sems.at[t],
                                         recv_sem=recv_sems.at[t], device_id=peer, device_id_type=MESH).start()
        for t in range(n):
            cp = pltpu.make_async_remote_copy(src_ref=outs[t].at[c], dst_ref=outs[t].at[1 - c], send_sem=send_sems.at[t],
                                              recv_sem=recv_sems.at[t], device_id=peer, device_id_type=MESH)
            cp.wait_send()
            cp.wait_recv()

    sem = pltpu.SemaphoreType.DMA
    return pl.pallas_call(
        body, name=name, in_specs=[HBM] * n, out_specs=[HBM] * n,
        out_shape=[jax.ShapeDtypeStruct(a.shape, a.dtype) for a in stacked],
        input_output_aliases={t: t for t in range(n)}, scratch_shapes=[sem((n,)), sem((n,))],
    )(*stacked)


def _pair_sum(stacked, landed, core, name):
    _, K, N = stacked.shape
    tr = _pick(K, max(16, (1 << 19) // N // 16 * 16), 16)

    def body(c_ref, s_ref, l_ref, o_ref):
        o_ref[...] = (s_ref[...].astype(F32) + l_ref[...].astype(F32)).astype(o_ref.dtype)

    return pl.pallas_call(
        body, name=name,
        grid_spec=pltpu.PrefetchScalarGridSpec(
            num_scalar_prefetch=1, grid=(K // tr,),
            in_specs=[pl.BlockSpec((None, tr, N), lambda i, c: (c[0], i, 0)), pl.BlockSpec((tr, N), lambda i, c: (i, 0))],
            out_specs=pl.BlockSpec((tr, N), lambda i, c: (i, 0))),
        out_shape=jax.ShapeDtypeStruct((K, N), stacked.dtype), compiler_params=_params("parallel"),
    )(core, stacked, landed)


def _sum_chips(landed, core, name):
    _, R, C = landed.shape
    tr = _pick(R, max(SUBLANES, (1 << 19) // C // 16 * 16), 16)

    def body(c_ref, l_ref, o_ref):
        acc = l_ref[0].astype(F32) + l_ref[1].astype(F32)
        acc = acc + l_ref[2].astype(F32)
        o_ref[...] = acc + l_ref[3].astype(F32)

    return pl.pallas_call(
        body, name=name,
        grid_spec=pltpu.PrefetchScalarGridSpec(
            num_scalar_prefetch=1, grid=(R // tr,),
            in_specs=[pl.BlockSpec((N_CHIPS, tr, C), lambda i, c: (0, i, 0))],
            out_specs=pl.BlockSpec((None, tr, C), lambda i, c: (c[0], i, 0))),
        out_shape=jax.ShapeDtypeStruct((2, R, C), F32), compiler_params=_params("parallel"),
    )(core, landed)


def _adamw(w, g, m, v, name):
    R, C = w.shape
    tr = _pick(R, max(SUBLANES, (1 << 18) // C // 8 * 8), SUBLANES)
    c1 = 1.0 / (1.0 - ADAM_B1 ** ADAM_STEP)
    c2 = 1.0 / (1.0 - ADAM_B2 ** ADAM_STEP)

    def body(w_ref, g_ref, m_ref, v_ref, d_ref, nm_ref, nv_ref):
        g = g_ref[...]
        nm = ADAM_B1 * m_ref[...] + (1.0 - ADAM_B1) * g
        nv = ADAM_B2 * v_ref[...] + (1.0 - ADAM_B2) * (g * g)
        nm_ref[...] = nm
        nv_ref[...] = nv
        d_ref[...] = -ADAM_LR * ((nm * c1) / (jnp.sqrt(nv * c2) + ADAM_EPS) + ADAM_WD * w_ref[...])

    blk = pl.BlockSpec((tr, C), lambda i: (i, 0))
    return pl.pallas_call(
        body, name=name, grid=(R // tr,), in_specs=[blk] * 4, out_specs=[blk] * 3,
        out_shape=[jax.ShapeDtypeStruct((R, C), F32)] * 3, compiler_params=_params("parallel"),
    )(w, g, m, v)


def _pack_small(d, prefix=""):
    flat = jnp.concatenate([d[prefix + n].astype(F32).reshape(-1) for n in SMALL])
    pad = (-flat.shape[0]) % (2 * 16 * LANES)
    return jnp.pad(flat, (0, pad)).reshape(-1, LANES)


def _unpack_small(packed, shapes):
    flat = packed.reshape(-1)
    out, off = {}, 0
    for n in SMALL:
        size = int(np.prod(shapes[n]))
        out[n] = flat[off:off + size].reshape(shapes[n])
        off += size
    return out


def kernel(*args):
    p = dict(zip(INPUTS, args))
    x, mem, target = p["x"][0], p["mem"][0], p["loss_target"][0]

    names = [n for n, _ in BIG] + ["small"]
    core = lax.axis_index("c").astype(jnp.int32).reshape(1)
    chip = (2 * lax.axis_index("x") + lax.axis_index("y")).astype(jnp.int32).reshape(1)
    placed = [_place_shard(p[n], ax, chip, f"place_{n}") for n, ax in BIG]
    wb = dict(zip(names, _gather_shards(placed, [ax for _, ax in BIG], "gather_weights")))

    loss_part, dx, grads = _local_step(x, mem, target, p, wb)
    loss = lax.psum(loss_part, ("x", "y", "c"))

    stacked = [grads[n] for n, _ in BIG] + [_pack_small(grads).reshape(2, -1, LANES)]
    theirs = _swap_layers(stacked, "swap_layers")
    pair = [_pair_sum(s, o, core, f"pair_sum_{n}") for n, s, o in zip(names, stacked, theirs)]
    landed = _scatter_slices(pair, [ax - 1 for _, ax in BIG] + [None], "scatter_grads")
    reduced = [_sum_chips(ld.reshape(N_CHIPS, -1, ld.shape[-1]), core, f"sum_chips_{n}") for n, ld in zip(names, landed)]
    total = _merge_layers(reduced, "merge_layers")

    out = {}
    for (n, _), g in zip(BIG, total):
        sh = p[n].shape
        two_d = lambda a: a.reshape(-1, sh[-1])
        res = (g,) + tuple(_adamw(two_d(p[n]), two_d(g), two_d(p["m_" + n]), two_d(p["v_" + n]), f"adamw_{n}"))
        for key, r in zip(("grad_", "delta_", "new_m_", "new_v_"), res):
            out[key + n] = r.reshape(sh)
    g = total[-1].reshape(-1, LANES)
    res = (g,) + tuple(_adamw(_pack_small(p), g, _pack_small(p, "m_"), _pack_small(p, "v_"), "adamw_small"))
    shapes = {n: p[n].shape for n in SMALL}
    for key, r in zip(("grad_", "delta_", "new_m_", "new_v_"), res):
        for n, a in _unpack_small(r, shapes).items():
            out[key + n] = a

    result = [loss, dx.reshape(p["x"].shape)]
    for key in ("grad_", "delta_", "new_m_", "new_v_"):
        result += [out[key + n] for n in WEIGHTS]
    return tuple(result)
```

```python
import math

import jax
import jax.numpy as jnp
import numpy as np
from jax import lax
from jax.experimental import pallas as pl
from jax.experimental.pallas import tpu as pltpu

F32 = jnp.float32
BF16 = jnp.bfloat16
MESH = pl.DeviceIdType.MESH
HBM = pl.BlockSpec(memory_space=pltpu.HBM)

EPS = 1e-6
SSM_GROUP = 16
SSM_STATE = 64
ATTN_HEAD_DIM = 64
HEADS_PER_GROUP = 4
ATTN_CONFIGS = ((128, 1), (512, 4), (2048, 16))
ATTN_BLOCK = 128
NUM_BUCKETS = 32
REL_MAX_DISTANCE = 2048
NEG_INF = -1e30
MEM_HEADS = 4
ADAM_LR = 0.001
ADAM_B1 = 0.9
ADAM_B2 = 0.999
ADAM_EPS = 1e-08
ADAM_WD = 0.01
ADAM_STEP = 10

LANES = 128
SUBLANES = 8
VMEM_LIMIT_BYTES = 48 * 1024 * 1024
SSM_BLOCK_CH = 128
SSM_SEGMENTS = SUBLANES
SSM_CHUNK_STEPS = 128

N_CHIPS = 4
BIG = (("w_in", 2), ("w_glu", 1), ("w_mem_kv", 1), ("w_br_ssm", 2), ("w_br_attn", 2), ("w_br_mem", 2), ("w_out", 1))
SMALL = ("norm_g", "mem_norm_g", "b_gate", "ssm_lambda_re", "ssm_lambda_im", "ssm_log_dt", "ssm_b_re", "ssm_b_im",
         "ssm_c_re", "ssm_c_im", "ssm_d", "b_glu", "rel_bias", "final_norm_g")
WEIGHTS = ("norm_g", "mem_norm_g", "w_in", "b_gate", "ssm_lambda_re", "ssm_lambda_im", "ssm_log_dt", "ssm_b_re",
           "ssm_b_im", "ssm_c_re", "ssm_c_im", "ssm_d", "w_glu", "b_glu", "w_mem_kv", "w_br_ssm", "w_br_attn",
           "w_br_mem", "w_out", "rel_bias", "final_norm_g")
INPUTS = ("x", "mem") + WEIGHTS + ("loss_target",) + tuple("m_" + n for n in WEIGHTS) + tuple("v_" + n for n in WEIGHTS)


def _params(*sem):
    return pltpu.CompilerParams(dimension_semantics=sem, vmem_limit_bytes=VMEM_LIMIT_BYTES)


def _pick(dim, pref, align):
    if dim <= pref:
        return dim
    t = pref - pref % align
    while t >= align:
        if dim % t == 0:
            return t
        t -= align
    return dim


def _sigmoid(v):
    return 1.0 / (1.0 + jnp.exp(-v))


def _silu_and_grad(z):
    s = _sigmoid(z)
    return z * s, s * (1.0 + z * (1.0 - s))


_GELU_C = math.sqrt(2.0 / math.pi)


def _gelu_and_grad(y):
    inner = _GELU_C * (y + 0.044715 * y * y * y)
    t = jnp.tanh(inner)
    g = 0.5 * y * (1.0 + t)
    dg = 0.5 * (1.0 + t) + 0.5 * y * (1.0 - t * t) * _GELU_C * (1.0 + 3.0 * 0.044715 * y * y)
    return g, dg


def _dot(a, b, dims):
    return lax.dot_general(a, b, (dims, ((), ())), preferred_element_type=F32)


NN = ((1,), (0,))
NT = ((1,), (1,))
TN = ((0,), (0,))


def _matmul(a, b, *, mode, name, out_dtype=F32, add=None, split_a=1, tm=1024, tn=768, tk=2304,
            b_lead=None, b_off=0, n_cols=None, stack=None):
    if mode == "tn":
        K, M = a.shape
    else:
        M, K = a.shape
    bshape = b.shape if b_lead is None else b.shape[1:]
    N = n_cols or (bshape[0] if mode == "nt" else bshape[1])
    if mode != "tn" and M >= 4 * tm:
        tm = 2 * tm
    tm = _pick(M, tm, LANES if mode == "tn" else SUBLANES)
    tn = _pick(math.gcd(N, b_off) if b_off else N, tn, LANES)
    tk = _pick(K, tk, LANES)
    nk = K // tk
    joff = b_off // tn
    dims = {"nn": NN, "nt": NT, "tn": TN}[mode]
    has_add = add is not None
    has_prev = stack is not None and stack[2] is not None

    def body(*refs):
        a_ref, b_ref = refs[:2]
        add_ref = refs[2] if has_add else None
        o_ref = refs[-2] if nk > 1 else refs[-1]
        k = pl.program_id(2)
        bv = b_ref[...].astype(BF16)
        if split_a > 1:
            rest = a_ref[...].astype(F32)
            part = 0.0
            for _ in range(split_a):
                piece = rest.astype(BF16)
                part = part + _dot(piece, bv, dims)
                rest = rest - piece.astype(F32)
        else:
            part = _dot(a_ref[...].astype(BF16), bv, dims)

        def finish(r):
            if has_add:
                r = r + add_ref[...]
            o_ref[...] = r.astype(out_dtype)

        if nk == 1:
            finish(part)
            return
        acc_ref = refs[-1]

        @pl.when(k == 0)
        def _():
            acc_ref[...] = part

        @pl.when((k > 0) & (k < nk - 1))
        def _():
            acc_ref[...] += part

        @pl.when(k == nk - 1)
        def _():
            finish(acc_ref[...] + part)

    a_spec = pl.BlockSpec((tk, tm), lambda i, j, k: (k, i)) if mode == "tn" else pl.BlockSpec((tm, tk), lambda i, j, k: (i, k))
    lead = () if b_lead is None else (b_lead,)
    lead_blk = () if b_lead is None else (None,)
    if mode == "nt":
        b_spec = pl.BlockSpec(lead_blk + (tn, tk), lambda i, j, k: lead + (j + joff, k))
    else:
        b_spec = pl.BlockSpec(lead_blk + (tk, tn), lambda i, j, k: lead + (k, j + joff))
    in_specs = [a_spec, b_spec]
    args = [a, b]
    if has_add:
        in_specs.append(pl.BlockSpec((tm, tn), lambda i, j, k: (i, j)))
        args.append(add)
    aliases = {}
    if stack is None:
        out_spec = pl.BlockSpec((tm, tn), lambda i, j, k: (i, j))
        out_shape = jax.ShapeDtypeStruct((M, N), out_dtype)
    else:
        layer, depth, prev = stack
        out_spec = pl.BlockSpec((None, tm, tn), lambda i, j, k: (layer, i, j))
        out_shape = jax.ShapeDtypeStruct((depth, M, N), out_dtype)
        if has_prev:
            in_specs.append(pl.BlockSpec(memory_space=pl.ANY))
            args.append(prev)
            aliases = {len(args) - 1: 0}
    return pl.pallas_call(
        body, name=name, grid=(M // tm, N // tn, nk), in_specs=in_specs, out_specs=out_spec, out_shape=out_shape,
        scratch_shapes=[pltpu.VMEM((tm, tn), F32)] if nk > 1 else [], input_output_aliases=aliases,
        compiler_params=_params("parallel", "parallel", "arbitrary"),
    )(*args)


def _rmsnorm(x, g, name):
    T, D = x.shape
    tm = _pick(T, 512, SUBLANES)

    def body(x_ref, g_ref, h_ref):
        xv = x_ref[...]
        r = lax.rsqrt(jnp.mean(xv * xv, axis=-1, keepdims=True) + EPS)
        h_ref[...] = (xv * r * g_ref[...]).astype(BF16)

    return pl.pallas_call(
        body, name=name, grid=(T // tm,),
        in_specs=[pl.BlockSpec((tm, D), lambda i: (i, 0)), pl.BlockSpec((1, D), lambda i: (0, 0))],
        out_specs=pl.BlockSpec((tm, D), lambda i: (i, 0)),
        out_shape=jax.ShapeDtypeStruct((T, D), BF16), compiler_params=_params("parallel"),
    )(x, g.reshape(1, D))


def _rmsnorm_bwd(x, g, dh, dres, name):
    T, D = x.shape
    tm = _pick(T, 512, SUBLANES)
    with_res = dres is not None

    def body(*refs):
        if with_res:
            x_ref, g_ref, dh_ref, dres_ref, dx_ref, dg_ref = refs
        else:
            x_ref, g_ref, dh_ref, dx_ref, dg_ref = refs
        xv = x_ref[...]
        dhv = dh_ref[...]
        r = lax.rsqrt(jnp.mean(xv * xv, axis=-1, keepdims=True) + EPS)
        dyg = dhv * g_ref[...]
        c = jnp.mean(dyg * xv, axis=-1, keepdims=True)
        dx = r * dyg - xv * (r * r * r) * c
        if with_res:
            dx = dx + dres_ref[...]
        dx_ref[...] = dx

        @pl.when(pl.program_id(0) == 0)
        def _():
            dg_ref[...] = jnp.zeros_like(dg_ref)

        dg_ref[...] += jnp.sum(dhv * xv * r, axis=0, keepdims=True)

    row = pl.BlockSpec((tm, D), lambda i: (i, 0))
    vec = pl.BlockSpec((1, D), lambda i: (0, 0))
    ins = [x, g.reshape(1, D), dh] + ([dres] if with_res else [])
    return pl.pallas_call(
        body, name=name, grid=(T // tm,), in_specs=[row, vec, row] + ([row] if with_res else []),
        out_specs=[row, vec],
        out_shape=[jax.ShapeDtypeStruct((T, D), F32), jax.ShapeDtypeStruct((1, D), F32)],
        compiler_params=_params("arbitrary"),
    )(*ins)


def _loss_head(x, g, target, name):
    T, D = x.shape
    tm = _pick(T, 512, SUBLANES)

    def body(x_ref, g_ref, t_ref, loss_ref, dx_ref, dg_ref):
        xv = x_ref[...]
        gv = g_ref[...]
        r = lax.rsqrt(jnp.mean(xv * xv, axis=-1, keepdims=True) + EPS)
        e = xv * r * gv - t_ref[...]
        dy = e * (1.0 / D)
        dyg = dy * gv
        c = jnp.mean(dyg * xv, axis=-1, keepdims=True)
        dx_ref[...] = r * dyg - xv * (r * r * r) * c

        @pl.when(pl.program_id(0) == 0)
        def _():
            loss_ref[...] = jnp.zeros_like(loss_ref)
            dg_ref[...] = jnp.zeros_like(dg_ref)

        loss_ref[...] += jnp.sum(e * e, axis=0, keepdims=True) * (0.5 / D)
        dg_ref[...] += jnp.sum(dy * xv * r, axis=0, keepdims=True)

    row = pl.BlockSpec((tm, D), lambda i: (i, 0))
    vec = pl.BlockSpec((1, D), lambda i: (0, 0))
    return pl.pallas_call(
        body, name=name, grid=(T // tm,), in_specs=[row, vec, row], out_specs=[vec, row, vec],
        out_shape=[jax.ShapeDtypeStruct((1, D), F32), jax.ShapeDtypeStruct((T, D), F32), jax.ShapeDtypeStruct((1, D), F32)],
        compiler_params=_params("arbitrary"),
    )(x, g.reshape(1, D), target)


def _ssm_disc_math(lre, lim, logdt, br, bi):
    dt = jnp.exp(logdt)
    mag = jnp.exp(lre * dt)
    ar = mag * jnp.cos(lim * dt)
    ai = mag * jnp.sin(lim * dt)
    den = lre * lre + lim * lim
    nr = ar - 1.0
    fr = (nr * lre + ai * lim) / den
    fi = (ai * lre - nr * lim) / den
    return ar, ai, fr[None] * br - fi[None] * bi, fr[None] * bi + fi[None] * br


def _ssm_disc(lre, lim, logdt, br, bi, name):
    def body(lre_ref, lim_ref, dt_ref, br_ref, bi_ref, ar_ref, ai_ref, bbr_ref, bbi_ref):
        ar, ai, bbr, bbi = _ssm_disc_math(lre_ref[...], lim_ref[...], dt_ref[...], br_ref[...], bi_ref[...])
        ar_ref[...] = ar
        ai_ref[...] = ai
        bbr_ref[...] = bbr
        bbi_ref[...] = bbi

    sd = jax.ShapeDtypeStruct
    return pl.pallas_call(
        body, name=name, out_shape=[sd(lre.shape, F32), sd(lre.shape, F32), sd(br.shape, F32), sd(br.shape, F32)],
    )(lre, lim, logdt, br, bi)


def _ssm_disc_bwd(lre, lim, logdt, br, bi, dar, dai, dbbr, dbbi, name):
    def body(lre_ref, lim_ref, dt_ref, br_ref, bi_ref, dar_ref, dai_ref, dbbr_ref, dbbi_ref,
             glre_ref, glim_ref, gdt_ref, gbr_ref, gbi_ref):
        _, vjp = jax.vjp(_ssm_disc_math, lre_ref[...], lim_ref[...], dt_ref[...], br_ref[...], bi_ref[...])
        glre, glim, gdt, gbr, gbi = vjp((dar_ref[...], dai_ref[...], dbbr_ref[...], dbbi_ref[...]))
        glre_ref[...] = glre
        glim_ref[...] = glim
        gdt_ref[...] = gdt
        gbr_ref[...] = gbr
        gbi_ref[...] = gbi

    sd = jax.ShapeDtypeStruct
    return pl.pallas_call(
        body, name=name,
        out_shape=[sd(lre.shape, F32), sd(lre.shape, F32), sd(logdt.shape, F32), sd(br.shape, F32), sd(br.shape, F32)],
    )(lre, lim, logdt, br, bi, dar, dai, dbbr, dbbi)


def _shift_segments(v, down):
    n = v.shape[0]
    rows = lax.broadcasted_iota(jnp.int32, v.shape, 0)
    if down:
        return jnp.where(rows >= 1, pltpu.roll(v, 1, 0), 0.0)
    return jnp.where(rows < n - 1, pltpu.roll(v, n - 1, 0), 0.0)


def _cpow(ar, ai, n):
    rr, ri = None, None
    pr, pi = ar, ai
    while n:
        if n & 1:
            if rr is None:
                rr, ri = pr, pi
            else:
                rr, ri = rr * pr - ri * pi, rr * pi + ri * pr
        n >>= 1
        if n:
            pr, pi = pr * pr - pi * pi, 2.0 * pr * pi
    return rr, ri


def _ssm_geometry(T, C):
    seg_steps = T // SSM_SEGMENTS
    kc = min(SSM_CHUNK_STEPS, seg_steps)
    return C // SSM_BLOCK_CH, seg_steps, kc, seg_steps // kc, SSM_SEGMENTS * kc


def _ssm_carries(u, dy, bmat, cmat, amat, *, reverse, name):
    src = dy if reverse else u
    T, C = src.shape
    nblk, seg_steps, kc, nchunk, rc = _ssm_geometry(T, C)
    half = bmat.shape[2] // 2

    def body(src_ref, w_ref, a_ref, out_ref, buf_ref, st_ref):
        c = pl.program_id(1)

        @pl.when(c == 0)
        def _():
            st_ref[...] = jnp.zeros_like(st_ref)

        if reverse:
            buf_ref[...] = _dot(src_ref[...].astype(BF16), w_ref[0], NT)
        else:
            buf_ref[...] = _dot(src_ref[...].astype(BF16), w_ref[0], NN)
        ar = a_ref[0, :, :half]
        ai = a_ref[0, :, half:]
        if reverse:
            ai = -ai

        def step(i, carry):
            xr, xi = carry
            k = (kc - 1 - i) if reverse else i
            row = pl.multiple_of(k * SUBLANES, SUBLANES)
            br = buf_ref[pl.ds(row, SUBLANES), :half]
            bi = buf_ref[pl.ds(row, SUBLANES), half:]
            return ar * xr - ai * xi + br, ar * xi + ai * xr + bi

        xr, xi = lax.fori_loop(0, kc, step, (st_ref[:, :half], st_ref[:, half:]), unroll=8)
        st_ref[:, :half] = xr
        st_ref[:, half:] = xi

        @pl.when(c == nchunk - 1)
        def _():
            pr, pi = _cpow(ar, ai, seg_steps)
            sr = jnp.zeros_like(xr)
            si = jnp.zeros_like(xi)
            for _ in range(SSM_SEGMENTS - 1):
                nr = xr + pr * sr - pi * si
                ni = xi + pr * si + pi * sr
                sr = _shift_segments(nr, not reverse)
                si = _shift_segments(ni, not reverse)
            out_ref[0, :, :half] = sr
            out_ref[0, :, half:] = si

    cidx = (lambda b, c: (nchunk - 1 - c, b)) if reverse else (lambda b, c: (c, b))
    w = cmat if reverse else bmat
    return pl.pallas_call(
        body, name=name, grid=(nblk, nchunk),
        in_specs=[pl.BlockSpec((rc, SSM_BLOCK_CH), cidx),
                  pl.BlockSpec((1,) + w.shape[1:], lambda b, c: (b, 0, 0)),
                  pl.BlockSpec((1, SUBLANES, 2 * half), lambda b, c: (b, 0, 0))],
        out_specs=pl.BlockSpec((1, SUBLANES, 2 * half), lambda b, c: (b, 0, 0)),
        out_shape=jax.ShapeDtypeStruct((nblk, SUBLANES, 2 * half), F32),
        scratch_shapes=[pltpu.VMEM((rc, 2 * half), F32), pltpu.VMEM((SUBLANES, 2 * half), F32)],
        compiler_params=_params("parallel", "arbitrary"),
    )(src, w, amat)


def _ssm_scan(u, bmat, cmat, amat, carries, dvec, name):
    T, C = u.shape
    nblk, seg_steps, kc, nchunk, rc = _ssm_geometry(T, C)
    half = bmat.shape[2] // 2

    def body(u_ref, b_ref, c_ref, a_ref, s_ref, d_ref, y_ref, x_ref, st_ref):
        c = pl.program_id(1)

        @pl.when(c == 0)
        def _():
            st_ref[...] = s_ref[0]

        uv = u_ref[...]
        x_ref[...] = _dot(uv.astype(BF16), b_ref[0], NN)
        ar = a_ref[0, :, :half]
        ai = a_ref[0, :, half:]

        def step(k, carry):
            xr, xi = carry
            row = pl.multiple_of(k * SUBLANES, SUBLANES)
            nr = ar * xr - ai * xi + x_ref[pl.ds(row, SUBLANES), :half]
            ni = ar * xi + ai * xr + x_ref[pl.ds(row, SUBLANES), half:]
            x_ref[pl.ds(row, SUBLANES), :half] = nr
            x_ref[pl.ds(row, SUBLANES), half:] = ni
            return nr, ni

        xr, xi = lax.fori_loop(0, kc, step, (st_ref[:, :half], st_ref[:, half:]), unroll=8)
        st_ref[:, :half] = xr
        st_ref[:, half:] = xi
        y_ref[...] = _dot(x_ref[...].astype(BF16), c_ref[0], NN) + d_ref[...] * uv

    return pl.pallas_call(
        body, name=name, grid=(nblk, nchunk),
        in_specs=[pl.BlockSpec((rc, SSM_BLOCK_CH), lambda b, c: (c, b)),
                  pl.BlockSpec((1,) + bmat.shape[1:], lambda b, c: (b, 0, 0)),
                  pl.BlockSpec((1,) + cmat.shape[1:], lambda b, c: (b, 0, 0)),
                  pl.BlockSpec((1, SUBLANES, 2 * half), lambda b, c: (b, 0, 0)),
                  pl.BlockSpec((1, SUBLANES, 2 * half), lambda b, c: (b, 0, 0)),
                  pl.BlockSpec((1, SSM_BLOCK_CH), lambda b, c: (0, b))],
        out_specs=[pl.BlockSpec((rc, SSM_BLOCK_CH), lambda b, c: (c, b)),
                   pl.BlockSpec((rc, 2 * half), lambda b, c: (c, b))],
        out_shape=[jax.ShapeDtypeStruct((T, C), F32), jax.ShapeDtypeStruct((T, nblk * 2 * half), F32)],
        scratch_shapes=[pltpu.VMEM((SUBLANES, 2 * half), F32)],
        compiler_params=_params("parallel", "arbitrary"),
    )(u, bmat, cmat, amat, carries, dvec)


def _ssm_scan_bwd(dy, u, xs, bmat, cmat, amat, carries, dvec, name):
    T, C = u.shape
    nblk, seg_steps, kc, nchunk, rc = _ssm_geometry(T, C)
    half = bmat.shape[2] // 2
    width = 2 * half

    def body(dy_ref, u_ref, x_ref, xp_ref, b_ref, c_ref, a_ref, s_ref, d_ref,
             du_ref, db_ref, dc_ref, da_ref, dd_ref, g_ref, st_ref, acc_ref):
        c = pl.program_id(1)

        @pl.when(c == 0)
        def _():
            st_ref[...] = s_ref[0]
            acc_ref[...] = jnp.zeros_like(acc_ref)
            db_ref[...] = jnp.zeros_like(db_ref)
            dc_ref[...] = jnp.zeros_like(dc_ref)
            dd_ref[...] = jnp.zeros_like(dd_ref)

        dyv = dy_ref[...]
        uv = u_ref[...]
        dyb = dyv.astype(BF16)
        g_ref[...] = _dot(dyb, c_ref[0], NT)
        ar = a_ref[0, :, :half]
        ai = a_ref[0, :, half:]

        def step(i, carry):
            gr, gi, sr, si = carry
            k = kc - 1 - i
            row = pl.multiple_of(k * SUBLANES, SUBLANES)
            nr = ar * gr + ai * gi + g_ref[pl.ds(row, SUBLANES), :half]
            ni = ar * gi - ai * gr + g_ref[pl.ds(row, SUBLANES), half:]
            g_ref[pl.ds(row, SUBLANES), :half] = nr
            g_ref[pl.ds(row, SUBLANES), half:] = ni
            prow = pl.multiple_of(jnp.maximum(k - 1, 0) * SUBLANES, SUBLANES)
            live = (k >= 1).astype(F32)
            xr = x_ref[pl.ds(prow, SUBLANES), :half] * live
            xi = x_ref[pl.ds(prow, SUBLANES), half:] * live
            return nr, ni, sr + xr * nr + xi * ni, si + xr * ni - xi * nr

        init = (st_ref[:, :half], st_ref[:, half:], acc_ref[:, :half], acc_ref[:, half:])
        gr, gi, sr, si = lax.fori_loop(0, kc, step, init, unroll=8)
        st_ref[:, :half] = gr
        st_ref[:, half:] = gi
        xpr = xp_ref[:, :half]
        xpi = xp_ref[:, half:]
        first = (c == nchunk - 1)
        xpr = jnp.where(first, _shift_segments(xpr, True), xpr)
        xpi = jnp.where(first, _shift_segments(xpi, True), xpi)
        acc_ref[:, :half] = sr + xpr * gr + xpi * gi
        acc_ref[:, half:] = si + xpr * gi - xpi * gr

        gb = g_ref[...].astype(BF16)
        du_ref[...] = _dot(gb, b_ref[0], NT) + dyv * d_ref[...]
        db_ref[0] += _dot(uv.astype(BF16), gb, TN)
        dc_ref[0] += _dot(dyb, x_ref[...].astype(BF16), TN)
        dd_ref[...] += jnp.sum(dyv * uv, axis=0, keepdims=True)

        @pl.when(c == nchunk - 1)
        def _():
            tot = jnp.sum(acc_ref[...], axis=0, keepdims=True)
            da_ref[0] = jnp.broadcast_to(tot, (SUBLANES, width))

    rev = lambda b, c: (nchunk - 1 - c, b)
    blk3 = lambda b, c: (b, 0, 0)
    prev_group = lambda b, c: (((nchunk - 1 - c) * kc - 1 + seg_steps) % seg_steps, b)
    sd = jax.ShapeDtypeStruct
    return pl.pallas_call(
        body, name=name, grid=(nblk, nchunk),
        in_specs=[pl.BlockSpec((rc, SSM_BLOCK_CH), rev), pl.BlockSpec((rc, SSM_BLOCK_CH), rev),
                  pl.BlockSpec((rc, width), rev), pl.BlockSpec((SUBLANES, width), prev_group),
                  pl.BlockSpec((1,) + bmat.shape[1:], blk3), pl.BlockSpec((1,) + cmat.shape[1:], blk3),
                  pl.BlockSpec((1, SUBLANES, width), blk3), pl.BlockSpec((1, SUBLANES, width), blk3),
                  pl.BlockSpec((1, SSM_BLOCK_CH), lambda b, c: (0, b))],
        out_specs=[pl.BlockSpec((rc, SSM_BLOCK_CH), rev), pl.BlockSpec((1, SSM_BLOCK_CH, width), blk3),
                   pl.BlockSpec((1, SSM_BLOCK_CH, width), blk3), pl.BlockSpec((1, SUBLANES, width), blk3),
                   pl.BlockSpec((1, SSM_BLOCK_CH), lambda b, c: (0, b))],
        out_shape=[sd((T, C), F32), sd((nblk, SSM_BLOCK_CH, width), F32), sd((nblk, SSM_BLOCK_CH, width), F32),
                   sd((nblk, SUBLANES, width), F32), sd((1, C), F32)],
        scratch_shapes=[pltpu.VMEM((rc, width), F32), pltpu.VMEM((SUBLANES, width), F32), pltpu.VMEM((SUBLANES, width), F32)],
        compiler_params=_params("parallel", "arbitrary"),
    )(dy, u, xs, xs, bmat, cmat, amat, carries, dvec)


SSM_STEPS = 256


def _ssm_tiles(ref, v, off, steps, n):
    return [ref[v, pl.ds(off + j, steps, stride=SUBLANES), :] for j in range(n)]


def _ssm_fwd(uz, bmat, cmat, art, ait, dvec, name):
    T = uz.shape[0]
    nblk, cb, width = bmat.shape
    C = nblk * cb
    half = width // 2
    nt = half // LANES
    npair = nblk // 2
    kc = min(SSM_STEPS, T)
    nchunk = T // kc

    def body(u_ref, b_ref, c_ref, ar_ref, ai_ref, d_ref, y_ref, xr_ref, xi_ref, sr_ref, si_ref):
        @pl.when(pl.program_id(0) == 0)
        def _():
            sr_ref[...] = jnp.zeros_like(sr_ref)
            si_ref[...] = jnp.zeros_like(si_ref)

        uv = u_ref[...]
        for b in range(nblk):
            bu = _dot(uv[:, b * cb:(b + 1) * cb].astype(BF16), b_ref[b], NN)
            v, off = b // 2, nt * (b % 2)
            for j in range(nt):
                xr_ref[v, pl.ds(off + j, kc, stride=SUBLANES), :] = bu[:, j * LANES:(j + 1) * LANES]
                xi_ref[v, pl.ds(off + j, kc, stride=SUBLANES), :] = bu[:, half + j * LANES:half + (j + 1) * LANES]
        ars = [ar_ref[v] for v in range(npair)]
        ais = [ai_ref[v] for v in range(npair)]

        def step(k, carry):
            row = pl.ds(pl.multiple_of(k * SUBLANES, SUBLANES), SUBLANES)
            out = []
            for v in range(npair):
                xr, xi = carry[2 * v], carry[2 * v + 1]
                nr = ars[v] * xr - ais[v] * xi + xr_ref[v, row, :]
                ni = ars[v] * xi + ais[v] * xr + xi_ref[v, row, :]
                xr_ref[v, row, :] = nr
                xi_ref[v, row, :] = ni
                out += [nr, ni]
            return tuple(out)

        init = tuple(ref[v] for v in range(npair) for ref in (sr_ref, si_ref))
        fin = lax.fori_loop(0, kc, step, init, unroll=4)
        for v in range(npair):
            sr_ref[v] = fin[2 * v]
            si_ref[v] = fin[2 * v + 1]
        for b in range(nblk):
            v, off = b // 2, nt * (b % 2)
            xb = jnp.concatenate(_ssm_tiles(xr_ref, v, off, kc, nt) + _ssm_tiles(xi_ref, v, off, kc, nt), axis=1)
            cols = slice(b * cb, (b + 1) * cb)
            y_ref[:, cols] = _dot(xb.astype(BF16), c_ref[b], NN) + d_ref[:, cols] * uv[:, cols]

    whole = lambda a: pl.BlockSpec(a.shape, lambda c: (0,) * a.ndim)
    st = pl.BlockSpec((npair, kc * SUBLANES, LANES), lambda c: (0, c, 0))
    sd = jax.ShapeDtypeStruct
    return pl.pallas_call(
        body, name=name, grid=(nchunk,),
        in_specs=[pl.BlockSpec((kc, C), lambda c: (c, 0)), whole(bmat), whole(cmat), whole(art), whole(ait), whole(dvec)],
        out_specs=[pl.BlockSpec((kc, C), lambda c: (c, 0)), st, st],
        out_shape=[sd((T, C), F32), sd((npair, T * SUBLANES, LANES), F32), sd((npair, T * SUBLANES, LANES), F32)],
        scratch_shapes=[pltpu.VMEM((npair, SUBLANES, LANES), F32), pltpu.VMEM((npair, SUBLANES, LANES), F32)],
        compiler_params=_params("arbitrary"),
    )(uz, bmat, cmat, art, ait, dvec)


def _ssm_bwd(dy, uz, xr, xi, bmat, cmat, art, ait, dvec, name):
    T = uz.shape[0]
    nblk, cb, width = bmat.shape
    C = nblk * cb
    half = width // 2
    nt = half // LANES
    npair = nblk // 2
    kc = min(SSM_STEPS, T)
    nchunk = T // kc

    def body(dy_ref, u_ref, xr_ref, xi_ref, xpr_ref, xpi_ref, b_ref, c_ref, ar_ref, ai_ref, d_ref,
             du_ref, db_ref, dc_ref, dar_ref, dai_ref, dd_ref, gr_ref, gi_ref, sr_ref, si_ref):
        c = pl.program_id(0)

        @pl.when(c == 0)
        def _():
            for ref in (sr_ref, si_ref, db_ref, dc_ref, dar_ref, dai_ref, dd_ref):
                ref[...] = jnp.zeros_like(ref)

        dyv = dy_ref[...]
        uv = u_ref[...]
        for b in range(nblk):
            dx = _dot(dyv[:, b * cb:(b + 1) * cb].astype(BF16), c_ref[b], NT)
            v, off = b // 2, nt * (b % 2)
            for j in range(nt):
                gr_ref[v, pl.ds(off + j, kc, stride=SUBLANES), :] = dx[:, j * LANES:(j + 1) * LANES]
                gi_ref[v, pl.ds(off + j, kc, stride=SUBLANES), :] = dx[:, half + j * LANES:half + (j + 1) * LANES]
        ars = [ar_ref[v] for v in range(npair)]
        ais = [ai_ref[v] for v in range(npair)]

        def pair_update(v, gr, gi, row):
            nr = ars[v] * gr + ais[v] * gi + gr_ref[v, row, :]
            ni = ars[v] * gi - ais[v] * gr + gi_ref[v, row, :]
            gr_ref[v, row, :] = nr
            gi_ref[v, row, :] = ni
            return nr, ni

        def step(i, carry):
            k = kc - 1 - i
            row = pl.ds(pl.multiple_of(k * SUBLANES, SUBLANES), SUBLANES)
            prow = pl.ds(pl.multiple_of((k - 1) * SUBLANES, SUBLANES), SUBLANES)
            out = []
            for v in range(npair):
                gr, gi, sr, si = carry[4 * v:4 * v + 4]
                nr, ni = pair_update(v, gr, gi, row)
                pr, pi = xr_ref[v, prow, :], xi_ref[v, prow, :]
                out += [nr, ni, sr + pr * nr + pi * ni, si + pr * ni - pi * nr]
            return tuple(out)

        init = tuple(ref[v] for v in range(npair) for ref in (sr_ref, si_ref, dar_ref, dai_ref))
        mid = lax.fori_loop(0, kc - 1, step, init, unroll=5 if (kc - 1) % 5 == 0 else 1)
        live = (c < nchunk - 1).astype(F32)
        row0 = pl.ds(0, SUBLANES)
        for v in range(npair):
            gr, gi, sr, si = mid[4 * v:4 * v + 4]
            nr, ni = pair_update(v, gr, gi, row0)
            pr, pi = xpr_ref[v] * live, xpi_ref[v] * live
            sr_ref[v] = nr
            si_ref[v] = ni
            dar_ref[v] = sr + pr * nr + pi * ni
            dai_ref[v] = si + pr * ni - pi * nr
        for b in range(nblk):
            v, off = b // 2, nt * (b % 2)
            cols = slice(b * cb, (b + 1) * cb)
            gb = jnp.concatenate(_ssm_tiles(gr_ref, v, off, kc, nt) + _ssm_tiles(gi_ref, v, off, kc, nt), axis=1).astype(BF16)
            xb = jnp.concatenate(_ssm_tiles(xr_ref, v, off, kc, nt) + _ssm_tiles(xi_ref, v, off, kc, nt), axis=1).astype(BF16)
            du_ref[:, cols] = _dot(gb, b_ref[b], NT) + dyv[:, cols] * d_ref[:, cols]
            db_ref[b] += _dot(uv[:, cols].astype(BF16), gb, TN)
            dc_ref[b] += _dot(dyv[:, cols].astype(BF16), xb, TN)
        dd_ref[...] += jnp.sum(dyv * uv, axis=0, keepdims=True)

    whole = lambda a: pl.BlockSpec(a.shape, lambda c: (0,) * a.ndim)
    rev = lambda c: (nchunk - 1 - c, 0)
    st = pl.BlockSpec((npair, kc * SUBLANES, LANES), lambda c: (0, nchunk - 1 - c, 0))
    stp = pl.BlockSpec((npair, SUBLANES, LANES), lambda c: (0, jnp.maximum((nchunk - 1 - c) * kc - 1, 0), 0))
    acc = lambda shape: pl.BlockSpec(shape, lambda c: (0,) * len(shape))
    sd = jax.ShapeDtypeStruct
    pair_shape = (npair, SUBLANES, LANES)
    return pl.pallas_call(
        body, name=name, grid=(nchunk,),
        in_specs=[pl.BlockSpec((kc, C), rev), pl.BlockSpec((kc, C), rev), st, st, stp, stp, whole(bmat), whole(cmat),
                  whole(art), whole(ait), whole(dvec)],
        out_specs=[pl.BlockSpec((kc, C), rev), acc(bmat.shape), acc(bmat.shape), acc(pair_shape), acc(pair_shape), acc((1, C))],
        out_shape=[sd((T, C), F32), sd(bmat.shape, F32), sd(bmat.shape, F32), sd(pair_shape, F32), sd(pair_shape, F32),
                   sd((1, C), F32)],
        scratch_shapes=[pltpu.VMEM((npair, kc * SUBLANES, LANES), F32), pltpu.VMEM((npair, kc * SUBLANES, LANES), F32),
                        pltpu.VMEM(pair_shape, F32), pltpu.VMEM(pair_shape, F32)],
        compiler_params=_params("arbitrary"),
    )(dy, uz, xr, xi, xr, xi, bmat, cmat, art, ait, dvec)


def _ssm_post(y, uz, w_glu, b_glu, name):
    T, C = y.shape
    tm = _pick(T, 512, SUBLANES)

    def body(y_ref, z_ref, w_ref, b_ref, o_ref, a_ref):
        a, _ = _gelu_and_grad(y_ref[...])
        ab = a.astype(BF16)
        sg = _sigmoid(_dot(ab, w_ref[...], NN) + b_ref[...])
        sz, _ = _silu_and_grad(z_ref[...])
        o_ref[...] = (a * sg * sz).astype(BF16)
        a_ref[...] = ab

    row = pl.BlockSpec((tm, C), lambda i: (i, 0))
    return pl.pallas_call(
        body, name=name, grid=(T // tm,),
        in_specs=[row, pl.BlockSpec((tm, C), lambda i: (i, 1)), pl.BlockSpec((C, C), lambda i: (0, 0)),
                  pl.BlockSpec((1, C), lambda i: (0, 0))],
        out_specs=[row, row], out_shape=[jax.ShapeDtypeStruct((T, C), BF16)] * 2, compiler_params=_params("parallel"),
    )(y, uz, w_glu, b_glu.reshape(1, C))


def _ssm_post_bwd(do, y, uz, w_glu, b_glu, name):
    T, C = y.shape
    tm = _pick(T, 512, SUBLANES)

    def body(do_ref, y_ref, z_ref, w_ref, b_ref, dy_ref, dz_ref, ds_ref, db_ref):
        dov = do_ref[...]
        a, da_dy = _gelu_and_grad(y_ref[...])
        sg = _sigmoid(_dot(a.astype(BF16), w_ref[...], NN) + b_ref[...])
        sz, dsz = _silu_and_grad(z_ref[...])
        yg = a * sg
        dz_ref[...] = (dov * yg * dsz).astype(BF16)
        dyg = dov * sz
        ds = dyg * a * sg * (1.0 - sg)
        dsb = ds.astype(BF16)
        ds_ref[...] = dsb
        da = dyg * sg + _dot(dsb, w_ref[...], NT)
        dy_ref[...] = da * da_dy

        @pl.when(pl.program_id(0) == 0)
        def _():
            db_ref[...] = jnp.zeros_like(db_ref)

        db_ref[...] += jnp.sum(ds, axis=0, keepdims=True)

    row = pl.BlockSpec((tm, C), lambda i: (i, 0))
    vec = pl.BlockSpec((1, C), lambda i: (0, 0))
    sd = jax.ShapeDtypeStruct
    return pl.pallas_call(
        body, name=name, grid=(T // tm,),
        in_specs=[row, row, pl.BlockSpec((tm, C), lambda i: (i, 1)), pl.BlockSpec((C, C), lambda i: (0, 0)), vec],
        out_specs=[row, row, row, vec],
        out_shape=[sd((T, C), F32), sd((T, C), BF16), sd((T, C), BF16), sd((1, C), F32)],
        compiler_params=_params("arbitrary"),
    )(do, y, uz, w_glu, b_glu.reshape(1, C))


def _rel_bucket(dist):
    n = jnp.maximum(dist, 0)
    max_exact = NUM_BUCKETS // 2
    n_f = jnp.maximum(n, 1).astype(F32)
    large = max_exact + (jnp.log(n_f / max_exact) / math.log(REL_MAX_DISTANCE / max_exact)
                         * (NUM_BUCKETS - max_exact)).astype(jnp.int32)
    large = jnp.minimum(large, NUM_BUCKETS - 1)
    return jnp.where(n < max_exact, n, large)


def _band_tables():
    qi = jnp.arange(ATTN_BLOCK)[:, None]
    kj = jnp.arange(2 * ATTN_BLOCK)[None, :]
    delta = ATTN_BLOCK + qi - kj
    buckets, bands = [], []
    for window, dilation in ATTN_CONFIGS:
        bands.append((delta >= 0) & (delta <= window // dilation))
        buckets.append(_rel_bucket(jnp.maximum(delta, 0) * dilation))
    return jnp.stack(buckets), jnp.stack(bands)


def _attn_blocks_per_residue(T):
    return [T // (ATTN_BLOCK * d) for _, d in ATTN_CONFIGS]


ATTN_UNITS = 4


def _attn_tile(T, r):
    nq = max(1, ATTN_UNITS // r)
    rows = ATTN_BLOCK * r * nq
    return nq, rows, T // rows


def _attn_units(r, nq, chunk):
    if r >= ATTN_UNITS:
        return [(chunk * ATTN_UNITS + i, None) for i in range(ATTN_UNITS)]
    units = []
    for j in range(nq):
        for s in range(r):
            units.append((ATTN_BLOCK * j * r + s, ATTN_BLOCK * (j - 1) * r + s if j else None))
    return units


def _rows(start, r):
    return pl.ds(start, ATTN_BLOCK, stride=r) if r > 1 else pl.ds(start, ATTN_BLOCK)


def _attn_group_fwd(qkv, biasm, g, name):
    T = qkv.shape[0]
    r = ATTN_CONFIGS[g][1]
    B, hd = ATTN_BLOCK, ATTN_HEAD_DIM
    nq, rows, ntiles = _attn_tile(T, r)
    nchunks = max(1, r // ATTN_UNITS)
    last_prev = B * (nq - 1) * r
    scale = hd ** -0.5
    tiles_per_tensor = 3 * HEADS_PER_GROUP * hd // LANES

    def body(q_ref, kc_ref, kp_ref, vc_ref, vp_ref, bias_ref, o_ref, lse_ref, s_ref, p_ref):
        n = pl.program_id(1)
        lane = lax.broadcasted_iota(jnp.int32, (1, LANES), 1)
        col = lax.broadcasted_iota(jnp.int32, (1, 2 * B), 1)
        masks = [lane < hd, lane >= hd]
        first_pen = jnp.where((col < B) & (n == 0), NEG_INF, 0.0)

        def chunk_body(chunk):
            units = _attn_units(r, nq, chunk)

            def keys(cur_ref, prev_ref, cs, ps):
                prev = prev_ref[_rows(last_prev + (cs if r >= ATTN_UNITS else cs % r), r), :] if ps is None else cur_ref[_rows(ps, r), :]
                return jnp.concatenate([prev, cur_ref[_rows(cs, r), :]], axis=0).astype(BF16)

            for u, (cs, ps) in enumerate(units):
                qv = q_ref[_rows(cs, r), :]
                kw = keys(kc_ref, kp_ref, cs, ps)
                for hh in range(2):
                    s_ref[2 * u + hh] = _dot(jnp.where(masks[hh], qv, 0.0).astype(BF16), kw, NT)
            for u, (cs, ps) in enumerate(units):
                lses = []
                for hh in range(2):
                    s = s_ref[2 * u + hh] * scale + bias_ref[hh]
                    if ps is None:
                        s = s + first_pen
                    m = jnp.max(s, axis=-1, keepdims=True)
                    p = jnp.exp(s - m)
                    l = jnp.sum(p, axis=-1, keepdims=True)
                    p_ref[2 * u + hh] = (p / l).astype(BF16)
                    lses.append(m + jnp.log(l))
                lse_ref[_rows(cs, r), :] = jnp.where(masks[0], lses[0], lses[1])
            for u, (cs, ps) in enumerate(units):
                vw = keys(vc_ref, vp_ref, cs, ps)
                o_ref[_rows(cs, r), :] = (_dot(p_ref[2 * u], jnp.where(masks[0], vw, 0), NN)
                                          + _dot(p_ref[2 * u + 1], jnp.where(masks[1], vw, 0), NN))

        if nchunks == 1:
            chunk_body(0)
        else:
            pl.loop(0, nchunks)(chunk_body)

    def cur(t):
        return pl.BlockSpec((rows, LANES), lambda hf, n: (n, t * tiles_per_tensor + 2 * g + hf))

    def prev(t):
        return pl.BlockSpec((rows, LANES), lambda hf, n: (jnp.maximum(n - 1, 0), t * tiles_per_tensor + 2 * g + hf))

    out = pl.BlockSpec((rows, LANES), lambda hf, n: (n, hf))
    sd = jax.ShapeDtypeStruct((T, 2 * LANES), F32)
    return pl.pallas_call(
        body, name=name, grid=(2, ntiles),
        in_specs=[cur(0), cur(1), prev(1), cur(2), prev(2), pl.BlockSpec((None, 2, B, 2 * B), lambda hf, n: (g, hf, 0, 0))],
        out_specs=[out, out], out_shape=[sd, sd],
        scratch_shapes=[pltpu.VMEM((2 * ATTN_UNITS, B, 2 * B), F32), pltpu.VMEM((2 * ATTN_UNITS, B, 2 * B), BF16)],
        compiler_params=_params("parallel", "parallel"),
    )(qkv, qkv, qkv, qkv, qkv, biasm)


def _attn_group_bwd(qkv, do, dvec, lse, biasm, g, name):
    T = qkv.shape[0]
    r = ATTN_CONFIGS[g][1]
    B, hd = ATTN_BLOCK, ATTN_HEAD_DIM
    nq, rows, ntiles = _attn_tile(T, r)
    nchunks = max(1, r // ATTN_UNITS)
    last_prev = B * (nq - 1) * r
    scale = hd ** -0.5
    tiles_per_tensor = 3 * HEADS_PER_GROUP * hd // LANES

    def body(q_ref, kc_ref, kp_ref, vc_ref, vp_ref, do_ref, dv_ref, lse_ref, bias_ref,
             dq_ref, dk_ref, dvo_ref, dbias_ref, ck_ref, cv_ref, ak_ref, av_ref, s_ref, dp_ref, p_ref, ds_ref):
        n = pl.program_id(1)
        lane = lax.broadcasted_iota(jnp.int32, (1, LANES), 1)
        col = lax.broadcasted_iota(jnp.int32, (1, 2 * B), 1)
        masks = [lane < hd, lane >= hd]
        first_pen = jnp.where((col < B) & (n == 0), NEG_INF, 0.0)

        @pl.when(n == 0)
        def _():
            dbias_ref[...] = jnp.zeros_like(dbias_ref)
            ck_ref[...] = jnp.zeros_like(ck_ref)
            cv_ref[...] = jnp.zeros_like(cv_ref)

        def chunk_body(chunk):
            units = _attn_units(r, nq, chunk)

            def prev_rows(cs):
                return _rows(last_prev + (cs if r >= ATTN_UNITS else cs % r), r)

            def keys(cur_ref, prev_ref, cs, ps):
                prev = prev_ref[prev_rows(cs), :] if ps is None else cur_ref[_rows(ps, r), :]
                return jnp.concatenate([prev, cur_ref[_rows(cs, r), :]], axis=0).astype(BF16)

            for u, (cs, ps) in enumerate(units):
                qv = q_ref[_rows(cs, r), :]
                dov = do_ref[_rows(cs, r), :]
                kw = keys(kc_ref, kp_ref, cs, ps)
                vw = keys(vc_ref, vp_ref, cs, ps)
                for hh in range(2):
                    s_ref[2 * u + hh] = _dot(jnp.where(masks[hh], qv, 0.0).astype(BF16), kw, NT)
                    dp_ref[2 * u + hh] = _dot(jnp.where(masks[hh], dov, 0.0).astype(BF16), vw, NT)
            for u, (cs, ps) in enumerate(units):
                lse_t = lse_ref[_rows(cs, r), :]
                dv_t = dv_ref[_rows(cs, r), :]
                for hh in range(2):
                    lo = hh * hd
                    s = s_ref[2 * u + hh] * scale + bias_ref[hh]
                    if ps is None:
                        s = s + first_pen
                    p = jnp.exp(s - lse_t[:, lo:lo + 1])
                    ds = p * (dp_ref[2 * u + hh] + dv_t[:, lo:lo + 1])
                    dbias_ref[hh] += ds
                    p_ref[2 * u + hh] = p.astype(BF16)
                    ds_ref[2 * u + hh] = ds.astype(BF16)
            for u, (cs, ps) in enumerate(units):
                qv = q_ref[_rows(cs, r), :]
                dov = do_ref[_rows(cs, r), :]
                kw = keys(kc_ref, kp_ref, cs, ps)
                dq, dkw, dvw = 0.0, 0.0, 0.0
                for hh in range(2):
                    dsb = ds_ref[2 * u + hh]
                    dq = dq + _dot(dsb, jnp.where(masks[hh], kw, 0), NN)
                    dkw = dkw + _dot(dsb, jnp.where(masks[hh], qv, 0.0).astype(BF16), TN)
                    dvw = dvw + _dot(p_ref[2 * u + hh], jnp.where(masks[hh], dov, 0.0).astype(BF16), TN)
                dq_ref[_rows(cs, r), :] = dq * scale
                ak_ref[_rows(cs, r), :] = dkw[B:] * scale
                av_ref[_rows(cs, r), :] = dvw[B:]
                if ps is None:
                    ck_ref[prev_rows(cs), :] += dkw[:B] * scale
                    cv_ref[prev_rows(cs), :] += dvw[:B]
                else:
                    ak_ref[_rows(ps, r), :] += dkw[:B] * scale
                    av_ref[_rows(ps, r), :] += dvw[:B]

        @pl.when(n < ntiles)
        def _():
            if nchunks == 1:
                chunk_body(0)
            else:
                pl.loop(0, nchunks)(chunk_body)

        dk_ref[...] = ck_ref[...].astype(BF16)
        dvo_ref[...] = cv_ref[...].astype(BF16)
        ck_ref[...] = ak_ref[...]
        cv_ref[...] = av_ref[...]

    last = ntiles - 1

    def cur(t):
        return pl.BlockSpec((rows, LANES), lambda hf, n: (jnp.minimum(n, last), t * tiles_per_tensor + 2 * g + hf))

    def prev(t):
        return pl.BlockSpec((rows, LANES), lambda hf, n: (jnp.clip(n - 1, 0, last), t * tiles_per_tensor + 2 * g + hf))

    nat = pl.BlockSpec((rows, LANES), lambda hf, n: (jnp.minimum(n, last), hf))
    nat_prev = pl.BlockSpec((rows, LANES), lambda hf, n: (jnp.clip(n - 1, 0, last), hf))
    tab = pl.BlockSpec((None, 2, B, 2 * B), lambda hf, n: (g, hf, 0, 0))
    dtab = pl.BlockSpec((2, B, 2 * B), lambda hf, n: (hf, 0, 0))
    sd = jax.ShapeDtypeStruct
    vm = pltpu.VMEM
    return pl.pallas_call(
        body, name=name, grid=(2, ntiles + 1),
        in_specs=[cur(0), cur(1), prev(1), cur(2), prev(2), nat, nat, nat, tab],
        out_specs=[nat, nat_prev, nat_prev, dtab],
        out_shape=[sd((T, 2 * LANES), F32), sd((T, 2 * LANES), BF16), sd((T, 2 * LANES), BF16),
                   sd((HEADS_PER_GROUP, B, 2 * B), F32)],
        scratch_shapes=[vm((rows, LANES), F32), vm((rows, LANES), F32), vm((rows, LANES), F32), vm((rows, LANES), F32),
                        vm((2 * ATTN_UNITS, B, 2 * B), F32), vm((2 * ATTN_UNITS, B, 2 * B), F32),
                        vm((2 * ATTN_UNITS, B, 2 * B), BF16), vm((2 * ATTN_UNITS, B, 2 * B), BF16)],
        compiler_params=_params("parallel", "arbitrary"),
    )(qkv, qkv, qkv, qkv, qkv, do, dvec, lse, biasm)


def _attn_fwd(q, k, v, biasm, name):
    ng, T, gw = q.shape
    hd = ATTN_HEAD_DIM
    nh = gw // hd
    nblk = T // ATTN_BLOCK
    nbs = _attn_blocks_per_residue(T)
    scale = hd ** -0.5
    B = ATTN_BLOCK

    def body(q_ref, kc_ref, kp_ref, vc_ref, vp_ref, bias_ref, o_ref, lse_ref, s_ref, p_ref):
        g = pl.program_id(0)
        b = pl.program_id(1)
        nb = jnp.where(g == 0, nbs[0], jnp.where(g == 1, nbs[1], nbs[2]))
        no_prev = (b % nb) == 0
        col = lax.broadcasted_iota(jnp.int32, (1, 2 * B), 1)
        pen = jnp.where((col < B) & no_prev, NEG_INF, 0.0)
        heads = [slice(h * hd, (h + 1) * hd) for h in range(nh)]
        for h, hs in enumerate(heads):
            kw = jnp.concatenate([kp_ref[0, :, hs], kc_ref[0, :, hs]], axis=0)
            s_ref[h] = _dot(q_ref[0, :, hs], kw, NT)
        for h, hs in enumerate(heads):
            s = s_ref[h] * scale + bias_ref[0, h] + pen
            m = jnp.max(s, axis=-1, keepdims=True)
            p = jnp.exp(s - m)
            l = jnp.sum(p, axis=-1, keepdims=True)
            p_ref[h] = (p / l).astype(BF16)
            lse_ref[0, :, hs] = jnp.broadcast_to(m + jnp.log(l), (B, hd))
        for h, hs in enumerate(heads):
            vw = jnp.concatenate([vp_ref[0, :, hs], vc_ref[0, :, hs]], axis=0)
            o_ref[0, :, hs] = _dot(p_ref[h], vw, NN)

    cur = pl.BlockSpec((1, B, gw), lambda g, b: (g, b, 0))
    prev = pl.BlockSpec((1, B, gw), lambda g, b: (g, jnp.maximum(b - 1, 0), 0))
    return pl.pallas_call(
        body, name=name, grid=(ng, nblk),
        in_specs=[cur, cur, prev, cur, prev, pl.BlockSpec((1, nh, B, 2 * B), lambda g, b: (g, 0, 0, 0))],
        out_specs=[cur, cur], out_shape=[jax.ShapeDtypeStruct(q.shape, F32)] * 2,
        scratch_shapes=[pltpu.VMEM((nh, B, 2 * B), F32), pltpu.VMEM((nh, B, 2 * B), BF16)],
        compiler_params=_params("parallel", "parallel"),
    )(q, k, k, v, v, biasm)


def _attn_bwd(q, k, v, do, dvec, lse, biasm, name):
    ng, T, gw = q.shape
    hd = ATTN_HEAD_DIM
    nh = gw // hd
    nblk = T // ATTN_BLOCK
    nbs = _attn_blocks_per_residue(T)
    scale = hd ** -0.5
    B = ATTN_BLOCK

    def body(q_ref, kc_ref, kp_ref, vc_ref, vp_ref, do_ref, dv_ref, lse_ref, bias_ref,
             dq_ref, dk_ref, dvo_ref, dbias_ref, ck_ref, cv_ref, s_ref, dp_ref, p_ref, ds_ref):
        g = pl.program_id(0)
        b = pl.program_id(1)
        nb = jnp.where(g == 0, nbs[0], jnp.where(g == 1, nbs[1], nbs[2]))
        no_prev = (b % nb) == 0

        @pl.when(b == 0)
        def _():
            dbias_ref[...] = jnp.zeros_like(dbias_ref)
            ck_ref[...] = jnp.zeros_like(ck_ref)
            cv_ref[...] = jnp.zeros_like(cv_ref)

        @pl.when(b < nblk)
        def _():
            col = lax.broadcasted_iota(jnp.int32, (1, 2 * B), 1)
            pen = jnp.where((col < B) & no_prev, NEG_INF, 0.0)
            heads = [slice(h * hd, (h + 1) * hd) for h in range(nh)]
            for h, hs in enumerate(heads):
                kw = jnp.concatenate([kp_ref[0, :, hs], kc_ref[0, :, hs]], axis=0)
                vw = jnp.concatenate([vp_ref[0, :, hs], vc_ref[0, :, hs]], axis=0)
                s_ref[h] = _dot(q_ref[0, :, hs], kw, NT)
                dp_ref[h] = _dot(do_ref[0, :, hs], vw, NT)
            for h, hs in enumerate(heads):
                lse_col = lse_ref[0, :, h * hd:h * hd + 1]
                d_col = dv_ref[0, :, h * hd:h * hd + 1]
                p = jnp.exp(s_ref[h] * scale + bias_ref[0, h] + pen - lse_col)
                ds = p * (dp_ref[h] + d_col)
                dbias_ref[0, h] += ds
                p_ref[h] = p.astype(BF16)
                ds_ref[h] = ds.astype(BF16)
            for h, hs in enumerate(heads):
                qh = q_ref[0, :, hs]
                kw = jnp.concatenate([kp_ref[0, :, hs], kc_ref[0, :, hs]], axis=0)
                dq_ref[0, :, hs] = (_dot(ds_ref[h], kw, NN) * scale).astype(BF16)
                dkw = _dot(ds_ref[h], qh, TN) * scale
                dvw = _dot(p_ref[h], do_ref[0, :, hs], TN)
                dk_ref[0, :, hs] = (ck_ref[:, hs] + dkw[:B]).astype(BF16)
                dvo_ref[0, :, hs] = (cv_ref[:, hs] + dvw[:B]).astype(BF16)
                ck_ref[:, hs] = dkw[B:]
                cv_ref[:, hs] = dvw[B:]

        @pl.when(b == nblk)
        def _():
            dk_ref[0] = ck_ref[...].astype(BF16)
            dvo_ref[0] = cv_ref[...].astype(BF16)

    last = nblk - 1
    cur = pl.BlockSpec((1, B, gw), lambda g, b: (g, jnp.minimum(b, last), 0))
    prev = pl.BlockSpec((1, B, gw), lambda g, b: (g, jnp.clip(b - 1, 0, last), 0))
    tab = pl.BlockSpec((1, nh, B, 2 * B), lambda g, b: (g, 0, 0, 0))
    sd = jax.ShapeDtypeStruct
    return pl.pallas_call(
        body, name=name, grid=(ng, nblk + 1),
        in_specs=[cur, cur, prev, cur, prev, cur, cur, cur, tab],
        out_specs=[cur, prev, prev, tab],
        out_shape=[sd(q.shape, BF16), sd(q.shape, BF16), sd(q.shape, BF16), sd(biasm.shape, F32)],
        scratch_shapes=[pltpu.VMEM((B, gw), F32), pltpu.VMEM((B, gw), F32), pltpu.VMEM((nh, B, 2 * B), F32),
                        pltpu.VMEM((nh, B, 2 * B), F32), pltpu.VMEM((nh, B, 2 * B), BF16), pltpu.VMEM((nh, B, 2 * B), BF16)],
        compiler_params=_params("parallel", "arbitrary"),
    )(q, k, k, v, v, do, dvec, lse, biasm)


def _attn_mix(os, lses, z, name):
    T, gw = os[0].shape
    C = z.shape[1]
    tm = _pick(T, 512, SUBLANES)

    def body(o0_ref, o1_ref, o2_ref, l0_ref, l1_ref, l2_ref, z_ref, out_ref):
        ls = [l0_ref[...], l1_ref[...], l2_ref[...]]
        mx = jnp.maximum(jnp.maximum(ls[0], ls[1]), ls[2])
        es = [jnp.exp(l - mx) for l in ls]
        den = es[0] + es[1] + es[2]
        for i, o_ref in enumerate((o0_ref, o1_ref, o2_ref)):
            sz, _ = _silu_and_grad(z_ref[:, i * gw:(i + 1) * gw])
            out_ref[:, i * gw:(i + 1) * gw] = (o_ref[...] * (es[i] / den) * sz).astype(BF16)

    row = pl.BlockSpec((tm, C), lambda i: (i, 0))
    grp = pl.BlockSpec((tm, gw), lambda i: (i, 0))
    return pl.pallas_call(
        body, name=name, grid=(T // tm,), in_specs=[grp] * 6 + [row], out_specs=row,
        out_shape=jax.ShapeDtypeStruct((T, C), BF16), compiler_params=_params("parallel"),
    )(*os, *lses, z)


def _attn_mix_bwd(dout, os, lses, z, name):
    T, gw = os[0].shape
    C = z.shape[1]
    tm = _pick(T, 512, SUBLANES)
    head_of = np.arange(gw) // ATTN_HEAD_DIM
    ones = jnp.asarray(head_of[:, None] == head_of[None, :], BF16)

    def body(dout_ref, o0_ref, o1_ref, o2_ref, l0_ref, l1_ref, l2_ref, z_ref, ones_ref,
             dz_ref, do0_ref, do1_ref, do2_ref, dv0_ref, dv1_ref, dv2_ref):
        ls = [l0_ref[...], l1_ref[...], l2_ref[...]]
        mx = jnp.maximum(jnp.maximum(ls[0], ls[1]), ls[2])
        es = [jnp.exp(l - mx) for l in ls]
        den = es[0] + es[1] + es[2]
        alphas, ebar = [], 0.0
        for i, (o_ref, do_ref) in enumerate(((o0_ref, do0_ref), (o1_ref, do1_ref), (o2_ref, do2_ref))):
            sl = slice(i * gw, (i + 1) * gw)
            alpha = es[i] / den
            ov = o_ref[...]
            dv = dout_ref[:, sl]
            sz, dsz = _silu_and_grad(z_ref[:, sl])
            dz_ref[:, sl] = (dv * ov * alpha * dsz).astype(BF16)
            da = dv * sz
            do_ref[...] = da * alpha
            t = da * ov
            t1 = t.astype(BF16)
            r1 = t - t1.astype(F32)
            t2 = r1.astype(BF16)
            t3 = (r1 - t2.astype(F32)).astype(BF16)
            e = _dot(t1, ones_ref[...], NN) + _dot(t2, ones_ref[...], NN) + _dot(t3, ones_ref[...], NN)
            ebar = ebar + alpha * e
            alphas.append(alpha)
        for alpha, dv_ref in zip(alphas, (dv0_ref, dv1_ref, dv2_ref)):
            dv_ref[...] = -alpha * ebar

    row = pl.BlockSpec((tm, C), lambda i: (i, 0))
    grp = pl.BlockSpec((tm, gw), lambda i: (i, 0))
    sd = jax.ShapeDtypeStruct
    res = pl.pallas_call(
        body, name=name, grid=(T // tm,),
        in_specs=[row] + [grp] * 6 + [row, pl.BlockSpec((gw, gw), lambda i: (0, 0))], out_specs=[row] + [grp] * 6,
        out_shape=[sd((T, C), BF16)] + [sd((T, gw), F32)] * 6, compiler_params=_params("parallel"),
    )(dout, *os, *lses, z, ones)
    return res[0], res[1:4], res[4:7]


def _mem_attn(qz, kv, name):
    T = qz.shape[0]
    dm = qz.shape[1] // 2
    M = kv.shape[0]
    hd = dm // MEM_HEADS
    scale = hd ** -0.5
    tm = _pick(T, 512, SUBLANES)

    def body(q_ref, z_ref, k_ref, v_ref, o_ref, s_ref, p_ref):
        heads = [slice(h * hd, (h + 1) * hd) for h in range(MEM_HEADS)]
        for h, sl in enumerate(heads):
            s_ref[h] = _dot(q_ref[:, sl].astype(BF16), k_ref[:, sl], NT)
        for h, sl in enumerate(heads):
            s = s_ref[h] * scale
            p = jnp.exp(s - jnp.max(s, axis=-1, keepdims=True))
            p_ref[h] = (p / jnp.sum(p, axis=-1, keepdims=True)).astype(BF16)
        for h, sl in enumerate(heads):
            sz, _ = _silu_and_grad(z_ref[:, sl])
            o_ref[:, sl] = (_dot(p_ref[h], v_ref[:, sl], NN) * sz).astype(BF16)

    return pl.pallas_call(
        body, name=name, grid=(T // tm,),
        in_specs=[pl.BlockSpec((tm, dm), lambda i: (i, 0)), pl.BlockSpec((tm, dm), lambda i: (i, 1)),
                  pl.BlockSpec((M, dm), lambda i: (0, 0)), pl.BlockSpec((M, dm), lambda i: (0, 1))],
        out_specs=pl.BlockSpec((tm, dm), lambda i: (i, 0)),
        out_shape=jax.ShapeDtypeStruct((T, dm), BF16),
        scratch_shapes=[pltpu.VMEM((MEM_HEADS, tm, M), F32), pltpu.VMEM((MEM_HEADS, tm, M), BF16)],
        compiler_params=_params("parallel"),
    )(qz, qz, kv, kv)


def _mem_attn_bwd(do, qz, kv, name):
    T = qz.shape[0]
    dm = qz.shape[1] // 2
    M = kv.shape[0]
    hd = dm // MEM_HEADS
    scale = hd ** -0.5
    tm = _pick(T, 512, SUBLANES)

    def body(do_ref, q_ref, z_ref, k_ref, v_ref, dq_ref, dz_ref, dk_ref, dv_ref, s_ref, dp_ref, p_ref, ds_ref, dob_ref):
        @pl.when(pl.program_id(0) == 0)
        def _():
            dk_ref[...] = jnp.zeros_like(dk_ref)
            dv_ref[...] = jnp.zeros_like(dv_ref)

        heads = [slice(h * hd, (h + 1) * hd) for h in range(MEM_HEADS)]
        for h, sl in enumerate(heads):
            sz, _ = _silu_and_grad(z_ref[:, sl])
            dob = (do_ref[:, sl] * sz).astype(BF16)
            dob_ref[:, sl] = dob
            s_ref[h] = _dot(q_ref[:, sl].astype(BF16), k_ref[:, sl], NT)
            dp_ref[h] = _dot(dob, v_ref[:, sl], NT)
        for h, sl in enumerate(heads):
            s = s_ref[h] * scale
            p = jnp.exp(s - jnp.max(s, axis=-1, keepdims=True))
            pn = p / jnp.sum(p, axis=-1, keepdims=True)
            dp = dp_ref[h]
            p_ref[h] = pn.astype(BF16)
            ds_ref[h] = (pn * (dp - jnp.sum(dp * pn, axis=-1, keepdims=True))).astype(BF16)
        for h, sl in enumerate(heads):
            _, dsz = _silu_and_grad(z_ref[:, sl])
            dz_ref[:, sl] = (do_ref[:, sl] * _dot(p_ref[h], v_ref[:, sl], NN) * dsz).astype(BF16)
            dq_ref[:, sl] = (_dot(ds_ref[h], k_ref[:, sl], NN) * scale).astype(BF16)
            dk_ref[:, sl] += _dot(ds_ref[h], q_ref[:, sl].astype(BF16), TN) * scale
            dv_ref[:, sl] += _dot(p_ref[h], dob_ref[:, sl], TN)

    rowq = pl.BlockSpec((tm, dm), lambda i: (i, 0))
    rowz = pl.BlockSpec((tm, dm), lambda i: (i, 1))
    kb = pl.BlockSpec((M, dm), lambda i: (0, 0))
    vb = pl.BlockSpec((M, dm), lambda i: (0, 1))
    sd = jax.ShapeDtypeStruct
    dq, dz, dk, dv = pl.pallas_call(
        body, name=name, grid=(T // tm,), in_specs=[rowq, rowq, rowz, kb, vb],
        out_specs=[rowq, rowq, kb, kb],
        out_shape=[sd((T, dm), BF16), sd((T, dm), BF16), sd((M, dm), F32), sd((M, dm), F32)],
        scratch_shapes=[pltpu.VMEM((MEM_HEADS, tm, M), F32), pltpu.VMEM((MEM_HEADS, tm, M), F32),
                        pltpu.VMEM((MEM_HEADS, tm, M), BF16), pltpu.VMEM((MEM_HEADS, tm, M), BF16), pltpu.VMEM((tm, dm), BF16)],
        compiler_params=_params("arbitrary"),
    )(do, qz, qz, kv, kv)
    return dq, dz, dk, dv


def _merge(bps, logits, b_gate, name):
    T, D = bps[0].shape
    tm = _pick(T, 512, SUBLANES)

    def body(p0_ref, p1_ref, p2_ref, l_ref, b_ref, o_ref):
        acc = 0.0
        for i, p_ref in enumerate((p0_ref, p1_ref, p2_ref)):
            sl = slice(i * D, (i + 1) * D)
            acc = acc + _sigmoid(l_ref[:, sl] + b_ref[:, sl]) * p_ref[...]
        o_ref[...] = acc.astype(BF16)

    row = pl.BlockSpec((tm, D), lambda i: (i, 0))
    return pl.pallas_call(
        body, name=name, grid=(T // tm,),
        in_specs=[row, row, row, pl.BlockSpec((tm, 3 * D), lambda i: (i, 0)), pl.BlockSpec((1, 3 * D), lambda i: (0, 0))],
        out_specs=row, out_shape=jax.ShapeDtypeStruct((T, D), BF16), compiler_params=_params("parallel"),
    )(*bps, logits, b_gate.reshape(1, 3 * D))


def _merge_bwd(dmerged, bps, logits, b_gate, name):
    T, D = bps[0].shape
    tm = _pick(T, 512, SUBLANES)

    def body(dm_ref, p0_ref, p1_ref, p2_ref, l_ref, b_ref, d0_ref, d1_ref, d2_ref, dl_ref, db_ref):
        @pl.when(pl.program_id(0) == 0)
        def _():
            db_ref[...] = jnp.zeros_like(db_ref)

        dmv = dm_ref[...]
        for i, (p_ref, d_ref) in enumerate(((p0_ref, d0_ref), (p1_ref, d1_ref), (p2_ref, d2_ref))):
            sl = slice(i * D, (i + 1) * D)
            gt = _sigmoid(l_ref[:, sl] + b_ref[:, sl])
            d_ref[...] = (dmv * gt).astype(BF16)
            dl = dmv * p_ref[...] * gt * (1.0 - gt)
            dl_ref[:, sl] = dl.astype(BF16)
            db_ref[:, sl] += jnp.sum(dl, axis=0, keepdims=True)

    row = pl.BlockSpec((tm, D), lambda i: (i, 0))
    wide = pl.BlockSpec((tm, 3 * D), lambda i: (i, 0))
    vec = pl.BlockSpec((1, 3 * D), lambda i: (0, 0))
    sd = jax.ShapeDtypeStruct
    return pl.pallas_call(
        body, name=name, grid=(T // tm,), in_specs=[row, row, row, row, wide, vec],
        out_specs=[row, row, row, wide, vec],
        out_shape=[sd((T, D), BF16)] * 3 + [sd((T, 3 * D), BF16), sd((1, 3 * D), F32)],
        compiler_params=_params("arbitrary"),
    )(dmerged, *bps, logits, b_gate.reshape(1, 3 * D))


def _to_segments(a):
    T, C = a.shape
    return a.reshape(SSM_SEGMENTS, T // SSM_SEGMENTS, C).transpose(1, 0, 2).reshape(T, C)


def _from_segments(a):
    T, C = a.shape
    return a.reshape(T // SSM_SEGMENTS, SSM_SEGMENTS, C).transpose(1, 0, 2).reshape(T, C)


def _to_residues(a):
    T = a.shape[0]
    gw = HEADS_PER_GROUP * ATTN_HEAD_DIM
    out = []
    for g, (_, r) in enumerate(ATTN_CONFIGS):
        ag = a[:, g * gw:(g + 1) * gw].reshape(T // r, r, gw)
        out.append(ag.transpose(1, 0, 2).reshape(T, gw))
    return jnp.stack(out)


def _from_residues(a):
    _, T, gw = a.shape
    out = []
    for g, (_, r) in enumerate(ATTN_CONFIGS):
        out.append(a[g].reshape(r, T // r, gw).transpose(1, 0, 2).reshape(T, gw))
    return jnp.concatenate(out, axis=1)


def _block_diag(w):
    nblk, ng, a, b = w.shape
    eye = jnp.eye(ng, dtype=w.dtype)
    return (w[:, :, :, None, :] * eye[None, :, None, :, None]).reshape(nblk, ng * a, ng * b)


def _block_diag_part(m, a, b):
    nblk = m.shape[0]
    ng = m.shape[1] // a
    m5 = m.reshape(nblk, ng, a, ng, b)
    eye = jnp.eye(ng, dtype=m.dtype)
    return jnp.sum(m5 * eye[None, :, None, :, None], axis=3)


def _ssm_matrices(p, L, tag):
    G, P = p["ssm_lambda_re"].shape[1:]
    Hg = SSM_GROUP
    gpb = SSM_BLOCK_CH // Hg
    nblk = G // gpb
    br = p["ssm_b_re"][L].transpose(2, 0, 1)
    bi = p["ssm_b_im"][L].transpose(2, 0, 1)
    disc_in = (p["ssm_lambda_re"][L], p["ssm_lambda_im"][L], p["ssm_log_dt"][L].reshape(G, 1), br, bi)
    ar, ai, bbr, bbi = _ssm_disc(*disc_in, name=f"ssm_disc_{tag}")
    amat = (ar.reshape(nblk // 2, SUBLANES, LANES), ai.reshape(nblk // 2, SUBLANES, LANES))
    bbr_g = bbr.transpose(1, 0, 2).reshape(nblk, gpb, Hg, P)
    bbi_g = bbi.transpose(1, 0, 2).reshape(nblk, gpb, Hg, P)
    bmat = jnp.concatenate([_block_diag(bbr_g), _block_diag(bbi_g)], axis=2).astype(BF16)
    cre = p["ssm_c_re"][L].reshape(nblk, gpb, Hg, P).transpose(0, 1, 3, 2)
    cim = p["ssm_c_im"][L].reshape(nblk, gpb, Hg, P).transpose(0, 1, 3, 2)
    cmat = jnp.concatenate([_block_diag(cre), -_block_diag(cim)], axis=1).astype(BF16)
    return disc_in, amat, bmat, cmat


def _layer_fwd(x, mem, p, wb, L, biasm):
    T, D = x.shape
    C = p["ssm_d"].shape[1]
    dm = wb["w_br_mem"].shape[1]
    tag = f"l{L}"
    s = {"x": x}
    h = _rmsnorm(x, p["norm_g"][L], f"norm_{tag}")
    offs = [int(o) for o in np.cumsum([0, 2 * C, 3 * 768, 768, 2 * dm, 3 * D])]
    names = ("uz", "qkv", "z_attn", "qz_mem", "logits")
    dts = (F32, F32, F32, F32, F32)
    for i, (nm, dt) in enumerate(zip(names, dts)):
        s[nm] = _matmul(h, wb["w_in"], mode="nn", name=f"in_{nm}_{tag}", out_dtype=dt, b_lead=L, b_off=offs[i],
                        n_cols=offs[i + 1] - offs[i])
    s["h"] = h

    disc_in, amat, bmat, cmat = _ssm_matrices(p, L, tag)
    dvec = p["ssm_d"][L].reshape(1, C)
    y, xr, xi = _ssm_fwd(s["uz"], bmat, cmat, *amat, dvec, f"ssm_scan_{tag}")
    o_ssm, a_glu = _ssm_post(y, s["uz"], wb["w_glu"][L], p["b_glu"][L], f"ssm_post_{tag}")
    s.update(disc_in=disc_in, amat=amat, bmat=bmat, cmat=cmat, xr=xr, xi=xi, y=y, a_glu=a_glu, o_ssm=o_ssm)

    groups = [_attn_group_fwd(s["qkv"], biasm, g, f"attn_g{g}_{tag}") for g in range(len(ATTN_CONFIGS))]
    os, lses = [o for o, _ in groups], [l for _, l in groups]
    o_attn = _attn_mix(os, lses, s["z_attn"], f"attn_mix_{tag}")
    s.update(os=os, lses=lses, o_attn=o_attn)

    mn = _rmsnorm(mem, p["mem_norm_g"][L], f"mem_norm_{tag}")
    kv = _matmul(mn, wb["w_mem_kv"], mode="nn", name=f"mem_kv_{tag}", out_dtype=BF16, b_lead=L)
    o_mem = _mem_attn(s["qz_mem"], kv, f"mem_attn_{tag}")
    s.update(mn=mn, kv=kv, o_mem=o_mem)

    bps = [_matmul(o, wb[n], mode="nn", name=f"br_{n}_{tag}", b_lead=L)
           for o, n in ((o_ssm, "w_br_ssm"), (o_attn, "w_br_attn"), (o_mem, "w_br_mem"))]
    merged = _merge(bps, s["logits"], p["b_gate"][L], f"merge_{tag}")
    s.update(bps=bps, merged=merged)
    x_new = _matmul(merged, wb["w_out"], mode="nn", name=f"out_{tag}", add=x, b_lead=L)
    return x_new, s


def _layer_bwd(dx, mem, p, wb, L, s, biasm, gprev):
    T, D = dx.shape
    C = p["ssm_d"].shape[1]
    depth = p["norm_g"].shape[0]
    tag = f"l{L}"
    g = {}

    def wgrad(n, a, b):
        g[n] = _matmul(a, b, mode="tn", name=f"d{n}_{tag}", out_dtype=BF16, stack=(L, depth, gprev.get(n)))

    dmerged = _matmul(dx, wb["w_out"], mode="nt", name=f"d_merged_{tag}", b_lead=L)
    wgrad("w_out", s["merged"], dx)
    dbp0, dbp1, dbp2, dlogits, g["b_gate"] = _merge_bwd(dmerged, s["bps"], s["logits"], p["b_gate"][L], f"merge_bwd_{tag}")
    dos = []
    for dbp, o, n in ((dbp0, s["o_ssm"], "w_br_ssm"), (dbp1, s["o_attn"], "w_br_attn"), (dbp2, s["o_mem"], "w_br_mem")):
        dos.append(_matmul(dbp, wb[n], mode="nt", name=f"d_o_{n}_{tag}", b_lead=L))
        wgrad(n, o, dbp)

    dy, dz_ssm, ds_glu, g["b_glu"] = _ssm_post_bwd(dos[0], s["y"], s["uz"], wb["w_glu"][L], p["b_glu"][L], f"ssm_post_bwd_{tag}")
    wgrad("w_glu", s["a_glu"], ds_glu)
    dvec = p["ssm_d"][L].reshape(1, C)
    du, dbm, dct, dar, dai, g["ssm_d"] = _ssm_bwd(dy, s["uz"], s["xr"], s["xi"], s["bmat"], s["cmat"], *s["amat"], dvec,
                                                  f"ssm_scan_bwd_{tag}")
    G, P = p["ssm_lambda_re"].shape[1:]
    Hg = SSM_GROUP
    half = dbm.shape[2] // 2
    dbbr = _block_diag_part(dbm[:, :, :half], Hg, P).reshape(G, Hg, P).transpose(1, 0, 2)
    dbbi = _block_diag_part(dbm[:, :, half:], Hg, P).reshape(G, Hg, P).transpose(1, 0, 2)
    g["ssm_c_re"] = _block_diag_part(dct[:, :, :half], Hg, P).reshape(G, Hg, P)
    g["ssm_c_im"] = -_block_diag_part(dct[:, :, half:], Hg, P).reshape(G, Hg, P)
    glre, glim, gdt, gbr, gbi = _ssm_disc_bwd(*s["disc_in"], dar.reshape(G, P), dai.reshape(G, P), dbbr, dbbi,
                                              name=f"ssm_disc_bwd_{tag}")
    g["ssm_lambda_re"], g["ssm_lambda_im"], g["ssm_log_dt"] = glre, glim, gdt.reshape(G)
    g["ssm_b_re"] = gbr.transpose(1, 2, 0)
    g["ssm_b_im"] = gbi.transpose(1, 2, 0)

    dz_attn, do_g, dvec_g = _attn_mix_bwd(dos[1], s["os"], s["lses"], s["z_attn"], f"attn_mix_bwd_{tag}")
    back = [_attn_group_bwd(s["qkv"], do_g[g], dvec_g[g], s["lses"][g], biasm, g, f"attn_bwd_g{g}_{tag}")
            for g in range(len(ATTN_CONFIGS))]
    dqkv = [b[i].astype(BF16) for i in range(3) for b in back]
    dbias = jnp.stack([b[3] for b in back])

    dq_mem, dz_mem, dk_mem, dv_mem = _mem_attn_bwd(dos[2], s["qz_mem"], s["kv"], f"mem_attn_bwd_{tag}")
    dkv = jnp.concatenate([dk_mem, dv_mem], axis=1)
    wgrad("w_mem_kv", s["mn"], dkv)
    dmn = _matmul(dkv, wb["w_mem_kv"], mode="nt", name=f"d_mn_{tag}", b_lead=L)
    _, g["mem_norm_g"] = _rmsnorm_bwd(mem, p["mem_norm_g"][L], dmn, None, f"mem_norm_bwd_{tag}")

    dproj = jnp.concatenate([du.astype(BF16), dz_ssm] + dqkv + [dz_attn, dq_mem, dz_mem, dlogits], axis=1)
    dh = _matmul(dproj, wb["w_in"], mode="nt", name=f"d_h_{tag}", b_lead=L)
    wgrad("w_in", s["h"], dproj)
    dx_in, g["norm_g"] = _rmsnorm_bwd(s["x"], p["norm_g"][L], dh, dx, f"norm_bwd_{tag}")
    return dx_in, g, dbias


def _bucket_onehot(gi):
    buckets, bands = _band_tables()
    hit = (buckets[gi].reshape(1, -1) == jnp.arange(NUM_BUCKETS)[:, None]) & bands[gi].reshape(1, -1)
    return hit.astype(BF16)


def _bias_tables(rel_bias, name):
    _, bands = _band_tables()
    out = []
    for gi in range(len(ATTN_CONFIGS)):
        tab = rel_bias[:, gi * HEADS_PER_GROUP:(gi + 1) * HEADS_PER_GROUP].T
        flat = _matmul(tab, _bucket_onehot(gi), mode="nn", name=f"{name}_{gi}", split_a=3, tn=4096)
        out.append(jnp.where(bands[gi][None], flat.reshape(HEADS_PER_GROUP, ATTN_BLOCK, 2 * ATTN_BLOCK), NEG_INF))
    return jnp.stack(out)


def _rel_bias_grad(dbias_sum, name):
    cols = []
    for gi in range(len(ATTN_CONFIGS)):
        flat = dbias_sum[gi].reshape(HEADS_PER_GROUP, -1)
        cols.append(_matmul(flat, _bucket_onehot(gi), mode="nt", name=f"{name}_{gi}", split_a=2, tk=4096).T)
    return jnp.concatenate(cols, axis=1)


def _local_step(x, mem, target, p, wb):
    depth = p["norm_g"].shape[0]
    biasm = _bias_tables(p["rel_bias"], "bias_table")
    saved = []
    for L in range(depth):
        x, s = _layer_fwd(x, mem, p, wb, L, biasm)
        saved.append(s)
    loss_vec, dx, dgf = _loss_head(x, p["final_norm_g"], target, "loss_head")
    grads = {"final_norm_g": dgf.reshape(-1)}
    per_layer = [None] * depth
    dbias_sum = 0.0
    stacked = {}
    for L in reversed(range(depth)):
        dx, per_layer[L], dbias = _layer_bwd(dx, mem, p, wb, L, saved[L], biasm, stacked)
        stacked = {n: per_layer[L][n] for n, _ in BIG}
        dbias_sum = dbias_sum + dbias
    grads.update(stacked)
    for n in per_layer[0]:
        if n not in stacked:
            grads[n] = jnp.stack([per_layer[L][n].reshape(p[n].shape[1:]) for L in range(depth)])
    grads["rel_bias"] = _rel_bias_grad(dbias_sum, "d_rel_bias")
    return jnp.sum(loss_vec), dx, grads


def _chip_coords(j):
    return j // 2, j % 2


def _place_shard(shard, ax, chip, name):
    _, a, b = shard.shape
    ra = _pick(a, 256, 16)
    full = (2, a * N_CHIPS, b) if ax == 1 else (2, a, b * N_CHIPS)
    per = a // ra

    def body(j_ref, s_ref, o_ref):
        o_ref[...] = s_ref[...].astype(BF16)

    out_idx = (lambda l, i, j: (l, j[0] * per + i, 0)) if ax == 1 else (lambda l, i, j: (l, i, j[0]))
    return pl.pallas_call(
        body, name=name,
        grid_spec=pltpu.PrefetchScalarGridSpec(
            num_scalar_prefetch=1, grid=(2, per),
            in_specs=[pl.BlockSpec((None, ra, b), lambda l, i, j: (l, i, 0))],
            out_specs=pl.BlockSpec((None, ra, b), out_idx)),
        out_shape=jax.ShapeDtypeStruct(full, BF16), compiler_params=_params("parallel", "parallel"),
    )(chip, shard)


def _gather_shards(fulls, axes, name):
    n = len(fulls)
    widths = [a.shape[ax] // N_CHIPS for a, ax in zip(fulls, axes)]
    aligns = [LANES if ax == 2 else 16 for ax in axes]

    def body(*refs):
        outs = refs[n:2 * n]
        send_sems, recv_sems, fsend_sems, frecv_sems = refs[2 * n:]
        x, y, c = lax.axis_index("x"), lax.axis_index("y"), lax.axis_index("c")
        mine = 2 * x + y
        sibling = (x, y, 1 - c)

        def window(t, layer, j):
            start = pl.ds(pl.multiple_of(j * widths[t], aligns[t]), widths[t])
            return outs[t].at[(layer, start, slice(None)) if axes[t] == 1 else (layer, slice(None), start)]

        def over_ici(t, j, block):
            return pltpu.make_async_remote_copy(
                src_ref=window(t, c, mine), dst_ref=window(t, c, block), send_sem=send_sems.at[t, j],
                recv_sem=recv_sems.at[t, block], device_id=(*_chip_coords(j), c), device_id_type=MESH)

        def over_d2d(t, j, layer):
            return pltpu.make_async_remote_copy(
                src_ref=window(t, layer, j), dst_ref=window(t, layer, j), send_sem=fsend_sems.at[t, j],
                recv_sem=frecv_sems.at[t, j], device_id=sibling, device_id_type=MESH)

        for t in range(n):
            for j in range(N_CHIPS):
                @pl.when(j != mine)
                def _():
                    over_ici(t, j, mine).start()
        for t in range(n):
            for j in range(N_CHIPS):
                @pl.when(j != mine)
                def _():
                    over_ici(t, j, j).wait_recv()
                    over_d2d(t, j, c).start()
        for t in range(n):
            for j in range(N_CHIPS):
                @pl.when(j != mine)
                def _():
                    over_ici(t, j, mine).wait_send()
                    over_d2d(t, j, c).wait_send()
                    over_d2d(t, j, 1 - c).wait_recv()

    sem = pltpu.SemaphoreType.DMA
    return pl.pallas_call(
        body, name=name, in_specs=[HBM] * n, out_specs=[HBM] * n,
        out_shape=[jax.ShapeDtypeStruct(a.shape, a.dtype) for a in fulls],
        input_output_aliases={t: t for t in range(n)},
        scratch_shapes=[sem((n, N_CHIPS)), sem((n, N_CHIPS)), sem((n, N_CHIPS)), sem((n, N_CHIPS))],
    )(*fulls)


def _scatter_slices(arrays, axes, name):
    n = len(arrays)

    def piece(a, ax):
        if ax is None:
            return a.shape, None
        w = a.shape[ax] // N_CHIPS
        return a.shape[:ax] + (w,) + a.shape[ax + 1:], w

    shapes = [piece(a, ax) for a, ax in zip(arrays, axes)]

    def body(*refs):
        ins, outs = refs[:n], refs[n:2 * n]
        send_sems, recv_sems, loc_sems = refs[2 * n:]
        x, y, c = lax.axis_index("x"), lax.axis_index("y"), lax.axis_index("c")
        mine = 2 * x + y

        def src(t, j):
            ax, w = axes[t], shapes[t][1]
            if ax is None:
                return ins[t]
            idx = tuple(pl.ds(j * w, w) if d == ax else slice(None) for d in range(len(arrays[t].shape)))
            return ins[t].at[idx]

        for t in range(n):
            for j in range(N_CHIPS):
                @pl.when(j == mine)
                def _():
                    pltpu.make_async_copy(src(t, j), outs[t].at[j], loc_sems.at[t]).start()

                @pl.when(j != mine)
                def _():
                    pltpu.make_async_remote_copy(
                        src_ref=src(t, j), dst_ref=outs[t].at[mine], send_sem=send_sems.at[t, j], recv_sem=recv_sems.at[t, mine],
                        device_id=(*_chip_coords(j), c), device_id_type=MESH).start()
        for t in range(n):
            for j in range(N_CHIPS):
                @pl.when(j == mine)
                def _():
                    pltpu.make_async_copy(src(t, j), outs[t].at[j], loc_sems.at[t]).wait()

                @pl.when(j != mine)
                def _():
                    cp = pltpu.make_async_remote_copy(
                        src_ref=src(t, j), dst_ref=outs[t].at[j], send_sem=send_sems.at[t, j], recv_sem=recv_sems.at[t, j],
                        device_id=(*_chip_coords(j), c), device_id_type=MESH)
                    cp.wait_send()
                    cp.wait_recv()

    return pl.pallas_call(
        body, name=name, in_specs=[HBM] * n, out_specs=[HBM] * n,
        out_shape=[jax.ShapeDtypeStruct((N_CHIPS,) + sh, a.dtype) for a, (sh, _) in zip(arrays, shapes)],
        scratch_shapes=[pltpu.SemaphoreType.DMA((n, N_CHIPS)), pltpu.SemaphoreType.DMA((n, N_CHIPS)), pltpu.SemaphoreType.DMA((n,))],
    )(*arrays)


def _swap_layers(stacked, name):
    n = len(stacked)

    def body(*refs):
        ins, outs = refs[:n], refs[n:2 * n]
        send_sems, recv_sems = refs[2 * n:]
        c = lax.axis_index("c")
        peer = (lax.axis_index("x"), lax.axis_index("y"), 1 - c)
        cps = [pltpu.make_async_remote_copy(src_ref=ins[t].at[1 - c], dst_ref=outs[t], send_sem=send_sems.at[t],
                                            recv_sem=recv_sems.at[t], device_id=peer, device_id_type=MESH) for t in range(n)]
        for cp in cps:
            cp.start()
        for cp in cps:
            cp.wait_send()
            cp.wait_recv()

    return pl.pallas_call(
        body, name=name, in_specs=[HBM] * n, out_specs=[HBM] * n,
        out_shape=[jax.ShapeDtypeStruct(a.shape[1:], a.dtype) for a in stacked],
        scratch_shapes=[pltpu.SemaphoreType.DMA((n,)), pltpu.SemaphoreType.DMA((n,))],
    )(*stacked)


def _merge_layers(stacked, name):
    n = len(stacked)

    def body(*refs):
        outs = refs[n:2 * n]
        send_sems, recv_sems = refs[2 * n:]
        c = lax.axis_index("c")
        peer = (lax.axis_index("x"), lax.axis_index("y"), 1 - c)
        for t in range(n):
            pltpu.make_async_remote_copy(src_ref=outs[t].at[c], dst_ref=outs[t].at[c], send_sem=send_sems.at[t],
                                         recv_sem=recv_sems.at[t], device_id=peer, device_id_type=MESH).start()
        for t in range(n):
            cp = pltpu.make_async_remote_copy(src_ref=outs[t].at[c], dst_ref=outs[t].at[1 - c], send_sem=send_sems.at[t],
                                              recv_sem=recv_sems.at[t], device_id=peer, device_id_type=MESH)
            cp.wait_send()
            cp.wait_recv()

    sem = pltpu.SemaphoreType.DMA
    return pl.pallas_call(
        body, name=name, in_specs=[HBM] * n, out_specs=[HBM] * n,
        out_shape=[jax.ShapeDtypeStruct(a.shape, a.dtype) for a in stacked],
        input_output_aliases={t: t for t in range(n)}, scratch_shapes=[sem((n,)), sem((n,))],
    )(*stacked)


def _pair_sum(stacked, landed, core, name):
    _, K, N = stacked.shape
    tr = _pick(K, max(16, (1 << 19) // N // 16 * 16), 16)

    def body(c_ref, s_ref, l_ref, o_ref):
        o_ref[...] = (s_ref[...].astype(F32) + l_ref[...].astype(F32)).astype(o_ref.dtype)

    return pl.pallas_call(
        body, name=name,
        grid_spec=pltpu.PrefetchScalarGridSpec(
            num_scalar_prefetch=1, grid=(K // tr,),
            in_specs=[pl.BlockSpec((None, tr, N), lambda i, c: (c[0], i, 0)), pl.BlockSpec((tr, N), lambda i, c: (i, 0))],
            out_specs=pl.BlockSpec((tr, N), lambda i, c: (i, 0))),
        out_shape=jax.ShapeDtypeStruct((K, N), stacked.dtype), compiler_params=_params("parallel"),
    )(core, stacked, landed)


def _sum_chips(landed, core, name):
    _, R, C = landed.shape
    tr = _pick(R, max(SUBLANES, (1 << 19) // C // 16 * 16), 16)

    def body(c_ref, l_ref, o_ref):
        acc = l_ref[0].astype(F32) + l_ref[1].astype(F32)
        acc = acc + l_ref[2].astype(F32)
        o_ref[...] = acc + l_ref[3].astype(F32)

    return pl.pallas_call(
        body, name=name,
        grid_spec=pltpu.PrefetchScalarGridSpec(
            num_scalar_prefetch=1, grid=(R // tr,),
            in_specs=[pl.BlockSpec((N_CHIPS, tr, C), lambda i, c: (0, i, 0))],
            out_specs=pl.BlockSpec((None, tr, C), lambda i, c: (c[0], i, 0))),
        out_shape=jax.ShapeDtypeStruct((2, R, C), F32), compiler_params=_params("parallel"),
    )(core, landed)


def _adamw(w, g, m, v, name):
    R, C = w.shape
    tr = _pick(R, max(SUBLANES, (1 << 18) // C // 8 * 8), SUBLANES)
    c1 = 1.0 / (1.0 - ADAM_B1 ** ADAM_STEP)
    c2 = 1.0 / (1.0 - ADAM_B2 ** ADAM_STEP)

    def body(w_ref, g_ref, m_ref, v_ref, d_ref, nm_ref, nv_ref):
        g = g_ref[...]
        nm = ADAM_B1 * m_ref[...] + (1.0 - ADAM_B1) * g
        nv = ADAM_B2 * v_ref[...] + (1.0 - ADAM_B2) * (g * g)
        nm_ref[...] = nm
        nv_ref[...] = nv
        d_ref[...] = -ADAM_LR * ((nm * c1) / (jnp.sqrt(nv * c2) + ADAM_EPS) + ADAM_WD * w_ref[...])

    blk = pl.BlockSpec((tr, C), lambda i: (i, 0))
    return pl.pallas_call(
        body, name=name, grid=(R // tr,), in_specs=[blk] * 4, out_specs=[blk] * 3,
        out_shape=[jax.ShapeDtypeStruct((R, C), F32)] * 3, compiler_params=_params("parallel"),
    )(w, g, m, v)


def _pack_small(d, prefix=""):
    flat = jnp.concatenate([d[prefix + n].astype(F32).reshape(-1) for n in SMALL])
    pad = (-flat.shape[0]) % (2 * 16 * LANES)
    return jnp.pad(flat, (0, pad)).reshape(-1, LANES)


def _unpack_small(packed, shapes):
    flat = packed.reshape(-1)
    out, off = {}, 0
    for n in SMALL:
        size = int(np.prod(shapes[n]))
        out[n] = flat[off:off + size].reshape(shapes[n])
        off += size
    return out


def kernel(*args):
    p = dict(zip(INPUTS, args))
    x, mem, target = p["x"][0], p["mem"][0], p["loss_target"][0]

    names = [n for n, _ in BIG] + ["small"]
    core = lax.axis_index("c").astype(jnp.int32).reshape(1)
    chip = (2 * lax.axis_index("x") + lax.axis_index("y")).astype(jnp.int32).reshape(1)
    placed = [_place_shard(p[n], ax, chip, f"place_{n}") for n, ax in BIG]
    wb = dict(zip(names, _gather_shards(placed, [ax for _, ax in BIG], "gather_weights")))

    loss_part, dx, grads = _local_step(x, mem, target, p, wb)
    loss = lax.psum(loss_part, ("x", "y", "c"))

    stacked = [grads[n] for n, _ in BIG] + [_pack_small(grads).reshape(2, -1, LANES)]
    theirs = _swap_layers(stacked, "swap_layers")
    pair = [_pair_sum(s, o, core, f"pair_sum_{n}") for n, s, o in zip(names, stacked, theirs)]
    landed = _scatter_slices(pair, [ax - 1 for _, ax in BIG] + [None], "scatter_grads")
    reduced = [_sum_chips(ld.reshape(N_CHIPS, -1, ld.shape[-1]), core, f"sum_chips_{n}") for n, ld in zip(names, landed)]
    total = _merge_layers(reduced, "merge_layers")

    out = {}
    for (n, _), g in zip(BIG, total):
        sh = p[n].shape
        two_d = lambda a: a.reshape(-1, sh[-1])
        res = (g,) + tuple(_adamw(two_d(p[n]), two_d(g), two_d(p["m_" + n]), two_d(p["v_" + n]), f"adamw_{n}"))
        for key, r in zip(("grad_", "delta_", "new_m_", "new_v_"), res):
            out[key + n] = r.reshape(sh)
    g = total[-1].reshape(-1, LANES)
    res = (g,) + tuple(_adamw(_pack_small(p), g, _pack_small(p, "m_"), _pack_small(p, "v_"), "adamw_small"))
    shapes = {n: p[n].shape for n in SMALL}
    for key, r in zip(("grad_", "delta_", "new_m_", "new_v_"), res):
        for n, a in _unpack_small(r, shapes).items():
            out[key + n] = a

    result = [loss, dx.reshape(p["x"].shape)]
    for key in ("grad_", "delta_", "new_m_", "new_v_"):
        result += [out[key + n] for n in WEIGHTS]
    return tuple(result)
```

```python
import math

import jax
import jax.numpy as jnp
import numpy as np
from jax import lax
from jax.experimental import pallas as pl
from jax.experimental.pallas import tpu as pltpu

F32 = jnp.float32
BF16 = jnp.bfloat16
MESH = pl.DeviceIdType.MESH
HBM = pl.BlockSpec(memory_space=pltpu.HBM)

EPS = 1e-6
SSM_GROUP = 16
SSM_STATE = 64
ATTN_HEAD_DIM = 64
HEADS_PER_GROUP = 4
ATTN_CONFIGS = ((128, 1), (512, 4), (2048, 16))
ATTN_BLOCK = 128
NUM_BUCKETS = 32
REL_MAX_DISTANCE = 2048
NEG_INF = -1e30
MEM_HEADS = 4
ADAM_LR = 0.001
ADAM_B1 = 0.9
ADAM_B2 = 0.999
ADAM_EPS = 1e-08
ADAM_WD = 0.01
ADAM_STEP = 10

LANES = 128
SUBLANES = 8
VMEM_LIMIT_BYTES = 48 * 1024 * 1024
SSM_BLOCK_CH = 128
SSM_SEGMENTS = SUBLANES
SSM_CHUNK_STEPS = 128

N_CHIPS = 4
BIG = (("w_in", 2), ("w_glu", 1), ("w_mem_kv", 1), ("w_br_ssm", 2), ("w_br_attn", 2), ("w_br_mem", 2), ("w_out", 1))
SMALL = ("norm_g", "mem_norm_g", "b_gate", "ssm_lambda_re", "ssm_lambda_im", "ssm_log_dt", "ssm_b_re", "ssm_b_im",
         "ssm_c_re", "ssm_c_im", "ssm_d", "b_glu", "rel_bias", "final_norm_g")
WEIGHTS = ("norm_g", "mem_norm_g", "w_in", "b_gate", "ssm_lambda_re", "ssm_lambda_im", "ssm_log_dt", "ssm_b_re",
           "ssm_b_im", "ssm_c_re", "ssm_c_im", "ssm_d", "w_glu", "b_glu", "w_mem_kv", "w_br_ssm", "w_br_attn",
           "w_br_mem", "w_out", "rel_bias", "final_norm_g")
INPUTS = ("x", "mem") + WEIGHTS + ("loss_target",) + tuple("m_" + n for n in WEIGHTS) + tuple("v_" + n for n in WEIGHTS)


def _params(*sem):
    return pltpu.CompilerParams(dimension_semantics=sem, vmem_limit_bytes=VMEM_LIMIT_BYTES)


def _pick(dim, pref, align):
    if dim <= pref:
        return dim
    t = pref - pref % align
    while t >= align:
        if dim % t == 0:
            return t
        t -= align
    return dim


def _sigmoid(v):
    return 1.0 / (1.0 + jnp.exp(-v))


def _silu_and_grad(z):
    s = _sigmoid(z)
    return z * s, s * (1.0 + z * (1.0 - s))


_GELU_C = math.sqrt(2.0 / math.pi)


def _gelu_and_grad(y):
    inner = _GELU_C * (y + 0.044715 * y * y * y)
    t = jnp.tanh(inner)
    g = 0.5 * y * (1.0 + t)
    dg = 0.5 * (1.0 + t) + 0.5 * y * (1.0 - t * t) * _GELU_C * (1.0 + 3.0 * 0.044715 * y * y)
    return g, dg


def _dot(a, b, dims):
    return lax.dot_general(a, b, (dims, ((), ())), preferred_element_type=F32)


NN = ((1,), (0,))
NT = ((1,), (1,))
TN = ((0,), (0,))


def _matmul(a, b, *, mode, name, out_dtype=F32, add=None, split_a=1, tm=1024, tn=768, tk=2304,
            b_lead=None, b_off=0, n_cols=None, stack=None):
    if mode == "tn":
        K, M = a.shape
    else:
        M, K = a.shape
    bshape = b.shape if b_lead is None else b.shape[1:]
    N = n_cols or (bshape[0] if mode == "nt" else bshape[1])
    if mode != "tn" and M >= 4 * tm:
        tm = 2 * tm
    tm = _pick(M, tm, LANES if mode == "tn" else SUBLANES)
    tn = _pick(math.gcd(N, b_off) if b_off else N, tn, LANES)
    tk = _pick(K, tk, LANES)
    nk = K // tk
    joff = b_off // tn
    dims = {"nn": NN, "nt": NT, "tn": TN}[mode]
    has_add = add is not None
    has_prev = stack is not None and stack[2] is not None

    def body(*refs):
        a_ref, b_ref = refs[:2]
        add_ref = refs[2] if has_add else None
        o_ref = refs[-2] if nk > 1 else refs[-1]
        k = pl.program_id(2)
        bv = b_ref[...].astype(BF16)
        if split_a > 1:
            rest = a_ref[...].astype(F32)
            part = 0.0
            for _ in range(split_a):
                piece = rest.astype(BF16)
                part = part + _dot(piece, bv, dims)
                rest = rest - piece.astype(F32)
        else:
            part = _dot(a_ref[...].astype(BF16), bv, dims)

        def finish(r):
            if has_add:
                r = r + add_ref[...]
            o_ref[...] = r.astype(out_dtype)

        if nk == 1:
            finish(part)
            return
        acc_ref = refs[-1]

        @pl.when(k == 0)
        def _():
            acc_ref[...] = part

        @pl.when((k > 0) & (k < nk - 1))
        def _():
            acc_ref[...] += part

        @pl.when(k == nk - 1)
        def _():
            finish(acc_ref[...] + part)

    a_spec = pl.BlockSpec((tk, tm), lambda i, j, k: (k, i)) if mode == "tn" else pl.BlockSpec((tm, tk), lambda i, j, k: (i, k))
    lead = () if b_lead is None else (b_lead,)
    lead_blk = () if b_lead is None else (None,)
    if mode == "nt":
        b_spec = pl.BlockSpec(lead_blk + (tn, tk), lambda i, j, k: lead + (j + joff, k))
    else:
        b_spec = pl.BlockSpec(lead_blk + (tk, tn), lambda i, j, k: lead + (k, j + joff))
    in_specs = [a_spec, b_spec]
    args = [a, b]
    if has_add:
        in_specs.append(pl.BlockSpec((tm, tn), lambda i, j, k: (i, j)))
        args.append(add)
    aliases = {}
    if stack is None:
        out_spec = pl.BlockSpec((tm, tn), lambda i, j, k: (i, j))
        out_shape = jax.ShapeDtypeStruct((M, N), out_dtype)
    else:
        layer, depth, prev = stack
        out_spec = pl.BlockSpec((None, tm, tn), lambda i, j, k: (layer, i, j))
        out_shape = jax.ShapeDtypeStruct((depth, M, N), out_dtype)
        if has_prev:
            in_specs.append(pl.BlockSpec(memory_space=pl.ANY))
            args.append(prev)
            aliases = {len(args) - 1: 0}
    return pl.pallas_call(
        body, name=name, grid=(M // tm, N // tn, nk), in_specs=in_specs, out_specs=out_spec, out_shape=out_shape,
        scratch_shapes=[pltpu.VMEM((tm, tn), F32)] if nk > 1 else [], input_output_aliases=aliases,
        compiler_params=_params("parallel", "parallel", "arbitrary"),
    )(*args)


def _rmsnorm(x, g, name):
    T, D = x.shape
    tm = _pick(T, 512, SUBLANES)

    def body(x_ref, g_ref, h_ref):
        xv = x_ref[...]
        r = lax.rsqrt(jnp.mean(xv * xv, axis=-1, keepdims=True) + EPS)
        h_ref[...] = (xv * r * g_ref[...]).astype(BF16)

    return pl.pallas_call(
        body, name=name, grid=(T // tm,),
        in_specs=[pl.BlockSpec((tm, D), lambda i: (i, 0)), pl.BlockSpec((1, D), lambda i: (0, 0))],
        out_specs=pl.BlockSpec((tm, D), lambda i: (i, 0)),
        out_shape=jax.ShapeDtypeStruct((T, D), BF16), compiler_params=_params("parallel"),
    )(x, g.reshape(1, D))


def _rmsnorm_bwd(x, g, dh, dres, name):
    T, D = x.shape
    tm = _pick(T, 512, SUBLANES)
    with_res = dres is not None

    def body(*refs):
        if with_res:
            x_ref, g_ref, dh_ref, dres_ref, dx_ref, dg_ref = refs
        else:
            x_ref, g_ref, dh_ref, dx_ref, dg_ref = refs
        xv = x_ref[...]
        dhv = dh_ref[...]
        r = lax.rsqrt(jnp.mean(xv * xv, axis=-1, keepdims=True) + EPS)
        dyg = dhv * g_ref[...]
        c = jnp.mean(dyg * xv, axis=-1, keepdims=True)
        dx = r * dyg - xv * (r * r * r) * c
        if with_res:
            dx = dx + dres_ref[...]
        dx_ref[...] = dx

        @pl.when(pl.program_id(0) == 0)
        def _():
            dg_ref[...] = jnp.zeros_like(dg_ref)

        dg_ref[...] += jnp.sum(dhv * xv * r, axis=0, keepdims=True)

    row = pl.BlockSpec((tm, D), lambda i: (i, 0))
    vec = pl.BlockSpec((1, D), lambda i: (0, 0))
    ins = [x, g.reshape(1, D), dh] + ([dres] if with_res else [])
    return pl.pallas_call(
        body, name=name, grid=(T // tm,), in_specs=[row, vec, row] + ([row] if with_res else []),
        out_specs=[row, vec],
        out_shape=[jax.ShapeDtypeStruct((T, D), F32), jax.ShapeDtypeStruct((1, D), F32)],
        compiler_params=_params("arbitrary"),
    )(*ins)


def _loss_head(x, g, target, name):
    T, D = x.shape
    tm = _pick(T, 512, SUBLANES)

    def body(x_ref, g_ref, t_ref, loss_ref, dx_ref, dg_ref):
        xv = x_ref[...]
        gv = g_ref[...]
        r = lax.rsqrt(jnp.mean(xv * xv, axis=-1, keepdims=True) + EPS)
        e = xv * r * gv - t_ref[...]
        dy = e * (1.0 / D)
        dyg = dy * gv
        c = jnp.mean(dyg * xv, axis=-1, keepdims=True)
        dx_ref[...] = r * dyg - xv * (r * r * r) * c

        @pl.when(pl.program_id(0) == 0)
        def _():
            loss_ref[...] = jnp.zeros_like(loss_ref)
            dg_ref[...] = jnp.zeros_like(dg_ref)

        loss_ref[...] += jnp.sum(e * e, axis=0, keepdims=True) * (0.5 / D)
        dg_ref[...] += jnp.sum(dy * xv * r, axis=0, keepdims=True)

    row = pl.BlockSpec((tm, D), lambda i: (i, 0))
    vec = pl.BlockSpec((1, D), lambda i: (0, 0))
    return pl.pallas_call(
        body, name=name, grid=(T // tm,), in_specs=[row, vec, row], out_specs=[vec, row, vec],
        out_shape=[jax.ShapeDtypeStruct((1, D), F32), jax.ShapeDtypeStruct((T, D), F32), jax.ShapeDtypeStruct((1, D), F32)],
        compiler_params=_params("arbitrary"),
    )(x, g.reshape(1, D), target)


def _ssm_disc_math(lre, lim, logdt, br, bi):
    dt = jnp.exp(logdt)
    mag = jnp.exp(lre * dt)
    ar = mag * jnp.cos(lim * dt)
    ai = mag * jnp.sin(lim * dt)
    den = lre * lre + lim * lim
    nr = ar - 1.0
    fr = (nr * lre + ai * lim) / den
    fi = (ai * lre - nr * lim) / den
    return ar, ai, fr[None] * br - fi[None] * bi, fr[None] * bi + fi[None] * br


def _ssm_disc(lre, lim, logdt, br, bi, name):
    def body(lre_ref, lim_ref, dt_ref, br_ref, bi_ref, ar_ref, ai_ref, bbr_ref, bbi_ref):
        ar, ai, bbr, bbi = _ssm_disc_math(lre_ref[...], lim_ref[...], dt_ref[...], br_ref[...], bi_ref[...])
        ar_ref[...] = ar
        ai_ref[...] = ai
        bbr_ref[...] = bbr
        bbi_ref[...] = bbi

    sd = jax.ShapeDtypeStruct
    return pl.pallas_call(
        body, name=name, out_shape=[sd(lre.shape, F32), sd(lre.shape, F32), sd(br.shape, F32), sd(br.shape, F32)],
    )(lre, lim, logdt, br, bi)


def _ssm_disc_bwd(lre, lim, logdt, br, bi, dar, dai, dbbr, dbbi, name):
    def body(lre_ref, lim_ref, dt_ref, br_ref, bi_ref, dar_ref, dai_ref, dbbr_ref, dbbi_ref,
             glre_ref, glim_ref, gdt_ref, gbr_ref, gbi_ref):
        _, vjp = jax.vjp(_ssm_disc_math, lre_ref[...], lim_ref[...], dt_ref[...], br_ref[...], bi_ref[...])
        glre, glim, gdt, gbr, gbi = vjp((dar_ref[...], dai_ref[...], dbbr_ref[...], dbbi_ref[...]))
        glre_ref[...] = glre
        glim_ref[...] = glim
        gdt_ref[...] = gdt
        gbr_ref[...] = gbr
        gbi_ref[...] = gbi

    sd = jax.ShapeDtypeStruct
    return pl.pallas_call(
        body, name=name,
        out_shape=[sd(lre.shape, F32), sd(lre.shape, F32), sd(logdt.shape, F32), sd(br.shape, F32), sd(br.shape, F32)],
    )(lre, lim, logdt, br, bi, dar, dai, dbbr, dbbi)


def _shift_segments(v, down):
    n = v.shape[0]
    rows = lax.broadcasted_iota(jnp.int32, v.shape, 0)
    if down:
        return jnp.where(rows >= 1, pltpu.roll(v, 1, 0), 0.0)
    return jnp.where(rows < n - 1, pltpu.roll(v, n - 1, 0), 0.0)


def _cpow(ar, ai, n):
    rr, ri = None, None
    pr, pi = ar, ai
    while n:
        if n & 1:
            if rr is None:
                rr, ri = pr, pi
            else:
                rr, ri = rr * pr - ri * pi, rr * pi + ri * pr
        n >>= 1
        if n:
            pr, pi = pr * pr - pi * pi, 2.0 * pr * pi
    return rr, ri


def _ssm_geometry(T, C):
    seg_steps = T // SSM_SEGMENTS
    kc = min(SSM_CHUNK_STEPS, seg_steps)
    return C // SSM_BLOCK_CH, seg_steps, kc, seg_steps // kc, SSM_SEGMENTS * kc


def _ssm_carries(u, dy, bmat, cmat, amat, *, reverse, name):
    src = dy if reverse else u
    T, C = src.shape
    nblk, seg_steps, kc, nchunk, rc = _ssm_geometry(T, C)
    half = bmat.shape[2] // 2

    def body(src_ref, w_ref, a_ref, out_ref, buf_ref, st_ref):
        c = pl.program_id(1)

        @pl.when(c == 0)
        def _():
            st_ref[...] = jnp.zeros_like(st_ref)

        if reverse:
            buf_ref[...] = _dot(src_ref[...].astype(BF16), w_ref[0], NT)
        else:
            buf_ref[...] = _dot(src_ref[...].astype(BF16), w_ref[0], NN)
        ar = a_ref[0, :, :half]
        ai = a_ref[0, :, half:]
        if reverse:
            ai = -ai

        def step(i, carry):
            xr, xi = carry
            k = (kc - 1 - i) if reverse else i
            row = pl.multiple_of(k * SUBLANES, SUBLANES)
            br = buf_ref[pl.ds(row, SUBLANES), :half]
            bi = buf_ref[pl.ds(row, SUBLANES), half:]
            return ar * xr - ai * xi + br, ar * xi + ai * xr + bi

        xr, xi = lax.fori_loop(0, kc, step, (st_ref[:, :half], st_ref[:, half:]), unroll=8)
        st_ref[:, :half] = xr
        st_ref[:, half:] = xi

        @pl.when(c == nchunk - 1)
        def _():
            pr, pi = _cpow(ar, ai, seg_steps)
            sr = jnp.zeros_like(xr)
            si = jnp.zeros_like(xi)
            for _ in range(SSM_SEGMENTS - 1):
                nr = xr + pr * sr - pi * si
                ni = xi + pr * si + pi * sr
                sr = _shift_segments(nr, not reverse)
                si = _shift_segments(ni, not reverse)
            out_ref[0, :, :half] = sr
            out_ref[0, :, half:] = si

    cidx = (lambda b, c: (nchunk - 1 - c, b)) if reverse else (lambda b, c: (c, b))
    w = cmat if reverse else bmat
    return pl.pallas_call(
        body, name=name, grid=(nblk, nchunk),
        in_specs=[pl.BlockSpec((rc, SSM_BLOCK_CH), cidx),
                  pl.BlockSpec((1,) + w.shape[1:], lambda b, c: (b, 0, 0)),
                  pl.BlockSpec((1, SUBLANES, 2 * half), lambda b, c: (b, 0, 0))],
        out_specs=pl.BlockSpec((1, SUBLANES, 2 * half), lambda b, c: (b, 0, 0)),
        out_shape=jax.ShapeDtypeStruct((nblk, SUBLANES, 2 * half), F32),
        scratch_shapes=[pltpu.VMEM((rc, 2 * half), F32), pltpu.VMEM((SUBLANES, 2 * half), F32)],
        compiler_params=_params("parallel", "arbitrary"),
    )(src, w, amat)


def _ssm_scan(u, bmat, cmat, amat, carries, dvec, name):
    T, C = u.shape
    nblk, seg_steps, kc, nchunk, rc = _ssm_geometry(T, C)
    half = bmat.shape[2] // 2

    def body(u_ref, b_ref, c_ref, a_ref, s_ref, d_ref, y_ref, x_ref, st_ref):
        c = pl.program_id(1)

        @pl.when(c == 0)
        def _():
            st_ref[...] = s_ref[0]

        uv = u_ref[...]
        x_ref[...] = _dot(uv.astype(BF16), b_ref[0], NN)
        ar = a_ref[0, :, :half]
        ai = a_ref[0, :, half:]

        def step(k, carry):
            xr, xi = carry
            row = pl.multiple_of(k * SUBLANES, SUBLANES)
            nr = ar * xr - ai * xi + x_ref[pl.ds(row, SUBLANES), :half]
            ni = ar * xi + ai * xr + x_ref[pl.ds(row, SUBLANES), half:]
            x_ref[pl.ds(row, SUBLANES), :half] = nr
            x_ref[pl.ds(row, SUBLANES), half:] = ni
            return nr, ni

        xr, xi = lax.fori_loop(0, kc, step, (st_ref[:, :half], st_ref[:, half:]), unroll=8)
        st_ref[:, :half] = xr
        st_ref[:, half:] = xi
        y_ref[...] = _dot(x_ref[...].astype(BF16), c_ref[0], NN) + d_ref[...] * uv

    return pl.pallas_call(
        body, name=name, grid=(nblk, nchunk),
        in_specs=[pl.BlockSpec((rc, SSM_BLOCK_CH), lambda b, c: (c, b)),
                  pl.BlockSpec((1,) + bmat.shape[1:], lambda b, c: (b, 0, 0)),
                  pl.BlockSpec((1,) + cmat.shape[1:], lambda b, c: (b, 0, 0)),
                  pl.BlockSpec((1, SUBLANES, 2 * half), lambda b, c: (b, 0, 0)),
                  pl.BlockSpec((1, SUBLANES, 2 * half), lambda b, c: (b, 0, 0)),
                  pl.BlockSpec((1, SSM_BLOCK_CH), lambda b, c: (0, b))],
        out_specs=[pl.BlockSpec((rc, SSM_BLOCK_CH), lambda b, c: (c, b)),
                   pl.BlockSpec((rc, 2 * half), lambda b, c: (c, b))],
        out_shape=[jax.ShapeDtypeStruct((T, C), F32), jax.ShapeDtypeStruct((T, nblk * 2 * half), F32)],
        scratch_shapes=[pltpu.VMEM((SUBLANES, 2 * half), F32)],
        compiler_params=_params("parallel", "arbitrary"),
    )(u, bmat, cmat, amat, carries, dvec)


def _ssm_scan_bwd(dy, u, xs, bmat, cmat, amat, carries, dvec, name):
    T, C = u.shape
    nblk, seg_steps, kc, nchunk, rc = _ssm_geometry(T, C)
    half = bmat.shape[2] // 2
    width = 2 * half

    def body(dy_ref, u_ref, x_ref, xp_ref, b_ref, c_ref, a_ref, s_ref, d_ref,
             du_ref, db_ref, dc_ref, da_ref, dd_ref, g_ref, st_ref, acc_ref):
        c = pl.program_id(1)

        @pl.when(c == 0)
        def _():
            st_ref[...] = s_ref[0]
            acc_ref[...] = jnp.zeros_like(acc_ref)
            db_ref[...] = jnp.zeros_like(db_ref)
            dc_ref[...] = jnp.zeros_like(dc_ref)
            dd_ref[...] = jnp.zeros_like(dd_ref)

        dyv = dy_ref[...]
        uv = u_ref[...]
        dyb = dyv.astype(BF16)
        g_ref[...] = _dot(dyb, c_ref[0], NT)
        ar = a_ref[0, :, :half]
        ai = a_ref[0, :, half:]

        def step(i, carry):
            gr, gi, sr, si = carry
            k = kc - 1 - i
            row = pl.multiple_of(k * SUBLANES, SUBLANES)
            nr = ar * gr + ai * gi + g_ref[pl.ds(row, SUBLANES), :half]
            ni = ar * gi - ai * gr + g_ref[pl.ds(row, SUBLANES), half:]
            g_ref[pl.ds(row, SUBLANES), :half] = nr
            g_ref[pl.ds(row, SUBLANES), half:] = ni
            prow = pl.multiple_of(jnp.maximum(k - 1, 0) * SUBLANES, SUBLANES)
            live = (k >= 1).astype(F32)
            xr = x_ref[pl.ds(prow, SUBLANES), :half] * live
            xi = x_ref[pl.ds(prow, SUBLANES), half:] * live
            return nr, ni, sr + xr * nr + xi * ni, si + xr * ni - xi * nr

        init = (st_ref[:, :half], st_ref[:, half:], acc_ref[:, :half], acc_ref[:, half:])
        gr, gi, sr, si = lax.fori_loop(0, kc, step, init, unroll=8)
        st_ref[:, :half] = gr
        st_ref[:, half:] = gi
        xpr = xp_ref[:, :half]
        xpi = xp_ref[:, half:]
        first = (c == nchunk - 1)
        xpr = jnp.where(first, _shift_segments(xpr, True), xpr)
        xpi = jnp.where(first, _shift_segments(xpi, True), xpi)
        acc_ref[:, :half] = sr + xpr * gr + xpi * gi
        acc_ref[:, half:] = si + xpr * gi - xpi * gr

        gb = g_ref[...].astype(BF16)
        du_ref[...] = _dot(gb, b_ref[0], NT) + dyv * d_ref[...]
        db_ref[0] += _dot(uv.astype(BF16), gb, TN)
        dc_ref[0] += _dot(dyb, x_ref[...].astype(BF16), TN)
        dd_ref[...] += jnp.sum(dyv * uv, axis=0, keepdims=True)

        @pl.when(c == nchunk - 1)
        def _():
            tot = jnp.sum(acc_ref[...], axis=0, keepdims=True)
            da_ref[0] = jnp.broadcast_to(tot, (SUBLANES, width))

    rev = lambda b, c: (nchunk - 1 - c, b)
    blk3 = lambda b, c: (b, 0, 0)
    prev_group = lambda b, c: (((nchunk - 1 - c) * kc - 1 + seg_steps) % seg_steps, b)
    sd = jax.ShapeDtypeStruct
    return pl.pallas_call(
        body, name=name, grid=(nblk, nchunk),
        in_specs=[pl.BlockSpec((rc, SSM_BLOCK_CH), rev), pl.BlockSpec((rc, SSM_BLOCK_CH), rev),
                  pl.BlockSpec((rc, width), rev), pl.BlockSpec((SUBLANES, width), prev_group),
                  pl.BlockSpec((1,) + bmat.shape[1:], blk3), pl.BlockSpec((1,) + cmat.shape[1:], blk3),
                  pl.BlockSpec((1, SUBLANES, width), blk3), pl.BlockSpec((1, SUBLANES, width), blk3),
                  pl.BlockSpec((1, SSM_BLOCK_CH), lambda b, c: (0, b))],
        out_specs=[pl.BlockSpec((rc, SSM_BLOCK_CH), rev), pl.BlockSpec((1, SSM_BLOCK_CH, width), blk3),
                   pl.BlockSpec((1, SSM_BLOCK_CH, width), blk3), pl.BlockSpec((1, SUBLANES, width), blk3),
                   pl.BlockSpec((1, SSM_BLOCK_CH), lambda b, c: (0, b))],
        out_shape=[sd((T, C), F32), sd((nblk, SSM_BLOCK_CH, width), F32), sd((nblk, SSM_BLOCK_CH, width), F32),
                   sd((nblk, SUBLANES, width), F32), sd((1, C), F32)],
        scratch_shapes=[pltpu.VMEM((rc, width), F32), pltpu.VMEM((SUBLANES, width), F32), pltpu.VMEM((SUBLANES, width), F32)],
        compiler_params=_params("parallel", "arbitrary"),
    )(dy, u, xs, xs, bmat, cmat, amat, carries, dvec)


SSM_STEPS = 256


def _ssm_tiles(ref, v, off, steps, n):
    return [ref[v, pl.ds(off + j, steps, stride=SUBLANES), :] for j in range(n)]


def _ssm_fwd(uz, bmat, cmat, art, ait, dvec, name):
    T = uz.shape[0]
    nblk, cb, width = bmat.shape
    C = nblk * cb
    half = width // 2
    nt = half // LANES
    npair = nblk // 2
    kc = min(SSM_STEPS, T)
    nchunk = T // kc

    def body(u_ref, b_ref, c_ref, ar_ref, ai_ref, d_ref, y_ref, xr_ref, xi_ref, sr_ref, si_ref):
        @pl.when(pl.program_id(0) == 0)
        def _():
            sr_ref[...] = jnp.zeros_like(sr_ref)
            si_ref[...] = jnp.zeros_like(si_ref)

        uv = u_ref[...]
        for b in range(nblk):
            bu = _dot(uv[:, b * cb:(b + 1) * cb].astype(BF16), b_ref[b], NN)
            v, off = b // 2, nt * (b % 2)
            for j in range(nt):
                xr_ref[v, pl.ds(off + j, kc, stride=SUBLANES), :] = bu[:, j * LANES:(j + 1) * LANES]
                xi_ref[v, pl.ds(off + j, kc, stride=SUBLANES), :] = bu[:, half + j * LANES:half + (j + 1) * LANES]
        ars = [ar_ref[v] for v in range(npair)]
        ais = [ai_ref[v] for v in range(npair)]

        def step(k, carry):
            row = pl.ds(pl.multiple_of(k * SUBLANES, SUBLANES), SUBLANES)
            out = []
            for v in range(npair):
                xr, xi = carry[2 * v], carry[2 * v + 1]
                nr = ars[v] * xr - ais[v] * xi + xr_ref[v, row, :]
                ni = ars[v] * xi + ais[v] * xr + xi_ref[v, row, :]
                xr_ref[v, row, :] = nr
                xi_ref[v, row, :] = ni
                out += [nr, ni]
            return tuple(out)

        init = tuple(ref[v] for v in range(npair) for ref in (sr_ref, si_ref))
        fin = lax.fori_loop(0, kc, step, init, unroll=4)
        for v in range(npair):
            sr_ref[v] = fin[2 * v]
            si_ref[v] = fin[2 * v + 1]
        for b in range(nblk):
            v, off = b // 2, nt * (b % 2)
            xb = jnp.concatenate(_ssm_tiles(xr_ref, v, off, kc, nt) + _ssm_tiles(xi_ref, v, off, kc, nt), axis=1)
            cols = slice(b * cb, (b + 1) * cb)
            y_ref[:, cols] = _dot(xb.astype(BF16), c_ref[b], NN) + d_ref[:, cols] * uv[:, cols]

    whole = lambda a: pl.BlockSpec(a.shape, lambda c: (0,) * a.ndim)
    st = pl.BlockSpec((npair, kc * SUBLANES, LANES), lambda c: (0, c, 0))
    sd = jax.ShapeDtypeStruct
    return pl.pallas_call(
        body, name=name, grid=(nchunk,),
        in_specs=[pl.BlockSpec((kc, C), lambda c: (c, 0)), whole(bmat), whole(cmat), whole(art), whole(ait), whole(dvec)],
        out_specs=[pl.BlockSpec((kc, C), lambda c: (c, 0)), st, st],
        out_shape=[sd((T, C), F32), sd((npair, T * SUBLANES, LANES), F32), sd((npair, T * SUBLANES, LANES), F32)],
        scratch_shapes=[pltpu.VMEM((npair, SUBLANES, LANES), F32), pltpu.VMEM((npair, SUBLANES, LANES), F32)],
        compiler_params=_params("arbitrary"),
    )(uz, bmat, cmat, art, ait, dvec)


def _ssm_bwd(dy, uz, xr, xi, bmat, cmat, art, ait, dvec, name):
    T = uz.shape[0]
    nblk, cb, width = bmat.shape
    C = nblk * cb
    half = width // 2
    nt = half // LANES
    npair = nblk // 2
    kc = min(SSM_STEPS, T)
    nchunk = T // kc

    def body(dy_ref, u_ref, xr_ref, xi_ref, xpr_ref, xpi_ref, b_ref, c_ref, ar_ref, ai_ref, d_ref,
             du_ref, db_ref, dc_ref, dar_ref, dai_ref, dd_ref, gr_ref, gi_ref, sr_ref, si_ref):
        c = pl.program_id(0)

        @pl.when(c == 0)
        def _():
            for ref in (sr_ref, si_ref, db_ref, dc_ref, dar_ref, dai_ref, dd_ref):
                ref[...] = jnp.zeros_like(ref)

        dyv = dy_ref[...]
        uv = u_ref[...]
        for b in range(nblk):
            dx = _dot(dyv[:, b * cb:(b + 1) * cb].astype(BF16), c_ref[b], NT)
            v, off = b // 2, nt * (b % 2)
            for j in range(nt):
                gr_ref[v, pl.ds(off + j, kc, stride=SUBLANES), :] = dx[:, j * LANES:(j + 1) * LANES]
                gi_ref[v, pl.ds(off + j, kc, stride=SUBLANES), :] = dx[:, half + j * LANES:half + (j + 1) * LANES]
        ars = [ar_ref[v] for v in range(npair)]
        ais = [ai_ref[v] for v in range(npair)]

        def pair_update(v, gr, gi, row):
            nr = ars[v] * gr + ais[v] * gi + gr_ref[v, row, :]
            ni = ars[v] * gi - ais[v] * gr + gi_ref[v, row, :]
            gr_ref[v, row, :] = nr
            gi_ref[v, row, :] = ni
            return nr, ni

        def step(i, carry):
            k = kc - 1 - i
            row = pl.ds(pl.multiple_of(k * SUBLANES, SUBLANES), SUBLANES)
            prow = pl.ds(pl.multiple_of((k - 1) * SUBLANES, SUBLANES), SUBLANES)
            out = []
            for v in range(npair):
                gr, gi, sr, si = carry[4 * v:4 * v + 4]
                nr, ni = pair_update(v, gr, gi, row)
                pr, pi = xr_ref[v, prow, :], xi_ref[v, prow, :]
                out += [nr, ni, sr + pr * nr + pi * ni, si + pr * ni - pi * nr]
            return tuple(out)

        init = tuple(ref[v] for v in range(npair) for ref in (sr_ref, si_ref, dar_ref, dai_ref))
        mid = lax.fori_loop(0, kc - 1, step, init, unroll=5 if (kc - 1) % 5 == 0 else 1)
        live = (c < nchunk - 1).astype(F32)
        row0 = pl.ds(0, SUBLANES)
        for v in range(npair):
            gr, gi, sr, si = mid[4 * v:4 * v + 4]
            nr, ni = pair_update(v, gr, gi, row0)
            pr, pi = xpr_ref[v] * live, xpi_ref[v] * live
            sr_ref[v] = nr
            si_ref[v] = ni
            dar_ref[v] = sr + pr * nr + pi * ni
            dai_ref[v] = si + pr * ni - pi * nr
        for b in range(nblk):
            v, off = b // 2, nt * (b % 2)
            cols = slice(b * cb, (b + 1) * cb)
            gb = jnp.concatenate(_ssm_tiles(gr_ref, v, off, kc, nt) + _ssm_tiles(gi_ref, v, off, kc, nt), axis=1).astype(BF16)
            xb = jnp.concatenate(_ssm_tiles(xr_ref, v, off, kc, nt) + _ssm_tiles(xi_ref, v, off, kc, nt), axis=1).astype(BF16)
            du_ref[:, cols] = _dot(gb, b_ref[b], NT) + dyv[:, cols] * d_ref[:, cols]
            db_ref[b] += _dot(uv[:, cols].astype(BF16), gb, TN)
            dc_ref[b] += _dot(dyv[:, cols].astype(BF16), xb, TN)
        dd_ref[...] += jnp.sum(dyv * uv, axis=0, keepdims=True)

    whole = lambda a: pl.BlockSpec(a.shape, lambda c: (0,) * a.ndim)
    rev = lambda c: (nchunk - 1 - c, 0)
    st = pl.BlockSpec((npair, kc * SUBLANES, LANES), lambda c: (0, nchunk - 1 - c, 0))
    stp = pl.BlockSpec((npair, SUBLANES, LANES), lambda c: (0, jnp.maximum((nchunk - 1 - c) * kc - 1, 0), 0))
    acc = lambda shape: pl.BlockSpec(shape, lambda c: (0,) * len(shape))
    sd = jax.ShapeDtypeStruct
    pair_shape = (npair, SUBLANES, LANES)
    return pl.pallas_call(
        body, name=name, grid=(nchunk,),
        in_specs=[pl.BlockSpec((kc, C), rev), pl.BlockSpec((kc, C), rev), st, st, stp, stp, whole(bmat), whole(cmat),
                  whole(art), whole(ait), whole(dvec)],
        out_specs=[pl.BlockSpec((kc, C), rev), acc(bmat.shape), acc(bmat.shape), acc(pair_shape), acc(pair_shape), acc((1, C))],
        out_shape=[sd((T, C), F32), sd(bmat.shape, F32), sd(bmat.shape, F32), sd(pair_shape, F32), sd(pair_shape, F32),
                   sd((1, C), F32)],
        scratch_shapes=[pltpu.VMEM((npair, kc * SUBLANES, LANES), F32), pltpu.VMEM((npair, kc * SUBLANES, LANES), F32),
                        pltpu.VMEM(pair_shape, F32), pltpu.VMEM(pair_shape, F32)],
        compiler_params=_params("arbitrary"),
    )(dy, uz, xr, xi, xr, xi, bmat, cmat, art, ait, dvec)


def _ssm_post(y, uz, w_glu, b_glu, name):
    T, C = y.shape
    tm = _pick(T, 512, SUBLANES)

    def body(y_ref, z_ref, w_ref, b_ref, o_ref, a_ref):
        a, _ = _gelu_and_grad(y_ref[...])
        ab = a.astype(BF16)
        sg = _sigmoid(_dot(ab, w_ref[...], NN) + b_ref[...])
        sz, _ = _silu_and_grad(z_ref[...])
        o_ref[...] = (a * sg * sz).astype(BF16)
        a_ref[...] = ab

    row = pl.BlockSpec((tm, C), lambda i: (i, 0))
    return pl.pallas_call(
        body, name=name, grid=(T // tm,),
        in_specs=[row, pl.BlockSpec((tm, C), lambda i: (i, 1)), pl.BlockSpec((C, C), lambda i: (0, 0)),
                  pl.BlockSpec((1, C), lambda i: (0, 0))],
        out_specs=[row, row], out_shape=[jax.ShapeDtypeStruct((T, C), BF16)] * 2, compiler_params=_params("parallel"),
    )(y, uz, w_glu, b_glu.reshape(1, C))


def _ssm_post_bwd(do, y, uz, w_glu, b_glu, name):
    T, C = y.shape
    tm = _pick(T, 512, SUBLANES)

    def body(do_ref, y_ref, z_ref, w_ref, b_ref, dy_ref, dz_ref, ds_ref, db_ref):
        dov = do_ref[...]
        a, da_dy = _gelu_and_grad(y_ref[...])
        sg = _sigmoid(_dot(a.astype(BF16), w_ref[...], NN) + b_ref[...])
        sz, dsz = _silu_and_grad(z_ref[...])
        yg = a * sg
        dz_ref[...] = (dov * yg * dsz).astype(BF16)
        dyg = dov * sz
        ds = dyg * a * sg * (1.0 - sg)
        dsb = ds.astype(BF16)
        ds_ref[...] = dsb
        da = dyg * sg + _dot(dsb, w_ref[...], NT)
        dy_ref[...] = da * da_dy

        @pl.when(pl.program_id(0) == 0)
        def _():
            db_ref[...] = jnp.zeros_like(db_ref)

        db_ref[...] += jnp.sum(ds, axis=0, keepdims=True)

    row = pl.BlockSpec((tm, C), lambda i: (i, 0))
    vec = pl.BlockSpec((1, C), lambda i: (0, 0))
    sd = jax.ShapeDtypeStruct
    return pl.pallas_call(
        body, name=name, grid=(T // tm,),
        in_specs=[row, row, pl.BlockSpec((tm, C), lambda i: (i, 1)), pl.BlockSpec((C, C), lambda i: (0, 0)), vec],
        out_specs=[row, row, row, vec],
        out_shape=[sd((T, C), F32), sd((T, C), BF16), sd((T, C), BF16), sd((1, C), F32)],
        compiler_params=_params("arbitrary"),
    )(do, y, uz, w_glu, b_glu.reshape(1, C))


def _rel_bucket(dist):
    n = jnp.maximum(dist, 0)
    max_exact = NUM_BUCKETS // 2
    n_f = jnp.maximum(n, 1).astype(F32)
    large = max_exact + (jnp.log(n_f / max_exact) / math.log(REL_MAX_DISTANCE / max_exact)
                         * (NUM_BUCKETS - max_exact)).astype(jnp.int32)
    large = jnp.minimum(large, NUM_BUCKETS - 1)
    return jnp.where(n < max_exact, n, large)


def _band_tables():
    qi = jnp.arange(ATTN_BLOCK)[:, None]
    kj = jnp.arange(2 * ATTN_BLOCK)[None, :]
    delta = ATTN_BLOCK + qi - kj
    buckets, bands = [], []
    for window, dilation in ATTN_CONFIGS:
        bands.append((delta >= 0) & (delta <= window // dilation))
        buckets.append(_rel_bucket(jnp.maximum(delta, 0) * dilation))
    return jnp.stack(buckets), jnp.stack(bands)


def _attn_blocks_per_residue(T):
    return [T // (ATTN_BLOCK * d) for _, d in ATTN_CONFIGS]


ATTN_UNITS = 4


def _attn_tile(T, r):
    nq = max(1, ATTN_UNITS // r)
    rows = ATTN_BLOCK * r * nq
    return nq, rows, T // rows


def _attn_units(r, nq, chunk):
    if r >= ATTN_UNITS:
        return [(chunk * ATTN_UNITS + i, None) for i in range(ATTN_UNITS)]
    units = []
    for j in range(nq):
        for s in range(r):
            units.append((ATTN_BLOCK * j * r + s, ATTN_BLOCK * (j - 1) * r + s if j else None))
    return units


def _rows(start, r):
    return pl.ds(start, ATTN_BLOCK, stride=r) if r > 1 else pl.ds(start, ATTN_BLOCK)


def _attn_group_fwd(qkv, biasm, g, name):
    T = qkv.shape[0]
    r = ATTN_CONFIGS[g][1]
    B, hd = ATTN_BLOCK, ATTN_HEAD_DIM
    nq, rows, ntiles = _attn_tile(T, r)
    nchunks = max(1, r // ATTN_UNITS)
    last_prev = B * (nq - 1) * r
    scale = hd ** -0.5
    tiles_per_tensor = 3 * HEADS_PER_GROUP * hd // LANES

    def body(q_ref, kc_ref, kp_ref, vc_ref, vp_ref, bias_ref, o_ref, lse_ref, s_ref, p_ref):
        n = pl.program_id(1)
        lane = lax.broadcasted_iota(jnp.int32, (1, LANES), 1)
        col = lax.broadcasted_iota(jnp.int32, (1, 2 * B), 1)
        masks = [lane < hd, lane >= hd]
        first_pen = jnp.where((col < B) & (n == 0), NEG_INF, 0.0)

        def chunk_body(chunk):
            units = _attn_units(r, nq, chunk)

            def keys(cur_ref, prev_ref, cs, ps):
                prev = prev_ref[_rows(last_prev + (cs if r >= ATTN_UNITS else cs % r), r), :] if ps is None else cur_ref[_rows(ps, r), :]
                return jnp.concatenate([prev, cur_ref[_rows(cs, r), :]], axis=0).astype(BF16)

            for u, (cs, ps) in enumerate(units):
                qv = q_ref[_rows(cs, r), :]
                kw = keys(kc_ref, kp_ref, cs, ps)
                for hh in range(2):
                    s_ref[2 * u + hh] = _dot(jnp.where(masks[hh], qv, 0.0).astype(BF16), kw, NT)
            for u, (cs, ps) in enumerate(units):
                lses = []
                for hh in range(2):
                    s = s_ref[2 * u + hh] * scale + bias_ref[hh]
                    if ps is None:
                        s = s + first_pen
                    m = jnp.max(s, axis=-1, keepdims=True)
                    p = jnp.exp(s - m)
                    l = jnp.sum(p, axis=-1, keepdims=True)
                    p_ref[2 * u + hh] = (p / l).astype(BF16)
                    lses.append(m + jnp.log(l))
                lse_ref[_rows(cs, r), :] = jnp.where(masks[0], lses[0], lses[1])
            for u, (cs, ps) in enumerate(units):
                vw = keys(vc_ref, vp_ref, cs, ps)
                o_ref[_rows(cs, r), :] = (_dot(p_ref[2 * u], jnp.where(masks[0], vw, 0), NN)
                                          + _dot(p_ref[2 * u + 1], jnp.where(masks[1], vw, 0), NN))

        if nchunks == 1:
            chunk_body(0)
        else:
            pl.loop(0, nchunks)(chunk_body)

    def cur(t):
        return pl.BlockSpec((rows, LANES), lambda hf, n: (n, t * tiles_per_tensor + 2 * g + hf))

    def prev(t):
        return pl.BlockSpec((rows, LANES), lambda hf, n: (jnp.maximum(n - 1, 0), t * tiles_per_tensor + 2 * g + hf))

    out = pl.BlockSpec((rows, LANES), lambda hf, n: (n, hf))
    sd = jax.ShapeDtypeStruct((T, 2 * LANES), F32)
    return pl.pallas_call(
        body, name=name, grid=(2, ntiles),
        in_specs=[cur(0), cur(1), prev(1), cur(2), prev(2), pl.BlockSpec((None, 2, B, 2 * B), lambda hf, n: (g, hf, 0, 0))],
        out_specs=[out, out], out_shape=[sd, sd],
        scratch_shapes=[pltpu.VMEM((2 * ATTN_UNITS, B, 2 * B), F32), pltpu.VMEM((2 * ATTN_UNITS, B, 2 * B), BF16)],
        compiler_params=_params("parallel", "parallel"),
    )(qkv, qkv, qkv, qkv, qkv, biasm)


def _attn_group_bwd(qkv, do, dvec, lse, biasm, g, name):
    T = qkv.shape[0]
    r = ATTN_CONFIGS[g][1]
    B, hd = ATTN_BLOCK, ATTN_HEAD_DIM
    nq, rows, ntiles = _attn_tile(T, r)
    nchunks = max(1, r // ATTN_UNITS)
    last_prev = B * (nq - 1) * r
    scale = hd ** -0.5
    tiles_per_tensor = 3 * HEADS_PER_GROUP * hd // LANES

    def body(q_ref, kc_ref, kp_ref, vc_ref, vp_ref, do_ref, dv_ref, lse_ref, bias_ref,
             dq_ref, dk_ref, dvo_ref, dbias_ref, ck_ref, cv_ref, ak_ref, av_ref, s_ref, dp_ref, p_ref, ds_ref):
        n = pl.program_id(1)
        lane = lax.broadcasted_iota(jnp.int32, (1, LANES), 1)
        col = lax.broadcasted_iota(jnp.int32, (1, 2 * B), 1)
        masks = [lane < hd, lane >= hd]
        first_pen = jnp.where((col < B) & (n == 0), NEG_INF, 0.0)

        @pl.when(n == 0)
        def _():
            dbias_ref[...] = jnp.zeros_like(dbias_ref)
            ck_ref[...] = jnp.zeros_like(ck_ref)
            cv_ref[...] = jnp.zeros_like(cv_ref)

        def chunk_body(chunk):
            units = _attn_units(r, nq, chunk)

            def prev_rows(cs):
                return _rows(last_prev + (cs if r >= ATTN_UNITS else cs % r), r)

            def keys(cur_ref, prev_ref, cs, ps):
                prev = prev_ref[prev_rows(cs), :] if ps is None else cur_ref[_rows(ps, r), :]
                return jnp.concatenate([prev, cur_ref[_rows(cs, r), :]], axis=0).astype(BF16)

            for u, (cs, ps) in enumerate(units):
                qv = q_ref[_rows(cs, r), :]
                dov = do_ref[_rows(cs, r), :]
                kw = keys(kc_ref, kp_ref, cs, ps)
                vw = keys(vc_ref, vp_ref, cs, ps)
                for hh in range(2):
                    s_ref[2 * u + hh] = _dot(jnp.where(masks[hh], qv, 0.0).astype(BF16), kw, NT)
                    dp_ref[2 * u + hh] = _dot(jnp.where(masks[hh], dov, 0.0).astype(BF16), vw, NT)
            for u, (cs, ps) in enumerate(units):
                lse_t = lse_ref[_rows(cs, r), :]
                dv_t = dv_ref[_rows(cs, r), :]
                for hh in range(2):
                    lo = hh * hd
                    s = s_ref[2 * u + hh] * scale + bias_ref[hh]
                    if ps is None:
                        s = s + first_pen
                    p = jnp.exp(s - lse_t[:, lo:lo + 1])
                    ds = p * (dp_ref[2 * u + hh] + dv_t[:, lo:lo + 1])
                    dbias_ref[hh] += ds
                    p_ref[2 * u + hh] = p.astype(BF16)
                    ds_ref[2 * u + hh] = ds.astype(BF16)
            for u, (cs, ps) in enumerate(units):
                qv = q_ref[_rows(cs, r), :]
                dov = do_ref[_rows(cs, r), :]
                kw = keys(kc_ref, kp_ref, cs, ps)
                dq, dkw, dvw = 0.0, 0.0, 0.0
                for hh in range(2):
                    dsb = ds_ref[2 * u + hh]
                    dq = dq + _dot(dsb, jnp.where(masks[hh], kw, 0), NN)
                    dkw = dkw + _dot(dsb, jnp.where(masks[hh], qv, 0.0).astype(BF16), TN)
                    dvw = dvw + _dot(p_ref[2 * u + hh], jnp.where(masks[hh], dov, 0.0).astype(BF16), TN)
                dq_ref[_rows(cs, r), :] = dq * scale
                ak_ref[_rows(cs, r), :] = dkw[B:] * scale
                av_ref[_rows(cs, r), :] = dvw[B:]
                if ps is None:
                    ck_ref[prev_rows(cs), :] += dkw[:B] * scale
                    cv_ref[prev_rows(cs), :] += dvw[:B]
                else:
                    ak_ref[_rows(ps, r), :] += dkw[:B] * scale
                    av_ref[_rows(ps, r), :] += dvw[:B]

        @pl.when(n < ntiles)
        def _():
            if nchunks == 1:
                chunk_body(0)
            else:
                pl.loop(0, nchunks)(chunk_body)

        dk_ref[...] = ck_ref[...].astype(BF16)
        dvo_ref[...] = cv_ref[...].astype(BF16)
        ck_ref[...] = ak_ref[...]
        cv_ref[...] = av_ref[...]

    last = ntiles - 1

    def cur(t):
        return pl.BlockSpec((rows, LANES), lambda hf, n: (jnp.minimum(n, last), t * tiles_per_tensor + 2 * g + hf))

    def prev(t):
        return pl.BlockSpec((rows, LANES), lambda hf, n: (jnp.clip(n - 1, 0, last), t * tiles_per_tensor + 2 * g + hf))

    nat = pl.BlockSpec((rows, LANES), lambda hf, n: (jnp.minimum(n, last), hf))
    nat_prev = pl.BlockSpec((rows, LANES), lambda hf, n: (jnp.clip(n - 1, 0, last), hf))
    tab = pl.BlockSpec((None, 2, B, 2 * B), lambda hf, n: (g, hf, 0, 0))
    dtab = pl.BlockSpec((2, B, 2 * B), lambda hf, n: (hf, 0, 0))
    sd = jax.ShapeDtypeStruct
    vm = pltpu.VMEM
    return pl.pallas_call(
        body, name=name, grid=(2, ntiles + 1),
        in_specs=[cur(0), cur(1), prev(1), cur(2), prev(2), nat, nat, nat, tab],
        out_specs=[nat, nat_prev, nat_prev, dtab],
        out_shape=[sd((T, 2 * LANES), F32), sd((T, 2 * LANES), BF16), sd((T, 2 * LANES), BF16),
                   sd((HEADS_PER_GROUP, B, 2 * B), F32)],
        scratch_shapes=[vm((rows, LANES), F32), vm((rows, LANES), F32), vm((rows, LANES), F32), vm((rows, LANES), F32),
                        vm((2 * ATTN_UNITS, B, 2 * B), F32), vm((2 * ATTN_UNITS, B, 2 * B), F32),
                        vm((2 * ATTN_UNITS, B, 2 * B), BF16), vm((2 * ATTN_UNITS, B, 2 * B), BF16)],
        compiler_params=_params("parallel", "arbitrary"),
    )(qkv, qkv, qkv, qkv, qkv, do, dvec, lse, biasm)


def _attn_fwd(q, k, v, biasm, name):
    ng, T, gw = q.shape
    hd = ATTN_HEAD_DIM
    nh = gw // hd
    nblk = T // ATTN_BLOCK
    nbs = _attn_blocks_per_residue(T)
    scale = hd ** -0.5
    B = ATTN_BLOCK

    def body(q_ref, kc_ref, kp_ref, vc_ref, vp_ref, bias_ref, o_ref, lse_ref, s_ref, p_ref):
        g = pl.program_id(0)
        b = pl.program_id(1)
        nb = jnp.where(g == 0, nbs[0], jnp.where(g == 1, nbs[1], nbs[2]))
        no_prev = (b % nb) == 0
        col = lax.broadcasted_iota(jnp.int32, (1, 2 * B), 1)
        pen = jnp.where((col < B) & no_prev, NEG_INF, 0.0)
        heads = [slice(h * hd, (h + 1) * hd) for h in range(nh)]
        for h, hs in enumerate(heads):
            kw = jnp.concatenate([kp_ref[0, :, hs], kc_ref[0, :, hs]], axis=0)
            s_ref[h] = _dot(q_ref[0, :, hs], kw, NT)
        for h, hs in enumerate(heads):
            s = s_ref[h] * scale + bias_ref[0, h] + pen
            m = jnp.max(s, axis=-1, keepdims=True)
            p = jnp.exp(s - m)
            l = jnp.sum(p, axis=-1, keepdims=True)
            p_ref[h] = (p / l).astype(BF16)
            lse_ref[0, :, hs] = jnp.broadcast_to(m + jnp.log(l), (B, hd))
        for h, hs in enumerate(heads):
            vw = jnp.concatenate([vp_ref[0, :, hs], vc_ref[0, :, hs]], axis=0)
            o_ref[0, :, hs] = _dot(p_ref[h], vw, NN)

    cur = pl.BlockSpec((1, B, gw), lambda g, b: (g, b, 0))
    prev = pl.BlockSpec((1, B, gw), lambda g, b: (g, jnp.maximum(b - 1, 0), 0))
    return pl.pallas_call(
        body, name=name, grid=(ng, nblk),
        in_specs=[cur, cur, prev, cur, prev, pl.BlockSpec((1, nh, B, 2 * B), lambda g, b: (g, 0, 0, 0))],
        out_specs=[cur, cur], out_shape=[jax.ShapeDtypeStruct(q.shape, F32)] * 2,
        scratch_shapes=[pltpu.VMEM((nh, B, 2 * B), F32), pltpu.VMEM((nh, B, 2 * B), BF16)],
        compiler_params=_params("parallel", "parallel"),
    )(q, k, k, v, v, biasm)


def _attn_bwd(q, k, v, do, dvec, lse, biasm, name):
    ng, T, gw = q.shape
    hd = ATTN_HEAD_DIM
    nh = gw // hd
    nblk = T // ATTN_BLOCK
    nbs = _attn_blocks_per_residue(T)
    scale = hd ** -0.5
    B = ATTN_BLOCK

    def body(q_ref, kc_ref, kp_ref, vc_ref, vp_ref, do_ref, dv_ref, lse_ref, bias_ref,
             dq_ref, dk_ref, dvo_ref, dbias_ref, ck_ref, cv_ref, s_ref, dp_ref, p_ref, ds_ref):
        g = pl.program_id(0)
        b = pl.program_id(1)
        nb = jnp.where(g == 0, nbs[0], jnp.where(g == 1, nbs[1], nbs[2]))
        no_prev = (b % nb) == 0

        @pl.when(b == 0)
        def _():
            dbias_ref[...] = jnp.zeros_like(dbias_ref)
            ck_ref[...] = jnp.zeros_like(ck_ref)
            cv_ref[...] = jnp.zeros_like(cv_ref)

        @pl.when(b < nblk)
        def _():
            col = lax.broadcasted_iota(jnp.int32, (1, 2 * B), 1)
            pen = jnp.where((col < B) & no_prev, NEG_INF, 0.0)
            heads = [slice(h * hd, (h + 1) * hd) for h in range(nh)]
            for h, hs in enumerate(heads):
                kw = jnp.concatenate([kp_ref[0, :, hs], kc_ref[0, :, hs]], axis=0)
                vw = jnp.concatenate([vp_ref[0, :, hs], vc_ref[0, :, hs]], axis=0)
                s_ref[h] = _dot(q_ref[0, :, hs], kw, NT)
                dp_ref[h] = _dot(do_ref[0, :, hs], vw, NT)
            for h, hs in enumerate(heads):
                lse_col = lse_ref[0, :, h * hd:h * hd + 1]
                d_col = dv_ref[0, :, h * hd:h * hd + 1]
                p = jnp.exp(s_ref[h] * scale + bias_ref[0, h] + pen - lse_col)
                ds = p * (dp_ref[h] + d_col)
                dbias_ref[0, h] += ds
                p_ref[h] = p.astype(BF16)
                ds_ref[h] = ds.astype(BF16)
            for h, hs in enumerate(heads):
                qh = q_ref[0, :, hs]
                kw = jnp.concatenate([kp_ref[0, :, hs], kc_ref[0, :, hs]], axis=0)
                dq_ref[0, :, hs] = (_dot(ds_ref[h], kw, NN) * scale).astype(BF16)
                dkw = _dot(ds_ref[h], qh, TN) * scale
                dvw = _dot(p_ref[h], do_ref[0, :, hs], TN)
                dk_ref[0, :, hs] = (ck_ref[:, hs] + dkw[:B]).astype(BF16)
                dvo_ref[0, :, hs] = (cv_ref[:, hs] + dvw[:B]).astype(BF16)
                ck_ref[:, hs] = dkw[B:]
                cv_ref[:, hs] = dvw[B:]

        @pl.when(b == nblk)
        def _():
            dk_ref[0] = ck_ref[...].astype(BF16)
            dvo_ref[0] = cv_ref[...].astype(BF16)

    last = nblk - 1
    cur = pl.BlockSpec((1, B, gw), lambda g, b: (g, jnp.minimum(b, last), 0))
    prev = pl.BlockSpec((1, B, gw), lambda g, b: (g, jnp.clip(b - 1, 0, last), 0))
    tab = pl.BlockSpec((1, nh, B, 2 * B), lambda g, b: (g, 0, 0, 0))
    sd = jax.ShapeDtypeStruct
    return pl.pallas_call(
        body, name=name, grid=(ng, nblk + 1),
        in_specs=[cur, cur, prev, cur, prev, cur, cur, cur, tab],
        out_specs=[cur, prev, prev, tab],
        out_shape=[sd(q.shape, BF16), sd(q.shape, BF16), sd(q.shape, BF16), sd(biasm.shape, F32)],
        scratch_shapes=[pltpu.VMEM((B, gw), F32), pltpu.VMEM((B, gw), F32), pltpu.VMEM((nh, B, 2 * B), F32),
                        pltpu.VMEM((nh, B, 2 * B), F32), pltpu.VMEM((nh, B, 2 * B), BF16), pltpu.VMEM((nh, B, 2 * B), BF16)],
        compiler_params=_params("parallel", "arbitrary"),
    )(q, k, k, v, v, do, dvec, lse, biasm)


def _attn_mix(os, lses, z, name):
    T, gw = os[0].shape
    C = z.shape[1]
    tm = _pick(T, 512, SUBLANES)

    def body(o0_ref, o1_ref, o2_ref, l0_ref, l1_ref, l2_ref, z_ref, out_ref):
        ls = [l0_ref[...], l1_ref[...], l2_ref[...]]
        mx = jnp.maximum(jnp.maximum(ls[0], ls[1]), ls[2])
        es = [jnp.exp(l - mx) for l in ls]
        den = es[0] + es[1] + es[2]
        for i, o_ref in enumerate((o0_ref, o1_ref, o2_ref)):
            sz, _ = _silu_and_grad(z_ref[:, i * gw:(i + 1) * gw])
            out_ref[:, i * gw:(i + 1) * gw] = (o_ref[...] * (es[i] / den) * sz).astype(BF16)

    row = pl.BlockSpec((tm, C), lambda i: (i, 0))
    grp = pl.BlockSpec((tm, gw), lambda i: (i, 0))
    return pl.pallas_call(
        body, name=name, grid=(T // tm,), in_specs=[grp] * 6 + [row], out_specs=row,
        out_shape=jax.ShapeDtypeStruct((T, C), BF16), compiler_params=_params("parallel"),
    )(*os, *lses, z)


def _attn_mix_bwd(dout, os, lses, z, name):
    T, gw = os[0].shape
    C = z.shape[1]
    tm = _pick(T, 512, SUBLANES)
    head_of = np.arange(gw) // ATTN_HEAD_DIM
    ones = jnp.asarray(head_of[:, None] == head_of[None, :], BF16)

    def body(dout_ref, o0_ref, o1_ref, o2_ref, l0_ref, l1_ref, l2_ref, z_ref, ones_ref,
             dz_ref, do0_ref, do1_ref, do2_ref, dv0_ref, dv1_ref, dv2_ref):
        ls = [l0_ref[...], l1_ref[...], l2_ref[...]]
        mx = jnp.maximum(jnp.maximum(ls[0], ls[1]), ls[2])
        es = [jnp.exp(l - mx) for l in ls]
        den = es[0] + es[1] + es[2]
        alphas, ebar = [], 0.0
        for i, (o_ref, do_ref) in enumerate(((o0_ref, do0_ref), (o1_ref, do1_ref), (o2_ref, do2_ref))):
            sl = slice(i * gw, (i + 1) * gw)
            alpha = es[i] / den
            ov = o_ref[...]
            dv = dout_ref[:, sl]
            sz, dsz = _silu_and_grad(z_ref[:, sl])
            dz_ref[:, sl] = (dv * ov * alpha * dsz).astype(BF16)
            da = dv * sz
            do_ref[...] = da * alpha
            t = da * ov
            t1 = t.astype(BF16)
            r1 = t - t1.astype(F32)
            t2 = r1.astype(BF16)
            t3 = (r1 - t2.astype(F32)).astype(BF16)
            e = _dot(t1, ones_ref[...], NN) + _dot(t2, ones_ref[...], NN) + _dot(t3, ones_ref[...], NN)
            ebar = ebar + alpha * e
            alphas.append(alpha)
        for alpha, dv_ref in zip(alphas, (dv0_ref, dv1_ref, dv2_ref)):
            dv_ref[...] = -alpha * ebar

    row = pl.BlockSpec((tm, C), lambda i: (i, 0))
    grp = pl.BlockSpec((tm, gw), lambda i: (i, 0))
    sd = jax.ShapeDtypeStruct
    res = pl.pallas_call(
        body, name=name, grid=(T // tm,),
        in_specs=[row] + [grp] * 6 + [row, pl.BlockSpec((gw, gw), lambda i: (0, 0))], out_specs=[row] + [grp] * 6,
        out_shape=[sd((T, C), BF16)] + [sd((T, gw), F32)] * 6, compiler_params=_params("parallel"),
    )(dout, *os, *lses, z, ones)
    return res[0], res[1:4], res[4:7]


def _mem_attn(qz, kv, name):
    T = qz.shape[0]
    dm = qz.shape[1] // 2
    M = kv.shape[0]
    hd = dm // MEM_HEADS
    scale = hd ** -0.5
    tm = _pick(T, 512, SUBLANES)

    def body(q_ref, z_ref, k_ref, v_ref, o_ref, s_ref, p_ref):
        heads = [slice(h * hd, (h + 1) * hd) for h in range(MEM_HEADS)]
        for h, sl in enumerate(heads):
            s_ref[h] = _dot(q_ref[:, sl].astype(BF16), k_ref[:, sl], NT)
        for h, sl in enumerate(heads):
            s = s_ref[h] * scale
            p = jnp.exp(s - jnp.max(s, axis=-1, keepdims=True))
            p_ref[h] = (p / jnp.sum(p, axis=-1, keepdims=True)).astype(BF16)
        for h, sl in enumerate(heads):
            sz, _ = _silu_and_grad(z_ref[:, sl])
            o_ref[:, sl] = (_dot(p_ref[h], v_ref[:, sl], NN) * sz).astype(BF16)

    return pl.pallas_call(
        body, name=name, grid=(T // tm,),
        in_specs=[pl.BlockSpec((tm, dm), lambda i: (i, 0)), pl.BlockSpec((tm, dm), lambda i: (i, 1)),
                  pl.BlockSpec((M, dm), lambda i: (0, 0)), pl.BlockSpec((M, dm), lambda i: (0, 1))],
        out_specs=pl.BlockSpec((tm, dm), lambda i: (i, 0)),
        out_shape=jax.ShapeDtypeStruct((T, dm), BF16),
        scratch_shapes=[pltpu.VMEM((MEM_HEADS, tm, M), F32), pltpu.VMEM((MEM_HEADS, tm, M), BF16)],
        compiler_params=_params("parallel"),
    )(qz, qz, kv, kv)


def _mem_attn_bwd(do, qz, kv, name):
    T = qz.shape[0]
    dm = qz.shape[1] // 2
    M = kv.shape[0]
    hd = dm // MEM_HEADS
    scale = hd ** -0.5
    tm = _pick(T, 512, SUBLANES)

    def body(do_ref, q_ref, z_ref, k_ref, v_ref, dq_ref, dz_ref, dk_ref, dv_ref, s_ref, dp_ref, p_ref, ds_ref, dob_ref):
        @pl.when(pl.program_id(0) == 0)
        def _():
            dk_ref[...] = jnp.zeros_like(dk_ref)
            dv_ref[...] = jnp.zeros_like(dv_ref)

        heads = [slice(h * hd, (h + 1) * hd) for h in range(MEM_HEADS)]
        for h, sl in enumerate(heads):
            sz, _ = _silu_and_grad(z_ref[:, sl])
            dob = (do_ref[:, sl] * sz).astype(BF16)
            dob_ref[:, sl] = dob
            s_ref[h] = _dot(q_ref[:, sl].astype(BF16), k_ref[:, sl], NT)
            dp_ref[h] = _dot(dob, v_ref[:, sl], NT)
        for h, sl in enumerate(heads):
            s = s_ref[h] * scale
            p = jnp.exp(s - jnp.max(s, axis=-1, keepdims=True))
            pn = p / jnp.sum(p, axis=-1, keepdims=True)
            dp = dp_ref[h]
            p_ref[h] = pn.astype(BF16)
            ds_ref[h] = (pn * (dp - jnp.sum(dp * pn, axis=-1, keepdims=True))).astype(BF16)
        for h, sl in enumerate(heads):
            _, dsz = _silu_and_grad(z_ref[:, sl])
            dz_ref[:, sl] = (do_ref[:, sl] * _dot(p_ref[h], v_ref[:, sl], NN) * dsz).astype(BF16)
            dq_ref[:, sl] = (_dot(ds_ref[h], k_ref[:, sl], NN) * scale).astype(BF16)
            dk_ref[:, sl] += _dot(ds_ref[h], q_ref[:, sl].astype(BF16), TN) * scale
            dv_ref[:, sl] += _dot(p_ref[h], dob_ref[:, sl], TN)

    rowq = pl.BlockSpec((tm, dm), lambda i: (i, 0))
    rowz = pl.BlockSpec((tm, dm), lambda i: (i, 1))
    kb = pl.BlockSpec((M, dm), lambda i: (0, 0))
    vb = pl.BlockSpec((M, dm), lambda i: (0, 1))
    sd = jax.ShapeDtypeStruct
    dq, dz, dk, dv = pl.pallas_call(
        body, name=name, grid=(T // tm,), in_specs=[rowq, rowq, rowz, kb, vb],
        out_specs=[rowq, rowq, kb, kb],
        out_shape=[sd((T, dm), BF16), sd((T, dm), BF16), sd((M, dm), F32), sd((M, dm), F32)],
        scratch_shapes=[pltpu.VMEM((MEM_HEADS, tm, M), F32), pltpu.VMEM((MEM_HEADS, tm, M), F32),
                        pltpu.VMEM((MEM_HEADS, tm, M), BF16), pltpu.VMEM((MEM_HEADS, tm, M), BF16), pltpu.VMEM((tm, dm), BF16)],
        compiler_params=_params("arbitrary"),
    )(do, qz, qz, kv, kv)
    return dq, dz, dk, dv


def _merge(os, ws, L, logits, b_gate, name):
    T = os[0].shape[0]
    D = ws[0].shape[2]
    tm = _pick(T, 512, SUBLANES)

    def body(o0_ref, o1_ref, o2_ref, w0_ref, w1_ref, w2_ref, l_ref, b_ref, m_ref, p0_ref, p1_ref, p2_ref):
        acc = 0.0
        for i, (o_ref, w_ref, p_ref) in enumerate(((o0_ref, w0_ref, p0_ref), (o1_ref, w1_ref, p1_ref), (o2_ref, w2_ref, p2_ref))):
            sl = slice(i * D, (i + 1) * D)
            bp = _dot(o_ref[...], w_ref[...], NN)
            p_ref[...] = bp.astype(BF16)
            acc = acc + _sigmoid(l_ref[:, sl] + b_ref[:, sl]) * bp
        m_ref[...] = acc.astype(BF16)

    row = pl.BlockSpec((tm, D), lambda i: (i, 0))
    return pl.pallas_call(
        body, name=name, grid=(T // tm,),
        in_specs=[pl.BlockSpec((tm, o.shape[1]), lambda i: (i, 0)) for o in os]
        + [pl.BlockSpec((None,) + w.shape[1:], lambda i: (L, 0, 0)) for w in ws]
        + [pl.BlockSpec((tm, 3 * D), lambda i: (i, 0)), pl.BlockSpec((1, 3 * D), lambda i: (0, 0))],
        out_specs=[row] * 4, out_shape=[jax.ShapeDtypeStruct((T, D), BF16)] * 4, compiler_params=_params("parallel"),
    )(*os, *ws, logits, b_gate.reshape(1, 3 * D))


def _merge_bwd(dmerged, bps, logits, b_gate, name):
    T, D = bps[0].shape
    tm = _pick(T, 512, SUBLANES)

    def body(dm_ref, p0_ref, p1_ref, p2_ref, l_ref, b_ref, d0_ref, d1_ref, d2_ref, dl_ref, db_ref):
        @pl.when(pl.program_id(0) == 0)
        def _():
            db_ref[...] = jnp.zeros_like(db_ref)

        dmv = dm_ref[...]
        for i, (p_ref, d_ref) in enumerate(((p0_ref, d0_ref), (p1_ref, d1_ref), (p2_ref, d2_ref))):
            sl = slice(i * D, (i + 1) * D)
            gt = _sigmoid(l_ref[:, sl] + b_ref[:, sl])
            d_ref[...] = (dmv * gt).astype(BF16)
            dl = dmv * p_ref[...].astype(F32) * gt * (1.0 - gt)
            dl_ref[:, sl] = dl.astype(BF16)
            db_ref[:, sl] += jnp.sum(dl, axis=0, keepdims=True)

    row = pl.BlockSpec((tm, D), lambda i: (i, 0))
    wide = pl.BlockSpec((tm, 3 * D), lambda i: (i, 0))
    vec = pl.BlockSpec((1, 3 * D), lambda i: (0, 0))
    sd = jax.ShapeDtypeStruct
    return pl.pallas_call(
        body, name=name, grid=(T // tm,), in_specs=[row, row, row, row, wide, vec],
        out_specs=[row, row, row, wide, vec],
        out_shape=[sd((T, D), BF16)] * 3 + [sd((T, 3 * D), BF16), sd((1, 3 * D), F32)],
        compiler_params=_params("arbitrary"),
    )(dmerged, *bps, logits, b_gate.reshape(1, 3 * D))


def _to_segments(a):
    T, C = a.shape
    return a.reshape(SSM_SEGMENTS, T // SSM_SEGMENTS, C).transpose(1, 0, 2).reshape(T, C)


def _from_segments(a):
    T, C = a.shape
    return a.reshape(T // SSM_SEGMENTS, SSM_SEGMENTS, C).transpose(1, 0, 2).reshape(T, C)


def _to_residues(a):
    T = a.shape[0]
    gw = HEADS_PER_GROUP * ATTN_HEAD_DIM
    out = []
    for g, (_, r) in enumerate(ATTN_CONFIGS):
        ag = a[:, g * gw:(g + 1) * gw].reshape(T // r, r, gw)
        out.append(ag.transpose(1, 0, 2).reshape(T, gw))
    return jnp.stack(out)


def _from_residues(a):
    _, T, gw = a.shape
    out = []
    for g, (_, r) in enumerate(ATTN_CONFIGS):
        out.append(a[g].reshape(r, T // r, gw).transpose(1, 0, 2).reshape(T, gw))
    return jnp.concatenate(out, axis=1)


def _block_diag(w):
    nblk, ng, a, b = w.shape
    eye = jnp.eye(ng, dtype=w.dtype)
    return (w[:, :, :, None, :] * eye[None, :, None, :, None]).reshape(nblk, ng * a, ng * b)


def _block_diag_part(m, a, b):
    nblk = m.shape[0]
    ng = m.shape[1] // a
    m5 = m.reshape(nblk, ng, a, ng, b)
    eye = jnp.eye(ng, dtype=m.dtype)
    return jnp.sum(m5 * eye[None, :, None, :, None], axis=3)


def _ssm_matrices(p, L, tag):
    G, P = p["ssm_lambda_re"].shape[1:]
    Hg = SSM_GROUP
    gpb = SSM_BLOCK_CH // Hg
    nblk = G // gpb
    br = p["ssm_b_re"][L].transpose(2, 0, 1)
    bi = p["ssm_b_im"][L].transpose(2, 0, 1)
    disc_in = (p["ssm_lambda_re"][L], p["ssm_lambda_im"][L], p["ssm_log_dt"][L].reshape(G, 1), br, bi)
    ar, ai, bbr, bbi = _ssm_disc(*disc_in, name=f"ssm_disc_{tag}")
    amat = (ar.reshape(nblk // 2, SUBLANES, LANES), ai.reshape(nblk // 2, SUBLANES, LANES))
    bbr_g = bbr.transpose(1, 0, 2).reshape(nblk, gpb, Hg, P)
    bbi_g = bbi.transpose(1, 0, 2).reshape(nblk, gpb, Hg, P)
    bmat = jnp.concatenate([_block_diag(bbr_g), _block_diag(bbi_g)], axis=2).astype(BF16)
    cre = p["ssm_c_re"][L].reshape(nblk, gpb, Hg, P).transpose(0, 1, 3, 2)
    cim = p["ssm_c_im"][L].reshape(nblk, gpb, Hg, P).transpose(0, 1, 3, 2)
    cmat = jnp.concatenate([_block_diag(cre), -_block_diag(cim)], axis=1).astype(BF16)
    return disc_in, amat, bmat, cmat


def _layer_fwd(x, mem, p, wb, L, biasm):
    T, D = x.shape
    C = p["ssm_d"].shape[1]
    dm = wb["w_br_mem"].shape[1]
    tag = f"l{L}"
    s = {"x": x}
    h = _rmsnorm(x, p["norm_g"][L], f"norm_{tag}")
    offs = [int(o) for o in np.cumsum([0, 2 * C, 3 * 768, 768, 2 * dm, 3 * D])]
    names = ("uz", "qkv", "z_attn", "qz_mem", "logits")
    dts = (F32, F32, F32, F32, F32)
    for i, (nm, dt) in enumerate(zip(names, dts)):
        s[nm] = _matmul(h, wb["w_in"], mode="nn", name=f"in_{nm}_{tag}", out_dtype=dt, b_lead=L, b_off=offs[i],
                        n_cols=offs[i + 1] - offs[i])
    s["h"] = h

    disc_in, amat, bmat, cmat = _ssm_matrices(p, L, tag)
    dvec = p["ssm_d"][L].reshape(1, C)
    y, xr, xi = _ssm_fwd(s["uz"], bmat, cmat, *amat, dvec, f"ssm_scan_{tag}")
    o_ssm, a_glu = _ssm_post(y, s["uz"], wb["w_glu"][L], p["b_glu"][L], f"ssm_post_{tag}")
    s.update(disc_in=disc_in, amat=amat, bmat=bmat, cmat=cmat, xr=xr, xi=xi, y=y, a_glu=a_glu, o_ssm=o_ssm)

    groups = [_attn_group_fwd(s["qkv"], biasm, g, f"attn_g{g}_{tag}") for g in range(len(ATTN_CONFIGS))]
    os, lses = [o for o, _ in groups], [l for _, l in groups]
    o_attn = _attn_mix(os, lses, s["z_attn"], f"attn_mix_{tag}")
    s.update(os=os, lses=lses, o_attn=o_attn)

    mn = _rmsnorm(mem, p["mem_norm_g"][L], f"mem_norm_{tag}")
    kv = _matmul(mn, wb["w_mem_kv"], mode="nn", name=f"mem_kv_{tag}", out_dtype=BF16, b_lead=L)
    o_mem = _mem_attn(s["qz_mem"], kv, f"mem_attn_{tag}")
    s.update(mn=mn, kv=kv, o_mem=o_mem)

    merged, *bps = _merge([o_ssm, o_attn, o_mem], [wb["w_br_ssm"], wb["w_br_attn"], wb["w_br_mem"]], L, s["logits"],
                          p["b_gate"][L], f"merge_{tag}")
    s.update(bps=bps, merged=merged)
    x_new = _matmul(merged, wb["w_out"], mode="nn", name=f"out_{tag}", add=x, b_lead=L)
    return x_new, s


def _layer_bwd(dx, mem, p, wb, L, s, biasm, gprev):
    T, D = dx.shape
    C = p["ssm_d"].shape[1]
    depth = p["norm_g"].shape[0]
    tag = f"l{L}"
    g = {}

    def wgrad(n, a, b, **tiles):
        g[n] = _matmul(a, b, mode="tn", name=f"d{n}_{tag}", out_dtype=BF16, stack=(L, depth, gprev.get(n)), **tiles)

    dmerged = _matmul(dx, wb["w_out"], mode="nt", name=f"d_merged_{tag}", b_lead=L)
    wgrad("w_out", s["merged"], dx)
    dbp0, dbp1, dbp2, dlogits, g["b_gate"] = _merge_bwd(dmerged, s["bps"], s["logits"], p["b_gate"][L], f"merge_bwd_{tag}")
    dos = []
    for dbp, o, n in ((dbp0, s["o_ssm"], "w_br_ssm"), (dbp1, s["o_attn"], "w_br_attn"), (dbp2, s["o_mem"], "w_br_mem")):
        dos.append(_matmul(dbp, wb[n], mode="nt", name=f"d_o_{n}_{tag}", b_lead=L))
        wgrad(n, o, dbp)

    dy, dz_ssm, ds_glu, g["b_glu"] = _ssm_post_bwd(dos[0], s["y"], s["uz"], wb["w_glu"][L], p["b_glu"][L], f"ssm_post_bwd_{tag}")
    wgrad("w_glu", s["a_glu"], ds_glu)
    dvec = p["ssm_d"][L].reshape(1, C)
    du, dbm, dct, dar, dai, g["ssm_d"] = _ssm_bwd(dy, s["uz"], s["xr"], s["xi"], s["bmat"], s["cmat"], *s["amat"], dvec,
                                                  f"ssm_scan_bwd_{tag}")
    G, P = p["ssm_lambda_re"].shape[1:]
    Hg = SSM_GROUP
    half = dbm.shape[2] // 2
    dbbr = _block_diag_part(dbm[:, :, :half], Hg, P).reshape(G, Hg, P).transpose(1, 0, 2)
    dbbi = _block_diag_part(dbm[:, :, half:], Hg, P).reshape(G, Hg, P).transpose(1, 0, 2)
    g["ssm_c_re"] = _block_diag_part(dct[:, :, :half], Hg, P).reshape(G, Hg, P)
    g["ssm_c_im"] = -_block_diag_part(dct[:, :, half:], Hg, P).reshape(G, Hg, P)
    glre, glim, gdt, gbr, gbi = _ssm_disc_bwd(*s["disc_in"], dar.reshape(G, P), dai.reshape(G, P), dbbr, dbbi,
                                              name=f"ssm_disc_bwd_{tag}")
    g["ssm_lambda_re"], g["ssm_lambda_im"], g["ssm_log_dt"] = glre, glim, gdt.reshape(G)
    g["ssm_b_re"] = gbr.transpose(1, 2, 0)
    g["ssm_b_im"] = gbi.transpose(1, 2, 0)

    dz_attn, do_g, dvec_g = _attn_mix_bwd(dos[1], s["os"], s["lses"], s["z_attn"], f"attn_mix_bwd_{tag}")
    back = [_attn_group_bwd(s["qkv"], do_g[g], dvec_g[g], s["lses"][g], biasm, g, f"attn_bwd_g{g}_{tag}")
            for g in range(len(ATTN_CONFIGS))]
    dqkv = [b[i].astype(BF16) for i in range(3) for b in back]
    dbias = jnp.stack([b[3] for b in back])

    dq_mem, dz_mem, dk_mem, dv_mem = _mem_attn_bwd(dos[2], s["qz_mem"], s["kv"], f"mem_attn_bwd_{tag}")
    dkv = jnp.concatenate([dk_mem, dv_mem], axis=1)
    wgrad("w_mem_kv", s["mn"], dkv)
    dmn = _matmul(dkv, wb["w_mem_kv"], mode="nt", name=f"d_mn_{tag}", b_lead=L)
    _, g["mem_norm_g"] = _rmsnorm_bwd(mem, p["mem_norm_g"][L], dmn, None, f"mem_norm_bwd_{tag}")

    dproj = jnp.concatenate([du.astype(BF16), dz_ssm] + dqkv + [dz_attn, dq_mem, dz_mem, dlogits], axis=1)
    dh = _matmul(dproj, wb["w_in"], mode="nt", name=f"d_h_{tag}", b_lead=L, tn=1024, tk=512)
    wgrad("w_in", s["h"], dproj, tn=2304, tk=1024)
    dx_in, g["norm_g"] = _rmsnorm_bwd(s["x"], p["norm_g"][L], dh, dx, f"norm_bwd_{tag}")
    return dx_in, g, dbias


def _bucket_onehot(gi):
    buckets, bands = _band_tables()
    hit = (buckets[gi].reshape(1, -1) == jnp.arange(NUM_BUCKETS)[:, None]) & bands[gi].reshape(1, -1)
    return hit.astype(BF16)


def _bias_tables(rel_bias, name):
    _, bands = _band_tables()
    out = []
    for gi in range(len(ATTN_CONFIGS)):
        tab = rel_bias[:, gi * HEADS_PER_GROUP:(gi + 1) * HEADS_PER_GROUP].T
        flat = _matmul(tab, _bucket_onehot(gi), mode="nn", name=f"{name}_{gi}", split_a=3, tn=4096)
        out.append(jnp.where(bands[gi][None], flat.reshape(HEADS_PER_GROUP, ATTN_BLOCK, 2 * ATTN_BLOCK), NEG_INF))
    return jnp.stack(out)


def _rel_bias_grad(dbias_sum, name):
    cols = []
    for gi in range(len(ATTN_CONFIGS)):
        flat = dbias_sum[gi].reshape(HEADS_PER_GROUP, -1)
        cols.append(_matmul(flat, _bucket_onehot(gi), mode="nt", name=f"{name}_{gi}", split_a=2, tk=4096).T)
    return jnp.concatenate(cols, axis=1)


def _local_step(x, mem, target, p, wb):
    depth = p["norm_g"].shape[0]
    biasm = _bias_tables(p["rel_bias"], "bias_table")
    saved = []
    for L in range(depth):
        x, s = _layer_fwd(x, mem, p, wb, L, biasm)
        saved.append(s)
    loss_vec, dx, dgf = _loss_head(x, p["final_norm_g"], target, "loss_head")
    grads = {"final_norm_g": dgf.reshape(-1)}
    per_layer = [None] * depth
    dbias_sum = 0.0
    stacked = {}
    for L in reversed(range(depth)):
        dx, per_layer[L], dbias = _layer_bwd(dx, mem, p, wb, L, saved[L], biasm, stacked)
        stacked = {n: per_layer[L][n] for n, _ in BIG}
        dbias_sum = dbias_sum + dbias
    grads.update(stacked)
    for n in per_layer[0]:
        if n not in stacked:
            grads[n] = jnp.stack([per_layer[L][n].reshape(p[n].shape[1:]) for L in range(depth)])
    grads["rel_bias"] = _rel_bias_grad(dbias_sum, "d_rel_bias")
    return jnp.sum(loss_vec), dx, grads


def _chip_coords(j):
    return j // 2, j % 2


def _place_shard(shard, ax, chip, name):
    _, a, b = shard.shape
    ra = _pick(a, 256, 16)
    full = (2, a * N_CHIPS, b) if ax == 1 else (2, a, b * N_CHIPS)
    per = a // ra

    def body(j_ref, s_ref, o_ref):
        o_ref[...] = s_ref[...].astype(BF16)

    out_idx = (lambda l, i, j: (l, j[0] * per + i, 0)) if ax == 1 else (lambda l, i, j: (l, i, j[0]))
    return pl.pallas_call(
        body, name=name,
        grid_spec=pltpu.PrefetchScalarGridSpec(
            num_scalar_prefetch=1, grid=(2, per),
            in_specs=[pl.BlockSpec((None, ra, b), lambda l, i, j: (l, i, 0))],
            out_specs=pl.BlockSpec((None, ra, b), out_idx)),
        out_shape=jax.ShapeDtypeStruct(full, BF16), compiler_params=_params("parallel", "parallel"),
    )(chip, shard)


def _gather_shards(fulls, axes, name):
    n = len(fulls)
    widths = [a.shape[ax] // N_CHIPS for a, ax in zip(fulls, axes)]
    aligns = [LANES if ax == 2 else 16 for ax in axes]

    def body(*refs):
        outs = refs[n:2 * n]
        send_sems, recv_sems, fsend_sems, frecv_sems = refs[2 * n:]
        x, y, c = lax.axis_index("x"), lax.axis_index("y"), lax.axis_index("c")
        mine = 2 * x + y
        sibling = (x, y, 1 - c)

        def window(t, layer, j):
            start = pl.ds(pl.multiple_of(j * widths[t], aligns[t]), widths[t])
            return outs[t].at[(layer, start, slice(None)) if axes[t] == 1 else (layer, slice(None), start)]

        def over_ici(t, j, block):
            return pltpu.make_async_remote_copy(
                src_ref=window(t, c, mine), dst_ref=window(t, c, block), send_sem=send_sems.at[t, j],
                recv_sem=recv_sems.at[t, block], device_id=(*_chip_coords(j), c), device_id_type=MESH)

        def over_d2d(t, j, layer):
            return pltpu.make_async_remote_copy(
                src_ref=window(t, layer, j), dst_ref=window(t, layer, j), send_sem=fsend_sems.at[t, j],
                recv_sem=frecv_sems.at[t, j], device_id=sibling, device_id_type=MESH)

        for t in range(n):
            for j in range(N_CHIPS):
                @pl.when(j != mine)
                def _():
                    over_ici(t, j, mine).start()
        for t in range(n):
            for j in range(N_CHIPS):
                @pl.when(j != mine)
                def _():
                    over_ici(t, j, j).wait_recv()
                    over_d2d(t, j, c).start()
        for t in range(n):
            for j in range(N_CHIPS):
                @pl.when(j != mine)
                def _():
                    over_ici(t, j, mine).wait_send()
                    over_d2d(t, j, c).wait_send()
                    over_d2d(t, j, 1 - c).wait_recv()

    sem = pltpu.SemaphoreType.DMA
    return pl.pallas_call(
        body, name=name, in_specs=[HBM] * n, out_specs=[HBM] * n,
        out_shape=[jax.ShapeDtypeStruct(a.shape, a.dtype) for a in fulls],
        input_output_aliases={t: t for t in range(n)},
        scratch_shapes=[sem((n, N_CHIPS)), sem((n, N_CHIPS)), sem((n, N_CHIPS)), sem((n, N_CHIPS))],
    )(*fulls)


def _scatter_slices(arrays, axes, name):
    n = len(arrays)

    def piece(a, ax):
        if ax is None:
            return a.shape, None
        w = a.shape[ax] // N_CHIPS
        return a.shape[:ax] + (w,) + a.shape[ax + 1:], w

    shapes = [piece(a, ax) for a, ax in zip(arrays, axes)]

    def body(*refs):
        ins, outs = refs[:n], refs[n:2 * n]
        send_sems, recv_sems, loc_sems = refs[2 * n:]
        x, y, c = lax.axis_index("x"), lax.axis_index("y"), lax.axis_index("c")
        mine = 2 * x + y

        def src(t, j):
            ax, w = axes[t], shapes[t][1]
            if ax is None:
                return ins[t]
            idx = tuple(pl.ds(j * w, w) if d == ax else slice(None) for d in range(len(arrays[t].shape)))
            return ins[t].at[idx]

        for t in range(n):
            for j in range(N_CHIPS):
                @pl.when(j == mine)
                def _():
                    pltpu.make_async_copy(src(t, j), outs[t].at[j], loc_sems.at[t]).start()

                @pl.when(j != mine)
                def _():
                    pltpu.make_async_remote_copy(
                        src_ref=src(t, j), dst_ref=outs[t].at[mine], send_sem=send_sems.at[t, j], recv_sem=recv_sems.at[t, mine],
                        device_id=(*_chip_coords(j), c), device_id_type=MESH).start()
        for t in range(n):
            for j in range(N_CHIPS):
                @pl.when(j == mine)
                def _():
                    pltpu.make_async_copy(src(t, j), outs[t].at[j], loc_sems.at[t]).wait()

                @pl.when(j != mine)
                def _():
                    cp = pltpu.make_async_remote_copy(
                        src_ref=src(t, j), dst_ref=outs[t].at[j], send_sem=send_sems.at[t, j], recv_sem=recv_sems.at[t, j],
                        device_id=(*_chip_coords(j), c), device_id_type=MESH)
                    cp.wait_send()
                    cp.wait_recv()

    return pl.pallas_call(
        body, name=name, in_specs=[HBM] * n, out_specs=[HBM] * n,
        out_shape=[jax.ShapeDtypeStruct((N_CHIPS,) + sh, a.dtype) for a, (sh, _) in zip(arrays, shapes)],
        scratch_shapes=[pltpu.SemaphoreType.DMA((n, N_CHIPS)), pltpu.SemaphoreType.DMA((n, N_CHIPS)), pltpu.SemaphoreType.DMA((n,))],
    )(*arrays)


def _swap_layers(stacked, name):
    n = len(stacked)

    def body(*refs):
        ins, outs = refs[:n], refs[n:2 * n]
        send_sems, recv_sems = refs[2 * n:]
        c = lax.axis_index("c")
        peer = (lax.axis_index("x"), lax.axis_index("y"), 1 - c)
        cps = [pltpu.make_async_remote_copy(src_ref=ins[t].at[1 - c], dst_ref=outs[t], send_sem=send_sems.at[t],
                                            recv_sem=recv_sems.at[t], device_id=peer, device_id_type=MESH) for t in range(n)]
        for cp in cps:
            cp.start()
        for cp in cps:
            cp.wait_send()
            cp.wait_recv()

    return pl.pallas_call(
        body, name=name, in_specs=[HBM] * n, out_specs=[HBM] * n,
        out_shape=[jax.ShapeDtypeStruct(a.shape[1:], a.dtype) for a in stacked],
        scratch_shapes=[pltpu.SemaphoreType.DMA((n,)), pltpu.SemaphoreType.DMA((n,))],
    )(*stacked)


def _merge_layers(stacked, name):
    n = len(stacked)

    def body(*refs):
        outs = refs[n:2 * n]
        send_sems, recv_sems = refs[2 * n:]
        c = lax.axis_index("c")
        peer = (lax.axis_index("x"), lax.axis_index("y"), 1 - c)
        for t in range(n):
            pltpu.make_async_remote_copy(src_ref=outs[t].at[c], dst_ref=outs[t].at[c], send_sem=send_sems.at[t],
                                         recv_sem=recv_sems.at[t], device_id=peer, device_id_type=MESH).start()
        for t in range(n):
            cp = pltpu.make_async_remote_copy(src_ref=outs[t].at[c], dst_ref=outs[t].at[1 - c], send_sem=send_sems.at[t],
                                              recv_sem=recv_sems.at[t], device_id=peer, device_id_type=MESH)
            cp.wait_send()
            cp.wait_recv()

    sem = pltpu.SemaphoreType.DMA
    return pl.pallas_call(
        body, name=name, in_specs=[HBM] * n, out_specs=[HBM] * n,
        out_shape=[jax.ShapeDtypeStruct(a.shape, a.dtype) for a in stacked],
        input_output_aliases={t: t for t in range(n)}, scratch_shapes=[sem((n,)), sem((n,))],
    )(*stacked)


def _pair_sum(stacked, landed, core, name):
    _, K, N = stacked.shape
    tr = _pick(K, max(16, (1 << 19) // N // 16 * 16), 16)

    def body(c_ref, s_ref, l_ref, o_ref):
        o_ref[...] = (s_ref[...].astype(F32) + l_ref[...].astype(F32)).astype(o_ref.dtype)

    return pl.pallas_call(
        body, name=name,
        grid_spec=pltpu.PrefetchScalarGridSpec(
            num_scalar_prefetch=1, grid=(K // tr,),
            in_specs=[pl.BlockSpec((None, tr, N), lambda i, c: (c[0], i, 0)), pl.BlockSpec((tr, N), lambda i, c: (i, 0))],
            out_specs=pl.BlockSpec((tr, N), lambda i, c: (i, 0))),
        out_shape=jax.ShapeDtypeStruct((K, N), stacked.dtype), compiler_params=_params("parallel"),
    )(core, stacked, landed)


def _sum_chips(landed, core, name):
    _, R, C = landed.shape
    tr = _pick(R, max(SUBLANES, (1 << 19) // C // 16 * 16), 16)

    def body(c_ref, l_ref, o_ref):
        acc = l_ref[0].astype(F32) + l_ref[1].astype(F32)
        acc = acc + l_ref[2].astype(F32)
        o_ref[...] = acc + l_ref[3].astype(F32)

    return pl.pallas_call(
        body, name=name,
        grid_spec=pltpu.PrefetchScalarGridSpec(
            num_scalar_prefetch=1, grid=(R // tr,),
            in_specs=[pl.BlockSpec((N_CHIPS, tr, C), lambda i, c: (0, i, 0))],
            out_specs=pl.BlockSpec((None, tr, C), lambda i, c: (c[0], i, 0))),
        out_shape=jax.ShapeDtypeStruct((2, R, C), F32), compiler_params=_params("parallel"),
    )(core, landed)


def _adamw_math(w_ref, g_ref, m_ref, v_ref, d_ref, nm_ref, nv_ref):
    c1 = 1.0 / (1.0 - ADAM_B1 ** ADAM_STEP)
    c2 = 1.0 / (1.0 - ADAM_B2 ** ADAM_STEP)
    g = g_ref[...]
    nm = ADAM_B1 * m_ref[...] + (1.0 - ADAM_B1) * g
    nv = ADAM_B2 * v_ref[...] + (1.0 - ADAM_B2) * (g * g)
    nm_ref[...] = nm
    nv_ref[...] = nv
    d_ref[...] = -ADAM_LR * ((nm * c1) / (jnp.sqrt(nv * c2) + ADAM_EPS) + ADAM_WD * w_ref[...])


def _adamw_whole(w, g, m, v, name):
    shape = w.shape
    view = (-1,) + shape[-2:] if w.ndim >= 2 else (1, 1, -1)

    def body(*refs):
        _adamw_math(*refs)

    res = pl.pallas_call(body, name=name, out_shape=[jax.ShapeDtypeStruct(w.reshape(view).shape, F32)] * 3,
                         compiler_params=pltpu.CompilerParams(vmem_limit_bytes=VMEM_LIMIT_BYTES))(
        *(a.reshape(view) for a in (w, g, m, v)))
    return [r.reshape(shape) for r in res]


def _adamw(w, g, m, v, name):
    R, C = w.shape
    tr = _pick(R, max(SUBLANES, (1 << 18) // C // 8 * 8), SUBLANES)

    def body(*refs):
        _adamw_math(*refs)

    blk = pl.BlockSpec((tr, C), lambda i: (i, 0))
    return pl.pallas_call(
        body, name=name, grid=(R // tr,), in_specs=[blk] * 4, out_specs=[blk] * 3,
        out_shape=[jax.ShapeDtypeStruct((R, C), F32)] * 3, compiler_params=_params("parallel"),
    )(w, g, m, v)


def _pack_small(d, prefix=""):
    flat = jnp.concatenate([d[prefix + n].astype(F32).reshape(-1) for n in SMALL])
    pad = (-flat.shape[0]) % (2 * 16 * LANES)
    return jnp.pad(flat, (0, pad)).reshape(-1, LANES)


def _unpack_small(packed, shapes):
    flat = packed.reshape(-1)
    out, off = {}, 0
    for n in SMALL:
        size = int(np.prod(shapes[n]))
        out[n] = flat[off:off + size].reshape(shapes[n])
        off += size
    return out


def kernel(*args):
    p = dict(zip(INPUTS, args))
    x, mem, target = p["x"][0], p["mem"][0], p["loss_target"][0]

    names = [n for n, _ in BIG] + ["small"]
    core = lax.axis_index("c").astype(jnp.int32).reshape(1)
    chip = (2 * lax.axis_index("x") + lax.axis_index("y")).astype(jnp.int32).reshape(1)
    placed = [_place_shard(p[n], ax, chip, f"place_{n}") for n, ax in BIG]
    wb = dict(zip(names, _gather_shards(placed, [ax for _, ax in BIG], "gather_weights")))

    loss_part, dx, grads = _local_step(x, mem, target, p, wb)
    loss = lax.psum(loss_part, ("x", "y", "c"))

    stacked = [grads[n] for n, _ in BIG] + [_pack_small(grads).reshape(2, -1, LANES)]
    theirs = _swap_layers(stacked, "swap_layers")
    pair = [_pair_sum(s, o, core, f"pair_sum_{n}") for n, s, o in zip(names, stacked, theirs)]
    landed = _scatter_slices(pair, [ax - 1 for _, ax in BIG] + [None], "scatter_grads")
    reduced = [_sum_chips(ld.reshape(N_CHIPS, -1, ld.shape[-1]), core, f"sum_chips_{n}") for n, ld in zip(names, landed)]
    total = _merge_layers(reduced, "merge_layers")

    out = {}
    for (n, _), g in zip(BIG, total):
        sh = p[n].shape
        two_d = lambda a: a.reshape(-1, sh[-1])
        res = (g,) + tuple(_adamw(two_d(p[n]), two_d(g), two_d(p["m_" + n]), two_d(p["v_" + n]), f"adamw_{n}"))
        for key, r in zip(("grad_", "delta_", "new_m_", "new_v_"), res):
            out[key + n] = r.reshape(sh)
    for n, g in _unpack_small(total[-1], {n: p[n].shape for n in SMALL}).items():
        res = (g,) + tuple(_adamw_whole(p[n], g, p["m_" + n], p["v_" + n], f"adamw_{n}"))
        for key, r in zip(("grad_", "delta_", "new_m_", "new_v_"), res):
            out[key + n] = r

    result = [loss, dx.reshape(p["x"].shape)]
    for key in ("grad_", "delta_", "new_m_", "new_v_"):
        result += [out[key + n] for n in WEIGHTS]
    return tuple(result)
```

```python
import math

import jax
import jax.numpy as jnp
import numpy as np
from jax import lax
from jax.experimental import pallas as pl
from jax.experimental.pallas import tpu as pltpu

F32 = jnp.float32
BF16 = jnp.bfloat16
MESH = pl.DeviceIdType.MESH
HBM = pl.BlockSpec(memory_space=pltpu.HBM)

EPS = 1e-6
SSM_GROUP = 16
SSM_STATE = 64
ATTN_HEAD_DIM = 64
HEADS_PER_GROUP = 4
ATTN_CONFIGS = ((128, 1), (512, 4), (2048, 16))
ATTN_BLOCK = 128
NUM_BUCKETS = 32
REL_MAX_DISTANCE = 2048
NEG_INF = -1e30
MEM_HEADS = 4
ADAM_LR = 0.001
ADAM_B1 = 0.9
ADAM_B2 = 0.999
ADAM_EPS = 1e-08
ADAM_WD = 0.01
ADAM_STEP = 10

LANES = 128
SUBLANES = 8
VMEM_LIMIT_BYTES = 48 * 1024 * 1024
SSM_BLOCK_CH = 128
SSM_SEGMENTS = SUBLANES
SSM_CHUNK_STEPS = 128

N_CHIPS = 4
BIG = (("w_in", 2), ("w_glu", 1), ("w_mem_kv", 1), ("w_br_ssm", 2), ("w_br_attn", 2), ("w_br_mem", 2), ("w_out", 1))
SMALL = ("norm_g", "mem_norm_g", "b_gate", "ssm_lambda_re", "ssm_lambda_im", "ssm_log_dt", "ssm_b_re", "ssm_b_im",
         "ssm_c_re", "ssm_c_im", "ssm_d", "b_glu", "rel_bias", "final_norm_g")
WEIGHTS = ("norm_g", "mem_norm_g", "w_in", "b_gate", "ssm_lambda_re", "ssm_lambda_im", "ssm_log_dt", "ssm_b_re",
           "ssm_b_im", "ssm_c_re", "ssm_c_im", "ssm_d", "w_glu", "b_glu", "w_mem_kv", "w_br_ssm", "w_br_attn",
           "w_br_mem", "w_out", "rel_bias", "final_norm_g")
INPUTS = ("x", "mem") + WEIGHTS + ("loss_target",) + tuple("m_" + n for n in WEIGHTS) + tuple("v_" + n for n in WEIGHTS)


def _params(*sem):
    return pltpu.CompilerParams(dimension_semantics=sem, vmem_limit_bytes=VMEM_LIMIT_BYTES)


def _pick(dim, pref, align):
    if dim <= pref:
        return dim
    t = pref - pref % align
    while t >= align:
        if dim % t == 0:
            return t
        t -= align
    return dim


def _sigmoid(v):
    return 1.0 / (1.0 + jnp.exp(-v))


def _silu_and_grad(z):
    s = _sigmoid(z)
    return z * s, s * (1.0 + z * (1.0 - s))


_GELU_C = math.sqrt(2.0 / math.pi)


def _gelu_and_grad(y):
    inner = _GELU_C * (y + 0.044715 * y * y * y)
    t = jnp.tanh(inner)
    g = 0.5 * y * (1.0 + t)
    dg = 0.5 * (1.0 + t) + 0.5 * y * (1.0 - t * t) * _GELU_C * (1.0 + 3.0 * 0.044715 * y * y)
    return g, dg


def _dot(a, b, dims):
    return lax.dot_general(a, b, (dims, ((), ())), preferred_element_type=F32)


NN = ((1,), (0,))
NT = ((1,), (1,))
TN = ((0,), (0,))


def _matmul(a, b, *, mode, name, out_dtype=F32, add=None, split_a=1, tm=1024, tn=768, tk=2304,
            b_lead=None, b_off=0, n_cols=None, stack=None):
    if mode == "tn":
        K, M = a.shape
    else:
        M, K = a.shape
    bshape = b.shape if b_lead is None else b.shape[1:]
    N = n_cols or (bshape[0] if mode == "nt" else bshape[1])
    if mode != "tn" and M >= 4 * tm:
        tm = 2 * tm
    tm = _pick(M, tm, LANES if mode == "tn" else SUBLANES)
    tn = _pick(math.gcd(N, b_off) if b_off else N, tn, LANES)
    tk = _pick(K, tk, LANES)
    nk = K // tk
    joff = b_off // tn
    dims = {"nn": NN, "nt": NT, "tn": TN}[mode]
    has_add = add is not None
    has_prev = stack is not None and stack[2] is not None

    def body(*refs):
        a_ref, b_ref = refs[:2]
        add_ref = refs[2] if has_add else None
        o_ref = refs[-2] if nk > 1 else refs[-1]
        k = pl.program_id(2)
        bv = b_ref[...].astype(BF16)
        if split_a > 1:
            rest = a_ref[...].astype(F32)
            part = 0.0
            for _ in range(split_a):
                piece = rest.astype(BF16)
                part = part + _dot(piece, bv, dims)
                rest = rest - piece.astype(F32)
        else:
            part = _dot(a_ref[...].astype(BF16), bv, dims)

        def finish(r):
            if has_add:
                r = r + add_ref[...]
            o_ref[...] = r.astype(out_dtype)

        if nk == 1:
            finish(part)
            return
        acc_ref = refs[-1]

        @pl.when(k == 0)
        def _():
            acc_ref[...] = part

        @pl.when((k > 0) & (k < nk - 1))
        def _():
            acc_ref[...] += part

        @pl.when(k == nk - 1)
        def _():
            finish(acc_ref[...] + part)

    a_spec = pl.BlockSpec((tk, tm), lambda i, j, k: (k, i)) if mode == "tn" else pl.BlockSpec((tm, tk), lambda i, j, k: (i, k))
    lead = () if b_lead is None else (b_lead,)
    lead_blk = () if b_lead is None else (None,)
    if mode == "nt":
        b_spec = pl.BlockSpec(lead_blk + (tn, tk), lambda i, j, k: lead + (j + joff, k))
    else:
        b_spec = pl.BlockSpec(lead_blk + (tk, tn), lambda i, j, k: lead + (k, j + joff))
    in_specs = [a_spec, b_spec]
    args = [a, b]
    if has_add:
        in_specs.append(pl.BlockSpec((tm, tn), lambda i, j, k: (i, j)))
        args.append(add)
    aliases = {}
    if stack is None:
        out_spec = pl.BlockSpec((tm, tn), lambda i, j, k: (i, j))
        out_shape = jax.ShapeDtypeStruct((M, N), out_dtype)
    else:
        layer, depth, prev = stack
        out_spec = pl.BlockSpec((None, tm, tn), lambda i, j, k: (layer, i, j))
        out_shape = jax.ShapeDtypeStruct((depth, M, N), out_dtype)
        if has_prev:
            in_specs.append(pl.BlockSpec(memory_space=pl.ANY))
            args.append(prev)
            aliases = {len(args) - 1: 0}
    return pl.pallas_call(
        body, name=name, grid=(M // tm, N // tn, nk), in_specs=in_specs, out_specs=out_spec, out_shape=out_shape,
        scratch_shapes=[pltpu.VMEM((tm, tn), F32)] if nk > 1 else [], input_output_aliases=aliases,
        compiler_params=_params("parallel", "parallel", "arbitrary"),
    )(*args)


def _rmsnorm(x, g, name):
    T, D = x.shape
    tm = _pick(T, 512, SUBLANES)

    def body(x_ref, g_ref, h_ref):
        xv = x_ref[...]
        r = lax.rsqrt(jnp.mean(xv * xv, axis=-1, keepdims=True) + EPS)
        h_ref[...] = (xv * r * g_ref[...]).astype(BF16)

    return pl.pallas_call(
        body, name=name, grid=(T // tm,),
        in_specs=[pl.BlockSpec((tm, D), lambda i: (i, 0)), pl.BlockSpec((1, D), lambda i: (0, 0))],
        out_specs=pl.BlockSpec((tm, D), lambda i: (i, 0)),
        out_shape=jax.ShapeDtypeStruct((T, D), BF16), compiler_params=_params("parallel"),
    )(x, g.reshape(1, D))


def _rmsnorm_bwd(x, g, dh, dres, name):
    T, D = x.shape
    tm = _pick(T, 512, SUBLANES)
    with_res = dres is not None

    def body(*refs):
        if with_res:
            x_ref, g_ref, dh_ref, dres_ref, dx_ref, dg_ref = refs
        else:
            x_ref, g_ref, dh_ref, dx_ref, dg_ref = refs
        xv = x_ref[...]
        dhv = dh_ref[...]
        r = lax.rsqrt(jnp.mean(xv * xv, axis=-1, keepdims=True) + EPS)
        dyg = dhv * g_ref[...]
        c = jnp.mean(dyg * xv, axis=-1, keepdims=True)
        dx = r * dyg - xv * (r * r * r) * c
        if with_res:
            dx = dx + dres_ref[...]
        dx_ref[...] = dx

        @pl.when(pl.program_id(0) == 0)
        def _():
            dg_ref[...] = jnp.zeros_like(dg_ref)

        dg_ref[...] += jnp.sum(dhv * xv * r, axis=0, keepdims=True)

    row = pl.BlockSpec((tm, D), lambda i: (i, 0))
    vec = pl.BlockSpec((1, D), lambda i: (0, 0))
    ins = [x, g.reshape(1, D), dh] + ([dres] if with_res else [])
    return pl.pallas_call(
        body, name=name, grid=(T // tm,), in_specs=[row, vec, row] + ([row] if with_res else []),
        out_specs=[row, vec],
        out_shape=[jax.ShapeDtypeStruct((T, D), F32), jax.ShapeDtypeStruct((1, D), F32)],
        compiler_params=_params("arbitrary"),
    )(*ins)


def _loss_head(x, g, target, name):
    T, D = x.shape
    tm = _pick(T, 512, SUBLANES)

    def body(x_ref, g_ref, t_ref, loss_ref, dx_ref, dg_ref):
        xv = x_ref[...]
        gv = g_ref[...]
        r = lax.rsqrt(jnp.mean(xv * xv, axis=-1, keepdims=True) + EPS)
        e = xv * r * gv - t_ref[...]
        dy = e * (1.0 / D)
        dyg = dy * gv
        c = jnp.mean(dyg * xv, axis=-1, keepdims=True)
        dx_ref[...] = r * dyg - xv * (r * r * r) * c

        @pl.when(pl.program_id(0) == 0)
        def _():
            loss_ref[...] = jnp.zeros_like(loss_ref)
            dg_ref[...] = jnp.zeros_like(dg_ref)

        loss_ref[...] += jnp.sum(e * e, axis=0, keepdims=True) * (0.5 / D)
        dg_ref[...] += jnp.sum(dy * xv * r, axis=0, keepdims=True)

    row = pl.BlockSpec((tm, D), lambda i: (i, 0))
    vec = pl.BlockSpec((1, D), lambda i: (0, 0))
    return pl.pallas_call(
        body, name=name, grid=(T // tm,), in_specs=[row, vec, row], out_specs=[vec, row, vec],
        out_shape=[jax.ShapeDtypeStruct((1, D), F32), jax.ShapeDtypeStruct((T, D), F32), jax.ShapeDtypeStruct((1, D), F32)],
        compiler_params=_params("arbitrary"),
    )(x, g.reshape(1, D), target)


def _ssm_disc_math(lre, lim, logdt, br, bi):
    dt = jnp.exp(logdt)
    mag = jnp.exp(lre * dt)
    ar = mag * jnp.cos(lim * dt)
    ai = mag * jnp.sin(lim * dt)
    den = lre * lre + lim * lim
    nr = ar - 1.0
    fr = (nr * lre + ai * lim) / den
    fi = (ai * lre - nr * lim) / den
    return ar, ai, fr[None] * br - fi[None] * bi, fr[None] * bi + fi[None] * br


def _ssm_disc(lre, lim, logdt, br, bi, name):
    def body(lre_ref, lim_ref, dt_ref, br_ref, bi_ref, ar_ref, ai_ref, bbr_ref, bbi_ref):
        ar, ai, bbr, bbi = _ssm_disc_math(lre_ref[...], lim_ref[...], dt_ref[...], br_ref[...], bi_ref[...])
        ar_ref[...] = ar
        ai_ref[...] = ai
        bbr_ref[...] = bbr
        bbi_ref[...] = bbi

    sd = jax.ShapeDtypeStruct
    return pl.pallas_call(
        body, name=name, out_shape=[sd(lre.shape, F32), sd(lre.shape, F32), sd(br.shape, F32), sd(br.shape, F32)],
    )(lre, lim, logdt, br, bi)


def _ssm_disc_bwd(lre, lim, logdt, br, bi, dar, dai, dbbr, dbbi, name):
    def body(lre_ref, lim_ref, dt_ref, br_ref, bi_ref, dar_ref, dai_ref, dbbr_ref, dbbi_ref,
             glre_ref, glim_ref, gdt_ref, gbr_ref, gbi_ref):
        _, vjp = jax.vjp(_ssm_disc_math, lre_ref[...], lim_ref[...], dt_ref[...], br_ref[...], bi_ref[...])
        glre, glim, gdt, gbr, gbi = vjp((dar_ref[...], dai_ref[...], dbbr_ref[...], dbbi_ref[...]))
        glre_ref[...] = glre
        glim_ref[...] = glim
        gdt_ref[...] = gdt
        gbr_ref[...] = gbr
        gbi_ref[...] = gbi

    sd = jax.ShapeDtypeStruct
    return pl.pallas_call(
        body, name=name,
        out_shape=[sd(lre.shape, F32), sd(lre.shape, F32), sd(logdt.shape, F32), sd(br.shape, F32), sd(br.shape, F32)],
    )(lre, lim, logdt, br, bi, dar, dai, dbbr, dbbi)


def _shift_segments(v, down):
    n = v.shape[0]
    rows = lax.broadcasted_iota(jnp.int32, v.shape, 0)
    if down:
        return jnp.where(rows >= 1, pltpu.roll(v, 1, 0), 0.0)
    return jnp.where(rows < n - 1, pltpu.roll(v, n - 1, 0), 0.0)


def _cpow(ar, ai, n):
    rr, ri = None, None
    pr, pi = ar, ai
    while n:
        if n & 1:
            if rr is None:
                rr, ri = pr, pi
            else:
                rr, ri = rr * pr - ri * pi, rr * pi + ri * pr
        n >>= 1
        if n:
            pr, pi = pr * pr - pi * pi, 2.0 * pr * pi
    return rr, ri


def _ssm_geometry(T, C):
    seg_steps = T // SSM_SEGMENTS
    kc = min(SSM_CHUNK_STEPS, seg_steps)
    return C // SSM_BLOCK_CH, seg_steps, kc, seg_steps // kc, SSM_SEGMENTS * kc


def _ssm_carries(u, dy, bmat, cmat, amat, *, reverse, name):
    src = dy if reverse else u
    T, C = src.shape
    nblk, seg_steps, kc, nchunk, rc = _ssm_geometry(T, C)
    half = bmat.shape[2] // 2

    def body(src_ref, w_ref, a_ref, out_ref, buf_ref, st_ref):
        c = pl.program_id(1)

        @pl.when(c == 0)
        def _():
            st_ref[...] = jnp.zeros_like(st_ref)

        if reverse:
            buf_ref[...] = _dot(src_ref[...].astype(BF16), w_ref[0], NT)
        else:
            buf_ref[...] = _dot(src_ref[...].astype(BF16), w_ref[0], NN)
        ar = a_ref[0, :, :half]
        ai = a_ref[0, :, half:]
        if reverse:
            ai = -ai

        def step(i, carry):
            xr, xi = carry
            k = (kc - 1 - i) if reverse else i
            row = pl.multiple_of(k * SUBLANES, SUBLANES)
            br = buf_ref[pl.ds(row, SUBLANES), :half]
            bi = buf_ref[pl.ds(row, SUBLANES), half:]
            return ar * xr - ai * xi + br, ar * xi + ai * xr + bi

        xr, xi = lax.fori_loop(0, kc, step, (st_ref[:, :half], st_ref[:, half:]), unroll=8)
        st_ref[:, :half] = xr
        st_ref[:, half:] = xi

        @pl.when(c == nchunk - 1)
        def _():
            pr, pi = _cpow(ar, ai, seg_steps)
            sr = jnp.zeros_like(xr)
            si = jnp.zeros_like(xi)
            for _ in range(SSM_SEGMENTS - 1):
                nr = xr + pr * sr - pi * si
                ni = xi + pr * si + pi * sr
                sr = _shift_segments(nr, not reverse)
                si = _shift_segments(ni, not reverse)
            out_ref[0, :, :half] = sr
            out_ref[0, :, half:] = si

    cidx = (lambda b, c: (nchunk - 1 - c, b)) if reverse else (lambda b, c: (c, b))
    w = cmat if reverse else bmat
    return pl.pallas_call(
        body, name=name, grid=(nblk, nchunk),
        in_specs=[pl.BlockSpec((rc, SSM_BLOCK_CH), cidx),
                  pl.BlockSpec((1,) + w.shape[1:], lambda b, c: (b, 0, 0)),
                  pl.BlockSpec((1, SUBLANES, 2 * half), lambda b, c: (b, 0, 0))],
        out_specs=pl.BlockSpec((1, SUBLANES, 2 * half), lambda b, c: (b, 0, 0)),
        out_shape=jax.ShapeDtypeStruct((nblk, SUBLANES, 2 * half), F32),
        scratch_shapes=[pltpu.VMEM((rc, 2 * half), F32), pltpu.VMEM((SUBLANES, 2 * half), F32)],
        compiler_params=_params("parallel", "arbitrary"),
    )(src, w, amat)


def _ssm_scan(u, bmat, cmat, amat, carries, dvec, name):
    T, C = u.shape
    nblk, seg_steps, kc, nchunk, rc = _ssm_geometry(T, C)
    half = bmat.shape[2] // 2

    def body(u_ref, b_ref, c_ref, a_ref, s_ref, d_ref, y_ref, x_ref, st_ref):
        c = pl.program_id(1)

        @pl.when(c == 0)
        def _():
            st_ref[...] = s_ref[0]

        uv = u_ref[...]
        x_ref[...] = _dot(uv.astype(BF16), b_ref[0], NN)
        ar = a_ref[0, :, :half]
        ai = a_ref[0, :, half:]

        def step(k, carry):
            xr, xi = carry
            row = pl.multiple_of(k * SUBLANES, SUBLANES)
            nr = ar * xr - ai * xi + x_ref[pl.ds(row, SUBLANES), :half]
            ni = ar * xi + ai * xr + x_ref[pl.ds(row, SUBLANES), half:]
            x_ref[pl.ds(row, SUBLANES), :half] = nr
            x_ref[pl.ds(row, SUBLANES), half:] = ni
            return nr, ni

        xr, xi = lax.fori_loop(0, kc, step, (st_ref[:, :half], st_ref[:, half:]), unroll=8)
        st_ref[:, :half] = xr
        st_ref[:, half:] = xi
        y_ref[...] = _dot(x_ref[...].astype(BF16), c_ref[0], NN) + d_ref[...] * uv

    return pl.pallas_call(
        body, name=name, grid=(nblk, nchunk),
        in_specs=[pl.BlockSpec((rc, SSM_BLOCK_CH), lambda b, c: (c, b)),
                  pl.BlockSpec((1,) + bmat.shape[1:], lambda b, c: (b, 0, 0)),
                  pl.BlockSpec((1,) + cmat.shape[1:], lambda b, c: (b, 0, 0)),
                  pl.BlockSpec((1, SUBLANES, 2 * half), lambda b, c: (b, 0, 0)),
                  pl.BlockSpec((1, SUBLANES, 2 * half), lambda b, c: (b, 0, 0)),
                  pl.BlockSpec((1, SSM_BLOCK_CH), lambda b, c: (0, b))],
        out_specs=[pl.BlockSpec((rc, SSM_BLOCK_CH), lambda b, c: (c, b)),
                   pl.BlockSpec((rc, 2 * half), lambda b, c: (c, b))],
        out_shape=[jax.ShapeDtypeStruct((T, C), F32), jax.ShapeDtypeStruct((T, nblk * 2 * half), F32)],
        scratch_shapes=[pltpu.VMEM((SUBLANES, 2 * half), F32)],
        compiler_params=_params("parallel", "arbitrary"),
    )(u, bmat, cmat, amat, carries, dvec)


def _ssm_scan_bwd(dy, u, xs, bmat, cmat, amat, carries, dvec, name):
    T, C = u.shape
    nblk, seg_steps, kc, nchunk, rc = _ssm_geometry(T, C)
    half = bmat.shape[2] // 2
    width = 2 * half

    def body(dy_ref, u_ref, x_ref, xp_ref, b_ref, c_ref, a_ref, s_ref, d_ref,
             du_ref, db_ref, dc_ref, da_ref, dd_ref, g_ref, st_ref, acc_ref):
        c = pl.program_id(1)

        @pl.when(c == 0)
        def _():
            st_ref[...] = s_ref[0]
            acc_ref[...] = jnp.zeros_like(acc_ref)
            db_ref[...] = jnp.zeros_like(db_ref)
            dc_ref[...] = jnp.zeros_like(dc_ref)
            dd_ref[...] = jnp.zeros_like(dd_ref)

        dyv = dy_ref[...]
        uv = u_ref[...]
        dyb = dyv.astype(BF16)
        g_ref[...] = _dot(dyb, c_ref[0], NT)
        ar = a_ref[0, :, :half]
        ai = a_ref[0, :, half:]

        def step(i, carry):
            gr, gi, sr, si = carry
            k = kc - 1 - i
            row = pl.multiple_of(k * SUBLANES, SUBLANES)
            nr = ar * gr + ai * gi + g_ref[pl.ds(row, SUBLANES), :half]
            ni = ar * gi - ai * gr + g_ref[pl.ds(row, SUBLANES), half:]
            g_ref[pl.ds(row, SUBLANES), :half] = nr
            g_ref[pl.ds(row, SUBLANES), half:] = ni
            prow = pl.multiple_of(jnp.maximum(k - 1, 0) * SUBLANES, SUBLANES)
            live = (k >= 1).astype(F32)
            xr = x_ref[pl.ds(prow, SUBLANES), :half] * live
            xi = x_ref[pl.ds(prow, SUBLANES), half:] * live
            return nr, ni, sr + xr * nr + xi * ni, si + xr * ni - xi * nr

        init = (st_ref[:, :half], st_ref[:, half:], acc_ref[:, :half], acc_ref[:, half:])
        gr, gi, sr, si = lax.fori_loop(0, kc, step, init, unroll=8)
        st_ref[:, :half] = gr
        st_ref[:, half:] = gi
        xpr = xp_ref[:, :half]
        xpi = xp_ref[:, half:]
        first = (c == nchunk - 1)
        xpr = jnp.where(first, _shift_segments(xpr, True), xpr)
        xpi = jnp.where(first, _shift_segments(xpi, True), xpi)
        acc_ref[:, :half] = sr + xpr * gr + xpi * gi
        acc_ref[:, half:] = si + xpr * gi - xpi * gr

        gb = g_ref[...].astype(BF16)
        du_ref[...] = _dot(gb, b_ref[0], NT) + dyv * d_ref[...]
        db_ref[0] += _dot(uv.astype(BF16), gb, TN)
        dc_ref[0] += _dot(dyb, x_ref[...].astype(BF16), TN)
        dd_ref[...] += jnp.sum(dyv * uv, axis=0, keepdims=True)

        @pl.when(c == nchunk - 1)
        def _():
            tot = jnp.sum(acc_ref[...], axis=0, keepdims=True)
            da_ref[0] = jnp.broadcast_to(tot, (SUBLANES, width))

    rev = lambda b, c: (nchunk - 1 - c, b)
    blk3 = lambda b, c: (b, 0, 0)
    prev_group = lambda b, c: (((nchunk - 1 - c) * kc - 1 + seg_steps) % seg_steps, b)
    sd = jax.ShapeDtypeStruct
    return pl.pallas_call(
        body, name=name, grid=(nblk, nchunk),
        in_specs=[pl.BlockSpec((rc, SSM_BLOCK_CH), rev), pl.BlockSpec((rc, SSM_BLOCK_CH), rev),
                  pl.BlockSpec((rc, width), rev), pl.BlockSpec((SUBLANES, width), prev_group),
                  pl.BlockSpec((1,) + bmat.shape[1:], blk3), pl.BlockSpec((1,) + cmat.shape[1:], blk3),
                  pl.BlockSpec((1, SUBLANES, width), blk3), pl.BlockSpec((1, SUBLANES, width), blk3),
                  pl.BlockSpec((1, SSM_BLOCK_CH), lambda b, c: (0, b))],
        out_specs=[pl.BlockSpec((rc, SSM_BLOCK_CH), rev), pl.BlockSpec((1, SSM_BLOCK_CH, width), blk3),
                   pl.BlockSpec((1, SSM_BLOCK_CH, width), blk3), pl.BlockSpec((1, SUBLANES, width), blk3),
                   pl.BlockSpec((1, SSM_BLOCK_CH), lambda b, c: (0, b))],
        out_shape=[sd((T, C), F32), sd((nblk, SSM_BLOCK_CH, width), F32), sd((nblk, SSM_BLOCK_CH, width), F32),
                   sd((nblk, SUBLANES, width), F32), sd((1, C), F32)],
        scratch_shapes=[pltpu.VMEM((rc, width), F32), pltpu.VMEM((SUBLANES, width), F32), pltpu.VMEM((SUBLANES, width), F32)],
        compiler_params=_params("parallel", "arbitrary"),
    )(dy, u, xs, xs, bmat, cmat, amat, carries, dvec)


SSM_STEPS_FWD = 128
SSM_STEPS_BWD = 256


def _ssm_tiles(ref, v, off, steps, n):
    return [ref[v, pl.ds(off + j, steps, stride=SUBLANES), :] for j in range(n)]


def _ssm_fwd(uz, bmat, cmat, art, ait, dvec, name):
    T = uz.shape[0]
    nblk, cb, width = bmat.shape
    C = nblk * cb
    half = width // 2
    nt = half // LANES
    npair = nblk // 2
    kc = min(SSM_STEPS_FWD, T)
    nchunk = T // kc

    def body(u_ref, b_ref, c_ref, ar_ref, ai_ref, d_ref, y_ref, xr_ref, xi_ref, sr_ref, si_ref):
        @pl.when(pl.program_id(0) == 0)
        def _():
            sr_ref[...] = jnp.zeros_like(sr_ref)
            si_ref[...] = jnp.zeros_like(si_ref)

        uv = u_ref[...]
        for b in range(nblk):
            bu = _dot(uv[:, b * cb:(b + 1) * cb].astype(BF16), b_ref[b], NN)
            v, off = b // 2, nt * (b % 2)
            for j in range(nt):
                xr_ref[v, pl.ds(off + j, kc, stride=SUBLANES), :] = bu[:, j * LANES:(j + 1) * LANES]
                xi_ref[v, pl.ds(off + j, kc, stride=SUBLANES), :] = bu[:, half + j * LANES:half + (j + 1) * LANES]
        ars = [ar_ref[v] for v in range(npair)]
        ais = [ai_ref[v] for v in range(npair)]

        def step(k, carry):
            row = pl.ds(k * SUBLANES, SUBLANES)
            out = []
            for v in range(npair):
                xr, xi = carry[2 * v], carry[2 * v + 1]
                nr = ars[v] * xr - ais[v] * xi + xr_ref[v, row, :]
                ni = ars[v] * xi + ais[v] * xr + xi_ref[v, row, :]
                xr_ref[v, row, :] = nr
                xi_ref[v, row, :] = ni
                out += [nr, ni]
            return tuple(out)

        fin = tuple(ref[v] for v in range(npair) for ref in (sr_ref, si_ref))
        for k in range(kc):
            fin = step(k, fin)
        for v in range(npair):
            sr_ref[v] = fin[2 * v]
            si_ref[v] = fin[2 * v + 1]
        for b in range(nblk):
            v, off = b // 2, nt * (b % 2)
            xb = jnp.concatenate(_ssm_tiles(xr_ref, v, off, kc, nt) + _ssm_tiles(xi_ref, v, off, kc, nt), axis=1)
            cols = slice(b * cb, (b + 1) * cb)
            y_ref[:, cols] = _dot(xb.astype(BF16), c_ref[b], NN) + d_ref[:, cols] * uv[:, cols]

    whole = lambda a: pl.BlockSpec(a.shape, lambda c: (0,) * a.ndim)
    st = pl.BlockSpec((npair, kc * SUBLANES, LANES), lambda c: (0, c, 0))
    sd = jax.ShapeDtypeStruct
    return pl.pallas_call(
        body, name=name, grid=(nchunk,),
        in_specs=[pl.BlockSpec((kc, C), lambda c: (c, 0)), whole(bmat), whole(cmat), whole(art), whole(ait), whole(dvec)],
        out_specs=[pl.BlockSpec((kc, C), lambda c: (c, 0)), st, st],
        out_shape=[sd((T, C), F32), sd((npair, T * SUBLANES, LANES), F32), sd((npair, T * SUBLANES, LANES), F32)],
        scratch_shapes=[pltpu.VMEM((npair, SUBLANES, LANES), F32), pltpu.VMEM((npair, SUBLANES, LANES), F32)],
        compiler_params=_params("arbitrary"),
    )(uz, bmat, cmat, art, ait, dvec)


def _ssm_bwd(dy, uz, xr, xi, bmat, cmat, art, ait, dvec, name):
    T = uz.shape[0]
    nblk, cb, width = bmat.shape
    C = nblk * cb
    half = width // 2
    nt = half // LANES
    npair = nblk // 2
    kc = min(SSM_STEPS_BWD, T)
    nchunk = T // kc

    def body(dy_ref, u_ref, xr_ref, xi_ref, xpr_ref, xpi_ref, b_ref, c_ref, ar_ref, ai_ref, d_ref,
             du_ref, db_ref, dc_ref, dar_ref, dai_ref, dd_ref, gr_ref, gi_ref, sr_ref, si_ref):
        c = pl.program_id(0)

        @pl.when(c == 0)
        def _():
            for ref in (sr_ref, si_ref, db_ref, dc_ref, dar_ref, dai_ref, dd_ref):
                ref[...] = jnp.zeros_like(ref)

        dyv = dy_ref[...]
        uv = u_ref[...]
        for b in range(nblk):
            dx = _dot(dyv[:, b * cb:(b + 1) * cb].astype(BF16), c_ref[b], NT)
            v, off = b // 2, nt * (b % 2)
            for j in range(nt):
                gr_ref[v, pl.ds(off + j, kc, stride=SUBLANES), :] = dx[:, j * LANES:(j + 1) * LANES]
                gi_ref[v, pl.ds(off + j, kc, stride=SUBLANES), :] = dx[:, half + j * LANES:half + (j + 1) * LANES]
        ars = [ar_ref[v] for v in range(npair)]
        ais = [ai_ref[v] for v in range(npair)]

        def pair_update(v, gr, gi, row):
            nr = ars[v] * gr + ais[v] * gi + gr_ref[v, row, :]
            ni = ars[v] * gi - ais[v] * gr + gi_ref[v, row, :]
            gr_ref[v, row, :] = nr
            gi_ref[v, row, :] = ni
            return nr, ni

        def step(i, carry):
            k = kc - 1 - i
            row = pl.ds(pl.multiple_of(k * SUBLANES, SUBLANES), SUBLANES)
            prow = pl.ds(pl.multiple_of((k - 1) * SUBLANES, SUBLANES), SUBLANES)
            out = []
            for v in range(npair):
                gr, gi, sr, si = carry[4 * v:4 * v + 4]
                nr, ni = pair_update(v, gr, gi, row)
                pr, pi = xr_ref[v, prow, :], xi_ref[v, prow, :]
                out += [nr, ni, sr + pr * nr + pi * ni, si + pr * ni - pi * nr]
            return tuple(out)

        init = tuple(ref[v] for v in range(npair) for ref in (sr_ref, si_ref, dar_ref, dai_ref))
        mid = lax.fori_loop(0, kc - 1, step, init, unroll=5 if (kc - 1) % 5 == 0 else 1)
        live = (c < nchunk - 1).astype(F32)
        row0 = pl.ds(0, SUBLANES)
        for v in range(npair):
            gr, gi, sr, si = mid[4 * v:4 * v + 4]
            nr, ni = pair_update(v, gr, gi, row0)
            pr, pi = xpr_ref[v] * live, xpi_ref[v] * live
            sr_ref[v] = nr
            si_ref[v] = ni
            dar_ref[v] = sr + pr * nr + pi * ni
            dai_ref[v] = si + pr * ni - pi * nr
        for b in range(nblk):
            v, off = b // 2, nt * (b % 2)
            cols = slice(b * cb, (b + 1) * cb)
            gb = jnp.concatenate(_ssm_tiles(gr_ref, v, off, kc, nt) + _ssm_tiles(gi_ref, v, off, kc, nt), axis=1).astype(BF16)
            xb = jnp.concatenate(_ssm_tiles(xr_ref, v, off, kc, nt) + _ssm_tiles(xi_ref, v, off, kc, nt), axis=1).astype(BF16)
            du_ref[:, cols] = _dot(gb, b_ref[b], NT) + dyv[:, cols] * d_ref[:, cols]
            db_ref[b] += _dot(uv[:, cols].astype(BF16), gb, TN)
            dc_ref[b] += _dot(dyv[:, cols].astype(BF16), xb, TN)
        dd_ref[...] += jnp.sum(dyv * uv, axis=0, keepdims=True)

    whole = lambda a: pl.BlockSpec(a.shape, lambda c: (0,) * a.ndim)
    rev = lambda c: (nchunk - 1 - c, 0)
    st = pl.BlockSpec((npair, kc * SUBLANES, LANES), lambda c: (0, nchunk - 1 - c, 0))
    stp = pl.BlockSpec((npair, SUBLANES, LANES), lambda c: (0, jnp.maximum((nchunk - 1 - c) * kc - 1, 0), 0))
    acc = lambda shape: pl.BlockSpec(shape, lambda c: (0,) * len(shape))
    sd = jax.ShapeDtypeStruct
    pair_shape = (npair, SUBLANES, LANES)
    return pl.pallas_call(
        body, name=name, grid=(nchunk,),
        in_specs=[pl.BlockSpec((kc, C), rev), pl.BlockSpec((kc, C), rev), st, st, stp, stp, whole(bmat), whole(cmat),
                  whole(art), whole(ait), whole(dvec)],
        out_specs=[pl.BlockSpec((kc, C), rev), acc(bmat.shape), acc(bmat.shape), acc(pair_shape), acc(pair_shape), acc((1, C))],
        out_shape=[sd((T, C), F32), sd(bmat.shape, F32), sd(bmat.shape, F32), sd(pair_shape, F32), sd(pair_shape, F32),
                   sd((1, C), F32)],
        scratch_shapes=[pltpu.VMEM((npair, kc * SUBLANES, LANES), F32), pltpu.VMEM((npair, kc * SUBLANES, LANES), F32),
                        pltpu.VMEM(pair_shape, F32), pltpu.VMEM(pair_shape, F32)],
        compiler_params=_params("arbitrary"),
    )(dy, uz, xr, xi, xr, xi, bmat, cmat, art, ait, dvec)


def _ssm_post(y, z, w_glu, b_glu, name):
    T, C = y.shape
    tm = _pick(T, 512, SUBLANES)

    def body(y_ref, z_ref, w_ref, b_ref, o_ref, a_ref):
        a, _ = _gelu_and_grad(y_ref[...])
        ab = a.astype(BF16)
        sg = _sigmoid(_dot(ab, w_ref[...], NN) + b_ref[...])
        sz, _ = _silu_and_grad(z_ref[...].astype(F32))
        o_ref[...] = (a * sg * sz).astype(BF16)
        a_ref[...] = ab

    row = pl.BlockSpec((tm, C), lambda i: (i, 0))
    return pl.pallas_call(
        body, name=name, grid=(T // tm,),
        in_specs=[row, row, pl.BlockSpec((C, C), lambda i: (0, 0)), pl.BlockSpec((1, C), lambda i: (0, 0))],
        out_specs=[row, row], out_shape=[jax.ShapeDtypeStruct((T, C), BF16)] * 2, compiler_params=_params("parallel"),
    )(y, z, w_glu, b_glu.reshape(1, C))


def _ssm_post_bwd(do, y, z, w_glu, b_glu, name):
    T, C = y.shape
    tm = _pick(T, 512, SUBLANES)

    def body(do_ref, y_ref, z_ref, w_ref, b_ref, dy_ref, dz_ref, ds_ref, db_ref):
        dov = do_ref[...]
        a, da_dy = _gelu_and_grad(y_ref[...])
        sg = _sigmoid(_dot(a.astype(BF16), w_ref[...], NN) + b_ref[...])
        sz, dsz = _silu_and_grad(z_ref[...].astype(F32))
        yg = a * sg
        dz_ref[...] = (dov * yg * dsz).astype(BF16)
        dyg = dov * sz
        ds = dyg * a * sg * (1.0 - sg)
        dsb = ds.astype(BF16)
        ds_ref[...] = dsb
        da = dyg * sg + _dot(dsb, w_ref[...], NT)
        dy_ref[...] = da * da_dy

        @pl.when(pl.program_id(0) == 0)
        def _():
            db_ref[...] = jnp.zeros_like(db_ref)

        db_ref[...] += jnp.sum(ds, axis=0, keepdims=True)

    row = pl.BlockSpec((tm, C), lambda i: (i, 0))
    vec = pl.BlockSpec((1, C), lambda i: (0, 0))
    sd = jax.ShapeDtypeStruct
    return pl.pallas_call(
        body, name=name, grid=(T // tm,),
        in_specs=[row, row, row, pl.BlockSpec((C, C), lambda i: (0, 0)), vec],
        out_specs=[row, row, row, vec],
        out_shape=[sd((T, C), F32), sd((T, C), BF16), sd((T, C), BF16), sd((1, C), F32)],
        compiler_params=_params("arbitrary"),
    )(do, y, z, w_glu, b_glu.reshape(1, C))


def _rel_bucket(dist):
    n = jnp.maximum(dist, 0)
    max_exact = NUM_BUCKETS // 2
    n_f = jnp.maximum(n, 1).astype(F32)
    large = max_exact + (jnp.log(n_f / max_exact) / math.log(REL_MAX_DISTANCE / max_exact)
                         * (NUM_BUCKETS - max_exact)).astype(jnp.int32)
    large = jnp.minimum(large, NUM_BUCKETS - 1)
    return jnp.where(n < max_exact, n, large)


def _band_tables():
    qi = jnp.arange(ATTN_BLOCK)[:, None]
    kj = jnp.arange(2 * ATTN_BLOCK)[None, :]
    delta = ATTN_BLOCK + qi - kj
    buckets, bands = [], []
    for window, dilation in ATTN_CONFIGS:
        bands.append((delta >= 0) & (delta <= window // dilation))
        buckets.append(_rel_bucket(jnp.maximum(delta, 0) * dilation))
    return jnp.stack(buckets), jnp.stack(bands)


def _attn_blocks_per_residue(T):
    return [T // (ATTN_BLOCK * d) for _, d in ATTN_CONFIGS]


ATTN_UNITS = 4


def _attn_tile(T, r):
    nq = max(1, ATTN_UNITS // r)
    rows = ATTN_BLOCK * r * nq
    return nq, rows, T // rows


def _attn_units(r, nq, chunk):
    if r >= ATTN_UNITS:
        return [(chunk * ATTN_UNITS + i, None) for i in range(ATTN_UNITS)]
    units = []
    for j in range(nq):
        for s in range(r):
            units.append((ATTN_BLOCK * j * r + s, ATTN_BLOCK * (j - 1) * r + s if j else None))
    return units


def _rows(start, r):
    return pl.ds(start, ATTN_BLOCK, stride=r) if r > 1 else pl.ds(start, ATTN_BLOCK)


def _attn_group_fwd(qkv, biasm, g, name):
    T = qkv.shape[0]
    r = ATTN_CONFIGS[g][1]
    B, hd = ATTN_BLOCK, ATTN_HEAD_DIM
    nq, rows, ntiles = _attn_tile(T, r)
    nchunks = max(1, r // ATTN_UNITS)
    last_prev = B * (nq - 1) * r
    scale = hd ** -0.5
    tiles_per_tensor = 3 * HEADS_PER_GROUP * hd // LANES

    def body(q_ref, kc_ref, kp_ref, vc_ref, vp_ref, bias_ref, o_ref, lse_ref, s_ref, p_ref):
        n = pl.program_id(1)
        lane = lax.broadcasted_iota(jnp.int32, (1, LANES), 1)
        col = lax.broadcasted_iota(jnp.int32, (1, 2 * B), 1)
        masks = [lane < hd, lane >= hd]
        first_pen = jnp.where((col < B) & (n == 0), NEG_INF, 0.0)

        def chunk_body(chunk):
            units = _attn_units(r, nq, chunk)

            def keys(cur_ref, prev_ref, cs, ps):
                prev = prev_ref[_rows(last_prev + (cs if r >= ATTN_UNITS else cs % r), r), :] if ps is None else cur_ref[_rows(ps, r), :]
                return jnp.concatenate([prev, cur_ref[_rows(cs, r), :]], axis=0).astype(BF16)

            for u, (cs, ps) in enumerate(units):
                qv = q_ref[_rows(cs, r), :]
                kw = keys(kc_ref, kp_ref, cs, ps)
                for hh in range(2):
                    s_ref[2 * u + hh] = _dot(jnp.where(masks[hh], qv, 0.0).astype(BF16), kw, NT)
            for u, (cs, ps) in enumerate(units):
                lses = []
                for hh in range(2):
                    s = s_ref[2 * u + hh] * scale + bias_ref[hh]
                    if ps is None:
                        s = s + first_pen
                    m = jnp.max(s, axis=-1, keepdims=True)
                    p = jnp.exp(s - m)
                    l = jnp.sum(p, axis=-1, keepdims=True)
                    p_ref[2 * u + hh] = (p / l).astype(BF16)
                    lses.append(m + jnp.log(l))
                lse_ref[_rows(cs, r), :] = jnp.where(masks[0], lses[0], lses[1])
            for u, (cs, ps) in enumerate(units):
                vw = keys(vc_ref, vp_ref, cs, ps)
                o_ref[_rows(cs, r), :] = (_dot(p_ref[2 * u], jnp.where(masks[0], vw, 0), NN)
                                          + _dot(p_ref[2 * u + 1], jnp.where(masks[1], vw, 0), NN))

        if nchunks == 1:
            chunk_body(0)
        else:
            pl.loop(0, nchunks)(chunk_body)

    def cur(t):
        return pl.BlockSpec((rows, LANES), lambda hf, n: (n, t * tiles_per_tensor + 2 * g + hf))

    def prev(t):
        return pl.BlockSpec((rows, LANES), lambda hf, n: (jnp.maximum(n - 1, 0), t * tiles_per_tensor + 2 * g + hf))

    out = pl.BlockSpec((rows, LANES), lambda hf, n: (n, hf))
    sd = jax.ShapeDtypeStruct((T, 2 * LANES), F32)
    return pl.pallas_call(
        body, name=name, grid=(2, ntiles),
        in_specs=[cur(0), cur(1), prev(1), cur(2), prev(2), pl.BlockSpec((None, 2, B, 2 * B), lambda hf, n: (g, hf, 0, 0))],
        out_specs=[out, out], out_shape=[sd, sd],
        scratch_shapes=[pltpu.VMEM((2 * ATTN_UNITS, B, 2 * B), F32), pltpu.VMEM((2 * ATTN_UNITS, B, 2 * B), BF16)],
        compiler_params=_params("parallel", "parallel"),
    )(qkv, qkv, qkv, qkv, qkv, biasm)


def _attn_group_bwd(qkv, do, dvec, lse, biasm, g, name):
    T = qkv.shape[0]
    r = ATTN_CONFIGS[g][1]
    B, hd = ATTN_BLOCK, ATTN_HEAD_DIM
    nq, rows, ntiles = _attn_tile(T, r)
    nchunks = max(1, r // ATTN_UNITS)
    last_prev = B * (nq - 1) * r
    scale = hd ** -0.5
    tiles_per_tensor = 3 * HEADS_PER_GROUP * hd // LANES

    def body(q_ref, kc_ref, kp_ref, vc_ref, vp_ref, do_ref, dv_ref, lse_ref, bias_ref,
             dq_ref, dk_ref, dvo_ref, dbias_ref, ck_ref, cv_ref, ak_ref, av_ref, s_ref, dp_ref, p_ref, ds_ref):
        n = pl.program_id(1)
        lane = lax.broadcasted_iota(jnp.int32, (1, LANES), 1)
        col = lax.broadcasted_iota(jnp.int32, (1, 2 * B), 1)
        masks = [lane < hd, lane >= hd]
        first_pen = jnp.where((col < B) & (n == 0), NEG_INF, 0.0)

        @pl.when(n == 0)
        def _():
            dbias_ref[...] = jnp.zeros_like(dbias_ref)
            ck_ref[...] = jnp.zeros_like(ck_ref)
            cv_ref[...] = jnp.zeros_like(cv_ref)

        def chunk_body(chunk):
            units = _attn_units(r, nq, chunk)

            def prev_rows(cs):
                return _rows(last_prev + (cs if r >= ATTN_UNITS else cs % r), r)

            def keys(cur_ref, prev_ref, cs, ps):
                prev = prev_ref[prev_rows(cs), :] if ps is None else cur_ref[_rows(ps, r), :]
                return jnp.concatenate([prev, cur_ref[_rows(cs, r), :]], axis=0).astype(BF16)

            for u, (cs, ps) in enumerate(units):
                qv = q_ref[_rows(cs, r), :]
                dov = do_ref[_rows(cs, r), :]
                kw = keys(kc_ref, kp_ref, cs, ps)
                vw = keys(vc_ref, vp_ref, cs, ps)
                for hh in range(2):
                    s_ref[2 * u + hh] = _dot(jnp.where(masks[hh], qv, 0.0).astype(BF16), kw, NT)
                    dp_ref[2 * u + hh] = _dot(jnp.where(masks[hh], dov, 0.0).astype(BF16), vw, NT)
            for u, (cs, ps) in enumerate(units):
                lse_t = lse_ref[_rows(cs, r), :]
                dv_t = dv_ref[_rows(cs, r), :]
                for hh in range(2):
                    lo = hh * hd
                    s = s_ref[2 * u + hh] * scale + bias_ref[hh]
                    if ps is None:
                        s = s + first_pen
                    p = jnp.exp(s - lse_t[:, lo:lo + 1])
                    ds = p * (dp_ref[2 * u + hh] + dv_t[:, lo:lo + 1])
                    dbias_ref[hh] += ds
                    p_ref[2 * u + hh] = p.astype(BF16)
                    ds_ref[2 * u + hh] = ds.astype(BF16)
            for u, (cs, ps) in enumerate(units):
                qv = q_ref[_rows(cs, r), :]
                dov = do_ref[_rows(cs, r), :]
                kw = keys(kc_ref, kp_ref, cs, ps)
                dq, dkw, dvw = 0.0, 0.0, 0.0
                for hh in range(2):
                    dsb = ds_ref[2 * u + hh]
                    dq = dq + _dot(dsb, jnp.where(masks[hh], kw, 0), NN)
                    dkw = dkw + _dot(dsb, jnp.where(masks[hh], qv, 0.0).astype(BF16), TN)
                    dvw = dvw + _dot(p_ref[2 * u + hh], jnp.where(masks[hh], dov, 0.0).astype(BF16), TN)
                dq_ref[_rows(cs, r), :] = dq * scale
                ak_ref[_rows(cs, r), :] = dkw[B:] * scale
                av_ref[_rows(cs, r), :] = dvw[B:]
                if ps is None:
                    ck_ref[prev_rows(cs), :] += dkw[:B] * scale
                    cv_ref[prev_rows(cs), :] += dvw[:B]
                else:
                    ak_ref[_rows(ps, r), :] += dkw[:B] * scale
                    av_ref[_rows(ps, r), :] += dvw[:B]

        @pl.when(n < ntiles)
        def _():
            for chunk in range(nchunks):
                chunk_body(chunk)

        dk_ref[...] = ck_ref[...].astype(BF16)
        dvo_ref[...] = cv_ref[...].astype(BF16)
        ck_ref[...] = ak_ref[...]
        cv_ref[...] = av_ref[...]

    last = ntiles - 1

    def cur(t):
        return pl.BlockSpec((rows, LANES), lambda hf, n: (jnp.minimum(n, last), t * tiles_per_tensor + 2 * g + hf))

    def prev(t):
        return pl.BlockSpec((rows, LANES), lambda hf, n: (jnp.clip(n - 1, 0, last), t * tiles_per_tensor + 2 * g + hf))

    nat = pl.BlockSpec((rows, LANES), lambda hf, n: (jnp.minimum(n, last), hf))
    nat_prev = pl.BlockSpec((rows, LANES), lambda hf, n: (jnp.clip(n - 1, 0, last), hf))
    tab = pl.BlockSpec((None, 2, B, 2 * B), lambda hf, n: (g, hf, 0, 0))
    dtab = pl.BlockSpec((2, B, 2 * B), lambda hf, n: (hf, 0, 0))
    sd = jax.ShapeDtypeStruct
    vm = pltpu.VMEM
    return pl.pallas_call(
        body, name=name, grid=(2, ntiles + 1),
        in_specs=[cur(0), cur(1), prev(1), cur(2), prev(2), nat, nat, nat, tab],
        out_specs=[nat, nat_prev, nat_prev, dtab],
        out_shape=[sd((T, 2 * LANES), F32), sd((T, 2 * LANES), BF16), sd((T, 2 * LANES), BF16),
                   sd((HEADS_PER_GROUP, B, 2 * B), F32)],
        scratch_shapes=[vm((rows, LANES), F32), vm((rows, LANES), F32), vm((rows, LANES), F32), vm((rows, LANES), F32),
                        vm((2 * ATTN_UNITS, B, 2 * B), F32), vm((2 * ATTN_UNITS, B, 2 * B), F32),
                        vm((2 * ATTN_UNITS, B, 2 * B), BF16), vm((2 * ATTN_UNITS, B, 2 * B), BF16)],
        compiler_params=_params("parallel", "arbitrary"),
    )(qkv, qkv, qkv, qkv, qkv, do, dvec, lse, biasm)


def _attn_fwd(q, k, v, biasm, name):
    ng, T, gw = q.shape
    hd = ATTN_HEAD_DIM
    nh = gw // hd
    nblk = T // ATTN_BLOCK
    nbs = _attn_blocks_per_residue(T)
    scale = hd ** -0.5
    B = ATTN_BLOCK

    def body(q_ref, kc_ref, kp_ref, vc_ref, vp_ref, bias_ref, o_ref, lse_ref, s_ref, p_ref):
        g = pl.program_id(0)
        b = pl.program_id(1)
        nb = jnp.where(g == 0, nbs[0], jnp.where(g == 1, nbs[1], nbs[2]))
        no_prev = (b % nb) == 0
        col = lax.broadcasted_iota(jnp.int32, (1, 2 * B), 1)
        pen = jnp.where((col < B) & no_prev, NEG_INF, 0.0)
        heads = [slice(h * hd, (h + 1) * hd) for h in range(nh)]
        for h, hs in enumerate(heads):
            kw = jnp.concatenate([kp_ref[0, :, hs], kc_ref[0, :, hs]], axis=0)
            s_ref[h] = _dot(q_ref[0, :, hs], kw, NT)
        for h, hs in enumerate(heads):
            s = s_ref[h] * scale + bias_ref[0, h] + pen
            m = jnp.max(s, axis=-1, keepdims=True)
            p = jnp.exp(s - m)
            l = jnp.sum(p, axis=-1, keepdims=True)
            p_ref[h] = (p / l).astype(BF16)
            lse_ref[0, :, hs] = jnp.broadcast_to(m + jnp.log(l), (B, hd))
        for h, hs in enumerate(heads):
            vw = jnp.concatenate([vp_ref[0, :, hs], vc_ref[0, :, hs]], axis=0)
            o_ref[0, :, hs] = _dot(p_ref[h], vw, NN)

    cur = pl.BlockSpec((1, B, gw), lambda g, b: (g, b, 0))
    prev = pl.BlockSpec((1, B, gw), lambda g, b: (g, jnp.maximum(b - 1, 0), 0))
    return pl.pallas_call(
        body, name=name, grid=(ng, nblk),
        in_specs=[cur, cur, prev, cur, prev, pl.BlockSpec((1, nh, B, 2 * B), lambda g, b: (g, 0, 0, 0))],
        out_specs=[cur, cur], out_shape=[jax.ShapeDtypeStruct(q.shape, F32)] * 2,
        scratch_shapes=[pltpu.VMEM((nh, B, 2 * B), F32), pltpu.VMEM((nh, B, 2 * B), BF16)],
        compiler_params=_params("parallel", "parallel"),
    )(q, k, k, v, v, biasm)


def _attn_bwd(q, k, v, do, dvec, lse, biasm, name):
    ng, T, gw = q.shape
    hd = ATTN_HEAD_DIM
    nh = gw // hd
    nblk = T // ATTN_BLOCK
    nbs = _attn_blocks_per_residue(T)
    scale = hd ** -0.5
    B = ATTN_BLOCK

    def body(q_ref, kc_ref, kp_ref, vc_ref, vp_ref, do_ref, dv_ref, lse_ref, bias_ref,
             dq_ref, dk_ref, dvo_ref, dbias_ref, ck_ref, cv_ref, s_ref, dp_ref, p_ref, ds_ref):
        g = pl.program_id(0)
        b = pl.program_id(1)
        nb = jnp.where(g == 0, nbs[0], jnp.where(g == 1, nbs[1], nbs[2]))
        no_prev = (b % nb) == 0

        @pl.when(b == 0)
        def _():
            dbias_ref[...] = jnp.zeros_like(dbias_ref)
            ck_ref[...] = jnp.zeros_like(ck_ref)
            cv_ref[...] = jnp.zeros_like(cv_ref)

        @pl.when(b < nblk)
        def _():
            col = lax.broadcasted_iota(jnp.int32, (1, 2 * B), 1)
            pen = jnp.where((col < B) & no_prev, NEG_INF, 0.0)
            heads = [slice(h * hd, (h + 1) * hd) for h in range(nh)]
            for h, hs in enumerate(heads):
                kw = jnp.concatenate([kp_ref[0, :, hs], kc_ref[0, :, hs]], axis=0)
                vw = jnp.concatenate([vp_ref[0, :, hs], vc_ref[0, :, hs]], axis=0)
                s_ref[h] = _dot(q_ref[0, :, hs], kw, NT)
                dp_ref[h] = _dot(do_ref[0, :, hs], vw, NT)
            for h, hs in enumerate(heads):
                lse_col = lse_ref[0, :, h * hd:h * hd + 1]
                d_col = dv_ref[0, :, h * hd:h * hd + 1]
                p = jnp.exp(s_ref[h] * scale + bias_ref[0, h] + pen - lse_col)
                ds = p * (dp_ref[h] + d_col)
                dbias_ref[0, h] += ds
                p_ref[h] = p.astype(BF16)
                ds_ref[h] = ds.astype(BF16)
            for h, hs in enumerate(heads):
                qh = q_ref[0, :, hs]
                kw = jnp.concatenate([kp_ref[0, :, hs], kc_ref[0, :, hs]], axis=0)
                dq_ref[0, :, hs] = (_dot(ds_ref[h], kw, NN) * scale).astype(BF16)
                dkw = _dot(ds_ref[h], qh, TN) * scale
                dvw = _dot(p_ref[h], do_ref[0, :, hs], TN)
                dk_ref[0, :, hs] = (ck_ref[:, hs] + dkw[:B]).astype(BF16)
                dvo_ref[0, :, hs] = (cv_ref[:, hs] + dvw[:B]).astype(BF16)
                ck_ref[:, hs] = dkw[B:]
                cv_ref[:, hs] = dvw[B:]

        @pl.when(b == nblk)
        def _():
            dk_ref[0] = ck_ref[...].astype(BF16)
            dvo_ref[0] = cv_ref[...].astype(BF16)

    last = nblk - 1
    cur = pl.BlockSpec((1, B, gw), lambda g, b: (g, jnp.minimum(b, last), 0))
    prev = pl.BlockSpec((1, B, gw), lambda g, b: (g, jnp.clip(b - 1, 0, last), 0))
    tab = pl.BlockSpec((1, nh, B, 2 * B), lambda g, b: (g, 0, 0, 0))
    sd = jax.ShapeDtypeStruct
    return pl.pallas_call(
        body, name=name, grid=(ng, nblk + 1),
        in_specs=[cur, cur, prev, cur, prev, cur, cur, cur, tab],
        out_specs=[cur, prev, prev, tab],
        out_shape=[sd(q.shape, BF16), sd(q.shape, BF16), sd(q.shape, BF16), sd(biasm.shape, F32)],
        scratch_shapes=[pltpu.VMEM((B, gw), F32), pltpu.VMEM((B, gw), F32), pltpu.VMEM((nh, B, 2 * B), F32),
                        pltpu.VMEM((nh, B, 2 * B), F32), pltpu.VMEM((nh, B, 2 * B), BF16), pltpu.VMEM((nh, B, 2 * B), BF16)],
        compiler_params=_params("parallel", "arbitrary"),
    )(q, k, k, v, v, do, dvec, lse, biasm)


def _attn_mix(os, lses, z, name):
    T, gw = os[0].shape
    C = z.shape[1]
    tm = _pick(T, 512, SUBLANES)

    def body(o0_ref, o1_ref, o2_ref, l0_ref, l1_ref, l2_ref, z_ref, out_ref):
        ls = [l0_ref[...], l1_ref[...], l2_ref[...]]
        mx = jnp.maximum(jnp.maximum(ls[0], ls[1]), ls[2])
        es = [jnp.exp(l - mx) for l in ls]
        den = es[0] + es[1] + es[2]
        for i, o_ref in enumerate((o0_ref, o1_ref, o2_ref)):
            sz, _ = _silu_and_grad(z_ref[:, i * gw:(i + 1) * gw].astype(F32))
            out_ref[:, i * gw:(i + 1) * gw] = (o_ref[...] * (es[i] / den) * sz).astype(BF16)

    row = pl.BlockSpec((tm, C), lambda i: (i, 0))
    grp = pl.BlockSpec((tm, gw), lambda i: (i, 0))
    return pl.pallas_call(
        body, name=name, grid=(T // tm,), in_specs=[grp] * 6 + [row], out_specs=row,
        out_shape=jax.ShapeDtypeStruct((T, C), BF16), compiler_params=_params("parallel"),
    )(*os, *lses, z)


def _attn_mix_bwd(dout, os, lses, z, name):
    T, gw = os[0].shape
    C = z.shape[1]
    tm = _pick(T, 512, SUBLANES)
    head_of = np.arange(gw) // ATTN_HEAD_DIM
    ones = jnp.asarray(head_of[:, None] == head_of[None, :], BF16)

    def body(dout_ref, o0_ref, o1_ref, o2_ref, l0_ref, l1_ref, l2_ref, z_ref, ones_ref,
             dz_ref, do0_ref, do1_ref, do2_ref, dv0_ref, dv1_ref, dv2_ref):
        ls = [l0_ref[...], l1_ref[...], l2_ref[...]]
        mx = jnp.maximum(jnp.maximum(ls[0], ls[1]), ls[2])
        es = [jnp.exp(l - mx) for l in ls]
        den = es[0] + es[1] + es[2]
        alphas, ebar = [], 0.0
        for i, (o_ref, do_ref) in enumerate(((o0_ref, do0_ref), (o1_ref, do1_ref), (o2_ref, do2_ref))):
            sl = slice(i * gw, (i + 1) * gw)
            alpha = es[i] / den
            ov = o_ref[...]
            dv = dout_ref[:, sl]
            sz, dsz = _silu_and_grad(z_ref[:, sl].astype(F32))
            dz_ref[:, sl] = (dv * ov * alpha * dsz).astype(BF16)
            da = dv * sz
            do_ref[...] = da * alpha
            t = da * ov
            t1 = t.astype(BF16)
            r1 = t - t1.astype(F32)
            t2 = r1.astype(BF16)
            t3 = (r1 - t2.astype(F32)).astype(BF16)
            e = _dot(t1, ones_ref[...], NN) + _dot(t2, ones_ref[...], NN) + _dot(t3, ones_ref[...], NN)
            ebar = ebar + alpha * e
            alphas.append(alpha)
        for alpha, dv_ref in zip(alphas, (dv0_ref, dv1_ref, dv2_ref)):
            dv_ref[...] = -alpha * ebar

    row = pl.BlockSpec((tm, C), lambda i: (i, 0))
    grp = pl.BlockSpec((tm, gw), lambda i: (i, 0))
    sd = jax.ShapeDtypeStruct
    res = pl.pallas_call(
        body, name=name, grid=(T // tm,),
        in_specs=[row] + [grp] * 6 + [row, pl.BlockSpec((gw, gw), lambda i: (0, 0))], out_specs=[row] + [grp] * 6,
        out_shape=[sd((T, C), BF16)] + [sd((T, gw), F32)] * 6, compiler_params=_params("parallel"),
    )(dout, *os, *lses, z, ones)
    return res[0], res[1:4], res[4:7]


def _mem_attn(qz, kv, name):
    T = qz.shape[0]
    dm = qz.shape[1] // 2
    M = kv.shape[0]
    hd = dm // MEM_HEADS
    scale = hd ** -0.5
    tm = _pick(T, 512, SUBLANES)

    def body(q_ref, z_ref, k_ref, v_ref, o_ref, s_ref, p_ref):
        heads = [slice(h * hd, (h + 1) * hd) for h in range(MEM_HEADS)]
        for h, sl in enumerate(heads):
            s_ref[h] = _dot(q_ref[:, sl].astype(BF16), k_ref[:, sl], NT)
        for h, sl in enumerate(heads):
            s = s_ref[h] * scale
            p = jnp.exp(s - jnp.max(s, axis=-1, keepdims=True))
            p_ref[h] = (p / jnp.sum(p, axis=-1, keepdims=True)).astype(BF16)
        for h, sl in enumerate(heads):
            sz, _ = _silu_and_grad(z_ref[:, sl].astype(F32))
            o_ref[:, sl] = (_dot(p_ref[h], v_ref[:, sl], NN) * sz).astype(BF16)

    return pl.pallas_call(
        body, name=name, grid=(T // tm,),
        in_specs=[pl.BlockSpec((tm, dm), lambda i: (i, 0)), pl.BlockSpec((tm, dm), lambda i: (i, 1)),
                  pl.BlockSpec((M, dm), lambda i: (0, 0)), pl.BlockSpec((M, dm), lambda i: (0, 1))],
        out_specs=pl.BlockSpec((tm, dm), lambda i: (i, 0)),
        out_shape=jax.ShapeDtypeStruct((T, dm), BF16),
        scratch_shapes=[pltpu.VMEM((MEM_HEADS, tm, M), F32), pltpu.VMEM((MEM_HEADS, tm, M), BF16)],
        compiler_params=_params("parallel"),
    )(qz, qz, kv, kv)


def _mem_attn_bwd(do, qz, kv, name):
    T = qz.shape[0]
    dm = qz.shape[1] // 2
    M = kv.shape[0]
    hd = dm // MEM_HEADS
    scale = hd ** -0.5
    tm = _pick(T, 512, SUBLANES)

    def body(do_ref, q_ref, z_ref, k_ref, v_ref, dq_ref, dz_ref, dk_ref, dv_ref, s_ref, dp_ref, p_ref, ds_ref, dob_ref):
        @pl.when(pl.program_id(0) == 0)
        def _():
            dk_ref[...] = jnp.zeros_like(dk_ref)
            dv_ref[...] = jnp.zeros_like(dv_ref)

        heads = [slice(h * hd, (h + 1) * hd) for h in range(MEM_HEADS)]
        for h, sl in enumerate(heads):
            sz, _ = _silu_and_grad(z_ref[:, sl].astype(F32))
            dob = (do_ref[:, sl] * sz).astype(BF16)
            dob_ref[:, sl] = dob
            s_ref[h] = _dot(q_ref[:, sl].astype(BF16), k_ref[:, sl], NT)
            dp_ref[h] = _dot(dob, v_ref[:, sl], NT)
        for h, sl in enumerate(heads):
            s = s_ref[h] * scale
            p = jnp.exp(s - jnp.max(s, axis=-1, keepdims=True))
            pn = p / jnp.sum(p, axis=-1, keepdims=True)
            dp = dp_ref[h]
            p_ref[h] = pn.astype(BF16)
            ds_ref[h] = (pn * (dp - jnp.sum(dp * pn, axis=-1, keepdims=True))).astype(BF16)
        for h, sl in enumerate(heads):
            _, dsz = _silu_and_grad(z_ref[:, sl].astype(F32))
            dz_ref[:, sl] = (do_ref[:, sl] * _dot(p_ref[h], v_ref[:, sl], NN) * dsz).astype(BF16)
            dq_ref[:, sl] = (_dot(ds_ref[h], k_ref[:, sl], NN) * scale).astype(BF16)
            dk_ref[:, sl] += _dot(ds_ref[h], q_ref[:, sl].astype(BF16), TN) * scale
            dv_ref[:, sl] += _dot(p_ref[h], dob_ref[:, sl], TN)

    rowq = pl.BlockSpec((tm, dm), lambda i: (i, 0))
    rowz = pl.BlockSpec((tm, dm), lambda i: (i, 1))
    kb = pl.BlockSpec((M, dm), lambda i: (0, 0))
    vb = pl.BlockSpec((M, dm), lambda i: (0, 1))
    sd = jax.ShapeDtypeStruct
    dq, dz, dk, dv = pl.pallas_call(
        body, name=name, grid=(T // tm,), in_specs=[rowq, rowq, rowz, kb, vb],
        out_specs=[rowq, rowq, kb, kb],
        out_shape=[sd((T, dm), BF16), sd((T, dm), BF16), sd((M, dm), F32), sd((M, dm), F32)],
        scratch_shapes=[pltpu.VMEM((MEM_HEADS, tm, M), F32), pltpu.VMEM((MEM_HEADS, tm, M), F32),
                        pltpu.VMEM((MEM_HEADS, tm, M), BF16), pltpu.VMEM((MEM_HEADS, tm, M), BF16), pltpu.VMEM((tm, dm), BF16)],
        compiler_params=_params("arbitrary"),
    )(do, qz, qz, kv, kv)
    return dq, dz, dk, dv


def _merge(os, ws, L, logits, b_gate, name):
    T = os[0].shape[0]
    D = ws[0].shape[2]
    tm = _pick(T, 512, SUBLANES)

    def body(o0_ref, o1_ref, o2_ref, w0_ref, w1_ref, w2_ref, l_ref, b_ref, m_ref, p0_ref, p1_ref, p2_ref):
        acc = 0.0
        for i, (o_ref, w_ref, p_ref) in enumerate(((o0_ref, w0_ref, p0_ref), (o1_ref, w1_ref, p1_ref), (o2_ref, w2_ref, p2_ref))):
            sl = slice(i * D, (i + 1) * D)
            bp = _dot(o_ref[...], w_ref[...], NN)
            p_ref[...] = bp.astype(BF16)
            acc = acc + _sigmoid(l_ref[:, sl].astype(F32) + b_ref[:, sl]) * bp
        m_ref[...] = acc.astype(BF16)

    row = pl.BlockSpec((tm, D), lambda i: (i, 0))
    return pl.pallas_call(
        body, name=name, grid=(T // tm,),
        in_specs=[pl.BlockSpec((tm, o.shape[1]), lambda i: (i, 0)) for o in os]
        + [pl.BlockSpec((None,) + w.shape[1:], lambda i: (L, 0, 0)) for w in ws]
        + [pl.BlockSpec((tm, 3 * D), lambda i: (i, 0)), pl.BlockSpec((1, 3 * D), lambda i: (0, 0))],
        out_specs=[row] * 4, out_shape=[jax.ShapeDtypeStruct((T, D), BF16)] * 4, compiler_params=_params("parallel"),
    )(*os, *ws, logits, b_gate.reshape(1, 3 * D))


def _merge_bwd(dmerged, bps, logits, b_gate, name):
    T, D = bps[0].shape
    tm = _pick(T, 512, SUBLANES)

    def body(dm_ref, p0_ref, p1_ref, p2_ref, l_ref, b_ref, d0_ref, d1_ref, d2_ref, dl_ref, db_ref):
        @pl.when(pl.program_id(0) == 0)
        def _():
            db_ref[...] = jnp.zeros_like(db_ref)

        dmv = dm_ref[...]
        for i, (p_ref, d_ref) in enumerate(((p0_ref, d0_ref), (p1_ref, d1_ref), (p2_ref, d2_ref))):
            sl = slice(i * D, (i + 1) * D)
            gt = _sigmoid(l_ref[:, sl].astype(F32) + b_ref[:, sl])
            d_ref[...] = (dmv * gt).astype(BF16)
            dl = dmv * p_ref[...].astype(F32) * gt * (1.0 - gt)
            dl_ref[:, sl] = dl.astype(BF16)
            db_ref[:, sl] += jnp.sum(dl, axis=0, keepdims=True)

    row = pl.BlockSpec((tm, D), lambda i: (i, 0))
    wide = pl.BlockSpec((tm, 3 * D), lambda i: (i, 0))
    vec = pl.BlockSpec((1, 3 * D), lambda i: (0, 0))
    sd = jax.ShapeDtypeStruct
    return pl.pallas_call(
        body, name=name, grid=(T // tm,), in_specs=[row, row, row, row, wide, vec],
        out_specs=[row, row, row, wide, vec],
        out_shape=[sd((T, D), BF16)] * 3 + [sd((T, 3 * D), BF16), sd((1, 3 * D), F32)],
        compiler_params=_params("arbitrary"),
    )(dmerged, *bps, logits, b_gate.reshape(1, 3 * D))


def _to_segments(a):
    T, C = a.shape
    return a.reshape(SSM_SEGMENTS, T // SSM_SEGMENTS, C).transpose(1, 0, 2).reshape(T, C)


def _from_segments(a):
    T, C = a.shape
    return a.reshape(T // SSM_SEGMENTS, SSM_SEGMENTS, C).transpose(1, 0, 2).reshape(T, C)


def _to_residues(a):
    T = a.shape[0]
    gw = HEADS_PER_GROUP * ATTN_HEAD_DIM
    out = []
    for g, (_, r) in enumerate(ATTN_CONFIGS):
        ag = a[:, g * gw:(g + 1) * gw].reshape(T // r, r, gw)
        out.append(ag.transpose(1, 0, 2).reshape(T, gw))
    return jnp.stack(out)


def _from_residues(a):
    _, T, gw = a.shape
    out = []
    for g, (_, r) in enumerate(ATTN_CONFIGS):
        out.append(a[g].reshape(r, T // r, gw).transpose(1, 0, 2).reshape(T, gw))
    return jnp.concatenate(out, axis=1)


def _block_diag(w):
    nblk, ng, a, b = w.shape
    eye = jnp.eye(ng, dtype=w.dtype)
    return (w[:, :, :, None, :] * eye[None, :, None, :, None]).reshape(nblk, ng * a, ng * b)


def _block_diag_part(m, a, b):
    nblk = m.shape[0]
    ng = m.shape[1] // a
    m5 = m.reshape(nblk, ng, a, ng, b)
    eye = jnp.eye(ng, dtype=m.dtype)
    return jnp.sum(m5 * eye[None, :, None, :, None], axis=3)


def _ssm_matrices(p, L, tag):
    G, P = p["ssm_lambda_re"].shape[1:]
    Hg = SSM_GROUP
    gpb = SSM_BLOCK_CH // Hg
    nblk = G // gpb
    br = p["ssm_b_re"][L].transpose(2, 0, 1)
    bi = p["ssm_b_im"][L].transpose(2, 0, 1)
    disc_in = (p["ssm_lambda_re"][L], p["ssm_lambda_im"][L], p["ssm_log_dt"][L].reshape(G, 1), br, bi)
    ar, ai, bbr, bbi = _ssm_disc(*disc_in, name=f"ssm_disc_{tag}")
    amat = (ar.reshape(nblk // 2, SUBLANES, LANES), ai.reshape(nblk // 2, SUBLANES, LANES))
    bbr_g = bbr.transpose(1, 0, 2).reshape(nblk, gpb, Hg, P)
    bbi_g = bbi.transpose(1, 0, 2).reshape(nblk, gpb, Hg, P)
    bmat = jnp.concatenate([_block_diag(bbr_g), _block_diag(bbi_g)], axis=2).astype(BF16)
    cre = p["ssm_c_re"][L].reshape(nblk, gpb, Hg, P).transpose(0, 1, 3, 2)
    cim = p["ssm_c_im"][L].reshape(nblk, gpb, Hg, P).transpose(0, 1, 3, 2)
    cmat = jnp.concatenate([_block_diag(cre), -_block_diag(cim)], axis=1).astype(BF16)
    return disc_in, amat, bmat, cmat


def _layer_fwd(x, mem, p, wb, L, biasm):
    T, D = x.shape
    C = p["ssm_d"].shape[1]
    dm = wb["w_br_mem"].shape[1]
    tag = f"l{L}"
    s = {"x": x}
    h = _rmsnorm(x, p["norm_g"][L], f"norm_{tag}")
    offs = [int(o) for o in np.cumsum([0, C, C, 3 * 768, 768, 2 * dm, 3 * D])]
    names = ("uz", "z_ssm", "qkv", "z_attn", "qz_mem", "logits")
    dts = (F32, BF16, F32, BF16, BF16, BF16)
    for i, (nm, dt) in enumerate(zip(names, dts)):
        s[nm] = _matmul(h, wb["w_in"], mode="nn", name=f"in_{nm}_{tag}", out_dtype=dt, b_lead=L, b_off=offs[i],
                        n_cols=offs[i + 1] - offs[i])
    s["h"] = h

    disc_in, amat, bmat, cmat = _ssm_matrices(p, L, tag)
    dvec = p["ssm_d"][L].reshape(1, C)
    y, xr, xi = _ssm_fwd(s["uz"], bmat, cmat, *amat, dvec, f"ssm_scan_{tag}")
    o_ssm, a_glu = _ssm_post(y, s["z_ssm"], wb["w_glu"][L], p["b_glu"][L], f"ssm_post_{tag}")
    s.update(disc_in=disc_in, amat=amat, bmat=bmat, cmat=cmat, xr=xr, xi=xi, y=y, a_glu=a_glu, o_ssm=o_ssm)

    groups = [_attn_group_fwd(s["qkv"], biasm, g, f"attn_g{g}_{tag}") for g in range(len(ATTN_CONFIGS))]
    os, lses = [o for o, _ in groups], [l for _, l in groups]
    o_attn = _attn_mix(os, lses, s["z_attn"], f"attn_mix_{tag}")
    s.update(os=os, lses=lses, o_attn=o_attn)

    mn = _rmsnorm(mem, p["mem_norm_g"][L], f"mem_norm_{tag}")
    kv = _matmul(mn, wb["w_mem_kv"], mode="nn", name=f"mem_kv_{tag}", out_dtype=BF16, b_lead=L)
    o_mem = _mem_attn(s["qz_mem"], kv, f"mem_attn_{tag}")
    s.update(mn=mn, kv=kv, o_mem=o_mem)

    merged, *bps = _merge([o_ssm, o_attn, o_mem], [wb["w_br_ssm"], wb["w_br_attn"], wb["w_br_mem"]], L, s["logits"],
                          p["b_gate"][L], f"merge_{tag}")
    s.update(bps=bps, merged=merged)
    x_new = _matmul(merged, wb["w_out"], mode="nn", name=f"out_{tag}", add=x, b_lead=L)
    return x_new, s


def _layer_bwd(dx, mem, p, wb, L, s, biasm, gprev):
    T, D = dx.shape
    C = p["ssm_d"].shape[1]
    depth = p["norm_g"].shape[0]
    tag = f"l{L}"
    g = {}

    def wgrad(n, a, b, **tiles):
        g[n] = _matmul(a, b, mode="tn", name=f"d{n}_{tag}", out_dtype=BF16, stack=(L, depth, gprev.get(n)), **tiles)

    dmerged = _matmul(dx, wb["w_out"], mode="nt", name=f"d_merged_{tag}", b_lead=L)
    wgrad("w_out", s["merged"], dx)
    dbp0, dbp1, dbp2, dlogits, g["b_gate"] = _merge_bwd(dmerged, s["bps"], s["logits"], p["b_gate"][L], f"merge_bwd_{tag}")
    dos = []
    for dbp, o, n in ((dbp0, s["o_ssm"], "w_br_ssm"), (dbp1, s["o_attn"], "w_br_attn"), (dbp2, s["o_mem"], "w_br_mem")):
        dos.append(_matmul(dbp, wb[n], mode="nt", name=f"d_o_{n}_{tag}", b_lead=L))
        wgrad(n, o, dbp)

    dy, dz_ssm, ds_glu, g["b_glu"] = _ssm_post_bwd(dos[0], s["y"], s["z_ssm"], wb["w_glu"][L], p["b_glu"][L], f"ssm_post_bwd_{tag}")
    wgrad("w_glu", s["a_glu"], ds_glu)
    dvec = p["ssm_d"][L].reshape(1, C)
    du, dbm, dct, dar, dai, g["ssm_d"] = _ssm_bwd(dy, s["uz"], s["xr"], s["xi"], s["bmat"], s["cmat"], *s["amat"], dvec,
                                                  f"ssm_scan_bwd_{tag}")
    G, P = p["ssm_lambda_re"].shape[1:]
    Hg = SSM_GROUP
    half = dbm.shape[2] // 2
    dbbr = _block_diag_part(dbm[:, :, :half], Hg, P).reshape(G, Hg, P).transpose(1, 0, 2)
    dbbi = _block_diag_part(dbm[:, :, half:], Hg, P).reshape(G, Hg, P).transpose(1, 0, 2)
    g["ssm_c_re"] = _block_diag_part(dct[:, :, :half], Hg, P).reshape(G, Hg, P)
    g["ssm_c_im"] = -_block_diag_part(dct[:, :, half:], Hg, P).reshape(G, Hg, P)
    glre, glim, gdt, gbr, gbi = _ssm_disc_bwd(*s["disc_in"], dar.reshape(G, P), dai.reshape(G, P), dbbr, dbbi,
                                              name=f"ssm_disc_bwd_{tag}")
    g["ssm_lambda_re"], g["ssm_lambda_im"], g["ssm_log_dt"] = glre, glim, gdt.reshape(G)
    g["ssm_b_re"] = gbr.transpose(1, 2, 0)
    g["ssm_b_im"] = gbi.transpose(1, 2, 0)

    dz_attn, do_g, dvec_g = _attn_mix_bwd(dos[1], s["os"], s["lses"], s["z_attn"], f"attn_mix_bwd_{tag}")
    back = [_attn_group_bwd(s["qkv"], do_g[g], dvec_g[g], s["lses"][g], biasm, g, f"attn_bwd_g{g}_{tag}")
            for g in range(len(ATTN_CONFIGS))]
    dqkv = [b[i].astype(BF16) for i in range(3) for b in back]
    dbias = jnp.stack([b[3] for b in back])

    dq_mem, dz_mem, dk_mem, dv_mem = _mem_attn_bwd(dos[2], s["qz_mem"], s["kv"], f"mem_attn_bwd_{tag}")
    dkv = jnp.concatenate([dk_mem, dv_mem], axis=1)
    wgrad("w_mem_kv", s["mn"], dkv)
    dmn = _matmul(dkv, wb["w_mem_kv"], mode="nt", name=f"d_mn_{tag}", b_lead=L)
    _, g["mem_norm_g"] = _rmsnorm_bwd(mem, p["mem_norm_g"][L], dmn, None, f"mem_norm_bwd_{tag}")

    dproj = jnp.concatenate([du.astype(BF16), dz_ssm] + dqkv + [dz_attn, dq_mem, dz_mem, dlogits], axis=1)
    dh = _matmul(dproj, wb["w_in"], mode="nt", name=f"d_h_{tag}", b_lead=L)
    wgrad("w_in", s["h"], dproj, tn=2304, tk=1024)
    dx_in, g["norm_g"] = _rmsnorm_bwd(s["x"], p["norm_g"][L], dh, dx, f"norm_bwd_{tag}")
    return dx_in, g, dbias


def _bucket_onehot(gi):
    buckets, bands = _band_tables()
    hit = (buckets[gi].reshape(1, -1) == jnp.arange(NUM_BUCKETS)[:, None]) & bands[gi].reshape(1, -1)
    return hit.astype(BF16)


def _bias_tables(rel_bias, name):
    _, bands = _band_tables()
    out = []
    for gi in range(len(ATTN_CONFIGS)):
        tab = rel_bias[:, gi * HEADS_PER_GROUP:(gi + 1) * HEADS_PER_GROUP].T
        flat = _matmul(tab, _bucket_onehot(gi), mode="nn", name=f"{name}_{gi}", split_a=3, tn=4096)
        out.append(jnp.where(bands[gi][None], flat.reshape(HEADS_PER_GROUP, ATTN_BLOCK, 2 * ATTN_BLOCK), NEG_INF))
    return jnp.stack(out)


def _rel_bias_grad(dbias_sum, name):
    cols = []
    for gi in range(len(ATTN_CONFIGS)):
        flat = dbias_sum[gi].reshape(HEADS_PER_GROUP, -1)
        cols.append(_matmul(flat, _bucket_onehot(gi), mode="nt", name=f"{name}_{gi}", split_a=2, tk=4096).T)
    return jnp.concatenate(cols, axis=1)


def _local_step(x, mem, target, p, wb):
    depth = p["norm_g"].shape[0]
    biasm = _bias_tables(p["rel_bias"], "bias_table")
    saved = []
    for L in range(depth):
        x, s = _layer_fwd(x, mem, p, wb, L, biasm)
        saved.append(s)
    loss_vec, dx, dgf = _loss_head(x, p["final_norm_g"], target, "loss_head")
    grads = {"final_norm_g": dgf.reshape(-1)}
    per_layer = [None] * depth
    dbias_sum = 0.0
    stacked = {}
    for L in reversed(range(depth)):
        dx, per_layer[L], dbias = _layer_bwd(dx, mem, p, wb, L, saved[L], biasm, stacked)
        stacked = {n: per_layer[L][n] for n, _ in BIG}
        dbias_sum = dbias_sum + dbias
    grads.update(stacked)
    for n in per_layer[0]:
        if n not in stacked:
            grads[n] = jnp.stack([per_layer[L][n].reshape(p[n].shape[1:]) for L in range(depth)])
    grads["rel_bias"] = _rel_bias_grad(dbias_sum, "d_rel_bias")
    return jnp.sum(loss_vec), dx, grads


def _chip_coords(j):
    return j // 2, j % 2


def _place_shard(shard, ax, chip, name):
    _, a, b = shard.shape
    ra = _pick(a, 256, 16)
    full = (2, a * N_CHIPS, b) if ax == 1 else (2, a, b * N_CHIPS)
    per = a // ra

    def body(j_ref, s_ref, o_ref):
        o_ref[...] = s_ref[...].astype(BF16)

    out_idx = (lambda l, i, j: (l, j[0] * per + i, 0)) if ax == 1 else (lambda l, i, j: (l, i, j[0]))
    return pl.pallas_call(
        body, name=name,
        grid_spec=pltpu.PrefetchScalarGridSpec(
            num_scalar_prefetch=1, grid=(2, per),
            in_specs=[pl.BlockSpec((None, ra, b), lambda l, i, j: (l, i, 0))],
            out_specs=pl.BlockSpec((None, ra, b), out_idx)),
        out_shape=jax.ShapeDtypeStruct(full, BF16), compiler_params=_params("parallel", "parallel"),
    )(chip, shard)


def _gather_shards(fulls, axes, name):
    n = len(fulls)
    widths = [a.shape[ax] // N_CHIPS for a, ax in zip(fulls, axes)]
    aligns = [LANES if ax == 2 else 16 for ax in axes]

    def body(*refs):
        outs = refs[n:2 * n]
        send_sems, recv_sems, fsend_sems, frecv_sems = refs[2 * n:]
        x, y, c = lax.axis_index("x"), lax.axis_index("y"), lax.axis_index("c")
        mine = 2 * x + y
        sibling = (x, y, 1 - c)

        def window(t, layer, j):
            start = pl.ds(pl.multiple_of(j * widths[t], aligns[t]), widths[t])
            return outs[t].at[(layer, start, slice(None)) if axes[t] == 1 else (layer, slice(None), start)]

        def over_ici(t, j, block):
            return pltpu.make_async_remote_copy(
                src_ref=window(t, c, mine), dst_ref=window(t, c, block), send_sem=send_sems.at[t, j],
                recv_sem=recv_sems.at[t, block], device_id=(*_chip_coords(j), c), device_id_type=MESH)

        def over_d2d(t, j, layer):
            return pltpu.make_async_remote_copy(
                src_ref=window(t, layer, j), dst_ref=window(t, layer, j), send_sem=fsend_sems.at[t, j],
                recv_sem=frecv_sems.at[t, j], device_id=sibling, device_id_type=MESH)

        for t in range(n):
            for j in range(N_CHIPS):
                @pl.when(j != mine)
                def _():
                    over_ici(t, j, mine).start()
        for t in range(n):
            for j in range(N_CHIPS):
                @pl.when(j != mine)
                def _():
                    over_ici(t, j, j).wait_recv()
                    over_d2d(t, j, c).start()
        for t in range(n):
            for j in range(N_CHIPS):
                @pl.when(j != mine)
                def _():
                    over_ici(t, j, mine).wait_send()
                    over_d2d(t, j, c).wait_send()
                    over_d2d(t, j, 1 - c).wait_recv()

    sem = pltpu.SemaphoreType.DMA
    return pl.pallas_call(
        body, name=name, in_specs=[HBM] * n, out_specs=[HBM] * n,
        out_shape=[jax.ShapeDtypeStruct(a.shape, a.dtype) for a in fulls],
        input_output_aliases={t: t for t in range(n)},
        scratch_shapes=[sem((n, N_CHIPS)), sem((n, N_CHIPS)), sem((n, N_CHIPS)), sem((n, N_CHIPS))],
    )(*fulls)


def _scatter_slices(arrays, axes, name):
    n = len(arrays)

    def piece(a, ax):
        if ax is None:
            return a.shape, None
        w = a.shape[ax] // N_CHIPS
        return a.shape[:ax] + (w,) + a.shape[ax + 1:], w

    shapes = [piece(a, ax) for a, ax in zip(arrays, axes)]

    def body(*refs):
        ins, outs = refs[:n], refs[n:2 * n]
        send_sems, recv_sems, loc_sems = refs[2 * n:]
        x, y, c = lax.axis_index("x"), lax.axis_index("y"), lax.axis_index("c")
        mine = 2 * x + y

        def src(t, j):
            ax, w = axes[t], shapes[t][1]
            if ax is None:
                return ins[t]
            idx = tuple(pl.ds(j * w, w) if d == ax else slice(None) for d in range(len(arrays[t].shape)))
            return ins[t].at[idx]

        for t in range(n):
            for j in range(N_CHIPS):
                @pl.when(j == mine)
                def _():
                    pltpu.make_async_copy(src(t, j), outs[t].at[j], loc_sems.at[t]).start()

                @pl.when(j != mine)
                def _():
                    pltpu.make_async_remote_copy(
                        src_ref=src(t, j), dst_ref=outs[t].at[mine], send_sem=send_sems.at[t, j], recv_sem=recv_sems.at[t, mine],
                        device_id=(*_chip_coords(j), c), device_id_type=MESH).start()
        for t in range(n):
            for j in range(N_CHIPS):
                @pl.when(j == mine)
                def _():
                    pltpu.make_async_copy(src(t, j), outs[t].at[j], loc_sems.at[t]).wait()

                @pl.when(j != mine)
                def _():
                    cp = pltpu.make_async_remote_copy(
                        src_ref=src(t, j), dst_ref=outs[t].at[j], send_sem=send_sems.at[t, j], recv_sem=recv_sems.at[t, j],
                        device_id=(*_chip_coords(j), c), device_id_type=MESH)
                    cp.wait_send()
                    cp.wait_recv()

    return pl.pallas_call(
        body, name=name, in_specs=[HBM] * n, out_specs=[HBM] * n,
        out_shape=[jax.ShapeDtypeStruct((N_CHIPS,) + sh, a.dtype) for a, (sh, _) in zip(arrays, shapes)],
        scratch_shapes=[pltpu.SemaphoreType.DMA((n, N_CHIPS)), pltpu.SemaphoreType.DMA((n, N_CHIPS)), pltpu.SemaphoreType.DMA((n,))],
    )(*arrays)


def _swap_layers(stacked, name):
    n = len(stacked)

    def body(*refs):
        ins, outs = refs[:n], refs[n:2 * n]
        send_sems, recv_sems = refs[2 * n:]
        c = lax.axis_index("c")
        peer = (lax.axis_index("x"), lax.axis_index("y"), 1 - c)
        cps = [pltpu.make_async_remote_copy(src_ref=ins[t].at[1 - c], dst_ref=outs[t], send_sem=send_sems.at[t],
                                            recv_sem=recv_sems.at[t], device_id=peer, device_id_type=MESH) for t in range(n)]
        for cp in cps:
            cp.start()
        for cp in cps:
            cp.wait_send()
            cp.wait_recv()

    return pl.pallas_call(
        body, name=name, in_specs=[HBM] * n, out_specs=[HBM] * n,
        out_shape=[jax.ShapeDtypeStruct(a.shape[1:], a.dtype) for a in stacked],
        scratch_shapes=[pltpu.SemaphoreType.DMA((n,)), pltpu.SemaphoreType.DMA((n,))],
    )(*stacked)


def _merge_layers(stacked, name):
    n = len(stacked)

    def body(*refs):
        outs = refs[n:2 * n]
        send_sems, recv_sems = refs[2 * n:]
        c = lax.axis_index("c")
        peer = (lax.axis_index("x"), lax.axis_index("y"), 1 - c)
        for t in range(n):
            pltpu.make_async_remote_copy(src_ref=outs[t].at[c], dst_ref=outs[t].at[c], send_sem=send_sems.at[t],
                                         recv_sem=recv_sems.at[t], device_id=peer, device_id_type=MESH).start()
        for t in range(n):
            cp = pltpu.make_async_remote_copy(src_ref=outs[t].at[c], dst_ref=outs[t].at[1 - c], send_sem=send_sems.at[t],
                                              recv_sem=recv_sems.at[t], device_id=peer, device_id_type=MESH)
            cp.wait_send()
            cp.wait_recv()

    sem = pltpu.SemaphoreType.DMA
    return pl.pallas_call(
        body, name=name, in_specs=[HBM] * n, out_specs=[HBM] * n,
        out_shape=[jax.ShapeDtypeStruct(a.shape, a.dtype) for a in stacked],
        input_output_aliases={t: t for t in range(n)}, scratch_shapes=[sem((n,)), sem((n,))],
    )(*stacked)


def _pair_sum(stacked, landed, core, name):
    _, K, N = stacked.shape
    tr = _pick(K, max(16, (1 << 19) // N // 16 * 16), 16)

    def body(c_ref, s_ref, l_ref, o_ref):
        o_ref[...] = (s_ref[...].astype(F32) + l_ref[...].astype(F32)).astype(o_ref.dtype)

    return pl.pallas_call(
        body, name=name,
        grid_spec=pltpu.PrefetchScalarGridSpec(
            num_scalar_prefetch=1, grid=(K // tr,),
            in_specs=[pl.BlockSpec((None, tr, N), lambda i, c: (c[0], i, 0)), pl.BlockSpec((tr, N), lambda i, c: (i, 0))],
            out_specs=pl.BlockSpec((tr, N), lambda i, c: (i, 0))),
        out_shape=jax.ShapeDtypeStruct((K, N), stacked.dtype), compiler_params=_params("parallel"),
    )(core, stacked, landed)


def _sum_chips(landed, core, name):
    _, R, C = landed.shape
    tr = _pick(R, max(SUBLANES, (1 << 19) // C // 16 * 16), 16)

    def body(c_ref, l_ref, o_ref):
        acc = l_ref[0].astype(F32) + l_ref[1].astype(F32)
        acc = acc + l_ref[2].astype(F32)
        o_ref[...] = acc + l_ref[3].astype(F32)

    return pl.pallas_call(
        body, name=name,
        grid_spec=pltpu.PrefetchScalarGridSpec(
            num_scalar_prefetch=1, grid=(R // tr,),
            in_specs=[pl.BlockSpec((N_CHIPS, tr, C), lambda i, c: (0, i, 0))],
            out_specs=pl.BlockSpec((None, tr, C), lambda i, c: (c[0], i, 0))),
        out_shape=jax.ShapeDtypeStruct((2, R, C), F32), compiler_params=_params("parallel"),
    )(core, landed)


def _adamw_math(w_ref, g_ref, m_ref, v_ref, d_ref, nm_ref, nv_ref):
    c1 = 1.0 / (1.0 - ADAM_B1 ** ADAM_STEP)
    c2 = 1.0 / (1.0 - ADAM_B2 ** ADAM_STEP)
    g = g_ref[...]
    nm = ADAM_B1 * m_ref[...] + (1.0 - ADAM_B1) * g
    nv = ADAM_B2 * v_ref[...] + (1.0 - ADAM_B2) * (g * g)
    nm_ref[...] = nm
    nv_ref[...] = nv
    d_ref[...] = -ADAM_LR * ((nm * c1) / (jnp.sqrt(nv * c2) + ADAM_EPS) + ADAM_WD * w_ref[...])


def _adamw_whole(w, g, m, v, name):
    shape = w.shape
    view = (-1,) + shape[-2:] if w.ndim >= 2 else (1, 1, -1)

    def body(*refs):
        _adamw_math(*refs)

    res = pl.pallas_call(body, name=name, out_shape=[jax.ShapeDtypeStruct(w.reshape(view).shape, F32)] * 3,
                         compiler_params=pltpu.CompilerParams(vmem_limit_bytes=VMEM_LIMIT_BYTES))(
        *(a.reshape(view) for a in (w, g, m, v)))
    return [r.reshape(shape) for r in res]


def _adamw(w, g, m, v, name):
    R, C = w.shape
    tr = _pick(R, max(SUBLANES, (1 << 18) // C // 8 * 8), SUBLANES)

    def body(*refs):
        _adamw_math(*refs)

    blk = pl.BlockSpec((tr, C), lambda i: (i, 0))
    return pl.pallas_call(
        body, name=name, grid=(R // tr,), in_specs=[blk] * 4, out_specs=[blk] * 3,
        out_shape=[jax.ShapeDtypeStruct((R, C), F32)] * 3, compiler_params=_params("parallel"),
    )(w, g, m, v)


def _pack_small(d, prefix=""):
    flat = jnp.concatenate([d[prefix + n].astype(F32).reshape(-1) for n in SMALL])
    pad = (-flat.shape[0]) % (2 * 16 * LANES)
    return jnp.pad(flat, (0, pad)).reshape(-1, LANES)


def _unpack_small(packed, shapes):
    flat = packed.reshape(-1)
    out, off = {}, 0
    for n in SMALL:
        size = int(np.prod(shapes[n]))
        out[n] = flat[off:off + size].reshape(shapes[n])
        off += size
    return out


def kernel(*args):
    p = dict(zip(INPUTS, args))
    x, mem, target = p["x"][0], p["mem"][0], p["loss_target"][0]

    names = [n for n, _ in BIG] + ["small"]
    core = lax.axis_index("c").astype(jnp.int32).reshape(1)
    chip = (2 * lax.axis_index("x") + lax.axis_index("y")).astype(jnp.int32).reshape(1)
    placed = [_place_shard(p[n], ax, chip, f"place_{n}") for n, ax in BIG]
    wb = dict(zip(names, _gather_shards(placed, [ax for _, ax in BIG], "gather_weights")))

    loss_part, dx, grads = _local_step(x, mem, target, p, wb)
    loss = lax.psum(loss_part, ("x", "y", "c"))

    stacked = [grads[n] for n, _ in BIG] + [_pack_small(grads).reshape(2, -1, LANES)]
    theirs = _swap_layers(stacked, "swap_layers")
    pair = [_pair_sum(s, o, core, f"pair_sum_{n}") for n, s, o in zip(names, stacked, theirs)]
    landed = _scatter_slices(pair, [ax - 1 for _, ax in BIG] + [None], "scatter_grads")
    reduced = [_sum_chips(ld.reshape(N_CHIPS, -1, ld.shape[-1]), core, f"sum_chips_{n}") for n, ld in zip(names, landed)]
    total = _merge_layers(reduced, "merge_layers")

    out = {}
    for (n, _), g in zip(BIG, total):
        sh = p[n].shape
        two_d = lambda a: a.reshape(-1, sh[-1])
        res = (g,) + tuple(_adamw(two_d(p[n]), two_d(g), two_d(p["m_" + n]), two_d(p["v_" + n]), f"adamw_{n}"))
        for key, r in zip(("grad_", "delta_", "new_m_", "new_v_"), res):
            out[key + n] = r.reshape(sh)
    for n, g in _unpack_small(total[-1], {n: p[n].shape for n in SMALL}).items():
        res = (g,) + tuple(_adamw_whole(p[n], g, p["m_" + n], p["v_" + n], f"adamw_{n}"))
        for key, r in zip(("grad_", "delta_", "new_m_", "new_v_"), res):
            out[key + n] = r

    result = [loss, dx.reshape(p["x"].shape)]
    for key in ("grad_", "delta_", "new_m_", "new_v_"):
        result += [out[key + n] for n in WEIGHTS]
    return tuple(result)
```

```python
import math

import jax
import jax.numpy as jnp
import numpy as np
from jax import lax
from jax.experimental import pallas as pl
from jax.experimental.pallas import tpu as pltpu

F32 = jnp.float32
BF16 = jnp.bfloat16
MESH = pl.DeviceIdType.MESH
HBM = pl.BlockSpec(memory_space=pltpu.HBM)

EPS = 1e-6
SSM_GROUP = 16
SSM_STATE = 64
ATTN_HEAD_DIM = 64
HEADS_PER_GROUP = 4
ATTN_CONFIGS = ((128, 1), (512, 4), (2048, 16))
ATTN_BLOCK = 128
NUM_BUCKETS = 32
REL_MAX_DISTANCE = 2048
NEG_INF = -1e30
MEM_HEADS = 4
ADAM_LR = 0.001
ADAM_B1 = 0.9
ADAM_B2 = 0.999
ADAM_EPS = 1e-08
ADAM_WD = 0.01
ADAM_STEP = 10

LANES = 128
SUBLANES = 8
VMEM_LIMIT_BYTES = 48 * 1024 * 1024
SSM_BLOCK_CH = 128
SSM_SEGMENTS = SUBLANES
SSM_CHUNK_STEPS = 128

N_CHIPS = 4
BIG = (("w_in", 2), ("w_glu", 1), ("w_mem_kv", 1), ("w_br_ssm", 2), ("w_br_attn", 2), ("w_br_mem", 2), ("w_out", 1))
SMALL = ("norm_g", "mem_norm_g", "b_gate", "ssm_lambda_re", "ssm_lambda_im", "ssm_log_dt", "ssm_b_re", "ssm_b_im",
         "ssm_c_re", "ssm_c_im", "ssm_d", "b_glu", "rel_bias", "final_norm_g")
WEIGHTS = ("norm_g", "mem_norm_g", "w_in", "b_gate", "ssm_lambda_re", "ssm_lambda_im", "ssm_log_dt", "ssm_b_re",
           "ssm_b_im", "ssm_c_re", "ssm_c_im", "ssm_d", "w_glu", "b_glu", "w_mem_kv", "w_br_ssm", "w_br_attn",
           "w_br_mem", "w_out", "rel_bias", "final_norm_g")
INPUTS = ("x", "mem") + WEIGHTS + ("loss_target",) + tuple("m_" + n for n in WEIGHTS) + tuple("v_" + n for n in WEIGHTS)


def _params(*sem):
    return pltpu.CompilerParams(dimension_semantics=sem, vmem_limit_bytes=VMEM_LIMIT_BYTES)


def _pick(dim, pref, align):
    if dim <= pref:
        return dim
    t = pref - pref % align
    while t >= align:
        if dim % t == 0:
            return t
        t -= align
    return dim


def _sigmoid(v):
    return 1.0 / (1.0 + jnp.exp(-v))


def _silu_and_grad(z):
    s = _sigmoid(z)
    return z * s, s * (1.0 + z * (1.0 - s))


_GELU_C = math.sqrt(2.0 / math.pi)


def _gelu_and_grad(y):
    inner = _GELU_C * (y + 0.044715 * y * y * y)
    t = jnp.tanh(inner)
    g = 0.5 * y * (1.0 + t)
    dg = 0.5 * (1.0 + t) + 0.5 * y * (1.0 - t * t) * _GELU_C * (1.0 + 3.0 * 0.044715 * y * y)
    return g, dg


def _dot(a, b, dims):
    return lax.dot_general(a, b, (dims, ((), ())), preferred_element_type=F32)


NN = ((1,), (0,))
NT = ((1,), (1,))
TN = ((0,), (0,))


def _matmul(a, b, *, mode, name, out_dtype=F32, add=None, split_a=1, tm=1024, tn=768, tk=2304,
            b_lead=None, b_off=0, n_cols=None, stack=None):
    if mode == "tn":
        K, M = a.shape
    else:
        M, K = a.shape
    bshape = b.shape if b_lead is None else b.shape[1:]
    N = n_cols or (bshape[0] if mode == "nt" else bshape[1])
    if mode != "tn" and M >= 4 * tm:
        tm = 2 * tm
    tm = _pick(M, tm, LANES if mode == "tn" else SUBLANES)
    tn = _pick(math.gcd(N, b_off) if b_off else N, tn, LANES)
    tk = _pick(K, tk, LANES)
    nk = K // tk
    joff = b_off // tn
    dims = {"nn": NN, "nt": NT, "tn": TN}[mode]
    has_add = add is not None
    has_prev = stack is not None and stack[2] is not None

    def body(*refs):
        a_ref, b_ref = refs[:2]
        add_ref = refs[2] if has_add else None
        o_ref = refs[-2] if nk > 1 else refs[-1]
        k = pl.program_id(2)
        bv = b_ref[...].astype(BF16)
        if split_a > 1:
            rest = a_ref[...].astype(F32)
            part = 0.0
            for _ in range(split_a):
                piece = rest.astype(BF16)
                part = part + _dot(piece, bv, dims)
                rest = rest - piece.astype(F32)
        else:
            part = _dot(a_ref[...].astype(BF16), bv, dims)

        def finish(r):
            if has_add:
                r = r + add_ref[...]
            o_ref[...] = r.astype(out_dtype)

        if nk == 1:
            finish(part)
            return
        acc_ref = refs[-1]

        @pl.when(k == 0)
        def _():
            acc_ref[...] = part

        @pl.when((k > 0) & (k < nk - 1))
        def _():
            acc_ref[...] += part

        @pl.when(k == nk - 1)
        def _():
            finish(acc_ref[...] + part)

    a_spec = pl.BlockSpec((tk, tm), lambda i, j, k: (k, i)) if mode == "tn" else pl.BlockSpec((tm, tk), lambda i, j, k: (i, k))
    lead = () if b_lead is None else (b_lead,)
    lead_blk = () if b_lead is None else (None,)
    if mode == "nt":
        b_spec = pl.BlockSpec(lead_blk + (tn, tk), lambda i, j, k: lead + (j + joff, k))
    else:
        b_spec = pl.BlockSpec(lead_blk + (tk, tn), lambda i, j, k: lead + (k, j + joff))
    in_specs = [a_spec, b_spec]
    args = [a, b]
    if has_add:
        in_specs.append(pl.BlockSpec((tm, tn), lambda i, j, k: (i, j)))
        args.append(add)
    aliases = {}
    if stack is None:
        out_spec = pl.BlockSpec((tm, tn), lambda i, j, k: (i, j))
        out_shape = jax.ShapeDtypeStruct((M, N), out_dtype)
    else:
        layer, depth, prev = stack
        out_spec = pl.BlockSpec((None, tm, tn), lambda i, j, k: (layer, i, j))
        out_shape = jax.ShapeDtypeStruct((depth, M, N), out_dtype)
        if has_prev:
            in_specs.append(pl.BlockSpec(memory_space=pl.ANY))
            args.append(prev)
            aliases = {len(args) - 1: 0}
    return pl.pallas_call(
        body, name=name, grid=(M // tm, N // tn, nk), in_specs=in_specs, out_specs=out_spec, out_shape=out_shape,
        scratch_shapes=[pltpu.VMEM((tm, tn), F32)] if nk > 1 else [], input_output_aliases=aliases,
        compiler_params=_params("parallel", "parallel", "arbitrary"),
    )(*args)


def _rmsnorm(x, g, name):
    T, D = x.shape
    tm = _pick(T, 512, SUBLANES)

    def body(x_ref, g_ref, h_ref):
        xv = x_ref[...]
        r = lax.rsqrt(jnp.mean(xv * xv, axis=-1, keepdims=True) + EPS)
        h_ref[...] = (xv * r * g_ref[...]).astype(BF16)

    return pl.pallas_call(
        body, name=name, grid=(T // tm,),
        in_specs=[pl.BlockSpec((tm, D), lambda i: (i, 0)), pl.BlockSpec((1, D), lambda i: (0, 0))],
        out_specs=pl.BlockSpec((tm, D), lambda i: (i, 0)),
        out_shape=jax.ShapeDtypeStruct((T, D), BF16), compiler_params=_params("parallel"),
    )(x, g.reshape(1, D))


def _rmsnorm_bwd(x, g, dh, dres, name):
    T, D = x.shape
    tm = _pick(T, 512, SUBLANES)
    with_res = dres is not None

    def body(*refs):
        if with_res:
            x_ref, g_ref, dh_ref, dres_ref, dx_ref, dg_ref = refs
        else:
            x_ref, g_ref, dh_ref, dx_ref, dg_ref = refs
        xv = x_ref[...]
        dhv = dh_ref[...]
        r = lax.rsqrt(jnp.mean(xv * xv, axis=-1, keepdims=True) + EPS)
        dyg = dhv * g_ref[...]
        c = jnp.mean(dyg * xv, axis=-1, keepdims=True)
        dx = r * dyg - xv * (r * r * r) * c
        if with_res:
            dx = dx + dres_ref[...]
        dx_ref[...] = dx

        @pl.when(pl.program_id(0) == 0)
        def _():
            dg_ref[...] = jnp.zeros_like(dg_ref)

        dg_ref[...] += jnp.sum(dhv * xv * r, axis=0, keepdims=True)

    row = pl.BlockSpec((tm, D), lambda i: (i, 0))
    vec = pl.BlockSpec((1, D), lambda i: (0, 0))
    ins = [x, g.reshape(1, D), dh] + ([dres] if with_res else [])
    return pl.pallas_call(
        body, name=name, grid=(T // tm,), in_specs=[row, vec, row] + ([row] if with_res else []),
        out_specs=[row, vec],
        out_shape=[jax.ShapeDtypeStruct((T, D), F32), jax.ShapeDtypeStruct((1, D), F32)],
        compiler_params=_params("arbitrary"),
    )(*ins)


def _loss_head(x, g, target, name):
    T, D = x.shape
    tm = _pick(T, 512, SUBLANES)

    def body(x_ref, g_ref, t_ref, loss_ref, dx_ref, dg_ref):
        xv = x_ref[...]
        gv = g_ref[...]
        r = lax.rsqrt(jnp.mean(xv * xv, axis=-1, keepdims=True) + EPS)
        e = xv * r * gv - t_ref[...]
        dy = e * (1.0 / D)
        dyg = dy * gv
        c = jnp.mean(dyg * xv, axis=-1, keepdims=True)
        dx_ref[...] = r * dyg - xv * (r * r * r) * c

        @pl.when(pl.program_id(0) == 0)
        def _():
            loss_ref[...] = jnp.zeros_like(loss_ref)
            dg_ref[...] = jnp.zeros_like(dg_ref)

        loss_ref[...] += jnp.sum(e * e, axis=0, keepdims=True) * (0.5 / D)
        dg_ref[...] += jnp.sum(dy * xv * r, axis=0, keepdims=True)

    row = pl.BlockSpec((tm, D), lambda i: (i, 0))
    vec = pl.BlockSpec((1, D), lambda i: (0, 0))
    return pl.pallas_call(
        body, name=name, grid=(T // tm,), in_specs=[row, vec, row], out_specs=[vec, row, vec],
        out_shape=[jax.ShapeDtypeStruct((1, D), F32), jax.ShapeDtypeStruct((T, D), F32), jax.ShapeDtypeStruct((1, D), F32)],
        compiler_params=_params("arbitrary"),
    )(x, g.reshape(1, D), target)


def _ssm_disc_math(lre, lim, logdt, br, bi):
    dt = jnp.exp(logdt)
    mag = jnp.exp(lre * dt)
    ar = mag * jnp.cos(lim * dt)
    ai = mag * jnp.sin(lim * dt)
    den = lre * lre + lim * lim
    nr = ar - 1.0
    fr = (nr * lre + ai * lim) / den
    fi = (ai * lre - nr * lim) / den
    return ar, ai, fr[None] * br - fi[None] * bi, fr[None] * bi + fi[None] * br


def _ssm_disc(lre, lim, logdt, br, bi, name):
    def body(lre_ref, lim_ref, dt_ref, br_ref, bi_ref, ar_ref, ai_ref, bbr_ref, bbi_ref):
        ar, ai, bbr, bbi = _ssm_disc_math(lre_ref[...], lim_ref[...], dt_ref[...], br_ref[...], bi_ref[...])
        ar_ref[...] = ar
        ai_ref[...] = ai
        bbr_ref[...] = bbr
        bbi_ref[...] = bbi

    sd = jax.ShapeDtypeStruct
    return pl.pallas_call(
        body, name=name, out_shape=[sd(lre.shape, F32), sd(lre.shape, F32), sd(br.shape, F32), sd(br.shape, F32)],
    )(lre, lim, logdt, br, bi)


def _ssm_disc_bwd(lre, lim, logdt, br, bi, dar, dai, dbbr, dbbi, name):
    def body(lre_ref, lim_ref, dt_ref, br_ref, bi_ref, dar_ref, dai_ref, dbbr_ref, dbbi_ref,
             glre_ref, glim_ref, gdt_ref, gbr_ref, gbi_ref):
        _, vjp = jax.vjp(_ssm_disc_math, lre_ref[...], lim_ref[...], dt_ref[...], br_ref[...], bi_ref[...])
        glre, glim, gdt, gbr, gbi = vjp((dar_ref[...], dai_ref[...], dbbr_ref[...], dbbi_ref[...]))
        glre_ref[...] = glre
        glim_ref[...] = glim
        gdt_ref[...] = gdt
        gbr_ref[...] = gbr
        gbi_ref[...] = gbi

    sd = jax.ShapeDtypeStruct
    return pl.pallas_call(
        body, name=name,
        out_shape=[sd(lre.shape, F32), sd(lre.shape, F32), sd(logdt.shape, F32), sd(br.shape, F32), sd(br.shape, F32)],
    )(lre, lim, logdt, br, bi, dar, dai, dbbr, dbbi)


def _shift_segments(v, down):
    n = v.shape[0]
    rows = lax.broadcasted_iota(jnp.int32, v.shape, 0)
    if down:
        return jnp.where(rows >= 1, pltpu.roll(v, 1, 0), 0.0)
    return jnp.where(rows < n - 1, pltpu.roll(v, n - 1, 0), 0.0)


def _cpow(ar, ai, n):
    rr, ri = None, None
    pr, pi = ar, ai
    while n:
        if n & 1:
            if rr is None:
                rr, ri = pr, pi
            else:
                rr, ri = rr * pr - ri * pi, rr * pi + ri * pr
        n >>= 1
        if n:
            pr, pi = pr * pr - pi * pi, 2.0 * pr * pi
    return rr, ri


def _ssm_geometry(T, C):
    seg_steps = T // SSM_SEGMENTS
    kc = min(SSM_CHUNK_STEPS, seg_steps)
    return C // SSM_BLOCK_CH, seg_steps, kc, seg_steps // kc, SSM_SEGMENTS * kc


def _ssm_carries(u, dy, bmat, cmat, amat, *, reverse, name):
    src = dy if reverse else u
    T, C = src.shape
    nblk, seg_steps, kc, nchunk, rc = _ssm_geometry(T, C)
    half = bmat.shape[2] // 2

    def body(src_ref, w_ref, a_ref, out_ref, buf_ref, st_ref):
        c = pl.program_id(1)

        @pl.when(c == 0)
        def _():
            st_ref[...] = jnp.zeros_like(st_ref)

        if reverse:
            buf_ref[...] = _dot(src_ref[...].astype(BF16), w_ref[0], NT)
        else:
            buf_ref[...] = _dot(src_ref[...].astype(BF16), w_ref[0], NN)
        ar = a_ref[0, :, :half]
        ai = a_ref[0, :, half:]
        if reverse:
            ai = -ai

        def step(i, carry):
            xr, xi = carry
            k = (kc - 1 - i) if reverse else i
            row = pl.multiple_of(k * SUBLANES, SUBLANES)
            br = buf_ref[pl.ds(row, SUBLANES), :half]
            bi = buf_ref[pl.ds(row, SUBLANES), half:]
            return ar * xr - ai * xi + br, ar * xi + ai * xr + bi

        xr, xi = lax.fori_loop(0, kc, step, (st_ref[:, :half], st_ref[:, half:]), unroll=8)
        st_ref[:, :half] = xr
        st_ref[:, half:] = xi

        @pl.when(c == nchunk - 1)
        def _():
            pr, pi = _cpow(ar, ai, seg_steps)
            sr = jnp.zeros_like(xr)
            si = jnp.zeros_like(xi)
            for _ in range(SSM_SEGMENTS - 1):
                nr = xr + pr * sr - pi * si
                ni = xi + pr * si + pi * sr
                sr = _shift_segments(nr, not reverse)
                si = _shift_segments(ni, not reverse)
            out_ref[0, :, :half] = sr
            out_ref[0, :, half:] = si

    cidx = (lambda b, c: (nchunk - 1 - c, b)) if reverse else (lambda b, c: (c, b))
    w = cmat if reverse else bmat
    return pl.pallas_call(
        body, name=name, grid=(nblk, nchunk),
        in_specs=[pl.BlockSpec((rc, SSM_BLOCK_CH), cidx),
                  pl.BlockSpec((1,) + w.shape[1:], lambda b, c: (b, 0, 0)),
                  pl.BlockSpec((1, SUBLANES, 2 * half), lambda b, c: (b, 0, 0))],
        out_specs=pl.BlockSpec((1, SUBLANES, 2 * half), lambda b, c: (b, 0, 0)),
        out_shape=jax.ShapeDtypeStruct((nblk, SUBLANES, 2 * half), F32),
        scratch_shapes=[pltpu.VMEM((rc, 2 * half), F32), pltpu.VMEM((SUBLANES, 2 * half), F32)],
        compiler_params=_params("parallel", "arbitrary"),
    )(src, w, amat)


def _ssm_scan(u, bmat, cmat, amat, carries, dvec, name):
    T, C = u.shape
    nblk, seg_steps, kc, nchunk, rc = _ssm_geometry(T, C)
    half = bmat.shape[2] // 2

    def body(u_ref, b_ref, c_ref, a_ref, s_ref, d_ref, y_ref, x_ref, st_ref):
        c = pl.program_id(1)

        @pl.when(c == 0)
        def _():
            st_ref[...] = s_ref[0]

        uv = u_ref[...]
        x_ref[...] = _dot(uv.astype(BF16), b_ref[0], NN)
        ar = a_ref[0, :, :half]
        ai = a_ref[0, :, half:]

        def step(k, carry):
            xr, xi = carry
            row = pl.multiple_of(k * SUBLANES, SUBLANES)
            nr = ar * xr - ai * xi + x_ref[pl.ds(row, SUBLANES), :half]
            ni = ar * xi + ai * xr + x_ref[pl.ds(row, SUBLANES), half:]
            x_ref[pl.ds(row, SUBLANES), :half] = nr
            x_ref[pl.ds(row, SUBLANES), half:] = ni
            return nr, ni

        xr, xi = lax.fori_loop(0, kc, step, (st_ref[:, :half], st_ref[:, half:]), unroll=8)
        st_ref[:, :half] = xr
        st_ref[:, half:] = xi
        y_ref[...] = _dot(x_ref[...].astype(BF16), c_ref[0], NN) + d_ref[...] * uv

    return pl.pallas_call(
        body, name=name, grid=(nblk, nchunk),
        in_specs=[pl.BlockSpec((rc, SSM_BLOCK_CH), lambda b, c: (c, b)),
                  pl.BlockSpec((1,) + bmat.shape[1:], lambda b, c: (b, 0, 0)),
                  pl.BlockSpec((1,) + cmat.shape[1:], lambda b, c: (b, 0, 0)),
                  pl.BlockSpec((1, SUBLANES, 2 * half), lambda b, c: (b, 0, 0)),
                  pl.BlockSpec((1, SUBLANES, 2 * half), lambda b, c: (b, 0, 0)),
                  pl.BlockSpec((1, SSM_BLOCK_CH), lambda b, c: (0, b))],
        out_specs=[pl.BlockSpec((rc, SSM_BLOCK_CH), lambda b, c: (c, b)),
                   pl.BlockSpec((rc, 2 * half), lambda b, c: (c, b))],
        out_shape=[jax.ShapeDtypeStruct((T, C), F32), jax.ShapeDtypeStruct((T, nblk * 2 * half), F32)],
        scratch_shapes=[pltpu.VMEM((SUBLANES, 2 * half), F32)],
        compiler_params=_params("parallel", "arbitrary"),
    )(u, bmat, cmat, amat, carries, dvec)


def _ssm_scan_bwd(dy, u, xs, bmat, cmat, amat, carries, dvec, name):
    T, C = u.shape
    nblk, seg_steps, kc, nchunk, rc = _ssm_geometry(T, C)
    half = bmat.shape[2] // 2
    width = 2 * half

    def body(dy_ref, u_ref, x_ref, xp_ref, b_ref, c_ref, a_ref, s_ref, d_ref,
             du_ref, db_ref, dc_ref, da_ref, dd_ref, g_ref, st_ref, acc_ref):
        c = pl.program_id(1)

        @pl.when(c == 0)
        def _():
            st_ref[...] = s_ref[0]
            acc_ref[...] = jnp.zeros_like(acc_ref)
            db_ref[...] = jnp.zeros_like(db_ref)
            dc_ref[...] = jnp.zeros_like(dc_ref)
            dd_ref[...] = jnp.zeros_like(dd_ref)

        dyv = dy_ref[...]
        uv = u_ref[...]
        dyb = dyv.astype(BF16)
        g_ref[...] = _dot(dyb, c_ref[0], NT)
        ar = a_ref[0, :, :half]
        ai = a_ref[0, :, half:]

        def step(i, carry):
            gr, gi, sr, si = carry
            k = kc - 1 - i
            row = pl.multiple_of(k * SUBLANES, SUBLANES)
            nr = ar * gr + ai * gi + g_ref[pl.ds(row, SUBLANES), :half]
            ni = ar * gi - ai * gr + g_ref[pl.ds(row, SUBLANES), half:]
            g_ref[pl.ds(row, SUBLANES), :half] = nr
            g_ref[pl.ds(row, SUBLANES), half:] = ni
            prow = pl.multiple_of(jnp.maximum(k - 1, 0) * SUBLANES, SUBLANES)
            live = (k >= 1).astype(F32)
            xr = x_ref[pl.ds(prow, SUBLANES), :half] * live
            xi = x_ref[pl.ds(prow, SUBLANES), half:] * live
            return nr, ni, sr + xr * nr + xi * ni, si + xr * ni - xi * nr

        init = (st_ref[:, :half], st_ref[:, half:], acc_ref[:, :half], acc_ref[:, half:])
        gr, gi, sr, si = lax.fori_loop(0, kc, step, init, unroll=8)
        st_ref[:, :half] = gr
        st_ref[:, half:] = gi
        xpr = xp_ref[:, :half]
        xpi = xp_ref[:, half:]
        first = (c == nchunk - 1)
        xpr = jnp.where(first, _shift_segments(xpr, True), xpr)
        xpi = jnp.where(first, _shift_segments(xpi, True), xpi)
        acc_ref[:, :half] = sr + xpr * gr + xpi * gi
        acc_ref[:, half:] = si + xpr * gi - xpi * gr

        gb = g_ref[...].astype(BF16)
        du_ref[...] = _dot(gb, b_ref[0], NT) + dyv * d_ref[...]
        db_ref[0] += _dot(uv.astype(BF16), gb, TN)
        dc_ref[0] += _dot(dyb, x_ref[...].astype(BF16), TN)
        dd_ref[...] += jnp.sum(dyv * uv, axis=0, keepdims=True)

        @pl.when(c == nchunk - 1)
        def _():
            tot = jnp.sum(acc_ref[...], axis=0, keepdims=True)
            da_ref[0] = jnp.broadcast_to(tot, (SUBLANES, width))

    rev = lambda b, c: (nchunk - 1 - c, b)
    blk3 = lambda b, c: (b, 0, 0)
    prev_group = lambda b, c: (((nchunk - 1 - c) * kc - 1 + seg_steps) % seg_steps, b)
    sd = jax.ShapeDtypeStruct
    return pl.pallas_call(
        body, name=name, grid=(nblk, nchunk),
        in_specs=[pl.BlockSpec((rc, SSM_BLOCK_CH), rev), pl.BlockSpec((rc, SSM_BLOCK_CH), rev),
                  pl.BlockSpec((rc, width), rev), pl.BlockSpec((SUBLANES, width), prev_group),
                  pl.BlockSpec((1,) + bmat.shape[1:], blk3), pl.BlockSpec((1,) + cmat.shape[1:], blk3),
                  pl.BlockSpec((1, SUBLANES, width), blk3), pl.BlockSpec((1, SUBLANES, width), blk3),
                  pl.BlockSpec((1, SSM_BLOCK_CH), lambda b, c: (0, b))],
        out_specs=[pl.BlockSpec((rc, SSM_BLOCK_CH), rev), pl.BlockSpec((1, SSM_BLOCK_CH, width), blk3),
                   pl.BlockSpec((1, SSM_BLOCK_CH, width), blk3), pl.BlockSpec((1, SUBLANES, width), blk3),
                   pl.BlockSpec((1, SSM_BLOCK_CH), lambda b, c: (0, b))],
        out_shape=[sd((T, C), F32), sd((nblk, SSM_BLOCK_CH, width), F32), sd((nblk, SSM_BLOCK_CH, width), F32),
                   sd((nblk, SUBLANES, width), F32), sd((1, C), F32)],
        scratch_shapes=[pltpu.VMEM((rc, width), F32), pltpu.VMEM((SUBLANES, width), F32), pltpu.VMEM((SUBLANES, width), F32)],
        compiler_params=_params("parallel", "arbitrary"),
    )(dy, u, xs, xs, bmat, cmat, amat, carries, dvec)


SSM_STEPS_FWD = 128
SSM_STEPS_BWD = 256


def _ssm_tiles(ref, v, off, steps, n):
    return [ref[v, pl.ds(off + j, steps, stride=SUBLANES), :] for j in range(n)]


def _ssm_fwd(uz, bmat, cmat, art, ait, dvec, name):
    T = uz.shape[0]
    nblk, cb, width = bmat.shape
    C = nblk * cb
    half = width // 2
    nt = half // LANES
    npair = nblk // 2
    kc = min(SSM_STEPS_FWD, T)
    nchunk = T // kc

    def body(u_ref, b_ref, c_ref, ar_ref, ai_ref, d_ref, y_ref, xr_ref, xi_ref, sr_ref, si_ref):
        @pl.when(pl.program_id(0) == 0)
        def _():
            sr_ref[...] = jnp.zeros_like(sr_ref)
            si_ref[...] = jnp.zeros_like(si_ref)

        uv = u_ref[...]
        for b in range(nblk):
            bu = _dot(uv[:, b * cb:(b + 1) * cb].astype(BF16), b_ref[b], NN)
            v, off = b // 2, nt * (b % 2)
            for j in range(nt):
                xr_ref[v, pl.ds(off + j, kc, stride=SUBLANES), :] = bu[:, j * LANES:(j + 1) * LANES]
                xi_ref[v, pl.ds(off + j, kc, stride=SUBLANES), :] = bu[:, half + j * LANES:half + (j + 1) * LANES]
        ars = [ar_ref[v] for v in range(npair)]
        ais = [ai_ref[v] for v in range(npair)]

        def step(k, carry):
            row = pl.ds(k * SUBLANES, SUBLANES)
            out = []
            for v in range(npair):
                xr, xi = carry[2 * v], carry[2 * v + 1]
                nr = ars[v] * xr - ais[v] * xi + xr_ref[v, row, :]
                ni = ars[v] * xi + ais[v] * xr + xi_ref[v, row, :]
                xr_ref[v, row, :] = nr
                xi_ref[v, row, :] = ni
                out += [nr, ni]
            return tuple(out)

        fin = tuple(ref[v] for v in range(npair) for ref in (sr_ref, si_ref))
        for k in range(kc):
            fin = step(k, fin)
        for v in range(npair):
            sr_ref[v] = fin[2 * v]
            si_ref[v] = fin[2 * v + 1]
        for b in range(nblk):
            v, off = b // 2, nt * (b % 2)
            xb = jnp.concatenate(_ssm_tiles(xr_ref, v, off, kc, nt) + _ssm_tiles(xi_ref, v, off, kc, nt), axis=1)
            cols = slice(b * cb, (b + 1) * cb)
            y_ref[:, cols] = _dot(xb.astype(BF16), c_ref[b], NN) + d_ref[:, cols] * uv[:, cols]

    whole = lambda a: pl.BlockSpec(a.shape, lambda c: (0,) * a.ndim)
    st = pl.BlockSpec((npair, kc * SUBLANES, LANES), lambda c: (0, c, 0))
    sd = jax.ShapeDtypeStruct
    return pl.pallas_call(
        body, name=name, grid=(nchunk,),
        in_specs=[pl.BlockSpec((kc, C), lambda c: (c, 0)), whole(bmat), whole(cmat), whole(art), whole(ait), whole(dvec)],
        out_specs=[pl.BlockSpec((kc, C), lambda c: (c, 0)), st, st],
        out_shape=[sd((T, C), F32), sd((npair, T * SUBLANES, LANES), F32), sd((npair, T * SUBLANES, LANES), F32)],
        scratch_shapes=[pltpu.VMEM((npair, SUBLANES, LANES), F32), pltpu.VMEM((npair, SUBLANES, LANES), F32)],
        compiler_params=_params("arbitrary"),
    )(uz, bmat, cmat, art, ait, dvec)


def _ssm_bwd(dy, uz, xr, xi, bmat, cmat, art, ait, dvec, name):
    T = uz.shape[0]
    nblk, cb, width = bmat.shape
    C = nblk * cb
    half = width // 2
    nt = half // LANES
    npair = nblk // 2
    kc = min(SSM_STEPS_BWD, T)
    nchunk = T // kc

    def body(dy_ref, u_ref, xr_ref, xi_ref, xpr_ref, xpi_ref, b_ref, c_ref, ar_ref, ai_ref, d_ref,
             du_ref, db_ref, dc_ref, dar_ref, dai_ref, dd_ref, gr_ref, gi_ref, sr_ref, si_ref):
        c = pl.program_id(0)

        @pl.when(c == 0)
        def _():
            for ref in (sr_ref, si_ref, db_ref, dc_ref, dar_ref, dai_ref, dd_ref):
                ref[...] = jnp.zeros_like(ref)

        dyv = dy_ref[...]
        uv = u_ref[...]
        for b in range(nblk):
            dx = _dot(dyv[:, b * cb:(b + 1) * cb].astype(BF16), c_ref[b], NT)
            v, off = b // 2, nt * (b % 2)
            for j in range(nt):
                gr_ref[v, pl.ds(off + j, kc, stride=SUBLANES), :] = dx[:, j * LANES:(j + 1) * LANES]
                gi_ref[v, pl.ds(off + j, kc, stride=SUBLANES), :] = dx[:, half + j * LANES:half + (j + 1) * LANES]
        ars = [ar_ref[v] for v in range(npair)]
        ais = [ai_ref[v] for v in range(npair)]

        def pair_update(v, gr, gi, row):
            nr = ars[v] * gr + ais[v] * gi + gr_ref[v, row, :]
            ni = ars[v] * gi - ais[v] * gr + gi_ref[v, row, :]
            gr_ref[v, row, :] = nr
            gi_ref[v, row, :] = ni
            return nr, ni

        def step(i, carry):
            k = kc - 1 - i
            row = pl.ds(k * SUBLANES, SUBLANES)
            prow = pl.ds((k - 1) * SUBLANES, SUBLANES)
            out = []
            for v in range(npair):
                gr, gi, sr, si = carry[4 * v:4 * v + 4]
                nr, ni = pair_update(v, gr, gi, row)
                pr, pi = xr_ref[v, prow, :], xi_ref[v, prow, :]
                out += [nr, ni, sr + pr * nr + pi * ni, si + pr * ni - pi * nr]
            return tuple(out)

        mid = tuple(ref[v] for v in range(npair) for ref in (sr_ref, si_ref, dar_ref, dai_ref))
        for i in range(kc - 1):
            mid = step(i, mid)
        live = (c < nchunk - 1).astype(F32)
        row0 = pl.ds(0, SUBLANES)
        for v in range(npair):
            gr, gi, sr, si = mid[4 * v:4 * v + 4]
            nr, ni = pair_update(v, gr, gi, row0)
            pr, pi = xpr_ref[v] * live, xpi_ref[v] * live
            sr_ref[v] = nr
            si_ref[v] = ni
            dar_ref[v] = sr + pr * nr + pi * ni
            dai_ref[v] = si + pr * ni - pi * nr
        for b in range(nblk):
            v, off = b // 2, nt * (b % 2)
            cols = slice(b * cb, (b + 1) * cb)
            gb = jnp.concatenate(_ssm_tiles(gr_ref, v, off, kc, nt) + _ssm_tiles(gi_ref, v, off, kc, nt), axis=1).astype(BF16)
            xb = jnp.concatenate(_ssm_tiles(xr_ref, v, off, kc, nt) + _ssm_tiles(xi_ref, v, off, kc, nt), axis=1).astype(BF16)
            du_ref[:, cols] = _dot(gb, b_ref[b], NT) + dyv[:, cols] * d_ref[:, cols]
            db_ref[b] += _dot(uv[:, cols].astype(BF16), gb, TN)
            dc_ref[b] += _dot(dyv[:, cols].astype(BF16), xb, TN)
        dd_ref[...] += jnp.sum(dyv * uv, axis=0, keepdims=True)

    whole = lambda a: pl.BlockSpec(a.shape, lambda c: (0,) * a.ndim)
    rev = lambda c: (nchunk - 1 - c, 0)
    st = pl.BlockSpec((npair, kc * SUBLANES, LANES), lambda c: (0, nchunk - 1 - c, 0))
    stp = pl.BlockSpec((npair, SUBLANES, LANES), lambda c: (0, jnp.maximum((nchunk - 1 - c) * kc - 1, 0), 0))
    acc = lambda shape: pl.BlockSpec(shape, lambda c: (0,) * len(shape))
    sd = jax.ShapeDtypeStruct
    pair_shape = (npair, SUBLANES, LANES)
    return pl.pallas_call(
        body, name=name, grid=(nchunk,),
        in_specs=[pl.BlockSpec((kc, C), rev), pl.BlockSpec((kc, C), rev), st, st, stp, stp, whole(bmat), whole(cmat),
                  whole(art), whole(ait), whole(dvec)],
        out_specs=[pl.BlockSpec((kc, C), rev), acc(bmat.shape), acc(bmat.shape), acc(pair_shape), acc(pair_shape), acc((1, C))],
        out_shape=[sd((T, C), F32), sd(bmat.shape, F32), sd(bmat.shape, F32), sd(pair_shape, F32), sd(pair_shape, F32),
                   sd((1, C), F32)],
        scratch_shapes=[pltpu.VMEM((npair, kc * SUBLANES, LANES), F32), pltpu.VMEM((npair, kc * SUBLANES, LANES), F32),
                        pltpu.VMEM(pair_shape, F32), pltpu.VMEM(pair_shape, F32)],
        compiler_params=_params("arbitrary"),
    )(dy, uz, xr, xi, xr, xi, bmat, cmat, art, ait, dvec)


def _ssm_post(y, z, w_glu, b_glu, name):
    T, C = y.shape
    tm = _pick(T, 512, SUBLANES)

    def body(y_ref, z_ref, w_ref, b_ref, o_ref, a_ref):
        a, _ = _gelu_and_grad(y_ref[...])
        ab = a.astype(BF16)
        sg = _sigmoid(_dot(ab, w_ref[...], NN) + b_ref[...])
        sz, _ = _silu_and_grad(z_ref[...].astype(F32))
        o_ref[...] = (a * sg * sz).astype(BF16)
        a_ref[...] = ab

    row = pl.BlockSpec((tm, C), lambda i: (i, 0))
    return pl.pallas_call(
        body, name=name, grid=(T // tm,),
        in_specs=[row, row, pl.BlockSpec((C, C), lambda i: (0, 0)), pl.BlockSpec((1, C), lambda i: (0, 0))],
        out_specs=[row, row], out_shape=[jax.ShapeDtypeStruct((T, C), BF16)] * 2, compiler_params=_params("parallel"),
    )(y, z, w_glu, b_glu.reshape(1, C))


def _ssm_post_bwd(do, y, z, w_glu, b_glu, name):
    T, C = y.shape
    tm = _pick(T, 512, SUBLANES)

    def body(do_ref, y_ref, z_ref, w_ref, b_ref, dy_ref, dz_ref, ds_ref, db_ref):
        dov = do_ref[...]
        a, da_dy = _gelu_and_grad(y_ref[...])
        sg = _sigmoid(_dot(a.astype(BF16), w_ref[...], NN) + b_ref[...])
        sz, dsz = _silu_and_grad(z_ref[...].astype(F32))
        yg = a * sg
        dz_ref[...] = (dov * yg * dsz).astype(BF16)
        dyg = dov * sz
        ds = dyg * a * sg * (1.0 - sg)
        dsb = ds.astype(BF16)
        ds_ref[...] = dsb
        da = dyg * sg + _dot(dsb, w_ref[...], NT)
        dy_ref[...] = da * da_dy

        @pl.when(pl.program_id(0) == 0)
        def _():
            db_ref[...] = jnp.zeros_like(db_ref)

        db_ref[...] += jnp.sum(ds, axis=0, keepdims=True)

    row = pl.BlockSpec((tm, C), lambda i: (i, 0))
    vec = pl.BlockSpec((1, C), lambda i: (0, 0))
    sd = jax.ShapeDtypeStruct
    return pl.pallas_call(
        body, name=name, grid=(T // tm,),
        in_specs=[row, row, row, pl.BlockSpec((C, C), lambda i: (0, 0)), vec],
        out_specs=[row, row, row, vec],
        out_shape=[sd((T, C), F32), sd((T, C), BF16), sd((T, C), BF16), sd((1, C), F32)],
        compiler_params=_params("arbitrary"),
    )(do, y, z, w_glu, b_glu.reshape(1, C))


def _rel_bucket(dist):
    n = jnp.maximum(dist, 0)
    max_exact = NUM_BUCKETS // 2
    n_f = jnp.maximum(n, 1).astype(F32)
    large = max_exact + (jnp.log(n_f / max_exact) / math.log(REL_MAX_DISTANCE / max_exact)
                         * (NUM_BUCKETS - max_exact)).astype(jnp.int32)
    large = jnp.minimum(large, NUM_BUCKETS - 1)
    return jnp.where(n < max_exact, n, large)


def _band_tables():
    qi = jnp.arange(ATTN_BLOCK)[:, None]
    kj = jnp.arange(2 * ATTN_BLOCK)[None, :]
    delta = ATTN_BLOCK + qi - kj
    buckets, bands = [], []
    for window, dilation in ATTN_CONFIGS:
        bands.append((delta >= 0) & (delta <= window // dilation))
        buckets.append(_rel_bucket(jnp.maximum(delta, 0) * dilation))
    return jnp.stack(buckets), jnp.stack(bands)


def _attn_blocks_per_residue(T):
    return [T // (ATTN_BLOCK * d) for _, d in ATTN_CONFIGS]


ATTN_UNITS = 4


def _attn_tile(T, r):
    nq = max(1, ATTN_UNITS // r)
    rows = ATTN_BLOCK * r * nq
    return nq, rows, T // rows


def _attn_units(r, nq, chunk):
    if r >= ATTN_UNITS:
        return [(chunk * ATTN_UNITS + i, None) for i in range(ATTN_UNITS)]
    units = []
    for j in range(nq):
        for s in range(r):
            units.append((ATTN_BLOCK * j * r + s, ATTN_BLOCK * (j - 1) * r + s if j else None))
    return units


def _rows(start, r):
    return pl.ds(start, ATTN_BLOCK, stride=r) if r > 1 else pl.ds(start, ATTN_BLOCK)


def _attn_group_fwd(qkv, biasm, g, name):
    T = qkv.shape[0]
    r = ATTN_CONFIGS[g][1]
    B, hd = ATTN_BLOCK, ATTN_HEAD_DIM
    nq, rows, ntiles = _attn_tile(T, r)
    nchunks = max(1, r // ATTN_UNITS)
    last_prev = B * (nq - 1) * r
    scale = hd ** -0.5
    tiles_per_tensor = 3 * HEADS_PER_GROUP * hd // LANES

    def body(q_ref, kc_ref, kp_ref, vc_ref, vp_ref, bias_ref, o_ref, lse_ref, s_ref, p_ref):
        n = pl.program_id(1)
        lane = lax.broadcasted_iota(jnp.int32, (1, LANES), 1)
        col = lax.broadcasted_iota(jnp.int32, (1, 2 * B), 1)
        masks = [lane < hd, lane >= hd]
        first_pen = jnp.where((col < B) & (n == 0), NEG_INF, 0.0)

        def chunk_body(chunk):
            units = _attn_units(r, nq, chunk)

            def keys(cur_ref, prev_ref, cs, ps):
                prev = prev_ref[_rows(last_prev + (cs if r >= ATTN_UNITS else cs % r), r), :] if ps is None else cur_ref[_rows(ps, r), :]
                return jnp.concatenate([prev, cur_ref[_rows(cs, r), :]], axis=0).astype(BF16)

            for u, (cs, ps) in enumerate(units):
                qv = q_ref[_rows(cs, r), :]
                kw = keys(kc_ref, kp_ref, cs, ps)
                for hh in range(2):
                    s_ref[2 * u + hh] = _dot(jnp.where(masks[hh], qv, 0.0).astype(BF16), kw, NT)
            for u, (cs, ps) in enumerate(units):
                lses = []
                for hh in range(2):
                    s = s_ref[2 * u + hh] * scale + bias_ref[hh]
                    if ps is None:
                        s = s + first_pen
                    m = jnp.max(s, axis=-1, keepdims=True)
                    p = jnp.exp(s - m)
                    l = jnp.sum(p, axis=-1, keepdims=True)
                    p_ref[2 * u + hh] = (p / l).astype(BF16)
                    lses.append(m + jnp.log(l))
                lse_ref[_rows(cs, r), :] = jnp.where(masks[0], lses[0], lses[1])
            for u, (cs, ps) in enumerate(units):
                vw = keys(vc_ref, vp_ref, cs, ps)
                o_ref[_rows(cs, r), :] = (_dot(p_ref[2 * u], jnp.where(masks[0], vw, 0), NN)
                                          + _dot(p_ref[2 * u + 1], jnp.where(masks[1], vw, 0), NN))

        if nchunks == 1:
            chunk_body(0)
        else:
            pl.loop(0, nchunks)(chunk_body)

    def cur(t):
        return pl.BlockSpec((rows, LANES), lambda hf, n: (n, t * tiles_per_tensor + 2 * g + hf))

    def prev(t):
        return pl.BlockSpec((rows, LANES), lambda hf, n: (jnp.maximum(n - 1, 0), t * tiles_per_tensor + 2 * g + hf))

    out = pl.BlockSpec((rows, LANES), lambda hf, n: (n, hf))
    sd = jax.ShapeDtypeStruct((T, 2 * LANES), F32)
    return pl.pallas_call(
        body, name=name, grid=(2, ntiles),
        in_specs=[cur(0), cur(1), prev(1), cur(2), prev(2), pl.BlockSpec((None, 2, B, 2 * B), lambda hf, n: (g, hf, 0, 0))],
        out_specs=[out, out], out_shape=[sd, sd],
        scratch_shapes=[pltpu.VMEM((2 * ATTN_UNITS, B, 2 * B), F32), pltpu.VMEM((2 * ATTN_UNITS, B, 2 * B), BF16)],
        compiler_params=_params("parallel", "parallel"),
    )(qkv, qkv, qkv, qkv, qkv, biasm)


def _attn_group_bwd(qkv, do, dvec, lse, biasm, g, name):
    T = qkv.shape[0]
    r = ATTN_CONFIGS[g][1]
    B, hd = ATTN_BLOCK, ATTN_HEAD_DIM
    nq, rows, ntiles = _attn_tile(T, r)
    nchunks = max(1, r // ATTN_UNITS)
    last_prev = B * (nq - 1) * r
    scale = hd ** -0.5
    tiles_per_tensor = 3 * HEADS_PER_GROUP * hd // LANES

    def body(q_ref, kc_ref, kp_ref, vc_ref, vp_ref, do_ref, dv_ref, lse_ref, bias_ref,
             dq_ref, dk_ref, dvo_ref, dbias_ref, ck_ref, cv_ref, ak_ref, av_ref, s_ref, dp_ref, p_ref, ds_ref):
        n = pl.program_id(1)
        lane = lax.broadcasted_iota(jnp.int32, (1, LANES), 1)
        col = lax.broadcasted_iota(jnp.int32, (1, 2 * B), 1)
        masks = [lane < hd, lane >= hd]
        first_pen = jnp.where((col < B) & (n == 0), NEG_INF, 0.0)

        @pl.when(n == 0)
        def _():
            dbias_ref[...] = jnp.zeros_like(dbias_ref)
            ck_ref[...] = jnp.zeros_like(ck_ref)
            cv_ref[...] = jnp.zeros_like(cv_ref)

        def chunk_body(chunk):
            units = _attn_units(r, nq, chunk)

            def prev_rows(cs):
                return _rows(last_prev + (cs if r >= ATTN_UNITS else cs % r), r)

            def keys(cur_ref, prev_ref, cs, ps):
                prev = prev_ref[prev_rows(cs), :] if ps is None else cur_ref[_rows(ps, r), :]
                return jnp.concatenate([prev, cur_ref[_rows(cs, r), :]], axis=0).astype(BF16)

            for u, (cs, ps) in enumerate(units):
                qv = q_ref[_rows(cs, r), :]
                dov = do_ref[_rows(cs, r), :]
                kw = keys(kc_ref, kp_ref, cs, ps)
                vw = keys(vc_ref, vp_ref, cs, ps)
                for hh in range(2):
                    s_ref[2 * u + hh] = _dot(jnp.where(masks[hh], qv, 0.0).astype(BF16), kw, NT)
                    dp_ref[2 * u + hh] = _dot(jnp.where(masks[hh], dov, 0.0).astype(BF16), vw, NT)
            for u, (cs, ps) in enumerate(units):
                lse_t = lse_ref[_rows(cs, r), :]
                dv_t = dv_ref[_rows(cs, r), :]
                for hh in range(2):
                    lo = hh * hd
                    s = s_ref[2 * u + hh] * scale + bias_ref[hh]
                    if ps is None:
                        s = s + first_pen
                    p = jnp.exp(s - lse_t[:, lo:lo + 1])
                    ds = p * (dp_ref[2 * u + hh] + dv_t[:, lo:lo + 1])
                    dbias_ref[hh] += ds
                    p_ref[2 * u + hh] = p.astype(BF16)
                    ds_ref[2 * u + hh] = ds.astype(BF16)
            for u, (cs, ps) in enumerate(units):
                qv = q_ref[_rows(cs, r), :]
                dov = do_ref[_rows(cs, r), :]
                kw = keys(kc_ref, kp_ref, cs, ps)
                dq, dkw, dvw = 0.0, 0.0, 0.0
                for hh in range(2):
                    dsb = ds_ref[2 * u + hh]
                    dq = dq + _dot(dsb, jnp.where(masks[hh], kw, 0), NN)
                    dkw = dkw + _dot(dsb, jnp.where(masks[hh], qv, 0.0).astype(BF16), TN)
                    dvw = dvw + _dot(p_ref[2 * u + hh], jnp.where(masks[hh], dov, 0.0).astype(BF16), TN)
                dq_ref[_rows(cs, r), :] = dq * scale
                ak_ref[_rows(cs, r), :] = dkw[B:] * scale
                av_ref[_rows(cs, r), :] = dvw[B:]
                if ps is None:
                    ck_ref[prev_rows(cs), :] += dkw[:B] * scale
                    cv_ref[prev_rows(cs), :] += dvw[:B]
                else:
                    ak_ref[_rows(ps, r), :] += dkw[:B] * scale
                    av_ref[_rows(ps, r), :] += dvw[:B]

        @pl.when(n < ntiles)
        def _():
            for chunk in range(nchunks):
                chunk_body(chunk)

        dk_ref[...] = ck_ref[...].astype(BF16)
        dvo_ref[...] = cv_ref[...].astype(BF16)
        ck_ref[...] = ak_ref[...]
        cv_ref[...] = av_ref[...]

    last = ntiles - 1

    def cur(t):
        return pl.BlockSpec((rows, LANES), lambda hf, n: (jnp.minimum(n, last), t * tiles_per_tensor + 2 * g + hf))

    def prev(t):
        return pl.BlockSpec((rows, LANES), lambda hf, n: (jnp.clip(n - 1, 0, last), t * tiles_per_tensor + 2 * g + hf))

    nat = pl.BlockSpec((rows, LANES), lambda hf, n: (jnp.minimum(n, last), hf))
    nat_prev = pl.BlockSpec((rows, LANES), lambda hf, n: (jnp.clip(n - 1, 0, last), hf))
    tab = pl.BlockSpec((None, 2, B, 2 * B), lambda hf, n: (g, hf, 0, 0))
    dtab = pl.BlockSpec((2, B, 2 * B), lambda hf, n: (hf, 0, 0))
    sd = jax.ShapeDtypeStruct
    vm = pltpu.VMEM
    return pl.pallas_call(
        body, name=name, grid=(2, ntiles + 1),
        in_specs=[cur(0), cur(1), prev(1), cur(2), prev(2), nat, nat, nat, tab],
        out_specs=[nat, nat_prev, nat_prev, dtab],
        out_shape=[sd((T, 2 * LANES), F32), sd((T, 2 * LANES), BF16), sd((T, 2 * LANES), BF16),
                   sd((HEADS_PER_GROUP, B, 2 * B), F32)],
        scratch_shapes=[vm((rows, LANES), F32), vm((rows, LANES), F32), vm((rows, LANES), F32), vm((rows, LANES), F32),
                        vm((2 * ATTN_UNITS, B, 2 * B), F32), vm((2 * ATTN_UNITS, B, 2 * B), F32),
                        vm((2 * ATTN_UNITS, B, 2 * B), BF16), vm((2 * ATTN_UNITS, B, 2 * B), BF16)],
        compiler_params=_params("parallel", "arbitrary"),
    )(qkv, qkv, qkv, qkv, qkv, do, dvec, lse, biasm)


def _attn_fwd(q, k, v, biasm, name):
    ng, T, gw = q.shape
    hd = ATTN_HEAD_DIM
    nh = gw // hd
    nblk = T // ATTN_BLOCK
    nbs = _attn_blocks_per_residue(T)
    scale = hd ** -0.5
    B = ATTN_BLOCK

    def body(q_ref, kc_ref, kp_ref, vc_ref, vp_ref, bias_ref, o_ref, lse_ref, s_ref, p_ref):
        g = pl.program_id(0)
        b = pl.program_id(1)
        nb = jnp.where(g == 0, nbs[0], jnp.where(g == 1, nbs[1], nbs[2]))
        no_prev = (b % nb) == 0
        col = lax.broadcasted_iota(jnp.int32, (1, 2 * B), 1)
        pen = jnp.where((col < B) & no_prev, NEG_INF, 0.0)
        heads = [slice(h * hd, (h + 1) * hd) for h in range(nh)]
        for h, hs in enumerate(heads):
            kw = jnp.concatenate([kp_ref[0, :, hs], kc_ref[0, :, hs]], axis=0)
            s_ref[h] = _dot(q_ref[0, :, hs], kw, NT)
        for h, hs in enumerate(heads):
            s = s_ref[h] * scale + bias_ref[0, h] + pen
            m = jnp.max(s, axis=-1, keepdims=True)
            p = jnp.exp(s - m)
            l = jnp.sum(p, axis=-1, keepdims=True)
            p_ref[h] = (p / l).astype(BF16)
            lse_ref[0, :, hs] = jnp.broadcast_to(m + jnp.log(l), (B, hd))
        for h, hs in enumerate(heads):
            vw = jnp.concatenate([vp_ref[0, :, hs], vc_ref[0, :, hs]], axis=0)
            o_ref[0, :, hs] = _dot(p_ref[h], vw, NN)

    cur = pl.BlockSpec((1, B, gw), lambda g, b: (g, b, 0))
    prev = pl.BlockSpec((1, B, gw), lambda g, b: (g, jnp.maximum(b - 1, 0), 0))
    return pl.pallas_call(
        body, name=name, grid=(ng, nblk),
        in_specs=[cur, cur, prev, cur, prev, pl.BlockSpec((1, nh, B, 2 * B), lambda g, b: (g, 0, 0, 0))],
        out_specs=[cur, cur], out_shape=[jax.ShapeDtypeStruct(q.shape, F32)] * 2,
        scratch_shapes=[pltpu.VMEM((nh, B, 2 * B), F32), pltpu.VMEM((nh, B, 2 * B), BF16)],
        compiler_params=_params("parallel", "parallel"),
    )(q, k, k, v, v, biasm)


def _attn_bwd(q, k, v, do, dvec, lse, biasm, name):
    ng, T, gw = q.shape
    hd = ATTN_HEAD_DIM
    nh = gw // hd
    nblk = T // ATTN_BLOCK
    nbs = _attn_blocks_per_residue(T)
    scale = hd ** -0.5
    B = ATTN_BLOCK

    def body(q_ref, kc_ref, kp_ref, vc_ref, vp_ref, do_ref, dv_ref, lse_ref, bias_ref,
             dq_ref, dk_ref, dvo_ref, dbias_ref, ck_ref, cv_ref, s_ref, dp_ref, p_ref, ds_ref):
        g = pl.program_id(0)
        b = pl.program_id(1)
        nb = jnp.where(g == 0, nbs[0], jnp.where(g == 1, nbs[1], nbs[2]))
        no_prev = (b % nb) == 0

        @pl.when(b == 0)
        def _():
            dbias_ref[...] = jnp.zeros_like(dbias_ref)
            ck_ref[...] = jnp.zeros_like(ck_ref)
            cv_ref[...] = jnp.zeros_like(cv_ref)

        @pl.when(b < nblk)
        def _():
            col = lax.broadcasted_iota(jnp.int32, (1, 2 * B), 1)
            pen = jnp.where((col < B) & no_prev, NEG_INF, 0.0)
            heads = [slice(h * hd, (h + 1) * hd) for h in range(nh)]
            for h, hs in enumerate(heads):
                kw = jnp.concatenate([kp_ref[0, :, hs], kc_ref[0, :, hs]], axis=0)
                vw = jnp.concatenate([vp_ref[0, :, hs], vc_ref[0, :, hs]], axis=0)
                s_ref[h] = _dot(q_ref[0, :, hs], kw, NT)
                dp_ref[h] = _dot(do_ref[0, :, hs], vw, NT)
            for h, hs in enumerate(heads):
                lse_col = lse_ref[0, :, h * hd:h * hd + 1]
                d_col = dv_ref[0, :, h * hd:h * hd + 1]
                p = jnp.exp(s_ref[h] * scale + bias_ref[0, h] + pen - lse_col)
                ds = p * (dp_ref[h] + d_col)
                dbias_ref[0, h] += ds
                p_ref[h] = p.astype(BF16)
                ds_ref[h] = ds.astype(BF16)
            for h, hs in enumerate(heads):
                qh = q_ref[0, :, hs]
                kw = jnp.concatenate([kp_ref[0, :, hs], kc_ref[0, :, hs]], axis=0)
                dq_ref[0, :, hs] = (_dot(ds_ref[h], kw, NN) * scale).astype(BF16)
                dkw = _dot(ds_ref[h], qh, TN) * scale
                dvw = _dot(p_ref[h], do_ref[0, :, hs], TN)
                dk_ref[0, :, hs] = (ck_ref[:, hs] + dkw[:B]).astype(BF16)
                dvo_ref[0, :, hs] = (cv_ref[:, hs] + dvw[:B]).astype(BF16)
                ck_ref[:, hs] = dkw[B:]
                cv_ref[:, hs] = dvw[B:]

        @pl.when(b == nblk)
        def _():
            dk_ref[0] = ck_ref[...].astype(BF16)
            dvo_ref[0] = cv_ref[...].astype(BF16)

    last = nblk - 1
    cur = pl.BlockSpec((1, B, gw), lambda g, b: (g, jnp.minimum(b, last), 0))
    prev = pl.BlockSpec((1, B, gw), lambda g, b: (g, jnp.clip(b - 1, 0, last), 0))
    tab = pl.BlockSpec((1, nh, B, 2 * B), lambda g, b: (g, 0, 0, 0))
    sd = jax.ShapeDtypeStruct
    return pl.pallas_call(
        body, name=name, grid=(ng, nblk + 1),
        in_specs=[cur, cur, prev, cur, prev, cur, cur, cur, tab],
        out_specs=[cur, prev, prev, tab],
        out_shape=[sd(q.shape, BF16), sd(q.shape, BF16), sd(q.shape, BF16), sd(biasm.shape, F32)],
        scratch_shapes=[pltpu.VMEM((B, gw), F32), pltpu.VMEM((B, gw), F32), pltpu.VMEM((nh, B, 2 * B), F32),
                        pltpu.VMEM((nh, B, 2 * B), F32), pltpu.VMEM((nh, B, 2 * B), BF16), pltpu.VMEM((nh, B, 2 * B), BF16)],
        compiler_params=_params("parallel", "arbitrary"),
    )(q, k, k, v, v, do, dvec, lse, biasm)


def _attn_mix(os, lses, z, name):
    T, gw = os[0].shape
    C = z.shape[1]
    tm = _pick(T, 512, SUBLANES)

    def body(o0_ref, o1_ref, o2_ref, l0_ref, l1_ref, l2_ref, z_ref, out_ref):
        ls = [l0_ref[...], l1_ref[...], l2_ref[...]]
        mx = jnp.maximum(jnp.maximum(ls[0], ls[1]), ls[2])
        es = [jnp.exp(l - mx) for l in ls]
        den = es[0] + es[1] + es[2]
        for i, o_ref in enumerate((o0_ref, o1_ref, o2_ref)):
            sz, _ = _silu_and_grad(z_ref[:, i * gw:(i + 1) * gw].astype(F32))
            out_ref[:, i * gw:(i + 1) * gw] = (o_ref[...] * (es[i] / den) * sz).astype(BF16)

    row = pl.BlockSpec((tm, C), lambda i: (i, 0))
    grp = pl.BlockSpec((tm, gw), lambda i: (i, 0))
    return pl.pallas_call(
        body, name=name, grid=(T // tm,), in_specs=[grp] * 6 + [row], out_specs=row,
        out_shape=jax.ShapeDtypeStruct((T, C), BF16), compiler_params=_params("parallel"),
    )(*os, *lses, z)


def _attn_mix_bwd(dout, os, lses, z, name):
    T, gw = os[0].shape
    C = z.shape[1]
    tm = _pick(T, 512, SUBLANES)
    head_of = np.arange(gw) // ATTN_HEAD_DIM
    ones = jnp.asarray(head_of[:, None] == head_of[None, :], BF16)

    def body(dout_ref, o0_ref, o1_ref, o2_ref, l0_ref, l1_ref, l2_ref, z_ref, ones_ref,
             dz_ref, do0_ref, do1_ref, do2_ref, dv0_ref, dv1_ref, dv2_ref):
        ls = [l0_ref[...], l1_ref[...], l2_ref[...]]
        mx = jnp.maximum(jnp.maximum(ls[0], ls[1]), ls[2])
        es = [jnp.exp(l - mx) for l in ls]
        den = es[0] + es[1] + es[2]
        alphas, ebar = [], 0.0
        for i, (o_ref, do_ref) in enumerate(((o0_ref, do0_ref), (o1_ref, do1_ref), (o2_ref, do2_ref))):
            sl = slice(i * gw, (i + 1) * gw)
            alpha = es[i] / den
            ov = o_ref[...]
            dv = dout_ref[:, sl]
            sz, dsz = _silu_and_grad(z_ref[:, sl].astype(F32))
            dz_ref[:, sl] = (dv * ov * alpha * dsz).astype(BF16)
            da = dv * sz
            do_ref[...] = da * alpha
            t = da * ov
            t1 = t.astype(BF16)
            r1 = t - t1.astype(F32)
            t2 = r1.astype(BF16)
            t3 = (r1 - t2.astype(F32)).astype(BF16)
            e = _dot(t1, ones_ref[...], NN) + _dot(t2, ones_ref[...], NN) + _dot(t3, ones_ref[...], NN)
            ebar = ebar + alpha * e
            alphas.append(alpha)
        for alpha, dv_ref in zip(alphas, (dv0_ref, dv1_ref, dv2_ref)):
            dv_ref[...] = -alpha * ebar

    row = pl.BlockSpec((tm, C), lambda i: (i, 0))
    grp = pl.BlockSpec((tm, gw), lambda i: (i, 0))
    sd = jax.ShapeDtypeStruct
    res = pl.pallas_call(
        body, name=name, grid=(T // tm,),
        in_specs=[row] + [grp] * 6 + [row, pl.BlockSpec((gw, gw), lambda i: (0, 0))], out_specs=[row] + [grp] * 6,
        out_shape=[sd((T, C), BF16)] + [sd((T, gw), F32)] * 6, compiler_params=_params("parallel"),
    )(dout, *os, *lses, z, ones)
    return res[0], res[1:4], res[4:7]


def _mem_attn(qz, kv, name):
    T = qz.shape[0]
    dm = qz.shape[1] // 2
    M = kv.shape[0]
    hd = dm // MEM_HEADS
    scale = hd ** -0.5
    tm = _pick(T, 512, SUBLANES)

    def body(q_ref, z_ref, k_ref, v_ref, o_ref, s_ref, p_ref):
        heads = [slice(h * hd, (h + 1) * hd) for h in range(MEM_HEADS)]
        for h, sl in enumerate(heads):
            s_ref[h] = _dot(q_ref[:, sl].astype(BF16), k_ref[:, sl], NT)
        for h, sl in enumerate(heads):
            s = s_ref[h] * scale
            p = jnp.exp(s - jnp.max(s, axis=-1, keepdims=True))
            p_ref[h] = (p / jnp.sum(p, axis=-1, keepdims=True)).astype(BF16)
        for h, sl in enumerate(heads):
            sz, _ = _silu_and_grad(z_ref[:, sl].astype(F32))
            o_ref[:, sl] = (_dot(p_ref[h], v_ref[:, sl], NN) * sz).astype(BF16)

    return pl.pallas_call(
        body, name=name, grid=(T // tm,),
        in_specs=[pl.BlockSpec((tm, dm), lambda i: (i, 0)), pl.BlockSpec((tm, dm), lambda i: (i, 1)),
                  pl.BlockSpec((M, dm), lambda i: (0, 0)), pl.BlockSpec((M, dm), lambda i: (0, 1))],
        out_specs=pl.BlockSpec((tm, dm), lambda i: (i, 0)),
        out_shape=jax.ShapeDtypeStruct((T, dm), BF16),
        scratch_shapes=[pltpu.VMEM((MEM_HEADS, tm, M), F32), pltpu.VMEM((MEM_HEADS, tm, M), BF16)],
        compiler_params=_params("parallel"),
    )(qz, qz, kv, kv)


def _mem_attn_bwd(do, qz, kv, name):
    T = qz.shape[0]
    dm = qz.shape[1] // 2
    M = kv.shape[0]
    hd = dm // MEM_HEADS
    scale = hd ** -0.5
    tm = _pick(T, 512, SUBLANES)

    def body(do_ref, q_ref, z_ref, k_ref, v_ref, dq_ref, dz_ref, dk_ref, dv_ref, s_ref, dp_ref, p_ref, ds_ref, dob_ref):
        @pl.when(pl.program_id(0) == 0)
        def _():
            dk_ref[...] = jnp.zeros_like(dk_ref)
            dv_ref[...] = jnp.zeros_like(dv_ref)

        heads = [slice(h * hd, (h + 1) * hd) for h in range(MEM_HEADS)]
        for h, sl in enumerate(heads):
            sz, _ = _silu_and_grad(z_ref[:, sl].astype(F32))
            dob = (do_ref[:, sl] * sz).astype(BF16)
            dob_ref[:, sl] = dob
            s_ref[h] = _dot(q_ref[:, sl].astype(BF16), k_ref[:, sl], NT)
            dp_ref[h] = _dot(dob, v_ref[:, sl], NT)
        for h, sl in enumerate(heads):
            s = s_ref[h] * scale
            p = jnp.exp(s - jnp.max(s, axis=-1, keepdims=True))
            pn = p / jnp.sum(p, axis=-1, keepdims=True)
            dp = dp_ref[h]
            p_ref[h] = pn.astype(BF16)
            ds_ref[h] = (pn * (dp - jnp.sum(dp * pn, axis=-1, keepdims=True))).astype(BF16)
        for h, sl in enumerate(heads):
            _, dsz = _silu_and_grad(z_ref[:, sl].astype(F32))
            dz_ref[:, sl] = (do_ref[:, sl] * _dot(p_ref[h], v_ref[:, sl], NN) * dsz).astype(BF16)
            dq_ref[:, sl] = (_dot(ds_ref[h], k_ref[:, sl], NN) * scale).astype(BF16)
            dk_ref[:, sl] += _dot(ds_ref[h], q_ref[:, sl].astype(BF16), TN) * scale
            dv_ref[:, sl] += _dot(p_ref[h], dob_ref[:, sl], TN)

    rowq = pl.BlockSpec((tm, dm), lambda i: (i, 0))
    rowz = pl.BlockSpec((tm, dm), lambda i: (i, 1))
    kb = pl.BlockSpec((M, dm), lambda i: (0, 0))
    vb = pl.BlockSpec((M, dm), lambda i: (0, 1))
    sd = jax.ShapeDtypeStruct
    dq, dz, dk, dv = pl.pallas_call(
        body, name=name, grid=(T // tm,), in_specs=[rowq, rowq, rowz, kb, vb],
        out_specs=[rowq, rowq, kb, kb],
        out_shape=[sd((T, dm), BF16), sd((T, dm), BF16), sd((M, dm), F32), sd((M, dm), F32)],
        scratch_shapes=[pltpu.VMEM((MEM_HEADS, tm, M), F32), pltpu.VMEM((MEM_HEADS, tm, M), F32),
                        pltpu.VMEM((MEM_HEADS, tm, M), BF16), pltpu.VMEM((MEM_HEADS, tm, M), BF16), pltpu.VMEM((tm, dm), BF16)],
        compiler_params=_params("arbitrary"),
    )(do, qz, qz, kv, kv)
    return dq, dz, dk, dv


def _merge(os, ws, L, logits, b_gate, name):
    T = os[0].shape[0]
    D = ws[0].shape[2]
    tm = _pick(T, 512, SUBLANES)

    def body(o0_ref, o1_ref, o2_ref, w0_ref, w1_ref, w2_ref, l_ref, b_ref, m_ref, p0_ref, p1_ref, p2_ref):
        acc = 0.0
        for i, (o_ref, w_ref, p_ref) in enumerate(((o0_ref, w0_ref, p0_ref), (o1_ref, w1_ref, p1_ref), (o2_ref, w2_ref, p2_ref))):
            sl = slice(i * D, (i + 1) * D)
            bp = _dot(o_ref[...], w_ref[...], NN)
            p_ref[...] = bp.astype(BF16)
            acc = acc + _sigmoid(l_ref[:, sl].astype(F32) + b_ref[:, sl]) * bp
        m_ref[...] = acc.astype(BF16)

    row = pl.BlockSpec((tm, D), lambda i: (i, 0))
    return pl.pallas_call(
        body, name=name, grid=(T // tm,),
        in_specs=[pl.BlockSpec((tm, o.shape[1]), lambda i: (i, 0)) for o in os]
        + [pl.BlockSpec((None,) + w.shape[1:], lambda i: (L, 0, 0)) for w in ws]
        + [pl.BlockSpec((tm, 3 * D), lambda i: (i, 0)), pl.BlockSpec((1, 3 * D), lambda i: (0, 0))],
        out_specs=[row] * 4, out_shape=[jax.ShapeDtypeStruct((T, D), BF16)] * 4, compiler_params=_params("parallel"),
    )(*os, *ws, logits, b_gate.reshape(1, 3 * D))


def _merge_bwd(dmerged, bps, logits, b_gate, name):
    T, D = bps[0].shape
    tm = _pick(T, 512, SUBLANES)

    def body(dm_ref, p0_ref, p1_ref, p2_ref, l_ref, b_ref, d0_ref, d1_ref, d2_ref, dl_ref, db_ref):
        @pl.when(pl.program_id(0) == 0)
        def _():
            db_ref[...] = jnp.zeros_like(db_ref)

        dmv = dm_ref[...]
        for i, (p_ref, d_ref) in enumerate(((p0_ref, d0_ref), (p1_ref, d1_ref), (p2_ref, d2_ref))):
            sl = slice(i * D, (i + 1) * D)
            gt = _sigmoid(l_ref[:, sl].astype(F32) + b_ref[:, sl])
            d_ref[...] = (dmv * gt).astype(BF16)
            dl = dmv * p_ref[...].astype(F32) * gt * (1.0 - gt)
            dl_ref[:, sl] = dl.astype(BF16)
            db_ref[:, sl] += jnp.sum(dl, axis=0, keepdims=True)

    row = pl.BlockSpec((tm, D), lambda i: (i, 0))
    wide = pl.BlockSpec((tm, 3 * D), lambda i: (i, 0))
    vec = pl.BlockSpec((1, 3 * D), lambda i: (0, 0))
    sd = jax.ShapeDtypeStruct
    return pl.pallas_call(
        body, name=name, grid=(T // tm,), in_specs=[row, row, row, row, wide, vec],
        out_specs=[row, row, row, wide, vec],
        out_shape=[sd((T, D), BF16)] * 3 + [sd((T, 3 * D), BF16), sd((1, 3 * D), F32)],
        compiler_params=_params("arbitrary"),
    )(dmerged, *bps, logits, b_gate.reshape(1, 3 * D))


def _to_segments(a):
    T, C = a.shape
    return a.reshape(SSM_SEGMENTS, T // SSM_SEGMENTS, C).transpose(1, 0, 2).reshape(T, C)


def _from_segments(a):
    T, C = a.shape
    return a.reshape(T // SSM_SEGMENTS, SSM_SEGMENTS, C).transpose(1, 0, 2).reshape(T, C)


def _to_residues(a):
    T = a.shape[0]
    gw = HEADS_PER_GROUP * ATTN_HEAD_DIM
    out = []
    for g, (_, r) in enumerate(ATTN_CONFIGS):
        ag = a[:, g * gw:(g + 1) * gw].reshape(T // r, r, gw)
        out.append(ag.transpose(1, 0, 2).reshape(T, gw))
    return jnp.stack(out)


def _from_residues(a):
    _, T, gw = a.shape
    out = []
    for g, (_, r) in enumerate(ATTN_CONFIGS):
        out.append(a[g].reshape(r, T // r, gw).transpose(1, 0, 2).reshape(T, gw))
    return jnp.concatenate(out, axis=1)


def _block_diag(w):
    nblk, ng, a, b = w.shape
    eye = jnp.eye(ng, dtype=w.dtype)
    return (w[:, :, :, None, :] * eye[None, :, None, :, None]).reshape(nblk, ng * a, ng * b)


def _block_diag_part(m, a, b):
    nblk = m.shape[0]
    ng = m.shape[1] // a
    m5 = m.reshape(nblk, ng, a, ng, b)
    eye = jnp.eye(ng, dtype=m.dtype)
    return jnp.sum(m5 * eye[None, :, None, :, None], axis=3)


def _ssm_matrices(p, L, tag):
    G, P = p["ssm_lambda_re"].shape[1:]
    Hg = SSM_GROUP
    gpb = SSM_BLOCK_CH // Hg
    nblk = G // gpb
    br = p["ssm_b_re"][L].transpose(2, 0, 1)
    bi = p["ssm_b_im"][L].transpose(2, 0, 1)
    disc_in = (p["ssm_lambda_re"][L], p["ssm_lambda_im"][L], p["ssm_log_dt"][L].reshape(G, 1), br, bi)
    ar, ai, bbr, bbi = _ssm_disc(*disc_in, name=f"ssm_disc_{tag}")
    amat = (ar.reshape(nblk // 2, SUBLANES, LANES), ai.reshape(nblk // 2, SUBLANES, LANES))
    bbr_g = bbr.transpose(1, 0, 2).reshape(nblk, gpb, Hg, P)
    bbi_g = bbi.transpose(1, 0, 2).reshape(nblk, gpb, Hg, P)
    bmat = jnp.concatenate([_block_diag(bbr_g), _block_diag(bbi_g)], axis=2).astype(BF16)
    cre = p["ssm_c_re"][L].reshape(nblk, gpb, Hg, P).transpose(0, 1, 3, 2)
    cim = p["ssm_c_im"][L].reshape(nblk, gpb, Hg, P).transpose(0, 1, 3, 2)
    cmat = jnp.concatenate([_block_diag(cre), -_block_diag(cim)], axis=1).astype(BF16)
    return disc_in, amat, bmat, cmat


def _layer_fwd(x, mem, p, wb, L, biasm):
    T, D = x.shape
    C = p["ssm_d"].shape[1]
    dm = wb["w_br_mem"].shape[1]
    tag = f"l{L}"
    s = {"x": x}
    h = _rmsnorm(x, p["norm_g"][L], f"norm_{tag}")
    offs = [int(o) for o in np.cumsum([0, C, C, 3 * 768, 768, 2 * dm, 3 * D])]
    names = ("uz", "z_ssm", "qkv", "z_attn", "qz_mem", "logits")
    dts = (F32, BF16, F32, BF16, BF16, BF16)
    for i, (nm, dt) in enumerate(zip(names, dts)):
        s[nm] = _matmul(h, wb["w_in"], mode="nn", name=f"in_{nm}_{tag}", out_dtype=dt, b_lead=L, b_off=offs[i],
                        n_cols=offs[i + 1] - offs[i])
    s["h"] = h

    disc_in, amat, bmat, cmat = _ssm_matrices(p, L, tag)
    dvec = p["ssm_d"][L].reshape(1, C)
    y, xr, xi = _ssm_fwd(s["uz"], bmat, cmat, *amat, dvec, f"ssm_scan_{tag}")
    o_ssm, a_glu = _ssm_post(y, s["z_ssm"], wb["w_glu"][L], p["b_glu"][L], f"ssm_post_{tag}")
    s.update(disc_in=disc_in, amat=amat, bmat=bmat, cmat=cmat, xr=xr, xi=xi, y=y, a_glu=a_glu, o_ssm=o_ssm)

    groups = [_attn_group_fwd(s["qkv"], biasm, g, f"attn_g{g}_{tag}") for g in range(len(ATTN_CONFIGS))]
    os, lses = [o for o, _ in groups], [l for _, l in groups]
    o_attn = _attn_mix(os, lses, s["z_attn"], f"attn_mix_{tag}")
    s.update(os=os, lses=lses, o_attn=o_attn)

    mn = _rmsnorm(mem, p["mem_norm_g"][L], f"mem_norm_{tag}")
    kv = _matmul(mn, wb["w_mem_kv"], mode="nn", name=f"mem_kv_{tag}", out_dtype=BF16, b_lead=L)
    o_mem = _mem_attn(s["qz_mem"], kv, f"mem_attn_{tag}")
    s.update(mn=mn, kv=kv, o_mem=o_mem)

    merged, *bps = _merge([o_ssm, o_attn, o_mem], [wb["w_br_ssm"], wb["w_br_attn"], wb["w_br_mem"]], L, s["logits"],
                          p["b_gate"][L], f"merge_{tag}")
    s.update(bps=bps, merged=merged)
    x_new = _matmul(merged, wb["w_out"], mode="nn", name=f"out_{tag}", add=x, b_lead=L)
    return x_new, s


def _layer_bwd(dx, mem, p, wb, L, s, biasm, gprev):
    T, D = dx.shape
    C = p["ssm_d"].shape[1]
    depth = p["norm_g"].shape[0]
    tag = f"l{L}"
    g = {}

    def wgrad(n, a, b, **tiles):
        g[n] = _matmul(a, b, mode="tn", name=f"d{n}_{tag}", out_dtype=BF16, stack=(L, depth, gprev.get(n)), **tiles)

    dmerged = _matmul(dx, wb["w_out"], mode="nt", name=f"d_merged_{tag}", b_lead=L)
    wgrad("w_out", s["merged"], dx)
    dbp0, dbp1, dbp2, dlogits, g["b_gate"] = _merge_bwd(dmerged, s["bps"], s["logits"], p["b_gate"][L], f"merge_bwd_{tag}")
    dos = []
    for dbp, o, n in ((dbp0, s["o_ssm"], "w_br_ssm"), (dbp1, s["o_attn"], "w_br_attn"), (dbp2, s["o_mem"], "w_br_mem")):
        dos.append(_matmul(dbp, wb[n], mode="nt", name=f"d_o_{n}_{tag}", b_lead=L))
        wgrad(n, o, dbp)

    dy, dz_ssm, ds_glu, g["b_glu"] = _ssm_post_bwd(dos[0], s["y"], s["z_ssm"], wb["w_glu"][L], p["b_glu"][L], f"ssm_post_bwd_{tag}")
    wgrad("w_glu", s["a_glu"], ds_glu)
    dvec = p["ssm_d"][L].reshape(1, C)
    du, dbm, dct, dar, dai, g["ssm_d"] = _ssm_bwd(dy, s["uz"], s["xr"], s["xi"], s["bmat"], s["cmat"], *s["amat"], dvec,
                                                  f"ssm_scan_bwd_{tag}")
    G, P = p["ssm_lambda_re"].shape[1:]
    Hg = SSM_GROUP
    half = dbm.shape[2] // 2
    dbbr = _block_diag_part(dbm[:, :, :half], Hg, P).reshape(G, Hg, P).transpose(1, 0, 2)
    dbbi = _block_diag_part(dbm[:, :, half:], Hg, P).reshape(G, Hg, P).transpose(1, 0, 2)
    g["ssm_c_re"] = _block_diag_part(dct[:, :, :half], Hg, P).reshape(G, Hg, P)
    g["ssm_c_im"] = -_block_diag_part(dct[:, :, half:], Hg, P).reshape(G, Hg, P)
    glre, glim, gdt, gbr, gbi = _ssm_disc_bwd(*s["disc_in"], dar.reshape(G, P), dai.reshape(G, P), dbbr, dbbi,
                                              name=f"ssm_disc_bwd_{tag}")
    g["ssm_lambda_re"], g["ssm_lambda_im"], g["ssm_log_dt"] = glre, glim, gdt.reshape(G)
    g["ssm_b_re"] = gbr.transpose(1, 2, 0)
    g["ssm_b_im"] = gbi.transpose(1, 2, 0)

    dz_attn, do_g, dvec_g = _attn_mix_bwd(dos[1], s["os"], s["lses"], s["z_attn"], f"attn_mix_bwd_{tag}")
    back = [_attn_group_bwd(s["qkv"], do_g[g], dvec_g[g], s["lses"][g], biasm, g, f"attn_bwd_g{g}_{tag}")
            for g in range(len(ATTN_CONFIGS))]
    dqkv = [b[i].astype(BF16) for i in range(3) for b in back]
    dbias = jnp.stack([b[3] for b in back])

    dq_mem, dz_mem, dk_mem, dv_mem = _mem_attn_bwd(dos[2], s["qz_mem"], s["kv"], f"mem_attn_bwd_{tag}")
    dkv = jnp.concatenate([dk_mem, dv_mem], axis=1)
    wgrad("w_mem_kv", s["mn"], dkv)
    dmn = _matmul(dkv, wb["w_mem_kv"], mode="nt", name=f"d_mn_{tag}", b_lead=L)
    _, g["mem_norm_g"] = _rmsnorm_bwd(mem, p["mem_norm_g"][L], dmn, None, f"mem_norm_bwd_{tag}")

    dproj = jnp.concatenate([du.astype(BF16), dz_ssm] + dqkv + [dz_attn, dq_mem, dz_mem, dlogits], axis=1)
    dh = _matmul(dproj, wb["w_in"], mode="nt", name=f"d_h_{tag}", b_lead=L)
    wgrad("w_in", s["h"], dproj, tn=2304, tk=1024)
    dx_in, g["norm_g"] = _rmsnorm_bwd(s["x"], p["norm_g"][L], dh, dx, f"norm_bwd_{tag}")
    return dx_in, g, dbias


def _bucket_onehot(gi):
    buckets, bands = _band_tables()
    hit = (buckets[gi].reshape(1, -1) == jnp.arange(NUM_BUCKETS)[:, None]) & bands[gi].reshape(1, -1)
    return hit.astype(BF16)


def _bias_tables(rel_bias, name):
    _, bands = _band_tables()
    out = []
    for gi in range(len(ATTN_CONFIGS)):
        tab = rel_bias[:, gi * HEADS_PER_GROUP:(gi + 1) * HEADS_PER_GROUP].T
        flat = _matmul(tab, _bucket_onehot(gi), mode="nn", name=f"{name}_{gi}", split_a=3, tn=4096)
        out.append(jnp.where(bands[gi][None], flat.reshape(HEADS_PER_GROUP, ATTN_BLOCK, 2 * ATTN_BLOCK), NEG_INF))
    return jnp.stack(out)


def _rel_bias_grad(dbias_sum, name):
    cols = []
    for gi in range(len(ATTN_CONFIGS)):
        flat = dbias_sum[gi].reshape(HEADS_PER_GROUP, -1)
        cols.append(_matmul(flat, _bucket_onehot(gi), mode="nt", name=f"{name}_{gi}", split_a=2, tk=4096).T)
    return jnp.concatenate(cols, axis=1)


def _local_step(x, mem, target, p, wb):
    depth = p["norm_g"].shape[0]
    biasm = _bias_tables(p["rel_bias"], "bias_table")
    saved = []
    for L in range(depth):
        x, s = _layer_fwd(x, mem, p, wb, L, biasm)
        saved.append(s)
    loss_vec, dx, dgf = _loss_head(x, p["final_norm_g"], target, "loss_head")
    grads = {"final_norm_g": dgf.reshape(-1)}
    per_layer = [None] * depth
    dbias_sum = 0.0
    stacked = {}
    for L in reversed(range(depth)):
        dx, per_layer[L], dbias = _layer_bwd(dx, mem, p, wb, L, saved[L], biasm, stacked)
        stacked = {n: per_layer[L][n] for n, _ in BIG}
        dbias_sum = dbias_sum + dbias
    grads.update(stacked)
    for n in per_layer[0]:
        if n not in stacked:
            grads[n] = jnp.stack([per_layer[L][n].reshape(p[n].shape[1:]) for L in range(depth)])
    grads["rel_bias"] = _rel_bias_grad(dbias_sum, "d_rel_bias")
    return jnp.sum(loss_vec), dx, grads


def _chip_coords(j):
    return j // 2, j % 2


def _place_shard(shard, ax, chip, name):
    _, a, b = shard.shape
    ra = _pick(a, 256, 16)
    full = (2, a * N_CHIPS, b) if ax == 1 else (2, a, b * N_CHIPS)
    per = a // ra

    def body(j_ref, s_ref, o_ref):
        o_ref[...] = s_ref[...].astype(BF16)

    out_idx = (lambda l, i, j: (l, j[0] * per + i, 0)) if ax == 1 else (lambda l, i, j: (l, i, j[0]))
    return pl.pallas_call(
        body, name=name,
        grid_spec=pltpu.PrefetchScalarGridSpec(
            num_scalar_prefetch=1, grid=(2, per),
            in_specs=[pl.BlockSpec((None, ra, b), lambda l, i, j: (l, i, 0))],
            out_specs=pl.BlockSpec((None, ra, b), out_idx)),
        out_shape=jax.ShapeDtypeStruct(full, BF16), compiler_params=_params("parallel", "parallel"),
    )(chip, shard)


def _gather_shards(fulls, axes, name):
    n = len(fulls)
    widths = [a.shape[ax] // N_CHIPS for a, ax in zip(fulls, axes)]
    aligns = [LANES if ax == 2 else 16 for ax in axes]

    def body(*refs):
        outs = refs[n:2 * n]
        send_sems, recv_sems, fsend_sems, frecv_sems = refs[2 * n:]
        x, y, c = lax.axis_index("x"), lax.axis_index("y"), lax.axis_index("c")
        mine = 2 * x + y
        sibling = (x, y, 1 - c)

        def window(t, layer, j):
            start = pl.ds(pl.multiple_of(j * widths[t], aligns[t]), widths[t])
            return outs[t].at[(layer, start, slice(None)) if axes[t] == 1 else (layer, slice(None), start)]

        def over_ici(t, j, block):
            return pltpu.make_async_remote_copy(
                src_ref=window(t, c, mine), dst_ref=window(t, c, block), send_sem=send_sems.at[t, j],
                recv_sem=recv_sems.at[t, block], device_id=(*_chip_coords(j), c), device_id_type=MESH)

        def over_d2d(t, j, layer):
            return pltpu.make_async_remote_copy(
                src_ref=window(t, layer, j), dst_ref=window(t, layer, j), send_sem=fsend_sems.at[t, j],
                recv_sem=frecv_sems.at[t, j], device_id=sibling, device_id_type=MESH)

        for t in range(n):
            for j in range(N_CHIPS):
                @pl.when(j != mine)
                def _():
                    over_ici(t, j, mine).start()
        for t in range(n):
            for j in range(N_CHIPS):
                @pl.when(j != mine)
                def _():
                    over_ici(t, j, j).wait_recv()
                    over_d2d(t, j, c).start()
        for t in range(n):
            for j in range(N_CHIPS):
                @pl.when(j != mine)
                def _():
                    over_ici(t, j, mine).wait_send()
                    over_d2d(t, j, c).wait_send()
                    over_d2d(t, j, 1 - c).wait_recv()

    sem = pltpu.SemaphoreType.DMA
    return pl.pallas_call(
        body, name=name, in_specs=[HBM] * n, out_specs=[HBM] * n,
        out_shape=[jax.ShapeDtypeStruct(a.shape, a.dtype) for a in fulls],
        input_output_aliases={t: t for t in range(n)},
        scratch_shapes=[sem((n, N_CHIPS)), sem((n, N_CHIPS)), sem((n, N_CHIPS)), sem((n, N_CHIPS))],
    )(*fulls)


def _scatter_slices(arrays, axes, name):
    n = len(arrays)

    def piece(a, ax):
        if ax is None:
            return a.shape, None
        w = a.shape[ax] // N_CHIPS
        return a.shape[:ax] + (w,) + a.shape[ax + 1:], w

    shapes = [piece(a, ax) for a, ax in zip(arrays, axes)]

    def body(*refs):
        ins, outs = refs[:n], refs[n:2 * n]
        send_sems, recv_sems, loc_sems = refs[2 * n:]
        x, y, c = lax.axis_index("x"), lax.axis_index("y"), lax.axis_index("c")
        mine = 2 * x + y

        def src(t, j):
            ax, w = axes[t], shapes[t][1]
            if ax is None:
                return ins[t]
            idx = tuple(pl.ds(j * w, w) if d == ax else slice(None) for d in range(len(arrays[t].shape)))
            return ins[t].at[idx]

        for t in range(n):
            for j in range(N_CHIPS):
                @pl.when(j == mine)
                def _():
                    pltpu.make_async_copy(src(t, j), outs[t].at[j], loc_sems.at[t]).start()

                @pl.when(j != mine)
                def _():
                    pltpu.make_async_remote_copy(
                        src_ref=src(t, j), dst_ref=outs[t].at[mine], send_sem=send_sems.at[t, j], recv_sem=recv_sems.at[t, mine],
                        device_id=(*_chip_coords(j), c), device_id_type=MESH).start()
        for t in range(n):
            for j in range(N_CHIPS):
                @pl.when(j == mine)
                def _():
                    pltpu.make_async_copy(src(t, j), outs[t].at[j], loc_sems.at[t]).wait()

                @pl.when(j != mine)
                def _():
                    cp = pltpu.make_async_remote_copy(
                        src_ref=src(t, j), dst_ref=outs[t].at[j], send_sem=send_sems.at[t, j], recv_sem=recv_sems.at[t, j],
                        device_id=(*_chip_coords(j), c), device_id_type=MESH)
                    cp.wait_send()
                    cp.wait_recv()

    return pl.pallas_call(
        body, name=name, in_specs=[HBM] * n, out_specs=[HBM] * n,
        out_shape=[jax.ShapeDtypeStruct((N_CHIPS,) + sh, a.dtype) for a, (sh, _) in zip(arrays, shapes)],
        scratch_shapes=[pltpu.SemaphoreType.DMA((n, N_CHIPS)), pltpu.SemaphoreType.DMA((n, N_CHIPS)), pltpu.SemaphoreType.DMA((n,))],
    )(*arrays)


def _swap_layers(stacked, name):
    n = len(stacked)

    def body(*refs):
        ins, outs = refs[:n], refs[n:2 * n]
        send_sems, recv_sems = refs[2 * n:]
        c = lax.axis_index("c")
        peer = (lax.axis_index("x"), lax.axis_index("y"), 1 - c)
        cps = [pltpu.make_async_remote_copy(src_ref=ins[t].at[1 - c], dst_ref=outs[t], send_sem=send_sems.at[t],
                                            recv_sem=recv_sems.at[t], device_id=peer, device_id_type=MESH) for t in range(n)]
        for cp in cps:
            cp.start()
        for cp in cps:
            cp.wait_send()
            cp.wait_recv()

    return pl.pallas_call(
        body, name=name, in_specs=[HBM] * n, out_specs=[HBM] * n,
        out_shape=[jax.ShapeDtypeStruct(a.shape[1:], a.dtype) for a in stacked],
        scratch_shapes=[pltpu.SemaphoreType.DMA((n,)), pltpu.SemaphoreType.DMA((n,))],
    )(*stacked)


def _merge_layers(stacked, name):
    n = len(stacked)

    def body(*refs):
        outs = refs[n:2 * n]
        send_sems, recv_sems = refs[2 * n:]
        c = lax.axis_index("c")
        peer = (lax.axis_index("x"), lax.axis_index("y"), 1 - c)
        for t in range(n):
            pltpu.make_async_remote_copy(src_ref=outs[t].at[c], dst_ref=outs[t].at[c], send_sem=send_sems.at[t],
                                         recv_sem=recv_sems.at[t], device_id=peer, device_id_type=MESH).start()
        for t in range(n):
            cp = pltpu.make_async_remote_copy(src_ref=outs[t].at[c], dst_ref=outs[t].at[1 - c], send_sem=send_sems.at[t],
                                              recv_sem=recv_sems.at[t], device_id=peer, device_id_type=MESH)
            cp.wait_send()
            cp.wait_recv()

    sem = pltpu.SemaphoreType.DMA
    return pl.pallas_call(
        body, name=name, in_specs=[HBM] * n, out_specs=[HBM] * n,
        out_shape=[jax.ShapeDtypeStruct(a.shape, a.dtype) for a in stacked],
        input_output_aliases={t: t for t in range(n)}, scratch_shapes=[sem((n,)), sem((n,))],
    )(*stacked)


def _pair_sum(stacked, landed, core, name):
    _, K, N = stacked.shape
    tr = _pick(K, max(16, (1 << 19) // N // 16 * 16), 16)

    def body(c_ref, s_ref, l_ref, o_ref):
        o_ref[...] = (s_ref[...].astype(F32) + l_ref[...].astype(F32)).astype(o_ref.dtype)

    return pl.pallas_call(
        body, name=name,
        grid_spec=pltpu.PrefetchScalarGridSpec(
            num_scalar_prefetch=1, grid=(K // tr,),
            in_specs=[pl.BlockSpec((None, tr, N), lambda i, c: (c[0], i, 0)), pl.BlockSpec((tr, N), lambda i, c: (i, 0))],
            out_specs=pl.BlockSpec((tr, N), lambda i, c: (i, 0))),
        out_shape=jax.ShapeDtypeStruct((K, N), stacked.dtype), compiler_params=_params("parallel"),
    )(core, stacked, landed)


def _sum_chips(landed, core, name):
    _, R, C = landed.shape
    tr = _pick(R, max(SUBLANES, (1 << 19) // C // 16 * 16), 16)

    def body(c_ref, l_ref, o_ref):
        acc = l_ref[0].astype(F32) + l_ref[1].astype(F32)
        acc = acc + l_ref[2].astype(F32)
        o_ref[...] = acc + l_ref[3].astype(F32)

    return pl.pallas_call(
        body, name=name,
        grid_spec=pltpu.PrefetchScalarGridSpec(
            num_scalar_prefetch=1, grid=(R // tr,),
            in_specs=[pl.BlockSpec((N_CHIPS, tr, C), lambda i, c: (0, i, 0))],
            out_specs=pl.BlockSpec((None, tr, C), lambda i, c: (c[0], i, 0))),
        out_shape=jax.ShapeDtypeStruct((2, R, C), F32), compiler_params=_params("parallel"),
    )(core, landed)


def _adamw_math(w_ref, g_ref, m_ref, v_ref, d_ref, nm_ref, nv_ref):
    c1 = 1.0 / (1.0 - ADAM_B1 ** ADAM_STEP)
    c2 = 1.0 / (1.0 - ADAM_B2 ** ADAM_STEP)
    g = g_ref[...]
    nm = ADAM_B1 * m_ref[...] + (1.0 - ADAM_B1) * g
    nv = ADAM_B2 * v_ref[...] + (1.0 - ADAM_B2) * (g * g)
    nm_ref[...] = nm
    nv_ref[...] = nv
    d_ref[...] = -ADAM_LR * ((nm * c1) / (jnp.sqrt(nv * c2) + ADAM_EPS) + ADAM_WD * w_ref[...])


def _adamw_whole(w, g, m, v, name):
    shape = w.shape
    view = (-1,) + shape[-2:] if w.ndim >= 2 else (1, 1, -1)

    def body(*refs):
        _adamw_math(*refs)

    res = pl.pallas_call(body, name=name, out_shape=[jax.ShapeDtypeStruct(w.reshape(view).shape, F32)] * 3,
                         compiler_params=pltpu.CompilerParams(vmem_limit_bytes=VMEM_LIMIT_BYTES))(
        *(a.reshape(view) for a in (w, g, m, v)))
    return [r.reshape(shape) for r in res]


def _adamw(w, g, m, v, name):
    R, C = w.shape
    tr = _pick(R, max(SUBLANES, (1 << 18) // C // 8 * 8), SUBLANES)

    def body(*refs):
        _adamw_math(*refs)

    blk = pl.BlockSpec((tr, C), lambda i: (i, 0))
    return pl.pallas_call(
        body, name=name, grid=(R // tr,), in_specs=[blk] * 4, out_specs=[blk] * 3,
        out_shape=[jax.ShapeDtypeStruct((R, C), F32)] * 3, compiler_params=_params("parallel"),
    )(w, g, m, v)


def _pack_small(d, prefix=""):
    flat = jnp.concatenate([d[prefix + n].astype(F32).reshape(-1) for n in SMALL])
    pad = (-flat.shape[0]) % (2 * 16 * LANES)
    return jnp.pad(flat, (0, pad)).reshape(-1, LANES)


def _unpack_small(packed, shapes):
    flat = packed.reshape(-1)
    out, off = {}, 0
    for n in SMALL:
        size = int(np.prod(shapes[n]))
        out[n] = flat[off:off + size].reshape(shapes[n])
        off += size
    return out


def kernel(*args):
    p = dict(zip(INPUTS, args))
    x, mem, target = p["x"][0], p["mem"][0], p["loss_target"][0]

    names = [n for n, _ in BIG] + ["small"]
    core = lax.axis_index("c").astype(jnp.int32).reshape(1)
    chip = (2 * lax.axis_index("x") + lax.axis_index("y")).astype(jnp.int32).reshape(1)
    placed = [_place_shard(p[n], ax, chip, f"place_{n}") for n, ax in BIG]
    wb = dict(zip(names, _gather_shards(placed, [ax for _, ax in BIG], "gather_weights")))

    loss_part, dx, grads = _local_step(x, mem, target, p, wb)
    loss = lax.psum(loss_part, ("x", "y", "c"))

    stacked = [grads[n] for n, _ in BIG] + [_pack_small(grads).reshape(2, -1, LANES)]
    theirs = _swap_layers(stacked, "swap_layers")
    pair = [_pair_sum(s, o, core, f"pair_sum_{n}") for n, s, o in zip(names, stacked, theirs)]
    landed = _scatter_slices(pair, [ax - 1 for _, ax in BIG] + [None], "scatter_grads")
    reduced = [_sum_chips(ld.reshape(N_CHIPS, -1, ld.shape[-1]), core, f"sum_chips_{n}") for n, ld in zip(names, landed)]
    total = _merge_layers(reduced, "merge_layers")

    out = {}
    for (n, _), g in zip(BIG, total):
        sh = p[n].shape
        two_d = lambda a: a.reshape(-1, sh[-1])
        res = (g,) + tuple(_adamw(two_d(p[n]), two_d(g), two_d(p["m_" + n]), two_d(p["v_" + n]), f"adamw_{n}"))
        for key, r in zip(("grad_", "delta_", "new_m_", "new_v_"), res):
            out[key + n] = r.reshape(sh)
    for n, g in _unpack_small(total[-1], {n: p[n].shape for n in SMALL}).items():
        res = (g,) + tuple(_adamw_whole(p[n], g, p["m_" + n], p["v_" + n], f"adamw_{n}"))
        for key, r in zip(("grad_", "delta_", "new_m_", "new_v_"), res):
            out[key + n] = r

    result = [loss, dx.reshape(p["x"].shape)]
    for key in ("grad_", "delta_", "new_m_", "new_v_"):
        result += [out[key + n] for n in WEIGHTS]
    return tuple(result)
```

```python
import math

import jax
import jax.numpy as jnp
import numpy as np
from jax import lax
from jax.experimental import pallas as pl
from jax.experimental.pallas import tpu as pltpu

F32 = jnp.float32
BF16 = jnp.bfloat16
MESH = pl.DeviceIdType.MESH
HBM = pl.BlockSpec(memory_space=pltpu.HBM)

EPS = 1e-6
SSM_GROUP = 16
SSM_STATE = 64
ATTN_HEAD_DIM = 64
HEADS_PER_GROUP = 4
ATTN_CONFIGS = ((128, 1), (512, 4), (2048, 16))
ATTN_BLOCK = 128
NUM_BUCKETS = 32
REL_MAX_DISTANCE = 2048
NEG_INF = -1e30
MEM_HEADS = 4
ADAM_LR = 0.001
ADAM_B1 = 0.9
ADAM_B2 = 0.999
ADAM_EPS = 1e-08
ADAM_WD = 0.01
ADAM_STEP = 10

LANES = 128
SUBLANES = 8
VMEM_LIMIT_BYTES = 48 * 1024 * 1024
SSM_BLOCK_CH = 128
SSM_SEGMENTS = SUBLANES
SSM_CHUNK_STEPS = 128

N_CHIPS = 4
BIG = (("w_in", 2), ("w_glu", 1), ("w_mem_kv", 1), ("w_br_ssm", 2), ("w_br_attn", 2), ("w_br_mem", 2), ("w_out", 1))
SMALL = ("norm_g", "mem_norm_g", "b_gate", "ssm_lambda_re", "ssm_lambda_im", "ssm_log_dt", "ssm_b_re", "ssm_b_im",
         "ssm_c_re", "ssm_c_im", "ssm_d", "b_glu", "rel_bias", "final_norm_g")
WEIGHTS = ("norm_g", "mem_norm_g", "w_in", "b_gate", "ssm_lambda_re", "ssm_lambda_im", "ssm_log_dt", "ssm_b_re",
           "ssm_b_im", "ssm_c_re", "ssm_c_im", "ssm_d", "w_glu", "b_glu", "w_mem_kv", "w_br_ssm", "w_br_attn",
           "w_br_mem", "w_out", "rel_bias", "final_norm_g")
INPUTS = ("x", "mem") + WEIGHTS + ("loss_target",) + tuple("m_" + n for n in WEIGHTS) + tuple("v_" + n for n in WEIGHTS)


def _params(*sem):
    return pltpu.CompilerParams(dimension_semantics=sem, vmem_limit_bytes=VMEM_LIMIT_BYTES)


def _pick(dim, pref, align):
    if dim <= pref:
        return dim
    t = pref - pref % align
    while t >= align:
        if dim % t == 0:
            return t
        t -= align
    return dim


def _sigmoid(v):
    return 1.0 / (1.0 + jnp.exp(-v))


def _silu_and_grad(z):
    s = _sigmoid(z)
    return z * s, s * (1.0 + z * (1.0 - s))


_GELU_C = math.sqrt(2.0 / math.pi)


def _gelu_and_grad(y):
    inner = _GELU_C * (y + 0.044715 * y * y * y)
    t = jnp.tanh(inner)
    g = 0.5 * y * (1.0 + t)
    dg = 0.5 * (1.0 + t) + 0.5 * y * (1.0 - t * t) * _GELU_C * (1.0 + 3.0 * 0.044715 * y * y)
    return g, dg


def _dot(a, b, dims):
    return lax.dot_general(a, b, (dims, ((), ())), preferred_element_type=F32)


NN = ((1,), (0,))
NT = ((1,), (1,))
TN = ((0,), (0,))


def _matmul(a, b, *, mode, name, out_dtype=F32, add=None, split_a=1, tm=1024, tn=768, tk=2304,
            a_lead=None, b_lead=None, b_off=0, n_cols=None, stack=None):
    ashape = a.shape if a_lead is None else a.shape[1:]
    K, M = ashape if mode == "tn" else ashape[::-1]
    bshape = b.shape if b_lead is None else b.shape[1:]
    N = n_cols or (bshape[0] if mode == "nt" else bshape[1])
    if mode != "tn" and M >= 4 * tm:
        tm = 2 * tm
    tm = _pick(M, tm, LANES if mode == "tn" else SUBLANES)
    tn = _pick(math.gcd(N, b_off) if b_off else N, tn, LANES)
    tk = _pick(K, tk, LANES)
    nk = K // tk
    joff = b_off // tn
    dims = {"nn": NN, "nt": NT, "tn": TN}[mode]
    has_add = add is not None
    has_prev = stack is not None and stack[2] is not None

    def body(*refs):
        a_ref, b_ref = refs[:2]
        add_ref = refs[2] if has_add else None
        o_ref = refs[-2] if nk > 1 else refs[-1]
        k = pl.program_id(2)
        bv = b_ref[...].astype(BF16)
        if split_a > 1:
            rest = a_ref[...].astype(F32)
            part = 0.0
            for _ in range(split_a):
                piece = rest.astype(BF16)
                part = part + _dot(piece, bv, dims)
                rest = rest - piece.astype(F32)
        else:
            part = _dot(a_ref[...].astype(BF16), bv, dims)

        def finish(r):
            if has_add:
                r = r + add_ref[...]
            o_ref[...] = r.astype(out_dtype)

        if nk == 1:
            finish(part)
            return
        acc_ref = refs[-1]

        @pl.when(k == 0)
        def _():
            acc_ref[...] = part

        @pl.when((k > 0) & (k < nk - 1))
        def _():
            acc_ref[...] += part

        @pl.when(k == nk - 1)
        def _():
            finish(acc_ref[...] + part)

    alead = () if a_lead is None else (a_lead,)
    alead_blk = () if a_lead is None else (None,)
    if mode == "tn":
        a_spec = pl.BlockSpec(alead_blk + (tk, tm), lambda i, j, k: alead + (k, i))
    else:
        a_spec = pl.BlockSpec(alead_blk + (tm, tk), lambda i, j, k: alead + (i, k))
    lead = () if b_lead is None else (b_lead,)
    lead_blk = () if b_lead is None else (None,)
    if mode == "nt":
        b_spec = pl.BlockSpec(lead_blk + (tn, tk), lambda i, j, k: lead + (j + joff, k))
    else:
        b_spec = pl.BlockSpec(lead_blk + (tk, tn), lambda i, j, k: lead + (k, j + joff))
    in_specs = [a_spec, b_spec]
    args = [a, b]
    if has_add:
        in_specs.append(pl.BlockSpec((tm, tn), lambda i, j, k: (i, j)))
        args.append(add)
    aliases = {}
    if stack is None:
        out_spec = pl.BlockSpec((tm, tn), lambda i, j, k: (i, j))
        out_shape = jax.ShapeDtypeStruct((M, N), out_dtype)
    else:
        layer, depth, prev = stack
        out_spec = pl.BlockSpec((None, tm, tn), lambda i, j, k: (layer, i, j))
        out_shape = jax.ShapeDtypeStruct((depth, M, N), out_dtype)
        if has_prev:
            in_specs.append(pl.BlockSpec(memory_space=pl.ANY))
            args.append(prev)
            aliases = {len(args) - 1: 0}
    return pl.pallas_call(
        body, name=name, grid=(M // tm, N // tn, nk), in_specs=in_specs, out_specs=out_spec, out_shape=out_shape,
        scratch_shapes=[pltpu.VMEM((tm, tn), F32)] if nk > 1 else [], input_output_aliases=aliases,
        compiler_params=_params("parallel", "parallel", "arbitrary"),
    )(*args)


def _rmsnorm(x, g, name):
    T, D = x.shape
    tm = _pick(T, 512, SUBLANES)

    def body(x_ref, g_ref, h_ref):
        xv = x_ref[...]
        r = lax.rsqrt(jnp.mean(xv * xv, axis=-1, keepdims=True) + EPS)
        h_ref[...] = (xv * r * g_ref[...]).astype(BF16)

    return pl.pallas_call(
        body, name=name, grid=(T // tm,),
        in_specs=[pl.BlockSpec((tm, D), lambda i: (i, 0)), pl.BlockSpec((1, D), lambda i: (0, 0))],
        out_specs=pl.BlockSpec((tm, D), lambda i: (i, 0)),
        out_shape=jax.ShapeDtypeStruct((T, D), BF16), compiler_params=_params("parallel"),
    )(x, g.reshape(1, D))


def _rmsnorm_bwd(x, g, dh, dres, name):
    T, D = x.shape
    tm = _pick(T, 512, SUBLANES)
    with_res = dres is not None

    def body(*refs):
        if with_res:
            x_ref, g_ref, dh_ref, dres_ref, dx_ref, dg_ref = refs
        else:
            x_ref, g_ref, dh_ref, dx_ref, dg_ref = refs
        xv = x_ref[...]
        dhv = dh_ref[...]
        r = lax.rsqrt(jnp.mean(xv * xv, axis=-1, keepdims=True) + EPS)
        dyg = dhv * g_ref[...]
        c = jnp.mean(dyg * xv, axis=-1, keepdims=True)
        dx = r * dyg - xv * (r * r * r) * c
        if with_res:
            dx = dx + dres_ref[...]
        dx_ref[...] = dx

        @pl.when(pl.program_id(0) == 0)
        def _():
            dg_ref[...] = jnp.zeros_like(dg_ref)

        dg_ref[...] += jnp.sum(dhv * xv * r, axis=0, keepdims=True)

    row = pl.BlockSpec((tm, D), lambda i: (i, 0))
    vec = pl.BlockSpec((1, D), lambda i: (0, 0))
    ins = [x, g.reshape(1, D), dh] + ([dres] if with_res else [])
    return pl.pallas_call(
        body, name=name, grid=(T // tm,), in_specs=[row, vec, row] + ([row] if with_res else []),
        out_specs=[row, vec],
        out_shape=[jax.ShapeDtypeStruct((T, D), F32), jax.ShapeDtypeStruct((1, D), F32)],
        compiler_params=_params("arbitrary"),
    )(*ins)


def _loss_head(x, g, target, name):
    T, D = x.shape
    tm = _pick(T, 512, SUBLANES)

    def body(x_ref, g_ref, t_ref, loss_ref, dx_ref, dg_ref):
        xv = x_ref[...]
        gv = g_ref[...]
        r = lax.rsqrt(jnp.mean(xv * xv, axis=-1, keepdims=True) + EPS)
        e = xv * r * gv - t_ref[...]
        dy = e * (1.0 / D)
        dyg = dy * gv
        c = jnp.mean(dyg * xv, axis=-1, keepdims=True)
        dx_ref[...] = r * dyg - xv * (r * r * r) * c

        @pl.when(pl.program_id(0) == 0)
        def _():
            loss_ref[...] = jnp.zeros_like(loss_ref)
            dg_ref[...] = jnp.zeros_like(dg_ref)

        loss_ref[...] += jnp.sum(e * e, axis=0, keepdims=True) * (0.5 / D)
        dg_ref[...] += jnp.sum(dy * xv * r, axis=0, keepdims=True)

    row = pl.BlockSpec((tm, D), lambda i: (i, 0))
    vec = pl.BlockSpec((1, D), lambda i: (0, 0))
    return pl.pallas_call(
        body, name=name, grid=(T // tm,), in_specs=[row, vec, row], out_specs=[vec, row, vec],
        out_shape=[jax.ShapeDtypeStruct((1, D), F32), jax.ShapeDtypeStruct((T, D), F32), jax.ShapeDtypeStruct((1, D), F32)],
        compiler_params=_params("arbitrary"),
    )(x, g.reshape(1, D), target)


def _ssm_disc_math(lre, lim, logdt, br, bi):
    dt = jnp.exp(logdt)
    mag = jnp.exp(lre * dt)
    ar = mag * jnp.cos(lim * dt)
    ai = mag * jnp.sin(lim * dt)
    den = lre * lre + lim * lim
    nr = ar - 1.0
    fr = (nr * lre + ai * lim) / den
    fi = (ai * lre - nr * lim) / den
    return ar, ai, fr[None] * br - fi[None] * bi, fr[None] * bi + fi[None] * br


def _ssm_disc(lre, lim, logdt, br, bi, name):
    def body(lre_ref, lim_ref, dt_ref, br_ref, bi_ref, ar_ref, ai_ref, bbr_ref, bbi_ref):
        ar, ai, bbr, bbi = _ssm_disc_math(lre_ref[...], lim_ref[...], dt_ref[...], br_ref[...], bi_ref[...])
        ar_ref[...] = ar
        ai_ref[...] = ai
        bbr_ref[...] = bbr
        bbi_ref[...] = bbi

    sd = jax.ShapeDtypeStruct
    return pl.pallas_call(
        body, name=name, out_shape=[sd(lre.shape, F32), sd(lre.shape, F32), sd(br.shape, F32), sd(br.shape, F32)],
    )(lre, lim, logdt, br, bi)


def _ssm_disc_bwd(lre, lim, logdt, br, bi, dar, dai, dbbr, dbbi, name):
    def body(lre_ref, lim_ref, dt_ref, br_ref, bi_ref, dar_ref, dai_ref, dbbr_ref, dbbi_ref,
             glre_ref, glim_ref, gdt_ref, gbr_ref, gbi_ref):
        _, vjp = jax.vjp(_ssm_disc_math, lre_ref[...], lim_ref[...], dt_ref[...], br_ref[...], bi_ref[...])
        glre, glim, gdt, gbr, gbi = vjp((dar_ref[...], dai_ref[...], dbbr_ref[...], dbbi_ref[...]))
        glre_ref[...] = glre
        glim_ref[...] = glim
        gdt_ref[...] = gdt
        gbr_ref[...] = gbr
        gbi_ref[...] = gbi

    sd = jax.ShapeDtypeStruct
    return pl.pallas_call(
        body, name=name,
        out_shape=[sd(lre.shape, F32), sd(lre.shape, F32), sd(logdt.shape, F32), sd(br.shape, F32), sd(br.shape, F32)],
    )(lre, lim, logdt, br, bi, dar, dai, dbbr, dbbi)


def _shift_segments(v, down):
    n = v.shape[0]
    rows = lax.broadcasted_iota(jnp.int32, v.shape, 0)
    if down:
        return jnp.where(rows >= 1, pltpu.roll(v, 1, 0), 0.0)
    return jnp.where(rows < n - 1, pltpu.roll(v, n - 1, 0), 0.0)


def _cpow(ar, ai, n):
    rr, ri = None, None
    pr, pi = ar, ai
    while n:
        if n & 1:
            if rr is None:
                rr, ri = pr, pi
            else:
                rr, ri = rr * pr - ri * pi, rr * pi + ri * pr
        n >>= 1
        if n:
            pr, pi = pr * pr - pi * pi, 2.0 * pr * pi
    return rr, ri


def _ssm_geometry(T, C):
    seg_steps = T // SSM_SEGMENTS
    kc = min(SSM_CHUNK_STEPS, seg_steps)
    return C // SSM_BLOCK_CH, seg_steps, kc, seg_steps // kc, SSM_SEGMENTS * kc


def _ssm_carries(u, dy, bmat, cmat, amat, *, reverse, name):
    src = dy if reverse else u
    T, C = src.shape
    nblk, seg_steps, kc, nchunk, rc = _ssm_geometry(T, C)
    half = bmat.shape[2] // 2

    def body(src_ref, w_ref, a_ref, out_ref, buf_ref, st_ref):
        c = pl.program_id(1)

        @pl.when(c == 0)
        def _():
            st_ref[...] = jnp.zeros_like(st_ref)

        if reverse:
            buf_ref[...] = _dot(src_ref[...].astype(BF16), w_ref[0], NT)
        else:
            buf_ref[...] = _dot(src_ref[...].astype(BF16), w_ref[0], NN)
        ar = a_ref[0, :, :half]
        ai = a_ref[0, :, half:]
        if reverse:
            ai = -ai

        def step(i, carry):
            xr, xi = carry
            k = (kc - 1 - i) if reverse else i
            row = pl.multiple_of(k * SUBLANES, SUBLANES)
            br = buf_ref[pl.ds(row, SUBLANES), :half]
            bi = buf_ref[pl.ds(row, SUBLANES), half:]
            return ar * xr - ai * xi + br, ar * xi + ai * xr + bi

        xr, xi = lax.fori_loop(0, kc, step, (st_ref[:, :half], st_ref[:, half:]), unroll=8)
        st_ref[:, :half] = xr
        st_ref[:, half:] = xi

        @pl.when(c == nchunk - 1)
        def _():
            pr, pi = _cpow(ar, ai, seg_steps)
            sr = jnp.zeros_like(xr)
            si = jnp.zeros_like(xi)
            for _ in range(SSM_SEGMENTS - 1):
                nr = xr + pr * sr - pi * si
                ni = xi + pr * si + pi * sr
                sr = _shift_segments(nr, not reverse)
                si = _shift_segments(ni, not reverse)
            out_ref[0, :, :half] = sr
            out_ref[0, :, half:] = si

    cidx = (lambda b, c: (nchunk - 1 - c, b)) if reverse else (lambda b, c: (c, b))
    w = cmat if reverse else bmat
    return pl.pallas_call(
        body, name=name, grid=(nblk, nchunk),
        in_specs=[pl.BlockSpec((rc, SSM_BLOCK_CH), cidx),
                  pl.BlockSpec((1,) + w.shape[1:], lambda b, c: (b, 0, 0)),
                  pl.BlockSpec((1, SUBLANES, 2 * half), lambda b, c: (b, 0, 0))],
        out_specs=pl.BlockSpec((1, SUBLANES, 2 * half), lambda b, c: (b, 0, 0)),
        out_shape=jax.ShapeDtypeStruct((nblk, SUBLANES, 2 * half), F32),
        scratch_shapes=[pltpu.VMEM((rc, 2 * half), F32), pltpu.VMEM((SUBLANES, 2 * half), F32)],
        compiler_params=_params("parallel", "arbitrary"),
    )(src, w, amat)


def _ssm_scan(u, bmat, cmat, amat, carries, dvec, name):
    T, C = u.shape
    nblk, seg_steps, kc, nchunk, rc = _ssm_geometry(T, C)
    half = bmat.shape[2] // 2

    def body(u_ref, b_ref, c_ref, a_ref, s_ref, d_ref, y_ref, x_ref, st_ref):
        c = pl.program_id(1)

        @pl.when(c == 0)
        def _():
            st_ref[...] = s_ref[0]

        uv = u_ref[...]
        x_ref[...] = _dot(uv.astype(BF16), b_ref[0], NN)
        ar = a_ref[0, :, :half]
        ai = a_ref[0, :, half:]

        def step(k, carry):
            xr, xi = carry
            row = pl.multiple_of(k * SUBLANES, SUBLANES)
            nr = ar * xr - ai * xi + x_ref[pl.ds(row, SUBLANES), :half]
            ni = ar * xi + ai * xr + x_ref[pl.ds(row, SUBLANES), half:]
            x_ref[pl.ds(row, SUBLANES), :half] = nr
            x_ref[pl.ds(row, SUBLANES), half:] = ni
            return nr, ni

        xr, xi = lax.fori_loop(0, kc, step, (st_ref[:, :half], st_ref[:, half:]), unroll=8)
        st_ref[:, :half] = xr
        st_ref[:, half:] = xi
        y_ref[...] = _dot(x_ref[...].astype(BF16), c_ref[0], NN) + d_ref[...] * uv

    return pl.pallas_call(
        body, name=name, grid=(nblk, nchunk),
        in_specs=[pl.BlockSpec((rc, SSM_BLOCK_CH), lambda b, c: (c, b)),
                  pl.BlockSpec((1,) + bmat.shape[1:], lambda b, c: (b, 0, 0)),
                  pl.BlockSpec((1,) + cmat.shape[1:], lambda b, c: (b, 0, 0)),
                  pl.BlockSpec((1, SUBLANES, 2 * half), lambda b, c: (b, 0, 0)),
                  pl.BlockSpec((1, SUBLANES, 2 * half), lambda b, c: (b, 0, 0)),
                  pl.BlockSpec((1, SSM_BLOCK_CH), lambda b, c: (0, b))],
        out_specs=[pl.BlockSpec((rc, SSM_BLOCK_CH), lambda b, c: (c, b)),
                   pl.BlockSpec((rc, 2 * half), lambda b, c: (c, b))],
        out_shape=[jax.ShapeDtypeStruct((T, C), F32), jax.ShapeDtypeStruct((T, nblk * 2 * half), F32)],
        scratch_shapes=[pltpu.VMEM((SUBLANES, 2 * half), F32)],
        compiler_params=_params("parallel", "arbitrary"),
    )(u, bmat, cmat, amat, carries, dvec)


def _ssm_scan_bwd(dy, u, xs, bmat, cmat, amat, carries, dvec, name):
    T, C = u.shape
    nblk, seg_steps, kc, nchunk, rc = _ssm_geometry(T, C)
    half = bmat.shape[2] // 2
    width = 2 * half

    def body(dy_ref, u_ref, x_ref, xp_ref, b_ref, c_ref, a_ref, s_ref, d_ref,
             du_ref, db_ref, dc_ref, da_ref, dd_ref, g_ref, st_ref, acc_ref):
        c = pl.program_id(1)

        @pl.when(c == 0)
        def _():
            st_ref[...] = s_ref[0]
            acc_ref[...] = jnp.zeros_like(acc_ref)
            db_ref[...] = jnp.zeros_like(db_ref)
            dc_ref[...] = jnp.zeros_like(dc_ref)
            dd_ref[...] = jnp.zeros_like(dd_ref)

        dyv = dy_ref[...]
        uv = u_ref[...]
        dyb = dyv.astype(BF16)
        g_ref[...] = _dot(dyb, c_ref[0], NT)
        ar = a_ref[0, :, :half]
        ai = a_ref[0, :, half:]

        def step(i, carry):
            gr, gi, sr, si = carry
            k = kc - 1 - i
            row = pl.multiple_of(k * SUBLANES, SUBLANES)
            nr = ar * gr + ai * gi + g_ref[pl.ds(row, SUBLANES), :half]
            ni = ar * gi - ai * gr + g_ref[pl.ds(row, SUBLANES), half:]
            g_ref[pl.ds(row, SUBLANES), :half] = nr
            g_ref[pl.ds(row, SUBLANES), half:] = ni
            prow = pl.multiple_of(jnp.maximum(k - 1, 0) * SUBLANES, SUBLANES)
            live = (k >= 1).astype(F32)
            xr = x_ref[pl.ds(prow, SUBLANES), :half] * live
            xi = x_ref[pl.ds(prow, SUBLANES), half:] * live
            return nr, ni, sr + xr * nr + xi * ni, si + xr * ni - xi * nr

        init = (st_ref[:, :half], st_ref[:, half:], acc_ref[:, :half], acc_ref[:, half:])
        gr, gi, sr, si = lax.fori_loop(0, kc, step, init, unroll=8)
        st_ref[:, :half] = gr
        st_ref[:, half:] = gi
        xpr = xp_ref[:, :half]
        xpi = xp_ref[:, half:]
        first = (c == nchunk - 1)
        xpr = jnp.where(first, _shift_segments(xpr, True), xpr)
        xpi = jnp.where(first, _shift_segments(xpi, True), xpi)
        acc_ref[:, :half] = sr + xpr * gr + xpi * gi
        acc_ref[:, half:] = si + xpr * gi - xpi * gr

        gb = g_ref[...].astype(BF16)
        du_ref[...] = _dot(gb, b_ref[0], NT) + dyv * d_ref[...]
        db_ref[0] += _dot(uv.astype(BF16), gb, TN)
        dc_ref[0] += _dot(dyb, x_ref[...].astype(BF16), TN)
        dd_ref[...] += jnp.sum(dyv * uv, axis=0, keepdims=True)

        @pl.when(c == nchunk - 1)
        def _():
            tot = jnp.sum(acc_ref[...], axis=0, keepdims=True)
            da_ref[0] = jnp.broadcast_to(tot, (SUBLANES, width))

    rev = lambda b, c: (nchunk - 1 - c, b)
    blk3 = lambda b, c: (b, 0, 0)
    prev_group = lambda b, c: (((nchunk - 1 - c) * kc - 1 + seg_steps) % seg_steps, b)
    sd = jax.ShapeDtypeStruct
    return pl.pallas_call(
        body, name=name, grid=(nblk, nchunk),
        in_specs=[pl.BlockSpec((rc, SSM_BLOCK_CH), rev), pl.BlockSpec((rc, SSM_BLOCK_CH), rev),
                  pl.BlockSpec((rc, width), rev), pl.BlockSpec((SUBLANES, width), prev_group),
                  pl.BlockSpec((1,) + bmat.shape[1:], blk3), pl.BlockSpec((1,) + cmat.shape[1:], blk3),
                  pl.BlockSpec((1, SUBLANES, width), blk3), pl.BlockSpec((1, SUBLANES, width), blk3),
                  pl.BlockSpec((1, SSM_BLOCK_CH), lambda b, c: (0, b))],
        out_specs=[pl.BlockSpec((rc, SSM_BLOCK_CH), rev), pl.BlockSpec((1, SSM_BLOCK_CH, width), blk3),
                   pl.BlockSpec((1, SSM_BLOCK_CH, width), blk3), pl.BlockSpec((1, SUBLANES, width), blk3),
                   pl.BlockSpec((1, SSM_BLOCK_CH), lambda b, c: (0, b))],
        out_shape=[sd((T, C), F32), sd((nblk, SSM_BLOCK_CH, width), F32), sd((nblk, SSM_BLOCK_CH, width), F32),
                   sd((nblk, SUBLANES, width), F32), sd((1, C), F32)],
        scratch_shapes=[pltpu.VMEM((rc, width), F32), pltpu.VMEM((SUBLANES, width), F32), pltpu.VMEM((SUBLANES, width), F32)],
        compiler_params=_params("parallel", "arbitrary"),
    )(dy, u, xs, xs, bmat, cmat, amat, carries, dvec)


SSM_STEPS_FWD = 128
SSM_STEPS_BWD = 256


def _ssm_tiles(ref, v, off, steps, n):
    return [ref[v, pl.ds(off + j, steps, stride=SUBLANES), :] for j in range(n)]


def _ssm_fwd(uz, bmat, cmat, art, ait, dvec, name):
    T = uz.shape[0]
    nblk, cb, width = bmat.shape
    C = nblk * cb
    half = width // 2
    nt = half // LANES
    npair = nblk // 2
    kc = min(SSM_STEPS_FWD, T)
    nchunk = T // kc

    def body(u_ref, b_ref, c_ref, ar_ref, ai_ref, d_ref, y_ref, xr_ref, xi_ref, sr_ref, si_ref):
        @pl.when(pl.program_id(0) == 0)
        def _():
            sr_ref[...] = jnp.zeros_like(sr_ref)
            si_ref[...] = jnp.zeros_like(si_ref)

        uv = u_ref[...]
        for b in range(nblk):
            bu = _dot(uv[:, b * cb:(b + 1) * cb].astype(BF16), b_ref[b], NN)
            v, off = b // 2, nt * (b % 2)
            for j in range(nt):
                xr_ref[v, pl.ds(off + j, kc, stride=SUBLANES), :] = bu[:, j * LANES:(j + 1) * LANES]
                xi_ref[v, pl.ds(off + j, kc, stride=SUBLANES), :] = bu[:, half + j * LANES:half + (j + 1) * LANES]
        ars = [ar_ref[v] for v in range(npair)]
        ais = [ai_ref[v] for v in range(npair)]

        def step(k, carry):
            row = pl.ds(k * SUBLANES, SUBLANES)
            out = []
            for v in range(npair):
                xr, xi = carry[2 * v], carry[2 * v + 1]
                nr = ars[v] * xr - ais[v] * xi + xr_ref[v, row, :]
                ni = ars[v] * xi + ais[v] * xr + xi_ref[v, row, :]
                xr_ref[v, row, :] = nr
                xi_ref[v, row, :] = ni
                out += [nr, ni]
            return tuple(out)

        fin = tuple(ref[v] for v in range(npair) for ref in (sr_ref, si_ref))
        for k in range(kc):
            fin = step(k, fin)
        for v in range(npair):
            sr_ref[v] = fin[2 * v]
            si_ref[v] = fin[2 * v + 1]
        for b in range(nblk):
            v, off = b // 2, nt * (b % 2)
            xb = jnp.concatenate(_ssm_tiles(xr_ref, v, off, kc, nt) + _ssm_tiles(xi_ref, v, off, kc, nt), axis=1)
            cols = slice(b * cb, (b + 1) * cb)
            y_ref[:, cols] = _dot(xb.astype(BF16), c_ref[b], NN) + d_ref[:, cols] * uv[:, cols]

    whole = lambda a: pl.BlockSpec(a.shape, lambda c: (0,) * a.ndim)
    st = pl.BlockSpec((npair, kc * SUBLANES, LANES), lambda c: (0, c, 0))
    sd = jax.ShapeDtypeStruct
    return pl.pallas_call(
        body, name=name, grid=(nchunk,),
        in_specs=[pl.BlockSpec((kc, C), lambda c: (c, 0)), whole(bmat), whole(cmat), whole(art), whole(ait), whole(dvec)],
        out_specs=[pl.BlockSpec((kc, C), lambda c: (c, 0)), st, st],
        out_shape=[sd((T, C), F32), sd((npair, T * SUBLANES, LANES), F32), sd((npair, T * SUBLANES, LANES), F32)],
        scratch_shapes=[pltpu.VMEM((npair, SUBLANES, LANES), F32), pltpu.VMEM((npair, SUBLANES, LANES), F32)],
        compiler_params=_params("arbitrary"),
    )(uz, bmat, cmat, art, ait, dvec)


def _ssm_bwd(dy, uz, xr, xi, bmat, cmat, art, ait, dvec, dproj, name):
    T = uz.shape[0]
    nblk, cb, width = bmat.shape
    C = nblk * cb
    half = width // 2
    nt = half // LANES
    npair = nblk // 2
    kc = min(SSM_STEPS_BWD, T)
    nchunk = T // kc

    def body(dy_ref, u_ref, xr_ref, xi_ref, xpr_ref, xpi_ref, b_ref, c_ref, ar_ref, ai_ref, d_ref, _,
             du_ref, db_ref, dc_ref, dar_ref, dai_ref, dd_ref, gr_ref, gi_ref, sr_ref, si_ref):
        c = pl.program_id(0)

        @pl.when(c == 0)
        def _():
            for ref in (sr_ref, si_ref, db_ref, dc_ref, dar_ref, dai_ref, dd_ref):
                ref[...] = jnp.zeros_like(ref)

        dyv = dy_ref[...]
        uv = u_ref[...]
        for b in range(nblk):
            dx = _dot(dyv[:, b * cb:(b + 1) * cb].astype(BF16), c_ref[b], NT)
            v, off = b // 2, nt * (b % 2)
            for j in range(nt):
                gr_ref[v, pl.ds(off + j, kc, stride=SUBLANES), :] = dx[:, j * LANES:(j + 1) * LANES]
                gi_ref[v, pl.ds(off + j, kc, stride=SUBLANES), :] = dx[:, half + j * LANES:half + (j + 1) * LANES]
        ars = [ar_ref[v] for v in range(npair)]
        ais = [ai_ref[v] for v in range(npair)]

        def pair_update(v, gr, gi, row):
            nr = ars[v] * gr + ais[v] * gi + gr_ref[v, row, :]
            ni = ars[v] * gi - ais[v] * gr + gi_ref[v, row, :]
            gr_ref[v, row, :] = nr
            gi_ref[v, row, :] = ni
            return nr, ni

        def step(i, carry):
            k = kc - 1 - i
            row = pl.ds(k * SUBLANES, SUBLANES)
            prow = pl.ds((k - 1) * SUBLANES, SUBLANES)
            out = []
            for v in range(npair):
                gr, gi, sr, si = carry[4 * v:4 * v + 4]
                nr, ni = pair_update(v, gr, gi, row)
                pr, pi = xr_ref[v, prow, :], xi_ref[v, prow, :]
                out += [nr, ni, sr + pr * nr + pi * ni, si + pr * ni - pi * nr]
            return tuple(out)

        mid = tuple(ref[v] for v in range(npair) for ref in (sr_ref, si_ref, dar_ref, dai_ref))
        for i in range(kc - 1):
            mid = step(i, mid)
        live = (c < nchunk - 1).astype(F32)
        row0 = pl.ds(0, SUBLANES)
        for v in range(npair):
            gr, gi, sr, si = mid[4 * v:4 * v + 4]
            nr, ni = pair_update(v, gr, gi, row0)
            pr, pi = xpr_ref[v] * live, xpi_ref[v] * live
            sr_ref[v] = nr
            si_ref[v] = ni
            dar_ref[v] = sr + pr * nr + pi * ni
            dai_ref[v] = si + pr * ni - pi * nr
        for b in range(nblk):
            v, off = b // 2, nt * (b % 2)
            cols = slice(b * cb, (b + 1) * cb)
            gb = jnp.concatenate(_ssm_tiles(gr_ref, v, off, kc, nt) + _ssm_tiles(gi_ref, v, off, kc, nt), axis=1).astype(BF16)
            xb = jnp.concatenate(_ssm_tiles(xr_ref, v, off, kc, nt) + _ssm_tiles(xi_ref, v, off, kc, nt), axis=1).astype(BF16)
            du_ref[:, cols] = (_dot(gb, b_ref[b], NT) + dyv[:, cols] * d_ref[:, cols]).astype(BF16)
            db_ref[b] += _dot(uv[:, cols].astype(BF16), gb, TN)
            dc_ref[b] += _dot(dyv[:, cols].astype(BF16), xb, TN)
        dd_ref[...] += jnp.sum(dyv * uv, axis=0, keepdims=True)

    whole = lambda a: pl.BlockSpec(a.shape, lambda c: (0,) * a.ndim)
    rev = lambda c: (nchunk - 1 - c, 0)
    st = pl.BlockSpec((npair, kc * SUBLANES, LANES), lambda c: (0, nchunk - 1 - c, 0))
    stp = pl.BlockSpec((npair, SUBLANES, LANES), lambda c: (0, jnp.maximum((nchunk - 1 - c) * kc - 1, 0), 0))
    acc = lambda shape: pl.BlockSpec(shape, lambda c: (0,) * len(shape))
    sd = jax.ShapeDtypeStruct
    pair_shape = (npair, SUBLANES, LANES)
    return pl.pallas_call(
        body, name=name, grid=(nchunk,),
        in_specs=[pl.BlockSpec((kc, C), rev), pl.BlockSpec((kc, C), rev), st, st, stp, stp, whole(bmat), whole(cmat),
                  whole(art), whole(ait), whole(dvec), pl.BlockSpec(memory_space=pl.ANY)],
        out_specs=[pl.BlockSpec((kc, C), rev), acc(bmat.shape), acc(bmat.shape), acc(pair_shape), acc(pair_shape), acc((1, C))],
        out_shape=[sd(dproj.shape, BF16), sd(bmat.shape, F32), sd(bmat.shape, F32), sd(pair_shape, F32), sd(pair_shape, F32),
                   sd((1, C), F32)],
        scratch_shapes=[pltpu.VMEM((npair, kc * SUBLANES, LANES), F32), pltpu.VMEM((npair, kc * SUBLANES, LANES), F32),
                        pltpu.VMEM(pair_shape, F32), pltpu.VMEM(pair_shape, F32)],
        input_output_aliases={11: 0}, compiler_params=_params("arbitrary"),
    )(dy, uz, xr, xi, xr, xi, bmat, cmat, art, ait, dvec, dproj)


def _ssm_post(y, z, w_glu, b_glu, name):
    T, C = y.shape
    tm = _pick(T, 512, SUBLANES)

    def body(y_ref, z_ref, w_ref, b_ref, o_ref, a_ref):
        a, _ = _gelu_and_grad(y_ref[...])
        ab = a.astype(BF16)
        sg = _sigmoid(_dot(ab, w_ref[...], NN) + b_ref[...])
        sz, _ = _silu_and_grad(z_ref[...].astype(F32))
        o_ref[...] = (a * sg * sz).astype(BF16)
        a_ref[...] = ab

    row = pl.BlockSpec((tm, C), lambda i: (i, 0))
    return pl.pallas_call(
        body, name=name, grid=(T // tm,),
        in_specs=[row, row, pl.BlockSpec((C, C), lambda i: (0, 0)), pl.BlockSpec((1, C), lambda i: (0, 0))],
        out_specs=[row, row], out_shape=[jax.ShapeDtypeStruct((T, C), BF16)] * 2, compiler_params=_params("parallel"),
    )(y, z, w_glu, b_glu.reshape(1, C))


def _ssm_post_bwd(do, y, z, w_glu, b_glu, dproj, col, name):
    T, C = y.shape
    tm = _pick(T, 512, SUBLANES)

    def body(do_ref, y_ref, z_ref, w_ref, b_ref, _, dy_ref, dz_ref, ds_ref, db_ref):
        dov = do_ref[...]
        a, da_dy = _gelu_and_grad(y_ref[...])
        sg = _sigmoid(_dot(a.astype(BF16), w_ref[...], NN) + b_ref[...])
        sz, dsz = _silu_and_grad(z_ref[...].astype(F32))
        yg = a * sg
        dz_ref[...] = (dov * yg * dsz).astype(BF16)
        dyg = dov * sz
        ds = dyg * a * sg * (1.0 - sg)
        dsb = ds.astype(BF16)
        ds_ref[...] = dsb
        da = dyg * sg + _dot(dsb, w_ref[...], NT)
        dy_ref[...] = da * da_dy

        @pl.when(pl.program_id(0) == 0)
        def _():
            db_ref[...] = jnp.zeros_like(db_ref)

        db_ref[...] += jnp.sum(ds, axis=0, keepdims=True)

    row = pl.BlockSpec((tm, C), lambda i: (i, 0))
    vec = pl.BlockSpec((1, C), lambda i: (0, 0))
    sd = jax.ShapeDtypeStruct
    return pl.pallas_call(
        body, name=name, grid=(T // tm,),
        in_specs=[row, row, row, pl.BlockSpec((C, C), lambda i: (0, 0)), vec, pl.BlockSpec(memory_space=pl.ANY)],
        out_specs=[row, pl.BlockSpec((tm, C), lambda i: (i, col // C)), row, vec],
        out_shape=[sd((T, C), F32), sd(dproj.shape, BF16), sd((T, C), BF16), sd((1, C), F32)],
        input_output_aliases={5: 1}, compiler_params=_params("arbitrary"),
    )(do, y, z, w_glu, b_glu.reshape(1, C), dproj)


def _rel_bucket(dist):
    n = jnp.maximum(dist, 0)
    max_exact = NUM_BUCKETS // 2
    n_f = jnp.maximum(n, 1).astype(F32)
    large = max_exact + (jnp.log(n_f / max_exact) / math.log(REL_MAX_DISTANCE / max_exact)
                         * (NUM_BUCKETS - max_exact)).astype(jnp.int32)
    large = jnp.minimum(large, NUM_BUCKETS - 1)
    return jnp.where(n < max_exact, n, large)


def _band_tables():
    qi = jnp.arange(ATTN_BLOCK)[:, None]
    kj = jnp.arange(2 * ATTN_BLOCK)[None, :]
    delta = ATTN_BLOCK + qi - kj
    buckets, bands = [], []
    for window, dilation in ATTN_CONFIGS:
        bands.append((delta >= 0) & (delta <= window // dilation))
        buckets.append(_rel_bucket(jnp.maximum(delta, 0) * dilation))
    return jnp.stack(buckets), jnp.stack(bands)


def _attn_blocks_per_residue(T):
    return [T // (ATTN_BLOCK * d) for _, d in ATTN_CONFIGS]


ATTN_UNITS = 4


def _attn_tile(T, r):
    nq = max(1, ATTN_UNITS // r)
    rows = ATTN_BLOCK * r * nq
    return nq, rows, T // rows


def _attn_units(r, nq, chunk):
    if r >= ATTN_UNITS:
        return [(chunk * ATTN_UNITS + i, None) for i in range(ATTN_UNITS)]
    units = []
    for j in range(nq):
        for s in range(r):
            units.append((ATTN_BLOCK * j * r + s, ATTN_BLOCK * (j - 1) * r + s if j else None))
    return units


def _rows(start, r):
    return pl.ds(start, ATTN_BLOCK, stride=r) if r > 1 else pl.ds(start, ATTN_BLOCK)


def _attn_group_fwd(qkv, biasm, g, name):
    T = qkv.shape[0]
    r = ATTN_CONFIGS[g][1]
    B, hd = ATTN_BLOCK, ATTN_HEAD_DIM
    nq, rows, ntiles = _attn_tile(T, r)
    nchunks = max(1, r // ATTN_UNITS)
    last_prev = B * (nq - 1) * r
    scale = hd ** -0.5
    tiles_per_tensor = 3 * HEADS_PER_GROUP * hd // LANES

    def body(q_ref, kc_ref, kp_ref, vc_ref, vp_ref, bias_ref, o_ref, lse_ref, s_ref, p_ref):
        n = pl.program_id(1)
        lane = lax.broadcasted_iota(jnp.int32, (1, LANES), 1)
        col = lax.broadcasted_iota(jnp.int32, (1, 2 * B), 1)
        masks = [lane < hd, lane >= hd]
        first_pen = jnp.where((col < B) & (n == 0), NEG_INF, 0.0)

        def chunk_body(chunk):
            units = _attn_units(r, nq, chunk)

            def keys(cur_ref, prev_ref, cs, ps):
                prev = prev_ref[_rows(last_prev + (cs if r >= ATTN_UNITS else cs % r), r), :] if ps is None else cur_ref[_rows(ps, r), :]
                return jnp.concatenate([prev, cur_ref[_rows(cs, r), :]], axis=0).astype(BF16)

            for u, (cs, ps) in enumerate(units):
                qv = q_ref[_rows(cs, r), :]
                kw = keys(kc_ref, kp_ref, cs, ps)
                for hh in range(2):
                    s_ref[2 * u + hh] = _dot(jnp.where(masks[hh], qv, 0.0).astype(BF16), kw, NT)
            for u, (cs, ps) in enumerate(units):
                lses = []
                for hh in range(2):
                    s = s_ref[2 * u + hh] * scale + bias_ref[hh]
                    if ps is None:
                        s = s + first_pen
                    m = jnp.max(s, axis=-1, keepdims=True)
                    p = jnp.exp(s - m)
                    l = jnp.sum(p, axis=-1, keepdims=True)
                    p_ref[2 * u + hh] = (p / l).astype(BF16)
                    lses.append(m + jnp.log(l))
                lse_ref[_rows(cs, r), :] = jnp.where(masks[0], lses[0], lses[1])
            for u, (cs, ps) in enumerate(units):
                vw = keys(vc_ref, vp_ref, cs, ps)
                o_ref[_rows(cs, r), :] = (_dot(p_ref[2 * u], jnp.where(masks[0], vw, 0), NN)
                                          + _dot(p_ref[2 * u + 1], jnp.where(masks[1], vw, 0), NN))

        if nchunks == 1:
            chunk_body(0)
        else:
            pl.loop(0, nchunks)(chunk_body)

    def cur(t):
        return pl.BlockSpec((rows, LANES), lambda hf, n: (n, t * tiles_per_tensor + 2 * g + hf))

    def prev(t):
        return pl.BlockSpec((rows, LANES), lambda hf, n: (jnp.maximum(n - 1, 0), t * tiles_per_tensor + 2 * g + hf))

    out = pl.BlockSpec((rows, LANES), lambda hf, n: (n, hf))
    sd = jax.ShapeDtypeStruct((T, 2 * LANES), F32)
    return pl.pallas_call(
        body, name=name, grid=(2, ntiles),
        in_specs=[cur(0), cur(1), prev(1), cur(2), prev(2), pl.BlockSpec((None, 2, B, 2 * B), lambda hf, n: (g, hf, 0, 0))],
        out_specs=[out, out], out_shape=[sd, sd],
        scratch_shapes=[pltpu.VMEM((2 * ATTN_UNITS, B, 2 * B), F32), pltpu.VMEM((2 * ATTN_UNITS, B, 2 * B), BF16)],
        compiler_params=_params("parallel", "parallel"),
    )(qkv, qkv, qkv, qkv, qkv, biasm)


def _attn_group_bwd(qkv, do, dvec, lse, biasm, g, dproj, dk_col, name):
    T = qkv.shape[0]
    r = ATTN_CONFIGS[g][1]
    B, hd = ATTN_BLOCK, ATTN_HEAD_DIM
    nq, rows, ntiles = _attn_tile(T, r)
    nchunks = max(1, r // ATTN_UNITS)
    last_prev = B * (nq - 1) * r
    scale = hd ** -0.5
    tiles_per_tensor = 3 * HEADS_PER_GROUP * hd // LANES

    def body(q_ref, kc_ref, kp_ref, vc_ref, vp_ref, do_ref, dv_ref, lse_ref, bias_ref, _,
             dq_ref, dk_ref, dvo_ref, dbias_ref, ck_ref, cv_ref, ak_ref, av_ref, s_ref, dp_ref, p_ref, ds_ref):
        n = pl.program_id(1)
        lane = lax.broadcasted_iota(jnp.int32, (1, LANES), 1)
        col = lax.broadcasted_iota(jnp.int32, (1, 2 * B), 1)
        masks = [lane < hd, lane >= hd]
        first_pen = jnp.where((col < B) & (n == 0), NEG_INF, 0.0)

        @pl.when(n == 0)
        def _():
            dbias_ref[...] = jnp.zeros_like(dbias_ref)
            ck_ref[...] = jnp.zeros_like(ck_ref)
            cv_ref[...] = jnp.zeros_like(cv_ref)

        def chunk_body(chunk):
            units = _attn_units(r, nq, chunk)

            def prev_rows(cs):
                return _rows(last_prev + (cs if r >= ATTN_UNITS else cs % r), r)

            def keys(cur_ref, prev_ref, cs, ps):
                prev = prev_ref[prev_rows(cs), :] if ps is None else cur_ref[_rows(ps, r), :]
                return jnp.concatenate([prev, cur_ref[_rows(cs, r), :]], axis=0).astype(BF16)

            for u, (cs, ps) in enumerate(units):
                qv = q_ref[_rows(cs, r), :]
                dov = do_ref[_rows(cs, r), :]
                kw = keys(kc_ref, kp_ref, cs, ps)
                vw = keys(vc_ref, vp_ref, cs, ps)
                for hh in range(2):
                    s_ref[2 * u + hh] = _dot(jnp.where(masks[hh], qv, 0.0).astype(BF16), kw, NT)
                    dp_ref[2 * u + hh] = _dot(jnp.where(masks[hh], dov, 0.0).astype(BF16), vw, NT)
            for u, (cs, ps) in enumerate(units):
                lse_t = lse_ref[_rows(cs, r), :]
                dv_t = dv_ref[_rows(cs, r), :]
                for hh in range(2):
                    lo = hh * hd
                    s = s_ref[2 * u + hh] * scale + bias_ref[hh]
                    if ps is None:
                        s = s + first_pen
                    p = jnp.exp(s - lse_t[:, lo:lo + 1])
                    ds = p * (dp_ref[2 * u + hh] + dv_t[:, lo:lo + 1])
                    dbias_ref[hh] += ds
                    p_ref[2 * u + hh] = p.astype(BF16)
                    ds_ref[2 * u + hh] = ds.astype(BF16)
            for u, (cs, ps) in enumerate(units):
                qv = q_ref[_rows(cs, r), :]
                dov = do_ref[_rows(cs, r), :]
                kw = keys(kc_ref, kp_ref, cs, ps)
                dq, dkw, dvw = 0.0, 0.0, 0.0
                for hh in range(2):
                    dsb = ds_ref[2 * u + hh]
                    dq = dq + _dot(dsb, jnp.where(masks[hh], kw, 0), NN)
                    dkw = dkw + _dot(dsb, jnp.where(masks[hh], qv, 0.0).astype(BF16), TN)
                    dvw = dvw + _dot(p_ref[2 * u + hh], jnp.where(masks[hh], dov, 0.0).astype(BF16), TN)
                dq_ref[_rows(cs, r), :] = dq * scale
                ak_ref[_rows(cs, r), :] = dkw[B:] * scale
                av_ref[_rows(cs, r), :] = dvw[B:]
                if ps is None:
                    ck_ref[prev_rows(cs), :] += dkw[:B] * scale
                    cv_ref[prev_rows(cs), :] += dvw[:B]
                else:
                    ak_ref[_rows(ps, r), :] += dkw[:B] * scale
                    av_ref[_rows(ps, r), :] += dvw[:B]

        @pl.when(n < ntiles)
        def _():
            for chunk in range(nchunks):
                chunk_body(chunk)

        dk_ref[...] = ck_ref[...].astype(BF16)
        dvo_ref[...] = cv_ref[...].astype(BF16)
        ck_ref[...] = ak_ref[...]
        cv_ref[...] = av_ref[...]

    last = ntiles - 1

    def cur(t):
        return pl.BlockSpec((rows, LANES), lambda hf, n: (jnp.minimum(n, last), t * tiles_per_tensor + 2 * g + hf))

    def prev(t):
        return pl.BlockSpec((rows, LANES), lambda hf, n: (jnp.clip(n - 1, 0, last), t * tiles_per_tensor + 2 * g + hf))

    nat = pl.BlockSpec((rows, LANES), lambda hf, n: (jnp.minimum(n, last), hf))
    nat_prev = pl.BlockSpec((rows, LANES), lambda hf, n: (jnp.clip(n - 1, 0, last), hf))
    tab = pl.BlockSpec((None, 2, B, 2 * B), lambda hf, n: (g, hf, 0, 0))
    dtab = pl.BlockSpec((2, B, 2 * B), lambda hf, n: (hf, 0, 0))
    sd = jax.ShapeDtypeStruct
    vm = pltpu.VMEM
    dk_tile = dk_col // LANES + 2 * g
    dk_spec = pl.BlockSpec((rows, LANES), lambda hf, n: (jnp.clip(n - 1, 0, last), dk_tile + hf))
    return pl.pallas_call(
        body, name=name, grid=(2, ntiles + 1),
        in_specs=[cur(0), cur(1), prev(1), cur(2), prev(2), nat, nat, nat, tab, pl.BlockSpec(memory_space=pl.ANY)],
        out_specs=[nat, dk_spec, nat_prev, dtab],
        out_shape=[sd((T, 2 * LANES), F32), sd(dproj.shape, BF16), sd((T, 2 * LANES), BF16),
                   sd((HEADS_PER_GROUP, B, 2 * B), F32)],
        scratch_shapes=[vm((rows, LANES), F32), vm((rows, LANES), F32), vm((rows, LANES), F32), vm((rows, LANES), F32),
                        vm((2 * ATTN_UNITS, B, 2 * B), F32), vm((2 * ATTN_UNITS, B, 2 * B), F32),
                        vm((2 * ATTN_UNITS, B, 2 * B), BF16), vm((2 * ATTN_UNITS, B, 2 * B), BF16)],
        input_output_aliases={9: 1}, compiler_params=_params("parallel", "arbitrary"),
    )(qkv, qkv, qkv, qkv, qkv, do, dvec, lse, biasm, dproj)


def _attn_fwd(q, k, v, biasm, name):
    ng, T, gw = q.shape
    hd = ATTN_HEAD_DIM
    nh = gw // hd
    nblk = T // ATTN_BLOCK
    nbs = _attn_blocks_per_residue(T)
    scale = hd ** -0.5
    B = ATTN_BLOCK

    def body(q_ref, kc_ref, kp_ref, vc_ref, vp_ref, bias_ref, o_ref, lse_ref, s_ref, p_ref):
        g = pl.program_id(0)
        b = pl.program_id(1)
        nb = jnp.where(g == 0, nbs[0], jnp.where(g == 1, nbs[1], nbs[2]))
        no_prev = (b % nb) == 0
        col = lax.broadcasted_iota(jnp.int32, (1, 2 * B), 1)
        pen = jnp.where((col < B) & no_prev, NEG_INF, 0.0)
        heads = [slice(h * hd, (h + 1) * hd) for h in range(nh)]
        for h, hs in enumerate(heads):
            kw = jnp.concatenate([kp_ref[0, :, hs], kc_ref[0, :, hs]], axis=0)
            s_ref[h] = _dot(q_ref[0, :, hs], kw, NT)
        for h, hs in enumerate(heads):
            s = s_ref[h] * scale + bias_ref[0, h] + pen
            m = jnp.max(s, axis=-1, keepdims=True)
            p = jnp.exp(s - m)
            l = jnp.sum(p, axis=-1, keepdims=True)
            p_ref[h] = (p / l).astype(BF16)
            lse_ref[0, :, hs] = jnp.broadcast_to(m + jnp.log(l), (B, hd))
        for h, hs in enumerate(heads):
            vw = jnp.concatenate([vp_ref[0, :, hs], vc_ref[0, :, hs]], axis=0)
            o_ref[0, :, hs] = _dot(p_ref[h], vw, NN)

    cur = pl.BlockSpec((1, B, gw), lambda g, b: (g, b, 0))
    prev = pl.BlockSpec((1, B, gw), lambda g, b: (g, jnp.maximum(b - 1, 0), 0))
    return pl.pallas_call(
        body, name=name, grid=(ng, nblk),
        in_specs=[cur, cur, prev, cur, prev, pl.BlockSpec((1, nh, B, 2 * B), lambda g, b: (g, 0, 0, 0))],
        out_specs=[cur, cur], out_shape=[jax.ShapeDtypeStruct(q.shape, F32)] * 2,
        scratch_shapes=[pltpu.VMEM((nh, B, 2 * B), F32), pltpu.VMEM((nh, B, 2 * B), BF16)],
        compiler_params=_params("parallel", "parallel"),
    )(q, k, k, v, v, biasm)


def _attn_bwd(q, k, v, do, dvec, lse, biasm, name):
    ng, T, gw = q.shape
    hd = ATTN_HEAD_DIM
    nh = gw // hd
    nblk = T // ATTN_BLOCK
    nbs = _attn_blocks_per_residue(T)
    scale = hd ** -0.5
    B = ATTN_BLOCK

    def body(q_ref, kc_ref, kp_ref, vc_ref, vp_ref, do_ref, dv_ref, lse_ref, bias_ref,
             dq_ref, dk_ref, dvo_ref, dbias_ref, ck_ref, cv_ref, s_ref, dp_ref, p_ref, ds_ref):
        g = pl.program_id(0)
        b = pl.program_id(1)
        nb = jnp.where(g == 0, nbs[0], jnp.where(g == 1, nbs[1], nbs[2]))
        no_prev = (b % nb) == 0

        @pl.when(b == 0)
        def _():
            dbias_ref[...] = jnp.zeros_like(dbias_ref)
            ck_ref[...] = jnp.zeros_like(ck_ref)
            cv_ref[...] = jnp.zeros_like(cv_ref)

        @pl.when(b < nblk)
        def _():
            col = lax.broadcasted_iota(jnp.int32, (1, 2 * B), 1)
            pen = jnp.where((col < B) & no_prev, NEG_INF, 0.0)
            heads = [slice(h * hd, (h + 1) * hd) for h in range(nh)]
            for h, hs in enumerate(heads):
                kw = jnp.concatenate([kp_ref[0, :, hs], kc_ref[0, :, hs]], axis=0)
                vw = jnp.concatenate([vp_ref[0, :, hs], vc_ref[0, :, hs]], axis=0)
                s_ref[h] = _dot(q_ref[0, :, hs], kw, NT)
                dp_ref[h] = _dot(do_ref[0, :, hs], vw, NT)
            for h, hs in enumerate(heads):
                lse_col = lse_ref[0, :, h * hd:h * hd + 1]
                d_col = dv_ref[0, :, h * hd:h * hd + 1]
                p = jnp.exp(s_ref[h] * scale + bias_ref[0, h] + pen - lse_col)
                ds = p * (dp_ref[h] + d_col)
                dbias_ref[0, h] += ds
                p_ref[h] = p.astype(BF16)
                ds_ref[h] = ds.astype(BF16)
            for h, hs in enumerate(heads):
                qh = q_ref[0, :, hs]
                kw = jnp.concatenate([kp_ref[0, :, hs], kc_ref[0, :, hs]], axis=0)
                dq_ref[0, :, hs] = (_dot(ds_ref[h], kw, NN) * scale).astype(BF16)
                dkw = _dot(ds_ref[h], qh, TN) * scale
                dvw = _dot(p_ref[h], do_ref[0, :, hs], TN)
                dk_ref[0, :, hs] = (ck_ref[:, hs] + dkw[:B]).astype(BF16)
                dvo_ref[0, :, hs] = (cv_ref[:, hs] + dvw[:B]).astype(BF16)
                ck_ref[:, hs] = dkw[B:]
                cv_ref[:, hs] = dvw[B:]

        @pl.when(b == nblk)
        def _():
            dk_ref[0] = ck_ref[...].astype(BF16)
            dvo_ref[0] = cv_ref[...].astype(BF16)

    last = nblk - 1
    cur = pl.BlockSpec((1, B, gw), lambda g, b: (g, jnp.minimum(b, last), 0))
    prev = pl.BlockSpec((1, B, gw), lambda g, b: (g, jnp.clip(b - 1, 0, last), 0))
    tab = pl.BlockSpec((1, nh, B, 2 * B), lambda g, b: (g, 0, 0, 0))
    sd = jax.ShapeDtypeStruct
    return pl.pallas_call(
        body, name=name, grid=(ng, nblk + 1),
        in_specs=[cur, cur, prev, cur, prev, cur, cur, cur, tab],
        out_specs=[cur, prev, prev, tab],
        out_shape=[sd(q.shape, BF16), sd(q.shape, BF16), sd(q.shape, BF16), sd(biasm.shape, F32)],
        scratch_shapes=[pltpu.VMEM((B, gw), F32), pltpu.VMEM((B, gw), F32), pltpu.VMEM((nh, B, 2 * B), F32),
                        pltpu.VMEM((nh, B, 2 * B), F32), pltpu.VMEM((nh, B, 2 * B), BF16), pltpu.VMEM((nh, B, 2 * B), BF16)],
        compiler_params=_params("parallel", "arbitrary"),
    )(q, k, k, v, v, do, dvec, lse, biasm)


def _attn_mix(os, lses, z, name):
    T, gw = os[0].shape
    C = z.shape[1]
    tm = _pick(T, 512, SUBLANES)

    def body(o0_ref, o1_ref, o2_ref, l0_ref, l1_ref, l2_ref, z_ref, out_ref):
        ls = [l0_ref[...], l1_ref[...], l2_ref[...]]
        mx = jnp.maximum(jnp.maximum(ls[0], ls[1]), ls[2])
        es = [jnp.exp(l - mx) for l in ls]
        den = es[0] + es[1] + es[2]
        for i, o_ref in enumerate((o0_ref, o1_ref, o2_ref)):
            sz, _ = _silu_and_grad(z_ref[:, i * gw:(i + 1) * gw].astype(F32))
            out_ref[:, i * gw:(i + 1) * gw] = (o_ref[...] * (es[i] / den) * sz).astype(BF16)

    row = pl.BlockSpec((tm, C), lambda i: (i, 0))
    grp = pl.BlockSpec((tm, gw), lambda i: (i, 0))
    return pl.pallas_call(
        body, name=name, grid=(T // tm,), in_specs=[grp] * 6 + [row], out_specs=row,
        out_shape=jax.ShapeDtypeStruct((T, C), BF16), compiler_params=_params("parallel"),
    )(*os, *lses, z)


def _attn_mix_bwd(dout, os, lses, z, dproj, col, name):
    T, gw = os[0].shape
    C = z.shape[1]
    tm = _pick(T, 512, SUBLANES)
    head_of = np.arange(gw) // ATTN_HEAD_DIM
    ones = jnp.asarray(head_of[:, None] == head_of[None, :], BF16)

    def body(dout_ref, o0_ref, o1_ref, o2_ref, l0_ref, l1_ref, l2_ref, z_ref, ones_ref, _,
             dz_ref, do0_ref, do1_ref, do2_ref, dv0_ref, dv1_ref, dv2_ref):
        ls = [l0_ref[...], l1_ref[...], l2_ref[...]]
        mx = jnp.maximum(jnp.maximum(ls[0], ls[1]), ls[2])
        es = [jnp.exp(l - mx) for l in ls]
        den = es[0] + es[1] + es[2]
        alphas, ebar = [], 0.0
        for i, (o_ref, do_ref) in enumerate(((o0_ref, do0_ref), (o1_ref, do1_ref), (o2_ref, do2_ref))):
            sl = slice(i * gw, (i + 1) * gw)
            alpha = es[i] / den
            ov = o_ref[...]
            dv = dout_ref[:, sl]
            sz, dsz = _silu_and_grad(z_ref[:, sl].astype(F32))
            dz_ref[:, sl] = (dv * ov * alpha * dsz).astype(BF16)
            da = dv * sz
            do_ref[...] = da * alpha
            t = da * ov
            t1 = t.astype(BF16)
            r1 = t - t1.astype(F32)
            t2 = r1.astype(BF16)
            t3 = (r1 - t2.astype(F32)).astype(BF16)
            e = _dot(t1, ones_ref[...], NN) + _dot(t2, ones_ref[...], NN) + _dot(t3, ones_ref[...], NN)
            ebar = ebar + alpha * e
            alphas.append(alpha)
        for alpha, dv_ref in zip(alphas, (dv0_ref, dv1_ref, dv2_ref)):
            dv_ref[...] = -alpha * ebar

    row = pl.BlockSpec((tm, C), lambda i: (i, 0))
    grp = pl.BlockSpec((tm, gw), lambda i: (i, 0))
    sd = jax.ShapeDtypeStruct
    res = pl.pallas_call(
        body, name=name, grid=(T // tm,),
        in_specs=[row] + [grp] * 6 + [row, pl.BlockSpec((gw, gw), lambda i: (0, 0)), pl.BlockSpec(memory_space=pl.ANY)],
        out_specs=[pl.BlockSpec((tm, C), lambda i: (i, col // C))] + [grp] * 6,
        out_shape=[sd(dproj.shape, BF16)] + [sd((T, gw), F32)] * 6, input_output_aliases={9: 0},
        compiler_params=_params("parallel"),
    )(dout, *os, *lses, z, ones, dproj)
    return res[0], res[1:4], res[4:7]


def _mem_attn(qz, kv, name):
    T = qz.shape[0]
    dm = qz.shape[1] // 2
    M = kv.shape[0]
    hd = dm // MEM_HEADS
    scale = hd ** -0.5
    tm = _pick(T, 512, SUBLANES)

    def body(q_ref, z_ref, k_ref, v_ref, o_ref, s_ref, p_ref):
        heads = [slice(h * hd, (h + 1) * hd) for h in range(MEM_HEADS)]
        for h, sl in enumerate(heads):
            s_ref[h] = _dot(q_ref[:, sl].astype(BF16), k_ref[:, sl], NT)
        for h, sl in enumerate(heads):
            s = s_ref[h] * scale
            p = jnp.exp(s - jnp.max(s, axis=-1, keepdims=True))
            p_ref[h] = (p / jnp.sum(p, axis=-1, keepdims=True)).astype(BF16)
        for h, sl in enumerate(heads):
            sz, _ = _silu_and_grad(z_ref[:, sl].astype(F32))
            o_ref[:, sl] = (_dot(p_ref[h], v_ref[:, sl], NN) * sz).astype(BF16)

    return pl.pallas_call(
        body, name=name, grid=(T // tm,),
        in_specs=[pl.BlockSpec((tm, dm), lambda i: (i, 0)), pl.BlockSpec((tm, dm), lambda i: (i, 1)),
                  pl.BlockSpec((M, dm), lambda i: (0, 0)), pl.BlockSpec((M, dm), lambda i: (0, 1))],
        out_specs=pl.BlockSpec((tm, dm), lambda i: (i, 0)),
        out_shape=jax.ShapeDtypeStruct((T, dm), BF16),
        scratch_shapes=[pltpu.VMEM((MEM_HEADS, tm, M), F32), pltpu.VMEM((MEM_HEADS, tm, M), BF16)],
        compiler_params=_params("parallel"),
    )(qz, qz, kv, kv)


def _mem_attn_bwd(do, qz, kv, dproj, col, name):
    T = qz.shape[0]
    dm = qz.shape[1] // 2
    M = kv.shape[0]
    hd = dm // MEM_HEADS
    scale = hd ** -0.5
    tm = _pick(T, 512, SUBLANES)

    def body(do_ref, q_ref, z_ref, k_ref, v_ref, _, dq_ref, dz_ref, dk_ref, dv_ref, s_ref, dp_ref, p_ref, ds_ref, dob_ref):
        @pl.when(pl.program_id(0) == 0)
        def _():
            dk_ref[...] = jnp.zeros_like(dk_ref)
            dv_ref[...] = jnp.zeros_like(dv_ref)

        heads = [slice(h * hd, (h + 1) * hd) for h in range(MEM_HEADS)]
        for h, sl in enumerate(heads):
            sz, _ = _silu_and_grad(z_ref[:, sl].astype(F32))
            dob = (do_ref[:, sl] * sz).astype(BF16)
            dob_ref[:, sl] = dob
            s_ref[h] = _dot(q_ref[:, sl].astype(BF16), k_ref[:, sl], NT)
            dp_ref[h] = _dot(dob, v_ref[:, sl], NT)
        for h, sl in enumerate(heads):
            s = s_ref[h] * scale
            p = jnp.exp(s - jnp.max(s, axis=-1, keepdims=True))
            pn = p / jnp.sum(p, axis=-1, keepdims=True)
            dp = dp_ref[h]
            p_ref[h] = pn.astype(BF16)
            ds_ref[h] = (pn * (dp - jnp.sum(dp * pn, axis=-1, keepdims=True))).astype(BF16)
        for h, sl in enumerate(heads):
            _, dsz = _silu_and_grad(z_ref[:, sl].astype(F32))
            dz_ref[:, sl] = (do_ref[:, sl] * _dot(p_ref[h], v_ref[:, sl], NN) * dsz).astype(BF16)
            dq_ref[:, sl] = (_dot(ds_ref[h], k_ref[:, sl], NN) * scale).astype(BF16)
            dk_ref[:, sl] += _dot(ds_ref[h], q_ref[:, sl].astype(BF16), TN) * scale
            dv_ref[:, sl] += _dot(p_ref[h], dob_ref[:, sl], TN)

    rowq = pl.BlockSpec((tm, dm), lambda i: (i, 0))
    rowz = pl.BlockSpec((tm, dm), lambda i: (i, 1))
    kb = pl.BlockSpec((M, dm), lambda i: (0, 0))
    vb = pl.BlockSpec((M, dm), lambda i: (0, 1))
    sd = jax.ShapeDtypeStruct
    dq, dz, dk, dv = pl.pallas_call(
        body, name=name, grid=(T // tm,), in_specs=[rowq, rowq, rowz, kb, vb, pl.BlockSpec(memory_space=pl.ANY)],
        out_specs=[pl.BlockSpec((tm, dm), lambda i: (i, col // dm)), rowq, kb, kb],
        out_shape=[sd(dproj.shape, BF16), sd((T, dm), BF16), sd((M, dm), F32), sd((M, dm), F32)],
        scratch_shapes=[pltpu.VMEM((MEM_HEADS, tm, M), F32), pltpu.VMEM((MEM_HEADS, tm, M), F32),
                        pltpu.VMEM((MEM_HEADS, tm, M), BF16), pltpu.VMEM((MEM_HEADS, tm, M), BF16), pltpu.VMEM((tm, dm), BF16)],
        input_output_aliases={5: 0}, compiler_params=_params("arbitrary"),
    )(do, qz, qz, kv, kv, dproj)
    return dq, dz, dk, dv


def _merge(os, ws, L, logits, b_gate, name):
    T = os[0].shape[0]
    D = ws[0].shape[2]
    tm = _pick(T, 512, SUBLANES)

    def body(o0_ref, o1_ref, o2_ref, w0_ref, w1_ref, w2_ref, l_ref, b_ref, m_ref, p_ref):
        acc = 0.0
        for i, (o_ref, w_ref) in enumerate(((o0_ref, w0_ref), (o1_ref, w1_ref), (o2_ref, w2_ref))):
            sl = slice(i * D, (i + 1) * D)
            bp = _dot(o_ref[...], w_ref[...], NN)
            p_ref[i] = bp.astype(BF16)
            acc = acc + _sigmoid(l_ref[:, sl].astype(F32) + b_ref[:, sl]) * bp
        m_ref[...] = acc.astype(BF16)

    return pl.pallas_call(
        body, name=name, grid=(T // tm,),
        in_specs=[pl.BlockSpec((tm, o.shape[1]), lambda i: (i, 0)) for o in os]
        + [pl.BlockSpec((None,) + w.shape[1:], lambda i: (L, 0, 0)) for w in ws]
        + [pl.BlockSpec((tm, 3 * D), lambda i: (i, 0)), pl.BlockSpec((1, 3 * D), lambda i: (0, 0))],
        out_specs=[pl.BlockSpec((tm, D), lambda i: (i, 0)), pl.BlockSpec((3, tm, D), lambda i: (0, i, 0))],
        out_shape=[jax.ShapeDtypeStruct((T, D), BF16), jax.ShapeDtypeStruct((3, T, D), BF16)],
        compiler_params=_params("parallel"),
    )(*os, *ws, logits, b_gate.reshape(1, 3 * D))


def _merge_bwd(dmerged, bps, logits, b_gate, dproj_cols, dl_off, name):
    _, T, D = bps.shape
    tm = _pick(T, 512, SUBLANES)
    cw = _pick(math.gcd(dl_off, D), 512, LANES)
    per = D // cw

    def body(dm_ref, p_ref, l_ref, b_ref, dl_ref, d_ref, db_ref):
        @pl.when(pl.program_id(1) == 0)
        def _():
            db_ref[...] = jnp.zeros_like(db_ref)

        dmv = dm_ref[...]
        gt = _sigmoid(l_ref[...].astype(F32) + b_ref[...])
        d_ref[...] = (dmv * gt).astype(BF16)
        dl = dmv * p_ref[...].astype(F32) * gt * (1.0 - gt)
        dl_ref[...] = dl.astype(BF16)
        db_ref[...] += jnp.sum(dl, axis=0, keepdims=True)

    stacked = pl.BlockSpec((None, tm, cw), lambda j, i: (j // per, i, j % per))
    sd = jax.ShapeDtypeStruct
    return pl.pallas_call(
        body, name=name, grid=(3 * per, T // tm),
        in_specs=[pl.BlockSpec((tm, cw), lambda j, i: (i, j % per)), stacked, pl.BlockSpec((tm, cw), lambda j, i: (i, j)),
                  pl.BlockSpec((1, cw), lambda j, i: (0, j))],
        out_specs=[pl.BlockSpec((tm, cw), lambda j, i: (i, dl_off // cw + j)), stacked, pl.BlockSpec((1, cw), lambda j, i: (0, j))],
        out_shape=[sd((T, dproj_cols), BF16), sd((3, T, D), BF16), sd((1, 3 * D), F32)],
        compiler_params=_params("parallel", "arbitrary"),
    )(dmerged, bps, logits, b_gate.reshape(1, 3 * D))


def _to_segments(a):
    T, C = a.shape
    return a.reshape(SSM_SEGMENTS, T // SSM_SEGMENTS, C).transpose(1, 0, 2).reshape(T, C)


def _from_segments(a):
    T, C = a.shape
    return a.reshape(T // SSM_SEGMENTS, SSM_SEGMENTS, C).transpose(1, 0, 2).reshape(T, C)


def _to_residues(a):
    T = a.shape[0]
    gw = HEADS_PER_GROUP * ATTN_HEAD_DIM
    out = []
    for g, (_, r) in enumerate(ATTN_CONFIGS):
        ag = a[:, g * gw:(g + 1) * gw].reshape(T // r, r, gw)
        out.append(ag.transpose(1, 0, 2).reshape(T, gw))
    return jnp.stack(out)


def _from_residues(a):
    _, T, gw = a.shape
    out = []
    for g, (_, r) in enumerate(ATTN_CONFIGS):
        out.append(a[g].reshape(r, T // r, gw).transpose(1, 0, 2).reshape(T, gw))
    return jnp.concatenate(out, axis=1)


def _block_diag(w):
    nblk, ng, a, b = w.shape
    eye = jnp.eye(ng, dtype=w.dtype)
    return (w[:, :, :, None, :] * eye[None, :, None, :, None]).reshape(nblk, ng * a, ng * b)


def _block_diag_part(m, a, b):
    nblk = m.shape[0]
    ng = m.shape[1] // a
    m5 = m.reshape(nblk, ng, a, ng, b)
    eye = jnp.eye(ng, dtype=m.dtype)
    return jnp.sum(m5 * eye[None, :, None, :, None], axis=3)


def _ssm_matrices(p, L, tag):
    G, P = p["ssm_lambda_re"].shape[1:]
    Hg = SSM_GROUP
    gpb = SSM_BLOCK_CH // Hg
    nblk = G // gpb
    br = p["ssm_b_re"][L].transpose(2, 0, 1)
    bi = p["ssm_b_im"][L].transpose(2, 0, 1)
    disc_in = (p["ssm_lambda_re"][L], p["ssm_lambda_im"][L], p["ssm_log_dt"][L].reshape(G, 1), br, bi)
    ar, ai, bbr, bbi = _ssm_disc(*disc_in, name=f"ssm_disc_{tag}")
    amat = (ar.reshape(nblk // 2, SUBLANES, LANES), ai.reshape(nblk // 2, SUBLANES, LANES))
    bbr_g = bbr.transpose(1, 0, 2).reshape(nblk, gpb, Hg, P)
    bbi_g = bbi.transpose(1, 0, 2).reshape(nblk, gpb, Hg, P)
    bmat = jnp.concatenate([_block_diag(bbr_g), _block_diag(bbi_g)], axis=2).astype(BF16)
    cre = p["ssm_c_re"][L].reshape(nblk, gpb, Hg, P).transpose(0, 1, 3, 2)
    cim = p["ssm_c_im"][L].reshape(nblk, gpb, Hg, P).transpose(0, 1, 3, 2)
    cmat = jnp.concatenate([_block_diag(cre), -_block_diag(cim)], axis=1).astype(BF16)
    return disc_in, amat, bmat, cmat


def _layer_fwd(x, mem, p, wb, L, biasm):
    T, D = x.shape
    C = p["ssm_d"].shape[1]
    dm = wb["w_br_mem"].shape[1]
    tag = f"l{L}"
    s = {"x": x}
    h = _rmsnorm(x, p["norm_g"][L], f"norm_{tag}")
    offs = [int(o) for o in np.cumsum([0, C, C, 3 * 768, 768, 2 * dm, 3 * D])]
    names = ("uz", "z_ssm", "qkv", "z_attn", "qz_mem", "logits")
    dts = (F32, BF16, F32, BF16, BF16, BF16)
    for i, (nm, dt) in enumerate(zip(names, dts)):
        s[nm] = _matmul(h, wb["w_in"], mode="nn", name=f"in_{nm}_{tag}", out_dtype=dt, b_lead=L, b_off=offs[i],
                        n_cols=offs[i + 1] - offs[i])
    s["h"] = h

    disc_in, amat, bmat, cmat = _ssm_matrices(p, L, tag)
    dvec = p["ssm_d"][L].reshape(1, C)
    y, xr, xi = _ssm_fwd(s["uz"], bmat, cmat, *amat, dvec, f"ssm_scan_{tag}")
    o_ssm, a_glu = _ssm_post(y, s["z_ssm"], wb["w_glu"][L], p["b_glu"][L], f"ssm_post_{tag}")
    s.update(disc_in=disc_in, amat=amat, bmat=bmat, cmat=cmat, xr=xr, xi=xi, y=y, a_glu=a_glu, o_ssm=o_ssm)

    groups = [_attn_group_fwd(s["qkv"], biasm, g, f"attn_g{g}_{tag}") for g in range(len(ATTN_CONFIGS))]
    os, lses = [o for o, _ in groups], [l for _, l in groups]
    o_attn = _attn_mix(os, lses, s["z_attn"], f"attn_mix_{tag}")
    s.update(os=os, lses=lses, o_attn=o_attn)

    mn = _rmsnorm(mem, p["mem_norm_g"][L], f"mem_norm_{tag}")
    kv = _matmul(mn, wb["w_mem_kv"], mode="nn", name=f"mem_kv_{tag}", out_dtype=BF16, b_lead=L)
    o_mem = _mem_attn(s["qz_mem"], kv, f"mem_attn_{tag}")
    s.update(mn=mn, kv=kv, o_mem=o_mem)

    merged, bps = _merge([o_ssm, o_attn, o_mem], [wb["w_br_ssm"], wb["w_br_attn"], wb["w_br_mem"]], L, s["logits"],
                         p["b_gate"][L], f"merge_{tag}")
    s.update(bps=bps, merged=merged)
    x_new = _matmul(merged, wb["w_out"], mode="nn", name=f"out_{tag}", add=x, b_lead=L)
    return x_new, s


def _layer_bwd(dx, mem, p, wb, L, s, biasm, gprev):
    T, D = dx.shape
    C = p["ssm_d"].shape[1]
    depth = p["norm_g"].shape[0]
    tag = f"l{L}"
    g = {}

    def wgrad(n, a, b, **tiles):
        g[n] = _matmul(a, b, mode="tn", name=f"d{n}_{tag}", out_dtype=BF16, stack=(L, depth, gprev.get(n)), **tiles)

    dmerged = _matmul(dx, wb["w_out"], mode="nt", name=f"d_merged_{tag}", b_lead=L)
    wgrad("w_out", s["merged"], dx)
    dm = s["qz_mem"].shape[1] // 2
    col = dict(zip(("u", "z_ssm", "q", "k", "v", "z_attn", "q_mem", "z_mem", "logits", "end"),
                   (int(o) for o in np.cumsum([0, C, C, 768, 768, 768, 768, dm, dm, 3 * D]))))
    dproj, dbps, g["b_gate"] = _merge_bwd(dmerged, s["bps"], s["logits"], p["b_gate"][L], col["end"], col["logits"],
                                          f"merge_bwd_{tag}")
    dos = []
    for i, (o, n) in enumerate(((s["o_ssm"], "w_br_ssm"), (s["o_attn"], "w_br_attn"), (s["o_mem"], "w_br_mem"))):
        dos.append(_matmul(dbps, wb[n], mode="nt", name=f"d_o_{n}_{tag}", a_lead=i, b_lead=L))
        g[n] = _matmul(o, dbps, mode="tn", name=f"d{n}_{tag}", out_dtype=BF16, b_lead=i, stack=(L, depth, gprev.get(n)))

    dy, dproj, ds_glu, g["b_glu"] = _ssm_post_bwd(dos[0], s["y"], s["z_ssm"], wb["w_glu"][L], p["b_glu"][L], dproj,
                                                  col["z_ssm"], f"ssm_post_bwd_{tag}")
    wgrad("w_glu", s["a_glu"], ds_glu)
    dvec = p["ssm_d"][L].reshape(1, C)
    dproj, dbm, dct, dar, dai, g["ssm_d"] = _ssm_bwd(dy, s["uz"], s["xr"], s["xi"], s["bmat"], s["cmat"], *s["amat"], dvec,
                                                     dproj, f"ssm_scan_bwd_{tag}")
    G, P = p["ssm_lambda_re"].shape[1:]
    Hg = SSM_GROUP
    half = dbm.shape[2] // 2
    dbbr = _block_diag_part(dbm[:, :, :half], Hg, P).reshape(G, Hg, P).transpose(1, 0, 2)
    dbbi = _block_diag_part(dbm[:, :, half:], Hg, P).reshape(G, Hg, P).transpose(1, 0, 2)
    g["ssm_c_re"] = _block_diag_part(dct[:, :, :half], Hg, P).reshape(G, Hg, P)
    g["ssm_c_im"] = -_block_diag_part(dct[:, :, half:], Hg, P).reshape(G, Hg, P)
    glre, glim, gdt, gbr, gbi = _ssm_disc_bwd(*s["disc_in"], dar.reshape(G, P), dai.reshape(G, P), dbbr, dbbi,
                                              name=f"ssm_disc_bwd_{tag}")
    g["ssm_lambda_re"], g["ssm_lambda_im"], g["ssm_log_dt"] = glre, glim, gdt.reshape(G)
    g["ssm_b_re"] = gbr.transpose(1, 2, 0)
    g["ssm_b_im"] = gbi.transpose(1, 2, 0)

    dproj, do_g, dvec_g = _attn_mix_bwd(dos[1], s["os"], s["lses"], s["z_attn"], dproj, col["z_attn"], f"attn_mix_bwd_{tag}")
    rest, dbias = [], []
    for gi in range(len(ATTN_CONFIGS)):
        dq_g, dproj, dv_g, db_g = _attn_group_bwd(s["qkv"], do_g[gi], dvec_g[gi], s["lses"][gi], biasm, gi, dproj, col["k"],
                                                  f"attn_bwd_g{gi}_{tag}")
        gw = dq_g.shape[1]
        rest += [(dq_g, col["q"] + gi * gw), (dv_g, col["v"] + gi * gw)]
        dbias.append(db_g)
    dbias = jnp.stack(dbias)

    dproj, dz_mem, dk_mem, dv_mem = _mem_attn_bwd(dos[2], s["qz_mem"], s["kv"], dproj, col["q_mem"], f"mem_attn_bwd_{tag}")
    rest.append((dz_mem, col["z_mem"]))
    for piece, at in rest:
        dproj = lax.dynamic_update_slice(dproj, piece.astype(BF16), (0, at))
    dkv = jnp.concatenate([dk_mem, dv_mem], axis=1)
    wgrad("w_mem_kv", s["mn"], dkv)
    dmn = _matmul(dkv, wb["w_mem_kv"], mode="nt", name=f"d_mn_{tag}", b_lead=L)
    _, g["mem_norm_g"] = _rmsnorm_bwd(mem, p["mem_norm_g"][L], dmn, None, f"mem_norm_bwd_{tag}")

    dh = _matmul(dproj, wb["w_in"], mode="nt", name=f"d_h_{tag}", b_lead=L)
    wgrad("w_in", s["h"], dproj, tn=2304, tk=1024)
    dx_in, g["norm_g"] = _rmsnorm_bwd(s["x"], p["norm_g"][L], dh, dx, f"norm_bwd_{tag}")
    return dx_in, g, dbias


def _bucket_onehot(gi):
    buckets, bands = _band_tables()
    hit = (buckets[gi].reshape(1, -1) == jnp.arange(NUM_BUCKETS)[:, None]) & bands[gi].reshape(1, -1)
    return hit.astype(BF16)


def _bias_tables(rel_bias, name):
    _, bands = _band_tables()
    out = []
    for gi in range(len(ATTN_CONFIGS)):
        tab = rel_bias[:, gi * HEADS_PER_GROUP:(gi + 1) * HEADS_PER_GROUP].T
        flat = _matmul(tab, _bucket_onehot(gi), mode="nn", name=f"{name}_{gi}", split_a=3, tn=4096)
        out.append(jnp.where(bands[gi][None], flat.reshape(HEADS_PER_GROUP, ATTN_BLOCK, 2 * ATTN_BLOCK), NEG_INF))
    return jnp.stack(out)


def _rel_bias_grad(dbias_sum, name):
    cols = []
    for gi in range(len(ATTN_CONFIGS)):
        flat = dbias_sum[gi].reshape(HEADS_PER_GROUP, -1)
        cols.append(_matmul(flat, _bucket_onehot(gi), mode="nt", name=f"{name}_{gi}", split_a=2, tk=4096).T)
    return jnp.concatenate(cols, axis=1)


def _local_step(x, mem, target, p, wb):
    depth = p["norm_g"].shape[0]
    biasm = _bias_tables(p["rel_bias"], "bias_table")
    saved = []
    for L in range(depth):
        x, s = _layer_fwd(x, mem, p, wb, L, biasm)
        saved.append(s)
    loss_vec, dx, dgf = _loss_head(x, p["final_norm_g"], target, "loss_head")
    grads = {"final_norm_g": dgf.reshape(-1)}
    per_layer = [None] * depth
    dbias_sum = 0.0
    stacked = {}
    for L in reversed(range(depth)):
        dx, per_layer[L], dbias = _layer_bwd(dx, mem, p, wb, L, saved[L], biasm, stacked)
        stacked = {n: per_layer[L][n] for n, _ in BIG}
        dbias_sum = dbias_sum + dbias
    grads.update(stacked)
    for n in per_layer[0]:
        if n not in stacked:
            grads[n] = jnp.stack([per_layer[L][n].reshape(p[n].shape[1:]) for L in range(depth)])
    grads["rel_bias"] = _rel_bias_grad(dbias_sum, "d_rel_bias")
    return jnp.sum(loss_vec), dx, grads


def _chip_coords(j):
    return j // 2, j % 2


def _place_shard(shard, ax, chip, name):
    _, a, b = shard.shape
    ra = _pick(a, 256, 16)
    full = (2, a * N_CHIPS, b) if ax == 1 else (2, a, b * N_CHIPS)
    per = a // ra

    def body(j_ref, s_ref, o_ref):
        o_ref[...] = s_ref[...].astype(BF16)

    out_idx = (lambda l, i, j: (l, j[0] * per + i, 0)) if ax == 1 else (lambda l, i, j: (l, i, j[0]))
    return pl.pallas_call(
        body, name=name,
        grid_spec=pltpu.PrefetchScalarGridSpec(
            num_scalar_prefetch=1, grid=(2, per),
            in_specs=[pl.BlockSpec((None, ra, b), lambda l, i, j: (l, i, 0))],
            out_specs=pl.BlockSpec((None, ra, b), out_idx)),
        out_shape=jax.ShapeDtypeStruct(full, BF16), compiler_params=_params("parallel", "parallel"),
    )(chip, shard)


def _gather_shards(fulls, axes, name):
    n = len(fulls)
    widths = [a.shape[ax] // N_CHIPS for a, ax in zip(fulls, axes)]
    aligns = [LANES if ax == 2 else 16 for ax in axes]

    def body(*refs):
        outs = refs[n:2 * n]
        send_sems, recv_sems, fsend_sems, frecv_sems = refs[2 * n:]
        x, y, c = lax.axis_index("x"), lax.axis_index("y"), lax.axis_index("c")
        mine = 2 * x + y
        sibling = (x, y, 1 - c)

        def window(t, layer, j):
            start = pl.ds(pl.multiple_of(j * widths[t], aligns[t]), widths[t])
            return outs[t].at[(layer, start, slice(None)) if axes[t] == 1 else (layer, slice(None), start)]

        def over_ici(t, j, block):
            return pltpu.make_async_remote_copy(
                src_ref=window(t, c, mine), dst_ref=window(t, c, block), send_sem=send_sems.at[t, j],
                recv_sem=recv_sems.at[t, block], device_id=(*_chip_coords(j), c), device_id_type=MESH)

        def over_d2d(t, j, layer):
            return pltpu.make_async_remote_copy(
                src_ref=window(t, layer, j), dst_ref=window(t, layer, j), send_sem=fsend_sems.at[t, j],
                recv_sem=frecv_sems.at[t, j], device_id=sibling, device_id_type=MESH)

        for t in range(n):
            for j in range(N_CHIPS):
                @pl.when(j != mine)
                def _():
                    over_ici(t, j, mine).start()
        for t in range(n):
            for j in range(N_CHIPS):
                @pl.when(j != mine)
                def _():
                    over_ici(t, j, j).wait_recv()
                    over_d2d(t, j, c).start()
        for t in range(n):
            for j in range(N_CHIPS):
                @pl.when(j != mine)
                def _():
                    over_ici(t, j, mine).wait_send()
                    over_d2d(t, j, c).wait_send()
                    over_d2d(t, j, 1 - c).wait_recv()

    sem = pltpu.SemaphoreType.DMA
    return pl.pallas_call(
        body, name=name, in_specs=[HBM] * n, out_specs=[HBM] * n,
        out_shape=[jax.ShapeDtypeStruct(a.shape, a.dtype) for a in fulls],
        input_output_aliases={t: t for t in range(n)},
        scratch_shapes=[sem((n, N_CHIPS)), sem((n, N_CHIPS)), sem((n, N_CHIPS)), sem((n, N_CHIPS))],
    )(*fulls)


def _scatter_slices(arrays, axes, name):
    n = len(arrays)

    def piece(a, ax):
        if ax is None:
            return a.shape, None
        w = a.shape[ax] // N_CHIPS
        return a.shape[:ax] + (w,) + a.shape[ax + 1:], w

    shapes = [piece(a, ax) for a, ax in zip(arrays, axes)]

    def body(*refs):
        ins, outs = refs[:n], refs[n:2 * n]
        send_sems, recv_sems, loc_sems = refs[2 * n:]
        x, y, c = lax.axis_index("x"), lax.axis_index("y"), lax.axis_index("c")
        mine = 2 * x + y

        def src(t, j):
            ax, w = axes[t], shapes[t][1]
            if ax is None:
                return ins[t]
            idx = tuple(pl.ds(j * w, w) if d == ax else slice(None) for d in range(len(arrays[t].shape)))
            return ins[t].at[idx]

        for t in range(n):
            for j in range(N_CHIPS):
                @pl.when(j == mine)
                def _():
                    pltpu.make_async_copy(src(t, j), outs[t].at[j], loc_sems.at[t]).start()

                @pl.when(j != mine)
                def _():
                    pltpu.make_async_remote_copy(
                        src_ref=src(t, j), dst_ref=outs[t].at[mine], send_sem=send_sems.at[t, j], recv_sem=recv_sems.at[t, mine],
                        device_id=(*_chip_coords(j), c), device_id_type=MESH).start()
        for t in range(n):
            for j in range(N_CHIPS):
                @pl.when(j == mine)
                def _():
                    pltpu.make_async_copy(src(t, j), outs[t].at[j], loc_sems.at[t]).wait()

                @pl.when(j != mine)
                def _():
                    cp = pltpu.make_async_remote_copy(
                        src_ref=src(t, j), dst_ref=outs[t].at[j], send_sem=send_sems.at[t, j], recv_sem=recv_sems.at[t, j],
                        device_id=(*_chip_coords(j), c), device_id_type=MESH)
                    cp.wait_send()
                    cp.wait_recv()

    return pl.pallas_call(
        body, name=name, in_specs=[HBM] * n, out_specs=[HBM] * n,
        out_shape=[jax.ShapeDtypeStruct((N_CHIPS,) + sh, a.dtype) for a, (sh, _) in zip(arrays, shapes)],
        scratch_shapes=[pltpu.SemaphoreType.DMA((n, N_CHIPS)), pltpu.SemaphoreType.DMA((n, N_CHIPS)), pltpu.SemaphoreType.DMA((n,))],
    )(*arrays)


def _swap_layers(stacked, name):
    n = len(stacked)

    def body(*refs):
        ins, outs = refs[:n], refs[n:2 * n]
        send_sems, recv_sems = refs[2 * n:]
        c = lax.axis_index("c")
        peer = (lax.axis_index("x"), lax.axis_index("y"), 1 - c)
        cps = [pltpu.make_async_remote_copy(src_ref=ins[t].at[1 - c], dst_ref=outs[t], send_sem=send_sems.at[t],
                                            recv_sem=recv_sems.at[t], device_id=peer, device_id_type=MESH) for t in range(n)]
        for cp in cps:
            cp.start()
        for cp in cps:
            cp.wait_send()
            cp.wait_recv()

    return pl.pallas_call(
        body, name=name, in_specs=[HBM] * n, out_specs=[HBM] * n,
        out_shape=[jax.ShapeDtypeStruct(a.shape[1:], a.dtype) for a in stacked],
        scratch_shapes=[pltpu.SemaphoreType.DMA((n,)), pltpu.SemaphoreType.DMA((n,))],
    )(*stacked)


def _merge_layers(stacked, name):
    n = len(stacked)

    def body(*refs):
        outs = refs[n:2 * n]
        send_sems, recv_sems = refs[2 * n:]
        c = lax.axis_index("c")
        peer = (lax.axis_index("x"), lax.axis_index("y"), 1 - c)
        for t in range(n):
            pltpu.make_async_remote_copy(src_ref=outs[t].at[c], dst_ref=outs[t].at[c], send_sem=send_sems.at[t],
                                         recv_sem=recv_sems.at[t], device_id=peer, device_id_type=MESH).start()
        for t in range(n):
            cp = pltpu.make_async_remote_copy(src_ref=outs[t].at[c], dst_ref=outs[t].at[1 - c], send_sem=send_sems.at[t],
                                              recv_sem=recv_sems.at[t], device_id=peer, device_id_type=MESH)
            cp.wait_send()
            cp.wait_recv()

    sem = pltpu.SemaphoreType.DMA
    return pl.pallas_call(
        body, name=name, in_specs=[HBM] * n, out_specs=[HBM] * n,
        out_shape=[jax.ShapeDtypeStruct(a.shape, a.dtype) for a in stacked],
        input_output_aliases={t: t for t in range(n)}, scratch_shapes=[sem((n,)), sem((n,))],
    )(*stacked)


def _pair_sum(stacked, landed, core, name):
    _, K, N = stacked.shape
    tr = _pick(K, max(16, (1 << 19) // N // 16 * 16), 16)

    def body(c_ref, s_ref, l_ref, o_ref):
        o_ref[...] = (s_ref[...].astype(F32) + l_ref[...].astype(F32)).astype(o_ref.dtype)

    return pl.pallas_call(
        body, name=name,
        grid_spec=pltpu.PrefetchScalarGridSpec(
            num_scalar_prefetch=1, grid=(K // tr,),
            in_specs=[pl.BlockSpec((None, tr, N), lambda i, c: (c[0], i, 0)), pl.BlockSpec((tr, N), lambda i, c: (i, 0))],
            out_specs=pl.BlockSpec((tr, N), lambda i, c: (i, 0))),
        out_shape=jax.ShapeDtypeStruct((K, N), stacked.dtype), compiler_params=_params("parallel"),
    )(core, stacked, landed)


def _sum_chips(landed, core, name):
    _, R, C = landed.shape
    tr = _pick(R, max(SUBLANES, (1 << 19) // C // 16 * 16), 16)

    def body(c_ref, l_ref, o_ref):
        acc = l_ref[0].astype(F32) + l_ref[1].astype(F32)
        acc = acc + l_ref[2].astype(F32)
        o_ref[...] = acc + l_ref[3].astype(F32)

    return pl.pallas_call(
        body, name=name,
        grid_spec=pltpu.PrefetchScalarGridSpec(
            num_scalar_prefetch=1, grid=(R // tr,),
            in_specs=[pl.BlockSpec((N_CHIPS, tr, C), lambda i, c: (0, i, 0))],
            out_specs=pl.BlockSpec((None, tr, C), lambda i, c: (c[0], i, 0))),
        out_shape=jax.ShapeDtypeStruct((2, R, C), F32), compiler_params=_params("parallel"),
    )(core, landed)


def _adamw_math(w_ref, g_ref, m_ref, v_ref, d_ref, nm_ref, nv_ref):
    c1 = 1.0 / (1.0 - ADAM_B1 ** ADAM_STEP)
    c2 = 1.0 / (1.0 - ADAM_B2 ** ADAM_STEP)
    g = g_ref[...]
    nm = ADAM_B1 * m_ref[...] + (1.0 - ADAM_B1) * g
    nv = ADAM_B2 * v_ref[...] + (1.0 - ADAM_B2) * (g * g)
    nm_ref[...] = nm
    nv_ref[...] = nv
    d_ref[...] = -ADAM_LR * ((nm * c1) / (jnp.sqrt(nv * c2) + ADAM_EPS) + ADAM_WD * w_ref[...])


def _adamw_whole(w, g, m, v, name):
    shape = w.shape
    view = (-1,) + shape[-2:] if w.ndim >= 2 else (1, 1, -1)

    def body(*refs):
        _adamw_math(*refs)

    res = pl.pallas_call(body, name=name, out_shape=[jax.ShapeDtypeStruct(w.reshape(view).shape, F32)] * 3,
                         compiler_params=pltpu.CompilerParams(vmem_limit_bytes=VMEM_LIMIT_BYTES))(
        *(a.reshape(view) for a in (w, g, m, v)))
    return [r.reshape(shape) for r in res]


def _adamw(w, g, m, v, name):
    R, C = w.shape
    tr = _pick(R, max(SUBLANES, (1 << 18) // C // 8 * 8), SUBLANES)

    def body(*refs):
        _adamw_math(*refs)

    blk = pl.BlockSpec((tr, C), lambda i: (i, 0))
    return pl.pallas_call(
        body, name=name, grid=(R // tr,), in_specs=[blk] * 4, out_specs=[blk] * 3,
        out_shape=[jax.ShapeDtypeStruct((R, C), F32)] * 3, compiler_params=_params("parallel"),
    )(w, g, m, v)


def _pack_small(d, prefix=""):
    flat = jnp.concatenate([d[prefix + n].astype(F32).reshape(-1) for n in SMALL])
    pad = (-flat.shape[0]) % (2 * 16 * LANES)
    return jnp.pad(flat, (0, pad)).reshape(-1, LANES)


def _unpack_small(packed, shapes):
    flat = packed.reshape(-1)
    out, off = {}, 0
    for n in SMALL:
        size = int(np.prod(shapes[n]))
        out[n] = flat[off:off + size].reshape(shapes[n])
        off += size
    return out


def kernel(*args):
    p = dict(zip(INPUTS, args))
    x, mem, target = p["x"][0], p["mem"][0], p["loss_target"][0]

    names = [n for n, _ in BIG] + ["small"]
    core = lax.axis_index("c").astype(jnp.int32).reshape(1)
    chip = (2 * lax.axis_index("x") + lax.axis_index("y")).astype(jnp.int32).reshape(1)
    placed = [_place_shard(p[n], ax, chip, f"place_{n}") for n, ax in BIG]
    wb = dict(zip(names, _gather_shards(placed, [ax for _, ax in BIG], "gather_weights")))

    loss_part, dx, grads = _local_step(x, mem, target, p, wb)
    loss = lax.psum(loss_part, ("x", "y", "c"))

    stacked = [grads[n] for n, _ in BIG] + [_pack_small(grads).reshape(2, -1, LANES)]
    theirs = _swap_layers(stacked, "swap_layers")
    pair = [_pair_sum(s, o, core, f"pair_sum_{n}") for n, s, o in zip(names, stacked, theirs)]
    landed = _scatter_slices(pair, [ax - 1 for _, ax in BIG] + [None], "scatter_grads")
    reduced = [_sum_chips(ld.reshape(N_CHIPS, -1, ld.shape[-1]), core, f"sum_chips_{n}") for n, ld in zip(names, landed)]
    total = _merge_layers(reduced, "merge_layers")

    out = {}
    for (n, _), g in zip(BIG, total):
        sh = p[n].shape
        two_d = lambda a: a.reshape(-1, sh[-1])
        res = (g,) + tuple(_adamw(two_d(p[n]), two_d(g), two_d(p["m_" + n]), two_d(p["v_" + n]), f"adamw_{n}"))
        for key, r in zip(("grad_", "delta_", "new_m_", "new_v_"), res):
            out[key + n] = r.reshape(sh)
    for n, g in _unpack_small(total[-1], {n: p[n].shape for n in SMALL}).items():
        res = (g,) + tuple(_adamw_whole(p[n], g, p["m_" + n], p["v_" + n], f"adamw_{n}"))
        for key, r in zip(("grad_", "delta_", "new_m_", "new_v_"), res):
            out[key + n] = r

    result = [loss, dx.reshape(p["x"].shape)]
    for key in ("grad_", "delta_", "new_m_", "new_v_"):
        result += [out[key + n] for n in WEIGHTS]
    return tuple(result)
```

```python
import math

import jax
import jax.numpy as jnp
import numpy as np
from jax import lax
from jax.experimental import pallas as pl
from jax.experimental.pallas import tpu as pltpu

F32 = jnp.float32
BF16 = jnp.bfloat16
MESH = pl.DeviceIdType.MESH
HBM = pl.BlockSpec(memory_space=pltpu.HBM)

EPS = 1e-6
SSM_GROUP = 16
SSM_STATE = 64
ATTN_HEAD_DIM = 64
HEADS_PER_GROUP = 4
ATTN_CONFIGS = ((128, 1), (512, 4), (2048, 16))
ATTN_BLOCK = 128
NUM_BUCKETS = 32
REL_MAX_DISTANCE = 2048
NEG_INF = -1e30
MEM_HEADS = 4
ADAM_LR = 0.001
ADAM_B1 = 0.9
ADAM_B2 = 0.999
ADAM_EPS = 1e-08
ADAM_WD = 0.01
ADAM_STEP = 10

LANES = 128
SUBLANES = 8
VMEM_LIMIT_BYTES = 48 * 1024 * 1024
SSM_BLOCK_CH = 128
SSM_SEGMENTS = SUBLANES
SSM_CHUNK_STEPS = 128

N_CHIPS = 4
BIG = (("w_in", 2), ("w_glu", 1), ("w_mem_kv", 1), ("w_br_ssm", 2), ("w_br_attn", 2), ("w_br_mem", 2), ("w_out", 1))
SMALL = ("norm_g", "mem_norm_g", "b_gate", "ssm_lambda_re", "ssm_lambda_im", "ssm_log_dt", "ssm_b_re", "ssm_b_im",
         "ssm_c_re", "ssm_c_im", "ssm_d", "b_glu", "rel_bias", "final_norm_g")
WEIGHTS = ("norm_g", "mem_norm_g", "w_in", "b_gate", "ssm_lambda_re", "ssm_lambda_im", "ssm_log_dt", "ssm_b_re",
           "ssm_b_im", "ssm_c_re", "ssm_c_im", "ssm_d", "w_glu", "b_glu", "w_mem_kv", "w_br_ssm", "w_br_attn",
           "w_br_mem", "w_out", "rel_bias", "final_norm_g")
INPUTS = ("x", "mem") + WEIGHTS + ("loss_target",) + tuple("m_" + n for n in WEIGHTS) + tuple("v_" + n for n in WEIGHTS)


def _params(*sem):
    return pltpu.CompilerParams(dimension_semantics=sem, vmem_limit_bytes=VMEM_LIMIT_BYTES)


def _pick(dim, pref, align):
    if dim <= pref:
        return dim
    t = pref - pref % align
    while t >= align:
        if dim % t == 0:
            return t
        t -= align
    return dim


def _sigmoid(v):
    return 1.0 / (1.0 + jnp.exp(-v))


def _silu_and_grad(z):
    s = _sigmoid(z)
    return z * s, s * (1.0 + z * (1.0 - s))


_GELU_C = math.sqrt(2.0 / math.pi)


def _gelu_and_grad(y):
    inner = _GELU_C * (y + 0.044715 * y * y * y)
    t = jnp.tanh(inner)
    g = 0.5 * y * (1.0 + t)
    dg = 0.5 * (1.0 + t) + 0.5 * y * (1.0 - t * t) * _GELU_C * (1.0 + 3.0 * 0.044715 * y * y)
    return g, dg


def _dot(a, b, dims):
    return lax.dot_general(a, b, (dims, ((), ())), preferred_element_type=F32)


NN = ((1,), (0,))
NT = ((1,), (1,))
TN = ((0,), (0,))


def _matmul(a, b, *, mode, name, out_dtype=F32, add=None, split_a=1, tm=1024, tn=768, tk=2304,
            a_lead=None, b_lead=None, b_off=0, n_cols=None, stack=None):
    ashape = a.shape if a_lead is None else a.shape[1:]
    K, M = ashape if mode == "tn" else ashape[::-1]
    bshape = b.shape if b_lead is None else b.shape[1:]
    N = n_cols or (bshape[0] if mode == "nt" else bshape[1])
    if mode != "tn" and M >= 4 * tm:
        tm = 2 * tm
    tm = _pick(M, tm, LANES if mode == "tn" else SUBLANES)
    tn = _pick(math.gcd(N, b_off) if b_off else N, tn, LANES)
    tk = _pick(K, tk, LANES)
    nk = K // tk
    joff = b_off // tn
    dims = {"nn": NN, "nt": NT, "tn": TN}[mode]
    has_add = add is not None
    has_prev = stack is not None and stack[2] is not None

    def body(*refs):
        a_ref, b_ref = refs[:2]
        add_ref = refs[2] if has_add else None
        o_ref = refs[-2] if nk > 1 else refs[-1]
        k = pl.program_id(2)
        bv = b_ref[...].astype(BF16)
        if split_a > 1:
            rest = a_ref[...].astype(F32)
            part = 0.0
            for _ in range(split_a):
                piece = rest.astype(BF16)
                part = part + _dot(piece, bv, dims)
                rest = rest - piece.astype(F32)
        else:
            part = _dot(a_ref[...].astype(BF16), bv, dims)

        def finish(r):
            if has_add:
                r = r + add_ref[...]
            o_ref[...] = r.astype(out_dtype)

        if nk == 1:
            finish(part)
            return
        acc_ref = refs[-1]

        @pl.when(k == 0)
        def _():
            acc_ref[...] = part

        @pl.when((k > 0) & (k < nk - 1))
        def _():
            acc_ref[...] += part

        @pl.when(k == nk - 1)
        def _():
            finish(acc_ref[...] + part)

    alead = () if a_lead is None else (a_lead,)
    alead_blk = () if a_lead is None else (None,)
    if mode == "tn":
        a_spec = pl.BlockSpec(alead_blk + (tk, tm), lambda i, j, k: alead + (k, i))
    else:
        a_spec = pl.BlockSpec(alead_blk + (tm, tk), lambda i, j, k: alead + (i, k))
    lead = () if b_lead is None else (b_lead,)
    lead_blk = () if b_lead is None else (None,)
    if mode == "nt":
        b_spec = pl.BlockSpec(lead_blk + (tn, tk), lambda i, j, k: lead + (j + joff, k))
    else:
        b_spec = pl.BlockSpec(lead_blk + (tk, tn), lambda i, j, k: lead + (k, j + joff))
    in_specs = [a_spec, b_spec]
    args = [a, b]
    if has_add:
        in_specs.append(pl.BlockSpec((tm, tn), lambda i, j, k: (i, j)))
        args.append(add)
    aliases = {}
    if stack is None:
        out_spec = pl.BlockSpec((tm, tn), lambda i, j, k: (i, j))
        out_shape = jax.ShapeDtypeStruct((M, N), out_dtype)
    else:
        layer, depth, prev = stack
        out_spec = pl.BlockSpec((None, tm, tn), lambda i, j, k: (layer, i, j))
        out_shape = jax.ShapeDtypeStruct((depth, M, N), out_dtype)
        if has_prev:
            in_specs.append(pl.BlockSpec(memory_space=pl.ANY))
            args.append(prev)
            aliases = {len(args) - 1: 0}
    return pl.pallas_call(
        body, name=name, grid=(M // tm, N // tn, nk), in_specs=in_specs, out_specs=out_spec, out_shape=out_shape,
        scratch_shapes=[pltpu.VMEM((tm, tn), F32)] if nk > 1 else [], input_output_aliases=aliases,
        compiler_params=_params("parallel", "parallel", "arbitrary"),
    )(*args)


def _rmsnorm(x, g, name):
    T, D = x.shape
    tm = _pick(T, 512, SUBLANES)

    def body(x_ref, g_ref, h_ref):
        xv = x_ref[...]
        r = lax.rsqrt(jnp.mean(xv * xv, axis=-1, keepdims=True) + EPS)
        h_ref[...] = (xv * r * g_ref[...]).astype(BF16)

    return pl.pallas_call(
        body, name=name, grid=(T // tm,),
        in_specs=[pl.BlockSpec((tm, D), lambda i: (i, 0)), pl.BlockSpec((1, D), lambda i: (0, 0))],
        out_specs=pl.BlockSpec((tm, D), lambda i: (i, 0)),
        out_shape=jax.ShapeDtypeStruct((T, D), BF16), compiler_params=_params("parallel"),
    )(x, g.reshape(1, D))


def _rmsnorm_bwd(x, g, dh, dres, name):
    T, D = x.shape
    tm = _pick(T, 512, SUBLANES)
    with_res = dres is not None

    def body(*refs):
        if with_res:
            x_ref, g_ref, dh_ref, dres_ref, dx_ref, dg_ref = refs
        else:
            x_ref, g_ref, dh_ref, dx_ref, dg_ref = refs
        xv = x_ref[...]
        dhv = dh_ref[...]
        r = lax.rsqrt(jnp.mean(xv * xv, axis=-1, keepdims=True) + EPS)
        dyg = dhv * g_ref[...]
        c = jnp.mean(dyg * xv, axis=-1, keepdims=True)
        dx = r * dyg - xv * (r * r * r) * c
        if with_res:
            dx = dx + dres_ref[...]
        dx_ref[...] = dx

        @pl.when(pl.program_id(0) == 0)
        def _():
            dg_ref[...] = jnp.zeros_like(dg_ref)

        dg_ref[...] += jnp.sum(dhv * xv * r, axis=0, keepdims=True)

    row = pl.BlockSpec((tm, D), lambda i: (i, 0))
    vec = pl.BlockSpec((1, D), lambda i: (0, 0))
    ins = [x, g.reshape(1, D), dh] + ([dres] if with_res else [])
    return pl.pallas_call(
        body, name=name, grid=(T // tm,), in_specs=[row, vec, row] + ([row] if with_res else []),
        out_specs=[row, vec],
        out_shape=[jax.ShapeDtypeStruct((T, D), F32), jax.ShapeDtypeStruct((1, D), F32)],
        compiler_params=_params("arbitrary"),
    )(*ins)


def _loss_head(x, g, target, name):
    T, D = x.shape
    tm = _pick(T, 512, SUBLANES)

    def body(x_ref, g_ref, t_ref, loss_ref, dx_ref, dg_ref):
        xv = x_ref[...]
        gv = g_ref[...]
        r = lax.rsqrt(jnp.mean(xv * xv, axis=-1, keepdims=True) + EPS)
        e = xv * r * gv - t_ref[...]
        dy = e * (1.0 / D)
        dyg = dy * gv
        c = jnp.mean(dyg * xv, axis=-1, keepdims=True)
        dx_ref[...] = r * dyg - xv * (r * r * r) * c

        @pl.when(pl.program_id(0) == 0)
        def _():
            loss_ref[...] = jnp.zeros_like(loss_ref)
            dg_ref[...] = jnp.zeros_like(dg_ref)

        loss_ref[...] += jnp.sum(e * e, axis=0, keepdims=True) * (0.5 / D)
        dg_ref[...] += jnp.sum(dy * xv * r, axis=0, keepdims=True)

    row = pl.BlockSpec((tm, D), lambda i: (i, 0))
    vec = pl.BlockSpec((1, D), lambda i: (0, 0))
    return pl.pallas_call(
        body, name=name, grid=(T // tm,), in_specs=[row, vec, row], out_specs=[vec, row, vec],
        out_shape=[jax.ShapeDtypeStruct((1, D), F32), jax.ShapeDtypeStruct((T, D), F32), jax.ShapeDtypeStruct((1, D), F32)],
        compiler_params=_params("arbitrary"),
    )(x, g.reshape(1, D), target)


def _ssm_disc_math(lre, lim, logdt, br, bi):
    dt = jnp.exp(logdt)
    mag = jnp.exp(lre * dt)
    ar = mag * jnp.cos(lim * dt)
    ai = mag * jnp.sin(lim * dt)
    den = lre * lre + lim * lim
    nr = ar - 1.0
    fr = (nr * lre + ai * lim) / den
    fi = (ai * lre - nr * lim) / den
    return ar, ai, fr[None] * br - fi[None] * bi, fr[None] * bi + fi[None] * br


def _ssm_disc(lre, lim, logdt, br, bi, name):
    def body(lre_ref, lim_ref, dt_ref, br_ref, bi_ref, ar_ref, ai_ref, bbr_ref, bbi_ref):
        ar, ai, bbr, bbi = _ssm_disc_math(lre_ref[...], lim_ref[...], dt_ref[...], br_ref[...], bi_ref[...])
        ar_ref[...] = ar
        ai_ref[...] = ai
        bbr_ref[...] = bbr
        bbi_ref[...] = bbi

    sd = jax.ShapeDtypeStruct
    return pl.pallas_call(
        body, name=name, out_shape=[sd(lre.shape, F32), sd(lre.shape, F32), sd(br.shape, F32), sd(br.shape, F32)],
    )(lre, lim, logdt, br, bi)


def _ssm_disc_bwd(lre, lim, logdt, br, bi, dar, dai, dbbr, dbbi, name):
    def body(lre_ref, lim_ref, dt_ref, br_ref, bi_ref, dar_ref, dai_ref, dbbr_ref, dbbi_ref,
             glre_ref, glim_ref, gdt_ref, gbr_ref, gbi_ref):
        _, vjp = jax.vjp(_ssm_disc_math, lre_ref[...], lim_ref[...], dt_ref[...], br_ref[...], bi_ref[...])
        glre, glim, gdt, gbr, gbi = vjp((dar_ref[...], dai_ref[...], dbbr_ref[...], dbbi_ref[...]))
        glre_ref[...] = glre
        glim_ref[...] = glim
        gdt_ref[...] = gdt
        gbr_ref[...] = gbr
        gbi_ref[...] = gbi

    sd = jax.ShapeDtypeStruct
    return pl.pallas_call(
        body, name=name,
        out_shape=[sd(lre.shape, F32), sd(lre.shape, F32), sd(logdt.shape, F32), sd(br.shape, F32), sd(br.shape, F32)],
    )(lre, lim, logdt, br, bi, dar, dai, dbbr, dbbi)


def _shift_segments(v, down):
    n = v.shape[0]
    rows = lax.broadcasted_iota(jnp.int32, v.shape, 0)
    if down:
        return jnp.where(rows >= 1, pltpu.roll(v, 1, 0), 0.0)
    return jnp.where(rows < n - 1, pltpu.roll(v, n - 1, 0), 0.0)


def _cpow(ar, ai, n):
    rr, ri = None, None
    pr, pi = ar, ai
    while n:
        if n & 1:
            if rr is None:
                rr, ri = pr, pi
            else:
                rr, ri = rr * pr - ri * pi, rr * pi + ri * pr
        n >>= 1
        if n:
            pr, pi = pr * pr - pi * pi, 2.0 * pr * pi
    return rr, ri


def _ssm_geometry(T, C):
    seg_steps = T // SSM_SEGMENTS
    kc = min(SSM_CHUNK_STEPS, seg_steps)
    return C // SSM_BLOCK_CH, seg_steps, kc, seg_steps // kc, SSM_SEGMENTS * kc


def _ssm_carries(u, dy, bmat, cmat, amat, *, reverse, name):
    src = dy if reverse else u
    T, C = src.shape
    nblk, seg_steps, kc, nchunk, rc = _ssm_geometry(T, C)
    half = bmat.shape[2] // 2

    def body(src_ref, w_ref, a_ref, out_ref, buf_ref, st_ref):
        c = pl.program_id(1)

        @pl.when(c == 0)
        def _():
            st_ref[...] = jnp.zeros_like(st_ref)

        if reverse:
            buf_ref[...] = _dot(src_ref[...].astype(BF16), w_ref[0], NT)
        else:
            buf_ref[...] = _dot(src_ref[...].astype(BF16), w_ref[0], NN)
        ar = a_ref[0, :, :half]
        ai = a_ref[0, :, half:]
        if reverse:
            ai = -ai

        def step(i, carry):
            xr, xi = carry
            k = (kc - 1 - i) if reverse else i
            row = pl.multiple_of(k * SUBLANES, SUBLANES)
            br = buf_ref[pl.ds(row, SUBLANES), :half]
            bi = buf_ref[pl.ds(row, SUBLANES), half:]
            return ar * xr - ai * xi + br, ar * xi + ai * xr + bi

        xr, xi = lax.fori_loop(0, kc, step, (st_ref[:, :half], st_ref[:, half:]), unroll=8)
        st_ref[:, :half] = xr
        st_ref[:, half:] = xi

        @pl.when(c == nchunk - 1)
        def _():
            pr, pi = _cpow(ar, ai, seg_steps)
            sr = jnp.zeros_like(xr)
            si = jnp.zeros_like(xi)
            for _ in range(SSM_SEGMENTS - 1):
                nr = xr + pr * sr - pi * si
                ni = xi + pr * si + pi * sr
                sr = _shift_segments(nr, not reverse)
                si = _shift_segments(ni, not reverse)
            out_ref[0, :, :half] = sr
            out_ref[0, :, half:] = si

    cidx = (lambda b, c: (nchunk - 1 - c, b)) if reverse else (lambda b, c: (c, b))
    w = cmat if reverse else bmat
    return pl.pallas_call(
        body, name=name, grid=(nblk, nchunk),
        in_specs=[pl.BlockSpec((rc, SSM_BLOCK_CH), cidx),
                  pl.BlockSpec((1,) + w.shape[1:], lambda b, c: (b, 0, 0)),
                  pl.BlockSpec((1, SUBLANES, 2 * half), lambda b, c: (b, 0, 0))],
        out_specs=pl.BlockSpec((1, SUBLANES, 2 * half), lambda b, c: (b, 0, 0)),
        out_shape=jax.ShapeDtypeStruct((nblk, SUBLANES, 2 * half), F32),
        scratch_shapes=[pltpu.VMEM((rc, 2 * half), F32), pltpu.VMEM((SUBLANES, 2 * half), F32)],
        compiler_params=_params("parallel", "arbitrary"),
    )(src, w, amat)


def _ssm_scan(u, bmat, cmat, amat, carries, dvec, name):
    T, C = u.shape
    nblk, seg_steps, kc, nchunk, rc = _ssm_geometry(T, C)
    half = bmat.shape[2] // 2

    def body(u_ref, b_ref, c_ref, a_ref, s_ref, d_ref, y_ref, x_ref, st_ref):
        c = pl.program_id(1)

        @pl.when(c == 0)
        def _():
            st_ref[...] = s_ref[0]

        uv = u_ref[...]
        x_ref[...] = _dot(uv.astype(BF16), b_ref[0], NN)
        ar = a_ref[0, :, :half]
        ai = a_ref[0, :, half:]

        def step(k, carry):
            xr, xi = carry
            row = pl.multiple_of(k * SUBLANES, SUBLANES)
            nr = ar * xr - ai * xi + x_ref[pl.ds(row, SUBLANES), :half]
            ni = ar * xi + ai * xr + x_ref[pl.ds(row, SUBLANES), half:]
            x_ref[pl.ds(row, SUBLANES), :half] = nr
            x_ref[pl.ds(row, SUBLANES), half:] = ni
            return nr, ni

        xr, xi = lax.fori_loop(0, kc, step, (st_ref[:, :half], st_ref[:, half:]), unroll=8)
        st_ref[:, :half] = xr
        st_ref[:, half:] = xi
        y_ref[...] = _dot(x_ref[...].astype(BF16), c_ref[0], NN) + d_ref[...] * uv

    return pl.pallas_call(
        body, name=name, grid=(nblk, nchunk),
        in_specs=[pl.BlockSpec((rc, SSM_BLOCK_CH), lambda b, c: (c, b)),
                  pl.BlockSpec((1,) + bmat.shape[1:], lambda b, c: (b, 0, 0)),
                  pl.BlockSpec((1,) + cmat.shape[1:], lambda b, c: (b, 0, 0)),
                  pl.BlockSpec((1, SUBLANES, 2 * half), lambda b, c: (b, 0, 0)),
                  pl.BlockSpec((1, SUBLANES, 2 * half), lambda b, c: (b, 0, 0)),
                  pl.BlockSpec((1, SSM_BLOCK_CH), lambda b, c: (0, b))],
        out_specs=[pl.BlockSpec((rc, SSM_BLOCK_CH), lambda b, c: (c, b)),
                   pl.BlockSpec((rc, 2 * half), lambda b, c: (c, b))],
        out_shape=[jax.ShapeDtypeStruct((T, C), F32), jax.ShapeDtypeStruct((T, nblk * 2 * half), F32)],
        scratch_shapes=[pltpu.VMEM((SUBLANES, 2 * half), F32)],
        compiler_params=_params("parallel", "arbitrary"),
    )(u, bmat, cmat, amat, carries, dvec)


def _ssm_scan_bwd(dy, u, xs, bmat, cmat, amat, carries, dvec, name):
    T, C = u.shape
    nblk, seg_steps, kc, nchunk, rc = _ssm_geometry(T, C)
    half = bmat.shape[2] // 2
    width = 2 * half

    def body(dy_ref, u_ref, x_ref, xp_ref, b_ref, c_ref, a_ref, s_ref, d_ref,
             du_ref, db_ref, dc_ref, da_ref, dd_ref, g_ref, st_ref, acc_ref):
        c = pl.program_id(1)

        @pl.when(c == 0)
        def _():
            st_ref[...] = s_ref[0]
            acc_ref[...] = jnp.zeros_like(acc_ref)
            db_ref[...] = jnp.zeros_like(db_ref)
            dc_ref[...] = jnp.zeros_like(dc_ref)
            dd_ref[...] = jnp.zeros_like(dd_ref)

        dyv = dy_ref[...]
        uv = u_ref[...]
        dyb = dyv.astype(BF16)
        g_ref[...] = _dot(dyb, c_ref[0], NT)
        ar = a_ref[0, :, :half]
        ai = a_ref[0, :, half:]

        def step(i, carry):
            gr, gi, sr, si = carry
            k = kc - 1 - i
            row = pl.multiple_of(k * SUBLANES, SUBLANES)
            nr = ar * gr + ai * gi + g_ref[pl.ds(row, SUBLANES), :half]
            ni = ar * gi - ai * gr + g_ref[pl.ds(row, SUBLANES), half:]
            g_ref[pl.ds(row, SUBLANES), :half] = nr
            g_ref[pl.ds(row, SUBLANES), half:] = ni
            prow = pl.multiple_of(jnp.maximum(k - 1, 0) * SUBLANES, SUBLANES)
            live = (k >= 1).astype(F32)
            xr = x_ref[pl.ds(prow, SUBLANES), :half] * live
            xi = x_ref[pl.ds(prow, SUBLANES), half:] * live
            return nr, ni, sr + xr * nr + xi * ni, si + xr * ni - xi * nr

        init = (st_ref[:, :half], st_ref[:, half:], acc_ref[:, :half], acc_ref[:, half:])
        gr, gi, sr, si = lax.fori_loop(0, kc, step, init, unroll=8)
        st_ref[:, :half] = gr
        st_ref[:, half:] = gi
        xpr = xp_ref[:, :half]
        xpi = xp_ref[:, half:]
        first = (c == nchunk - 1)
        xpr = jnp.where(first, _shift_segments(xpr, True), xpr)
        xpi = jnp.where(first, _shift_segments(xpi, True), xpi)
        acc_ref[:, :half] = sr + xpr * gr + xpi * gi
        acc_ref[:, half:] = si + xpr * gi - xpi * gr

        gb = g_ref[...].astype(BF16)
        du_ref[...] = _dot(gb, b_ref[0], NT) + dyv * d_ref[...]
        db_ref[0] += _dot(uv.astype(BF16), gb, TN)
        dc_ref[0] += _dot(dyb, x_ref[...].astype(BF16), TN)
        dd_ref[...] += jnp.sum(dyv * uv, axis=0, keepdims=True)

        @pl.when(c == nchunk - 1)
        def _():
            tot = jnp.sum(acc_ref[...], axis=0, keepdims=True)
            da_ref[0] = jnp.broadcast_to(tot, (SUBLANES, width))

    rev = lambda b, c: (nchunk - 1 - c, b)
    blk3 = lambda b, c: (b, 0, 0)
    prev_group = lambda b, c: (((nchunk - 1 - c) * kc - 1 + seg_steps) % seg_steps, b)
    sd = jax.ShapeDtypeStruct
    return pl.pallas_call(
        body, name=name, grid=(nblk, nchunk),
        in_specs=[pl.BlockSpec((rc, SSM_BLOCK_CH), rev), pl.BlockSpec((rc, SSM_BLOCK_CH), rev),
                  pl.BlockSpec((rc, width), rev), pl.BlockSpec((SUBLANES, width), prev_group),
                  pl.BlockSpec((1,) + bmat.shape[1:], blk3), pl.BlockSpec((1,) + cmat.shape[1:], blk3),
                  pl.BlockSpec((1, SUBLANES, width), blk3), pl.BlockSpec((1, SUBLANES, width), blk3),
                  pl.BlockSpec((1, SSM_BLOCK_CH), lambda b, c: (0, b))],
        out_specs=[pl.BlockSpec((rc, SSM_BLOCK_CH), rev), pl.BlockSpec((1, SSM_BLOCK_CH, width), blk3),
                   pl.BlockSpec((1, SSM_BLOCK_CH, width), blk3), pl.BlockSpec((1, SUBLANES, width), blk3),
                   pl.BlockSpec((1, SSM_BLOCK_CH), lambda b, c: (0, b))],
        out_shape=[sd((T, C), F32), sd((nblk, SSM_BLOCK_CH, width), F32), sd((nblk, SSM_BLOCK_CH, width), F32),
                   sd((nblk, SUBLANES, width), F32), sd((1, C), F32)],
        scratch_shapes=[pltpu.VMEM((rc, width), F32), pltpu.VMEM((SUBLANES, width), F32), pltpu.VMEM((SUBLANES, width), F32)],
        compiler_params=_params("parallel", "arbitrary"),
    )(dy, u, xs, xs, bmat, cmat, amat, carries, dvec)


SSM_STEPS_FWD = 128
SSM_STEPS_BWD = 256


def _ssm_tiles(ref, v, off, steps, n):
    return [ref[v, pl.ds(off + j, steps, stride=SUBLANES), :] for j in range(n)]


def _ssm_fwd(uz, bmat, cmat, art, ait, dvec, name):
    T = uz.shape[0]
    nblk, cb, width = bmat.shape
    C = nblk * cb
    half = width // 2
    nt = half // LANES
    npair = nblk // 2
    kc = min(SSM_STEPS_FWD, T)
    nchunk = T // kc

    def body(u_ref, b_ref, c_ref, ar_ref, ai_ref, d_ref, y_ref, xr_ref, xi_ref, sr_ref, si_ref):
        @pl.when(pl.program_id(0) == 0)
        def _():
            sr_ref[...] = jnp.zeros_like(sr_ref)
            si_ref[...] = jnp.zeros_like(si_ref)

        uv = u_ref[...]
        for b in range(nblk):
            bu = _dot(uv[:, b * cb:(b + 1) * cb].astype(BF16), b_ref[b], NN)
            v, off = b // 2, nt * (b % 2)
            for j in range(nt):
                xr_ref[v, pl.ds(off + j, kc, stride=SUBLANES), :] = bu[:, j * LANES:(j + 1) * LANES]
                xi_ref[v, pl.ds(off + j, kc, stride=SUBLANES), :] = bu[:, half + j * LANES:half + (j + 1) * LANES]
        ars = [ar_ref[v] for v in range(npair)]
        ais = [ai_ref[v] for v in range(npair)]

        def step(k, carry):
            row = pl.ds(k * SUBLANES, SUBLANES)
            out = []
            for v in range(npair):
                xr, xi = carry[2 * v], carry[2 * v + 1]
                nr = ars[v] * xr - ais[v] * xi + xr_ref[v, row, :]
                ni = ars[v] * xi + ais[v] * xr + xi_ref[v, row, :]
                xr_ref[v, row, :] = nr
                xi_ref[v, row, :] = ni
                out += [nr, ni]
            return tuple(out)

        fin = tuple(ref[v] for v in range(npair) for ref in (sr_ref, si_ref))
        for k in range(kc):
            fin = step(k, fin)
        for v in range(npair):
            sr_ref[v] = fin[2 * v]
            si_ref[v] = fin[2 * v + 1]
        for b in range(nblk):
            v, off = b // 2, nt * (b % 2)
            xb = jnp.concatenate(_ssm_tiles(xr_ref, v, off, kc, nt) + _ssm_tiles(xi_ref, v, off, kc, nt), axis=1)
            cols = slice(b * cb, (b + 1) * cb)
            y_ref[:, cols] = _dot(xb.astype(BF16), c_ref[b], NN) + d_ref[:, cols] * uv[:, cols]

    whole = lambda a: pl.BlockSpec(a.shape, lambda c: (0,) * a.ndim)
    st = pl.BlockSpec((npair, kc * SUBLANES, LANES), lambda c: (0, c, 0))
    sd = jax.ShapeDtypeStruct
    return pl.pallas_call(
        body, name=name, grid=(nchunk,),
        in_specs=[pl.BlockSpec((kc, C), lambda c: (c, 0)), whole(bmat), whole(cmat), whole(art), whole(ait), whole(dvec)],
        out_specs=[pl.BlockSpec((kc, C), lambda c: (c, 0)), st, st],
        out_shape=[sd((T, C), F32), sd((npair, T * SUBLANES, LANES), F32), sd((npair, T * SUBLANES, LANES), F32)],
        scratch_shapes=[pltpu.VMEM((npair, SUBLANES, LANES), F32), pltpu.VMEM((npair, SUBLANES, LANES), F32)],
        compiler_params=_params("arbitrary"),
    )(uz, bmat, cmat, art, ait, dvec)


def _ssm_bwd(dy, uz, xr, xi, bmat, cmat, art, ait, dvec, dproj, name):
    T = uz.shape[0]
    nblk, cb, width = bmat.shape
    C = nblk * cb
    half = width // 2
    nt = half // LANES
    npair = nblk // 2
    kc = min(SSM_STEPS_BWD, T)
    nchunk = T // kc

    def body(dy_ref, u_ref, xr_ref, xi_ref, xpr_ref, xpi_ref, b_ref, c_ref, ar_ref, ai_ref, d_ref, _,
             du_ref, db_ref, dc_ref, dar_ref, dai_ref, dd_ref, gr_ref, gi_ref, sr_ref, si_ref):
        c = pl.program_id(0)

        @pl.when(c == 0)
        def _():
            for ref in (sr_ref, si_ref, db_ref, dc_ref, dar_ref, dai_ref, dd_ref):
                ref[...] = jnp.zeros_like(ref)

        dyv = dy_ref[...]
        uv = u_ref[...]
        for b in range(nblk):
            dx = _dot(dyv[:, b * cb:(b + 1) * cb].astype(BF16), c_ref[b], NT)
            v, off = b // 2, nt * (b % 2)
            for j in range(nt):
                gr_ref[v, pl.ds(off + j, kc, stride=SUBLANES), :] = dx[:, j * LANES:(j + 1) * LANES]
                gi_ref[v, pl.ds(off + j, kc, stride=SUBLANES), :] = dx[:, half + j * LANES:half + (j + 1) * LANES]
        ars = [ar_ref[v] for v in range(npair)]
        ais = [ai_ref[v] for v in range(npair)]

        def pair_update(v, gr, gi, row):
            nr = ars[v] * gr + ais[v] * gi + gr_ref[v, row, :]
            ni = ars[v] * gi - ais[v] * gr + gi_ref[v, row, :]
            gr_ref[v, row, :] = nr
            gi_ref[v, row, :] = ni
            return nr, ni

        def step(i, carry):
            k = kc - 1 - i
            row = pl.ds(k * SUBLANES, SUBLANES)
            prow = pl.ds((k - 1) * SUBLANES, SUBLANES)
            out = []
            for v in range(npair):
                gr, gi, sr, si = carry[4 * v:4 * v + 4]
                nr, ni = pair_update(v, gr, gi, row)
                pr, pi = xr_ref[v, prow, :], xi_ref[v, prow, :]
                out += [nr, ni, sr + pr * nr + pi * ni, si + pr * ni - pi * nr]
            return tuple(out)

        mid = tuple(ref[v] for v in range(npair) for ref in (sr_ref, si_ref, dar_ref, dai_ref))
        for i in range(kc - 1):
            mid = step(i, mid)
        live = (c < nchunk - 1).astype(F32)
        row0 = pl.ds(0, SUBLANES)
        for v in range(npair):
            gr, gi, sr, si = mid[4 * v:4 * v + 4]
            nr, ni = pair_update(v, gr, gi, row0)
            pr, pi = xpr_ref[v] * live, xpi_ref[v] * live
            sr_ref[v] = nr
            si_ref[v] = ni
            dar_ref[v] = sr + pr * nr + pi * ni
            dai_ref[v] = si + pr * ni - pi * nr
        for b in range(nblk):
            v, off = b // 2, nt * (b % 2)
            cols = slice(b * cb, (b + 1) * cb)
            gb = jnp.concatenate(_ssm_tiles(gr_ref, v, off, kc, nt) + _ssm_tiles(gi_ref, v, off, kc, nt), axis=1).astype(BF16)
            xb = jnp.concatenate(_ssm_tiles(xr_ref, v, off, kc, nt) + _ssm_tiles(xi_ref, v, off, kc, nt), axis=1).astype(BF16)
            du_ref[:, cols] = (_dot(gb, b_ref[b], NT) + dyv[:, cols] * d_ref[:, cols]).astype(BF16)
            db_ref[b] += _dot(uv[:, cols].astype(BF16), gb, TN)
            dc_ref[b] += _dot(dyv[:, cols].astype(BF16), xb, TN)
        dd_ref[...] += jnp.sum(dyv * uv, axis=0, keepdims=True)

    whole = lambda a: pl.BlockSpec(a.shape, lambda c: (0,) * a.ndim)
    rev = lambda c: (nchunk - 1 - c, 0)
    st = pl.BlockSpec((npair, kc * SUBLANES, LANES), lambda c: (0, nchunk - 1 - c, 0))
    stp = pl.BlockSpec((npair, SUBLANES, LANES), lambda c: (0, jnp.maximum((nchunk - 1 - c) * kc - 1, 0), 0))
    acc = lambda shape: pl.BlockSpec(shape, lambda c: (0,) * len(shape))
    sd = jax.ShapeDtypeStruct
    pair_shape = (npair, SUBLANES, LANES)
    return pl.pallas_call(
        body, name=name, grid=(nchunk,),
        in_specs=[pl.BlockSpec((kc, C), rev), pl.BlockSpec((kc, C), rev), st, st, stp, stp, whole(bmat), whole(cmat),
                  whole(art), whole(ait), whole(dvec), pl.BlockSpec(memory_space=pl.ANY)],
        out_specs=[pl.BlockSpec((kc, C), rev), acc(bmat.shape), acc(bmat.shape), acc(pair_shape), acc(pair_shape), acc((1, C))],
        out_shape=[sd(dproj.shape, BF16), sd(bmat.shape, F32), sd(bmat.shape, F32), sd(pair_shape, F32), sd(pair_shape, F32),
                   sd((1, C), F32)],
        scratch_shapes=[pltpu.VMEM((npair, kc * SUBLANES, LANES), F32), pltpu.VMEM((npair, kc * SUBLANES, LANES), F32),
                        pltpu.VMEM(pair_shape, F32), pltpu.VMEM(pair_shape, F32)],
        input_output_aliases={11: 0}, compiler_params=_params("arbitrary"),
    )(dy, uz, xr, xi, xr, xi, bmat, cmat, art, ait, dvec, dproj)


def _ssm_post(y, z, w_glu, b_glu, name):
    T, C = y.shape
    tm = _pick(T, 512, SUBLANES)

    def body(y_ref, z_ref, w_ref, b_ref, o_ref, a_ref):
        a, _ = _gelu_and_grad(y_ref[...])
        ab = a.astype(BF16)
        sg = _sigmoid(_dot(ab, w_ref[...], NN) + b_ref[...])
        sz, _ = _silu_and_grad(z_ref[...].astype(F32))
        o_ref[...] = (a * sg * sz).astype(BF16)
        a_ref[...] = ab

    row = pl.BlockSpec((tm, C), lambda i: (i, 0))
    return pl.pallas_call(
        body, name=name, grid=(T // tm,),
        in_specs=[row, row, pl.BlockSpec((C, C), lambda i: (0, 0)), pl.BlockSpec((1, C), lambda i: (0, 0))],
        out_specs=[row, row], out_shape=[jax.ShapeDtypeStruct((T, C), BF16)] * 2, compiler_params=_params("parallel"),
    )(y, z, w_glu, b_glu.reshape(1, C))


def _ssm_post_bwd(do, y, z, w_glu, b_glu, dproj, col, name):
    T, C = y.shape
    tm = _pick(T, 512, SUBLANES)

    def body(do_ref, y_ref, z_ref, w_ref, b_ref, _, dy_ref, dz_ref, ds_ref, db_ref):
        dov = do_ref[...]
        a, da_dy = _gelu_and_grad(y_ref[...])
        sg = _sigmoid(_dot(a.astype(BF16), w_ref[...], NN) + b_ref[...])
        sz, dsz = _silu_and_grad(z_ref[...].astype(F32))
        yg = a * sg
        dz_ref[...] = (dov * yg * dsz).astype(BF16)
        dyg = dov * sz
        ds = dyg * a * sg * (1.0 - sg)
        dsb = ds.astype(BF16)
        ds_ref[...] = dsb
        da = dyg * sg + _dot(dsb, w_ref[...], NT)
        dy_ref[...] = da * da_dy

        @pl.when(pl.program_id(0) == 0)
        def _():
            db_ref[...] = jnp.zeros_like(db_ref)

        db_ref[...] += jnp.sum(ds, axis=0, keepdims=True)

    row = pl.BlockSpec((tm, C), lambda i: (i, 0))
    vec = pl.BlockSpec((1, C), lambda i: (0, 0))
    sd = jax.ShapeDtypeStruct
    return pl.pallas_call(
        body, name=name, grid=(T // tm,),
        in_specs=[row, row, row, pl.BlockSpec((C, C), lambda i: (0, 0)), vec, pl.BlockSpec(memory_space=pl.ANY)],
        out_specs=[row, pl.BlockSpec((tm, C), lambda i: (i, col // C)), row, vec],
        out_shape=[sd((T, C), F32), sd(dproj.shape, BF16), sd((T, C), BF16), sd((1, C), F32)],
        input_output_aliases={5: 1}, compiler_params=_params("arbitrary"),
    )(do, y, z, w_glu, b_glu.reshape(1, C), dproj)


def _rel_bucket(dist):
    n = jnp.maximum(dist, 0)
    max_exact = NUM_BUCKETS // 2
    n_f = jnp.maximum(n, 1).astype(F32)
    large = max_exact + (jnp.log(n_f / max_exact) / math.log(REL_MAX_DISTANCE / max_exact)
                         * (NUM_BUCKETS - max_exact)).astype(jnp.int32)
    large = jnp.minimum(large, NUM_BUCKETS - 1)
    return jnp.where(n < max_exact, n, large)


def _band_tables():
    qi = jnp.arange(ATTN_BLOCK)[:, None]
    kj = jnp.arange(2 * ATTN_BLOCK)[None, :]
    delta = ATTN_BLOCK + qi - kj
    buckets, bands = [], []
    for window, dilation in ATTN_CONFIGS:
        bands.append((delta >= 0) & (delta <= window // dilation))
        buckets.append(_rel_bucket(jnp.maximum(delta, 0) * dilation))
    return jnp.stack(buckets), jnp.stack(bands)


def _attn_blocks_per_residue(T):
    return [T // (ATTN_BLOCK * d) for _, d in ATTN_CONFIGS]


ATTN_UNITS = 4


def _attn_tile(T, r):
    nq = max(1, ATTN_UNITS // r)
    rows = ATTN_BLOCK * r * nq
    return nq, rows, T // rows


def _attn_units(r, nq, chunk):
    if r >= ATTN_UNITS:
        return [(chunk * ATTN_UNITS + i, None) for i in range(ATTN_UNITS)]
    units = []
    for j in range(nq):
        for s in range(r):
            units.append((ATTN_BLOCK * j * r + s, ATTN_BLOCK * (j - 1) * r + s if j else None))
    return units


def _rows(start, r):
    return pl.ds(start, ATTN_BLOCK, stride=r) if r > 1 else pl.ds(start, ATTN_BLOCK)


def _attn_group_fwd(qkv, biasm, g, name):
    T = qkv.shape[0]
    r = ATTN_CONFIGS[g][1]
    B, hd = ATTN_BLOCK, ATTN_HEAD_DIM
    nq, rows, ntiles = _attn_tile(T, r)
    nchunks = max(1, r // ATTN_UNITS)
    last_prev = B * (nq - 1) * r
    scale = hd ** -0.5
    tiles_per_tensor = 3 * HEADS_PER_GROUP * hd // LANES

    def body(q_ref, kc_ref, kp_ref, vc_ref, vp_ref, bias_ref, o_ref, lse_ref, s_ref, p_ref):
        n = pl.program_id(1)
        lane = lax.broadcasted_iota(jnp.int32, (1, LANES), 1)
        col = lax.broadcasted_iota(jnp.int32, (1, 2 * B), 1)
        masks = [lane < hd, lane >= hd]
        first_pen = jnp.where((col < B) & (n == 0), NEG_INF, 0.0)

        def chunk_body(chunk):
            units = _attn_units(r, nq, chunk)

            def keys(cur_ref, prev_ref, cs, ps):
                prev = prev_ref[_rows(last_prev + (cs if r >= ATTN_UNITS else cs % r), r), :] if ps is None else cur_ref[_rows(ps, r), :]
                return jnp.concatenate([prev, cur_ref[_rows(cs, r), :]], axis=0).astype(BF16)

            for u, (cs, ps) in enumerate(units):
                qv = q_ref[_rows(cs, r), :]
                kw = keys(kc_ref, kp_ref, cs, ps)
                for hh in range(2):
                    s_ref[2 * u + hh] = _dot(jnp.where(masks[hh], qv, 0.0).astype(BF16), kw, NT)
            for u, (cs, ps) in enumerate(units):
                lses = []
                for hh in range(2):
                    s = s_ref[2 * u + hh] * scale + bias_ref[hh]
                    if ps is None:
                        s = s + first_pen
                    m = jnp.max(s, axis=-1, keepdims=True)
                    p = jnp.exp(s - m)
                    l = jnp.sum(p, axis=-1, keepdims=True)
                    p_ref[2 * u + hh] = (p / l).astype(BF16)
                    lses.append(m + jnp.log(l))
                lse_ref[_rows(cs, r), :] = jnp.where(masks[0], lses[0], lses[1])
            for u, (cs, ps) in enumerate(units):
                vw = keys(vc_ref, vp_ref, cs, ps)
                o_ref[_rows(cs, r), :] = (_dot(p_ref[2 * u], jnp.where(masks[0], vw, 0), NN)
                                          + _dot(p_ref[2 * u + 1], jnp.where(masks[1], vw, 0), NN))

        if nchunks == 1:
            chunk_body(0)
        else:
            pl.loop(0, nchunks)(chunk_body)

    def cur(t):
        return pl.BlockSpec((rows, LANES), lambda hf, n: (n, t * tiles_per_tensor + 2 * g + hf))

    def prev(t):
        return pl.BlockSpec((rows, LANES), lambda hf, n: (jnp.maximum(n - 1, 0), t * tiles_per_tensor + 2 * g + hf))

    out = pl.BlockSpec((rows, LANES), lambda hf, n: (n, hf))
    sd = jax.ShapeDtypeStruct((T, 2 * LANES), F32)
    return pl.pallas_call(
        body, name=name, grid=(2, ntiles),
        in_specs=[cur(0), cur(1), prev(1), cur(2), prev(2), pl.BlockSpec((None, 2, B, 2 * B), lambda hf, n: (g, hf, 0, 0))],
        out_specs=[out, out], out_shape=[sd, sd],
        scratch_shapes=[pltpu.VMEM((2 * ATTN_UNITS, B, 2 * B), F32), pltpu.VMEM((2 * ATTN_UNITS, B, 2 * B), BF16)],
        compiler_params=_params("parallel", "parallel"),
    )(qkv, qkv, qkv, qkv, qkv, biasm)


def _attn_group_bwd(qkv, do, dvec, lse, biasm, g, dproj, dk_col, name):
    T = qkv.shape[0]
    r = ATTN_CONFIGS[g][1]
    B, hd = ATTN_BLOCK, ATTN_HEAD_DIM
    nq, rows, ntiles = _attn_tile(T, r)
    nchunks = max(1, r // ATTN_UNITS)
    last_prev = B * (nq - 1) * r
    scale = hd ** -0.5
    tiles_per_tensor = 3 * HEADS_PER_GROUP * hd // LANES

    def body(q_ref, kc_ref, kp_ref, vc_ref, vp_ref, do_ref, dv_ref, lse_ref, bias_ref, _,
             dq_ref, dk_ref, dvo_ref, dbias_ref, ck_ref, cv_ref, ak_ref, av_ref, s_ref, dp_ref, p_ref, ds_ref):
        n = pl.program_id(1)
        lane = lax.broadcasted_iota(jnp.int32, (1, LANES), 1)
        col = lax.broadcasted_iota(jnp.int32, (1, 2 * B), 1)
        masks = [lane < hd, lane >= hd]
        first_pen = jnp.where((col < B) & (n == 0), NEG_INF, 0.0)

        @pl.when(n == 0)
        def _():
            dbias_ref[...] = jnp.zeros_like(dbias_ref)
            ck_ref[...] = jnp.zeros_like(ck_ref)
            cv_ref[...] = jnp.zeros_like(cv_ref)

        def chunk_body(chunk):
            units = _attn_units(r, nq, chunk)

            def prev_rows(cs):
                return _rows(last_prev + (cs if r >= ATTN_UNITS else cs % r), r)

            def keys(cur_ref, prev_ref, cs, ps):
                prev = prev_ref[prev_rows(cs), :] if ps is None else cur_ref[_rows(ps, r), :]
                return jnp.concatenate([prev, cur_ref[_rows(cs, r), :]], axis=0).astype(BF16)

            for u, (cs, ps) in enumerate(units):
                qv = q_ref[_rows(cs, r), :]
                dov = do_ref[_rows(cs, r), :]
                kw = keys(kc_ref, kp_ref, cs, ps)
                vw = keys(vc_ref, vp_ref, cs, ps)
                for hh in range(2):
                    s_ref[2 * u + hh] = _dot(jnp.where(masks[hh], qv, 0.0).astype(BF16), kw, NT)
                    dp_ref[2 * u + hh] = _dot(jnp.where(masks[hh], dov, 0.0).astype(BF16), vw, NT)
            for u, (cs, ps) in enumerate(units):
                lse_t = lse_ref[_rows(cs, r), :]
                dv_t = dv_ref[_rows(cs, r), :]
                for hh in range(2):
                    lo = hh * hd
                    s = s_ref[2 * u + hh] * scale + bias_ref[hh]
                    if ps is None:
                        s = s + first_pen
                    p = jnp.exp(s - lse_t[:, lo:lo + 1])
                    ds = p * (dp_ref[2 * u + hh] + dv_t[:, lo:lo + 1])
                    dbias_ref[hh] += ds
                    p_ref[2 * u + hh] = p.astype(BF16)
                    ds_ref[2 * u + hh] = ds.astype(BF16)
            for u, (cs, ps) in enumerate(units):
                qv = q_ref[_rows(cs, r), :]
                dov = do_ref[_rows(cs, r), :]
                kw = keys(kc_ref, kp_ref, cs, ps)
                dq, dkw, dvw = 0.0, 0.0, 0.0
                for hh in range(2):
                    dsb = ds_ref[2 * u + hh]
                    dq = dq + _dot(dsb, jnp.where(masks[hh], kw, 0), NN)
                    dkw = dkw + _dot(dsb, jnp.where(masks[hh], qv, 0.0).astype(BF16), TN)
                    dvw = dvw + _dot(p_ref[2 * u + hh], jnp.where(masks[hh], dov, 0.0).astype(BF16), TN)
                dq_ref[_rows(cs, r), :] = dq * scale
                ak_ref[_rows(cs, r), :] = dkw[B:] * scale
                av_ref[_rows(cs, r), :] = dvw[B:]
                if ps is None:
                    ck_ref[prev_rows(cs), :] += dkw[:B] * scale
                    cv_ref[prev_rows(cs), :] += dvw[:B]
                else:
                    ak_ref[_rows(ps, r), :] += dkw[:B] * scale
                    av_ref[_rows(ps, r), :] += dvw[:B]

        @pl.when(n < ntiles)
        def _():
            for chunk in range(nchunks):
                chunk_body(chunk)

        dk_ref[...] = ck_ref[...].astype(BF16)
        dvo_ref[...] = cv_ref[...].astype(BF16)
        ck_ref[...] = ak_ref[...]
        cv_ref[...] = av_ref[...]

    last = ntiles - 1

    def cur(t):
        return pl.BlockSpec((rows, LANES), lambda hf, n: (jnp.minimum(n, last), t * tiles_per_tensor + 2 * g + hf))

    def prev(t):
        return pl.BlockSpec((rows, LANES), lambda hf, n: (jnp.clip(n - 1, 0, last), t * tiles_per_tensor + 2 * g + hf))

    nat = pl.BlockSpec((rows, LANES), lambda hf, n: (jnp.minimum(n, last), hf))
    nat_prev = pl.BlockSpec((rows, LANES), lambda hf, n: (jnp.clip(n - 1, 0, last), hf))
    tab = pl.BlockSpec((None, 2, B, 2 * B), lambda hf, n: (g, hf, 0, 0))
    dtab = pl.BlockSpec((2, B, 2 * B), lambda hf, n: (hf, 0, 0))
    sd = jax.ShapeDtypeStruct
    vm = pltpu.VMEM
    dk_tile = dk_col // LANES + 2 * g
    dk_spec = pl.BlockSpec((rows, LANES), lambda hf, n: (jnp.clip(n - 1, 0, last), dk_tile + hf))
    return pl.pallas_call(
        body, name=name, grid=(2, ntiles + 1),
        in_specs=[cur(0), cur(1), prev(1), cur(2), prev(2), nat, nat, nat, tab, pl.BlockSpec(memory_space=pl.ANY)],
        out_specs=[nat, dk_spec, nat_prev, dtab],
        out_shape=[sd((T, 2 * LANES), F32), sd(dproj.shape, BF16), sd((T, 2 * LANES), BF16),
                   sd((HEADS_PER_GROUP, B, 2 * B), F32)],
        scratch_shapes=[vm((rows, LANES), F32), vm((rows, LANES), F32), vm((rows, LANES), F32), vm((rows, LANES), F32),
                        vm((2 * ATTN_UNITS, B, 2 * B), F32), vm((2 * ATTN_UNITS, B, 2 * B), F32),
                        vm((2 * ATTN_UNITS, B, 2 * B), BF16), vm((2 * ATTN_UNITS, B, 2 * B), BF16)],
        input_output_aliases={9: 1}, compiler_params=_params("parallel", "arbitrary"),
    )(qkv, qkv, qkv, qkv, qkv, do, dvec, lse, biasm, dproj)


def _attn_fwd(q, k, v, biasm, name):
    ng, T, gw = q.shape
    hd = ATTN_HEAD_DIM
    nh = gw // hd
    nblk = T // ATTN_BLOCK
    nbs = _attn_blocks_per_residue(T)
    scale = hd ** -0.5
    B = ATTN_BLOCK

    def body(q_ref, kc_ref, kp_ref, vc_ref, vp_ref, bias_ref, o_ref, lse_ref, s_ref, p_ref):
        g = pl.program_id(0)
        b = pl.program_id(1)
        nb = jnp.where(g == 0, nbs[0], jnp.where(g == 1, nbs[1], nbs[2]))
        no_prev = (b % nb) == 0
        col = lax.broadcasted_iota(jnp.int32, (1, 2 * B), 1)
        pen = jnp.where((col < B) & no_prev, NEG_INF, 0.0)
        heads = [slice(h * hd, (h + 1) * hd) for h in range(nh)]
        for h, hs in enumerate(heads):
            kw = jnp.concatenate([kp_ref[0, :, hs], kc_ref[0, :, hs]], axis=0)
            s_ref[h] = _dot(q_ref[0, :, hs], kw, NT)
        for h, hs in enumerate(heads):
            s = s_ref[h] * scale + bias_ref[0, h] + pen
            m = jnp.max(s, axis=-1, keepdims=True)
            p = jnp.exp(s - m)
            l = jnp.sum(p, axis=-1, keepdims=True)
            p_ref[h] = (p / l).astype(BF16)
            lse_ref[0, :, hs] = jnp.broadcast_to(m + jnp.log(l), (B, hd))
        for h, hs in enumerate(heads):
            vw = jnp.concatenate([vp_ref[0, :, hs], vc_ref[0, :, hs]], axis=0)
            o_ref[0, :, hs] = _dot(p_ref[h], vw, NN)

    cur = pl.BlockSpec((1, B, gw), lambda g, b: (g, b, 0))
    prev = pl.BlockSpec((1, B, gw), lambda g, b: (g, jnp.maximum(b - 1, 0), 0))
    return pl.pallas_call(
        body, name=name, grid=(ng, nblk),
        in_specs=[cur, cur, prev, cur, prev, pl.BlockSpec((1, nh, B, 2 * B), lambda g, b: (g, 0, 0, 0))],
        out_specs=[cur, cur], out_shape=[jax.ShapeDtypeStruct(q.shape, F32)] * 2,
        scratch_shapes=[pltpu.VMEM((nh, B, 2 * B), F32), pltpu.VMEM((nh, B, 2 * B), BF16)],
        compiler_params=_params("parallel", "parallel"),
    )(q, k, k, v, v, biasm)


def _attn_bwd(q, k, v, do, dvec, lse, biasm, name):
    ng, T, gw = q.shape
    hd = ATTN_HEAD_DIM
    nh = gw // hd
    nblk = T // ATTN_BLOCK
    nbs = _attn_blocks_per_residue(T)
    scale = hd ** -0.5
    B = ATTN_BLOCK

    def body(q_ref, kc_ref, kp_ref, vc_ref, vp_ref, do_ref, dv_ref, lse_ref, bias_ref,
             dq_ref, dk_ref, dvo_ref, dbias_ref, ck_ref, cv_ref, s_ref, dp_ref, p_ref, ds_ref):
        g = pl.program_id(0)
        b = pl.program_id(1)
        nb = jnp.where(g == 0, nbs[0], jnp.where(g == 1, nbs[1], nbs[2]))
        no_prev = (b % nb) == 0

        @pl.when(b == 0)
        def _():
            dbias_ref[...] = jnp.zeros_like(dbias_ref)
            ck_ref[...] = jnp.zeros_like(ck_ref)
            cv_ref[...] = jnp.zeros_like(cv_ref)

        @pl.when(b < nblk)
        def _():
            col = lax.broadcasted_iota(jnp.int32, (1, 2 * B), 1)
            pen = jnp.where((col < B) & no_prev, NEG_INF, 0.0)
            heads = [slice(h * hd, (h + 1) * hd) for h in range(nh)]
            for h, hs in enumerate(heads):
                kw = jnp.concatenate([kp_ref[0, :, hs], kc_ref[0, :, hs]], axis=0)
                vw = jnp.concatenate([vp_ref[0, :, hs], vc_ref[0, :, hs]], axis=0)
                s_ref[h] = _dot(q_ref[0, :, hs], kw, NT)
                dp_ref[h] = _dot(do_ref[0, :, hs], vw, NT)
            for h, hs in enumerate(heads):
                lse_col = lse_ref[0, :, h * hd:h * hd + 1]
                d_col = dv_ref[0, :, h * hd:h * hd + 1]
                p = jnp.exp(s_ref[h] * scale + bias_ref[0, h] + pen - lse_col)
                ds = p * (dp_ref[h] + d_col)
                dbias_ref[0, h] += ds
                p_ref[h] = p.astype(BF16)
                ds_ref[h] = ds.astype(BF16)
            for h, hs in enumerate(heads):
                qh = q_ref[0, :, hs]
                kw = jnp.concatenate([kp_ref[0, :, hs], kc_ref[0, :, hs]], axis=0)
                dq_ref[0, :, hs] = (_dot(ds_ref[h], kw, NN) * scale).astype(BF16)
                dkw = _dot(ds_ref[h], qh, TN) * scale
                dvw = _dot(p_ref[h], do_ref[0, :, hs], TN)
                dk_ref[0, :, hs] = (ck_ref[:, hs] + dkw[:B]).astype(BF16)
                dvo_ref[0, :, hs] = (cv_ref[:, hs] + dvw[:B]).astype(BF16)
                ck_ref[:, hs] = dkw[B:]
                cv_ref[:, hs] = dvw[B:]

        @pl.when(b == nblk)
        def _():
            dk_ref[0] = ck_ref[...].astype(BF16)
            dvo_ref[0] = cv_ref[...].astype(BF16)

    last = nblk - 1
    cur = pl.BlockSpec((1, B, gw), lambda g, b: (g, jnp.minimum(b, last), 0))
    prev = pl.BlockSpec((1, B, gw), lambda g, b: (g, jnp.clip(b - 1, 0, last), 0))
    tab = pl.BlockSpec((1, nh, B, 2 * B), lambda g, b: (g, 0, 0, 0))
    sd = jax.ShapeDtypeStruct
    return pl.pallas_call(
        body, name=name, grid=(ng, nblk + 1),
        in_specs=[cur, cur, prev, cur, prev, cur, cur, cur, tab],
        out_specs=[cur, prev, prev, tab],
        out_shape=[sd(q.shape, BF16), sd(q.shape, BF16), sd(q.shape, BF16), sd(biasm.shape, F32)],
        scratch_shapes=[pltpu.VMEM((B, gw), F32), pltpu.VMEM((B, gw), F32), pltpu.VMEM((nh, B, 2 * B), F32),
                        pltpu.VMEM((nh, B, 2 * B), F32), pltpu.VMEM((nh, B, 2 * B), BF16), pltpu.VMEM((nh, B, 2 * B), BF16)],
        compiler_params=_params("parallel", "arbitrary"),
    )(q, k, k, v, v, do, dvec, lse, biasm)


def _attn_mix(os, lses, z, name):
    T, gw = os[0].shape
    C = z.shape[1]
    tm = _pick(T, 512, SUBLANES)

    def body(o0_ref, o1_ref, o2_ref, l0_ref, l1_ref, l2_ref, z_ref, out_ref):
        ls = [l0_ref[...], l1_ref[...], l2_ref[...]]
        mx = jnp.maximum(jnp.maximum(ls[0], ls[1]), ls[2])
        es = [jnp.exp(l - mx) for l in ls]
        den = es[0] + es[1] + es[2]
        for i, o_ref in enumerate((o0_ref, o1_ref, o2_ref)):
            sz, _ = _silu_and_grad(z_ref[:, i * gw:(i + 1) * gw].astype(F32))
            out_ref[:, i * gw:(i + 1) * gw] = (o_ref[...] * (es[i] / den) * sz).astype(BF16)

    row = pl.BlockSpec((tm, C), lambda i: (i, 0))
    grp = pl.BlockSpec((tm, gw), lambda i: (i, 0))
    return pl.pallas_call(
        body, name=name, grid=(T // tm,), in_specs=[grp] * 6 + [row], out_specs=row,
        out_shape=jax.ShapeDtypeStruct((T, C), BF16), compiler_params=_params("parallel"),
    )(*os, *lses, z)


def _attn_mix_bwd(dout, os, lses, z, dproj, col, name):
    T, gw = os[0].shape
    C = z.shape[1]
    tm = _pick(T, 512, SUBLANES)
    head_of = np.arange(gw) // ATTN_HEAD_DIM
    ones = jnp.asarray(head_of[:, None] == head_of[None, :], BF16)

    def body(dout_ref, o0_ref, o1_ref, o2_ref, l0_ref, l1_ref, l2_ref, z_ref, ones_ref, _,
             dz_ref, do0_ref, do1_ref, do2_ref, dv0_ref, dv1_ref, dv2_ref):
        ls = [l0_ref[...], l1_ref[...], l2_ref[...]]
        mx = jnp.maximum(jnp.maximum(ls[0], ls[1]), ls[2])
        es = [jnp.exp(l - mx) for l in ls]
        den = es[0] + es[1] + es[2]
        alphas, ebar = [], 0.0
        for i, (o_ref, do_ref) in enumerate(((o0_ref, do0_ref), (o1_ref, do1_ref), (o2_ref, do2_ref))):
            sl = slice(i * gw, (i + 1) * gw)
            alpha = es[i] / den
            ov = o_ref[...]
            dv = dout_ref[:, sl]
            sz, dsz = _silu_and_grad(z_ref[:, sl].astype(F32))
            dz_ref[:, sl] = (dv * ov * alpha * dsz).astype(BF16)
            da = dv * sz
            do_ref[...] = da * alpha
            t = da * ov
            t1 = t.astype(BF16)
            r1 = t - t1.astype(F32)
            t2 = r1.astype(BF16)
            t3 = (r1 - t2.astype(F32)).astype(BF16)
            e = _dot(t1, ones_ref[...], NN) + _dot(t2, ones_ref[...], NN) + _dot(t3, ones_ref[...], NN)
            ebar = ebar + alpha * e
            alphas.append(alpha)
        for alpha, dv_ref in zip(alphas, (dv0_ref, dv1_ref, dv2_ref)):
            dv_ref[...] = -alpha * ebar

    row = pl.BlockSpec((tm, C), lambda i: (i, 0))
    grp = pl.BlockSpec((tm, gw), lambda i: (i, 0))
    sd = jax.ShapeDtypeStruct
    res = pl.pallas_call(
        body, name=name, grid=(T // tm,),
        in_specs=[row] + [grp] * 6 + [row, pl.BlockSpec((gw, gw), lambda i: (0, 0)), pl.BlockSpec(memory_space=pl.ANY)],
        out_specs=[pl.BlockSpec((tm, C), lambda i: (i, col // C))] + [grp] * 6,
        out_shape=[sd(dproj.shape, BF16)] + [sd((T, gw), F32)] * 6, input_output_aliases={9: 0},
        compiler_params=_params("parallel"),
    )(dout, *os, *lses, z, ones, dproj)
    return res[0], res[1:4], res[4:7]


def _mem_attn(qz, kv, name):
    T = qz.shape[0]
    dm = qz.shape[1] // 2
    M = kv.shape[0]
    hd = dm // MEM_HEADS
    scale = hd ** -0.5
    tm = _pick(T, 512, SUBLANES)

    def body(q_ref, z_ref, k_ref, v_ref, o_ref, s_ref, p_ref):
        heads = [slice(h * hd, (h + 1) * hd) for h in range(MEM_HEADS)]
        for h, sl in enumerate(heads):
            s_ref[h] = _dot(q_ref[:, sl].astype(BF16), k_ref[:, sl], NT)
        for h, sl in enumerate(heads):
            s = s_ref[h] * scale
            p = jnp.exp(s - jnp.max(s, axis=-1, keepdims=True))
            p_ref[h] = (p / jnp.sum(p, axis=-1, keepdims=True)).astype(BF16)
        for h, sl in enumerate(heads):
            sz, _ = _silu_and_grad(z_ref[:, sl].astype(F32))
            o_ref[:, sl] = (_dot(p_ref[h], v_ref[:, sl], NN) * sz).astype(BF16)

    return pl.pallas_call(
        body, name=name, grid=(T // tm,),
        in_specs=[pl.BlockSpec((tm, dm), lambda i: (i, 0)), pl.BlockSpec((tm, dm), lambda i: (i, 1)),
                  pl.BlockSpec((M, dm), lambda i: (0, 0)), pl.BlockSpec((M, dm), lambda i: (0, 1))],
        out_specs=pl.BlockSpec((tm, dm), lambda i: (i, 0)),
        out_shape=jax.ShapeDtypeStruct((T, dm), BF16),
        scratch_shapes=[pltpu.VMEM((MEM_HEADS, tm, M), F32), pltpu.VMEM((MEM_HEADS, tm, M), BF16)],
        compiler_params=_params("parallel"),
    )(qz, qz, kv, kv)


def _mem_attn_bwd(do, qz, kv, dproj, col, name):
    T = qz.shape[0]
    dm = qz.shape[1] // 2
    M = kv.shape[0]
    hd = dm // MEM_HEADS
    scale = hd ** -0.5
    tm = _pick(T, 512, SUBLANES)

    def body(do_ref, q_ref, z_ref, k_ref, v_ref, _, dq_ref, dz_ref, dk_ref, dv_ref, s_ref, dp_ref, p_ref, ds_ref, dob_ref):
        @pl.when(pl.program_id(0) == 0)
        def _():
            dk_ref[...] = jnp.zeros_like(dk_ref)
            dv_ref[...] = jnp.zeros_like(dv_ref)

        heads = [slice(h * hd, (h + 1) * hd) for h in range(MEM_HEADS)]
        for h, sl in enumerate(heads):
            sz, _ = _silu_and_grad(z_ref[:, sl].astype(F32))
            dob = (do_ref[:, sl] * sz).astype(BF16)
            dob_ref[:, sl] = dob
            s_ref[h] = _dot(q_ref[:, sl].astype(BF16), k_ref[:, sl], NT)
            dp_ref[h] = _dot(dob, v_ref[:, sl], NT)
        for h, sl in enumerate(heads):
            s = s_ref[h] * scale
            p = jnp.exp(s - jnp.max(s, axis=-1, keepdims=True))
            pn = p / jnp.sum(p, axis=-1, keepdims=True)
            dp = dp_ref[h]
            p_ref[h] = pn.astype(BF16)
            ds_ref[h] = (pn * (dp - jnp.sum(dp * pn, axis=-1, keepdims=True))).astype(BF16)
        for h, sl in enumerate(heads):
            _, dsz = _silu_and_grad(z_ref[:, sl].astype(F32))
            dz_ref[:, sl] = (do_ref[:, sl] * _dot(p_ref[h], v_ref[:, sl], NN) * dsz).astype(BF16)
            dq_ref[:, sl] = (_dot(ds_ref[h], k_ref[:, sl], NN) * scale).astype(BF16)
            dk_ref[:, sl] += _dot(ds_ref[h], q_ref[:, sl].astype(BF16), TN) * scale
            dv_ref[:, sl] += _dot(p_ref[h], dob_ref[:, sl], TN)

    rowq = pl.BlockSpec((tm, dm), lambda i: (i, 0))
    rowz = pl.BlockSpec((tm, dm), lambda i: (i, 1))
    kb = pl.BlockSpec((M, dm), lambda i: (0, 0))
    vb = pl.BlockSpec((M, dm), lambda i: (0, 1))
    sd = jax.ShapeDtypeStruct
    dq, dz, dk, dv = pl.pallas_call(
        body, name=name, grid=(T // tm,), in_specs=[rowq, rowq, rowz, kb, vb, pl.BlockSpec(memory_space=pl.ANY)],
        out_specs=[pl.BlockSpec((tm, dm), lambda i: (i, col // dm)), rowq, kb, kb],
        out_shape=[sd(dproj.shape, BF16), sd((T, dm), BF16), sd((M, dm), F32), sd((M, dm), F32)],
        scratch_shapes=[pltpu.VMEM((MEM_HEADS, tm, M), F32), pltpu.VMEM((MEM_HEADS, tm, M), F32),
                        pltpu.VMEM((MEM_HEADS, tm, M), BF16), pltpu.VMEM((MEM_HEADS, tm, M), BF16), pltpu.VMEM((tm, dm), BF16)],
        input_output_aliases={5: 0}, compiler_params=_params("arbitrary"),
    )(do, qz, qz, kv, kv, dproj)
    return dq, dz, dk, dv


def _merge(os, ws, L, logits, b_gate, name):
    T = os[0].shape[0]
    D = ws[0].shape[2]
    tm = _pick(T, 512, SUBLANES)

    def body(o0_ref, o1_ref, o2_ref, w0_ref, w1_ref, w2_ref, l_ref, b_ref, m_ref, p_ref):
        acc = 0.0
        for i, (o_ref, w_ref) in enumerate(((o0_ref, w0_ref), (o1_ref, w1_ref), (o2_ref, w2_ref))):
            sl = slice(i * D, (i + 1) * D)
            bp = _dot(o_ref[...], w_ref[...], NN)
            p_ref[i] = bp.astype(BF16)
            acc = acc + _sigmoid(l_ref[:, sl].astype(F32) + b_ref[:, sl]) * bp
        m_ref[...] = acc.astype(BF16)

    return pl.pallas_call(
        body, name=name, grid=(T // tm,),
        in_specs=[pl.BlockSpec((tm, o.shape[1]), lambda i: (i, 0)) for o in os]
        + [pl.BlockSpec((None,) + w.shape[1:], lambda i: (L, 0, 0)) for w in ws]
        + [pl.BlockSpec((tm, 3 * D), lambda i: (i, 0)), pl.BlockSpec((1, 3 * D), lambda i: (0, 0))],
        out_specs=[pl.BlockSpec((tm, D), lambda i: (i, 0)), pl.BlockSpec((3, tm, D), lambda i: (0, i, 0))],
        out_shape=[jax.ShapeDtypeStruct((T, D), BF16), jax.ShapeDtypeStruct((3, T, D), BF16)],
        compiler_params=_params("parallel"),
    )(*os, *ws, logits, b_gate.reshape(1, 3 * D))


def _merge_bwd(dmerged, bps, logits, b_gate, dproj_cols, dl_off, name):
    _, T, D = bps.shape
    tm = _pick(T, 2048, SUBLANES)
    cw = _pick(math.gcd(dl_off, D), 512, LANES)
    per = D // cw

    def body(dm_ref, p_ref, l_ref, b_ref, dl_ref, d_ref, db_ref):
        @pl.when(pl.program_id(1) == 0)
        def _():
            db_ref[...] = jnp.zeros_like(db_ref)

        dmv = dm_ref[...]
        gt = _sigmoid(l_ref[...].astype(F32) + b_ref[...])
        d_ref[...] = (dmv * gt).astype(BF16)
        dl = dmv * p_ref[...].astype(F32) * gt * (1.0 - gt)
        dl_ref[...] = dl.astype(BF16)
        db_ref[...] += jnp.sum(dl, axis=0, keepdims=True)

    stacked = pl.BlockSpec((None, tm, cw), lambda j, i: (j // per, i, j % per))
    sd = jax.ShapeDtypeStruct
    return pl.pallas_call(
        body, name=name, grid=(3 * per, T // tm),
        in_specs=[pl.BlockSpec((tm, cw), lambda j, i: (i, j % per)), stacked, pl.BlockSpec((tm, cw), lambda j, i: (i, j)),
                  pl.BlockSpec((1, cw), lambda j, i: (0, j))],
        out_specs=[pl.BlockSpec((tm, cw), lambda j, i: (i, dl_off // cw + j)), stacked, pl.BlockSpec((1, cw), lambda j, i: (0, j))],
        out_shape=[sd((T, dproj_cols), BF16), sd((3, T, D), BF16), sd((1, 3 * D), F32)],
        compiler_params=_params("parallel", "arbitrary"),
    )(dmerged, bps, logits, b_gate.reshape(1, 3 * D))


def _to_segments(a):
    T, C = a.shape
    return a.reshape(SSM_SEGMENTS, T // SSM_SEGMENTS, C).transpose(1, 0, 2).reshape(T, C)


def _from_segments(a):
    T, C = a.shape
    return a.reshape(T // SSM_SEGMENTS, SSM_SEGMENTS, C).transpose(1, 0, 2).reshape(T, C)


def _to_residues(a):
    T = a.shape[0]
    gw = HEADS_PER_GROUP * ATTN_HEAD_DIM
    out = []
    for g, (_, r) in enumerate(ATTN_CONFIGS):
        ag = a[:, g * gw:(g + 1) * gw].reshape(T // r, r, gw)
        out.append(ag.transpose(1, 0, 2).reshape(T, gw))
    return jnp.stack(out)


def _from_residues(a):
    _, T, gw = a.shape
    out = []
    for g, (_, r) in enumerate(ATTN_CONFIGS):
        out.append(a[g].reshape(r, T // r, gw).transpose(1, 0, 2).reshape(T, gw))
    return jnp.concatenate(out, axis=1)


def _block_diag(w):
    nblk, ng, a, b = w.shape
    eye = jnp.eye(ng, dtype=w.dtype)
    return (w[:, :, :, None, :] * eye[None, :, None, :, None]).reshape(nblk, ng * a, ng * b)


def _block_diag_part(m, a, b):
    nblk = m.shape[0]
    ng = m.shape[1] // a
    m5 = m.reshape(nblk, ng, a, ng, b)
    eye = jnp.eye(ng, dtype=m.dtype)
    return jnp.sum(m5 * eye[None, :, None, :, None], axis=3)


def _ssm_matrices(p, L, tag):
    G, P = p["ssm_lambda_re"].shape[1:]
    Hg = SSM_GROUP
    gpb = SSM_BLOCK_CH // Hg
    nblk = G // gpb
    br = p["ssm_b_re"][L].transpose(2, 0, 1)
    bi = p["ssm_b_im"][L].transpose(2, 0, 1)
    disc_in = (p["ssm_lambda_re"][L], p["ssm_lambda_im"][L], p["ssm_log_dt"][L].reshape(G, 1), br, bi)
    ar, ai, bbr, bbi = _ssm_disc(*disc_in, name=f"ssm_disc_{tag}")
    amat = (ar.reshape(nblk // 2, SUBLANES, LANES), ai.reshape(nblk // 2, SUBLANES, LANES))
    bbr_g = bbr.transpose(1, 0, 2).reshape(nblk, gpb, Hg, P)
    bbi_g = bbi.transpose(1, 0, 2).reshape(nblk, gpb, Hg, P)
    bmat = jnp.concatenate([_block_diag(bbr_g), _block_diag(bbi_g)], axis=2).astype(BF16)
    cre = p["ssm_c_re"][L].reshape(nblk, gpb, Hg, P).transpose(0, 1, 3, 2)
    cim = p["ssm_c_im"][L].reshape(nblk, gpb, Hg, P).transpose(0, 1, 3, 2)
    cmat = jnp.concatenate([_block_diag(cre), -_block_diag(cim)], axis=1).astype(BF16)
    return disc_in, amat, bmat, cmat


def _layer_fwd(x, mem, p, wb, L, biasm):
    T, D = x.shape
    C = p["ssm_d"].shape[1]
    dm = wb["w_br_mem"].shape[1]
    tag = f"l{L}"
    s = {"x": x}
    h = _rmsnorm(x, p["norm_g"][L], f"norm_{tag}")
    offs = [int(o) for o in np.cumsum([0, C, C, 3 * 768, 768, 2 * dm, 3 * D])]
    names = ("uz", "z_ssm", "qkv", "z_attn", "qz_mem", "logits")
    dts = (F32, BF16, F32, BF16, BF16, BF16)
    for i, (nm, dt) in enumerate(zip(names, dts)):
        tiles = dict(tm=512, tn=1024) if (L == 1 and nm == "logits") else {}
        s[nm] = _matmul(h, wb["w_in"], mode="nn", name=f"in_{nm}_{tag}", out_dtype=dt, b_lead=L, b_off=offs[i],
                        n_cols=offs[i + 1] - offs[i], **tiles)
    s["h"] = h

    disc_in, amat, bmat, cmat = _ssm_matrices(p, L, tag)
    dvec = p["ssm_d"][L].reshape(1, C)
    y, xr, xi = _ssm_fwd(s["uz"], bmat, cmat, *amat, dvec, f"ssm_scan_{tag}")
    o_ssm, a_glu = _ssm_post(y, s["z_ssm"], wb["w_glu"][L], p["b_glu"][L], f"ssm_post_{tag}")
    s.update(disc_in=disc_in, amat=amat, bmat=bmat, cmat=cmat, xr=xr, xi=xi, y=y, a_glu=a_glu, o_ssm=o_ssm)

    groups = [_attn_group_fwd(s["qkv"], biasm, g, f"attn_g{g}_{tag}") for g in range(len(ATTN_CONFIGS))]
    os, lses = [o for o, _ in groups], [l for _, l in groups]
    o_attn = _attn_mix(os, lses, s["z_attn"], f"attn_mix_{tag}")
    s.update(os=os, lses=lses, o_attn=o_attn)

    mn = _rmsnorm(mem, p["mem_norm_g"][L], f"mem_norm_{tag}")
    kv = _matmul(mn, wb["w_mem_kv"], mode="nn", name=f"mem_kv_{tag}", out_dtype=BF16, b_lead=L)
    o_mem = _mem_attn(s["qz_mem"], kv, f"mem_attn_{tag}")
    s.update(mn=mn, kv=kv, o_mem=o_mem)

    merged, bps = _merge([o_ssm, o_attn, o_mem], [wb["w_br_ssm"], wb["w_br_attn"], wb["w_br_mem"]], L, s["logits"],
                         p["b_gate"][L], f"merge_{tag}")
    s.update(bps=bps, merged=merged)
    x_new = _matmul(merged, wb["w_out"], mode="nn", name=f"out_{tag}", add=x, b_lead=L)
    return x_new, s


def _layer_bwd(dx, mem, p, wb, L, s, biasm, gprev):
    T, D = dx.shape
    C = p["ssm_d"].shape[1]
    depth = p["norm_g"].shape[0]
    tag = f"l{L}"
    g = {}

    def wgrad(n, a, b, **tiles):
        g[n] = _matmul(a, b, mode="tn", name=f"d{n}_{tag}", out_dtype=BF16, stack=(L, depth, gprev.get(n)), **tiles)

    dmerged = _matmul(dx, wb["w_out"], mode="nt", name=f"d_merged_{tag}", b_lead=L)
    wgrad("w_out", s["merged"], dx)
    dm = s["qz_mem"].shape[1] // 2
    col = dict(zip(("u", "z_ssm", "q", "k", "v", "z_attn", "q_mem", "z_mem", "logits", "end"),
                   (int(o) for o in np.cumsum([0, C, C, 768, 768, 768, 768, dm, dm, 3 * D]))))
    dproj, dbps, g["b_gate"] = _merge_bwd(dmerged, s["bps"], s["logits"], p["b_gate"][L], col["end"], col["logits"],
                                          f"merge_bwd_{tag}")
    dos = []
    for i, (o, n) in enumerate(((s["o_ssm"], "w_br_ssm"), (s["o_attn"], "w_br_attn"), (s["o_mem"], "w_br_mem"))):
        dos.append(_matmul(dbps, wb[n], mode="nt", name=f"d_o_{n}_{tag}", a_lead=i, b_lead=L))
        g[n] = _matmul(o, dbps, mode="tn", name=f"d{n}_{tag}", out_dtype=BF16, b_lead=i, stack=(L, depth, gprev.get(n)))

    dy, dproj, ds_glu, g["b_glu"] = _ssm_post_bwd(dos[0], s["y"], s["z_ssm"], wb["w_glu"][L], p["b_glu"][L], dproj,
                                                  col["z_ssm"], f"ssm_post_bwd_{tag}")
    wgrad("w_glu", s["a_glu"], ds_glu)
    dvec = p["ssm_d"][L].reshape(1, C)
    dproj, dbm, dct, dar, dai, g["ssm_d"] = _ssm_bwd(dy, s["uz"], s["xr"], s["xi"], s["bmat"], s["cmat"], *s["amat"], dvec,
                                                     dproj, f"ssm_scan_bwd_{tag}")
    G, P = p["ssm_lambda_re"].shape[1:]
    Hg = SSM_GROUP
    half = dbm.shape[2] // 2
    dbbr = _block_diag_part(dbm[:, :, :half], Hg, P).reshape(G, Hg, P).transpose(1, 0, 2)
    dbbi = _block_diag_part(dbm[:, :, half:], Hg, P).reshape(G, Hg, P).transpose(1, 0, 2)
    g["ssm_c_re"] = _block_diag_part(dct[:, :, :half], Hg, P).reshape(G, Hg, P)
    g["ssm_c_im"] = -_block_diag_part(dct[:, :, half:], Hg, P).reshape(G, Hg, P)
    glre, glim, gdt, gbr, gbi = _ssm_disc_bwd(*s["disc_in"], dar.reshape(G, P), dai.reshape(G, P), dbbr, dbbi,
                                              name=f"ssm_disc_bwd_{tag}")
    g["ssm_lambda_re"], g["ssm_lambda_im"], g["ssm_log_dt"] = glre, glim, gdt.reshape(G)
    g["ssm_b_re"] = gbr.transpose(1, 2, 0)
    g["ssm_b_im"] = gbi.transpose(1, 2, 0)

    dproj, do_g, dvec_g = _attn_mix_bwd(dos[1], s["os"], s["lses"], s["z_attn"], dproj, col["z_attn"], f"attn_mix_bwd_{tag}")
    rest, dbias = [], []
    for gi in range(len(ATTN_CONFIGS)):
        dq_g, dproj, dv_g, db_g = _attn_group_bwd(s["qkv"], do_g[gi], dvec_g[gi], s["lses"][gi], biasm, gi, dproj, col["k"],
                                                  f"attn_bwd_g{gi}_{tag}")
        gw = dq_g.shape[1]
        rest += [(dq_g, col["q"] + gi * gw), (dv_g, col["v"] + gi * gw)]
        dbias.append(db_g)
    dbias = jnp.stack(dbias)

    dproj, dz_mem, dk_mem, dv_mem = _mem_attn_bwd(dos[2], s["qz_mem"], s["kv"], dproj, col["q_mem"], f"mem_attn_bwd_{tag}")
    rest.append((dz_mem, col["z_mem"]))
    for piece, at in rest:
        dproj = lax.dynamic_update_slice(dproj, piece.astype(BF16), (0, at))
    dkv = jnp.concatenate([dk_mem, dv_mem], axis=1)
    wgrad("w_mem_kv", s["mn"], dkv)
    dmn = _matmul(dkv, wb["w_mem_kv"], mode="nt", name=f"d_mn_{tag}", b_lead=L)
    _, g["mem_norm_g"] = _rmsnorm_bwd(mem, p["mem_norm_g"][L], dmn, None, f"mem_norm_bwd_{tag}")

    dh = _matmul(dproj, wb["w_in"], mode="nt", name=f"d_h_{tag}", b_lead=L, **(dict(tm=512, tn=1024) if L == 1 else {}))
    wgrad("w_in", s["h"], dproj, tn=2304 if L == 1 else 1152, tk=1024)
    dx_in, g["norm_g"] = _rmsnorm_bwd(s["x"], p["norm_g"][L], dh, dx, f"norm_bwd_{tag}")
    return dx_in, g, dbias


def _bucket_onehot(gi):
    buckets, bands = _band_tables()
    hit = (buckets[gi].reshape(1, -1) == jnp.arange(NUM_BUCKETS)[:, None]) & bands[gi].reshape(1, -1)
    return hit.astype(BF16)


def _bias_tables(rel_bias, name):
    _, bands = _band_tables()
    out = []
    for gi in range(len(ATTN_CONFIGS)):
        tab = rel_bias[:, gi * HEADS_PER_GROUP:(gi + 1) * HEADS_PER_GROUP].T
        flat = _matmul(tab, _bucket_onehot(gi), mode="nn", name=f"{name}_{gi}", split_a=3, tn=4096)
        out.append(jnp.where(bands[gi][None], flat.reshape(HEADS_PER_GROUP, ATTN_BLOCK, 2 * ATTN_BLOCK), NEG_INF))
    return jnp.stack(out)


def _rel_bias_grad(dbias_sum, name):
    cols = []
    for gi in range(len(ATTN_CONFIGS)):
        flat = dbias_sum[gi].reshape(HEADS_PER_GROUP, -1)
        cols.append(_matmul(flat, _bucket_onehot(gi), mode="nt", name=f"{name}_{gi}", split_a=2, tk=4096).T)
    return jnp.concatenate(cols, axis=1)


def _local_step(x, mem, target, p, wb):
    depth = p["norm_g"].shape[0]
    biasm = _bias_tables(p["rel_bias"], "bias_table")
    saved = []
    for L in range(depth):
        x, s = _layer_fwd(x, mem, p, wb, L, biasm)
        saved.append(s)
    loss_vec, dx, dgf = _loss_head(x, p["final_norm_g"], target, "loss_head")
    grads = {"final_norm_g": dgf.reshape(-1)}
    per_layer = [None] * depth
    dbias_sum = 0.0
    stacked = {}
    for L in reversed(range(depth)):
        dx, per_layer[L], dbias = _layer_bwd(dx, mem, p, wb, L, saved[L], biasm, stacked)
        stacked = {n: per_layer[L][n] for n, _ in BIG}
        dbias_sum = dbias_sum + dbias
    grads.update(stacked)
    for n in per_layer[0]:
        if n not in stacked:
            grads[n] = jnp.stack([per_layer[L][n].reshape(p[n].shape[1:]) for L in range(depth)])
    grads["rel_bias"] = _rel_bias_grad(dbias_sum, "d_rel_bias")
    return jnp.sum(loss_vec), dx, grads


def _chip_coords(j):
    return j // 2, j % 2


def _place_shard(shard, ax, chip, name):
    _, a, b = shard.shape
    ra = _pick(a, 256, 16)
    full = (2, a * N_CHIPS, b) if ax == 1 else (2, a, b * N_CHIPS)
    per = a // ra

    def body(j_ref, s_ref, o_ref):
        o_ref[...] = s_ref[...].astype(BF16)

    out_idx = (lambda l, i, j: (l, j[0] * per + i, 0)) if ax == 1 else (lambda l, i, j: (l, i, j[0]))
    return pl.pallas_call(
        body, name=name,
        grid_spec=pltpu.PrefetchScalarGridSpec(
            num_scalar_prefetch=1, grid=(2, per),
            in_specs=[pl.BlockSpec((None, ra, b), lambda l, i, j: (l, i, 0))],
            out_specs=pl.BlockSpec((None, ra, b), out_idx)),
        out_shape=jax.ShapeDtypeStruct(full, BF16), compiler_params=_params("parallel", "parallel"),
    )(chip, shard)


def _gather_shards(fulls, axes, name):
    n = len(fulls)
    widths = [a.shape[ax] // N_CHIPS for a, ax in zip(fulls, axes)]
    aligns = [LANES if ax == 2 else 16 for ax in axes]

    def body(*refs):
        outs = refs[n:2 * n]
        send_sems, recv_sems, fsend_sems, frecv_sems = refs[2 * n:]
        x, y, c = lax.axis_index("x"), lax.axis_index("y"), lax.axis_index("c")
        mine = 2 * x + y
        sibling = (x, y, 1 - c)

        def window(t, layer, j):
            start = pl.ds(pl.multiple_of(j * widths[t], aligns[t]), widths[t])
            return outs[t].at[(layer, start, slice(None)) if axes[t] == 1 else (layer, slice(None), start)]

        def over_ici(t, j, block):
            return pltpu.make_async_remote_copy(
                src_ref=window(t, c, mine), dst_ref=window(t, c, block), send_sem=send_sems.at[t, j],
                recv_sem=recv_sems.at[t, block], device_id=(*_chip_coords(j), c), device_id_type=MESH)

        def over_d2d(t, j, layer):
            return pltpu.make_async_remote_copy(
                src_ref=window(t, layer, j), dst_ref=window(t, layer, j), send_sem=fsend_sems.at[t, j],
                recv_sem=frecv_sems.at[t, j], device_id=sibling, device_id_type=MESH)

        for t in range(n):
            for j in range(N_CHIPS):
                @pl.when(j != mine)
                def _():
                    over_ici(t, j, mine).start()
        for t in range(n):
            for j in range(N_CHIPS):
                @pl.when(j != mine)
                def _():
                    over_ici(t, j, j).wait_recv()
                    over_d2d(t, j, c).start()
        for t in range(n):
            for j in range(N_CHIPS):
                @pl.when(j != mine)
                def _():
                    over_ici(t, j, mine).wait_send()
                    over_d2d(t, j, c).wait_send()
                    over_d2d(t, j, 1 - c).wait_recv()

    sem = pltpu.SemaphoreType.DMA
    return pl.pallas_call(
        body, name=name, in_specs=[HBM] * n, out_specs=[HBM] * n,
        out_shape=[jax.ShapeDtypeStruct(a.shape, a.dtype) for a in fulls],
        input_output_aliases={t: t for t in range(n)},
        scratch_shapes=[sem((n, N_CHIPS)), sem((n, N_CHIPS)), sem((n, N_CHIPS)), sem((n, N_CHIPS))],
    )(*fulls)


def _scatter_slices(arrays, axes, name):
    n = len(arrays)

    def piece(a, ax):
        if ax is None:
            return a.shape, None
        w = a.shape[ax] // N_CHIPS
        return a.shape[:ax] + (w,) + a.shape[ax + 1:], w

    shapes = [piece(a, ax) for a, ax in zip(arrays, axes)]

    def body(*refs):
        ins, outs = refs[:n], refs[n:2 * n]
        send_sems, recv_sems, loc_sems = refs[2 * n:]
        x, y, c = lax.axis_index("x"), lax.axis_index("y"), lax.axis_index("c")
        mine = 2 * x + y

        def src(t, j):
            ax, w = axes[t], shapes[t][1]
            if ax is None:
                return ins[t]
            idx = tuple(pl.ds(j * w, w) if d == ax else slice(None) for d in range(len(arrays[t].shape)))
            return ins[t].at[idx]

        for t in range(n):
            for j in range(N_CHIPS):
                @pl.when(j == mine)
                def _():
                    pltpu.make_async_copy(src(t, j), outs[t].at[j], loc_sems.at[t]).start()

                @pl.when(j != mine)
                def _():
                    pltpu.make_async_remote_copy(
                        src_ref=src(t, j), dst_ref=outs[t].at[mine], send_sem=send_sems.at[t, j], recv_sem=recv_sems.at[t, mine],
                        device_id=(*_chip_coords(j), c), device_id_type=MESH).start()
        for t in range(n):
            for j in range(N_CHIPS):
                @pl.when(j == mine)
                def _():
                    pltpu.make_async_copy(src(t, j), outs[t].at[j], loc_sems.at[t]).wait()

                @pl.when(j != mine)
                def _():
                    cp = pltpu.make_async_remote_copy(
                        src_ref=src(t, j), dst_ref=outs[t].at[j], send_sem=send_sems.at[t, j], recv_sem=recv_sems.at[t, j],
                        device_id=(*_chip_coords(j), c), device_id_type=MESH)
                    cp.wait_send()
                    cp.wait_recv()

    return pl.pallas_call(
        body, name=name, in_specs=[HBM] * n, out_specs=[HBM] * n,
        out_shape=[jax.ShapeDtypeStruct((N_CHIPS,) + sh, a.dtype) for a, (sh, _) in zip(arrays, shapes)],
        scratch_shapes=[pltpu.SemaphoreType.DMA((n, N_CHIPS)), pltpu.SemaphoreType.DMA((n, N_CHIPS)), pltpu.SemaphoreType.DMA((n,))],
    )(*arrays)


def _swap_layers(stacked, name):
    n = len(stacked)

    def body(*refs):
        ins, outs = refs[:n], refs[n:2 * n]
        send_sems, recv_sems = refs[2 * n:]
        c = lax.axis_index("c")
        peer = (lax.axis_index("x"), lax.axis_index("y"), 1 - c)
        cps = [pltpu.make_async_remote_copy(src_ref=ins[t].at[1 - c], dst_ref=outs[t], send_sem=send_sems.at[t],
                                            recv_sem=recv_sems.at[t], device_id=peer, device_id_type=MESH) for t in range(n)]
        for cp in cps:
            cp.start()
        for cp in cps:
            cp.wait_send()
            cp.wait_recv()

    return pl.pallas_call(
        body, name=name, in_specs=[HBM] * n, out_specs=[HBM] * n,
        out_shape=[jax.ShapeDtypeStruct(a.shape[1:], a.dtype) for a in stacked],
        scratch_shapes=[pltpu.SemaphoreType.DMA((n,)), pltpu.SemaphoreType.DMA((n,))],
    )(*stacked)


def _merge_layers(stacked, name):
    n = len(stacked)

    def body(*refs):
        outs = refs[n:2 * n]
        send_sems, recv_sems = refs[2 * n:]
        c = lax.axis_index("c")
        peer = (lax.axis_index("x"), lax.axis_index("y"), 1 - c)
        for t in range(n):
            pltpu.make_async_remote_copy(src_ref=outs[t].at[c], dst_ref=outs[t].at[c], send_sem=send_sems.at[t],
                                         recv_sem=recv_sems.at[t], device_id=peer, device_id_type=MESH).start()
        for t in range(n):
            cp = pltpu.make_async_remote_copy(src_ref=outs[t].at[c], dst_ref=outs[t].at[1 - c], send_sem=send_sems.at[t],
                                              recv_sem=recv_sems.at[t], device_id=peer, device_id_type=MESH)
            cp.wait_send()
            cp.wait_recv()

    sem = pltpu.SemaphoreType.DMA
    return pl.pallas_call(
        body, name=name, in_specs=[HBM] * n, out_specs=[HBM] * n,
        out_shape=[jax.ShapeDtypeStruct(a.shape, a.dtype) for a in stacked],
        input_output_aliases={t: t for t in range(n)}, scratch_shapes=[sem((n,)), sem((n,))],
    )(*stacked)


def _pair_sum(stacked, landed, core, name):
    _, K, N = stacked.shape
    tr = _pick(K, max(16, (1 << 19) // N // 16 * 16), 16)

    def body(c_ref, s_ref, l_ref, o_ref):
        o_ref[...] = (s_ref[...].astype(F32) + l_ref[...].astype(F32)).astype(o_ref.dtype)

    return pl.pallas_call(
        body, name=name,
        grid_spec=pltpu.PrefetchScalarGridSpec(
            num_scalar_prefetch=1, grid=(K // tr,),
            in_specs=[pl.BlockSpec((None, tr, N), lambda i, c: (c[0], i, 0)), pl.BlockSpec((tr, N), lambda i, c: (i, 0))],
            out_specs=pl.BlockSpec((tr, N), lambda i, c: (i, 0))),
        out_shape=jax.ShapeDtypeStruct((K, N), stacked.dtype), compiler_params=_params("parallel"),
    )(core, stacked, landed)


def _sum_chips(landed, core, name):
    _, R, C = landed.shape
    tr = _pick(R, max(SUBLANES, (1 << 19) // C // 16 * 16), 16)

    def body(c_ref, l_ref, o_ref):
        acc = l_ref[0].astype(F32) + l_ref[1].astype(F32)
        acc = acc + l_ref[2].astype(F32)
        o_ref[...] = acc + l_ref[3].astype(F32)

    return pl.pallas_call(
        body, name=name,
        grid_spec=pltpu.PrefetchScalarGridSpec(
            num_scalar_prefetch=1, grid=(R // tr,),
            in_specs=[pl.BlockSpec((N_CHIPS, tr, C), lambda i, c: (0, i, 0))],
            out_specs=pl.BlockSpec((None, tr, C), lambda i, c: (c[0], i, 0))),
        out_shape=jax.ShapeDtypeStruct((2, R, C), F32), compiler_params=_params("parallel"),
    )(core, landed)


def _adamw_math(w_ref, g_ref, m_ref, v_ref, d_ref, nm_ref, nv_ref):
    c1 = 1.0 / (1.0 - ADAM_B1 ** ADAM_STEP)
    c2 = 1.0 / (1.0 - ADAM_B2 ** ADAM_STEP)
    g = g_ref[...]
    nm = ADAM_B1 * m_ref[...] + (1.0 - ADAM_B1) * g
    nv = ADAM_B2 * v_ref[...] + (1.0 - ADAM_B2) * (g * g)
    nm_ref[...] = nm
    nv_ref[...] = nv
    d_ref[...] = -ADAM_LR * ((nm * c1) / (jnp.sqrt(nv * c2) + ADAM_EPS) + ADAM_WD * w_ref[...])


def _adamw_whole(w, g, m, v, name):
    shape = w.shape
    view = (-1,) + shape[-2:] if w.ndim >= 2 else (1, 1, -1)

    def body(*refs):
        _adamw_math(*refs)

    res = pl.pallas_call(body, name=name, out_shape=[jax.ShapeDtypeStruct(w.reshape(view).shape, F32)] * 3,
                         compiler_params=pltpu.CompilerParams(vmem_limit_bytes=VMEM_LIMIT_BYTES))(
        *(a.reshape(view) for a in (w, g, m, v)))
    return [r.reshape(shape) for r in res]


def _adamw(w, g, m, v, name):
    R, C = w.shape
    tr = _pick(R, max(SUBLANES, (1 << 18) // C // 8 * 8), SUBLANES)

    def body(*refs):
        _adamw_math(*refs)

    blk = pl.BlockSpec((tr, C), lambda i: (i, 0))
    return pl.pallas_call(
        body, name=name, grid=(R // tr,), in_specs=[blk] * 4, out_specs=[blk] * 3,
        out_shape=[jax.ShapeDtypeStruct((R, C), F32)] * 3, compiler_params=_params("parallel"),
    )(w, g, m, v)


def _pack_small(d, prefix=""):
    flat = jnp.concatenate([d[prefix + n].astype(F32).reshape(-1) for n in SMALL])
    pad = (-flat.shape[0]) % (2 * 16 * LANES)
    return jnp.pad(flat, (0, pad)).reshape(-1, LANES)


def _unpack_small(packed, shapes):
    flat = packed.reshape(-1)
    out, off = {}, 0
    for n in SMALL:
        size = int(np.prod(shapes[n]))
        out[n] = flat[off:off + size].reshape(shapes[n])
        off += size
    return out


def kernel(*args):
    p = dict(zip(INPUTS, args))
    x, mem, target = p["x"][0], p["mem"][0], p["loss_target"][0]

    names = [n for n, _ in BIG] + ["small"]
    core = lax.axis_index("c").astype(jnp.int32).reshape(1)
    chip = (2 * lax.axis_index("x") + lax.axis_index("y")).astype(jnp.int32).reshape(1)
    placed = [_place_shard(p[n], ax, chip, f"place_{n}") for n, ax in BIG]
    wb = dict(zip(names, _gather_shards(placed, [ax for _, ax in BIG], "gather_weights")))

    loss_part, dx, grads = _local_step(x, mem, target, p, wb)
    loss = lax.psum(loss_part, ("x", "y", "c"))

    stacked = [grads[n] for n, _ in BIG] + [_pack_small(grads).reshape(2, -1, LANES)]
    theirs = _swap_layers(stacked, "swap_layers")
    pair = [_pair_sum(s, o, core, f"pair_sum_{n}") for n, s, o in zip(names, stacked, theirs)]
    landed = _scatter_slices(pair, [ax - 1 for _, ax in BIG] + [None], "scatter_grads")
    reduced = [_sum_chips(ld.reshape(N_CHIPS, -1, ld.shape[-1]), core, f"sum_chips_{n}") for n, ld in zip(names, landed)]
    total = _merge_layers(reduced, "merge_layers")

    out = {}
    for (n, _), g in zip(BIG, total):
        sh = p[n].shape
        two_d = lambda a: a.reshape(-1, sh[-1])
        res = (g,) + tuple(_adamw(two_d(p[n]), two_d(g), two_d(p["m_" + n]), two_d(p["v_" + n]), f"adamw_{n}"))
        for key, r in zip(("grad_", "delta_", "new_m_", "new_v_"), res):
            out[key + n] = r.reshape(sh)
    for n, g in _unpack_small(total[-1], {n: p[n].shape for n in SMALL}).items():
        res = (g,) + tuple(_adamw_whole(p[n], g, p["m_" + n], p["v_" + n], f"adamw_{n}"))
        for key, r in zip(("grad_", "delta_", "new_m_", "new_v_"), res):
            out[key + n] = r

    result = [loss, dx.reshape(p["x"].shape)]
    for key in ("grad_", "delta_", "new_m_", "new_v_"):
        result += [out[key + n] for n in WEIGHTS]
    return tuple(result)
```

```python
import math

import jax
import jax.numpy as jnp
import numpy as np
from jax import lax
from jax.experimental import pallas as pl
from jax.experimental.pallas import tpu as pltpu

F32 = jnp.float32
BF16 = jnp.bfloat16
MESH = pl.DeviceIdType.MESH
HBM = pl.BlockSpec(memory_space=pltpu.HBM)

EPS = 1e-6
SSM_GROUP = 16
SSM_STATE = 64
ATTN_HEAD_DIM = 64
HEADS_PER_GROUP = 4
ATTN_CONFIGS = ((128, 1), (512, 4), (2048, 16))
ATTN_BLOCK = 128
NUM_BUCKETS = 32
REL_MAX_DISTANCE = 2048
NEG_INF = -1e30
MEM_HEADS = 4
ADAM_LR = 0.001
ADAM_B1 = 0.9
ADAM_B2 = 0.999
ADAM_EPS = 1e-08
ADAM_WD = 0.01
ADAM_STEP = 10

LANES = 128
SUBLANES = 8
VMEM_LIMIT_BYTES = 48 * 1024 * 1024
SSM_BLOCK_CH = 128

N_CHIPS = 4
BIG = (("w_in", 2), ("w_glu", 1), ("w_mem_kv", 1), ("w_br_ssm", 2), ("w_br_attn", 2), ("w_br_mem", 2), ("w_out", 1))
SMALL = ("norm_g", "mem_norm_g", "b_gate", "ssm_lambda_re", "ssm_lambda_im", "ssm_log_dt", "ssm_b_re", "ssm_b_im",
         "ssm_c_re", "ssm_c_im", "ssm_d", "b_glu", "rel_bias", "final_norm_g")
WEIGHTS = ("norm_g", "mem_norm_g", "w_in", "b_gate", "ssm_lambda_re", "ssm_lambda_im", "ssm_log_dt", "ssm_b_re",
           "ssm_b_im", "ssm_c_re", "ssm_c_im", "ssm_d", "w_glu", "b_glu", "w_mem_kv", "w_br_ssm", "w_br_attn",
           "w_br_mem", "w_out", "rel_bias", "final_norm_g")
INPUTS = ("x", "mem") + WEIGHTS + ("loss_target",) + tuple("m_" + n for n in WEIGHTS) + tuple("v_" + n for n in WEIGHTS)


def _params(*sem):
    return pltpu.CompilerParams(dimension_semantics=sem, vmem_limit_bytes=VMEM_LIMIT_BYTES)


def _pick(dim, pref, align):
    if dim <= pref:
        return dim
    t = pref - pref % align
    while t >= align:
        if dim % t == 0:
            return t
        t -= align
    return dim


def _sigmoid(v):
    return 1.0 / (1.0 + jnp.exp(-v))


def _silu_and_grad(z):
    s = _sigmoid(z)
    return z * s, s * (1.0 + z * (1.0 - s))


_GELU_C = math.sqrt(2.0 / math.pi)


def _gelu_and_grad(y):
    inner = _GELU_C * (y + 0.044715 * y * y * y)
    t = jnp.tanh(inner)
    g = 0.5 * y * (1.0 + t)
    dg = 0.5 * (1.0 + t) + 0.5 * y * (1.0 - t * t) * _GELU_C * (1.0 + 3.0 * 0.044715 * y * y)
    return g, dg


def _dot(a, b, dims):
    return lax.dot_general(a, b, (dims, ((), ())), preferred_element_type=F32)


NN = ((1,), (0,))
NT = ((1,), (1,))
TN = ((0,), (0,))


def _matmul(a, b, *, mode, name, out_dtype=F32, add=None, split_a=1, tm=1024, tn=768, tk=2304,
            a_lead=None, b_lead=None, b_off=0, n_cols=None, stack=None):
    ashape = a.shape if a_lead is None else a.shape[1:]
    K, M = ashape if mode == "tn" else ashape[::-1]
    bshape = b.shape if b_lead is None else b.shape[1:]
    N = n_cols or (bshape[0] if mode == "nt" else bshape[1])
    if mode != "tn" and M >= 4 * tm:
        tm = 2 * tm
    tm = _pick(M, tm, LANES if mode == "tn" else SUBLANES)
    tn = _pick(math.gcd(N, b_off) if b_off else N, tn, LANES)
    tk = _pick(K, tk, LANES)
    nk = K // tk
    joff = b_off // tn
    dims = {"nn": NN, "nt": NT, "tn": TN}[mode]
    has_add = add is not None
    has_prev = stack is not None and stack[2] is not None

    def body(*refs):
        a_ref, b_ref = refs[:2]
        add_ref = refs[2] if has_add else None
        o_ref = refs[-2] if nk > 1 else refs[-1]
        k = pl.program_id(2)
        bv = b_ref[...].astype(BF16)
        if split_a > 1:
            rest = a_ref[...].astype(F32)
            part = 0.0
            for _ in range(split_a):
                piece = rest.astype(BF16)
                part = part + _dot(piece, bv, dims)
                rest = rest - piece.astype(F32)
        else:
            part = _dot(a_ref[...].astype(BF16), bv, dims)

        def finish(r):
            if has_add:
                r = r + add_ref[...]
            o_ref[...] = r.astype(out_dtype)

        if nk == 1:
            finish(part)
            return
        acc_ref = refs[-1]

        @pl.when(k == 0)
        def _():
            acc_ref[...] = part

        @pl.when((k > 0) & (k < nk - 1))
        def _():
            acc_ref[...] += part

        @pl.when(k == nk - 1)
        def _():
            finish(acc_ref[...] + part)

    alead = () if a_lead is None else (a_lead,)
    alead_blk = () if a_lead is None else (None,)
    if mode == "tn":
        a_spec = pl.BlockSpec(alead_blk + (tk, tm), lambda i, j, k: alead + (k, i))
    else:
        a_spec = pl.BlockSpec(alead_blk + (tm, tk), lambda i, j, k: alead + (i, k))
    lead = () if b_lead is None else (b_lead,)
    lead_blk = () if b_lead is None else (None,)
    if mode == "nt":
        b_spec = pl.BlockSpec(lead_blk + (tn, tk), lambda i, j, k: lead + (j + joff, k))
    else:
        b_spec = pl.BlockSpec(lead_blk + (tk, tn), lambda i, j, k: lead + (k, j + joff))
    in_specs = [a_spec, b_spec]
    args = [a, b]
    if has_add:
        in_specs.append(pl.BlockSpec((tm, tn), lambda i, j, k: (i, j)))
        args.append(add)
    aliases = {}
    if stack is None:
        out_spec = pl.BlockSpec((tm, tn), lambda i, j, k: (i, j))
        out_shape = jax.ShapeDtypeStruct((M, N), out_dtype)
    else:
        layer, depth, prev = stack
        out_spec = pl.BlockSpec((None, tm, tn), lambda i, j, k: (layer, i, j))
        out_shape = jax.ShapeDtypeStruct((depth, M, N), out_dtype)
        if has_prev:
            in_specs.append(pl.BlockSpec(memory_space=pl.ANY))
            args.append(prev)
            aliases = {len(args) - 1: 0}
    return pl.pallas_call(
        body, name=name, grid=(M // tm, N // tn, nk), in_specs=in_specs, out_specs=out_spec, out_shape=out_shape,
        scratch_shapes=[pltpu.VMEM((tm, tn), F32)] if nk > 1 else [], input_output_aliases=aliases,
        compiler_params=_params("parallel", "parallel", "arbitrary"),
    )(*args)


def _rmsnorm(x, g, name):
    T, D = x.shape
    tm = _pick(T, 512, SUBLANES)

    def body(x_ref, g_ref, h_ref):
        xv = x_ref[...]
        r = lax.rsqrt(jnp.mean(xv * xv, axis=-1, keepdims=True) + EPS)
        h_ref[...] = (xv * r * g_ref[...]).astype(BF16)

    return pl.pallas_call(
        body, name=name, grid=(T // tm,),
        in_specs=[pl.BlockSpec((tm, D), lambda i: (i, 0)), pl.BlockSpec((1, D), lambda i: (0, 0))],
        out_specs=pl.BlockSpec((tm, D), lambda i: (i, 0)),
        out_shape=jax.ShapeDtypeStruct((T, D), BF16), compiler_params=_params("parallel"),
    )(x, g.reshape(1, D))


def _rmsnorm_bwd(x, g, dh, dres, name):
    T, D = x.shape
    tm = _pick(T, 512, SUBLANES)
    with_res = dres is not None

    def body(*refs):
        if with_res:
            x_ref, g_ref, dh_ref, dres_ref, dx_ref, dg_ref = refs
        else:
            x_ref, g_ref, dh_ref, dx_ref, dg_ref = refs
        xv = x_ref[...]
        dhv = dh_ref[...]
        r = lax.rsqrt(jnp.mean(xv * xv, axis=-1, keepdims=True) + EPS)
        dyg = dhv * g_ref[...]
        c = jnp.mean(dyg * xv, axis=-1, keepdims=True)
        dx = r * dyg - xv * (r * r * r) * c
        if with_res:
            dx = dx + dres_ref[...]
        dx_ref[...] = dx

        @pl.when(pl.program_id(0) == 0)
        def _():
            dg_ref[...] = jnp.zeros_like(dg_ref)

        dg_ref[...] += jnp.sum(dhv * xv * r, axis=0, keepdims=True)

    row = pl.BlockSpec((tm, D), lambda i: (i, 0))
    vec = pl.BlockSpec((1, D), lambda i: (0, 0))
    ins = [x, g.reshape(1, D), dh] + ([dres] if with_res else [])
    return pl.pallas_call(
        body, name=name, grid=(T // tm,), in_specs=[row, vec, row] + ([row] if with_res else []),
        out_specs=[row, vec],
        out_shape=[jax.ShapeDtypeStruct((T, D), F32), jax.ShapeDtypeStruct((1, D), F32)],
        compiler_params=_params("arbitrary"),
    )(*ins)


def _loss_head(x, g, target, name):
    T, D = x.shape
    tm = _pick(T, 512, SUBLANES)

    def body(x_ref, g_ref, t_ref, loss_ref, dx_ref, dg_ref):
        xv = x_ref[...]
        gv = g_ref[...]
        r = lax.rsqrt(jnp.mean(xv * xv, axis=-1, keepdims=True) + EPS)
        e = xv * r * gv - t_ref[...]
        dy = e * (1.0 / D)
        dyg = dy * gv
        c = jnp.mean(dyg * xv, axis=-1, keepdims=True)
        dx_ref[...] = r * dyg - xv * (r * r * r) * c

        @pl.when(pl.program_id(0) == 0)
        def _():
            loss_ref[...] = jnp.zeros_like(loss_ref)
            dg_ref[...] = jnp.zeros_like(dg_ref)

        loss_ref[...] += jnp.sum(e * e, axis=0, keepdims=True) * (0.5 / D)
        dg_ref[...] += jnp.sum(dy * xv * r, axis=0, keepdims=True)

    row = pl.BlockSpec((tm, D), lambda i: (i, 0))
    vec = pl.BlockSpec((1, D), lambda i: (0, 0))
    return pl.pallas_call(
        body, name=name, grid=(T // tm,), in_specs=[row, vec, row], out_specs=[vec, row, vec],
        out_shape=[jax.ShapeDtypeStruct((1, D), F32), jax.ShapeDtypeStruct((T, D), F32), jax.ShapeDtypeStruct((1, D), F32)],
        compiler_params=_params("arbitrary"),
    )(x, g.reshape(1, D), target)


def _ssm_disc_math(lre, lim, logdt, br, bi):
    dt = jnp.exp(logdt)
    mag = jnp.exp(lre * dt)
    ar = mag * jnp.cos(lim * dt)
    ai = mag * jnp.sin(lim * dt)
    den = lre * lre + lim * lim
    nr = ar - 1.0
    fr = (nr * lre + ai * lim) / den
    fi = (ai * lre - nr * lim) / den
    return ar, ai, fr[None] * br - fi[None] * bi, fr[None] * bi + fi[None] * br


def _ssm_disc(lre, lim, logdt, br, bi, name):
    def body(lre_ref, lim_ref, dt_ref, br_ref, bi_ref, ar_ref, ai_ref, bbr_ref, bbi_ref):
        ar, ai, bbr, bbi = _ssm_disc_math(lre_ref[...], lim_ref[...], dt_ref[...], br_ref[...], bi_ref[...])
        ar_ref[...] = ar
        ai_ref[...] = ai
        bbr_ref[...] = bbr
        bbi_ref[...] = bbi

    sd = jax.ShapeDtypeStruct
    return pl.pallas_call(
        body, name=name, out_shape=[sd(lre.shape, F32), sd(lre.shape, F32), sd(br.shape, F32), sd(br.shape, F32)],
    )(lre, lim, logdt, br, bi)


def _ssm_disc_bwd(lre, lim, logdt, br, bi, dar, dai, dbbr, dbbi, name):
    def body(lre_ref, lim_ref, dt_ref, br_ref, bi_ref, dar_ref, dai_ref, dbbr_ref, dbbi_ref,
             glre_ref, glim_ref, gdt_ref, gbr_ref, gbi_ref):
        _, vjp = jax.vjp(_ssm_disc_math, lre_ref[...], lim_ref[...], dt_ref[...], br_ref[...], bi_ref[...])
        glre, glim, gdt, gbr, gbi = vjp((dar_ref[...], dai_ref[...], dbbr_ref[...], dbbi_ref[...]))
        glre_ref[...] = glre
        glim_ref[...] = glim
        gdt_ref[...] = gdt
        gbr_ref[...] = gbr
        gbi_ref[...] = gbi

    sd = jax.ShapeDtypeStruct
    return pl.pallas_call(
        body, name=name,
        out_shape=[sd(lre.shape, F32), sd(lre.shape, F32), sd(logdt.shape, F32), sd(br.shape, F32), sd(br.shape, F32)],
    )(lre, lim, logdt, br, bi, dar, dai, dbbr, dbbi)


SSM_STEPS_FWD = 128
SSM_STEPS_BWD = 256


def _ssm_tiles(ref, v, off, steps, n):
    return [ref[v, pl.ds(off + j, steps, stride=SUBLANES), :] for j in range(n)]


def _ssm_fwd(uz, bmat, cmat, art, ait, dvec, name, steps=SSM_STEPS_FWD):
    T = uz.shape[0]
    nblk, cb, width = bmat.shape
    C = nblk * cb
    half = width // 2
    nt = half // LANES
    npair = nblk // 2
    kc = min(steps, T)
    nchunk = T // kc

    def body(u_ref, b_ref, c_ref, ar_ref, ai_ref, d_ref, y_ref, xr_ref, xi_ref, sr_ref, si_ref):
        @pl.when(pl.program_id(0) == 0)
        def _():
            sr_ref[...] = jnp.zeros_like(sr_ref)
            si_ref[...] = jnp.zeros_like(si_ref)

        uv = u_ref[...]
        for b in range(nblk):
            bu = _dot(uv[:, b * cb:(b + 1) * cb].astype(BF16), b_ref[b], NN)
            v, off = b // 2, nt * (b % 2)
            for j in range(nt):
                xr_ref[v, pl.ds(off + j, kc, stride=SUBLANES), :] = bu[:, j * LANES:(j + 1) * LANES]
                xi_ref[v, pl.ds(off + j, kc, stride=SUBLANES), :] = bu[:, half + j * LANES:half + (j + 1) * LANES]
        ars = [ar_ref[v] for v in range(npair)]
        ais = [ai_ref[v] for v in range(npair)]

        def step(k, carry):
            row = pl.ds(k * SUBLANES, SUBLANES)
            out = []
            for v in range(npair):
                xr, xi = carry[2 * v], carry[2 * v + 1]
                nr = ars[v] * xr - ais[v] * xi + xr_ref[v, row, :]
                ni = ars[v] * xi + ais[v] * xr + xi_ref[v, row, :]
                xr_ref[v, row, :] = nr
                xi_ref[v, row, :] = ni
                out += [nr, ni]
            return tuple(out)

        fin = tuple(ref[v] for v in range(npair) for ref in (sr_ref, si_ref))
        for k in range(kc):
            fin = step(k, fin)
        for v in range(npair):
            sr_ref[v] = fin[2 * v]
            si_ref[v] = fin[2 * v + 1]
        for b in range(nblk):
            v, off = b // 2, nt * (b % 2)
            xb = jnp.concatenate(_ssm_tiles(xr_ref, v, off, kc, nt) + _ssm_tiles(xi_ref, v, off, kc, nt), axis=1)
            cols = slice(b * cb, (b + 1) * cb)
            y_ref[:, cols] = _dot(xb.astype(BF16), c_ref[b], NN) + d_ref[:, cols] * uv[:, cols]

    whole = lambda a: pl.BlockSpec(a.shape, lambda c: (0,) * a.ndim)
    st = pl.BlockSpec((npair, kc * SUBLANES, LANES), lambda c: (0, c, 0))
    sd = jax.ShapeDtypeStruct
    return pl.pallas_call(
        body, name=name, grid=(nchunk,),
        in_specs=[pl.BlockSpec((kc, C), lambda c: (c, 0)), whole(bmat), whole(cmat), whole(art), whole(ait), whole(dvec)],
        out_specs=[pl.BlockSpec((kc, C), lambda c: (c, 0)), st, st],
        out_shape=[sd((T, C), F32), sd((npair, T * SUBLANES, LANES), F32), sd((npair, T * SUBLANES, LANES), F32)],
        scratch_shapes=[pltpu.VMEM((npair, SUBLANES, LANES), F32), pltpu.VMEM((npair, SUBLANES, LANES), F32)],
        compiler_params=_params("arbitrary"),
    )(uz, bmat, cmat, art, ait, dvec)


def _ssm_bwd(dy, uz, xr, xi, bmat, cmat, art, ait, dvec, dproj, name):
    T = uz.shape[0]
    nblk, cb, width = bmat.shape
    C = nblk * cb
    half = width // 2
    nt = half // LANES
    npair = nblk // 2
    kc = min(SSM_STEPS_BWD, T)
    nchunk = T // kc

    def body(dy_ref, u_ref, xr_ref, xi_ref, xpr_ref, xpi_ref, b_ref, c_ref, ar_ref, ai_ref, d_ref, _,
             du_ref, db_ref, dc_ref, dar_ref, dai_ref, dd_ref, gr_ref, gi_ref, sr_ref, si_ref):
        c = pl.program_id(0)

        @pl.when(c == 0)
        def _():
            for ref in (sr_ref, si_ref, db_ref, dc_ref, dar_ref, dai_ref, dd_ref):
                ref[...] = jnp.zeros_like(ref)

        dyv = dy_ref[...]
        uv = u_ref[...]
        for b in range(nblk):
            dx = _dot(dyv[:, b * cb:(b + 1) * cb].astype(BF16), c_ref[b], NT)
            v, off = b // 2, nt * (b % 2)
            for j in range(nt):
                gr_ref[v, pl.ds(off + j, kc, stride=SUBLANES), :] = dx[:, j * LANES:(j + 1) * LANES]
                gi_ref[v, pl.ds(off + j, kc, stride=SUBLANES), :] = dx[:, half + j * LANES:half + (j + 1) * LANES]
        ars = [ar_ref[v] for v in range(npair)]
        ais = [ai_ref[v] for v in range(npair)]

        def pair_update(v, gr, gi, row):
            nr = ars[v] * gr + ais[v] * gi + gr_ref[v, row, :]
            ni = ars[v] * gi - ais[v] * gr + gi_ref[v, row, :]
            gr_ref[v, row, :] = nr
            gi_ref[v, row, :] = ni
            return nr, ni

        def step(i, carry):
            k = kc - 1 - i
            row = pl.ds(k * SUBLANES, SUBLANES)
            prow = pl.ds((k - 1) * SUBLANES, SUBLANES)
            out = []
            for v in range(npair):
                gr, gi, sr, si = carry[4 * v:4 * v + 4]
                nr, ni = pair_update(v, gr, gi, row)
                pr, pi = xr_ref[v, prow, :], xi_ref[v, prow, :]
                out += [nr, ni, sr + pr * nr + pi * ni, si + pr * ni - pi * nr]
            return tuple(out)

        mid = tuple(ref[v] for v in range(npair) for ref in (sr_ref, si_ref, dar_ref, dai_ref))
        for i in range(kc - 1):
            mid = step(i, mid)
        live = (c < nchunk - 1).astype(F32)
        row0 = pl.ds(0, SUBLANES)
        for v in range(npair):
            gr, gi, sr, si = mid[4 * v:4 * v + 4]
            nr, ni = pair_update(v, gr, gi, row0)
            pr, pi = xpr_ref[v] * live, xpi_ref[v] * live
            sr_ref[v] = nr
            si_ref[v] = ni
            dar_ref[v] = sr + pr * nr + pi * ni
            dai_ref[v] = si + pr * ni - pi * nr
        for b in range(nblk):
            v, off = b // 2, nt * (b % 2)
            cols = slice(b * cb, (b + 1) * cb)
            gb = jnp.concatenate(_ssm_tiles(gr_ref, v, off, kc, nt) + _ssm_tiles(gi_ref, v, off, kc, nt), axis=1).astype(BF16)
            xb = jnp.concatenate(_ssm_tiles(xr_ref, v, off, kc, nt) + _ssm_tiles(xi_ref, v, off, kc, nt), axis=1).astype(BF16)
            du_ref[:, cols] = (_dot(gb, b_ref[b], NT) + dyv[:, cols] * d_ref[:, cols]).astype(BF16)
            db_ref[b] += _dot(uv[:, cols].astype(BF16), gb, TN)
            dc_ref[b] += _dot(dyv[:, cols].astype(BF16), xb, TN)
        dd_ref[...] += jnp.sum(dyv * uv, axis=0, keepdims=True)

    whole = lambda a: pl.BlockSpec(a.shape, lambda c: (0,) * a.ndim)
    rev = lambda c: (nchunk - 1 - c, 0)
    st = pl.BlockSpec((npair, kc * SUBLANES, LANES), lambda c: (0, nchunk - 1 - c, 0))
    stp = pl.BlockSpec((npair, SUBLANES, LANES), lambda c: (0, jnp.maximum((nchunk - 1 - c) * kc - 1, 0), 0))
    acc = lambda shape: pl.BlockSpec(shape, lambda c: (0,) * len(shape))
    sd = jax.ShapeDtypeStruct
    pair_shape = (npair, SUBLANES, LANES)
    return pl.pallas_call(
        body, name=name, grid=(nchunk,),
        in_specs=[pl.BlockSpec((kc, C), rev), pl.BlockSpec((kc, C), rev), st, st, stp, stp, whole(bmat), whole(cmat),
                  whole(art), whole(ait), whole(dvec), pl.BlockSpec(memory_space=pl.ANY)],
        out_specs=[pl.BlockSpec((kc, C), rev), acc(bmat.shape), acc(bmat.shape), acc(pair_shape), acc(pair_shape), acc((1, C))],
        out_shape=[sd(dproj.shape, BF16), sd(bmat.shape, F32), sd(bmat.shape, F32), sd(pair_shape, F32), sd(pair_shape, F32),
                   sd((1, C), F32)],
        scratch_shapes=[pltpu.VMEM((npair, kc * SUBLANES, LANES), F32), pltpu.VMEM((npair, kc * SUBLANES, LANES), F32),
                        pltpu.VMEM(pair_shape, F32), pltpu.VMEM(pair_shape, F32)],
        input_output_aliases={11: 0}, compiler_params=_params("arbitrary"),
    )(dy, uz, xr, xi, xr, xi, bmat, cmat, art, ait, dvec, dproj)


def _ssm_post(y, z, w_glu, b_glu, name):
    T, C = y.shape
    tm = _pick(T, 512, SUBLANES)

    def body(y_ref, z_ref, w_ref, b_ref, o_ref, a_ref):
        a, _ = _gelu_and_grad(y_ref[...])
        ab = a.astype(BF16)
        sg = _sigmoid(_dot(ab, w_ref[...], NN) + b_ref[...])
        sz, _ = _silu_and_grad(z_ref[...].astype(F32))
        o_ref[...] = (a * sg * sz).astype(BF16)
        a_ref[...] = ab

    row = pl.BlockSpec((tm, C), lambda i: (i, 0))
    return pl.pallas_call(
        body, name=name, grid=(T // tm,),
        in_specs=[row, row, pl.BlockSpec((C, C), lambda i: (0, 0)), pl.BlockSpec((1, C), lambda i: (0, 0))],
        out_specs=[row, row], out_shape=[jax.ShapeDtypeStruct((T, C), BF16)] * 2, compiler_params=_params("parallel"),
    )(y, z, w_glu, b_glu.reshape(1, C))


def _ssm_post_bwd(do, y, z, w_glu, b_glu, dproj, col, name):
    T, C = y.shape
    tm = _pick(T, 512, SUBLANES)

    def body(do_ref, y_ref, z_ref, w_ref, b_ref, _, dy_ref, dz_ref, ds_ref, db_ref):
        dov = do_ref[...]
        a, da_dy = _gelu_and_grad(y_ref[...])
        sg = _sigmoid(_dot(a.astype(BF16), w_ref[...], NN) + b_ref[...])
        sz, dsz = _silu_and_grad(z_ref[...].astype(F32))
        yg = a * sg
        dz_ref[...] = (dov * yg * dsz).astype(BF16)
        dyg = dov * sz
        ds = dyg * a * sg * (1.0 - sg)
        dsb = ds.astype(BF16)
        ds_ref[...] = dsb
        da = dyg * sg + _dot(dsb, w_ref[...], NT)
        dy_ref[...] = da * da_dy

        @pl.when(pl.program_id(0) == 0)
        def _():
            db_ref[...] = jnp.zeros_like(db_ref)

        db_ref[...] += jnp.sum(ds, axis=0, keepdims=True)

    row = pl.BlockSpec((tm, C), lambda i: (i, 0))
    vec = pl.BlockSpec((1, C), lambda i: (0, 0))
    sd = jax.ShapeDtypeStruct
    return pl.pallas_call(
        body, name=name, grid=(T // tm,),
        in_specs=[row, row, row, pl.BlockSpec((C, C), lambda i: (0, 0)), vec, pl.BlockSpec(memory_space=pl.ANY)],
        out_specs=[row, pl.BlockSpec((tm, C), lambda i: (i, col // C)), row, vec],
        out_shape=[sd((T, C), F32), sd(dproj.shape, BF16), sd((T, C), BF16), sd((1, C), F32)],
        input_output_aliases={5: 1}, compiler_params=_params("arbitrary"),
    )(do, y, z, w_glu, b_glu.reshape(1, C), dproj)


def _rel_bucket(dist):
    n = jnp.maximum(dist, 0)
    max_exact = NUM_BUCKETS // 2
    n_f = jnp.maximum(n, 1).astype(F32)
    large = max_exact + (jnp.log(n_f / max_exact) / math.log(REL_MAX_DISTANCE / max_exact)
                         * (NUM_BUCKETS - max_exact)).astype(jnp.int32)
    large = jnp.minimum(large, NUM_BUCKETS - 1)
    return jnp.where(n < max_exact, n, large)


def _band_tables():
    qi = jnp.arange(ATTN_BLOCK)[:, None]
    kj = jnp.arange(2 * ATTN_BLOCK)[None, :]
    delta = ATTN_BLOCK + qi - kj
    buckets, bands = [], []
    for window, dilation in ATTN_CONFIGS:
        bands.append((delta >= 0) & (delta <= window // dilation))
        buckets.append(_rel_bucket(jnp.maximum(delta, 0) * dilation))
    return jnp.stack(buckets), jnp.stack(bands)


ATTN_UNITS = 4


def _attn_tile(T, r):
    nq = max(1, ATTN_UNITS // r)
    rows = ATTN_BLOCK * r * nq
    return nq, rows, T // rows


def _attn_units(r, nq, chunk):
    if r >= ATTN_UNITS:
        return [(chunk * ATTN_UNITS + i, None) for i in range(ATTN_UNITS)]
    units = []
    for j in range(nq):
        for s in range(r):
            units.append((ATTN_BLOCK * j * r + s, ATTN_BLOCK * (j - 1) * r + s if j else None))
    return units


def _rows(start, r):
    return pl.ds(start, ATTN_BLOCK, stride=r) if r > 1 else pl.ds(start, ATTN_BLOCK)


def _attn_group_fwd(qkv, biasm, g, name):
    T = qkv.shape[0]
    r = ATTN_CONFIGS[g][1]
    B, hd = ATTN_BLOCK, ATTN_HEAD_DIM
    nq, rows, ntiles = _attn_tile(T, r)
    nchunks = max(1, r // ATTN_UNITS)
    last_prev = B * (nq - 1) * r
    scale = hd ** -0.5
    tiles_per_tensor = 3 * HEADS_PER_GROUP * hd // LANES

    def body(q_ref, kc_ref, kp_ref, vc_ref, vp_ref, bias_ref, o_ref, lse_ref, s_ref, p_ref):
        n = pl.program_id(1)
        lane = lax.broadcasted_iota(jnp.int32, (1, LANES), 1)
        col = lax.broadcasted_iota(jnp.int32, (1, 2 * B), 1)
        masks = [lane < hd, lane >= hd]
        first_pen = jnp.where((col < B) & (n == 0), NEG_INF, 0.0)

        def chunk_body(chunk):
            units = _attn_units(r, nq, chunk)

            def keys(cur_ref, prev_ref, cs, ps):
                prev = prev_ref[_rows(last_prev + (cs if r >= ATTN_UNITS else cs % r), r), :] if ps is None else cur_ref[_rows(ps, r), :]
                return jnp.concatenate([prev, cur_ref[_rows(cs, r), :]], axis=0).astype(BF16)

            for u, (cs, ps) in enumerate(units):
                qv = q_ref[_rows(cs, r), :]
                kw = keys(kc_ref, kp_ref, cs, ps)
                for hh in range(2):
                    s_ref[2 * u + hh] = _dot(jnp.where(masks[hh], qv, 0.0).astype(BF16), kw, NT)
            for u, (cs, ps) in enumerate(units):
                lses = []
                for hh in range(2):
                    s = s_ref[2 * u + hh] * scale + bias_ref[hh]
                    if ps is None:
                        s = s + first_pen
                    m = jnp.max(s, axis=-1, keepdims=True)
                    p = jnp.exp(s - m)
                    l = jnp.sum(p, axis=-1, keepdims=True)
                    p_ref[2 * u + hh] = (p / l).astype(BF16)
                    lses.append(m + jnp.log(l))
                lse_ref[_rows(cs, r), :] = jnp.where(masks[0], lses[0], lses[1])
            for u, (cs, ps) in enumerate(units):
                vw = keys(vc_ref, vp_ref, cs, ps)
                o_ref[_rows(cs, r), :] = (_dot(p_ref[2 * u], jnp.where(masks[0], vw, 0), NN)
                                          + _dot(p_ref[2 * u + 1], jnp.where(masks[1], vw, 0), NN))

        if nchunks == 1:
            chunk_body(0)
        else:
            pl.loop(0, nchunks)(chunk_body)

    def cur(t):
        return pl.BlockSpec((rows, LANES), lambda hf, n: (n, t * tiles_per_tensor + 2 * g + hf))

    def prev(t):
        return pl.BlockSpec((rows, LANES), lambda hf, n: (jnp.maximum(n - 1, 0), t * tiles_per_tensor + 2 * g + hf))

    out = pl.BlockSpec((rows, LANES), lambda hf, n: (n, hf))
    sd = jax.ShapeDtypeStruct((T, 2 * LANES), F32)
    return pl.pallas_call(
        body, name=name, grid=(2, ntiles),
        in_specs=[cur(0), cur(1), prev(1), cur(2), prev(2), pl.BlockSpec((None, 2, B, 2 * B), lambda hf, n: (g, hf, 0, 0))],
        out_specs=[out, out], out_shape=[sd, sd],
        scratch_shapes=[pltpu.VMEM((2 * ATTN_UNITS, B, 2 * B), F32), pltpu.VMEM((2 * ATTN_UNITS, B, 2 * B), BF16)],
        compiler_params=_params("parallel", "parallel"),
    )(qkv, qkv, qkv, qkv, qkv, biasm)


def _attn_group_bwd(qkv, do, dvec, lse, biasm, g, dproj, dk_col, name):
    T = qkv.shape[0]
    r = ATTN_CONFIGS[g][1]
    B, hd = ATTN_BLOCK, ATTN_HEAD_DIM
    nq, rows, ntiles = _attn_tile(T, r)
    nchunks = max(1, r // ATTN_UNITS)
    last_prev = B * (nq - 1) * r
    scale = hd ** -0.5
    tiles_per_tensor = 3 * HEADS_PER_GROUP * hd // LANES

    def body(q_ref, kc_ref, kp_ref, vc_ref, vp_ref, do_ref, dv_ref, lse_ref, bias_ref, _,
             dq_ref, dk_ref, dvo_ref, dbias_ref, ck_ref, cv_ref, ak_ref, av_ref, s_ref, dp_ref, p_ref, ds_ref):
        n = pl.program_id(1)
        lane = lax.broadcasted_iota(jnp.int32, (1, LANES), 1)
        col = lax.broadcasted_iota(jnp.int32, (1, 2 * B), 1)
        masks = [lane < hd, lane >= hd]
        first_pen = jnp.where((col < B) & (n == 0), NEG_INF, 0.0)

        @pl.when(n == 0)
        def _():
            dbias_ref[...] = jnp.zeros_like(dbias_ref)
            ck_ref[...] = jnp.zeros_like(ck_ref)
            cv_ref[...] = jnp.zeros_like(cv_ref)

        def chunk_body(chunk):
            units = _attn_units(r, nq, chunk)

            def prev_rows(cs):
                return _rows(last_prev + (cs if r >= ATTN_UNITS else cs % r), r)

            def keys(cur_ref, prev_ref, cs, ps):
                prev = prev_ref[prev_rows(cs), :] if ps is None else cur_ref[_rows(ps, r), :]
                return jnp.concatenate([prev, cur_ref[_rows(cs, r), :]], axis=0).astype(BF16)

            for u, (cs, ps) in enumerate(units):
                qv = q_ref[_rows(cs, r), :]
                dov = do_ref[_rows(cs, r), :]
                kw = keys(kc_ref, kp_ref, cs, ps)
                vw = keys(vc_ref, vp_ref, cs, ps)
                for hh in range(2):
                    s_ref[2 * u + hh] = _dot(jnp.where(masks[hh], qv, 0.0).astype(BF16), kw, NT)
                    dp_ref[2 * u + hh] = _dot(jnp.where(masks[hh], dov, 0.0).astype(BF16), vw, NT)
            for u, (cs, ps) in enumerate(units):
                lse_t = lse_ref[_rows(cs, r), :]
                dv_t = dv_ref[_rows(cs, r), :]
                for hh in range(2):
                    lo = hh * hd
                    s = s_ref[2 * u + hh] * scale + bias_ref[hh]
                    if ps is None:
                        s = s + first_pen
                    p = jnp.exp(s - lse_t[:, lo:lo + 1])
                    ds = p * (dp_ref[2 * u + hh] + dv_t[:, lo:lo + 1])
                    dbias_ref[hh] += ds
                    p_ref[2 * u + hh] = p.astype(BF16)
                    ds_ref[2 * u + hh] = ds.astype(BF16)
            for u, (cs, ps) in enumerate(units):
                qv = q_ref[_rows(cs, r), :]
                dov = do_ref[_rows(cs, r), :]
                kw = keys(kc_ref, kp_ref, cs, ps)
                dq, dkw, dvw = 0.0, 0.0, 0.0
                for hh in range(2):
                    dsb = ds_ref[2 * u + hh]
                    dq = dq + _dot(dsb, jnp.where(masks[hh], kw, 0), NN)
                    dkw = dkw + _dot(dsb, jnp.where(masks[hh], qv, 0.0).astype(BF16), TN)
                    dvw = dvw + _dot(p_ref[2 * u + hh], jnp.where(masks[hh], dov, 0.0).astype(BF16), TN)
                dq_ref[_rows(cs, r), :] = dq * scale
                ak_ref[_rows(cs, r), :] = dkw[B:] * scale
                av_ref[_rows(cs, r), :] = dvw[B:]
                if ps is None:
                    ck_ref[prev_rows(cs), :] += dkw[:B] * scale
                    cv_ref[prev_rows(cs), :] += dvw[:B]
                else:
                    ak_ref[_rows(ps, r), :] += dkw[:B] * scale
                    av_ref[_rows(ps, r), :] += dvw[:B]

        @pl.when(n < ntiles)
        def _():
            for chunk in range(nchunks):
                chunk_body(chunk)

        dk_ref[...] = ck_ref[...].astype(BF16)
        dvo_ref[...] = cv_ref[...].astype(BF16)
        ck_ref[...] = ak_ref[...]
        cv_ref[...] = av_ref[...]

    last = ntiles - 1

    def cur(t):
        return pl.BlockSpec((rows, LANES), lambda hf, n: (jnp.minimum(n, last), t * tiles_per_tensor + 2 * g + hf))

    def prev(t):
        return pl.BlockSpec((rows, LANES), lambda hf, n: (jnp.clip(n - 1, 0, last), t * tiles_per_tensor + 2 * g + hf))

    nat = pl.BlockSpec((rows, LANES), lambda hf, n: (jnp.minimum(n, last), hf))
    nat_prev = pl.BlockSpec((rows, LANES), lambda hf, n: (jnp.clip(n - 1, 0, last), hf))
    tab = pl.BlockSpec((None, 2, B, 2 * B), lambda hf, n: (g, hf, 0, 0))
    dtab = pl.BlockSpec((2, B, 2 * B), lambda hf, n: (hf, 0, 0))
    sd = jax.ShapeDtypeStruct
    vm = pltpu.VMEM
    dk_tile = dk_col // LANES + 2 * g
    dk_spec = pl.BlockSpec((rows, LANES), lambda hf, n: (jnp.clip(n - 1, 0, last), dk_tile + hf))
    return pl.pallas_call(
        body, name=name, grid=(2, ntiles + 1),
        in_specs=[cur(0), cur(1), prev(1), cur(2), prev(2), nat, nat, nat, tab, pl.BlockSpec(memory_space=pl.ANY)],
        out_specs=[nat, dk_spec, nat_prev, dtab],
        out_shape=[sd((T, 2 * LANES), F32), sd(dproj.shape, BF16), sd((T, 2 * LANES), BF16),
                   sd((HEADS_PER_GROUP, B, 2 * B), F32)],
        scratch_shapes=[vm((rows, LANES), F32), vm((rows, LANES), F32), vm((rows, LANES), F32), vm((rows, LANES), F32),
                        vm((2 * ATTN_UNITS, B, 2 * B), F32), vm((2 * ATTN_UNITS, B, 2 * B), F32),
                        vm((2 * ATTN_UNITS, B, 2 * B), BF16), vm((2 * ATTN_UNITS, B, 2 * B), BF16)],
        input_output_aliases={9: 1}, compiler_params=_params("parallel", "arbitrary"),
    )(qkv, qkv, qkv, qkv, qkv, do, dvec, lse, biasm, dproj)


def _attn_mix(os, lses, z, name):
    T, gw = os[0].shape
    C = z.shape[1]
    tm = _pick(T, 512, SUBLANES)

    def body(o0_ref, o1_ref, o2_ref, l0_ref, l1_ref, l2_ref, z_ref, out_ref):
        ls = [l0_ref[...], l1_ref[...], l2_ref[...]]
        mx = jnp.maximum(jnp.maximum(ls[0], ls[1]), ls[2])
        es = [jnp.exp(l - mx) for l in ls]
        den = es[0] + es[1] + es[2]
        for i, o_ref in enumerate((o0_ref, o1_ref, o2_ref)):
            sz, _ = _silu_and_grad(z_ref[:, i * gw:(i + 1) * gw].astype(F32))
            out_ref[:, i * gw:(i + 1) * gw] = (o_ref[...] * (es[i] / den) * sz).astype(BF16)

    row = pl.BlockSpec((tm, C), lambda i: (i, 0))
    grp = pl.BlockSpec((tm, gw), lambda i: (i, 0))
    return pl.pallas_call(
        body, name=name, grid=(T // tm,), in_specs=[grp] * 6 + [row], out_specs=row,
        out_shape=jax.ShapeDtypeStruct((T, C), BF16), compiler_params=_params("parallel"),
    )(*os, *lses, z)


def _attn_mix_bwd(dout, os, lses, z, dproj, col, name):
    T, gw = os[0].shape
    C = z.shape[1]
    tm = _pick(T, 512, SUBLANES)
    head_of = np.arange(gw) // ATTN_HEAD_DIM
    ones = jnp.asarray(head_of[:, None] == head_of[None, :], BF16)

    def body(dout_ref, o0_ref, o1_ref, o2_ref, l0_ref, l1_ref, l2_ref, z_ref, ones_ref, _,
             dz_ref, do0_ref, do1_ref, do2_ref, dv0_ref, dv1_ref, dv2_ref):
        ls = [l0_ref[...], l1_ref[...], l2_ref[...]]
        mx = jnp.maximum(jnp.maximum(ls[0], ls[1]), ls[2])
        es = [jnp.exp(l - mx) for l in ls]
        den = es[0] + es[1] + es[2]
        alphas, ebar = [], 0.0
        for i, (o_ref, do_ref) in enumerate(((o0_ref, do0_ref), (o1_ref, do1_ref), (o2_ref, do2_ref))):
            sl = slice(i * gw, (i + 1) * gw)
            alpha = es[i] / den
            ov = o_ref[...]
            dv = dout_ref[:, sl]
            sz, dsz = _silu_and_grad(z_ref[:, sl].astype(F32))
            dz_ref[:, sl] = (dv * ov * alpha * dsz).astype(BF16)
            da = dv * sz
            do_ref[...] = da * alpha
            t = da * ov
            t1 = t.astype(BF16)
            r1 = t - t1.astype(F32)
            t2 = r1.astype(BF16)
            t3 = (r1 - t2.astype(F32)).astype(BF16)
            e = _dot(t1, ones_ref[...], NN) + _dot(t2, ones_ref[...], NN) + _dot(t3, ones_ref[...], NN)
            ebar = ebar + alpha * e
            alphas.append(alpha)
        for alpha, dv_ref in zip(alphas, (dv0_ref, dv1_ref, dv2_ref)):
            dv_ref[...] = -alpha * ebar

    row = pl.BlockSpec((tm, C), lambda i: (i, 0))
    grp = pl.BlockSpec((tm, gw), lambda i: (i, 0))
    sd = jax.ShapeDtypeStruct
    res = pl.pallas_call(
        body, name=name, grid=(T // tm,),
        in_specs=[row] + [grp] * 6 + [row, pl.BlockSpec((gw, gw), lambda i: (0, 0)), pl.BlockSpec(memory_space=pl.ANY)],
        out_specs=[pl.BlockSpec((tm, C), lambda i: (i, col // C))] + [grp] * 6,
        out_shape=[sd(dproj.shape, BF16)] + [sd((T, gw), F32)] * 6, input_output_aliases={9: 0},
        compiler_params=_params("parallel"),
    )(dout, *os, *lses, z, ones, dproj)
    return res[0], res[1:4], res[4:7]


def _mem_attn(qz, kv, name):
    T = qz.shape[0]
    dm = qz.shape[1] // 2
    M = kv.shape[0]
    hd = dm // MEM_HEADS
    scale = hd ** -0.5
    tm = _pick(T, 512, SUBLANES)

    def body(q_ref, z_ref, k_ref, v_ref, o_ref, s_ref, p_ref):
        heads = [slice(h * hd, (h + 1) * hd) for h in range(MEM_HEADS)]
        for h, sl in enumerate(heads):
            s_ref[h] = _dot(q_ref[:, sl].astype(BF16), k_ref[:, sl], NT)
        for h, sl in enumerate(heads):
            s = s_ref[h] * scale
            p = jnp.exp(s - jnp.max(s, axis=-1, keepdims=True))
            p_ref[h] = (p / jnp.sum(p, axis=-1, keepdims=True)).astype(BF16)
        for h, sl in enumerate(heads):
            sz, _ = _silu_and_grad(z_ref[:, sl].astype(F32))
            o_ref[:, sl] = (_dot(p_ref[h], v_ref[:, sl], NN) * sz).astype(BF16)

    return pl.pallas_call(
        body, name=name, grid=(T // tm,),
        in_specs=[pl.BlockSpec((tm, dm), lambda i: (i, 0)), pl.BlockSpec((tm, dm), lambda i: (i, 1)),
                  pl.BlockSpec((M, dm), lambda i: (0, 0)), pl.BlockSpec((M, dm), lambda i: (0, 1))],
        out_specs=pl.BlockSpec((tm, dm), lambda i: (i, 0)),
        out_shape=jax.ShapeDtypeStruct((T, dm), BF16),
        scratch_shapes=[pltpu.VMEM((MEM_HEADS, tm, M), F32), pltpu.VMEM((MEM_HEADS, tm, M), BF16)],
        compiler_params=_params("parallel"),
    )(qz, qz, kv, kv)


def _mem_attn_bwd(do, qz, kv, dproj, col, name):
    T = qz.shape[0]
    dm = qz.shape[1] // 2
    M = kv.shape[0]
    hd = dm // MEM_HEADS
    scale = hd ** -0.5
    tm = _pick(T, 512, SUBLANES)

    def body(do_ref, q_ref, z_ref, k_ref, v_ref, _, dq_ref, dz_ref, dk_ref, dv_ref, s_ref, dp_ref, p_ref, ds_ref, dob_ref):
        @pl.when(pl.program_id(0) == 0)
        def _():
            dk_ref[...] = jnp.zeros_like(dk_ref)
            dv_ref[...] = jnp.zeros_like(dv_ref)

        heads = [slice(h * hd, (h + 1) * hd) for h in range(MEM_HEADS)]
        for h, sl in enumerate(heads):
            sz, _ = _silu_and_grad(z_ref[:, sl].astype(F32))
            dob = (do_ref[:, sl] * sz).astype(BF16)
            dob_ref[:, sl] = dob
            s_ref[h] = _dot(q_ref[:, sl].astype(BF16), k_ref[:, sl], NT)
            dp_ref[h] = _dot(dob, v_ref[:, sl], NT)
        for h, sl in enumerate(heads):
            s = s_ref[h] * scale
            p = jnp.exp(s - jnp.max(s, axis=-1, keepdims=True))
            pn = p / jnp.sum(p, axis=-1, keepdims=True)
            dp = dp_ref[h]
            p_ref[h] = pn.astype(BF16)
            ds_ref[h] = (pn * (dp - jnp.sum(dp * pn, axis=-1, keepdims=True))).astype(BF16)
        for h, sl in enumerate(heads):
            _, dsz = _silu_and_grad(z_ref[:, sl].astype(F32))
            dz_ref[:, sl] = (do_ref[:, sl] * _dot(p_ref[h], v_ref[:, sl], NN) * dsz).astype(BF16)
            dq_ref[:, sl] = (_dot(ds_ref[h], k_ref[:, sl], NN) * scale).astype(BF16)
            dk_ref[:, sl] += _dot(ds_ref[h], q_ref[:, sl].astype(BF16), TN) * scale
            dv_ref[:, sl] += _dot(p_ref[h], dob_ref[:, sl], TN)

    rowq = pl.BlockSpec((tm, dm), lambda i: (i, 0))
    rowz = pl.BlockSpec((tm, dm), lambda i: (i, 1))
    kb = pl.BlockSpec((M, dm), lambda i: (0, 0))
    vb = pl.BlockSpec((M, dm), lambda i: (0, 1))
    sd = jax.ShapeDtypeStruct
    dq, dz, dk, dv = pl.pallas_call(
        body, name=name, grid=(T // tm,), in_specs=[rowq, rowq, rowz, kb, vb, pl.BlockSpec(memory_space=pl.ANY)],
        out_specs=[pl.BlockSpec((tm, dm), lambda i: (i, col // dm)), rowq, kb, kb],
        out_shape=[sd(dproj.shape, BF16), sd((T, dm), BF16), sd((M, dm), F32), sd((M, dm), F32)],
        scratch_shapes=[pltpu.VMEM((MEM_HEADS, tm, M), F32), pltpu.VMEM((MEM_HEADS, tm, M), F32),
                        pltpu.VMEM((MEM_HEADS, tm, M), BF16), pltpu.VMEM((MEM_HEADS, tm, M), BF16), pltpu.VMEM((tm, dm), BF16)],
        input_output_aliases={5: 0}, compiler_params=_params("arbitrary"),
    )(do, qz, qz, kv, kv, dproj)
    return dq, dz, dk, dv


def _merge(os, ws, L, logits, b_gate, name):
    T = os[0].shape[0]
    D = ws[0].shape[2]
    tm = _pick(T, 512, SUBLANES)

    def body(o0_ref, o1_ref, o2_ref, w0_ref, w1_ref, w2_ref, l_ref, b_ref, m_ref, p_ref):
        acc = 0.0
        for i, (o_ref, w_ref) in enumerate(((o0_ref, w0_ref), (o1_ref, w1_ref), (o2_ref, w2_ref))):
            sl = slice(i * D, (i + 1) * D)
            bp = _dot(o_ref[...], w_ref[...], NN)
            p_ref[i] = bp.astype(BF16)
            acc = acc + _sigmoid(l_ref[:, sl].astype(F32) + b_ref[:, sl]) * bp
        m_ref[...] = acc.astype(BF16)

    return pl.pallas_call(
        body, name=name, grid=(T // tm,),
        in_specs=[pl.BlockSpec((tm, o.shape[1]), lambda i: (i, 0)) for o in os]
        + [pl.BlockSpec((None,) + w.shape[1:], lambda i: (L, 0, 0)) for w in ws]
        + [pl.BlockSpec((tm, 3 * D), lambda i: (i, 0)), pl.BlockSpec((1, 3 * D), lambda i: (0, 0))],
        out_specs=[pl.BlockSpec((tm, D), lambda i: (i, 0)), pl.BlockSpec((3, tm, D), lambda i: (0, i, 0))],
        out_shape=[jax.ShapeDtypeStruct((T, D), BF16), jax.ShapeDtypeStruct((3, T, D), BF16)],
        compiler_params=_params("parallel"),
    )(*os, *ws, logits, b_gate.reshape(1, 3 * D))


def _merge_bwd(dmerged, bps, logits, b_gate, dproj_cols, dl_off, name):
    _, T, D = bps.shape
    tm = _pick(T, 2048, SUBLANES)
    cw = _pick(math.gcd(dl_off, D), 512, LANES)
    per = D // cw

    def body(dm_ref, p_ref, l_ref, b_ref, dl_ref, d_ref, db_ref):
        @pl.when(pl.program_id(1) == 0)
        def _():
            db_ref[...] = jnp.zeros_like(db_ref)

        dmv = dm_ref[...]
        gt = _sigmoid(l_ref[...].astype(F32) + b_ref[...])
        d_ref[...] = (dmv * gt).astype(BF16)
        dl = dmv * p_ref[...].astype(F32) * gt * (1.0 - gt)
        dl_ref[...] = dl.astype(BF16)
        db_ref[...] += jnp.sum(dl, axis=0, keepdims=True)

    stacked = pl.BlockSpec((None, tm, cw), lambda j, i: (j // per, i, j % per))
    sd = jax.ShapeDtypeStruct
    return pl.pallas_call(
        body, name=name, grid=(3 * per, T // tm),
        in_specs=[pl.BlockSpec((tm, cw), lambda j, i: (i, j % per)), stacked, pl.BlockSpec((tm, cw), lambda j, i: (i, j)),
                  pl.BlockSpec((1, cw), lambda j, i: (0, j))],
        out_specs=[pl.BlockSpec((tm, cw), lambda j, i: (i, dl_off // cw + j)), stacked, pl.BlockSpec((1, cw), lambda j, i: (0, j))],
        out_shape=[sd((T, dproj_cols), BF16), sd((3, T, D), BF16), sd((1, 3 * D), F32)],
        compiler_params=_params("parallel", "arbitrary"),
    )(dmerged, bps, logits, b_gate.reshape(1, 3 * D))


def _block_diag(w):
    nblk, ng, a, b = w.shape
    eye = jnp.eye(ng, dtype=w.dtype)
    return (w[:, :, :, None, :] * eye[None, :, None, :, None]).reshape(nblk, ng * a, ng * b)


def _block_diag_part(m, a, b):
    nblk = m.shape[0]
    ng = m.shape[1] // a
    m5 = m.reshape(nblk, ng, a, ng, b)
    eye = jnp.eye(ng, dtype=m.dtype)
    return jnp.sum(m5 * eye[None, :, None, :, None], axis=3)


def _ssm_matrices(p, L, tag):
    G, P = p["ssm_lambda_re"].shape[1:]
    Hg = SSM_GROUP
    gpb = SSM_BLOCK_CH // Hg
    nblk = G // gpb
    br = p["ssm_b_re"][L].transpose(2, 0, 1)
    bi = p["ssm_b_im"][L].transpose(2, 0, 1)
    disc_in = (p["ssm_lambda_re"][L], p["ssm_lambda_im"][L], p["ssm_log_dt"][L].reshape(G, 1), br, bi)
    ar, ai, bbr, bbi = _ssm_disc(*disc_in, name=f"ssm_disc_{tag}")
    amat = (ar.reshape(nblk // 2, SUBLANES, LANES), ai.reshape(nblk // 2, SUBLANES, LANES))
    bbr_g = bbr.transpose(1, 0, 2).reshape(nblk, gpb, Hg, P)
    bbi_g = bbi.transpose(1, 0, 2).reshape(nblk, gpb, Hg, P)
    bmat = jnp.concatenate([_block_diag(bbr_g), _block_diag(bbi_g)], axis=2).astype(BF16)
    cre = p["ssm_c_re"][L].reshape(nblk, gpb, Hg, P).transpose(0, 1, 3, 2)
    cim = p["ssm_c_im"][L].reshape(nblk, gpb, Hg, P).transpose(0, 1, 3, 2)
    cmat = jnp.concatenate([_block_diag(cre), -_block_diag(cim)], axis=1).astype(BF16)
    return disc_in, amat, bmat, cmat


def _layer_fwd(x, mem, p, wb, L, biasm):
    T, D = x.shape
    C = p["ssm_d"].shape[1]
    dm = wb["w_br_mem"].shape[1]
    tag = f"l{L}"
    s = {"x": x}
    h = _rmsnorm(x, p["norm_g"][L], f"norm_{tag}")
    offs = [int(o) for o in np.cumsum([0, C, C, 3 * 768, 768, 2 * dm, 3 * D])]
    names = ("uz", "z_ssm", "qkv", "z_attn", "qz_mem", "logits")
    dts = (F32, BF16, F32, BF16, BF16, BF16)
    for i, (nm, dt) in enumerate(zip(names, dts)):
        tiles = dict(tm=2048) if (L == 1 and dt == BF16) else {}
        s[nm] = _matmul(h, wb["w_in"], mode="nn", name=f"in_{nm}_{tag}", out_dtype=dt, b_lead=L, b_off=offs[i],
                        n_cols=offs[i + 1] - offs[i], **tiles)
    s["h"] = h

    disc_in, amat, bmat, cmat = _ssm_matrices(p, L, tag)
    dvec = p["ssm_d"][L].reshape(1, C)
    y, xr, xi = _ssm_fwd(s["uz"], bmat, cmat, *amat, dvec, f"ssm_scan_{tag}", steps=256 if L == 1 else SSM_STEPS_FWD)
    o_ssm, a_glu = _ssm_post(y, s["z_ssm"], wb["w_glu"][L], p["b_glu"][L], f"ssm_post_{tag}")
    s.update(disc_in=disc_in, amat=amat, bmat=bmat, cmat=cmat, xr=xr, xi=xi, y=y, a_glu=a_glu, o_ssm=o_ssm)

    groups = [_attn_group_fwd(s["qkv"], biasm, g, f"attn_g{g}_{tag}") for g in range(len(ATTN_CONFIGS))]
    os, lses = [o for o, _ in groups], [l for _, l in groups]
    o_attn = _attn_mix(os, lses, s["z_attn"], f"attn_mix_{tag}")
    s.update(os=os, lses=lses, o_attn=o_attn)

    mn = _rmsnorm(mem, p["mem_norm_g"][L], f"mem_norm_{tag}")
    kv = _matmul(mn, wb["w_mem_kv"], mode="nn", name=f"mem_kv_{tag}", out_dtype=BF16, b_lead=L)
    o_mem = _mem_attn(s["qz_mem"], kv, f"mem_attn_{tag}")
    s.update(mn=mn, kv=kv, o_mem=o_mem)

    merged, bps = _merge([o_ssm, o_attn, o_mem], [wb["w_br_ssm"], wb["w_br_attn"], wb["w_br_mem"]], L, s["logits"],
                         p["b_gate"][L], f"merge_{tag}")
    s.update(bps=bps, merged=merged)
    x_new = _matmul(merged, wb["w_out"], mode="nn", name=f"out_{tag}", add=x, b_lead=L)
    return x_new, s


def _layer_bwd(dx, mem, p, wb, L, s, biasm, gprev):
    T, D = dx.shape
    C = p["ssm_d"].shape[1]
    depth = p["norm_g"].shape[0]
    tag = f"l{L}"
    g = {}

    def wgrad(n, a, b, **tiles):
        g[n] = _matmul(a, b, mode="tn", name=f"d{n}_{tag}", out_dtype=BF16, stack=(L, depth, gprev.get(n)), **tiles)

    dmerged = _matmul(dx, wb["w_out"], mode="nt", name=f"d_merged_{tag}", b_lead=L)
    wgrad("w_out", s["merged"], dx)
    dm = s["qz_mem"].shape[1] // 2
    col = dict(zip(("u", "z_ssm", "q", "k", "v", "z_attn", "q_mem", "z_mem", "logits", "end"),
                   (int(o) for o in np.cumsum([0, C, C, 768, 768, 768, 768, dm, dm, 3 * D]))))
    dproj, dbps, g["b_gate"] = _merge_bwd(dmerged, s["bps"], s["logits"], p["b_gate"][L], col["end"], col["logits"],
                                          f"merge_bwd_{tag}")
    dos = []
    for i, (o, n) in enumerate(((s["o_ssm"], "w_br_ssm"), (s["o_attn"], "w_br_attn"), (s["o_mem"], "w_br_mem"))):
        dos.append(_matmul(dbps, wb[n], mode="nt", name=f"d_o_{n}_{tag}", a_lead=i, b_lead=L))
        g[n] = _matmul(o, dbps, mode="tn", name=f"d{n}_{tag}", out_dtype=BF16, b_lead=i, stack=(L, depth, gprev.get(n)))

    dy, dproj, ds_glu, g["b_glu"] = _ssm_post_bwd(dos[0], s["y"], s["z_ssm"], wb["w_glu"][L], p["b_glu"][L], dproj,
                                                  col["z_ssm"], f"ssm_post_bwd_{tag}")
    wgrad("w_glu", s["a_glu"], ds_glu)
    dvec = p["ssm_d"][L].reshape(1, C)
    dproj, dbm, dct, dar, dai, g["ssm_d"] = _ssm_bwd(dy, s["uz"], s["xr"], s["xi"], s["bmat"], s["cmat"], *s["amat"], dvec,
                                                     dproj, f"ssm_scan_bwd_{tag}")
    G, P = p["ssm_lambda_re"].shape[1:]
    Hg = SSM_GROUP
    half = dbm.shape[2] // 2
    dbbr = _block_diag_part(dbm[:, :, :half], Hg, P).reshape(G, Hg, P).transpose(1, 0, 2)
    dbbi = _block_diag_part(dbm[:, :, half:], Hg, P).reshape(G, Hg, P).transpose(1, 0, 2)
    g["ssm_c_re"] = _block_diag_part(dct[:, :, :half], Hg, P).reshape(G, Hg, P)
    g["ssm_c_im"] = -_block_diag_part(dct[:, :, half:], Hg, P).reshape(G, Hg, P)
    glre, glim, gdt, gbr, gbi = _ssm_disc_bwd(*s["disc_in"], dar.reshape(G, P), dai.reshape(G, P), dbbr, dbbi,
                                              name=f"ssm_disc_bwd_{tag}")
    g["ssm_lambda_re"], g["ssm_lambda_im"], g["ssm_log_dt"] = glre, glim, gdt.reshape(G)
    g["ssm_b_re"] = gbr.transpose(1, 2, 0)
    g["ssm_b_im"] = gbi.transpose(1, 2, 0)

    dproj, do_g, dvec_g = _attn_mix_bwd(dos[1], s["os"], s["lses"], s["z_attn"], dproj, col["z_attn"], f"attn_mix_bwd_{tag}")
    rest, dbias = [], []
    for gi in range(len(ATTN_CONFIGS)):
        dq_g, dproj, dv_g, db_g = _attn_group_bwd(s["qkv"], do_g[gi], dvec_g[gi], s["lses"][gi], biasm, gi, dproj, col["k"],
                                                  f"attn_bwd_g{gi}_{tag}")
        gw = dq_g.shape[1]
        rest += [(dq_g, col["q"] + gi * gw), (dv_g, col["v"] + gi * gw)]
        dbias.append(db_g)
    dbias = jnp.stack(dbias)

    dproj, dz_mem, dk_mem, dv_mem = _mem_attn_bwd(dos[2], s["qz_mem"], s["kv"], dproj, col["q_mem"], f"mem_attn_bwd_{tag}")
    rest.append((dz_mem, col["z_mem"]))
    for piece, at in rest:
        dproj = lax.dynamic_update_slice(dproj, piece.astype(BF16), (0, at))
    dkv = jnp.concatenate([dk_mem, dv_mem], axis=1)
    wgrad("w_mem_kv", s["mn"], dkv)
    dmn = _matmul(dkv, wb["w_mem_kv"], mode="nt", name=f"d_mn_{tag}", b_lead=L)
    _, g["mem_norm_g"] = _rmsnorm_bwd(mem, p["mem_norm_g"][L], dmn, None, f"mem_norm_bwd_{tag}")

    dh = _matmul(dproj, wb["w_in"], mode="nt", name=f"d_h_{tag}", b_lead=L, tm=512, tn=1024)
    wgrad("w_in", s["h"], dproj, tn=2304, tk=1024)
    dx_in, g["norm_g"] = _rmsnorm_bwd(s["x"], p["norm_g"][L], dh, dx, f"norm_bwd_{tag}")
    return dx_in, g, dbias


def _bucket_onehot(gi):
    buckets, bands = _band_tables()
    hit = (buckets[gi].reshape(1, -1) == jnp.arange(NUM_BUCKETS)[:, None]) & bands[gi].reshape(1, -1)
    return hit.astype(BF16)


def _bias_tables(rel_bias, name):
    _, bands = _band_tables()
    out = []
    for gi in range(len(ATTN_CONFIGS)):
        tab = rel_bias[:, gi * HEADS_PER_GROUP:(gi + 1) * HEADS_PER_GROUP].T
        flat = _matmul(tab, _bucket_onehot(gi), mode="nn", name=f"{name}_{gi}", split_a=3, tn=4096)
        out.append(jnp.where(bands[gi][None], flat.reshape(HEADS_PER_GROUP, ATTN_BLOCK, 2 * ATTN_BLOCK), NEG_INF))
    return jnp.stack(out)


def _rel_bias_grad(dbias_sum, name):
    cols = []
    for gi in range(len(ATTN_CONFIGS)):
        flat = dbias_sum[gi].reshape(HEADS_PER_GROUP, -1)
        cols.append(_matmul(flat, _bucket_onehot(gi), mode="nt", name=f"{name}_{gi}", split_a=2, tk=4096).T)
    return jnp.concatenate(cols, axis=1)


def _local_step(x, mem, target, p, wb):
    depth = p["norm_g"].shape[0]
    biasm = _bias_tables(p["rel_bias"], "bias_table")
    saved = []
    for L in range(depth):
        x, s = _layer_fwd(x, mem, p, wb, L, biasm)
        saved.append(s)
    loss_vec, dx, dgf = _loss_head(x, p["final_norm_g"], target, "loss_head")
    grads = {"final_norm_g": dgf.reshape(-1)}
    per_layer = [None] * depth
    dbias_sum = 0.0
    stacked = {}
    for L in reversed(range(depth)):
        dx, per_layer[L], dbias = _layer_bwd(dx, mem, p, wb, L, saved[L], biasm, stacked)
        stacked = {n: per_layer[L][n] for n, _ in BIG}
        dbias_sum = dbias_sum + dbias
    grads.update(stacked)
    for n in per_layer[0]:
        if n not in stacked:
            grads[n] = jnp.stack([per_layer[L][n].reshape(p[n].shape[1:]) for L in range(depth)])
    grads["rel_bias"] = _rel_bias_grad(dbias_sum, "d_rel_bias")
    return jnp.sum(loss_vec), dx, grads


def _chip_coords(j):
    return j // 2, j % 2


def _place_shard(shard, ax, chip, name):
    _, a, b = shard.shape
    ra = _pick(a, 256, 16)
    full = (2, a * N_CHIPS, b) if ax == 1 else (2, a, b * N_CHIPS)
    per = a // ra

    def body(j_ref, s_ref, o_ref):
        o_ref[...] = s_ref[...].astype(BF16)

    out_idx = (lambda l, i, j: (l, j[0] * per + i, 0)) if ax == 1 else (lambda l, i, j: (l, i, j[0]))
    return pl.pallas_call(
        body, name=name,
        grid_spec=pltpu.PrefetchScalarGridSpec(
            num_scalar_prefetch=1, grid=(2, per),
            in_specs=[pl.BlockSpec((None, ra, b), lambda l, i, j: (l, i, 0))],
            out_specs=pl.BlockSpec((None, ra, b), out_idx)),
        out_shape=jax.ShapeDtypeStruct(full, BF16), compiler_params=_params("parallel", "parallel"),
    )(chip, shard)


def _gather_shards(fulls, axes, name):
    n = len(fulls)
    widths = [a.shape[ax] // N_CHIPS for a, ax in zip(fulls, axes)]
    aligns = [LANES if ax == 2 else 16 for ax in axes]

    def body(*refs):
        outs = refs[n:2 * n]
        send_sems, recv_sems, fsend_sems, frecv_sems = refs[2 * n:]
        x, y, c = lax.axis_index("x"), lax.axis_index("y"), lax.axis_index("c")
        mine = 2 * x + y
        sibling = (x, y, 1 - c)

        def window(t, layer, j):
            start = pl.ds(pl.multiple_of(j * widths[t], aligns[t]), widths[t])
            return outs[t].at[(layer, start, slice(None)) if axes[t] == 1 else (layer, slice(None), start)]

        def over_ici(t, j, block):
            return pltpu.make_async_remote_copy(
                src_ref=window(t, c, mine), dst_ref=window(t, c, block), send_sem=send_sems.at[t, j],
                recv_sem=recv_sems.at[t, block], device_id=(*_chip_coords(j), c), device_id_type=MESH)

        def over_d2d(t, j, layer):
            return pltpu.make_async_remote_copy(
                src_ref=window(t, layer, j), dst_ref=window(t, layer, j), send_sem=fsend_sems.at[t, j],
                recv_sem=frecv_sems.at[t, j], device_id=sibling, device_id_type=MESH)

        for t in range(n):
            for j in range(N_CHIPS):
                @pl.when(j != mine)
                def _():
                    over_ici(t, j, mine).start()
        for t in range(n):
            for j in range(N_CHIPS):
                @pl.when(j != mine)
                def _():
                    over_ici(t, j, j).wait_recv()
                    over_d2d(t, j, c).start()
        for t in range(n):
            for j in range(N_CHIPS):
                @pl.when(j != mine)
                def _():
                    over_ici(t, j, mine).wait_send()
                    over_d2d(t, j, c).wait_send()
                    over_d2d(t, j, 1 - c).wait_recv()

    sem = pltpu.SemaphoreType.DMA
    return pl.pallas_call(
        body, name=name, in_specs=[HBM] * n, out_specs=[HBM] * n,
        out_shape=[jax.ShapeDtypeStruct(a.shape, a.dtype) for a in fulls],
        input_output_aliases={t: t for t in range(n)},
        scratch_shapes=[sem((n, N_CHIPS)), sem((n, N_CHIPS)), sem((n, N_CHIPS)), sem((n, N_CHIPS))],
    )(*fulls)


def _scatter_slices(arrays, axes, name):
    n = len(arrays)

    def piece(a, ax):
        if ax is None:
            return a.shape, None
        w = a.shape[ax] // N_CHIPS
        return a.shape[:ax] + (w,) + a.shape[ax + 1:], w

    shapes = [piece(a, ax) for a, ax in zip(arrays, axes)]

    def body(*refs):
        ins, outs = refs[:n], refs[n:2 * n]
        send_sems, recv_sems, loc_sems = refs[2 * n:]
        x, y, c = lax.axis_index("x"), lax.axis_index("y"), lax.axis_index("c")
        mine = 2 * x + y

        def src(t, j):
            ax, w = axes[t], shapes[t][1]
            if ax is None:
                return ins[t]
            idx = tuple(pl.ds(j * w, w) if d == ax else slice(None) for d in range(len(arrays[t].shape)))
            return ins[t].at[idx]

        for t in range(n):
            for j in range(N_CHIPS):
                @pl.when(j == mine)
                def _():
                    pltpu.make_async_copy(src(t, j), outs[t].at[j], loc_sems.at[t]).start()

                @pl.when(j != mine)
                def _():
                    pltpu.make_async_remote_copy(
                        src_ref=src(t, j), dst_ref=outs[t].at[mine], send_sem=send_sems.at[t, j], recv_sem=recv_sems.at[t, mine],
                        device_id=(*_chip_coords(j), c), device_id_type=MESH).start()
        for t in range(n):
            for j in range(N_CHIPS):
                @pl.when(j == mine)
                def _():
                    pltpu.make_async_copy(src(t, j), outs[t].at[j], loc_sems.at[t]).wait()

                @pl.when(j != mine)
                def _():
                    cp = pltpu.make_async_remote_copy(
                        src_ref=src(t, j), dst_ref=outs[t].at[j], send_sem=send_sems.at[t, j], recv_sem=recv_sems.at[t, j],
                        device_id=(*_chip_coords(j), c), device_id_type=MESH)
                    cp.wait_send()
                    cp.wait_recv()

    return pl.pallas_call(
        body, name=name, in_specs=[HBM] * n, out_specs=[HBM] * n,
        out_shape=[jax.ShapeDtypeStruct((N_CHIPS,) + sh, a.dtype) for a, (sh, _) in zip(arrays, shapes)],
        scratch_shapes=[pltpu.SemaphoreType.DMA((n, N_CHIPS)), pltpu.SemaphoreType.DMA((n, N_CHIPS)), pltpu.SemaphoreType.DMA((n,))],
    )(*arrays)


def _swap_layers(stacked, name):
    n = len(stacked)

    def body(*refs):
        ins, outs = refs[:n], refs[n:2 * n]
        send_sems, recv_sems = refs[2 * n:]
        c = lax.axis_index("c")
        peer = (lax.axis_index("x"), lax.axis_index("y"), 1 - c)
        cps = [pltpu.make_async_remote_copy(src_ref=ins[t].at[1 - c], dst_ref=outs[t], send_sem=send_sems.at[t],
                                            recv_sem=recv_sems.at[t], device_id=peer, device_id_type=MESH) for t in range(n)]
        for cp in cps:
            cp.start()
        for cp in cps:
            cp.wait_send()
            cp.wait_recv()

    return pl.pallas_call(
        body, name=name, in_specs=[HBM] * n, out_specs=[HBM] * n,
        out_shape=[jax.ShapeDtypeStruct(a.shape[1:], a.dtype) for a in stacked],
        scratch_shapes=[pltpu.SemaphoreType.DMA((n,)), pltpu.SemaphoreType.DMA((n,))],
    )(*stacked)


def _merge_layers(stacked, name):
    n = len(stacked)

    def body(*refs):
        outs = refs[n:2 * n]
        send_sems, recv_sems = refs[2 * n:]
        c = lax.axis_index("c")
        peer = (lax.axis_index("x"), lax.axis_index("y"), 1 - c)
        for t in range(n):
            pltpu.make_async_remote_copy(src_ref=outs[t].at[c], dst_ref=outs[t].at[c], send_sem=send_sems.at[t],
                                         recv_sem=recv_sems.at[t], device_id=peer, device_id_type=MESH).start()
        for t in range(n):
            cp = pltpu.make_async_remote_copy(src_ref=outs[t].at[c], dst_ref=outs[t].at[1 - c], send_sem=send_sems.at[t],
                                              recv_sem=recv_sems.at[t], device_id=peer, device_id_type=MESH)
            cp.wait_send()
            cp.wait_recv()

    sem = pltpu.SemaphoreType.DMA
    return pl.pallas_call(
        body, name=name, in_specs=[HBM] * n, out_specs=[HBM] * n,
        out_shape=[jax.ShapeDtypeStruct(a.shape, a.dtype) for a in stacked],
        input_output_aliases={t: t for t in range(n)}, scratch_shapes=[sem((n,)), sem((n,))],
    )(*stacked)


def _pair_sum(stacked, landed, core, name):
    _, K, N = stacked.shape
    tr = _pick(K, max(16, (1 << 19) // N // 16 * 16), 16)

    def body(c_ref, s_ref, l_ref, o_ref):
        o_ref[...] = (s_ref[...].astype(F32) + l_ref[...].astype(F32)).astype(o_ref.dtype)

    return pl.pallas_call(
        body, name=name,
        grid_spec=pltpu.PrefetchScalarGridSpec(
            num_scalar_prefetch=1, grid=(K // tr,),
            in_specs=[pl.BlockSpec((None, tr, N), lambda i, c: (c[0], i, 0)), pl.BlockSpec((tr, N), lambda i, c: (i, 0))],
            out_specs=pl.BlockSpec((tr, N), lambda i, c: (i, 0))),
        out_shape=jax.ShapeDtypeStruct((K, N), stacked.dtype), compiler_params=_params("parallel"),
    )(core, stacked, landed)


def _sum_chips(landed, core, name):
    _, R, C = landed.shape
    tr = _pick(R, max(SUBLANES, (1 << 19) // C // 16 * 16), 16)

    def body(c_ref, l_ref, o_ref):
        acc = l_ref[0].astype(F32) + l_ref[1].astype(F32)
        acc = acc + l_ref[2].astype(F32)
        o_ref[...] = acc + l_ref[3].astype(F32)

    return pl.pallas_call(
        body, name=name,
        grid_spec=pltpu.PrefetchScalarGridSpec(
            num_scalar_prefetch=1, grid=(R // tr,),
            in_specs=[pl.BlockSpec((N_CHIPS, tr, C), lambda i, c: (0, i, 0))],
            out_specs=pl.BlockSpec((None, tr, C), lambda i, c: (c[0], i, 0))),
        out_shape=jax.ShapeDtypeStruct((2, R, C), F32), compiler_params=_params("parallel"),
    )(core, landed)


def _adamw_math(w_ref, g_ref, m_ref, v_ref, d_ref, nm_ref, nv_ref):
    c1 = 1.0 / (1.0 - ADAM_B1 ** ADAM_STEP)
    c2 = 1.0 / (1.0 - ADAM_B2 ** ADAM_STEP)
    g = g_ref[...]
    nm = ADAM_B1 * m_ref[...] + (1.0 - ADAM_B1) * g
    nv = ADAM_B2 * v_ref[...] + (1.0 - ADAM_B2) * (g * g)
    nm_ref[...] = nm
    nv_ref[...] = nv
    d_ref[...] = -ADAM_LR * ((nm * c1) / (jnp.sqrt(nv * c2) + ADAM_EPS) + ADAM_WD * w_ref[...])


def _adamw_whole(w, g, m, v, name):
    shape = w.shape
    view = (-1,) + shape[-2:] if w.ndim >= 2 else (1, 1, -1)

    def body(*refs):
        _adamw_math(*refs)

    res = pl.pallas_call(body, name=name, out_shape=[jax.ShapeDtypeStruct(w.reshape(view).shape, F32)] * 3,
                         compiler_params=pltpu.CompilerParams(vmem_limit_bytes=VMEM_LIMIT_BYTES))(
        *(a.reshape(view) for a in (w, g, m, v)))
    return [r.reshape(shape) for r in res]


def _adamw(w, g, m, v, name):
    R, C = w.shape
    tr = _pick(R, max(SUBLANES, (1 << 18) // C // 8 * 8), SUBLANES)

    def body(*refs):
        _adamw_math(*refs)

    blk = pl.BlockSpec((tr, C), lambda i: (i, 0))
    return pl.pallas_call(
        body, name=name, grid=(R // tr,), in_specs=[blk] * 4, out_specs=[blk] * 3,
        out_shape=[jax.ShapeDtypeStruct((R, C), F32)] * 3, compiler_params=_params("parallel"),
    )(w, g, m, v)


def _pack_small(d, prefix=""):
    flat = jnp.concatenate([d[prefix + n].astype(F32).reshape(-1) for n in SMALL])
    pad = (-flat.shape[0]) % (2 * 16 * LANES)
    return jnp.pad(flat, (0, pad)).reshape(-1, LANES)


def _unpack_small(packed, shapes):
    flat = packed.reshape(-1)
    out, off = {}, 0
    for n in SMALL:
        size = int(np.prod(shapes[n]))
        out[n] = flat[off:off + size].reshape(shapes[n])
        off += size
    return out


def kernel(*args):
    p = dict(zip(INPUTS, args))
    x, mem, target = p["x"][0], p["mem"][0], p["loss_target"][0]

    names = [n for n, _ in BIG] + ["small"]
    core = lax.axis_index("c").astype(jnp.int32).reshape(1)
    chip = (2 * lax.axis_index("x") + lax.axis_index("y")).astype(jnp.int32).reshape(1)
    placed = [_place_shard(p[n], ax, chip, f"place_{n}") for n, ax in BIG]
    wb = dict(zip(names, _gather_shards(placed, [ax for _, ax in BIG], "gather_weights")))

    loss_part, dx, grads = _local_step(x, mem, target, p, wb)
    loss = lax.psum(loss_part, ("x", "y", "c"))

    stacked = [grads[n] for n, _ in BIG] + [_pack_small(grads).reshape(2, -1, LANES)]
    theirs = _swap_layers(stacked, "swap_layers")
    pair = [_pair_sum(s, o, core, f"pair_sum_{n}") for n, s, o in zip(names, stacked, theirs)]
    landed = _scatter_slices(pair, [ax - 1 for _, ax in BIG] + [None], "scatter_grads")
    reduced = [_sum_chips(ld.reshape(N_CHIPS, -1, ld.shape[-1]), core, f"sum_chips_{n}") for n, ld in zip(names, landed)]
    total = _merge_layers(reduced, "merge_layers")

    out = {}
    for (n, _), g in zip(BIG, total):
        sh = p[n].shape
        two_d = lambda a: a.reshape(-1, sh[-1])
        res = (g,) + tuple(_adamw(two_d(p[n]), two_d(g), two_d(p["m_" + n]), two_d(p["v_" + n]), f"adamw_{n}"))
        for key, r in zip(("grad_", "delta_", "new_m_", "new_v_"), res):
            out[key + n] = r.reshape(sh)
    for n, g in _unpack_small(total[-1], {n: p[n].shape for n in SMALL}).items():
        res = (g,) + tuple(_adamw_whole(p[n], g, p["m_" + n], p["v_" + n], f"adamw_{n}"))
        for key, r in zip(("grad_", "delta_", "new_m_", "new_v_"), res):
            out[key + n] = r

    result = [loss, dx.reshape(p["x"].shape)]
    for key in ("grad_", "delta_", "new_m_", "new_v_"):
        result += [out[key + n] for n in WEIGHTS]
    return tuple(result)
```

```python
import math

import jax
import jax.numpy as jnp
import numpy as np
from jax import lax
from jax.experimental import pallas as pl
from jax.experimental.pallas import tpu as pltpu

F32 = jnp.float32
BF16 = jnp.bfloat16
MESH = pl.DeviceIdType.MESH
HBM = pl.BlockSpec(memory_space=pltpu.HBM)

EPS = 1e-6
SSM_GROUP = 16
SSM_STATE = 64
ATTN_HEAD_DIM = 64
HEADS_PER_GROUP = 4
ATTN_CONFIGS = ((128, 1), (512, 4), (2048, 16))
ATTN_BLOCK = 128
NUM_BUCKETS = 32
REL_MAX_DISTANCE = 2048
NEG_INF = -1e30
MEM_HEADS = 4
ADAM_LR = 0.001
ADAM_B1 = 0.9
ADAM_B2 = 0.999
ADAM_EPS = 1e-08
ADAM_WD = 0.01
ADAM_STEP = 10

LANES = 128
SUBLANES = 8
VMEM_LIMIT_BYTES = 48 * 1024 * 1024
SSM_BLOCK_CH = 128

N_CHIPS = 4
BIG = (("w_in", 2), ("w_glu", 1), ("w_mem_kv", 1), ("w_br_ssm", 2), ("w_br_attn", 2), ("w_br_mem", 2), ("w_out", 1))
SMALL = ("norm_g", "mem_norm_g", "b_gate", "ssm_lambda_re", "ssm_lambda_im", "ssm_log_dt", "ssm_b_re", "ssm_b_im",
         "ssm_c_re", "ssm_c_im", "ssm_d", "b_glu", "rel_bias", "final_norm_g")
WEIGHTS = ("norm_g", "mem_norm_g", "w_in", "b_gate", "ssm_lambda_re", "ssm_lambda_im", "ssm_log_dt", "ssm_b_re",
           "ssm_b_im", "ssm_c_re", "ssm_c_im", "ssm_d", "w_glu", "b_glu", "w_mem_kv", "w_br_ssm", "w_br_attn",
           "w_br_mem", "w_out", "rel_bias", "final_norm_g")
INPUTS = ("x", "mem") + WEIGHTS + ("loss_target",) + tuple("m_" + n for n in WEIGHTS) + tuple("v_" + n for n in WEIGHTS)


def _params(*sem):
    return pltpu.CompilerParams(dimension_semantics=sem, vmem_limit_bytes=VMEM_LIMIT_BYTES)


def _pick(dim, pref, align):
    if dim <= pref:
        return dim
    t = pref - pref % align
    while t >= align:
        if dim % t == 0:
            return t
        t -= align
    return dim


def _sigmoid(v):
    return 0.5 * jnp.tanh(0.5 * v) + 0.5


def _silu_and_grad(z):
    s = _sigmoid(z)
    return z * s, s * (1.0 + z * (1.0 - s))


_GELU_C = math.sqrt(2.0 / math.pi)


def _gelu_and_grad(y):
    inner = _GELU_C * (y + 0.044715 * y * y * y)
    t = jnp.tanh(inner)
    g = 0.5 * y * (1.0 + t)
    dg = 0.5 * (1.0 + t) + 0.5 * y * (1.0 - t * t) * _GELU_C * (1.0 + 3.0 * 0.044715 * y * y)
    return g, dg


def _dot(a, b, dims):
    return lax.dot_general(a, b, (dims, ((), ())), preferred_element_type=F32)


NN = ((1,), (0,))
NT = ((1,), (1,))
TN = ((0,), (0,))


def _matmul(a, b, *, mode, name, out_dtype=F32, add=None, split_a=1, tm=1024, tn=768, tk=2304,
            a_lead=None, b_lead=None, b_off=0, n_cols=None, stack=None):
    ashape = a.shape if a_lead is None else a.shape[1:]
    K, M = ashape if mode == "tn" else ashape[::-1]
    bshape = b.shape if b_lead is None else b.shape[1:]
    N = n_cols or (bshape[0] if mode == "nt" else bshape[1])
    if mode != "tn" and M >= 4 * tm:
        tm = 2 * tm
    tm = _pick(M, tm, LANES if mode == "tn" else SUBLANES)
    tn = _pick(math.gcd(N, b_off) if b_off else N, tn, LANES)
    tk = _pick(K, tk, LANES)
    nk = K // tk
    joff = b_off // tn
    dims = {"nn": NN, "nt": NT, "tn": TN}[mode]
    has_add = add is not None
    has_prev = stack is not None and stack[2] is not None

    def body(*refs):
        a_ref, b_ref = refs[:2]
        add_ref = refs[2] if has_add else None
        o_ref = refs[-2] if nk > 1 else refs[-1]
        k = pl.program_id(2)
        bv = b_ref[...].astype(BF16)
        if split_a > 1:
            rest = a_ref[...].astype(F32)
            part = 0.0
            for _ in range(split_a):
                piece = rest.astype(BF16)
                part = part + _dot(piece, bv, dims)
                rest = rest - piece.astype(F32)
        else:
            part = _dot(a_ref[...].astype(BF16), bv, dims)

        def finish(r):
            if has_add:
                r = r + add_ref[...]
            o_ref[...] = r.astype(out_dtype)

        if nk == 1:
            finish(part)
            return
        acc_ref = refs[-1]

        @pl.when(k == 0)
        def _():
            acc_ref[...] = part

        @pl.when((k > 0) & (k < nk - 1))
        def _():
            acc_ref[...] += part

        @pl.when(k == nk - 1)
        def _():
            finish(acc_ref[...] + part)

    alead = () if a_lead is None else (a_lead,)
    alead_blk = () if a_lead is None else (None,)
    if mode == "tn":
        a_spec = pl.BlockSpec(alead_blk + (tk, tm), lambda i, j, k: alead + (k, i))
    else:
        a_spec = pl.BlockSpec(alead_blk + (tm, tk), lambda i, j, k: alead + (i, k))
    lead = () if b_lead is None else (b_lead,)
    lead_blk = () if b_lead is None else (None,)
    if mode == "nt":
        b_spec = pl.BlockSpec(lead_blk + (tn, tk), lambda i, j, k: lead + (j + joff, k))
    else:
        b_spec = pl.BlockSpec(lead_blk + (tk, tn), lambda i, j, k: lead + (k, j + joff))
    in_specs = [a_spec, b_spec]
    args = [a, b]
    if has_add:
        in_specs.append(pl.BlockSpec((tm, tn), lambda i, j, k: (i, j)))
        args.append(add)
    aliases = {}
    if stack is None:
        out_spec = pl.BlockSpec((tm, tn), lambda i, j, k: (i, j))
        out_shape = jax.ShapeDtypeStruct((M, N), out_dtype)
    else:
        layer, depth, prev = stack
        out_spec = pl.BlockSpec((None, tm, tn), lambda i, j, k: (layer, i, j))
        out_shape = jax.ShapeDtypeStruct((depth, M, N), out_dtype)
        if has_prev:
            in_specs.append(pl.BlockSpec(memory_space=pl.ANY))
            args.append(prev)
            aliases = {len(args) - 1: 0}
    return pl.pallas_call(
        body, name=name, grid=(M // tm, N // tn, nk), in_specs=in_specs, out_specs=out_spec, out_shape=out_shape,
        scratch_shapes=[pltpu.VMEM((tm, tn), F32)] if nk > 1 else [], input_output_aliases=aliases,
        compiler_params=_params("parallel", "parallel", "arbitrary"),
    )(*args)


def _rmsnorm(x, g, name):
    T, D = x.shape
    tm = _pick(T, 512, SUBLANES)

    def body(x_ref, g_ref, h_ref):
        xv = x_ref[...]
        r = lax.rsqrt(jnp.mean(xv * xv, axis=-1, keepdims=True) + EPS)
        h_ref[...] = (xv * r * g_ref[...]).astype(BF16)

    return pl.pallas_call(
        body, name=name, grid=(T // tm,),
        in_specs=[pl.BlockSpec((tm, D), lambda i: (i, 0)), pl.BlockSpec((1, D), lambda i: (0, 0))],
        out_specs=pl.BlockSpec((tm, D), lambda i: (i, 0)),
        out_shape=jax.ShapeDtypeStruct((T, D), BF16), compiler_params=_params("parallel"),
    )(x, g.reshape(1, D))


def _rmsnorm_bwd(x, g, dh, dres, name):
    T, D = x.shape
    tm = _pick(T, 512, SUBLANES)
    with_res = dres is not None

    def body(*refs):
        if with_res:
            x_ref, g_ref, dh_ref, dres_ref, dx_ref, dg_ref = refs
        else:
            x_ref, g_ref, dh_ref, dx_ref, dg_ref = refs
        xv = x_ref[...]
        dhv = dh_ref[...]
        r = lax.rsqrt(jnp.mean(xv * xv, axis=-1, keepdims=True) + EPS)
        dyg = dhv * g_ref[...]
        c = jnp.mean(dyg * xv, axis=-1, keepdims=True)
        dx = r * dyg - xv * (r * r * r) * c
        if with_res:
            dx = dx + dres_ref[...]
        dx_ref[...] = dx

        @pl.when(pl.program_id(0) == 0)
        def _():
            dg_ref[...] = jnp.zeros_like(dg_ref)

        dg_ref[...] += jnp.sum(dhv * xv * r, axis=0, keepdims=True)

    row = pl.BlockSpec((tm, D), lambda i: (i, 0))
    vec = pl.BlockSpec((1, D), lambda i: (0, 0))
    ins = [x, g.reshape(1, D), dh] + ([dres] if with_res else [])
    return pl.pallas_call(
        body, name=name, grid=(T // tm,), in_specs=[row, vec, row] + ([row] if with_res else []),
        out_specs=[row, vec],
        out_shape=[jax.ShapeDtypeStruct((T, D), F32), jax.ShapeDtypeStruct((1, D), F32)],
        compiler_params=_params("arbitrary"),
    )(*ins)


def _loss_head(x, g, target, name):
    T, D = x.shape
    tm = _pick(T, 512, SUBLANES)

    def body(x_ref, g_ref, t_ref, loss_ref, dx_ref, dg_ref):
        xv = x_ref[...]
        gv = g_ref[...]
        r = lax.rsqrt(jnp.mean(xv * xv, axis=-1, keepdims=True) + EPS)
        e = xv * r * gv - t_ref[...]
        dy = e * (1.0 / D)
        dyg = dy * gv
        c = jnp.mean(dyg * xv, axis=-1, keepdims=True)
        dx_ref[...] = r * dyg - xv * (r * r * r) * c

        @pl.when(pl.program_id(0) == 0)
        def _():
            loss_ref[...] = jnp.zeros_like(loss_ref)
            dg_ref[...] = jnp.zeros_like(dg_ref)

        loss_ref[...] += jnp.sum(e * e, axis=0, keepdims=True) * (0.5 / D)
        dg_ref[...] += jnp.sum(dy * xv * r, axis=0, keepdims=True)

    row = pl.BlockSpec((tm, D), lambda i: (i, 0))
    vec = pl.BlockSpec((1, D), lambda i: (0, 0))
    return pl.pallas_call(
        body, name=name, grid=(T // tm,), in_specs=[row, vec, row], out_specs=[vec, row, vec],
        out_shape=[jax.ShapeDtypeStruct((1, D), F32), jax.ShapeDtypeStruct((T, D), F32), jax.ShapeDtypeStruct((1, D), F32)],
        compiler_params=_params("arbitrary"),
    )(x, g.reshape(1, D), target)


def _ssm_disc_math(lre, lim, logdt, br, bi):
    dt = jnp.exp(logdt)
    mag = jnp.exp(lre * dt)
    ar = mag * jnp.cos(lim * dt)
    ai = mag * jnp.sin(lim * dt)
    den = lre * lre + lim * lim
    nr = ar - 1.0
    fr = (nr * lre + ai * lim) / den
    fi = (ai * lre - nr * lim) / den
    return ar, ai, fr[None] * br - fi[None] * bi, fr[None] * bi + fi[None] * br


def _ssm_disc(lre, lim, logdt, br, bi, name):
    def body(lre_ref, lim_ref, dt_ref, br_ref, bi_ref, ar_ref, ai_ref, bbr_ref, bbi_ref):
        ar, ai, bbr, bbi = _ssm_disc_math(lre_ref[...], lim_ref[...], dt_ref[...], br_ref[...], bi_ref[...])
        ar_ref[...] = ar
        ai_ref[...] = ai
        bbr_ref[...] = bbr
        bbi_ref[...] = bbi

    sd = jax.ShapeDtypeStruct
    return pl.pallas_call(
        body, name=name, out_shape=[sd(lre.shape, F32), sd(lre.shape, F32), sd(br.shape, F32), sd(br.shape, F32)],
    )(lre, lim, logdt, br, bi)


def _ssm_disc_bwd(lre, lim, logdt, br, bi, dar, dai, dbbr, dbbi, name):
    def body(lre_ref, lim_ref, dt_ref, br_ref, bi_ref, dar_ref, dai_ref, dbbr_ref, dbbi_ref,
             glre_ref, glim_ref, gdt_ref, gbr_ref, gbi_ref):
        _, vjp = jax.vjp(_ssm_disc_math, lre_ref[...], lim_ref[...], dt_ref[...], br_ref[...], bi_ref[...])
        glre, glim, gdt, gbr, gbi = vjp((dar_ref[...], dai_ref[...], dbbr_ref[...], dbbi_ref[...]))
        glre_ref[...] = glre
        glim_ref[...] = glim
        gdt_ref[...] = gdt
        gbr_ref[...] = gbr
        gbi_ref[...] = gbi

    sd = jax.ShapeDtypeStruct
    return pl.pallas_call(
        body, name=name,
        out_shape=[sd(lre.shape, F32), sd(lre.shape, F32), sd(logdt.shape, F32), sd(br.shape, F32), sd(br.shape, F32)],
    )(lre, lim, logdt, br, bi, dar, dai, dbbr, dbbi)


SSM_STEPS_FWD = 256
SSM_STEPS_BWD = 256


def _ssm_tiles(ref, v, off, steps, n):
    return [ref[v, pl.ds(off + j, steps, stride=SUBLANES), :] for j in range(n)]


def _ssm_fwd(uz, bmat, cmat, art, ait, dvec, name, steps=SSM_STEPS_FWD):
    T = uz.shape[0]
    nblk, cb, width = bmat.shape
    C = nblk * cb
    half = width // 2
    nt = half // LANES
    npair = nblk // 2
    kc = min(steps, T)
    nchunk = T // kc

    def body(u_ref, b_ref, c_ref, ar_ref, ai_ref, d_ref, y_ref, xr_ref, xi_ref, sr_ref, si_ref):
        @pl.when(pl.program_id(0) == 0)
        def _():
            sr_ref[...] = jnp.zeros_like(sr_ref)
            si_ref[...] = jnp.zeros_like(si_ref)

        uv = u_ref[...]
        for b in range(nblk):
            bu = _dot(uv[:, b * cb:(b + 1) * cb].astype(BF16), b_ref[b], NN)
            v, off = b // 2, nt * (b % 2)
            for j in range(nt):
                xr_ref[v, pl.ds(off + j, kc, stride=SUBLANES), :] = bu[:, j * LANES:(j + 1) * LANES]
                xi_ref[v, pl.ds(off + j, kc, stride=SUBLANES), :] = bu[:, half + j * LANES:half + (j + 1) * LANES]
        ars = [ar_ref[v] for v in range(npair)]
        ais = [ai_ref[v] for v in range(npair)]

        def step(k, carry):
            row = pl.ds(k * SUBLANES, SUBLANES)
            out = []
            for v in range(npair):
                xr, xi = carry[2 * v], carry[2 * v + 1]
                nr = ars[v] * xr - ais[v] * xi + xr_ref[v, row, :]
                ni = ars[v] * xi + ais[v] * xr + xi_ref[v, row, :]
                xr_ref[v, row, :] = nr
                xi_ref[v, row, :] = ni
                out += [nr, ni]
            return tuple(out)

        fin = tuple(ref[v] for v in range(npair) for ref in (sr_ref, si_ref))
        for k in range(kc):
            fin = step(k, fin)
        for v in range(npair):
            sr_ref[v] = fin[2 * v]
            si_ref[v] = fin[2 * v + 1]
        for b in range(nblk):
            v, off = b // 2, nt * (b % 2)
            xb = jnp.concatenate(_ssm_tiles(xr_ref, v, off, kc, nt) + _ssm_tiles(xi_ref, v, off, kc, nt), axis=1)
            cols = slice(b * cb, (b + 1) * cb)
            y_ref[:, cols] = _dot(xb.astype(BF16), c_ref[b], NN) + d_ref[:, cols] * uv[:, cols]

    whole = lambda a: pl.BlockSpec(a.shape, lambda c: (0,) * a.ndim)
    st = pl.BlockSpec((npair, kc * SUBLANES, LANES), lambda c: (0, c, 0))
    sd = jax.ShapeDtypeStruct
    return pl.pallas_call(
        body, name=name, grid=(nchunk,),
        in_specs=[pl.BlockSpec((kc, C), lambda c: (c, 0)), whole(bmat), whole(cmat), whole(art), whole(ait), whole(dvec)],
        out_specs=[pl.BlockSpec((kc, C), lambda c: (c, 0)), st, st],
        out_shape=[sd((T, C), F32), sd((npair, T * SUBLANES, LANES), F32), sd((npair, T * SUBLANES, LANES), F32)],
        scratch_shapes=[pltpu.VMEM((npair, SUBLANES, LANES), F32), pltpu.VMEM((npair, SUBLANES, LANES), F32)],
        compiler_params=_params("arbitrary"),
    )(uz, bmat, cmat, art, ait, dvec)


def _ssm_bwd(dy, uz, xr, xi, bmat, cmat, art, ait, dvec, dproj, name):
    T = uz.shape[0]
    nblk, cb, width = bmat.shape
    C = nblk * cb
    half = width // 2
    nt = half // LANES
    npair = nblk // 2
    kc = min(SSM_STEPS_BWD, T)
    nchunk = T // kc

    def body(dy_ref, u_ref, xr_ref, xi_ref, xpr_ref, xpi_ref, b_ref, c_ref, ar_ref, ai_ref, d_ref, _,
             du_ref, db_ref, dc_ref, dar_ref, dai_ref, dd_ref, gr_ref, gi_ref, sr_ref, si_ref):
        c = pl.program_id(0)

        @pl.when(c == 0)
        def _():
            for ref in (sr_ref, si_ref, db_ref, dc_ref, dar_ref, dai_ref, dd_ref):
                ref[...] = jnp.zeros_like(ref)

        dyv = dy_ref[...]
        uv = u_ref[...]
        for b in range(nblk):
            dx = _dot(dyv[:, b * cb:(b + 1) * cb].astype(BF16), c_ref[b], NT)
            v, off = b // 2, nt * (b % 2)
            for j in range(nt):
                gr_ref[v, pl.ds(off + j, kc, stride=SUBLANES), :] = dx[:, j * LANES:(j + 1) * LANES]
                gi_ref[v, pl.ds(off + j, kc, stride=SUBLANES), :] = dx[:, half + j * LANES:half + (j + 1) * LANES]
        ars = [ar_ref[v] for v in range(npair)]
        ais = [ai_ref[v] for v in range(npair)]

        def pair_update(v, gr, gi, row):
            nr = ars[v] * gr + ais[v] * gi + gr_ref[v, row, :]
            ni = ars[v] * gi - ais[v] * gr + gi_ref[v, row, :]
            gr_ref[v, row, :] = nr
            gi_ref[v, row, :] = ni
            return nr, ni

        def step(i, carry):
            k = kc - 1 - i
            row = pl.ds(k * SUBLANES, SUBLANES)
            prow = pl.ds((k - 1) * SUBLANES, SUBLANES)
            out = []
            for v in range(npair):
                gr, gi, sr, si = carry[4 * v:4 * v + 4]
                nr, ni = pair_update(v, gr, gi, row)
                pr, pi = xr_ref[v, prow, :], xi_ref[v, prow, :]
                out += [nr, ni, sr + pr * nr + pi * ni, si + pr * ni - pi * nr]
            return tuple(out)

        mid = tuple(ref[v] for v in range(npair) for ref in (sr_ref, si_ref, dar_ref, dai_ref))
        for i in range(kc - 1):
            mid = step(i, mid)
        live = (c < nchunk - 1).astype(F32)
        row0 = pl.ds(0, SUBLANES)
        for v in range(npair):
            gr, gi, sr, si = mid[4 * v:4 * v + 4]
            nr, ni = pair_update(v, gr, gi, row0)
            pr, pi = xpr_ref[v] * live, xpi_ref[v] * live
            sr_ref[v] = nr
            si_ref[v] = ni
            dar_ref[v] = sr + pr * nr + pi * ni
            dai_ref[v] = si + pr * ni - pi * nr
        for b in range(nblk):
            v, off = b // 2, nt * (b % 2)
            cols = slice(b * cb, (b + 1) * cb)
            gb = jnp.concatenate(_ssm_tiles(gr_ref, v, off, kc, nt) + _ssm_tiles(gi_ref, v, off, kc, nt), axis=1).astype(BF16)
            xb = jnp.concatenate(_ssm_tiles(xr_ref, v, off, kc, nt) + _ssm_tiles(xi_ref, v, off, kc, nt), axis=1).astype(BF16)
            du_ref[:, cols] = (_dot(gb, b_ref[b], NT) + dyv[:, cols] * d_ref[:, cols]).astype(BF16)
            db_ref[b] += _dot(uv[:, cols].astype(BF16), gb, TN)
            dc_ref[b] += _dot(dyv[:, cols].astype(BF16), xb, TN)
        dd_ref[...] += jnp.sum(dyv * uv, axis=0, keepdims=True)

    whole = lambda a: pl.BlockSpec(a.shape, lambda c: (0,) * a.ndim)
    rev = lambda c: (nchunk - 1 - c, 0)
    st = pl.BlockSpec((npair, kc * SUBLANES, LANES), lambda c: (0, nchunk - 1 - c, 0))
    stp = pl.BlockSpec((npair, SUBLANES, LANES), lambda c: (0, jnp.maximum((nchunk - 1 - c) * kc - 1, 0), 0))
    acc = lambda shape: pl.BlockSpec(shape, lambda c: (0,) * len(shape))
    sd = jax.ShapeDtypeStruct
    pair_shape = (npair, SUBLANES, LANES)
    return pl.pallas_call(
        body, name=name, grid=(nchunk,),
        in_specs=[pl.BlockSpec((kc, C), rev), pl.BlockSpec((kc, C), rev), st, st, stp, stp, whole(bmat), whole(cmat),
                  whole(art), whole(ait), whole(dvec), pl.BlockSpec(memory_space=pl.ANY)],
        out_specs=[pl.BlockSpec((kc, C), rev), acc(bmat.shape), acc(bmat.shape), acc(pair_shape), acc(pair_shape), acc((1, C))],
        out_shape=[sd(dproj.shape, BF16), sd(bmat.shape, F32), sd(bmat.shape, F32), sd(pair_shape, F32), sd(pair_shape, F32),
                   sd((1, C), F32)],
        scratch_shapes=[pltpu.VMEM((npair, kc * SUBLANES, LANES), F32), pltpu.VMEM((npair, kc * SUBLANES, LANES), F32),
                        pltpu.VMEM(pair_shape, F32), pltpu.VMEM(pair_shape, F32)],
        input_output_aliases={11: 0}, compiler_params=_params("arbitrary"),
    )(dy, uz, xr, xi, xr, xi, bmat, cmat, art, ait, dvec, dproj)


def _ssm_post(y, z, w_glu, b_glu, name):
    T, C = y.shape
    tm = _pick(T, 512, SUBLANES)

    def body(y_ref, z_ref, w_ref, b_ref, o_ref, a_ref):
        a, _ = _gelu_and_grad(y_ref[...])
        ab = a.astype(BF16)
        sg = _sigmoid(_dot(ab, w_ref[...], NN) + b_ref[...])
        sz, _ = _silu_and_grad(z_ref[...].astype(F32))
        o_ref[...] = (a * sg * sz).astype(BF16)
        a_ref[...] = ab

    row = pl.BlockSpec((tm, C), lambda i: (i, 0))
    return pl.pallas_call(
        body, name=name, grid=(T // tm,),
        in_specs=[row, row, pl.BlockSpec((C, C), lambda i: (0, 0)), pl.BlockSpec((1, C), lambda i: (0, 0))],
        out_specs=[row, row], out_shape=[jax.ShapeDtypeStruct((T, C), BF16)] * 2, compiler_params=_params("parallel"),
    )(y, z, w_glu, b_glu.reshape(1, C))


def _ssm_post_bwd(do, y, z, w_glu, b_glu, dproj, col, name):
    T, C = y.shape
    tm = _pick(T, 512, SUBLANES)

    def body(do_ref, y_ref, z_ref, w_ref, b_ref, _, dy_ref, dz_ref, ds_ref, db_ref):
        dov = do_ref[...]
        a, da_dy = _gelu_and_grad(y_ref[...])
        sg = _sigmoid(_dot(a.astype(BF16), w_ref[...], NN) + b_ref[...])
        sz, dsz = _silu_and_grad(z_ref[...].astype(F32))
        yg = a * sg
        dz_ref[...] = (dov * yg * dsz).astype(BF16)
        dyg = dov * sz
        ds = dyg * a * sg * (1.0 - sg)
        dsb = ds.astype(BF16)
        ds_ref[...] = dsb
        da = dyg * sg + _dot(dsb, w_ref[...], NT)
        dy_ref[...] = da * da_dy

        @pl.when(pl.program_id(0) == 0)
        def _():
            db_ref[...] = jnp.zeros_like(db_ref)

        db_ref[...] += jnp.sum(ds, axis=0, keepdims=True)

    row = pl.BlockSpec((tm, C), lambda i: (i, 0))
    vec = pl.BlockSpec((1, C), lambda i: (0, 0))
    sd = jax.ShapeDtypeStruct
    return pl.pallas_call(
        body, name=name, grid=(T // tm,),
        in_specs=[row, row, row, pl.BlockSpec((C, C), lambda i: (0, 0)), vec, pl.BlockSpec(memory_space=pl.ANY)],
        out_specs=[row, pl.BlockSpec((tm, C), lambda i: (i, col // C)), row, vec],
        out_shape=[sd((T, C), F32), sd(dproj.shape, BF16), sd((T, C), BF16), sd((1, C), F32)],
        input_output_aliases={5: 1}, compiler_params=_params("arbitrary"),
    )(do, y, z, w_glu, b_glu.reshape(1, C), dproj)


def _rel_bucket(dist):
    n = jnp.maximum(dist, 0)
    max_exact = NUM_BUCKETS // 2
    n_f = jnp.maximum(n, 1).astype(F32)
    large = max_exact + (jnp.log(n_f / max_exact) / math.log(REL_MAX_DISTANCE / max_exact)
                         * (NUM_BUCKETS - max_exact)).astype(jnp.int32)
    large = jnp.minimum(large, NUM_BUCKETS - 1)
    return jnp.where(n < max_exact, n, large)


def _band_tables():
    qi = jnp.arange(ATTN_BLOCK)[:, None]
    kj = jnp.arange(2 * ATTN_BLOCK)[None, :]
    delta = ATTN_BLOCK + qi - kj
    buckets, bands = [], []
    for window, dilation in ATTN_CONFIGS:
        bands.append((delta >= 0) & (delta <= window // dilation))
        buckets.append(_rel_bucket(jnp.maximum(delta, 0) * dilation))
    return jnp.stack(buckets), jnp.stack(bands)


ATTN_UNITS = 4


def _attn_tile(T, r):
    nq = max(1, ATTN_UNITS // r)
    rows = ATTN_BLOCK * r * nq
    return nq, rows, T // rows


def _attn_units(r, nq, chunk):
    if r >= ATTN_UNITS:
        return [(chunk * ATTN_UNITS + i, None) for i in range(ATTN_UNITS)]
    units = []
    for j in range(nq):
        for s in range(r):
            units.append((ATTN_BLOCK * j * r + s, ATTN_BLOCK * (j - 1) * r + s if j else None))
    return units


def _rows(start, r):
    return pl.ds(start, ATTN_BLOCK, stride=r) if r > 1 else pl.ds(start, ATTN_BLOCK)


def _attn_group_fwd(qkv, biasm, g, name):
    T = qkv.shape[0]
    r = ATTN_CONFIGS[g][1]
    B, hd = ATTN_BLOCK, ATTN_HEAD_DIM
    nq, rows, ntiles = _attn_tile(T, r)
    nchunks = max(1, r // ATTN_UNITS)
    last_prev = B * (nq - 1) * r
    scale = hd ** -0.5
    tiles_per_tensor = 3 * HEADS_PER_GROUP * hd // LANES

    def body(q_ref, kc_ref, kp_ref, vc_ref, vp_ref, bias_ref, o_ref, lse_ref, s_ref, p_ref):
        n = pl.program_id(1)
        lane = lax.broadcasted_iota(jnp.int32, (1, LANES), 1)
        col = lax.broadcasted_iota(jnp.int32, (1, 2 * B), 1)
        masks = [lane < hd, lane >= hd]
        first_pen = jnp.where((col < B) & (n == 0), NEG_INF, 0.0)

        def chunk_body(chunk):
            units = _attn_units(r, nq, chunk)

            def keys(cur_ref, prev_ref, cs, ps):
                prev = prev_ref[_rows(last_prev + (cs if r >= ATTN_UNITS else cs % r), r), :] if ps is None else cur_ref[_rows(ps, r), :]
                return jnp.concatenate([prev, cur_ref[_rows(cs, r), :]], axis=0).astype(BF16)

            for u, (cs, ps) in enumerate(units):
                qv = q_ref[_rows(cs, r), :]
                kw = keys(kc_ref, kp_ref, cs, ps)
                for hh in range(2):
                    s_ref[2 * u + hh] = _dot(jnp.where(masks[hh], qv, 0.0).astype(BF16), kw, NT)
            for u, (cs, ps) in enumerate(units):
                lses = []
                for hh in range(2):
                    s = s_ref[2 * u + hh] * scale + bias_ref[hh]
                    if ps is None:
                        s = s + first_pen
                    m = jnp.max(s, axis=-1, keepdims=True)
                    p = jnp.exp(s - m)
                    l = jnp.sum(p, axis=-1, keepdims=True)
                    p_ref[2 * u + hh] = (p * (1.0 / l)).astype(BF16)
                    lses.append(m + jnp.log(l))
                lse_ref[_rows(cs, r), :] = jnp.where(masks[0], lses[0], lses[1])
            for u, (cs, ps) in enumerate(units):
                vw = keys(vc_ref, vp_ref, cs, ps)
                o_ref[_rows(cs, r), :] = (_dot(p_ref[2 * u], jnp.where(masks[0], vw, 0), NN)
                                          + _dot(p_ref[2 * u + 1], jnp.where(masks[1], vw, 0), NN))

        if nchunks == 1:
            chunk_body(0)
        else:
            pl.loop(0, nchunks)(chunk_body)

    def cur(t):
        return pl.BlockSpec((rows, LANES), lambda hf, n: (n, t * tiles_per_tensor + 2 * g + hf))

    def prev(t):
        return pl.BlockSpec((rows, LANES), lambda hf, n: (jnp.maximum(n - 1, 0), t * tiles_per_tensor + 2 * g + hf))

    out = pl.BlockSpec((rows, LANES), lambda hf, n: (n, hf))
    sd = jax.ShapeDtypeStruct((T, 2 * LANES), F32)
    return pl.pallas_call(
        body, name=name, grid=(2, ntiles),
        in_specs=[cur(0), cur(1), prev(1), cur(2), prev(2), pl.BlockSpec((None, 2, B, 2 * B), lambda hf, n: (g, hf, 0, 0))],
        out_specs=[out, out], out_shape=[sd, sd],
        scratch_shapes=[pltpu.VMEM((2 * ATTN_UNITS, B, 2 * B), F32), pltpu.VMEM((2 * ATTN_UNITS, B, 2 * B), BF16)],
        compiler_params=_params("parallel", "parallel"),
    )(qkv, qkv, qkv, qkv, qkv, biasm)


def _attn_group_bwd(qkv, do, dvec, lse, biasm, g, dproj, dk_col, name):
    T = qkv.shape[0]
    r = ATTN_CONFIGS[g][1]
    B, hd = ATTN_BLOCK, ATTN_HEAD_DIM
    nq, rows, ntiles = _attn_tile(T, r)
    nchunks = max(1, r // ATTN_UNITS)
    last_prev = B * (nq - 1) * r
    scale = hd ** -0.5
    tiles_per_tensor = 3 * HEADS_PER_GROUP * hd // LANES

    def body(q_ref, kc_ref, kp_ref, vc_ref, vp_ref, do_ref, dv_ref, lse_ref, bias_ref, _,
             dq_ref, dk_ref, dvo_ref, dbias_ref, ck_ref, cv_ref, ak_ref, av_ref, s_ref, dp_ref, p_ref, ds_ref):
        n = pl.program_id(1)
        lane = lax.broadcasted_iota(jnp.int32, (1, LANES), 1)
        col = lax.broadcasted_iota(jnp.int32, (1, 2 * B), 1)
        masks = [lane < hd, lane >= hd]
        first_pen = jnp.where((col < B) & (n == 0), NEG_INF, 0.0)

        @pl.when(n == 0)
        def _():
            dbias_ref[...] = jnp.zeros_like(dbias_ref)
            ck_ref[...] = jnp.zeros_like(ck_ref)
            cv_ref[...] = jnp.zeros_like(cv_ref)

        def chunk_body(chunk):
            units = _attn_units(r, nq, chunk)

            def prev_rows(cs):
                return _rows(last_prev + (cs if r >= ATTN_UNITS else cs % r), r)

            def keys(cur_ref, prev_ref, cs, ps):
                prev = prev_ref[prev_rows(cs), :] if ps is None else cur_ref[_rows(ps, r), :]
                return jnp.concatenate([prev, cur_ref[_rows(cs, r), :]], axis=0).astype(BF16)

            for u, (cs, ps) in enumerate(units):
                qv = q_ref[_rows(cs, r), :]
                dov = do_ref[_rows(cs, r), :]
                kw = keys(kc_ref, kp_ref, cs, ps)
                vw = keys(vc_ref, vp_ref, cs, ps)
                for hh in range(2):
                    s_ref[2 * u + hh] = _dot(jnp.where(masks[hh], qv, 0.0).astype(BF16), kw, NT)
                    dp_ref[2 * u + hh] = _dot(jnp.where(masks[hh], dov, 0.0).astype(BF16), vw, NT)
            for u, (cs, ps) in enumerate(units):
                lse_t = lse_ref[_rows(cs, r), :]
                dv_t = dv_ref[_rows(cs, r), :]
                for hh in range(2):
                    lo = hh * hd
                    s = s_ref[2 * u + hh] * scale + bias_ref[hh]
                    if ps is None:
                        s = s + first_pen
                    p = jnp.exp(s - lse_t[:, lo:lo + 1])
                    ds = p * (dp_ref[2 * u + hh] + dv_t[:, lo:lo + 1])
                    dbias_ref[hh] += ds
                    p_ref[2 * u + hh] = p.astype(BF16)
                    ds_ref[2 * u + hh] = ds.astype(BF16)
            for u, (cs, ps) in enumerate(units):
                qv = q_ref[_rows(cs, r), :]
                dov = do_ref[_rows(cs, r), :]
                kw = keys(kc_ref, kp_ref, cs, ps)
                dq, dkw, dvw = 0.0, 0.0, 0.0
                for hh in range(2):
                    dsb = ds_ref[2 * u + hh]
                    dq = dq + _dot(dsb, jnp.where(masks[hh], kw, 0), NN)
                    dkw = dkw + _dot(dsb, jnp.where(masks[hh], qv, 0.0).astype(BF16), TN)
                    dvw = dvw + _dot(p_ref[2 * u + hh], jnp.where(masks[hh], dov, 0.0).astype(BF16), TN)
                dq_ref[_rows(cs, r), :] = dq * scale
                ak_ref[_rows(cs, r), :] = dkw[B:] * scale
                av_ref[_rows(cs, r), :] = dvw[B:]
                if ps is None:
                    ck_ref[prev_rows(cs), :] += dkw[:B] * scale
                    cv_ref[prev_rows(cs), :] += dvw[:B]
                else:
                    ak_ref[_rows(ps, r), :] += dkw[:B] * scale
                    av_ref[_rows(ps, r), :] += dvw[:B]

        @pl.when(n < ntiles)
        def _():
            for chunk in range(nchunks):
                chunk_body(chunk)

        dk_ref[...] = ck_ref[...].astype(BF16)
        dvo_ref[...] = cv_ref[...].astype(BF16)
        ck_ref[...] = ak_ref[...]
        cv_ref[...] = av_ref[...]

    last = ntiles - 1

    def cur(t):
        return pl.BlockSpec((rows, LANES), lambda hf, n: (jnp.minimum(n, last), t * tiles_per_tensor + 2 * g + hf))

    def prev(t):
        return pl.BlockSpec((rows, LANES), lambda hf, n: (jnp.clip(n - 1, 0, last), t * tiles_per_tensor + 2 * g + hf))

    nat = pl.BlockSpec((rows, LANES), lambda hf, n: (jnp.minimum(n, last), hf))
    nat_prev = pl.BlockSpec((rows, LANES), lambda hf, n: (jnp.clip(n - 1, 0, last), hf))
    tab = pl.BlockSpec((None, 2, B, 2 * B), lambda hf, n: (g, hf, 0, 0))
    dtab = pl.BlockSpec((2, B, 2 * B), lambda hf, n: (hf, 0, 0))
    sd = jax.ShapeDtypeStruct
    vm = pltpu.VMEM
    dk_tile = dk_col // LANES + 2 * g
    dk_spec = pl.BlockSpec((rows, LANES), lambda hf, n: (jnp.clip(n - 1, 0, last), dk_tile + hf))
    return pl.pallas_call(
        body, name=name, grid=(2, ntiles + 1),
        in_specs=[cur(0), cur(1), prev(1), cur(2), prev(2), nat, nat, nat, tab, pl.BlockSpec(memory_space=pl.ANY)],
        out_specs=[nat, dk_spec, nat_prev, dtab],
        out_shape=[sd((T, 2 * LANES), F32), sd(dproj.shape, BF16), sd((T, 2 * LANES), BF16),
                   sd((HEADS_PER_GROUP, B, 2 * B), F32)],
        scratch_shapes=[vm((rows, LANES), F32), vm((rows, LANES), F32), vm((rows, LANES), F32), vm((rows, LANES), F32),
                        vm((2 * ATTN_UNITS, B, 2 * B), F32), vm((2 * ATTN_UNITS, B, 2 * B), F32),
                        vm((2 * ATTN_UNITS, B, 2 * B), BF16), vm((2 * ATTN_UNITS, B, 2 * B), BF16)],
        input_output_aliases={9: 1}, compiler_params=_params("parallel", "arbitrary"),
    )(qkv, qkv, qkv, qkv, qkv, do, dvec, lse, biasm, dproj)


def _attn_mix(os, lses, z, name):
    T, gw = os[0].shape
    C = z.shape[1]
    tm = _pick(T, 512, SUBLANES)

    def body(o0_ref, o1_ref, o2_ref, l0_ref, l1_ref, l2_ref, z_ref, out_ref):
        ls = [l0_ref[...], l1_ref[...], l2_ref[...]]
        mx = jnp.maximum(jnp.maximum(ls[0], ls[1]), ls[2])
        es = [jnp.exp(l - mx) for l in ls]
        inv = 1.0 / (es[0] + es[1] + es[2])
        for i, o_ref in enumerate((o0_ref, o1_ref, o2_ref)):
            sz, _ = _silu_and_grad(z_ref[:, i * gw:(i + 1) * gw].astype(F32))
            out_ref[:, i * gw:(i + 1) * gw] = (o_ref[...] * (es[i] * inv) * sz).astype(BF16)

    row = pl.BlockSpec((tm, C), lambda i: (i, 0))
    grp = pl.BlockSpec((tm, gw), lambda i: (i, 0))
    return pl.pallas_call(
        body, name=name, grid=(T // tm,), in_specs=[grp] * 6 + [row], out_specs=row,
        out_shape=jax.ShapeDtypeStruct((T, C), BF16), compiler_params=_params("parallel"),
    )(*os, *lses, z)


def _attn_mix_bwd(dout, os, lses, z, dproj, col, name):
    T, gw = os[0].shape
    C = z.shape[1]
    tm = _pick(T, 512, SUBLANES)
    head_of = np.arange(gw) // ATTN_HEAD_DIM
    ones = jnp.asarray(head_of[:, None] == head_of[None, :], BF16)

    def body(dout_ref, o0_ref, o1_ref, o2_ref, l0_ref, l1_ref, l2_ref, z_ref, ones_ref, _,
             dz_ref, do0_ref, do1_ref, do2_ref, dv0_ref, dv1_ref, dv2_ref):
        ls = [l0_ref[...], l1_ref[...], l2_ref[...]]
        mx = jnp.maximum(jnp.maximum(ls[0], ls[1]), ls[2])
        es = [jnp.exp(l - mx) for l in ls]
        inv = 1.0 / (es[0] + es[1] + es[2])
        alphas, ebar = [], 0.0
        for i, (o_ref, do_ref) in enumerate(((o0_ref, do0_ref), (o1_ref, do1_ref), (o2_ref, do2_ref))):
            sl = slice(i * gw, (i + 1) * gw)
            alpha = es[i] * inv
            ov = o_ref[...]
            dv = dout_ref[:, sl]
            sz, dsz = _silu_and_grad(z_ref[:, sl].astype(F32))
            dz_ref[:, sl] = (dv * ov * alpha * dsz).astype(BF16)
            da = dv * sz
            do_ref[...] = da * alpha
            t = da * ov
            t1 = t.astype(BF16)
            r1 = t - t1.astype(F32)
            t2 = r1.astype(BF16)
            t3 = (r1 - t2.astype(F32)).astype(BF16)
            e = _dot(t1, ones_ref[...], NN) + _dot(t2, ones_ref[...], NN) + _dot(t3, ones_ref[...], NN)
            ebar = ebar + alpha * e
            alphas.append(alpha)
        for alpha, dv_ref in zip(alphas, (dv0_ref, dv1_ref, dv2_ref)):
            dv_ref[...] = -alpha * ebar

    row = pl.BlockSpec((tm, C), lambda i: (i, 0))
    grp = pl.BlockSpec((tm, gw), lambda i: (i, 0))
    sd = jax.ShapeDtypeStruct
    res = pl.pallas_call(
        body, name=name, grid=(T // tm,),
        in_specs=[row] + [grp] * 6 + [row, pl.BlockSpec((gw, gw), lambda i: (0, 0)), pl.BlockSpec(memory_space=pl.ANY)],
        out_specs=[pl.BlockSpec((tm, C), lambda i: (i, col // C))] + [grp] * 6,
        out_shape=[sd(dproj.shape, BF16)] + [sd((T, gw), F32)] * 6, input_output_aliases={9: 0},
        compiler_params=_params("parallel"),
    )(dout, *os, *lses, z, ones, dproj)
    return res[0], res[1:4], res[4:7]


def _mem_attn(qz, kv, name):
    T = qz.shape[0]
    dm = qz.shape[1] // 2
    M = kv.shape[0]
    hd = dm // MEM_HEADS
    scale = hd ** -0.5
    tm = _pick(T, 512, SUBLANES)

    def body(q_ref, z_ref, k_ref, v_ref, o_ref, s_ref, p_ref):
        heads = [slice(h * hd, (h + 1) * hd) for h in range(MEM_HEADS)]
        for h, sl in enumerate(heads):
            s_ref[h] = _dot(q_ref[:, sl].astype(BF16), k_ref[:, sl], NT)
        for h, sl in enumerate(heads):
            s = s_ref[h] * scale
            p = jnp.exp(s - jnp.max(s, axis=-1, keepdims=True))
            p_ref[h] = (p * (1.0 / jnp.sum(p, axis=-1, keepdims=True))).astype(BF16)
        for h, sl in enumerate(heads):
            sz, _ = _silu_and_grad(z_ref[:, sl].astype(F32))
            o_ref[:, sl] = (_dot(p_ref[h], v_ref[:, sl], NN) * sz).astype(BF16)

    return pl.pallas_call(
        body, name=name, grid=(T // tm,),
        in_specs=[pl.BlockSpec((tm, dm), lambda i: (i, 0)), pl.BlockSpec((tm, dm), lambda i: (i, 1)),
                  pl.BlockSpec((M, dm), lambda i: (0, 0)), pl.BlockSpec((M, dm), lambda i: (0, 1))],
        out_specs=pl.BlockSpec((tm, dm), lambda i: (i, 0)),
        out_shape=jax.ShapeDtypeStruct((T, dm), BF16),
        scratch_shapes=[pltpu.VMEM((MEM_HEADS, tm, M), F32), pltpu.VMEM((MEM_HEADS, tm, M), BF16)],
        compiler_params=_params("parallel"),
    )(qz, qz, kv, kv)


def _mem_attn_bwd(do, qz, kv, dproj, col, name):
    T = qz.shape[0]
    dm = qz.shape[1] // 2
    M = kv.shape[0]
    hd = dm // MEM_HEADS
    scale = hd ** -0.5
    tm = _pick(T, 512, SUBLANES)

    def body(do_ref, q_ref, z_ref, k_ref, v_ref, _, dq_ref, dz_ref, dk_ref, dv_ref, s_ref, dp_ref, p_ref, ds_ref, dob_ref):
        @pl.when(pl.program_id(0) == 0)
        def _():
            dk_ref[...] = jnp.zeros_like(dk_ref)
            dv_ref[...] = jnp.zeros_like(dv_ref)

        heads = [slice(h * hd, (h + 1) * hd) for h in range(MEM_HEADS)]
        for h, sl in enumerate(heads):
            sz, _ = _silu_and_grad(z_ref[:, sl].astype(F32))
            dob = (do_ref[:, sl] * sz).astype(BF16)
            dob_ref[:, sl] = dob
            s_ref[h] = _dot(q_ref[:, sl].astype(BF16), k_ref[:, sl], NT)
            dp_ref[h] = _dot(dob, v_ref[:, sl], NT)
        for h, sl in enumerate(heads):
            s = s_ref[h] * scale
            p = jnp.exp(s - jnp.max(s, axis=-1, keepdims=True))
            pn = p * (1.0 / jnp.sum(p, axis=-1, keepdims=True))
            dp = dp_ref[h]
            p_ref[h] = pn.astype(BF16)
            ds_ref[h] = (pn * (dp - jnp.sum(dp * pn, axis=-1, keepdims=True))).astype(BF16)
        for h, sl in enumerate(heads):
            _, dsz = _silu_and_grad(z_ref[:, sl].astype(F32))
            dz_ref[:, sl] = (do_ref[:, sl] * _dot(p_ref[h], v_ref[:, sl], NN) * dsz).astype(BF16)
            dq_ref[:, sl] = (_dot(ds_ref[h], k_ref[:, sl], NN) * scale).astype(BF16)
            dk_ref[:, sl] += _dot(ds_ref[h], q_ref[:, sl].astype(BF16), TN) * scale
            dv_ref[:, sl] += _dot(p_ref[h], dob_ref[:, sl], TN)

    rowq = pl.BlockSpec((tm, dm), lambda i: (i, 0))
    rowz = pl.BlockSpec((tm, dm), lambda i: (i, 1))
    kb = pl.BlockSpec((M, dm), lambda i: (0, 0))
    vb = pl.BlockSpec((M, dm), lambda i: (0, 1))
    sd = jax.ShapeDtypeStruct
    dq, dz, dk, dv = pl.pallas_call(
        body, name=name, grid=(T // tm,), in_specs=[rowq, rowq, rowz, kb, vb, pl.BlockSpec(memory_space=pl.ANY)],
        out_specs=[pl.BlockSpec((tm, dm), lambda i: (i, col // dm)), rowq, kb, kb],
        out_shape=[sd(dproj.shape, BF16), sd((T, dm), BF16), sd((M, dm), F32), sd((M, dm), F32)],
        scratch_shapes=[pltpu.VMEM((MEM_HEADS, tm, M), F32), pltpu.VMEM((MEM_HEADS, tm, M), F32),
                        pltpu.VMEM((MEM_HEADS, tm, M), BF16), pltpu.VMEM((MEM_HEADS, tm, M), BF16), pltpu.VMEM((tm, dm), BF16)],
        input_output_aliases={5: 0}, compiler_params=_params("arbitrary"),
    )(do, qz, qz, kv, kv, dproj)
    return dq, dz, dk, dv


def _merge(os, ws, L, logits, b_gate, name):
    T = os[0].shape[0]
    D = ws[0].shape[2]
    tm = _pick(T, 512, SUBLANES)

    def body(o0_ref, o1_ref, o2_ref, w0_ref, w1_ref, w2_ref, l_ref, b_ref, m_ref, p_ref):
        acc = 0.0
        for i, (o_ref, w_ref) in enumerate(((o0_ref, w0_ref), (o1_ref, w1_ref), (o2_ref, w2_ref))):
            sl = slice(i * D, (i + 1) * D)
            bp = _dot(o_ref[...], w_ref[...], NN)
            p_ref[i] = bp.astype(BF16)
            acc = acc + _sigmoid(l_ref[:, sl].astype(F32) + b_ref[:, sl]) * bp
        m_ref[...] = acc.astype(BF16)

    return pl.pallas_call(
        body, name=name, grid=(T // tm,),
        in_specs=[pl.BlockSpec((tm, o.shape[1]), lambda i: (i, 0)) for o in os]
        + [pl.BlockSpec((None,) + w.shape[1:], lambda i: (L, 0, 0)) for w in ws]
        + [pl.BlockSpec((tm, 3 * D), lambda i: (i, 0)), pl.BlockSpec((1, 3 * D), lambda i: (0, 0))],
        out_specs=[pl.BlockSpec((tm, D), lambda i: (i, 0)), pl.BlockSpec((3, tm, D), lambda i: (0, i, 0))],
        out_shape=[jax.ShapeDtypeStruct((T, D), BF16), jax.ShapeDtypeStruct((3, T, D), BF16)],
        compiler_params=_params("parallel"),
    )(*os, *ws, logits, b_gate.reshape(1, 3 * D))


def _merge_bwd(dmerged, bps, logits, b_gate, dproj_cols, dl_off, name):
    _, T, D = bps.shape
    tm = _pick(T, 2048, SUBLANES)
    cw = _pick(math.gcd(dl_off, D), 512, LANES)
    per = D // cw

    def body(dm_ref, p_ref, l_ref, b_ref, dl_ref, d_ref, db_ref):
        @pl.when(pl.program_id(1) == 0)
        def _():
            db_ref[...] = jnp.zeros_like(db_ref)

        dmv = dm_ref[...]
        gt = _sigmoid(l_ref[...].astype(F32) + b_ref[...])
        d_ref[...] = (dmv * gt).astype(BF16)
        dl = dmv * p_ref[...].astype(F32) * gt * (1.0 - gt)
        dl_ref[...] = dl.astype(BF16)
        db_ref[...] += jnp.sum(dl, axis=0, keepdims=True)

    stacked = pl.BlockSpec((None, tm, cw), lambda j, i: (j // per, i, j % per))
    sd = jax.ShapeDtypeStruct
    return pl.pallas_call(
        body, name=name, grid=(3 * per, T // tm),
        in_specs=[pl.BlockSpec((tm, cw), lambda j, i: (i, j % per)), stacked, pl.BlockSpec((tm, cw), lambda j, i: (i, j)),
                  pl.BlockSpec((1, cw), lambda j, i: (0, j))],
        out_specs=[pl.BlockSpec((tm, cw), lambda j, i: (i, dl_off // cw + j)), stacked, pl.BlockSpec((1, cw), lambda j, i: (0, j))],
        out_shape=[sd((T, dproj_cols), BF16), sd((3, T, D), BF16), sd((1, 3 * D), F32)],
        compiler_params=_params("parallel", "arbitrary"),
    )(dmerged, bps, logits, b_gate.reshape(1, 3 * D))


def _block_diag(w):
    nblk, ng, a, b = w.shape
    eye = jnp.eye(ng, dtype=w.dtype)
    return (w[:, :, :, None, :] * eye[None, :, None, :, None]).reshape(nblk, ng * a, ng * b)


def _block_diag_part(m, a, b):
    nblk = m.shape[0]
    ng = m.shape[1] // a
    m5 = m.reshape(nblk, ng, a, ng, b)
    eye = jnp.eye(ng, dtype=m.dtype)
    return jnp.sum(m5 * eye[None, :, None, :, None], axis=3)


def _ssm_matrices(p, L, tag):
    G, P = p["ssm_lambda_re"].shape[1:]
    Hg = SSM_GROUP
    gpb = SSM_BLOCK_CH // Hg
    nblk = G // gpb
    br = p["ssm_b_re"][L].transpose(2, 0, 1)
    bi = p["ssm_b_im"][L].transpose(2, 0, 1)
    disc_in = (p["ssm_lambda_re"][L], p["ssm_lambda_im"][L], p["ssm_log_dt"][L].reshape(G, 1), br, bi)
    ar, ai, bbr, bbi = _ssm_disc(*disc_in, name=f"ssm_disc_{tag}")
    amat = (ar.reshape(nblk // 2, SUBLANES, LANES), ai.reshape(nblk // 2, SUBLANES, LANES))
    bbr_g = bbr.transpose(1, 0, 2).reshape(nblk, gpb, Hg, P)
    bbi_g = bbi.transpose(1, 0, 2).reshape(nblk, gpb, Hg, P)
    bmat = jnp.concatenate([_block_diag(bbr_g), _block_diag(bbi_g)], axis=2).astype(BF16)
    cre = p["ssm_c_re"][L].reshape(nblk, gpb, Hg, P).transpose(0, 1, 3, 2)
    cim = p["ssm_c_im"][L].reshape(nblk, gpb, Hg, P).transpose(0, 1, 3, 2)
    cmat = jnp.concatenate([_block_diag(cre), -_block_diag(cim)], axis=1).astype(BF16)
    return disc_in, amat, bmat, cmat


def _layer_fwd(x, mem, p, wb, L, biasm):
    T, D = x.shape
    C = p["ssm_d"].shape[1]
    dm = wb["w_br_mem"].shape[1]
    tag = f"l{L}"
    s = {"x": x}
    h = _rmsnorm(x, p["norm_g"][L], f"norm_{tag}")
    offs = [int(o) for o in np.cumsum([0, C, C, 3 * 768, 768, 2 * dm, 3 * D])]
    names = ("uz", "z_ssm", "qkv", "z_attn", "qz_mem", "logits")
    dts = (F32, BF16, F32, BF16, BF16, BF16)
    for i, (nm, dt) in enumerate(zip(names, dts)):
        tiles = dict(tm=2048) if offs[i + 1] - offs[i] >= 1024 and dt == BF16 else {}
        s[nm] = _matmul(h, wb["w_in"], mode="nn", name=f"in_{nm}_{tag}", out_dtype=dt, b_lead=L, b_off=offs[i],
                        n_cols=offs[i + 1] - offs[i], **tiles)
    s["h"] = h

    disc_in, amat, bmat, cmat = _ssm_matrices(p, L, tag)
    dvec = p["ssm_d"][L].reshape(1, C)
    y, xr, xi = _ssm_fwd(s["uz"], bmat, cmat, *amat, dvec, f"ssm_scan_{tag}")
    o_ssm, a_glu = _ssm_post(y, s["z_ssm"], wb["w_glu"][L], p["b_glu"][L], f"ssm_post_{tag}")
    s.update(disc_in=disc_in, amat=amat, bmat=bmat, cmat=cmat, xr=xr, xi=xi, y=y, a_glu=a_glu, o_ssm=o_ssm)

    groups = [_attn_group_fwd(s["qkv"], biasm, g, f"attn_g{g}_{tag}") for g in range(len(ATTN_CONFIGS))]
    os, lses = [o for o, _ in groups], [l for _, l in groups]
    o_attn = _attn_mix(os, lses, s["z_attn"], f"attn_mix_{tag}")
    s.update(os=os, lses=lses, o_attn=o_attn)

    mn = _rmsnorm(mem, p["mem_norm_g"][L], f"mem_norm_{tag}")
    kv = _matmul(mn, wb["w_mem_kv"], mode="nn", name=f"mem_kv_{tag}", out_dtype=BF16, b_lead=L)
    o_mem = _mem_attn(s["qz_mem"], kv, f"mem_attn_{tag}")
    s.update(mn=mn, kv=kv, o_mem=o_mem)

    merged, bps = _merge([o_ssm, o_attn, o_mem], [wb["w_br_ssm"], wb["w_br_attn"], wb["w_br_mem"]], L, s["logits"],
                         p["b_gate"][L], f"merge_{tag}")
    s.update(bps=bps, merged=merged)
    x_new = _matmul(merged, wb["w_out"], mode="nn", name=f"out_{tag}", add=x, b_lead=L)
    return x_new, s


def _layer_bwd(dx, mem, p, wb, L, s, biasm, gprev):
    T, D = dx.shape
    C = p["ssm_d"].shape[1]
    depth = p["norm_g"].shape[0]
    tag = f"l{L}"
    g = {}

    def wgrad(n, a, b, **tiles):
        g[n] = _matmul(a, b, mode="tn", name=f"d{n}_{tag}", out_dtype=BF16, stack=(L, depth, gprev.get(n)), **tiles)

    dmerged = _matmul(dx, wb["w_out"], mode="nt", name=f"d_merged_{tag}", b_lead=L)
    wgrad("w_out", s["merged"], dx)
    dm = s["qz_mem"].shape[1] // 2
    col = dict(zip(("u", "z_ssm", "q", "k", "v", "z_attn", "q_mem", "z_mem", "logits", "end"),
                   (int(o) for o in np.cumsum([0, C, C, 768, 768, 768, 768, dm, dm, 3 * D]))))
    dproj, dbps, g["b_gate"] = _merge_bwd(dmerged, s["bps"], s["logits"], p["b_gate"][L], col["end"], col["logits"],
                                          f"merge_bwd_{tag}")
    dos = []
    for i, (o, n) in enumerate(((s["o_ssm"], "w_br_ssm"), (s["o_attn"], "w_br_attn"), (s["o_mem"], "w_br_mem"))):
        dos.append(_matmul(dbps, wb[n], mode="nt", name=f"d_o_{n}_{tag}", a_lead=i, b_lead=L))
        g[n] = _matmul(o, dbps, mode="tn", name=f"d{n}_{tag}", out_dtype=BF16, b_lead=i, stack=(L, depth, gprev.get(n)))

    dy, dproj, ds_glu, g["b_glu"] = _ssm_post_bwd(dos[0], s["y"], s["z_ssm"], wb["w_glu"][L], p["b_glu"][L], dproj,
                                                  col["z_ssm"], f"ssm_post_bwd_{tag}")
    wgrad("w_glu", s["a_glu"], ds_glu)
    dvec = p["ssm_d"][L].reshape(1, C)
    dproj, dbm, dct, dar, dai, g["ssm_d"] = _ssm_bwd(dy, s["uz"], s["xr"], s["xi"], s["bmat"], s["cmat"], *s["amat"], dvec,
                                                     dproj, f"ssm_scan_bwd_{tag}")
    G, P = p["ssm_lambda_re"].shape[1:]
    Hg = SSM_GROUP
    half = dbm.shape[2] // 2
    dbbr = _block_diag_part(dbm[:, :, :half], Hg, P).reshape(G, Hg, P).transpose(1, 0, 2)
    dbbi = _block_diag_part(dbm[:, :, half:], Hg, P).reshape(G, Hg, P).transpose(1, 0, 2)
    g["ssm_c_re"] = _block_diag_part(dct[:, :, :half], Hg, P).reshape(G, Hg, P)
    g["ssm_c_im"] = -_block_diag_part(dct[:, :, half:], Hg, P).reshape(G, Hg, P)
    glre, glim, gdt, gbr, gbi = _ssm_disc_bwd(*s["disc_in"], dar.reshape(G, P), dai.reshape(G, P), dbbr, dbbi,
                                              name=f"ssm_disc_bwd_{tag}")
    g["ssm_lambda_re"], g["ssm_lambda_im"], g["ssm_log_dt"] = glre, glim, gdt.reshape(G)
    g["ssm_b_re"] = gbr.transpose(1, 2, 0)
    g["ssm_b_im"] = gbi.transpose(1, 2, 0)

    dproj, do_g, dvec_g = _attn_mix_bwd(dos[1], s["os"], s["lses"], s["z_attn"], dproj, col["z_attn"], f"attn_mix_bwd_{tag}")
    rest, dbias = [], []
    for gi in range(len(ATTN_CONFIGS)):
        dq_g, dproj, dv_g, db_g = _attn_group_bwd(s["qkv"], do_g[gi], dvec_g[gi], s["lses"][gi], biasm, gi, dproj, col["k"],
                                                  f"attn_bwd_g{gi}_{tag}")
        gw = dq_g.shape[1]
        rest += [(dq_g, col["q"] + gi * gw), (dv_g, col["v"] + gi * gw)]
        dbias.append(db_g)
    dbias = jnp.stack(dbias)

    dproj, dz_mem, dk_mem, dv_mem = _mem_attn_bwd(dos[2], s["qz_mem"], s["kv"], dproj, col["q_mem"], f"mem_attn_bwd_{tag}")
    rest.append((dz_mem, col["z_mem"]))
    for piece, at in rest:
        dproj = lax.dynamic_update_slice(dproj, piece.astype(BF16), (0, at))
    dkv = jnp.concatenate([dk_mem, dv_mem], axis=1)
    wgrad("w_mem_kv", s["mn"], dkv)
    dmn = _matmul(dkv, wb["w_mem_kv"], mode="nt", name=f"d_mn_{tag}", b_lead=L)
    _, g["mem_norm_g"] = _rmsnorm_bwd(mem, p["mem_norm_g"][L], dmn, None, f"mem_norm_bwd_{tag}")

    dh = _matmul(dproj, wb["w_in"], mode="nt", name=f"d_h_{tag}", b_lead=L, tm=512, tn=1024)
    wgrad("w_in", s["h"], dproj, tn=2304, tk=1024)
    dx_in, g["norm_g"] = _rmsnorm_bwd(s["x"], p["norm_g"][L], dh, dx, f"norm_bwd_{tag}")
    return dx_in, g, dbias


def _bucket_onehot(gi):
    buckets, bands = _band_tables()
    hit = (buckets[gi].reshape(1, -1) == jnp.arange(NUM_BUCKETS)[:, None]) & bands[gi].reshape(1, -1)
    return hit.astype(BF16)


def _bias_tables(rel_bias, name):
    _, bands = _band_tables()
    out = []
    for gi in range(len(ATTN_CONFIGS)):
        tab = rel_bias[:, gi * HEADS_PER_GROUP:(gi + 1) * HEADS_PER_GROUP].T
        flat = _matmul(tab, _bucket_onehot(gi), mode="nn", name=f"{name}_{gi}", split_a=3, tn=4096)
        out.append(jnp.where(bands[gi][None], flat.reshape(HEADS_PER_GROUP, ATTN_BLOCK, 2 * ATTN_BLOCK), NEG_INF))
    return jnp.stack(out)


def _rel_bias_grad(dbias_sum, name):
    cols = []
    for gi in range(len(ATTN_CONFIGS)):
        flat = dbias_sum[gi].reshape(HEADS_PER_GROUP, -1)
        cols.append(_matmul(flat, _bucket_onehot(gi), mode="nt", name=f"{name}_{gi}", split_a=2, tk=4096).T)
    return jnp.concatenate(cols, axis=1)


def _local_step(x, mem, target, p, wb):
    depth = p["norm_g"].shape[0]
    biasm = _bias_tables(p["rel_bias"], "bias_table")
    saved = []
    for L in range(depth):
        x, s = _layer_fwd(x, mem, p, wb, L, biasm)
        saved.append(s)
    loss_vec, dx, dgf = _loss_head(x, p["final_norm_g"], target, "loss_head")
    grads = {"final_norm_g": dgf.reshape(-1)}
    per_layer = [None] * depth
    dbias_sum = 0.0
    stacked = {}
    for L in reversed(range(depth)):
        dx, per_layer[L], dbias = _layer_bwd(dx, mem, p, wb, L, saved[L], biasm, stacked)
        stacked = {n: per_layer[L][n] for n, _ in BIG}
        dbias_sum = dbias_sum + dbias
    grads.update(stacked)
    for n in per_layer[0]:
        if n not in stacked:
            grads[n] = jnp.stack([per_layer[L][n].reshape(p[n].shape[1:]) for L in range(depth)])
    grads["rel_bias"] = _rel_bias_grad(dbias_sum, "d_rel_bias")
    return jnp.sum(loss_vec), dx, grads


def _chip_coords(j):
    return j // 2, j % 2


def _place_shard(shard, ax, chip, name):
    _, a, b = shard.shape
    ra = _pick(a, 256, 16)
    full = (2, a * N_CHIPS, b) if ax == 1 else (2, a, b * N_CHIPS)
    per = a // ra

    def body(j_ref, s_ref, o_ref):
        o_ref[...] = s_ref[...].astype(BF16)

    out_idx = (lambda l, i, j: (l, j[0] * per + i, 0)) if ax == 1 else (lambda l, i, j: (l, i, j[0]))
    return pl.pallas_call(
        body, name=name,
        grid_spec=pltpu.PrefetchScalarGridSpec(
            num_scalar_prefetch=1, grid=(2, per),
            in_specs=[pl.BlockSpec((None, ra, b), lambda l, i, j: (l, i, 0))],
            out_specs=pl.BlockSpec((None, ra, b), out_idx)),
        out_shape=jax.ShapeDtypeStruct(full, BF16), compiler_params=_params("parallel", "parallel"),
    )(chip, shard)


def _gather_shards(fulls, axes, name):
    n = len(fulls)
    widths = [a.shape[ax] // N_CHIPS for a, ax in zip(fulls, axes)]
    aligns = [LANES if ax == 2 else 16 for ax in axes]

    def body(*refs):
        outs = refs[n:2 * n]
        send_sems, recv_sems, fsend_sems, frecv_sems = refs[2 * n:]
        x, y, c = lax.axis_index("x"), lax.axis_index("y"), lax.axis_index("c")
        mine = 2 * x + y
        sibling = (x, y, 1 - c)

        def window(t, layer, j):
            start = pl.ds(pl.multiple_of(j * widths[t], aligns[t]), widths[t])
            return outs[t].at[(layer, start, slice(None)) if axes[t] == 1 else (layer, slice(None), start)]

        def over_ici(t, j, block):
            return pltpu.make_async_remote_copy(
                src_ref=window(t, c, mine), dst_ref=window(t, c, block), send_sem=send_sems.at[t, j],
                recv_sem=recv_sems.at[t, block], device_id=(*_chip_coords(j), c), device_id_type=MESH)

        def over_d2d(t, j, layer):
            return pltpu.make_async_remote_copy(
                src_ref=window(t, layer, j), dst_ref=window(t, layer, j), send_sem=fsend_sems.at[t, j],
                recv_sem=frecv_sems.at[t, j], device_id=sibling, device_id_type=MESH)

        for t in range(n):
            for j in range(N_CHIPS):
                @pl.when(j != mine)
                def _():
                    over_ici(t, j, mine).start()
        for t in range(n):
            for j in range(N_CHIPS):
                @pl.when(j != mine)
                def _():
                    over_ici(t, j, j).wait_recv()
                    over_d2d(t, j, c).start()
        for t in range(n):
            for j in range(N_CHIPS):
                @pl.when(j != mine)
                def _():
                    over_ici(t, j, mine).wait_send()
                    over_d2d(t, j, c).wait_send()
                    over_d2d(t, j, 1 - c).wait_recv()

    sem = pltpu.SemaphoreType.DMA
    return pl.pallas_call(
        body, name=name, in_specs=[HBM] * n, out_specs=[HBM] * n,
        out_shape=[jax.ShapeDtypeStruct(a.shape, a.dtype) for a in fulls],
        input_output_aliases={t: t for t in range(n)},
        scratch_shapes=[sem((n, N_CHIPS)), sem((n, N_CHIPS)), sem((n, N_CHIPS)), sem((n, N_CHIPS))],
    )(*fulls)


def _scatter_slices(arrays, axes, name):
    n = len(arrays)

    def piece(a, ax):
        if ax is None:
            return a.shape, None
        w = a.shape[ax] // N_CHIPS
        return a.shape[:ax] + (w,) + a.shape[ax + 1:], w

    shapes = [piece(a, ax) for a, ax in zip(arrays, axes)]

    def body(*refs):
        ins, outs = refs[:n], refs[n:2 * n]
        send_sems, recv_sems, loc_sems = refs[2 * n:]
        x, y, c = lax.axis_index("x"), lax.axis_index("y"), lax.axis_index("c")
        mine = 2 * x + y

        def src(t, j):
            ax, w = axes[t], shapes[t][1]
            if ax is None:
                return ins[t]
            idx = tuple(pl.ds(j * w, w) if d == ax else slice(None) for d in range(len(arrays[t].shape)))
            return ins[t].at[idx]

        for t in range(n):
            for j in range(N_CHIPS):
                @pl.when(j == mine)
                def _():
                    pltpu.make_async_copy(src(t, j), outs[t].at[j], loc_sems.at[t]).start()

                @pl.when(j != mine)
                def _():
                    pltpu.make_async_remote_copy(
                        src_ref=src(t, j), dst_ref=outs[t].at[mine], send_sem=send_sems.at[t, j], recv_sem=recv_sems.at[t, mine],
                        device_id=(*_chip_coords(j), c), device_id_type=MESH).start()
        for t in range(n):
            for j in range(N_CHIPS):
                @pl.when(j == mine)
                def _():
                    pltpu.make_async_copy(src(t, j), outs[t].at[j], loc_sems.at[t]).wait()

                @pl.when(j != mine)
                def _():
                    cp = pltpu.make_async_remote_copy(
                        src_ref=src(t, j), dst_ref=outs[t].at[j], send_sem=send_sems.at[t, j], recv_sem=recv_sems.at[t, j],
                        device_id=(*_chip_coords(j), c), device_id_type=MESH)
                    cp.wait_send()
                    cp.wait_recv()

    return pl.pallas_call(
        body, name=name, in_specs=[HBM] * n, out_specs=[HBM] * n,
        out_shape=[jax.ShapeDtypeStruct((N_CHIPS,) + sh, a.dtype) for a, (sh, _) in zip(arrays, shapes)],
        scratch_shapes=[pltpu.SemaphoreType.DMA((n, N_CHIPS)), pltpu.SemaphoreType.DMA((n, N_CHIPS)), pltpu.SemaphoreType.DMA((n,))],
    )(*arrays)


def _swap_layers(stacked, name):
    n = len(stacked)

    def body(*refs):
        ins, outs = refs[:n], refs[n:2 * n]
        send_sems, recv_sems = refs[2 * n:]
        c = lax.axis_index("c")
        peer = (lax.axis_index("x"), lax.axis_index("y"), 1 - c)
        cps = [pltpu.make_async_remote_copy(src_ref=ins[t].at[1 - c], dst_ref=outs[t], send_sem=send_sems.at[t],
                                            recv_sem=recv_sems.at[t], device_id=peer, device_id_type=MESH) for t in range(n)]
        for cp in cps:
            cp.start()
        for cp in cps:
            cp.wait_send()
            cp.wait_recv()

    return pl.pallas_call(
        body, name=name, in_specs=[HBM] * n, out_specs=[HBM] * n,
        out_shape=[jax.ShapeDtypeStruct(a.shape[1:], a.dtype) for a in stacked],
        scratch_shapes=[pltpu.SemaphoreType.DMA((n,)), pltpu.SemaphoreType.DMA((n,))],
    )(*stacked)


def _merge_layers(stacked, name):
    n = len(stacked)

    def body(*refs):
        outs = refs[n:2 * n]
        send_sems, recv_sems = refs[2 * n:]
        c = lax.axis_index("c")
        peer = (lax.axis_index("x"), lax.axis_index("y"), 1 - c)
        for t in range(n):
            pltpu.make_async_remote_copy(src_ref=outs[t].at[c], dst_ref=outs[t].at[c], send_sem=send_sems.at[t],
                                         recv_sem=recv_sems.at[t], device_id=peer, device_id_type=MESH).start()
        for t in range(n):
            cp = pltpu.make_async_remote_copy(src_ref=outs[t].at[c], dst_ref=outs[t].at[1 - c], send_sem=send_sems.at[t],
                                              recv_sem=recv_sems.at[t], device_id=peer, device_id_type=MESH)
            cp.wait_send()
            cp.wait_recv()

    sem = pltpu.SemaphoreType.DMA
    return pl.pallas_call(
        body, name=name, in_specs=[HBM] * n, out_specs=[HBM] * n,
        out_shape=[jax.ShapeDtypeStruct(a.shape, a.dtype) for a in stacked],
        input_output_aliases={t: t for t in range(n)}, scratch_shapes=[sem((n,)), sem((n,))],
    )(*stacked)


def _pair_sum(stacked, landed, core, name):
    _, K, N = stacked.shape
    tr = _pick(K, max(16, (1 << 19) // N // 16 * 16), 16)

    def body(c_ref, s_ref, l_ref, o_ref):
        o_ref[...] = (s_ref[...].astype(F32) + l_ref[...].astype(F32)).astype(o_ref.dtype)

    return pl.pallas_call(
        body, name=name,
        grid_spec=pltpu.PrefetchScalarGridSpec(
            num_scalar_prefetch=1, grid=(K // tr,),
            in_specs=[pl.BlockSpec((None, tr, N), lambda i, c: (c[0], i, 0)), pl.BlockSpec((tr, N), lambda i, c: (i, 0))],
            out_specs=pl.BlockSpec((tr, N), lambda i, c: (i, 0))),
        out_shape=jax.ShapeDtypeStruct((K, N), stacked.dtype), compiler_params=_params("parallel"),
    )(core, stacked, landed)


def _sum_chips(landed, core, name):
    _, R, C = landed.shape
    tr = _pick(R, max(SUBLANES, (1 << 19) // C // 16 * 16), 16)

    def body(c_ref, l_ref, o_ref):
        acc = l_ref[0].astype(F32) + l_ref[1].astype(F32)
        acc = acc + l_ref[2].astype(F32)
        o_ref[...] = acc + l_ref[3].astype(F32)

    return pl.pallas_call(
        body, name=name,
        grid_spec=pltpu.PrefetchScalarGridSpec(
            num_scalar_prefetch=1, grid=(R // tr,),
            in_specs=[pl.BlockSpec((N_CHIPS, tr, C), lambda i, c: (0, i, 0))],
            out_specs=pl.BlockSpec((None, tr, C), lambda i, c: (c[0], i, 0))),
        out_shape=jax.ShapeDtypeStruct((2, R, C), F32), compiler_params=_params("parallel"),
    )(core, landed)


def _adamw_math(w_ref, g_ref, m_ref, v_ref, d_ref, nm_ref, nv_ref):
    c1 = 1.0 / (1.0 - ADAM_B1 ** ADAM_STEP)
    c2 = 1.0 / (1.0 - ADAM_B2 ** ADAM_STEP)
    g = g_ref[...]
    nm = ADAM_B1 * m_ref[...] + (1.0 - ADAM_B1) * g
    nv = ADAM_B2 * v_ref[...] + (1.0 - ADAM_B2) * (g * g)
    nm_ref[...] = nm
    nv_ref[...] = nv
    d_ref[...] = -ADAM_LR * ((nm * c1) / (jnp.sqrt(nv * c2) + ADAM_EPS) + ADAM_WD * w_ref[...])


def _adamw_whole(w, g, m, v, name):
    shape = w.shape
    view = (-1,) + shape[-2:] if w.ndim >= 2 else (1, 1, -1)

    def body(*refs):
        _adamw_math(*refs)

    res = pl.pallas_call(body, name=name, out_shape=[jax.ShapeDtypeStruct(w.reshape(view).shape, F32)] * 3,
                         compiler_params=pltpu.CompilerParams(vmem_limit_bytes=VMEM_LIMIT_BYTES))(
        *(a.reshape(view) for a in (w, g, m, v)))
    return [r.reshape(shape) for r in res]


def _adamw(w, g, m, v, name):
    R, C = w.shape
    tr = _pick(R, max(SUBLANES, (1 << 18) // C // 8 * 8), SUBLANES)

    def body(*refs):
        _adamw_math(*refs)

    blk = pl.BlockSpec((tr, C), lambda i: (i, 0))
    return pl.pallas_call(
        body, name=name, grid=(R // tr,), in_specs=[blk] * 4, out_specs=[blk] * 3,
        out_shape=[jax.ShapeDtypeStruct((R, C), F32)] * 3, compiler_params=_params("parallel"),
    )(w, g, m, v)


def _pack_small(d, prefix=""):
    flat = jnp.concatenate([d[prefix + n].astype(F32).reshape(-1) for n in SMALL])
    pad = (-flat.shape[0]) % (2 * 16 * LANES)
    return jnp.pad(flat, (0, pad)).reshape(-1, LANES)


def _unpack_small(packed, shapes):
    flat = packed.reshape(-1)
    out, off = {}, 0
    for n in SMALL:
        size = int(np.prod(shapes[n]))
        out[n] = flat[off:off + size].reshape(shapes[n])
        off += size
    return out


def kernel(*args):
    p = dict(zip(INPUTS, args))
    x, mem, target = p["x"][0], p["mem"][0], p["loss_target"][0]

    names = [n for n, _ in BIG] + ["small"]
    core = lax.axis_index("c").astype(jnp.int32).reshape(1)
    chip = (2 * lax.axis_index("x") + lax.axis_index("y")).astype(jnp.int32).reshape(1)
    placed = [_place_shard(p[n], ax, chip, f"place_{n}") for n, ax in BIG]
    wb = dict(zip(names, _gather_shards(placed, [ax for _, ax in BIG], "gather_weights")))

    loss_part, dx, grads = _local_step(x, mem, target, p, wb)
    loss = lax.psum(loss_part, ("x", "y", "c"))

    stacked = [grads[n] for n, _ in BIG] + [_pack_small(grads).reshape(2, -1, LANES)]
    theirs = _swap_layers(stacked, "swap_layers")
    pair = [_pair_sum(s, o, core, f"pair_sum_{n}") for n, s, o in zip(names, stacked, theirs)]
    landed = _scatter_slices(pair, [ax - 1 for _, ax in BIG] + [None], "scatter_grads")
    reduced = [_sum_chips(ld.reshape(N_CHIPS, -1, ld.shape[-1]), core, f"sum_chips_{n}") for n, ld in zip(names, landed)]
    total = _merge_layers(reduced, "merge_layers")

    out = {}
    for (n, _), g in zip(BIG, total):
        sh = p[n].shape
        two_d = lambda a: a.reshape(-1, sh[-1])
        res = (g,) + tuple(_adamw(two_d(p[n]), two_d(g), two_d(p["m_" + n]), two_d(p["v_" + n]), f"adamw_{n}"))
        for key, r in zip(("grad_", "delta_", "new_m_", "new_v_"), res):
            out[key + n] = r.reshape(sh)
    for n, g in _unpack_small(total[-1], {n: p[n].shape for n in SMALL}).items():
        res = (g,) + tuple(_adamw_whole(p[n], g, p["m_" + n], p["v_" + n], f"adamw_{n}"))
        for key, r in zip(("grad_", "delta_", "new_m_", "new_v_"), res):
            out[key + n] = r

    result = [loss, dx.reshape(p["x"].shape)]
    for key in ("grad_", "delta_", "new_m_", "new_v_"):
        result += [out[key + n] for n in WEIGHTS]
    return tuple(result)
```

```python
import math

import jax
import jax.numpy as jnp
import numpy as np
from jax import lax
from jax.experimental import pallas as pl
from jax.experimental.pallas import tpu as pltpu

F32 = jnp.float32
BF16 = jnp.bfloat16
MESH = pl.DeviceIdType.MESH
HBM = pl.BlockSpec(memory_space=pltpu.HBM)

EPS = 1e-6
SSM_GROUP = 16
SSM_STATE = 64
ATTN_HEAD_DIM = 64
HEADS_PER_GROUP = 4
ATTN_CONFIGS = ((128, 1), (512, 4), (2048, 16))
ATTN_BLOCK = 128
NUM_BUCKETS = 32
REL_MAX_DISTANCE = 2048
NEG_INF = -1e30
MEM_HEADS = 4
ADAM_LR = 0.001
ADAM_B1 = 0.9
ADAM_B2 = 0.999
ADAM_EPS = 1e-08
ADAM_WD = 0.01
ADAM_STEP = 10

LANES = 128
SUBLANES = 8
VMEM_LIMIT_BYTES = 48 * 1024 * 1024
SSM_BLOCK_CH = 128

N_CHIPS = 4
BIG = (("w_in", 2), ("w_glu", 1), ("w_mem_kv", 1), ("w_br_ssm", 2), ("w_br_attn", 2), ("w_br_mem", 2), ("w_out", 1))
SMALL = ("norm_g", "mem_norm_g", "b_gate", "ssm_lambda_re", "ssm_lambda_im", "ssm_log_dt", "ssm_b_re", "ssm_b_im",
         "ssm_c_re", "ssm_c_im", "ssm_d", "b_glu", "rel_bias", "final_norm_g")
WEIGHTS = ("norm_g", "mem_norm_g", "w_in", "b_gate", "ssm_lambda_re", "ssm_lambda_im", "ssm_log_dt", "ssm_b_re",
           "ssm_b_im", "ssm_c_re", "ssm_c_im", "ssm_d", "w_glu", "b_glu", "w_mem_kv", "w_br_ssm", "w_br_attn",
           "w_br_mem", "w_out", "rel_bias", "final_norm_g")
INPUTS = ("x", "mem") + WEIGHTS + ("loss_target",) + tuple("m_" + n for n in WEIGHTS) + tuple("v_" + n for n in WEIGHTS)


def _params(*sem):
    return pltpu.CompilerParams(dimension_semantics=sem, vmem_limit_bytes=VMEM_LIMIT_BYTES)


def _pick(dim, pref, align):
    if dim <= pref:
        return dim
    t = pref - pref % align
    while t >= align:
        if dim % t == 0:
            return t
        t -= align
    return dim


def _sigmoid(v):
    return 0.5 * jnp.tanh(0.5 * v) + 0.5


def _silu_and_grad(z):
    s = _sigmoid(z)
    return z * s, s * (1.0 + z * (1.0 - s))


_GELU_C = math.sqrt(2.0 / math.pi)


def _gelu_and_grad(y):
    inner = _GELU_C * (y + 0.044715 * y * y * y)
    t = jnp.tanh(inner)
    g = 0.5 * y * (1.0 + t)
    dg = 0.5 * (1.0 + t) + 0.5 * y * (1.0 - t * t) * _GELU_C * (1.0 + 3.0 * 0.044715 * y * y)
    return g, dg


def _dot(a, b, dims):
    return lax.dot_general(a, b, (dims, ((), ())), preferred_element_type=F32)


NN = ((1,), (0,))
NT = ((1,), (1,))
TN = ((0,), (0,))


def _matmul(a, b, *, mode, name, out_dtype=F32, add=None, split_a=1, tm=1024, tn=768, tk=2304,
            a_lead=None, b_lead=None, b_off=0, n_cols=None, stack=None):
    ashape = a.shape if a_lead is None else a.shape[1:]
    K, M = ashape if mode == "tn" else ashape[::-1]
    bshape = b.shape if b_lead is None else b.shape[1:]
    N = n_cols or (bshape[0] if mode == "nt" else bshape[1])
    if mode != "tn" and M >= 4 * tm:
        tm = 2 * tm
    tm = _pick(M, tm, LANES if mode == "tn" else SUBLANES)
    tn = _pick(math.gcd(N, b_off) if b_off else N, tn, LANES)
    tk = _pick(K, tk, LANES)
    nk = K // tk
    joff = b_off // tn
    dims = {"nn": NN, "nt": NT, "tn": TN}[mode]
    has_add = add is not None
    has_prev = stack is not None and stack[2] is not None

    def body(*refs):
        a_ref, b_ref = refs[:2]
        add_ref = refs[2] if has_add else None
        o_ref = refs[-2] if nk > 1 else refs[-1]
        k = pl.program_id(2)
        bv = b_ref[...].astype(BF16)
        if split_a > 1:
            rest = a_ref[...].astype(F32)
            part = 0.0
            for _ in range(split_a):
                piece = rest.astype(BF16)
                part = part + _dot(piece, bv, dims)
                rest = rest - piece.astype(F32)
        else:
            part = _dot(a_ref[...].astype(BF16), bv, dims)

        def finish(r):
            if has_add:
                r = r + add_ref[...]
            o_ref[...] = r.astype(out_dtype)

        if nk == 1:
            finish(part)
            return
        acc_ref = refs[-1]

        @pl.when(k == 0)
        def _():
            acc_ref[...] = part

        @pl.when((k > 0) & (k < nk - 1))
        def _():
            acc_ref[...] += part

        @pl.when(k == nk - 1)
        def _():
            finish(acc_ref[...] + part)

    alead = () if a_lead is None else (a_lead,)
    alead_blk = () if a_lead is None else (None,)
    if mode == "tn":
        a_spec = pl.BlockSpec(alead_blk + (tk, tm), lambda i, j, k: alead + (k, i))
    else:
        a_spec = pl.BlockSpec(alead_blk + (tm, tk), lambda i, j, k: alead + (i, k))
    lead = () if b_lead is None else (b_lead,)
    lead_blk = () if b_lead is None else (None,)
    if mode == "nt":
        b_spec = pl.BlockSpec(lead_blk + (tn, tk), lambda i, j, k: lead + (j + joff, k))
    else:
        b_spec = pl.BlockSpec(lead_blk + (tk, tn), lambda i, j, k: lead + (k, j + joff))
    in_specs = [a_spec, b_spec]
    args = [a, b]
    if has_add:
        in_specs.append(pl.BlockSpec((tm, tn), lambda i, j, k: (i, j)))
        args.append(add)
    aliases = {}
    if stack is None:
        out_spec = pl.BlockSpec((tm, tn), lambda i, j, k: (i, j))
        out_shape = jax.ShapeDtypeStruct((M, N), out_dtype)
    else:
        layer, depth, prev = stack
        out_spec = pl.BlockSpec((None, tm, tn), lambda i, j, k: (layer, i, j))
        out_shape = jax.ShapeDtypeStruct((depth, M, N), out_dtype)
        if has_prev:
            in_specs.append(pl.BlockSpec(memory_space=pl.ANY))
            args.append(prev)
            aliases = {len(args) - 1: 0}
    return pl.pallas_call(
        body, name=name, grid=(M // tm, N // tn, nk), in_specs=in_specs, out_specs=out_spec, out_shape=out_shape,
        scratch_shapes=[pltpu.VMEM((tm, tn), F32)] if nk > 1 else [], input_output_aliases=aliases,
        compiler_params=_params("parallel", "parallel", "arbitrary"),
    )(*args)


def _rmsnorm(x, g, name):
    T, D = x.shape
    tm = _pick(T, 512, SUBLANES)

    def body(x_ref, g_ref, h_ref):
        xv = x_ref[...]
        r = lax.rsqrt(jnp.mean(xv * xv, axis=-1, keepdims=True) + EPS)
        h_ref[...] = (xv * r * g_ref[...]).astype(BF16)

    return pl.pallas_call(
        body, name=name, grid=(T // tm,),
        in_specs=[pl.BlockSpec((tm, D), lambda i: (i, 0)), pl.BlockSpec((1, D), lambda i: (0, 0))],
        out_specs=pl.BlockSpec((tm, D), lambda i: (i, 0)),
        out_shape=jax.ShapeDtypeStruct((T, D), BF16), compiler_params=_params("parallel"),
    )(x, g.reshape(1, D))


def _rmsnorm_bwd(x, g, dh, dres, name):
    T, D = x.shape
    tm = _pick(T, 512, SUBLANES)
    with_res = dres is not None

    def body(*refs):
        if with_res:
            x_ref, g_ref, dh_ref, dres_ref, dx_ref, dg_ref = refs
        else:
            x_ref, g_ref, dh_ref, dx_ref, dg_ref = refs
        xv = x_ref[...]
        dhv = dh_ref[...]
        r = lax.rsqrt(jnp.mean(xv * xv, axis=-1, keepdims=True) + EPS)
        dyg = dhv * g_ref[...]
        c = jnp.mean(dyg * xv, axis=-1, keepdims=True)
        dx = r * dyg - xv * (r * r * r) * c
        if with_res:
            dx = dx + dres_ref[...]
        dx_ref[...] = dx

        @pl.when(pl.program_id(0) == 0)
        def _():
            dg_ref[...] = jnp.zeros_like(dg_ref)

        dg_ref[...] += jnp.sum(dhv * xv * r, axis=0, keepdims=True)

    row = pl.BlockSpec((tm, D), lambda i: (i, 0))
    vec = pl.BlockSpec((1, D), lambda i: (0, 0))
    ins = [x, g.reshape(1, D), dh] + ([dres] if with_res else [])
    return pl.pallas_call(
        body, name=name, grid=(T // tm,), in_specs=[row, vec, row] + ([row] if with_res else []),
        out_specs=[row, vec],
        out_shape=[jax.ShapeDtypeStruct((T, D), F32), jax.ShapeDtypeStruct((1, D), F32)],
        compiler_params=_params("arbitrary"),
    )(*ins)


def _loss_head(x, g, target, name):
    T, D = x.shape
    tm = _pick(T, 512, SUBLANES)

    def body(x_ref, g_ref, t_ref, loss_ref, dx_ref, dg_ref):
        xv = x_ref[...]
        gv = g_ref[...]
        r = lax.rsqrt(jnp.mean(xv * xv, axis=-1, keepdims=True) + EPS)
        e = xv * r * gv - t_ref[...]
        dy = e * (1.0 / D)
        dyg = dy * gv
        c = jnp.mean(dyg * xv, axis=-1, keepdims=True)
        dx_ref[...] = r * dyg - xv * (r * r * r) * c

        @pl.when(pl.program_id(0) == 0)
        def _():
            loss_ref[...] = jnp.zeros_like(loss_ref)
            dg_ref[...] = jnp.zeros_like(dg_ref)

        loss_ref[...] += jnp.sum(e * e, axis=0, keepdims=True) * (0.5 / D)
        dg_ref[...] += jnp.sum(dy * xv * r, axis=0, keepdims=True)

    row = pl.BlockSpec((tm, D), lambda i: (i, 0))
    vec = pl.BlockSpec((1, D), lambda i: (0, 0))
    return pl.pallas_call(
        body, name=name, grid=(T // tm,), in_specs=[row, vec, row], out_specs=[vec, row, vec],
        out_shape=[jax.ShapeDtypeStruct((1, D), F32), jax.ShapeDtypeStruct((T, D), F32), jax.ShapeDtypeStruct((1, D), F32)],
        compiler_params=_params("arbitrary"),
    )(x, g.reshape(1, D), target)


def _ssm_disc_math(lre, lim, logdt, br, bi):
    dt = jnp.exp(logdt)
    mag = jnp.exp(lre * dt)
    ar = mag * jnp.cos(lim * dt)
    ai = mag * jnp.sin(lim * dt)
    den = lre * lre + lim * lim
    nr = ar - 1.0
    fr = (nr * lre + ai * lim) / den
    fi = (ai * lre - nr * lim) / den
    return ar, ai, fr[None] * br - fi[None] * bi, fr[None] * bi + fi[None] * br


def _ssm_disc(lre, lim, logdt, br, bi, name):
    def body(lre_ref, lim_ref, dt_ref, br_ref, bi_ref, ar_ref, ai_ref, bbr_ref, bbi_ref):
        ar, ai, bbr, bbi = _ssm_disc_math(lre_ref[...], lim_ref[...], dt_ref[...], br_ref[...], bi_ref[...])
        ar_ref[...] = ar
        ai_ref[...] = ai
        bbr_ref[...] = bbr
        bbi_ref[...] = bbi

    sd = jax.ShapeDtypeStruct
    return pl.pallas_call(
        body, name=name, out_shape=[sd(lre.shape, F32), sd(lre.shape, F32), sd(br.shape, F32), sd(br.shape, F32)],
    )(lre, lim, logdt, br, bi)


def _ssm_disc_bwd(lre, lim, logdt, br, bi, dar, dai, dbbr, dbbi, name):
    def body(lre_ref, lim_ref, dt_ref, br_ref, bi_ref, dar_ref, dai_ref, dbbr_ref, dbbi_ref,
             glre_ref, glim_ref, gdt_ref, gbr_ref, gbi_ref):
        _, vjp = jax.vjp(_ssm_disc_math, lre_ref[...], lim_ref[...], dt_ref[...], br_ref[...], bi_ref[...])
        glre, glim, gdt, gbr, gbi = vjp((dar_ref[...], dai_ref[...], dbbr_ref[...], dbbi_ref[...]))
        glre_ref[...] = glre
        glim_ref[...] = glim
        gdt_ref[...] = gdt
        gbr_ref[...] = gbr
        gbi_ref[...] = gbi

    sd = jax.ShapeDtypeStruct
    return pl.pallas_call(
        body, name=name,
        out_shape=[sd(lre.shape, F32), sd(lre.shape, F32), sd(logdt.shape, F32), sd(br.shape, F32), sd(br.shape, F32)],
    )(lre, lim, logdt, br, bi, dar, dai, dbbr, dbbi)


SSM_STEPS_FWD = 256
SSM_STEPS_BWD = 256


def _ssm_tiles(ref, v, off, steps, n):
    return [ref[v, pl.ds(off + j, steps, stride=SUBLANES), :] for j in range(n)]


def _ssm_fwd(uz, bmat, cmat, art, ait, dvec, name, steps=SSM_STEPS_FWD):
    T = uz.shape[0]
    nblk, cb, width = bmat.shape
    C = nblk * cb
    half = width // 2
    nt = half // LANES
    npair = nblk // 2
    kc = min(steps, T)
    nchunk = T // kc

    def body(u_ref, b_ref, c_ref, ar_ref, ai_ref, d_ref, y_ref, xr_ref, xi_ref, sr_ref, si_ref):
        @pl.when(pl.program_id(0) == 0)
        def _():
            sr_ref[...] = jnp.zeros_like(sr_ref)
            si_ref[...] = jnp.zeros_like(si_ref)

        uv = u_ref[...]
        for b in range(nblk):
            bu = _dot(uv[:, b * cb:(b + 1) * cb].astype(BF16), b_ref[b], NN)
            v, off = b // 2, nt * (b % 2)
            for j in range(nt):
                xr_ref[v, pl.ds(off + j, kc, stride=SUBLANES), :] = bu[:, j * LANES:(j + 1) * LANES]
                xi_ref[v, pl.ds(off + j, kc, stride=SUBLANES), :] = bu[:, half + j * LANES:half + (j + 1) * LANES]
        ars = [ar_ref[v] for v in range(npair)]
        ais = [ai_ref[v] for v in range(npair)]

        def step(k, carry):
            row = pl.ds(k * SUBLANES, SUBLANES)
            out = []
            for v in range(npair):
                xr, xi = carry[2 * v], carry[2 * v + 1]
                nr = ars[v] * xr - ais[v] * xi + xr_ref[v, row, :]
                ni = ars[v] * xi + ais[v] * xr + xi_ref[v, row, :]
                xr_ref[v, row, :] = nr
                xi_ref[v, row, :] = ni
                out += [nr, ni]
            return tuple(out)

        fin = tuple(ref[v] for v in range(npair) for ref in (sr_ref, si_ref))
        for k in range(kc):
            fin = step(k, fin)
        for v in range(npair):
            sr_ref[v] = fin[2 * v]
            si_ref[v] = fin[2 * v + 1]
        for b in range(nblk):
            v, off = b // 2, nt * (b % 2)
            xb = jnp.concatenate(_ssm_tiles(xr_ref, v, off, kc, nt) + _ssm_tiles(xi_ref, v, off, kc, nt), axis=1)
            cols = slice(b * cb, (b + 1) * cb)
            y_ref[:, cols] = _dot(xb.astype(BF16), c_ref[b], NN) + d_ref[:, cols] * uv[:, cols]

    whole = lambda a: pl.BlockSpec(a.shape, lambda c: (0,) * a.ndim)
    st = pl.BlockSpec((npair, kc * SUBLANES, LANES), lambda c: (0, c, 0))
    sd = jax.ShapeDtypeStruct
    return pl.pallas_call(
        body, name=name, grid=(nchunk,),
        in_specs=[pl.BlockSpec((kc, C), lambda c: (c, 0)), whole(bmat), whole(cmat), whole(art), whole(ait), whole(dvec)],
        out_specs=[pl.BlockSpec((kc, C), lambda c: (c, 0)), st, st],
        out_shape=[sd((T, C), F32), sd((npair, T * SUBLANES, LANES), F32), sd((npair, T * SUBLANES, LANES), F32)],
        scratch_shapes=[pltpu.VMEM((npair, SUBLANES, LANES), F32), pltpu.VMEM((npair, SUBLANES, LANES), F32)],
        compiler_params=_params("arbitrary"),
    )(uz, bmat, cmat, art, ait, dvec)


def _ssm_bwd(dy, uz, xr, xi, bmat, cmat, art, ait, dvec, dproj, name):
    T = uz.shape[0]
    nblk, cb, width = bmat.shape
    C = nblk * cb
    half = width // 2
    nt = half // LANES
    npair = nblk // 2
    kc = min(SSM_STEPS_BWD, T)
    nchunk = T // kc

    def body(dy_ref, u_ref, xr_ref, xi_ref, xpr_ref, xpi_ref, b_ref, c_ref, ar_ref, ai_ref, d_ref, _,
             du_ref, db_ref, dc_ref, dar_ref, dai_ref, dd_ref, gr_ref, gi_ref, sr_ref, si_ref):
        c = pl.program_id(0)

        @pl.when(c == 0)
        def _():
            for ref in (sr_ref, si_ref, db_ref, dc_ref, dar_ref, dai_ref, dd_ref):
                ref[...] = jnp.zeros_like(ref)

        dyv = dy_ref[...]
        uv = u_ref[...]
        for b in range(nblk):
            dx = _dot(dyv[:, b * cb:(b + 1) * cb].astype(BF16), c_ref[b], NT)
            v, off = b // 2, nt * (b % 2)
            for j in range(nt):
                gr_ref[v, pl.ds(off + j, kc, stride=SUBLANES), :] = dx[:, j * LANES:(j + 1) * LANES]
                gi_ref[v, pl.ds(off + j, kc, stride=SUBLANES), :] = dx[:, half + j * LANES:half + (j + 1) * LANES]
        ars = [ar_ref[v] for v in range(npair)]
        ais = [ai_ref[v] for v in range(npair)]

        def pair_update(v, gr, gi, row):
            nr = ars[v] * gr + ais[v] * gi + gr_ref[v, row, :]
            ni = ars[v] * gi - ais[v] * gr + gi_ref[v, row, :]
            gr_ref[v, row, :] = nr
            gi_ref[v, row, :] = ni
            return nr, ni

        def step(i, carry):
            k = kc - 1 - i
            row = pl.ds(k * SUBLANES, SUBLANES)
            prow = pl.ds((k - 1) * SUBLANES, SUBLANES)
            out = []
            for v in range(npair):
                gr, gi, sr, si = carry[4 * v:4 * v + 4]
                nr, ni = pair_update(v, gr, gi, row)
                pr, pi = xr_ref[v, prow, :], xi_ref[v, prow, :]
                out += [nr, ni, sr + pr * nr + pi * ni, si + pr * ni - pi * nr]
            return tuple(out)

        mid = tuple(ref[v] for v in range(npair) for ref in (sr_ref, si_ref, dar_ref, dai_ref))
        for i in range(kc - 1):
            mid = step(i, mid)
        live = (c < nchunk - 1).astype(F32)
        row0 = pl.ds(0, SUBLANES)
        for v in range(npair):
            gr, gi, sr, si = mid[4 * v:4 * v + 4]
            nr, ni = pair_update(v, gr, gi, row0)
            pr, pi = xpr_ref[v] * live, xpi_ref[v] * live
            sr_ref[v] = nr
            si_ref[v] = ni
            dar_ref[v] = sr + pr * nr + pi * ni
            dai_ref[v] = si + pr * ni - pi * nr
        for b in range(nblk):
            v, off = b // 2, nt * (b % 2)
            cols = slice(b * cb, (b + 1) * cb)
            gb = jnp.concatenate(_ssm_tiles(gr_ref, v, off, kc, nt) + _ssm_tiles(gi_ref, v, off, kc, nt), axis=1).astype(BF16)
            xb = jnp.concatenate(_ssm_tiles(xr_ref, v, off, kc, nt) + _ssm_tiles(xi_ref, v, off, kc, nt), axis=1).astype(BF16)
            du_ref[:, cols] = (_dot(gb, b_ref[b], NT) + dyv[:, cols] * d_ref[:, cols]).astype(BF16)
            db_ref[b] += _dot(uv[:, cols].astype(BF16), gb, TN)
            dc_ref[b] += _dot(dyv[:, cols].astype(BF16), xb, TN)
        dd_ref[...] += jnp.sum(dyv * uv, axis=0, keepdims=True)

    whole = lambda a: pl.BlockSpec(a.shape, lambda c: (0,) * a.ndim)
    rev = lambda c: (nchunk - 1 - c, 0)
    st = pl.BlockSpec((npair, kc * SUBLANES, LANES), lambda c: (0, nchunk - 1 - c, 0))
    stp = pl.BlockSpec((npair, SUBLANES, LANES), lambda c: (0, jnp.maximum((nchunk - 1 - c) * kc - 1, 0), 0))
    acc = lambda shape: pl.BlockSpec(shape, lambda c: (0,) * len(shape))
    sd = jax.ShapeDtypeStruct
    pair_shape = (npair, SUBLANES, LANES)
    return pl.pallas_call(
        body, name=name, grid=(nchunk,),
        in_specs=[pl.BlockSpec((kc, C), rev), pl.BlockSpec((kc, C), rev), st, st, stp, stp, whole(bmat), whole(cmat),
                  whole(art), whole(ait), whole(dvec), pl.BlockSpec(memory_space=pl.ANY)],
        out_specs=[pl.BlockSpec((kc, C), rev), acc(bmat.shape), acc(bmat.shape), acc(pair_shape), acc(pair_shape), acc((1, C))],
        out_shape=[sd(dproj.shape, BF16), sd(bmat.shape, F32), sd(bmat.shape, F32), sd(pair_shape, F32), sd(pair_shape, F32),
                   sd((1, C), F32)],
        scratch_shapes=[pltpu.VMEM((npair, kc * SUBLANES, LANES), F32), pltpu.VMEM((npair, kc * SUBLANES, LANES), F32),
                        pltpu.VMEM(pair_shape, F32), pltpu.VMEM(pair_shape, F32)],
        input_output_aliases={11: 0}, compiler_params=_params("arbitrary"),
    )(dy, uz, xr, xi, xr, xi, bmat, cmat, art, ait, dvec, dproj)


def _ssm_post(y, z, w_glu, b_glu, name):
    T, C = y.shape
    tm = _pick(T, 512, SUBLANES)

    def body(y_ref, z_ref, w_ref, b_ref, o_ref, a_ref):
        a, _ = _gelu_and_grad(y_ref[...])
        ab = a.astype(BF16)
        sg = _sigmoid(_dot(ab, w_ref[...], NN) + b_ref[...])
        sz, _ = _silu_and_grad(z_ref[...].astype(F32))
        o_ref[...] = (a * sg * sz).astype(BF16)
        a_ref[...] = ab

    row = pl.BlockSpec((tm, C), lambda i: (i, 0))
    return pl.pallas_call(
        body, name=name, grid=(T // tm,),
        in_specs=[row, row, pl.BlockSpec((C, C), lambda i: (0, 0)), pl.BlockSpec((1, C), lambda i: (0, 0))],
        out_specs=[row, row], out_shape=[jax.ShapeDtypeStruct((T, C), BF16)] * 2, compiler_params=_params("parallel"),
    )(y, z, w_glu, b_glu.reshape(1, C))


def _ssm_post_bwd(do, y, z, w_glu, b_glu, dproj, col, name):
    T, C = y.shape
    tm = _pick(T, 512, SUBLANES)

    def body(do_ref, y_ref, z_ref, w_ref, b_ref, _, dy_ref, dz_ref, ds_ref, db_ref):
        dov = do_ref[...]
        a, da_dy = _gelu_and_grad(y_ref[...])
        sg = _sigmoid(_dot(a.astype(BF16), w_ref[...], NN) + b_ref[...])
        sz, dsz = _silu_and_grad(z_ref[...].astype(F32))
        yg = a * sg
        dz_ref[...] = (dov * yg * dsz).astype(BF16)
        dyg = dov * sz
        ds = dyg * a * sg * (1.0 - sg)
        dsb = ds.astype(BF16)
        ds_ref[...] = dsb
        da = dyg * sg + _dot(dsb, w_ref[...], NT)
        dy_ref[...] = da * da_dy

        @pl.when(pl.program_id(0) == 0)
        def _():
            db_ref[...] = jnp.zeros_like(db_ref)

        db_ref[...] += jnp.sum(ds, axis=0, keepdims=True)

    row = pl.BlockSpec((tm, C), lambda i: (i, 0))
    vec = pl.BlockSpec((1, C), lambda i: (0, 0))
    sd = jax.ShapeDtypeStruct
    return pl.pallas_call(
        body, name=name, grid=(T // tm,),
        in_specs=[row, row, row, pl.BlockSpec((C, C), lambda i: (0, 0)), vec, pl.BlockSpec(memory_space=pl.ANY)],
        out_specs=[row, pl.BlockSpec((tm, C), lambda i: (i, col // C)), row, vec],
        out_shape=[sd((T, C), F32), sd(dproj.shape, BF16), sd((T, C), BF16), sd((1, C), F32)],
        input_output_aliases={5: 1}, compiler_params=_params("arbitrary"),
    )(do, y, z, w_glu, b_glu.reshape(1, C), dproj)


def _rel_bucket(dist):
    n = jnp.maximum(dist, 0)
    max_exact = NUM_BUCKETS // 2
    n_f = jnp.maximum(n, 1).astype(F32)
    large = max_exact + (jnp.log(n_f / max_exact) / math.log(REL_MAX_DISTANCE / max_exact)
                         * (NUM_BUCKETS - max_exact)).astype(jnp.int32)
    large = jnp.minimum(large, NUM_BUCKETS - 1)
    return jnp.where(n < max_exact, n, large)


def _band_tables():
    qi = jnp.arange(ATTN_BLOCK)[:, None]
    kj = jnp.arange(2 * ATTN_BLOCK)[None, :]
    delta = ATTN_BLOCK + qi - kj
    buckets, bands = [], []
    for window, dilation in ATTN_CONFIGS:
        bands.append((delta >= 0) & (delta <= window // dilation))
        buckets.append(_rel_bucket(jnp.maximum(delta, 0) * dilation))
    return jnp.stack(buckets), jnp.stack(bands)


ATTN_UNITS = 4


def _attn_tile(T, r, nu):
    nq = max(1, nu // r)
    rows = ATTN_BLOCK * r * nq
    return nq, rows, T // rows


def _attn_units(r, nq, chunk, nu):
    if r >= nu:
        return [(chunk * nu + i, None) for i in range(nu)]
    units = []
    for j in range(nq):
        for s in range(r):
            units.append((ATTN_BLOCK * j * r + s, ATTN_BLOCK * (j - 1) * r + s if j else None))
    return units


def _rows(start, r):
    return pl.ds(start, ATTN_BLOCK, stride=r) if r > 1 else pl.ds(start, ATTN_BLOCK)


def _attn_group_fwd(qkv, biasm, g, name, nu=ATTN_UNITS):
    T = qkv.shape[0]
    r = ATTN_CONFIGS[g][1]
    B, hd = ATTN_BLOCK, ATTN_HEAD_DIM
    nq, rows, ntiles = _attn_tile(T, r, nu)
    nchunks = max(1, r // nu)
    last_prev = B * (nq - 1) * r
    scale = hd ** -0.5
    tiles_per_tensor = 3 * HEADS_PER_GROUP * hd // LANES

    def body(q_ref, kc_ref, kp_ref, vc_ref, vp_ref, bias_ref, o_ref, lse_ref, s_ref, p_ref):
        n = pl.program_id(1)
        lane = lax.broadcasted_iota(jnp.int32, (1, LANES), 1)
        col = lax.broadcasted_iota(jnp.int32, (1, 2 * B), 1)
        masks = [lane < hd, lane >= hd]
        first_pen = jnp.where((col < B) & (n == 0), NEG_INF, 0.0)

        def chunk_body(chunk):
            units = _attn_units(r, nq, chunk, nu)

            def keys(cur_ref, prev_ref, cs, ps):
                prev = prev_ref[_rows(last_prev + (cs if r >= nu else cs % r), r), :] if ps is None else cur_ref[_rows(ps, r), :]
                return jnp.concatenate([prev, cur_ref[_rows(cs, r), :]], axis=0).astype(BF16)

            for u, (cs, ps) in enumerate(units):
                qv = q_ref[_rows(cs, r), :]
                kw = keys(kc_ref, kp_ref, cs, ps)
                for hh in range(2):
                    s_ref[2 * u + hh] = _dot(jnp.where(masks[hh], qv, 0.0).astype(BF16), kw, NT)
            for u, (cs, ps) in enumerate(units):
                lses = []
                for hh in range(2):
                    s = s_ref[2 * u + hh] * scale + bias_ref[hh]
                    if ps is None:
                        s = s + first_pen
                    m = jnp.max(s, axis=-1, keepdims=True)
                    p = jnp.exp(s - m)
                    l = jnp.sum(p, axis=-1, keepdims=True)
                    p_ref[2 * u + hh] = (p * (1.0 / l)).astype(BF16)
                    lses.append(m + jnp.log(l))
                lse_ref[_rows(cs, r), :] = jnp.where(masks[0], lses[0], lses[1])
            for u, (cs, ps) in enumerate(units):
                vw = keys(vc_ref, vp_ref, cs, ps)
                o_ref[_rows(cs, r), :] = (_dot(p_ref[2 * u], jnp.where(masks[0], vw, 0), NN)
                                          + _dot(p_ref[2 * u + 1], jnp.where(masks[1], vw, 0), NN))

        if nchunks == 1:
            chunk_body(0)
        else:
            pl.loop(0, nchunks)(chunk_body)

    def cur(t):
        return pl.BlockSpec((rows, LANES), lambda hf, n: (n, t * tiles_per_tensor + 2 * g + hf))

    def prev(t):
        return pl.BlockSpec((rows, LANES), lambda hf, n: (jnp.maximum(n - 1, 0), t * tiles_per_tensor + 2 * g + hf))

    out = pl.BlockSpec((rows, LANES), lambda hf, n: (n, hf))
    sd = jax.ShapeDtypeStruct((T, 2 * LANES), F32)
    return pl.pallas_call(
        body, name=name, grid=(2, ntiles),
        in_specs=[cur(0), cur(1), prev(1), cur(2), prev(2), pl.BlockSpec((None, 2, B, 2 * B), lambda hf, n: (g, hf, 0, 0))],
        out_specs=[out, out], out_shape=[sd, sd],
        scratch_shapes=[pltpu.VMEM((2 * nu, B, 2 * B), F32), pltpu.VMEM((2 * nu, B, 2 * B), BF16)],
        compiler_params=_params("parallel", "parallel"),
    )(qkv, qkv, qkv, qkv, qkv, biasm)


def _attn_group_bwd(qkv, do, dvec, lse, biasm, g, dproj, dk_col, name, nu=ATTN_UNITS):
    T = qkv.shape[0]
    r = ATTN_CONFIGS[g][1]
    B, hd = ATTN_BLOCK, ATTN_HEAD_DIM
    nq, rows, ntiles = _attn_tile(T, r, nu)
    nchunks = max(1, r // nu)
    last_prev = B * (nq - 1) * r
    scale = hd ** -0.5
    tiles_per_tensor = 3 * HEADS_PER_GROUP * hd // LANES

    def body(q_ref, kc_ref, kp_ref, vc_ref, vp_ref, do_ref, dv_ref, lse_ref, bias_ref, _,
             dq_ref, dk_ref, dvo_ref, dbias_ref, ck_ref, cv_ref, ak_ref, av_ref, s_ref, dp_ref, p_ref, ds_ref):
        n = pl.program_id(1)
        lane = lax.broadcasted_iota(jnp.int32, (1, LANES), 1)
        col = lax.broadcasted_iota(jnp.int32, (1, 2 * B), 1)
        masks = [lane < hd, lane >= hd]
        first_pen = jnp.where((col < B) & (n == 0), NEG_INF, 0.0)

        @pl.when(n == 0)
        def _():
            dbias_ref[...] = jnp.zeros_like(dbias_ref)
            ck_ref[...] = jnp.zeros_like(ck_ref)
            cv_ref[...] = jnp.zeros_like(cv_ref)

        def chunk_body(chunk):
            units = _attn_units(r, nq, chunk, nu)

            def prev_rows(cs):
                return _rows(last_prev + (cs if r >= nu else cs % r), r)

            def keys(cur_ref, prev_ref, cs, ps):
                prev = prev_ref[prev_rows(cs), :] if ps is None else cur_ref[_rows(ps, r), :]
                return jnp.concatenate([prev, cur_ref[_rows(cs, r), :]], axis=0).astype(BF16)

            for u, (cs, ps) in enumerate(units):
                qv = q_ref[_rows(cs, r), :]
                dov = do_ref[_rows(cs, r), :]
                kw = keys(kc_ref, kp_ref, cs, ps)
                vw = keys(vc_ref, vp_ref, cs, ps)
                for hh in range(2):
                    s_ref[2 * u + hh] = _dot(jnp.where(masks[hh], qv, 0.0).astype(BF16), kw, NT)
                    dp_ref[2 * u + hh] = _dot(jnp.where(masks[hh], dov, 0.0).astype(BF16), vw, NT)
            for u, (cs, ps) in enumerate(units):
                lse_t = lse_ref[_rows(cs, r), :]
                dv_t = dv_ref[_rows(cs, r), :]
                for hh in range(2):
                    lo = hh * hd
                    s = s_ref[2 * u + hh] * scale + bias_ref[hh]
                    if ps is None:
                        s = s + first_pen
                    p = jnp.exp(s - lse_t[:, lo:lo + 1])
                    ds = p * (dp_ref[2 * u + hh] + dv_t[:, lo:lo + 1])
                    dbias_ref[hh] += ds
                    p_ref[2 * u + hh] = p.astype(BF16)
                    ds_ref[2 * u + hh] = ds.astype(BF16)
            for u, (cs, ps) in enumerate(units):
                qv = q_ref[_rows(cs, r), :]
                dov = do_ref[_rows(cs, r), :]
                kw = keys(kc_ref, kp_ref, cs, ps)
                dq, dkw, dvw = 0.0, 0.0, 0.0
                for hh in range(2):
                    dsb = ds_ref[2 * u + hh]
                    dq = dq + _dot(dsb, jnp.where(masks[hh], kw, 0), NN)
                    dkw = dkw + _dot(dsb, jnp.where(masks[hh], qv, 0.0).astype(BF16), TN)
                    dvw = dvw + _dot(p_ref[2 * u + hh], jnp.where(masks[hh], dov, 0.0).astype(BF16), TN)
                dq_ref[_rows(cs, r), :] = dq * scale
                ak_ref[_rows(cs, r), :] = dkw[B:] * scale
                av_ref[_rows(cs, r), :] = dvw[B:]
                if ps is None:
                    ck_ref[prev_rows(cs), :] += dkw[:B] * scale
                    cv_ref[prev_rows(cs), :] += dvw[:B]
                else:
                    ak_ref[_rows(ps, r), :] += dkw[:B] * scale
                    av_ref[_rows(ps, r), :] += dvw[:B]

        @pl.when(n < ntiles)
        def _():
            for chunk in range(nchunks):
                chunk_body(chunk)

        dk_ref[...] = ck_ref[...].astype(BF16)
        dvo_ref[...] = cv_ref[...].astype(BF16)
        ck_ref[...] = ak_ref[...]
        cv_ref[...] = av_ref[...]

    last = ntiles - 1

    def cur(t):
        return pl.BlockSpec((rows, LANES), lambda hf, n: (jnp.minimum(n, last), t * tiles_per_tensor + 2 * g + hf))

    def prev(t):
        return pl.BlockSpec((rows, LANES), lambda hf, n: (jnp.clip(n - 1, 0, last), t * tiles_per_tensor + 2 * g + hf))

    nat = pl.BlockSpec((rows, LANES), lambda hf, n: (jnp.minimum(n, last), hf))
    nat_prev = pl.BlockSpec((rows, LANES), lambda hf, n: (jnp.clip(n - 1, 0, last), hf))
    tab = pl.BlockSpec((None, 2, B, 2 * B), lambda hf, n: (g, hf, 0, 0))
    dtab = pl.BlockSpec((2, B, 2 * B), lambda hf, n: (hf, 0, 0))
    sd = jax.ShapeDtypeStruct
    vm = pltpu.VMEM
    dk_tile = dk_col // LANES + 2 * g
    dk_spec = pl.BlockSpec((rows, LANES), lambda hf, n: (jnp.clip(n - 1, 0, last), dk_tile + hf))
    return pl.pallas_call(
        body, name=name, grid=(2, ntiles + 1),
        in_specs=[cur(0), cur(1), prev(1), cur(2), prev(2), nat, nat, nat, tab, pl.BlockSpec(memory_space=pl.ANY)],
        out_specs=[nat, dk_spec, nat_prev, dtab],
        out_shape=[sd((T, 2 * LANES), F32), sd(dproj.shape, BF16), sd((T, 2 * LANES), BF16),
                   sd((HEADS_PER_GROUP, B, 2 * B), F32)],
        scratch_shapes=[vm((rows, LANES), F32), vm((rows, LANES), F32), vm((rows, LANES), F32), vm((rows, LANES), F32),
                        vm((2 * nu, B, 2 * B), F32), vm((2 * nu, B, 2 * B), F32),
                        vm((2 * nu, B, 2 * B), BF16), vm((2 * nu, B, 2 * B), BF16)],
        input_output_aliases={9: 1}, compiler_params=_params("parallel", "arbitrary"),
    )(qkv, qkv, qkv, qkv, qkv, do, dvec, lse, biasm, dproj)


def _attn_mix(os, lses, z, name):
    T, gw = os[0].shape
    C = z.shape[1]
    tm = _pick(T, 512, SUBLANES)

    def body(o0_ref, o1_ref, o2_ref, l0_ref, l1_ref, l2_ref, z_ref, out_ref):
        ls = [l0_ref[...], l1_ref[...], l2_ref[...]]
        mx = jnp.maximum(jnp.maximum(ls[0], ls[1]), ls[2])
        es = [jnp.exp(l - mx) for l in ls]
        inv = 1.0 / (es[0] + es[1] + es[2])
        for i, o_ref in enumerate((o0_ref, o1_ref, o2_ref)):
            sz, _ = _silu_and_grad(z_ref[:, i * gw:(i + 1) * gw].astype(F32))
            out_ref[:, i * gw:(i + 1) * gw] = (o_ref[...] * (es[i] * inv) * sz).astype(BF16)

    row = pl.BlockSpec((tm, C), lambda i: (i, 0))
    grp = pl.BlockSpec((tm, gw), lambda i: (i, 0))
    return pl.pallas_call(
        body, name=name, grid=(T // tm,), in_specs=[grp] * 6 + [row], out_specs=row,
        out_shape=jax.ShapeDtypeStruct((T, C), BF16), compiler_params=_params("parallel"),
    )(*os, *lses, z)


def _attn_mix_bwd(dout, os, lses, z, dproj, col, name):
    T, gw = os[0].shape
    C = z.shape[1]
    tm = _pick(T, 512, SUBLANES)
    head_of = np.arange(gw) // ATTN_HEAD_DIM
    ones = jnp.asarray(head_of[:, None] == head_of[None, :], BF16)

    def body(dout_ref, o0_ref, o1_ref, o2_ref, l0_ref, l1_ref, l2_ref, z_ref, ones_ref, _,
             dz_ref, do0_ref, do1_ref, do2_ref, dv0_ref, dv1_ref, dv2_ref):
        ls = [l0_ref[...], l1_ref[...], l2_ref[...]]
        mx = jnp.maximum(jnp.maximum(ls[0], ls[1]), ls[2])
        es = [jnp.exp(l - mx) for l in ls]
        inv = 1.0 / (es[0] + es[1] + es[2])
        alphas, ebar = [], 0.0
        for i, (o_ref, do_ref) in enumerate(((o0_ref, do0_ref), (o1_ref, do1_ref), (o2_ref, do2_ref))):
            sl = slice(i * gw, (i + 1) * gw)
            alpha = es[i] * inv
            ov = o_ref[...]
            dv = dout_ref[:, sl]
            sz, dsz = _silu_and_grad(z_ref[:, sl].astype(F32))
            dz_ref[:, sl] = (dv * ov * alpha * dsz).astype(BF16)
            da = dv * sz
            do_ref[...] = da * alpha
            t = da * ov
            t1 = t.astype(BF16)
            r1 = t - t1.astype(F32)
            t2 = r1.astype(BF16)
            t3 = (r1 - t2.astype(F32)).astype(BF16)
            e = _dot(t1, ones_ref[...], NN) + _dot(t2, ones_ref[...], NN) + _dot(t3, ones_ref[...], NN)
            ebar = ebar + alpha * e
            alphas.append(alpha)
        for alpha, dv_ref in zip(alphas, (dv0_ref, dv1_ref, dv2_ref)):
            dv_ref[...] = -alpha * ebar

    row = pl.BlockSpec((tm, C), lambda i: (i, 0))
    grp = pl.BlockSpec((tm, gw), lambda i: (i, 0))
    sd = jax.ShapeDtypeStruct
    res = pl.pallas_call(
        body, name=name, grid=(T // tm,),
        in_specs=[row] + [grp] * 6 + [row, pl.BlockSpec((gw, gw), lambda i: (0, 0)), pl.BlockSpec(memory_space=pl.ANY)],
        out_specs=[pl.BlockSpec((tm, C), lambda i: (i, col // C))] + [grp] * 6,
        out_shape=[sd(dproj.shape, BF16)] + [sd((T, gw), F32)] * 6, input_output_aliases={9: 0},
        compiler_params=_params("parallel"),
    )(dout, *os, *lses, z, ones, dproj)
    return res[0], res[1:4], res[4:7]


def _mem_attn(qz, kv, name):
    T = qz.shape[0]
    dm = qz.shape[1] // 2
    M = kv.shape[0]
    hd = dm // MEM_HEADS
    scale = hd ** -0.5
    tm = _pick(T, 512, SUBLANES)

    def body(q_ref, z_ref, k_ref, v_ref, o_ref, s_ref, p_ref):
        heads = [slice(h * hd, (h + 1) * hd) for h in range(MEM_HEADS)]
        for h, sl in enumerate(heads):
            s_ref[h] = _dot(q_ref[:, sl].astype(BF16), k_ref[:, sl], NT)
        for h, sl in enumerate(heads):
            s = s_ref[h] * scale
            p = jnp.exp(s - jnp.max(s, axis=-1, keepdims=True))
            p_ref[h] = (p * (1.0 / jnp.sum(p, axis=-1, keepdims=True))).astype(BF16)
        for h, sl in enumerate(heads):
            sz, _ = _silu_and_grad(z_ref[:, sl].astype(F32))
            o_ref[:, sl] = (_dot(p_ref[h], v_ref[:, sl], NN) * sz).astype(BF16)

    return pl.pallas_call(
        body, name=name, grid=(T // tm,),
        in_specs=[pl.BlockSpec((tm, dm), lambda i: (i, 0)), pl.BlockSpec((tm, dm), lambda i: (i, 1)),
                  pl.BlockSpec((M, dm), lambda i: (0, 0)), pl.BlockSpec((M, dm), lambda i: (0, 1))],
        out_specs=pl.BlockSpec((tm, dm), lambda i: (i, 0)),
        out_shape=jax.ShapeDtypeStruct((T, dm), BF16),
        scratch_shapes=[pltpu.VMEM((MEM_HEADS, tm, M), F32), pltpu.VMEM((MEM_HEADS, tm, M), BF16)],
        compiler_params=_params("parallel"),
    )(qz, qz, kv, kv)


def _mem_attn_bwd(do, qz, kv, dproj, col, name):
    T = qz.shape[0]
    dm = qz.shape[1] // 2
    M = kv.shape[0]
    hd = dm // MEM_HEADS
    scale = hd ** -0.5
    tm = _pick(T, 512, SUBLANES)

    def body(do_ref, q_ref, z_ref, k_ref, v_ref, _, dq_ref, dz_ref, dk_ref, dv_ref, s_ref, dp_ref, p_ref, ds_ref, dob_ref):
        @pl.when(pl.program_id(0) == 0)
        def _():
            dk_ref[...] = jnp.zeros_like(dk_ref)
            dv_ref[...] = jnp.zeros_like(dv_ref)

        heads = [slice(h * hd, (h + 1) * hd) for h in range(MEM_HEADS)]
        for h, sl in enumerate(heads):
            sz, _ = _silu_and_grad(z_ref[:, sl].astype(F32))
            dob = (do_ref[:, sl] * sz).astype(BF16)
            dob_ref[:, sl] = dob
            s_ref[h] = _dot(q_ref[:, sl].astype(BF16), k_ref[:, sl], NT)
            dp_ref[h] = _dot(dob, v_ref[:, sl], NT)
        for h, sl in enumerate(heads):
            s = s_ref[h] * scale
            p = jnp.exp(s - jnp.max(s, axis=-1, keepdims=True))
            pn = p * (1.0 / jnp.sum(p, axis=-1, keepdims=True))
            dp = dp_ref[h]
            p_ref[h] = pn.astype(BF16)
            ds_ref[h] = (pn * (dp - jnp.sum(dp * pn, axis=-1, keepdims=True))).astype(BF16)
        for h, sl in enumerate(heads):
            _, dsz = _silu_and_grad(z_ref[:, sl].astype(F32))
            dz_ref[:, sl] = (do_ref[:, sl] * _dot(p_ref[h], v_ref[:, sl], NN) * dsz).astype(BF16)
            dq_ref[:, sl] = (_dot(ds_ref[h], k_ref[:, sl], NN) * scale).astype(BF16)
            dk_ref[:, sl] += _dot(ds_ref[h], q_ref[:, sl].astype(BF16), TN) * scale
            dv_ref[:, sl] += _dot(p_ref[h], dob_ref[:, sl], TN)

    rowq = pl.BlockSpec((tm, dm), lambda i: (i, 0))
    rowz = pl.BlockSpec((tm, dm), lambda i: (i, 1))
    kb = pl.BlockSpec((M, dm), lambda i: (0, 0))
    vb = pl.BlockSpec((M, dm), lambda i: (0, 1))
    sd = jax.ShapeDtypeStruct
    dq, dz, dk, dv = pl.pallas_call(
        body, name=name, grid=(T // tm,), in_specs=[rowq, rowq, rowz, kb, vb, pl.BlockSpec(memory_space=pl.ANY)],
        out_specs=[pl.BlockSpec((tm, dm), lambda i: (i, col // dm)), rowq, kb, kb],
        out_shape=[sd(dproj.shape, BF16), sd((T, dm), BF16), sd((M, dm), F32), sd((M, dm), F32)],
        scratch_shapes=[pltpu.VMEM((MEM_HEADS, tm, M), F32), pltpu.VMEM((MEM_HEADS, tm, M), F32),
                        pltpu.VMEM((MEM_HEADS, tm, M), BF16), pltpu.VMEM((MEM_HEADS, tm, M), BF16), pltpu.VMEM((tm, dm), BF16)],
        input_output_aliases={5: 0}, compiler_params=_params("arbitrary"),
    )(do, qz, qz, kv, kv, dproj)
    return dq, dz, dk, dv


def _merge(os, ws, L, logits, b_gate, name):
    T = os[0].shape[0]
    D = ws[0].shape[2]
    tm = _pick(T, 512, SUBLANES)

    def body(o0_ref, o1_ref, o2_ref, w0_ref, w1_ref, w2_ref, l_ref, b_ref, m_ref, p_ref):
        acc = 0.0
        for i, (o_ref, w_ref) in enumerate(((o0_ref, w0_ref), (o1_ref, w1_ref), (o2_ref, w2_ref))):
            sl = slice(i * D, (i + 1) * D)
            bp = _dot(o_ref[...], w_ref[...], NN)
            p_ref[i] = bp.astype(BF16)
            acc = acc + _sigmoid(l_ref[:, sl].astype(F32) + b_ref[:, sl]) * bp
        m_ref[...] = acc.astype(BF16)

    return pl.pallas_call(
        body, name=name, grid=(T // tm,),
        in_specs=[pl.BlockSpec((tm, o.shape[1]), lambda i: (i, 0)) for o in os]
        + [pl.BlockSpec((None,) + w.shape[1:], lambda i: (L, 0, 0)) for w in ws]
        + [pl.BlockSpec((tm, 3 * D), lambda i: (i, 0)), pl.BlockSpec((1, 3 * D), lambda i: (0, 0))],
        out_specs=[pl.BlockSpec((tm, D), lambda i: (i, 0)), pl.BlockSpec((3, tm, D), lambda i: (0, i, 0))],
        out_shape=[jax.ShapeDtypeStruct((T, D), BF16), jax.ShapeDtypeStruct((3, T, D), BF16)],
        compiler_params=_params("parallel"),
    )(*os, *ws, logits, b_gate.reshape(1, 3 * D))


def _merge_bwd(dmerged, bps, logits, b_gate, dproj_cols, dl_off, name):
    _, T, D = bps.shape
    tm = _pick(T, 2048, SUBLANES)
    cw = _pick(math.gcd(dl_off, D), 512, LANES)
    per = D // cw

    def body(dm_ref, p_ref, l_ref, b_ref, dl_ref, d_ref, db_ref):
        @pl.when(pl.program_id(1) == 0)
        def _():
            db_ref[...] = jnp.zeros_like(db_ref)

        dmv = dm_ref[...]
        gt = _sigmoid(l_ref[...].astype(F32) + b_ref[...])
        d_ref[...] = (dmv * gt).astype(BF16)
        dl = dmv * p_ref[...].astype(F32) * gt * (1.0 - gt)
        dl_ref[...] = dl.astype(BF16)
        db_ref[...] += jnp.sum(dl, axis=0, keepdims=True)

    stacked = pl.BlockSpec((None, tm, cw), lambda j, i: (j // per, i, j % per))
    sd = jax.ShapeDtypeStruct
    return pl.pallas_call(
        body, name=name, grid=(3 * per, T // tm),
        in_specs=[pl.BlockSpec((tm, cw), lambda j, i: (i, j % per)), stacked, pl.BlockSpec((tm, cw), lambda j, i: (i, j)),
                  pl.BlockSpec((1, cw), lambda j, i: (0, j))],
        out_specs=[pl.BlockSpec((tm, cw), lambda j, i: (i, dl_off // cw + j)), stacked, pl.BlockSpec((1, cw), lambda j, i: (0, j))],
        out_shape=[sd((T, dproj_cols), BF16), sd((3, T, D), BF16), sd((1, 3 * D), F32)],
        compiler_params=_params("parallel", "arbitrary"),
    )(dmerged, bps, logits, b_gate.reshape(1, 3 * D))


def _block_diag(w):
    nblk, ng, a, b = w.shape
    eye = jnp.eye(ng, dtype=w.dtype)
    return (w[:, :, :, None, :] * eye[None, :, None, :, None]).reshape(nblk, ng * a, ng * b)


def _block_diag_part(m, a, b):
    nblk = m.shape[0]
    ng = m.shape[1] // a
    m5 = m.reshape(nblk, ng, a, ng, b)
    eye = jnp.eye(ng, dtype=m.dtype)
    return jnp.sum(m5 * eye[None, :, None, :, None], axis=3)


def _ssm_matrices(p, L, tag):
    G, P = p["ssm_lambda_re"].shape[1:]
    Hg = SSM_GROUP
    gpb = SSM_BLOCK_CH // Hg
    nblk = G // gpb
    br = p["ssm_b_re"][L].transpose(2, 0, 1)
    bi = p["ssm_b_im"][L].transpose(2, 0, 1)
    disc_in = (p["ssm_lambda_re"][L], p["ssm_lambda_im"][L], p["ssm_log_dt"][L].reshape(G, 1), br, bi)
    ar, ai, bbr, bbi = _ssm_disc(*disc_in, name=f"ssm_disc_{tag}")
    amat = (ar.reshape(nblk // 2, SUBLANES, LANES), ai.reshape(nblk // 2, SUBLANES, LANES))
    bbr_g = bbr.transpose(1, 0, 2).reshape(nblk, gpb, Hg, P)
    bbi_g = bbi.transpose(1, 0, 2).reshape(nblk, gpb, Hg, P)
    bmat = jnp.concatenate([_block_diag(bbr_g), _block_diag(bbi_g)], axis=2).astype(BF16)
    cre = p["ssm_c_re"][L].reshape(nblk, gpb, Hg, P).transpose(0, 1, 3, 2)
    cim = p["ssm_c_im"][L].reshape(nblk, gpb, Hg, P).transpose(0, 1, 3, 2)
    cmat = jnp.concatenate([_block_diag(cre), -_block_diag(cim)], axis=1).astype(BF16)
    return disc_in, amat, bmat, cmat


def _layer_fwd(x, mem, p, wb, L, biasm):
    T, D = x.shape
    C = p["ssm_d"].shape[1]
    dm = wb["w_br_mem"].shape[1]
    tag = f"l{L}"
    s = {"x": x}
    h = _rmsnorm(x, p["norm_g"][L], f"norm_{tag}")
    offs = [int(o) for o in np.cumsum([0, C, C, 3 * 768, 768, 2 * dm, 3 * D])]
    names = ("uz", "z_ssm", "qkv", "z_attn", "qz_mem", "logits")
    dts = (F32, BF16, F32, BF16, BF16, BF16)
    for i, (nm, dt) in enumerate(zip(names, dts)):
        tiles = dict(tm=2048) if offs[i + 1] - offs[i] >= 1024 and dt == BF16 else {}
        s[nm] = _matmul(h, wb["w_in"], mode="nn", name=f"in_{nm}_{tag}", out_dtype=dt, b_lead=L, b_off=offs[i],
                        n_cols=offs[i + 1] - offs[i], **tiles)
    s["h"] = h

    disc_in, amat, bmat, cmat = _ssm_matrices(p, L, tag)
    dvec = p["ssm_d"][L].reshape(1, C)
    y, xr, xi = _ssm_fwd(s["uz"], bmat, cmat, *amat, dvec, f"ssm_scan_{tag}")
    o_ssm, a_glu = _ssm_post(y, s["z_ssm"], wb["w_glu"][L], p["b_glu"][L], f"ssm_post_{tag}")
    s.update(disc_in=disc_in, amat=amat, bmat=bmat, cmat=cmat, xr=xr, xi=xi, y=y, a_glu=a_glu, o_ssm=o_ssm)

    groups = [_attn_group_fwd(s["qkv"], biasm, g, f"attn_g{g}_{tag}", nu=8 if L == 1 else ATTN_UNITS)
              for g in range(len(ATTN_CONFIGS))]
    os, lses = [o for o, _ in groups], [l for _, l in groups]
    o_attn = _attn_mix(os, lses, s["z_attn"], f"attn_mix_{tag}")
    s.update(os=os, lses=lses, o_attn=o_attn)

    mn = _rmsnorm(mem, p["mem_norm_g"][L], f"mem_norm_{tag}")
    kv = _matmul(mn, wb["w_mem_kv"], mode="nn", name=f"mem_kv_{tag}", out_dtype=BF16, b_lead=L)
    o_mem = _mem_attn(s["qz_mem"], kv, f"mem_attn_{tag}")
    s.update(mn=mn, kv=kv, o_mem=o_mem)

    merged, bps = _merge([o_ssm, o_attn, o_mem], [wb["w_br_ssm"], wb["w_br_attn"], wb["w_br_mem"]], L, s["logits"],
                         p["b_gate"][L], f"merge_{tag}")
    s.update(bps=bps, merged=merged)
    x_new = _matmul(merged, wb["w_out"], mode="nn", name=f"out_{tag}", add=x, b_lead=L)
    return x_new, s


def _layer_bwd(dx, mem, p, wb, L, s, biasm, gprev):
    T, D = dx.shape
    C = p["ssm_d"].shape[1]
    depth = p["norm_g"].shape[0]
    tag = f"l{L}"
    g = {}

    def wgrad(n, a, b, **tiles):
        g[n] = _matmul(a, b, mode="tn", name=f"d{n}_{tag}", out_dtype=BF16, stack=(L, depth, gprev.get(n)), **tiles)

    dmerged = _matmul(dx, wb["w_out"], mode="nt", name=f"d_merged_{tag}", b_lead=L)
    wgrad("w_out", s["merged"], dx)
    dm = s["qz_mem"].shape[1] // 2
    col = dict(zip(("u", "z_ssm", "q", "k", "v", "z_attn", "q_mem", "z_mem", "logits", "end"),
                   (int(o) for o in np.cumsum([0, C, C, 768, 768, 768, 768, dm, dm, 3 * D]))))
    dproj, dbps, g["b_gate"] = _merge_bwd(dmerged, s["bps"], s["logits"], p["b_gate"][L], col["end"], col["logits"],
                                          f"merge_bwd_{tag}")
    dos = []
    for i, (o, n) in enumerate(((s["o_ssm"], "w_br_ssm"), (s["o_attn"], "w_br_attn"), (s["o_mem"], "w_br_mem"))):
        dos.append(_matmul(dbps, wb[n], mode="nt", name=f"d_o_{n}_{tag}", a_lead=i, b_lead=L))
        g[n] = _matmul(o, dbps, mode="tn", name=f"d{n}_{tag}", out_dtype=BF16, b_lead=i, stack=(L, depth, gprev.get(n)))

    dy, dproj, ds_glu, g["b_glu"] = _ssm_post_bwd(dos[0], s["y"], s["z_ssm"], wb["w_glu"][L], p["b_glu"][L], dproj,
                                                  col["z_ssm"], f"ssm_post_bwd_{tag}")
    wgrad("w_glu", s["a_glu"], ds_glu)
    dvec = p["ssm_d"][L].reshape(1, C)
    dproj, dbm, dct, dar, dai, g["ssm_d"] = _ssm_bwd(dy, s["uz"], s["xr"], s["xi"], s["bmat"], s["cmat"], *s["amat"], dvec,
                                                     dproj, f"ssm_scan_bwd_{tag}")
    G, P = p["ssm_lambda_re"].shape[1:]
    Hg = SSM_GROUP
    half = dbm.shape[2] // 2
    dbbr = _block_diag_part(dbm[:, :, :half], Hg, P).reshape(G, Hg, P).transpose(1, 0, 2)
    dbbi = _block_diag_part(dbm[:, :, half:], Hg, P).reshape(G, Hg, P).transpose(1, 0, 2)
    g["ssm_c_re"] = _block_diag_part(dct[:, :, :half], Hg, P).reshape(G, Hg, P)
    g["ssm_c_im"] = -_block_diag_part(dct[:, :, half:], Hg, P).reshape(G, Hg, P)
    glre, glim, gdt, gbr, gbi = _ssm_disc_bwd(*s["disc_in"], dar.reshape(G, P), dai.reshape(G, P), dbbr, dbbi,
                                              name=f"ssm_disc_bwd_{tag}")
    g["ssm_lambda_re"], g["ssm_lambda_im"], g["ssm_log_dt"] = glre, glim, gdt.reshape(G)
    g["ssm_b_re"] = gbr.transpose(1, 2, 0)
    g["ssm_b_im"] = gbi.transpose(1, 2, 0)

    dproj, do_g, dvec_g = _attn_mix_bwd(dos[1], s["os"], s["lses"], s["z_attn"], dproj, col["z_attn"], f"attn_mix_bwd_{tag}")
    rest, dbias = [], []
    for gi in range(len(ATTN_CONFIGS)):
        dq_g, dproj, dv_g, db_g = _attn_group_bwd(s["qkv"], do_g[gi], dvec_g[gi], s["lses"][gi], biasm, gi, dproj, col["k"],
                                                  f"attn_bwd_g{gi}_{tag}", nu=8 if L == 1 else ATTN_UNITS)
        gw = dq_g.shape[1]
        rest += [(dq_g, col["q"] + gi * gw), (dv_g, col["v"] + gi * gw)]
        dbias.append(db_g)
    dbias = jnp.stack(dbias)

    dproj, dz_mem, dk_mem, dv_mem = _mem_attn_bwd(dos[2], s["qz_mem"], s["kv"], dproj, col["q_mem"], f"mem_attn_bwd_{tag}")
    rest.append((dz_mem, col["z_mem"]))
    for piece, at in rest:
        dproj = lax.dynamic_update_slice(dproj, piece.astype(BF16), (0, at))
    dkv = jnp.concatenate([dk_mem, dv_mem], axis=1)
    wgrad("w_mem_kv", s["mn"], dkv)
    dmn = _matmul(dkv, wb["w_mem_kv"], mode="nt", name=f"d_mn_{tag}", b_lead=L)
    _, g["mem_norm_g"] = _rmsnorm_bwd(mem, p["mem_norm_g"][L], dmn, None, f"mem_norm_bwd_{tag}")

    dh = _matmul(dproj, wb["w_in"], mode="nt", name=f"d_h_{tag}", b_lead=L, tm=512, tn=1024)
    wgrad("w_in", s["h"], dproj, tn=2304, tk=1024)
    dx_in, g["norm_g"] = _rmsnorm_bwd(s["x"], p["norm_g"][L], dh, dx, f"norm_bwd_{tag}")
    return dx_in, g, dbias


def _bucket_onehot(gi):
    buckets, bands = _band_tables()
    hit = (buckets[gi].reshape(1, -1) == jnp.arange(NUM_BUCKETS)[:, None]) & bands[gi].reshape(1, -1)
    return hit.astype(BF16)


def _bias_tables(rel_bias, name):
    _, bands = _band_tables()
    out = []
    for gi in range(len(ATTN_CONFIGS)):
        tab = rel_bias[:, gi * HEADS_PER_GROUP:(gi + 1) * HEADS_PER_GROUP].T
        flat = _matmul(tab, _bucket_onehot(gi), mode="nn", name=f"{name}_{gi}", split_a=3, tn=4096)
        out.append(jnp.where(bands[gi][None], flat.reshape(HEADS_PER_GROUP, ATTN_BLOCK, 2 * ATTN_BLOCK), NEG_INF))
    return jnp.stack(out)


def _rel_bias_grad(dbias_sum, name):
    cols = []
    for gi in range(len(ATTN_CONFIGS)):
        flat = dbias_sum[gi].reshape(HEADS_PER_GROUP, -1)
        cols.append(_matmul(flat, _bucket_onehot(gi), mode="nt", name=f"{name}_{gi}", split_a=2, tk=4096).T)
    return jnp.concatenate(cols, axis=1)


def _local_step(x, mem, target, p, wb):
    depth = p["norm_g"].shape[0]
    biasm = _bias_tables(p["rel_bias"], "bias_table")
    saved = []
    for L in range(depth):
        x, s = _layer_fwd(x, mem, p, wb, L, biasm)
        saved.append(s)
    loss_vec, dx, dgf = _loss_head(x, p["final_norm_g"], target, "loss_head")
    grads = {"final_norm_g": dgf.reshape(-1)}
    per_layer = [None] * depth
    dbias_sum = 0.0
    stacked = {}
    for L in reversed(range(depth)):
        dx, per_layer[L], dbias = _layer_bwd(dx, mem, p, wb, L, saved[L], biasm, stacked)
        stacked = {n: per_layer[L][n] for n, _ in BIG}
        dbias_sum = dbias_sum + dbias
    grads.update(stacked)
    for n in per_layer[0]:
        if n not in stacked:
            grads[n] = jnp.stack([per_layer[L][n].reshape(p[n].shape[1:]) for L in range(depth)])
    grads["rel_bias"] = _rel_bias_grad(dbias_sum, "d_rel_bias")
    return jnp.sum(loss_vec), dx, grads


def _chip_coords(j):
    return j // 2, j % 2


def _place_shard(shard, ax, chip, name):
    _, a, b = shard.shape
    ra = _pick(a, 256, 16)
    full = (2, a * N_CHIPS, b) if ax == 1 else (2, a, b * N_CHIPS)
    per = a // ra

    def body(j_ref, s_ref, o_ref):
        o_ref[...] = s_ref[...].astype(BF16)

    out_idx = (lambda l, i, j: (l, j[0] * per + i, 0)) if ax == 1 else (lambda l, i, j: (l, i, j[0]))
    return pl.pallas_call(
        body, name=name,
        grid_spec=pltpu.PrefetchScalarGridSpec(
            num_scalar_prefetch=1, grid=(2, per),
            in_specs=[pl.BlockSpec((None, ra, b), lambda l, i, j: (l, i, 0))],
            out_specs=pl.BlockSpec((None, ra, b), out_idx)),
        out_shape=jax.ShapeDtypeStruct(full, BF16), compiler_params=_params("parallel", "parallel"),
    )(chip, shard)


def _gather_shards(fulls, axes, name):
    n = len(fulls)
    widths = [a.shape[ax] // N_CHIPS for a, ax in zip(fulls, axes)]
    aligns = [LANES if ax == 2 else 16 for ax in axes]

    def body(*refs):
        outs = refs[n:2 * n]
        send_sems, recv_sems, fsend_sems, frecv_sems = refs[2 * n:]
        x, y, c = lax.axis_index("x"), lax.axis_index("y"), lax.axis_index("c")
        mine = 2 * x + y
        sibling = (x, y, 1 - c)

        def window(t, layer, j):
            start = pl.ds(pl.multiple_of(j * widths[t], aligns[t]), widths[t])
            return outs[t].at[(layer, start, slice(None)) if axes[t] == 1 else (layer, slice(None), start)]

        def over_ici(t, j, block):
            return pltpu.make_async_remote_copy(
                src_ref=window(t, c, mine), dst_ref=window(t, c, block), send_sem=send_sems.at[t, j],
                recv_sem=recv_sems.at[t, block], device_id=(*_chip_coords(j), c), device_id_type=MESH)

        def over_d2d(t, j, layer):
            return pltpu.make_async_remote_copy(
                src_ref=window(t, layer, j), dst_ref=window(t, layer, j), send_sem=fsend_sems.at[t, j],
                recv_sem=frecv_sems.at[t, j], device_id=sibling, device_id_type=MESH)

        for t in range(n):
            for j in range(N_CHIPS):
                @pl.when(j != mine)
                def _():
                    over_ici(t, j, mine).start()
        for t in range(n):
            for j in range(N_CHIPS):
                @pl.when(j != mine)
                def _():
                    over_ici(t, j, j).wait_recv()
                    over_d2d(t, j, c).start()
        for t in range(n):
            for j in range(N_CHIPS):
                @pl.when(j != mine)
                def _():
                    over_ici(t, j, mine).wait_send()
                    over_d2d(t, j, c).wait_send()
                    over_d2d(t, j, 1 - c).wait_recv()

    sem = pltpu.SemaphoreType.DMA
    return pl.pallas_call(
        body, name=name, in_specs=[HBM] * n, out_specs=[HBM] * n,
        out_shape=[jax.ShapeDtypeStruct(a.shape, a.dtype) for a in fulls],
        input_output_aliases={t: t for t in range(n)},
        scratch_shapes=[sem((n, N_CHIPS)), sem((n, N_CHIPS)), sem((n, N_CHIPS)), sem((n, N_CHIPS))],
    )(*fulls)


def _scatter_slices(arrays, axes, name):
    n = len(arrays)

    def piece(a, ax):
        if ax is None:
            return a.shape, None
        w = a.shape[ax] // N_CHIPS
        return a.shape[:ax] + (w,) + a.shape[ax + 1:], w

    shapes = [piece(a, ax) for a, ax in zip(arrays, axes)]

    def body(*refs):
        ins, outs = refs[:n], refs[n:2 * n]
        send_sems, recv_sems, loc_sems = refs[2 * n:]
        x, y, c = lax.axis_index("x"), lax.axis_index("y"), lax.axis_index("c")
        mine = 2 * x + y

        def src(t, j):
            ax, w = axes[t], shapes[t][1]
            if ax is None:
                return ins[t]
            idx = tuple(pl.ds(j * w, w) if d == ax else slice(None) for d in range(len(arrays[t].shape)))
            return ins[t].at[idx]

        for t in range(n):
            for j in range(N_CHIPS):
                @pl.when(j == mine)
                def _():
                    pltpu.make_async_copy(src(t, j), outs[t].at[j], loc_sems.at[t]).start()

                @pl.when(j != mine)
                def _():
                    pltpu.make_async_remote_copy(
                        src_ref=src(t, j), dst_ref=outs[t].at[mine], send_sem=send_sems.at[t, j], recv_sem=recv_sems.at[t, mine],
                        device_id=(*_chip_coords(j), c), device_id_type=MESH).start()
        for t in range(n):
            for j in range(N_CHIPS):
                @pl.when(j == mine)
                def _():
                    pltpu.make_async_copy(src(t, j), outs[t].at[j], loc_sems.at[t]).wait()

                @pl.when(j != mine)
                def _():
                    cp = pltpu.make_async_remote_copy(
                        src_ref=src(t, j), dst_ref=outs[t].at[j], send_sem=send_sems.at[t, j], recv_sem=recv_sems.at[t, j],
                        device_id=(*_chip_coords(j), c), device_id_type=MESH)
                    cp.wait_send()
                    cp.wait_recv()

    return pl.pallas_call(
        body, name=name, in_specs=[HBM] * n, out_specs=[HBM] * n,
        out_shape=[jax.ShapeDtypeStruct((N_CHIPS,) + sh, a.dtype) for a, (sh, _) in zip(arrays, shapes)],
        scratch_shapes=[pltpu.SemaphoreType.DMA((n, N_CHIPS)), pltpu.SemaphoreType.DMA((n, N_CHIPS)), pltpu.SemaphoreType.DMA((n,))],
    )(*arrays)


def _swap_layers(stacked, name):
    n = len(stacked)

    def body(*refs):
        ins, outs = refs[:n], refs[n:2 * n]
        send_sems, recv_sems = refs[2 * n:]
        c = lax.axis_index("c")
        peer = (lax.axis_index("x"), lax.axis_index("y"), 1 - c)
        cps = [pltpu.make_async_remote_copy(src_ref=ins[t].at[1 - c], dst_ref=outs[t], send_sem=send_sems.at[t],
                                            recv_sem=recv_sems.at[t], device_id=peer, device_id_type=MESH) for t in range(n)]
        for cp in cps:
            cp.start()
        for cp in cps:
            cp.wait_send()
            cp.wait_recv()

    return pl.pallas_call(
        body, name=name, in_specs=[HBM] * n, out_specs=[HBM] * n,
        out_shape=[jax.ShapeDtypeStruct(a.shape[1:], a.dtype) for a in stacked],
        scratch_shapes=[pltpu.SemaphoreType.DMA((n,)), pltpu.SemaphoreType.DMA((n,))],
    )(*stacked)


def _merge_layers(stacked, name):
    n = len(stacked)

    def body(*refs):
        outs = refs[n:2 * n]
        send_sems, recv_sems = refs[2 * n:]
        c = lax.axis_index("c")
        peer = (lax.axis_index("x"), lax.axis_index("y"), 1 - c)
        for t in range(n):
            pltpu.make_async_remote_copy(src_ref=outs[t].at[c], dst_ref=outs[t].at[c], send_sem=send_sems.at[t],
                                         recv_sem=recv_sems.at[t], device_id=peer, device_id_type=MESH).start()
        for t in range(n):
            cp = pltpu.make_async_remote_copy(src_ref=outs[t].at[c], dst_ref=outs[t].at[1 - c], send_sem=send_sems.at[t],
                                              recv_sem=recv_sems.at[t], device_id=peer, device_id_type=MESH)
            cp.wait_send()
            cp.wait_recv()

    sem = pltpu.SemaphoreType.DMA
    return pl.pallas_call(
        body, name=name, in_specs=[HBM] * n, out_specs=[HBM] * n,
        out_shape=[jax.ShapeDtypeStruct(a.shape, a.dtype) for a in stacked],
        input_output_aliases={t: t for t in range(n)}, scratch_shapes=[sem((n,)), sem((n,))],
    )(*stacked)


def _pair_sum(stacked, landed, core, name):
    _, K, N = stacked.shape
    tr = _pick(K, max(16, (1 << 19) // N // 16 * 16), 16)

    def body(c_ref, s_ref, l_ref, o_ref):
        o_ref[...] = (s_ref[...].astype(F32) + l_ref[...].astype(F32)).astype(o_ref.dtype)

    return pl.pallas_call(
        body, name=name,
        grid_spec=pltpu.PrefetchScalarGridSpec(
            num_scalar_prefetch=1, grid=(K // tr,),
            in_specs=[pl.BlockSpec((None, tr, N), lambda i, c: (c[0], i, 0)), pl.BlockSpec((tr, N), lambda i, c: (i, 0))],
            out_specs=pl.BlockSpec((tr, N), lambda i, c: (i, 0))),
        out_shape=jax.ShapeDtypeStruct((K, N), stacked.dtype), compiler_params=_params("parallel"),
    )(core, stacked, landed)


def _sum_chips(landed, core, name):
    _, R, C = landed.shape
    tr = _pick(R, max(SUBLANES, (1 << 19) // C // 16 * 16), 16)

    def body(c_ref, l_ref, o_ref):
        acc = l_ref[0].astype(F32) + l_ref[1].astype(F32)
        acc = acc + l_ref[2].astype(F32)
        o_ref[...] = acc + l_ref[3].astype(F32)

    return pl.pallas_call(
        body, name=name,
        grid_spec=pltpu.PrefetchScalarGridSpec(
            num_scalar_prefetch=1, grid=(R // tr,),
            in_specs=[pl.BlockSpec((N_CHIPS, tr, C), lambda i, c: (0, i, 0))],
            out_specs=pl.BlockSpec((None, tr, C), lambda i, c: (c[0], i, 0))),
        out_shape=jax.ShapeDtypeStruct((2, R, C), F32), compiler_params=_params("parallel"),
    )(core, landed)


def _adamw_math(w_ref, g_ref, m_ref, v_ref, d_ref, nm_ref, nv_ref):
    c1 = 1.0 / (1.0 - ADAM_B1 ** ADAM_STEP)
    c2 = 1.0 / (1.0 - ADAM_B2 ** ADAM_STEP)
    g = g_ref[...]
    nm = ADAM_B1 * m_ref[...] + (1.0 - ADAM_B1) * g
    nv = ADAM_B2 * v_ref[...] + (1.0 - ADAM_B2) * (g * g)
    nm_ref[...] = nm
    nv_ref[...] = nv
    d_ref[...] = -ADAM_LR * ((nm * c1) / (jnp.sqrt(nv * c2) + ADAM_EPS) + ADAM_WD * w_ref[...])


def _adamw_whole(w, g, m, v, name):
    shape = w.shape
    view = (-1,) + shape[-2:] if w.ndim >= 2 else (1, 1, -1)

    def body(*refs):
        _adamw_math(*refs)

    res = pl.pallas_call(body, name=name, out_shape=[jax.ShapeDtypeStruct(w.reshape(view).shape, F32)] * 3,
                         compiler_params=pltpu.CompilerParams(vmem_limit_bytes=VMEM_LIMIT_BYTES))(
        *(a.reshape(view) for a in (w, g, m, v)))
    return [r.reshape(shape) for r in res]


def _adamw(w, g, m, v, name):
    R, C = w.shape
    tr = _pick(R, max(SUBLANES, (1 << 18) // C // 8 * 8), SUBLANES)

    def body(*refs):
        _adamw_math(*refs)

    blk = pl.BlockSpec((tr, C), lambda i: (i, 0))
    return pl.pallas_call(
        body, name=name, grid=(R // tr,), in_specs=[blk] * 4, out_specs=[blk] * 3,
        out_shape=[jax.ShapeDtypeStruct((R, C), F32)] * 3, compiler_params=_params("parallel"),
    )(w, g, m, v)


def _pack_small(d, prefix=""):
    flat = jnp.concatenate([d[prefix + n].astype(F32).reshape(-1) for n in SMALL])
    pad = (-flat.shape[0]) % (2 * 16 * LANES)
    return jnp.pad(flat, (0, pad)).reshape(-1, LANES)


def _unpack_small(packed, shapes):
    flat = packed.reshape(-1)
    out, off = {}, 0
    for n in SMALL:
        size = int(np.prod(shapes[n]))
        out[n] = flat[off:off + size].reshape(shapes[n])
        off += size
    return out


def kernel(*args):
    p = dict(zip(INPUTS, args))
    x, mem, target = p["x"][0], p["mem"][0], p["loss_target"][0]

    names = [n for n, _ in BIG] + ["small"]
    core = lax.axis_index("c").astype(jnp.int32).reshape(1)
    chip = (2 * lax.axis_index("x") + lax.axis_index("y")).astype(jnp.int32).reshape(1)
    placed = [_place_shard(p[n], ax, chip, f"place_{n}") for n, ax in BIG]
    wb = dict(zip(names, _gather_shards(placed, [ax for _, ax in BIG], "gather_weights")))

    loss_part, dx, grads = _local_step(x, mem, target, p, wb)
    loss = lax.psum(loss_part, ("x", "y", "c"))

    stacked = [grads[n] for n, _ in BIG] + [_pack_small(grads).reshape(2, -1, LANES)]
    theirs = _swap_layers(stacked, "swap_layers")
    pair = [_pair_sum(s, o, core, f"pair_sum_{n}") for n, s, o in zip(names, stacked, theirs)]
    landed = _scatter_slices(pair, [ax - 1 for _, ax in BIG] + [None], "scatter_grads")
    reduced = [_sum_chips(ld.reshape(N_CHIPS, -1, ld.shape[-1]), core, f"sum_chips_{n}") for n, ld in zip(names, landed)]
    total = _merge_layers(reduced, "merge_layers")

    out = {}
    for (n, _), g in zip(BIG, total):
        sh = p[n].shape
        two_d = lambda a: a.reshape(-1, sh[-1])
        res = (g,) + tuple(_adamw(two_d(p[n]), two_d(g), two_d(p["m_" + n]), two_d(p["v_" + n]), f"adamw_{n}"))
        for key, r in zip(("grad_", "delta_", "new_m_", "new_v_"), res):
            out[key + n] = r.reshape(sh)
    for n, g in _unpack_small(total[-1], {n: p[n].shape for n in SMALL}).items():
        res = (g,) + tuple(_adamw_whole(p[n], g, p["m_" + n], p["v_" + n], f"adamw_{n}"))
        for key, r in zip(("grad_", "delta_", "new_m_", "new_v_"), res):
            out[key + n] = r

    result = [loss, dx.reshape(p["x"].shape)]
    for key in ("grad_", "delta_", "new_m_", "new_v_"):
        result += [out[key + n] for n in WEIGHTS]
    return tuple(result)
```

```python
import math

import jax
import jax.numpy as jnp
import numpy as np
from jax import lax
from jax.experimental import pallas as pl
from jax.experimental.pallas import tpu as pltpu

F32 = jnp.float32
BF16 = jnp.bfloat16
MESH = pl.DeviceIdType.MESH
HBM = pl.BlockSpec(memory_space=pltpu.HBM)

EPS = 1e-6
SSM_GROUP = 16
SSM_STATE = 64
ATTN_HEAD_DIM = 64
HEADS_PER_GROUP = 4
ATTN_CONFIGS = ((128, 1), (512, 4), (2048, 16))
ATTN_BLOCK = 128
NUM_BUCKETS = 32
REL_MAX_DISTANCE = 2048
NEG_INF = -1e30
MEM_HEADS = 4
ADAM_LR = 0.001
ADAM_B1 = 0.9
ADAM_B2 = 0.999
ADAM_EPS = 1e-08
ADAM_WD = 0.01
ADAM_STEP = 10

LANES = 128
SUBLANES = 8
VMEM_LIMIT_BYTES = 48 * 1024 * 1024
SSM_BLOCK_CH = 128

N_CHIPS = 4
BIG = (("w_in", 2), ("w_glu", 1), ("w_mem_kv", 1), ("w_br_ssm", 2), ("w_br_attn", 2), ("w_br_mem", 2), ("w_out", 1))
SMALL = ("norm_g", "mem_norm_g", "b_gate", "ssm_lambda_re", "ssm_lambda_im", "ssm_log_dt", "ssm_b_re", "ssm_b_im",
         "ssm_c_re", "ssm_c_im", "ssm_d", "b_glu", "rel_bias", "final_norm_g")
WEIGHTS = ("norm_g", "mem_norm_g", "w_in", "b_gate", "ssm_lambda_re", "ssm_lambda_im", "ssm_log_dt", "ssm_b_re",
           "ssm_b_im", "ssm_c_re", "ssm_c_im", "ssm_d", "w_glu", "b_glu", "w_mem_kv", "w_br_ssm", "w_br_attn",
           "w_br_mem", "w_out", "rel_bias", "final_norm_g")
INPUTS = ("x", "mem") + WEIGHTS + ("loss_target",) + tuple("m_" + n for n in WEIGHTS) + tuple("v_" + n for n in WEIGHTS)


def _params(*sem):
    return pltpu.CompilerParams(dimension_semantics=sem, vmem_limit_bytes=VMEM_LIMIT_BYTES)


def _pick(dim, pref, align):
    if dim <= pref:
        return dim
    t = pref - pref % align
    while t >= align:
        if dim % t == 0:
            return t
        t -= align
    return dim


def _sigmoid(v):
    return 0.5 * jnp.tanh(0.5 * v) + 0.5


def _silu_and_grad(z):
    s = _sigmoid(z)
    return z * s, s * (1.0 + z * (1.0 - s))


_GELU_C = math.sqrt(2.0 / math.pi)


def _gelu_and_grad(y):
    inner = _GELU_C * (y + 0.044715 * y * y * y)
    t = jnp.tanh(inner)
    g = 0.5 * y * (1.0 + t)
    dg = 0.5 * (1.0 + t) + 0.5 * y * (1.0 - t * t) * _GELU_C * (1.0 + 3.0 * 0.044715 * y * y)
    return g, dg


def _dot(a, b, dims):
    return lax.dot_general(a, b, (dims, ((), ())), preferred_element_type=F32)


NN = ((1,), (0,))
NT = ((1,), (1,))
TN = ((0,), (0,))


def _matmul(a, b, *, mode, name, out_dtype=F32, add=None, split_a=1, tm=1024, tn=768, tk=2304,
            a_lead=None, b_lead=None, b_off=0, n_cols=None, stack=None):
    ashape = a.shape if a_lead is None else a.shape[1:]
    K, M = ashape if mode == "tn" else ashape[::-1]
    bshape = b.shape if b_lead is None else b.shape[1:]
    N = n_cols or (bshape[0] if mode == "nt" else bshape[1])
    if mode != "tn" and M >= 4 * tm:
        tm = 2 * tm
    tm = _pick(M, tm, LANES if mode == "tn" else SUBLANES)
    tn = _pick(math.gcd(N, b_off) if b_off else N, tn, LANES)
    tk = _pick(K, tk, LANES)
    nk = K // tk
    joff = b_off // tn
    dims = {"nn": NN, "nt": NT, "tn": TN}[mode]
    has_add = add is not None
    has_prev = stack is not None and stack[2] is not None

    def body(*refs):
        a_ref, b_ref = refs[:2]
        add_ref = refs[2] if has_add else None
        o_ref = refs[-2] if nk > 1 else refs[-1]
        k = pl.program_id(2)
        bv = b_ref[...].astype(BF16)
        if split_a > 1:
            rest = a_ref[...].astype(F32)
            part = 0.0
            for _ in range(split_a):
                piece = rest.astype(BF16)
                part = part + _dot(piece, bv, dims)
                rest = rest - piece.astype(F32)
        else:
            part = _dot(a_ref[...].astype(BF16), bv, dims)

        def finish(r):
            if has_add:
                r = r + add_ref[...]
            o_ref[...] = r.astype(out_dtype)

        if nk == 1:
            finish(part)
            return
        acc_ref = refs[-1]

        @pl.when(k == 0)
        def _():
            acc_ref[...] = part

        @pl.when((k > 0) & (k < nk - 1))
        def _():
            acc_ref[...] += part

        @pl.when(k == nk - 1)
        def _():
            finish(acc_ref[...] + part)

    alead = () if a_lead is None else (a_lead,)
    alead_blk = () if a_lead is None else (None,)
    if mode == "tn":
        a_spec = pl.BlockSpec(alead_blk + (tk, tm), lambda i, j, k: alead + (k, i))
    else:
        a_spec = pl.BlockSpec(alead_blk + (tm, tk), lambda i, j, k: alead + (i, k))
    lead = () if b_lead is None else (b_lead,)
    lead_blk = () if b_lead is None else (None,)
    if mode == "nt":
        b_spec = pl.BlockSpec(lead_blk + (tn, tk), lambda i, j, k: lead + (j + joff, k))
    else:
        b_spec = pl.BlockSpec(lead_blk + (tk, tn), lambda i, j, k: lead + (k, j + joff))
    in_specs = [a_spec, b_spec]
    args = [a, b]
    if has_add:
        in_specs.append(pl.BlockSpec((tm, tn), lambda i, j, k: (i, j)))
        args.append(add)
    aliases = {}
    if stack is None:
        out_spec = pl.BlockSpec((tm, tn), lambda i, j, k: (i, j))
        out_shape = jax.ShapeDtypeStruct((M, N), out_dtype)
    else:
        layer, depth, prev = stack
        out_spec = pl.BlockSpec((None, tm, tn), lambda i, j, k: (layer, i, j))
        out_shape = jax.ShapeDtypeStruct((depth, M, N), out_dtype)
        if has_prev:
            in_specs.append(pl.BlockSpec(memory_space=pl.ANY))
            args.append(prev)
            aliases = {len(args) - 1: 0}
    return pl.pallas_call(
        body, name=name, grid=(M // tm, N // tn, nk), in_specs=in_specs, out_specs=out_spec, out_shape=out_shape,
        scratch_shapes=[pltpu.VMEM((tm, tn), F32)] if nk > 1 else [], input_output_aliases=aliases,
        compiler_params=_params("parallel", "parallel", "arbitrary"),
    )(*args)


def _rmsnorm(x, g, name):
    T, D = x.shape
    tm = _pick(T, 512, SUBLANES)

    def body(x_ref, g_ref, h_ref):
        xv = x_ref[...]
        r = lax.rsqrt(jnp.mean(xv * xv, axis=-1, keepdims=True) + EPS)
        h_ref[...] = (xv * r * g_ref[...]).astype(BF16)

    return pl.pallas_call(
        body, name=name, grid=(T // tm,),
        in_specs=[pl.BlockSpec((tm, D), lambda i: (i, 0)), pl.BlockSpec((1, D), lambda i: (0, 0))],
        out_specs=pl.BlockSpec((tm, D), lambda i: (i, 0)),
        out_shape=jax.ShapeDtypeStruct((T, D), BF16), compiler_params=_params("parallel"),
    )(x, g.reshape(1, D))


def _rmsnorm_bwd(x, g, dh, dres, name):
    T, D = x.shape
    tm = _pick(T, 512, SUBLANES)
    with_res = dres is not None

    def body(*refs):
        if with_res:
            x_ref, g_ref, dh_ref, dres_ref, dx_ref, dg_ref = refs
        else:
            x_ref, g_ref, dh_ref, dx_ref, dg_ref = refs
        xv = x_ref[...]
        dhv = dh_ref[...]
        r = lax.rsqrt(jnp.mean(xv * xv, axis=-1, keepdims=True) + EPS)
        dyg = dhv * g_ref[...]
        c = jnp.mean(dyg * xv, axis=-1, keepdims=True)
        dx = r * dyg - xv * (r * r * r) * c
        if with_res:
            dx = dx + dres_ref[...]
        dx_ref[...] = dx

        @pl.when(pl.program_id(0) == 0)
        def _():
            dg_ref[...] = jnp.zeros_like(dg_ref)

        dg_ref[...] += jnp.sum(dhv * xv * r, axis=0, keepdims=True)

    row = pl.BlockSpec((tm, D), lambda i: (i, 0))
    vec = pl.BlockSpec((1, D), lambda i: (0, 0))
    ins = [x, g.reshape(1, D), dh] + ([dres] if with_res else [])
    return pl.pallas_call(
        body, name=name, grid=(T // tm,), in_specs=[row, vec, row] + ([row] if with_res else []),
        out_specs=[row, vec],
        out_shape=[jax.ShapeDtypeStruct((T, D), F32), jax.ShapeDtypeStruct((1, D), F32)],
        compiler_params=_params("arbitrary"),
    )(*ins)


def _loss_head(x, g, target, name):
    T, D = x.shape
    tm = _pick(T, 512, SUBLANES)

    def body(x_ref, g_ref, t_ref, loss_ref, dx_ref, dg_ref):
        xv = x_ref[...]
        gv = g_ref[...]
        r = lax.rsqrt(jnp.mean(xv * xv, axis=-1, keepdims=True) + EPS)
        e = xv * r * gv - t_ref[...]
        dy = e * (1.0 / D)
        dyg = dy * gv
        c = jnp.mean(dyg * xv, axis=-1, keepdims=True)
        dx_ref[...] = r * dyg - xv * (r * r * r) * c

        @pl.when(pl.program_id(0) == 0)
        def _():
            loss_ref[...] = jnp.zeros_like(loss_ref)
            dg_ref[...] = jnp.zeros_like(dg_ref)

        loss_ref[...] += jnp.sum(e * e, axis=0, keepdims=True) * (0.5 / D)
        dg_ref[...] += jnp.sum(dy * xv * r, axis=0, keepdims=True)

    row = pl.BlockSpec((tm, D), lambda i: (i, 0))
    vec = pl.BlockSpec((1, D), lambda i: (0, 0))
    return pl.pallas_call(
        body, name=name, grid=(T // tm,), in_specs=[row, vec, row], out_specs=[vec, row, vec],
        out_shape=[jax.ShapeDtypeStruct((1, D), F32), jax.ShapeDtypeStruct((T, D), F32), jax.ShapeDtypeStruct((1, D), F32)],
        compiler_params=_params("arbitrary"),
    )(x, g.reshape(1, D), target)


def _ssm_disc_math(lre, lim, logdt, br, bi):
    dt = jnp.exp(logdt)
    mag = jnp.exp(lre * dt)
    ar = mag * jnp.cos(lim * dt)
    ai = mag * jnp.sin(lim * dt)
    den = lre * lre + lim * lim
    nr = ar - 1.0
    fr = (nr * lre + ai * lim) / den
    fi = (ai * lre - nr * lim) / den
    return ar, ai, fr[None] * br - fi[None] * bi, fr[None] * bi + fi[None] * br


def _ssm_disc(lre, lim, logdt, br, bi, name):
    def body(lre_ref, lim_ref, dt_ref, br_ref, bi_ref, ar_ref, ai_ref, bbr_ref, bbi_ref):
        ar, ai, bbr, bbi = _ssm_disc_math(lre_ref[...], lim_ref[...], dt_ref[...], br_ref[...], bi_ref[...])
        ar_ref[...] = ar
        ai_ref[...] = ai
        bbr_ref[...] = bbr
        bbi_ref[...] = bbi

    sd = jax.ShapeDtypeStruct
    return pl.pallas_call(
        body, name=name, out_shape=[sd(lre.shape, F32), sd(lre.shape, F32), sd(br.shape, F32), sd(br.shape, F32)],
    )(lre, lim, logdt, br, bi)


def _ssm_disc_bwd(lre, lim, logdt, br, bi, dar, dai, dbbr, dbbi, name):
    def body(lre_ref, lim_ref, dt_ref, br_ref, bi_ref, dar_ref, dai_ref, dbbr_ref, dbbi_ref,
             glre_ref, glim_ref, gdt_ref, gbr_ref, gbi_ref):
        _, vjp = jax.vjp(_ssm_disc_math, lre_ref[...], lim_ref[...], dt_ref[...], br_ref[...], bi_ref[...])
        glre, glim, gdt, gbr, gbi = vjp((dar_ref[...], dai_ref[...], dbbr_ref[...], dbbi_ref[...]))
        glre_ref[...] = glre
        glim_ref[...] = glim
        gdt_ref[...] = gdt
        gbr_ref[...] = gbr
        gbi_ref[...] = gbi

    sd = jax.ShapeDtypeStruct
    return pl.pallas_call(
        body, name=name,
        out_shape=[sd(lre.shape, F32), sd(lre.shape, F32), sd(logdt.shape, F32), sd(br.shape, F32), sd(br.shape, F32)],
    )(lre, lim, logdt, br, bi, dar, dai, dbbr, dbbi)


SSM_STEPS_FWD = 256
SSM_STEPS_BWD = 256


def _ssm_tiles(ref, v, off, steps, n):
    return [ref[v, pl.ds(off + j, steps, stride=SUBLANES), :] for j in range(n)]


def _ssm_fwd(uz, bmat, cmat, art, ait, dvec, name, steps=SSM_STEPS_FWD):
    T = uz.shape[0]
    nblk, cb, width = bmat.shape
    C = nblk * cb
    half = width // 2
    nt = half // LANES
    npair = nblk // 2
    kc = min(steps, T)
    nchunk = T // kc

    def body(u_ref, b_ref, c_ref, ar_ref, ai_ref, d_ref, y_ref, xr_ref, xi_ref, sr_ref, si_ref):
        @pl.when(pl.program_id(0) == 0)
        def _():
            sr_ref[...] = jnp.zeros_like(sr_ref)
            si_ref[...] = jnp.zeros_like(si_ref)

        uv = u_ref[...]
        for b in range(nblk):
            bu = _dot(uv[:, b * cb:(b + 1) * cb].astype(BF16), b_ref[b], NN)
            v, off = b // 2, nt * (b % 2)
            for j in range(nt):
                xr_ref[v, pl.ds(off + j, kc, stride=SUBLANES), :] = bu[:, j * LANES:(j + 1) * LANES]
                xi_ref[v, pl.ds(off + j, kc, stride=SUBLANES), :] = bu[:, half + j * LANES:half + (j + 1) * LANES]
        ars = [ar_ref[v] for v in range(npair)]
        ais = [ai_ref[v] for v in range(npair)]

        def step(k, carry):
            row = pl.ds(k * SUBLANES, SUBLANES)
            out = []
            for v in range(npair):
                xr, xi = carry[2 * v], carry[2 * v + 1]
                nr = ars[v] * xr - ais[v] * xi + xr_ref[v, row, :]
                ni = ars[v] * xi + ais[v] * xr + xi_ref[v, row, :]
                xr_ref[v, row, :] = nr
                xi_ref[v, row, :] = ni
                out += [nr, ni]
            return tuple(out)

        fin = tuple(ref[v] for v in range(npair) for ref in (sr_ref, si_ref))
        for k in range(kc):
            fin = step(k, fin)
        for v in range(npair):
            sr_ref[v] = fin[2 * v]
            si_ref[v] = fin[2 * v + 1]
        for b in range(nblk):
            v, off = b // 2, nt * (b % 2)
            xb = jnp.concatenate(_ssm_tiles(xr_ref, v, off, kc, nt) + _ssm_tiles(xi_ref, v, off, kc, nt), axis=1)
            cols = slice(b * cb, (b + 1) * cb)
            y_ref[:, cols] = _dot(xb.astype(BF16), c_ref[b], NN) + d_ref[:, cols] * uv[:, cols]

    whole = lambda a: pl.BlockSpec(a.shape, lambda c: (0,) * a.ndim)
    st = pl.BlockSpec((npair, kc * SUBLANES, LANES), lambda c: (0, c, 0))
    sd = jax.ShapeDtypeStruct
    return pl.pallas_call(
        body, name=name, grid=(nchunk,),
        in_specs=[pl.BlockSpec((kc, C), lambda c: (c, 0)), whole(bmat), whole(cmat), whole(art), whole(ait), whole(dvec)],
        out_specs=[pl.BlockSpec((kc, C), lambda c: (c, 0)), st, st],
        out_shape=[sd((T, C), F32), sd((npair, T * SUBLANES, LANES), F32), sd((npair, T * SUBLANES, LANES), F32)],
        scratch_shapes=[pltpu.VMEM((npair, SUBLANES, LANES), F32), pltpu.VMEM((npair, SUBLANES, LANES), F32)],
        compiler_params=_params("arbitrary"),
    )(uz, bmat, cmat, art, ait, dvec)


def _ssm_bwd(dy, uz, xr, xi, bmat, cmat, art, ait, dvec, dproj, name):
    T = uz.shape[0]
    nblk, cb, width = bmat.shape
    C = nblk * cb
    half = width // 2
    nt = half // LANES
    npair = nblk // 2
    kc = min(SSM_STEPS_BWD, T)
    nchunk = T // kc

    def body(dy_ref, u_ref, xr_ref, xi_ref, xpr_ref, xpi_ref, b_ref, c_ref, ar_ref, ai_ref, d_ref, _,
             du_ref, db_ref, dc_ref, dar_ref, dai_ref, dd_ref, gr_ref, gi_ref, sr_ref, si_ref):
        c = pl.program_id(0)

        @pl.when(c == 0)
        def _():
            for ref in (sr_ref, si_ref, db_ref, dc_ref, dar_ref, dai_ref, dd_ref):
                ref[...] = jnp.zeros_like(ref)

        dyv = dy_ref[...]
        uv = u_ref[...]
        for b in range(nblk):
            dx = _dot(dyv[:, b * cb:(b + 1) * cb].astype(BF16), c_ref[b], NT)
            v, off = b // 2, nt * (b % 2)
            for j in range(nt):
                gr_ref[v, pl.ds(off + j, kc, stride=SUBLANES), :] = dx[:, j * LANES:(j + 1) * LANES]
                gi_ref[v, pl.ds(off + j, kc, stride=SUBLANES), :] = dx[:, half + j * LANES:half + (j + 1) * LANES]
        ars = [ar_ref[v] for v in range(npair)]
        ais = [ai_ref[v] for v in range(npair)]

        def pair_update(v, gr, gi, row):
            nr = ars[v] * gr + ais[v] * gi + gr_ref[v, row, :]
            ni = ars[v] * gi - ais[v] * gr + gi_ref[v, row, :]
            gr_ref[v, row, :] = nr
            gi_ref[v, row, :] = ni
            return nr, ni

        def step(i, carry):
            k = kc - 1 - i
            row = pl.ds(k * SUBLANES, SUBLANES)
            prow = pl.ds((k - 1) * SUBLANES, SUBLANES)
            out = []
            for v in range(npair):
                gr, gi, sr, si = carry[4 * v:4 * v + 4]
                nr, ni = pair_update(v, gr, gi, row)
                pr, pi = xr_ref[v, prow, :], xi_ref[v, prow, :]
                out += [nr, ni, sr + pr * nr + pi * ni, si + pr * ni - pi * nr]
            return tuple(out)

        mid = tuple(ref[v] for v in range(npair) for ref in (sr_ref, si_ref, dar_ref, dai_ref))
        for i in range(kc - 1):
            mid = step(i, mid)
        live = (c < nchunk - 1).astype(F32)
        row0 = pl.ds(0, SUBLANES)
        for v in range(npair):
            gr, gi, sr, si = mid[4 * v:4 * v + 4]
            nr, ni = pair_update(v, gr, gi, row0)
            pr, pi = xpr_ref[v] * live, xpi_ref[v] * live
            sr_ref[v] = nr
            si_ref[v] = ni
            dar_ref[v] = sr + pr * nr + pi * ni
            dai_ref[v] = si + pr * ni - pi * nr
        for b in range(nblk):
            v, off = b // 2, nt * (b % 2)
            cols = slice(b * cb, (b + 1) * cb)
            gb = jnp.concatenate(_ssm_tiles(gr_ref, v, off, kc, nt) + _ssm_tiles(gi_ref, v, off, kc, nt), axis=1).astype(BF16)
            xb = jnp.concatenate(_ssm_tiles(xr_ref, v, off, kc, nt) + _ssm_tiles(xi_ref, v, off, kc, nt), axis=1).astype(BF16)
            du_ref[:, cols] = (_dot(gb, b_ref[b], NT) + dyv[:, cols] * d_ref[:, cols]).astype(BF16)
            db_ref[b] += _dot(uv[:, cols].astype(BF16), gb, TN)
            dc_ref[b] += _dot(dyv[:, cols].astype(BF16), xb, TN)
        dd_ref[...] += jnp.sum(dyv * uv, axis=0, keepdims=True)

    whole = lambda a: pl.BlockSpec(a.shape, lambda c: (0,) * a.ndim)
    rev = lambda c: (nchunk - 1 - c, 0)
    st = pl.BlockSpec((npair, kc * SUBLANES, LANES), lambda c: (0, nchunk - 1 - c, 0))
    stp = pl.BlockSpec((npair, SUBLANES, LANES), lambda c: (0, jnp.maximum((nchunk - 1 - c) * kc - 1, 0), 0))
    acc = lambda shape: pl.BlockSpec(shape, lambda c: (0,) * len(shape))
    sd = jax.ShapeDtypeStruct
    pair_shape = (npair, SUBLANES, LANES)
    return pl.pallas_call(
        body, name=name, grid=(nchunk,),
        in_specs=[pl.BlockSpec((kc, C), rev), pl.BlockSpec((kc, C), rev), st, st, stp, stp, whole(bmat), whole(cmat),
                  whole(art), whole(ait), whole(dvec), pl.BlockSpec(memory_space=pl.ANY)],
        out_specs=[pl.BlockSpec((kc, C), rev), acc(bmat.shape), acc(bmat.shape), acc(pair_shape), acc(pair_shape), acc((1, C))],
        out_shape=[sd(dproj.shape, BF16), sd(bmat.shape, F32), sd(bmat.shape, F32), sd(pair_shape, F32), sd(pair_shape, F32),
                   sd((1, C), F32)],
        scratch_shapes=[pltpu.VMEM((npair, kc * SUBLANES, LANES), F32), pltpu.VMEM((npair, kc * SUBLANES, LANES), F32),
                        pltpu.VMEM(pair_shape, F32), pltpu.VMEM(pair_shape, F32)],
        input_output_aliases={11: 0}, compiler_params=_params("arbitrary"),
    )(dy, uz, xr, xi, xr, xi, bmat, cmat, art, ait, dvec, dproj)


def _ssm_post(y, z, w_glu, b_glu, name):
    T, C = y.shape
    tm = _pick(T, 512, SUBLANES)

    def body(y_ref, z_ref, w_ref, b_ref, o_ref, a_ref):
        a, _ = _gelu_and_grad(y_ref[...])
        ab = a.astype(BF16)
        sg = _sigmoid(_dot(ab, w_ref[...], NN) + b_ref[...])
        sz, _ = _silu_and_grad(z_ref[...].astype(F32))
        o_ref[...] = (a * sg * sz).astype(BF16)
        a_ref[...] = ab

    row = pl.BlockSpec((tm, C), lambda i: (i, 0))
    return pl.pallas_call(
        body, name=name, grid=(T // tm,),
        in_specs=[row, row, pl.BlockSpec((C, C), lambda i: (0, 0)), pl.BlockSpec((1, C), lambda i: (0, 0))],
        out_specs=[row, row], out_shape=[jax.ShapeDtypeStruct((T, C), BF16)] * 2, compiler_params=_params("parallel"),
    )(y, z, w_glu, b_glu.reshape(1, C))


def _ssm_post_bwd(do, y, z, w_glu, b_glu, dproj, col, name):
    T, C = y.shape
    tm = _pick(T, 512, SUBLANES)

    def body(do_ref, y_ref, z_ref, w_ref, b_ref, _, dy_ref, dz_ref, ds_ref, db_ref):
        dov = do_ref[...]
        a, da_dy = _gelu_and_grad(y_ref[...])
        sg = _sigmoid(_dot(a.astype(BF16), w_ref[...], NN) + b_ref[...])
        sz, dsz = _silu_and_grad(z_ref[...].astype(F32))
        yg = a * sg
        dz_ref[...] = (dov * yg * dsz).astype(BF16)
        dyg = dov * sz
        ds = dyg * a * sg * (1.0 - sg)
        dsb = ds.astype(BF16)
        ds_ref[...] = dsb
        da = dyg * sg + _dot(dsb, w_ref[...], NT)
        dy_ref[...] = da * da_dy

        @pl.when(pl.program_id(0) == 0)
        def _():
            db_ref[...] = jnp.zeros_like(db_ref)

        db_ref[...] += jnp.sum(ds, axis=0, keepdims=True)

    row = pl.BlockSpec((tm, C), lambda i: (i, 0))
    vec = pl.BlockSpec((1, C), lambda i: (0, 0))
    sd = jax.ShapeDtypeStruct
    return pl.pallas_call(
        body, name=name, grid=(T // tm,),
        in_specs=[row, row, row, pl.BlockSpec((C, C), lambda i: (0, 0)), vec, pl.BlockSpec(memory_space=pl.ANY)],
        out_specs=[row, pl.BlockSpec((tm, C), lambda i: (i, col // C)), row, vec],
        out_shape=[sd((T, C), F32), sd(dproj.shape, BF16), sd((T, C), BF16), sd((1, C), F32)],
        input_output_aliases={5: 1}, compiler_params=_params("arbitrary"),
    )(do, y, z, w_glu, b_glu.reshape(1, C), dproj)


def _rel_bucket(dist):
    n = jnp.maximum(dist, 0)
    max_exact = NUM_BUCKETS // 2
    n_f = jnp.maximum(n, 1).astype(F32)
    large = max_exact + (jnp.log(n_f / max_exact) / math.log(REL_MAX_DISTANCE / max_exact)
                         * (NUM_BUCKETS - max_exact)).astype(jnp.int32)
    large = jnp.minimum(large, NUM_BUCKETS - 1)
    return jnp.where(n < max_exact, n, large)


def _band_tables():
    qi = jnp.arange(ATTN_BLOCK)[:, None]
    kj = jnp.arange(2 * ATTN_BLOCK)[None, :]
    delta = ATTN_BLOCK + qi - kj
    buckets, bands = [], []
    for window, dilation in ATTN_CONFIGS:
        bands.append((delta >= 0) & (delta <= window // dilation))
        buckets.append(_rel_bucket(jnp.maximum(delta, 0) * dilation))
    return jnp.stack(buckets), jnp.stack(bands)


ATTN_UNITS = 4


def _attn_units_for(g, L):
    if ATTN_CONFIGS[g][1] >= 16:
        return ATTN_UNITS
    return 16 if L == 1 else 8


def _attn_tile(T, r, nu):
    nq = max(1, nu // r)
    rows = ATTN_BLOCK * r * nq
    return nq, rows, T // rows


def _attn_units(r, nq, chunk, nu):
    if r >= nu:
        return [(chunk * nu + i, None) for i in range(nu)]
    units = []
    for j in range(nq):
        for s in range(r):
            units.append((ATTN_BLOCK * j * r + s, ATTN_BLOCK * (j - 1) * r + s if j else None))
    return units


def _rows(start, r):
    return pl.ds(start, ATTN_BLOCK, stride=r) if r > 1 else pl.ds(start, ATTN_BLOCK)


def _attn_group_fwd(qkv, biasm, g, name, nu=ATTN_UNITS):
    T = qkv.shape[0]
    r = ATTN_CONFIGS[g][1]
    B, hd = ATTN_BLOCK, ATTN_HEAD_DIM
    nq, rows, ntiles = _attn_tile(T, r, nu)
    nchunks = max(1, r // nu)
    last_prev = B * (nq - 1) * r
    scale = hd ** -0.5
    tiles_per_tensor = 3 * HEADS_PER_GROUP * hd // LANES

    def body(q_ref, kc_ref, kp_ref, vc_ref, vp_ref, bias_ref, o_ref, lse_ref, s_ref, p_ref):
        n = pl.program_id(1)
        lane = lax.broadcasted_iota(jnp.int32, (1, LANES), 1)
        col = lax.broadcasted_iota(jnp.int32, (1, 2 * B), 1)
        masks = [lane < hd, lane >= hd]
        first_pen = jnp.where((col < B) & (n == 0), NEG_INF, 0.0)

        def chunk_body(chunk):
            units = _attn_units(r, nq, chunk, nu)

            def keys(cur_ref, prev_ref, cs, ps):
                prev = prev_ref[_rows(last_prev + (cs if r >= nu else cs % r), r), :] if ps is None else cur_ref[_rows(ps, r), :]
                return jnp.concatenate([prev, cur_ref[_rows(cs, r), :]], axis=0).astype(BF16)

            for u, (cs, ps) in enumerate(units):
                qv = q_ref[_rows(cs, r), :]
                kw = keys(kc_ref, kp_ref, cs, ps)
                for hh in range(2):
                    s_ref[2 * u + hh] = _dot(jnp.where(masks[hh], qv, 0.0).astype(BF16), kw, NT)
            for u, (cs, ps) in enumerate(units):
                lses = []
                for hh in range(2):
                    s = s_ref[2 * u + hh] * scale + bias_ref[hh]
                    if ps is None:
                        s = s + first_pen
                    m = jnp.max(s, axis=-1, keepdims=True)
                    p = jnp.exp(s - m)
                    l = jnp.sum(p, axis=-1, keepdims=True)
                    p_ref[2 * u + hh] = (p * (1.0 / l)).astype(BF16)
                    lses.append(m + jnp.log(l))
                lse_ref[_rows(cs, r), :] = jnp.where(masks[0], lses[0], lses[1])
            for u, (cs, ps) in enumerate(units):
                vw = keys(vc_ref, vp_ref, cs, ps)
                o_ref[_rows(cs, r), :] = (_dot(p_ref[2 * u], jnp.where(masks[0], vw, 0), NN)
                                          + _dot(p_ref[2 * u + 1], jnp.where(masks[1], vw, 0), NN))

        if nchunks == 1:
            chunk_body(0)
        else:
            pl.loop(0, nchunks)(chunk_body)

    def cur(t):
        return pl.BlockSpec((rows, LANES), lambda hf, n: (n, t * tiles_per_tensor + 2 * g + hf))

    def prev(t):
        return pl.BlockSpec((rows, LANES), lambda hf, n: (jnp.maximum(n - 1, 0), t * tiles_per_tensor + 2 * g + hf))

    out = pl.BlockSpec((rows, LANES), lambda hf, n: (n, hf))
    sd = jax.ShapeDtypeStruct((T, 2 * LANES), F32)
    return pl.pallas_call(
        body, name=name, grid=(2, ntiles),
        in_specs=[cur(0), cur(1), prev(1), cur(2), prev(2), pl.BlockSpec((None, 2, B, 2 * B), lambda hf, n: (g, hf, 0, 0))],
        out_specs=[out, out], out_shape=[sd, sd],
        scratch_shapes=[pltpu.VMEM((2 * nu, B, 2 * B), F32), pltpu.VMEM((2 * nu, B, 2 * B), BF16)],
        compiler_params=_params("parallel", "parallel"),
    )(qkv, qkv, qkv, qkv, qkv, biasm)


def _attn_group_bwd(qkv, do, dvec, lse, biasm, g, dproj, dk_col, name, nu=ATTN_UNITS):
    T = qkv.shape[0]
    r = ATTN_CONFIGS[g][1]
    B, hd = ATTN_BLOCK, ATTN_HEAD_DIM
    nq, rows, ntiles = _attn_tile(T, r, nu)
    nchunks = max(1, r // nu)
    last_prev = B * (nq - 1) * r
    scale = hd ** -0.5
    tiles_per_tensor = 3 * HEADS_PER_GROUP * hd // LANES

    def body(q_ref, kc_ref, kp_ref, vc_ref, vp_ref, do_ref, dv_ref, lse_ref, bias_ref, _,
             dq_ref, dk_ref, dvo_ref, dbias_ref, ck_ref, cv_ref, ak_ref, av_ref, s_ref, dp_ref, p_ref, ds_ref):
        n = pl.program_id(1)
        lane = lax.broadcasted_iota(jnp.int32, (1, LANES), 1)
        col = lax.broadcasted_iota(jnp.int32, (1, 2 * B), 1)
        masks = [lane < hd, lane >= hd]
        first_pen = jnp.where((col < B) & (n == 0), NEG_INF, 0.0)

        @pl.when(n == 0)
        def _():
            dbias_ref[...] = jnp.zeros_like(dbias_ref)
            ck_ref[...] = jnp.zeros_like(ck_ref)
            cv_ref[...] = jnp.zeros_like(cv_ref)

        def chunk_body(chunk):
            units = _attn_units(r, nq, chunk, nu)

            def prev_rows(cs):
                return _rows(last_prev + (cs if r >= nu else cs % r), r)

            def keys(cur_ref, prev_ref, cs, ps):
                prev = prev_ref[prev_rows(cs), :] if ps is None else cur_ref[_rows(ps, r), :]
                return jnp.concatenate([prev, cur_ref[_rows(cs, r), :]], axis=0).astype(BF16)

            for u, (cs, ps) in enumerate(units):
                qv = q_ref[_rows(cs, r), :]
                dov = do_ref[_rows(cs, r), :]
                kw = keys(kc_ref, kp_ref, cs, ps)
                vw = keys(vc_ref, vp_ref, cs, ps)
                for hh in range(2):
                    s_ref[2 * u + hh] = _dot(jnp.where(masks[hh], qv, 0.0).astype(BF16), kw, NT)
                    dp_ref[2 * u + hh] = _dot(jnp.where(masks[hh], dov, 0.0).astype(BF16), vw, NT)
            for u, (cs, ps) in enumerate(units):
                lse_t = lse_ref[_rows(cs, r), :]
                dv_t = dv_ref[_rows(cs, r), :]
                for hh in range(2):
                    lo = hh * hd
                    s = s_ref[2 * u + hh] * scale + bias_ref[hh]
                    if ps is None:
                        s = s + first_pen
                    p = jnp.exp(s - lse_t[:, lo:lo + 1])
                    ds = p * (dp_ref[2 * u + hh] + dv_t[:, lo:lo + 1])
                    dbias_ref[hh] += ds
                    p_ref[2 * u + hh] = p.astype(BF16)
                    ds_ref[2 * u + hh] = ds.astype(BF16)
            for u, (cs, ps) in enumerate(units):
                qv = q_ref[_rows(cs, r), :]
                dov = do_ref[_rows(cs, r), :]
                kw = keys(kc_ref, kp_ref, cs, ps)
                dq, dkw, dvw = 0.0, 0.0, 0.0
                for hh in range(2):
                    dsb = ds_ref[2 * u + hh]
                    dq = dq + _dot(dsb, jnp.where(masks[hh], kw, 0), NN)
                    dkw = dkw + _dot(dsb, jnp.where(masks[hh], qv, 0.0).astype(BF16), TN)
                    dvw = dvw + _dot(p_ref[2 * u + hh], jnp.where(masks[hh], dov, 0.0).astype(BF16), TN)
                dq_ref[_rows(cs, r), :] = dq * scale
                ak_ref[_rows(cs, r), :] = dkw[B:] * scale
                av_ref[_rows(cs, r), :] = dvw[B:]
                if ps is None:
                    ck_ref[prev_rows(cs), :] += dkw[:B] * scale
                    cv_ref[prev_rows(cs), :] += dvw[:B]
                else:
                    ak_ref[_rows(ps, r), :] += dkw[:B] * scale
                    av_ref[_rows(ps, r), :] += dvw[:B]

        @pl.when(n < ntiles)
        def _():
            for chunk in range(nchunks):
                chunk_body(chunk)

        dk_ref[...] = ck_ref[...].astype(BF16)
        dvo_ref[...] = cv_ref[...].astype(BF16)
        ck_ref[...] = ak_ref[...]
        cv_ref[...] = av_ref[...]

    last = ntiles - 1

    def cur(t):
        return pl.BlockSpec((rows, LANES), lambda hf, n: (jnp.minimum(n, last), t * tiles_per_tensor + 2 * g + hf))

    def prev(t):
        return pl.BlockSpec((rows, LANES), lambda hf, n: (jnp.clip(n - 1, 0, last), t * tiles_per_tensor + 2 * g + hf))

    nat = pl.BlockSpec((rows, LANES), lambda hf, n: (jnp.minimum(n, last), hf))
    nat_prev = pl.BlockSpec((rows, LANES), lambda hf, n: (jnp.clip(n - 1, 0, last), hf))
    tab = pl.BlockSpec((None, 2, B, 2 * B), lambda hf, n: (g, hf, 0, 0))
    dtab = pl.BlockSpec((2, B, 2 * B), lambda hf, n: (hf, 0, 0))
    sd = jax.ShapeDtypeStruct
    vm = pltpu.VMEM
    dk_tile = dk_col // LANES + 2 * g
    dk_spec = pl.BlockSpec((rows, LANES), lambda hf, n: (jnp.clip(n - 1, 0, last), dk_tile + hf))
    return pl.pallas_call(
        body, name=name, grid=(2, ntiles + 1),
        in_specs=[cur(0), cur(1), prev(1), cur(2), prev(2), nat, nat, nat, tab, pl.BlockSpec(memory_space=pl.ANY)],
        out_specs=[nat, dk_spec, nat_prev, dtab],
        out_shape=[sd((T, 2 * LANES), F32), sd(dproj.shape, BF16), sd((T, 2 * LANES), BF16),
                   sd((HEADS_PER_GROUP, B, 2 * B), F32)],
        scratch_shapes=[vm((rows, LANES), F32), vm((rows, LANES), F32), vm((rows, LANES), F32), vm((rows, LANES), F32),
                        vm((2 * nu, B, 2 * B), F32), vm((2 * nu, B, 2 * B), F32),
                        vm((2 * nu, B, 2 * B), BF16), vm((2 * nu, B, 2 * B), BF16)],
        input_output_aliases={9: 1}, compiler_params=_params("parallel", "arbitrary"),
    )(qkv, qkv, qkv, qkv, qkv, do, dvec, lse, biasm, dproj)


def _attn_mix(os, lses, z, name):
    T, gw = os[0].shape
    C = z.shape[1]
    tm = _pick(T, 512, SUBLANES)

    def body(o0_ref, o1_ref, o2_ref, l0_ref, l1_ref, l2_ref, z_ref, out_ref):
        ls = [l0_ref[...], l1_ref[...], l2_ref[...]]
        mx = jnp.maximum(jnp.maximum(ls[0], ls[1]), ls[2])
        es = [jnp.exp(l - mx) for l in ls]
        inv = 1.0 / (es[0] + es[1] + es[2])
        for i, o_ref in enumerate((o0_ref, o1_ref, o2_ref)):
            sz, _ = _silu_and_grad(z_ref[:, i * gw:(i + 1) * gw].astype(F32))
            out_ref[:, i * gw:(i + 1) * gw] = (o_ref[...] * (es[i] * inv) * sz).astype(BF16)

    row = pl.BlockSpec((tm, C), lambda i: (i, 0))
    grp = pl.BlockSpec((tm, gw), lambda i: (i, 0))
    return pl.pallas_call(
        body, name=name, grid=(T // tm,), in_specs=[grp] * 6 + [row], out_specs=row,
        out_shape=jax.ShapeDtypeStruct((T, C), BF16), compiler_params=_params("parallel"),
    )(*os, *lses, z)


def _attn_mix_bwd(dout, os, lses, z, dproj, col, name):
    T, gw = os[0].shape
    C = z.shape[1]
    tm = _pick(T, 512, SUBLANES)
    head_of = np.arange(gw) // ATTN_HEAD_DIM
    ones = jnp.asarray(head_of[:, None] == head_of[None, :], BF16)

    def body(dout_ref, o0_ref, o1_ref, o2_ref, l0_ref, l1_ref, l2_ref, z_ref, ones_ref, _,
             dz_ref, do0_ref, do1_ref, do2_ref, dv0_ref, dv1_ref, dv2_ref):
        ls = [l0_ref[...], l1_ref[...], l2_ref[...]]
        mx = jnp.maximum(jnp.maximum(ls[0], ls[1]), ls[2])
        es = [jnp.exp(l - mx) for l in ls]
        inv = 1.0 / (es[0] + es[1] + es[2])
        alphas, ebar = [], 0.0
        for i, (o_ref, do_ref) in enumerate(((o0_ref, do0_ref), (o1_ref, do1_ref), (o2_ref, do2_ref))):
            sl = slice(i * gw, (i + 1) * gw)
            alpha = es[i] * inv
            ov = o_ref[...]
            dv = dout_ref[:, sl]
            sz, dsz = _silu_and_grad(z_ref[:, sl].astype(F32))
            dz_ref[:, sl] = (dv * ov * alpha * dsz).astype(BF16)
            da = dv * sz
            do_ref[...] = da * alpha
            t = da * ov
            t1 = t.astype(BF16)
            r1 = t - t1.astype(F32)
            t2 = r1.astype(BF16)
            t3 = (r1 - t2.astype(F32)).astype(BF16)
            e = _dot(t1, ones_ref[...], NN) + _dot(t2, ones_ref[...], NN) + _dot(t3, ones_ref[...], NN)
            ebar = ebar + alpha * e
            alphas.append(alpha)
        for alpha, dv_ref in zip(alphas, (dv0_ref, dv1_ref, dv2_ref)):
            dv_ref[...] = -alpha * ebar

    row = pl.BlockSpec((tm, C), lambda i: (i, 0))
    grp = pl.BlockSpec((tm, gw), lambda i: (i, 0))
    sd = jax.ShapeDtypeStruct
    res = pl.pallas_call(
        body, name=name, grid=(T // tm,),
        in_specs=[row] + [grp] * 6 + [row, pl.BlockSpec((gw, gw), lambda i: (0, 0)), pl.BlockSpec(memory_space=pl.ANY)],
        out_specs=[pl.BlockSpec((tm, C), lambda i: (i, col // C))] + [grp] * 6,
        out_shape=[sd(dproj.shape, BF16)] + [sd((T, gw), F32)] * 6, input_output_aliases={9: 0},
        compiler_params=_params("parallel"),
    )(dout, *os, *lses, z, ones, dproj)
    return res[0], res[1:4], res[4:7]


def _mem_attn(qz, kv, name):
    T = qz.shape[0]
    dm = qz.shape[1] // 2
    M = kv.shape[0]
    hd = dm // MEM_HEADS
    scale = hd ** -0.5
    tm = _pick(T, 512, SUBLANES)

    def body(q_ref, z_ref, k_ref, v_ref, o_ref, s_ref, p_ref):
        heads = [slice(h * hd, (h + 1) * hd) for h in range(MEM_HEADS)]
        for h, sl in enumerate(heads):
            s_ref[h] = _dot(q_ref[:, sl].astype(BF16), k_ref[:, sl], NT)
        for h, sl in enumerate(heads):
            s = s_ref[h] * scale
            p = jnp.exp(s - jnp.max(s, axis=-1, keepdims=True))
            p_ref[h] = (p * (1.0 / jnp.sum(p, axis=-1, keepdims=True))).astype(BF16)
        for h, sl in enumerate(heads):
            sz, _ = _silu_and_grad(z_ref[:, sl].astype(F32))
            o_ref[:, sl] = (_dot(p_ref[h], v_ref[:, sl], NN) * sz).astype(BF16)

    return pl.pallas_call(
        body, name=name, grid=(T // tm,),
        in_specs=[pl.BlockSpec((tm, dm), lambda i: (i, 0)), pl.BlockSpec((tm, dm), lambda i: (i, 1)),
                  pl.BlockSpec((M, dm), lambda i: (0, 0)), pl.BlockSpec((M, dm), lambda i: (0, 1))],
        out_specs=pl.BlockSpec((tm, dm), lambda i: (i, 0)),
        out_shape=jax.ShapeDtypeStruct((T, dm), BF16),
        scratch_shapes=[pltpu.VMEM((MEM_HEADS, tm, M), F32), pltpu.VMEM((MEM_HEADS, tm, M), BF16)],
        compiler_params=_params("parallel"),
    )(qz, qz, kv, kv)


def _mem_attn_bwd(do, qz, kv, dproj, col, name):
    T = qz.shape[0]
    dm = qz.shape[1] // 2
    M = kv.shape[0]
    hd = dm // MEM_HEADS
    scale = hd ** -0.5
    tm = _pick(T, 512, SUBLANES)

    def body(do_ref, q_ref, z_ref, k_ref, v_ref, _, dq_ref, dz_ref, dk_ref, dv_ref, s_ref, dp_ref, p_ref, ds_ref, dob_ref):
        @pl.when(pl.program_id(0) == 0)
        def _():
            dk_ref[...] = jnp.zeros_like(dk_ref)
            dv_ref[...] = jnp.zeros_like(dv_ref)

        heads = [slice(h * hd, (h + 1) * hd) for h in range(MEM_HEADS)]
        for h, sl in enumerate(heads):
            sz, _ = _silu_and_grad(z_ref[:, sl].astype(F32))
            dob = (do_ref[:, sl] * sz).astype(BF16)
            dob_ref[:, sl] = dob
            s_ref[h] = _dot(q_ref[:, sl].astype(BF16), k_ref[:, sl], NT)
            dp_ref[h] = _dot(dob, v_ref[:, sl], NT)
        for h, sl in enumerate(heads):
            s = s_ref[h] * scale
            p = jnp.exp(s - jnp.max(s, axis=-1, keepdims=True))
            pn = p * (1.0 / jnp.sum(p, axis=-1, keepdims=True))
            dp = dp_ref[h]
            p_ref[h] = pn.astype(BF16)
            ds_ref[h] = (pn * (dp - jnp.sum(dp * pn, axis=-1, keepdims=True))).astype(BF16)
        for h, sl in enumerate(heads):
            _, dsz = _silu_and_grad(z_ref[:, sl].astype(F32))
            dz_ref[:, sl] = (do_ref[:, sl] * _dot(p_ref[h], v_ref[:, sl], NN) * dsz).astype(BF16)
            dq_ref[:, sl] = (_dot(ds_ref[h], k_ref[:, sl], NN) * scale).astype(BF16)
            dk_ref[:, sl] += _dot(ds_ref[h], q_ref[:, sl].astype(BF16), TN) * scale
            dv_ref[:, sl] += _dot(p_ref[h], dob_ref[:, sl], TN)

    rowq = pl.BlockSpec((tm, dm), lambda i: (i, 0))
    rowz = pl.BlockSpec((tm, dm), lambda i: (i, 1))
    kb = pl.BlockSpec((M, dm), lambda i: (0, 0))
    vb = pl.BlockSpec((M, dm), lambda i: (0, 1))
    sd = jax.ShapeDtypeStruct
    dq, dz, dk, dv = pl.pallas_call(
        body, name=name, grid=(T // tm,), in_specs=[rowq, rowq, rowz, kb, vb, pl.BlockSpec(memory_space=pl.ANY)],
        out_specs=[pl.BlockSpec((tm, dm), lambda i: (i, col // dm)), rowq, kb, kb],
        out_shape=[sd(dproj.shape, BF16), sd((T, dm), BF16), sd((M, dm), F32), sd((M, dm), F32)],
        scratch_shapes=[pltpu.VMEM((MEM_HEADS, tm, M), F32), pltpu.VMEM((MEM_HEADS, tm, M), F32),
                        pltpu.VMEM((MEM_HEADS, tm, M), BF16), pltpu.VMEM((MEM_HEADS, tm, M), BF16), pltpu.VMEM((tm, dm), BF16)],
        input_output_aliases={5: 0}, compiler_params=_params("arbitrary"),
    )(do, qz, qz, kv, kv, dproj)
    return dq, dz, dk, dv


def _merge(os, ws, L, logits, b_gate, name):
    T = os[0].shape[0]
    D = ws[0].shape[2]
    tm = _pick(T, 512, SUBLANES)

    def body(o0_ref, o1_ref, o2_ref, w0_ref, w1_ref, w2_ref, l_ref, b_ref, m_ref, p_ref):
        acc = 0.0
        for i, (o_ref, w_ref) in enumerate(((o0_ref, w0_ref), (o1_ref, w1_ref), (o2_ref, w2_ref))):
            sl = slice(i * D, (i + 1) * D)
            bp = _dot(o_ref[...], w_ref[...], NN)
            p_ref[i] = bp.astype(BF16)
            acc = acc + _sigmoid(l_ref[:, sl].astype(F32) + b_ref[:, sl]) * bp
        m_ref[...] = acc.astype(BF16)

    return pl.pallas_call(
        body, name=name, grid=(T // tm,),
        in_specs=[pl.BlockSpec((tm, o.shape[1]), lambda i: (i, 0)) for o in os]
        + [pl.BlockSpec((None,) + w.shape[1:], lambda i: (L, 0, 0)) for w in ws]
        + [pl.BlockSpec((tm, 3 * D), lambda i: (i, 0)), pl.BlockSpec((1, 3 * D), lambda i: (0, 0))],
        out_specs=[pl.BlockSpec((tm, D), lambda i: (i, 0)), pl.BlockSpec((3, tm, D), lambda i: (0, i, 0))],
        out_shape=[jax.ShapeDtypeStruct((T, D), BF16), jax.ShapeDtypeStruct((3, T, D), BF16)],
        compiler_params=_params("parallel"),
    )(*os, *ws, logits, b_gate.reshape(1, 3 * D))


def _merge_bwd(dmerged, bps, logits, b_gate, dproj_cols, dl_off, name):
    _, T, D = bps.shape
    tm = _pick(T, 2048, SUBLANES)
    cw = _pick(math.gcd(dl_off, D), 512, LANES)
    per = D // cw

    def body(dm_ref, p_ref, l_ref, b_ref, dl_ref, d_ref, db_ref):
        @pl.when(pl.program_id(1) == 0)
        def _():
            db_ref[...] = jnp.zeros_like(db_ref)

        dmv = dm_ref[...]
        gt = _sigmoid(l_ref[...].astype(F32) + b_ref[...])
        d_ref[...] = (dmv * gt).astype(BF16)
        dl = dmv * p_ref[...].astype(F32) * gt * (1.0 - gt)
        dl_ref[...] = dl.astype(BF16)
        db_ref[...] += jnp.sum(dl, axis=0, keepdims=True)

    stacked = pl.BlockSpec((None, tm, cw), lambda j, i: (j // per, i, j % per))
    sd = jax.ShapeDtypeStruct
    return pl.pallas_call(
        body, name=name, grid=(3 * per, T // tm),
        in_specs=[pl.BlockSpec((tm, cw), lambda j, i: (i, j % per)), stacked, pl.BlockSpec((tm, cw), lambda j, i: (i, j)),
                  pl.BlockSpec((1, cw), lambda j, i: (0, j))],
        out_specs=[pl.BlockSpec((tm, cw), lambda j, i: (i, dl_off // cw + j)), stacked, pl.BlockSpec((1, cw), lambda j, i: (0, j))],
        out_shape=[sd((T, dproj_cols), BF16), sd((3, T, D), BF16), sd((1, 3 * D), F32)],
        compiler_params=_params("parallel", "arbitrary"),
    )(dmerged, bps, logits, b_gate.reshape(1, 3 * D))


def _block_diag(w):
    nblk, ng, a, b = w.shape
    eye = jnp.eye(ng, dtype=w.dtype)
    return (w[:, :, :, None, :] * eye[None, :, None, :, None]).reshape(nblk, ng * a, ng * b)


def _block_diag_part(m, a, b):
    nblk = m.shape[0]
    ng = m.shape[1] // a
    m5 = m.reshape(nblk, ng, a, ng, b)
    eye = jnp.eye(ng, dtype=m.dtype)
    return jnp.sum(m5 * eye[None, :, None, :, None], axis=3)


def _ssm_matrices(p, L, tag):
    G, P = p["ssm_lambda_re"].shape[1:]
    Hg = SSM_GROUP
    gpb = SSM_BLOCK_CH // Hg
    nblk = G // gpb
    br = p["ssm_b_re"][L].transpose(2, 0, 1)
    bi = p["ssm_b_im"][L].transpose(2, 0, 1)
    disc_in = (p["ssm_lambda_re"][L], p["ssm_lambda_im"][L], p["ssm_log_dt"][L].reshape(G, 1), br, bi)
    ar, ai, bbr, bbi = _ssm_disc(*disc_in, name=f"ssm_disc_{tag}")
    amat = (ar.reshape(nblk // 2, SUBLANES, LANES), ai.reshape(nblk // 2, SUBLANES, LANES))
    bbr_g = bbr.transpose(1, 0, 2).reshape(nblk, gpb, Hg, P)
    bbi_g = bbi.transpose(1, 0, 2).reshape(nblk, gpb, Hg, P)
    bmat = jnp.concatenate([_block_diag(bbr_g), _block_diag(bbi_g)], axis=2).astype(BF16)
    cre = p["ssm_c_re"][L].reshape(nblk, gpb, Hg, P).transpose(0, 1, 3, 2)
    cim = p["ssm_c_im"][L].reshape(nblk, gpb, Hg, P).transpose(0, 1, 3, 2)
    cmat = jnp.concatenate([_block_diag(cre), -_block_diag(cim)], axis=1).astype(BF16)
    return disc_in, amat, bmat, cmat


def _layer_fwd(x, mem, p, wb, L, biasm):
    T, D = x.shape
    C = p["ssm_d"].shape[1]
    dm = wb["w_br_mem"].shape[1]
    tag = f"l{L}"
    s = {"x": x}
    h = _rmsnorm(x, p["norm_g"][L], f"norm_{tag}")
    offs = [int(o) for o in np.cumsum([0, C, C, 3 * 768, 768, 2 * dm, 3 * D])]
    names = ("uz", "z_ssm", "qkv", "z_attn", "qz_mem", "logits")
    dts = (F32, BF16, F32, BF16, BF16, BF16)
    for i, (nm, dt) in enumerate(zip(names, dts)):
        tiles = dict(tm=2048) if offs[i + 1] - offs[i] >= 1024 and dt == BF16 else {}
        s[nm] = _matmul(h, wb["w_in"], mode="nn", name=f"in_{nm}_{tag}", out_dtype=dt, b_lead=L, b_off=offs[i],
                        n_cols=offs[i + 1] - offs[i], **tiles)
    s["h"] = h

    disc_in, amat, bmat, cmat = _ssm_matrices(p, L, tag)
    dvec = p["ssm_d"][L].reshape(1, C)
    y, xr, xi = _ssm_fwd(s["uz"], bmat, cmat, *amat, dvec, f"ssm_scan_{tag}")
    o_ssm, a_glu = _ssm_post(y, s["z_ssm"], wb["w_glu"][L], p["b_glu"][L], f"ssm_post_{tag}")
    s.update(disc_in=disc_in, amat=amat, bmat=bmat, cmat=cmat, xr=xr, xi=xi, y=y, a_glu=a_glu, o_ssm=o_ssm)

    groups = [_attn_group_fwd(s["qkv"], biasm, g, f"attn_g{g}_{tag}", nu=_attn_units_for(g, L))
              for g in range(len(ATTN_CONFIGS))]
    os, lses = [o for o, _ in groups], [l for _, l in groups]
    o_attn = _attn_mix(os, lses, s["z_attn"], f"attn_mix_{tag}")
    s.update(os=os, lses=lses, o_attn=o_attn)

    mn = _rmsnorm(mem, p["mem_norm_g"][L], f"mem_norm_{tag}")
    kv = _matmul(mn, wb["w_mem_kv"], mode="nn", name=f"mem_kv_{tag}", out_dtype=BF16, b_lead=L)
    o_mem = _mem_attn(s["qz_mem"], kv, f"mem_attn_{tag}")
    s.update(mn=mn, kv=kv, o_mem=o_mem)

    merged, bps = _merge([o_ssm, o_attn, o_mem], [wb["w_br_ssm"], wb["w_br_attn"], wb["w_br_mem"]], L, s["logits"],
                         p["b_gate"][L], f"merge_{tag}")
    s.update(bps=bps, merged=merged)
    x_new = _matmul(merged, wb["w_out"], mode="nn", name=f"out_{tag}", add=x, b_lead=L)
    return x_new, s


def _layer_bwd(dx, mem, p, wb, L, s, biasm, gprev):
    T, D = dx.shape
    C = p["ssm_d"].shape[1]
    depth = p["norm_g"].shape[0]
    tag = f"l{L}"
    g = {}

    def wgrad(n, a, b, **tiles):
        g[n] = _matmul(a, b, mode="tn", name=f"d{n}_{tag}", out_dtype=BF16, stack=(L, depth, gprev.get(n)), **tiles)

    dmerged = _matmul(dx, wb["w_out"], mode="nt", name=f"d_merged_{tag}", b_lead=L)
    wgrad("w_out", s["merged"], dx)
    dm = s["qz_mem"].shape[1] // 2
    col = dict(zip(("u", "z_ssm", "q", "k", "v", "z_attn", "q_mem", "z_mem", "logits", "end"),
                   (int(o) for o in np.cumsum([0, C, C, 768, 768, 768, 768, dm, dm, 3 * D]))))
    dproj, dbps, g["b_gate"] = _merge_bwd(dmerged, s["bps"], s["logits"], p["b_gate"][L], col["end"], col["logits"],
                                          f"merge_bwd_{tag}")
    dos = []
    for i, (o, n) in enumerate(((s["o_ssm"], "w_br_ssm"), (s["o_attn"], "w_br_attn"), (s["o_mem"], "w_br_mem"))):
        dos.append(_matmul(dbps, wb[n], mode="nt", name=f"d_o_{n}_{tag}", a_lead=i, b_lead=L))
        g[n] = _matmul(o, dbps, mode="tn", name=f"d{n}_{tag}", out_dtype=BF16, b_lead=i, stack=(L, depth, gprev.get(n)))

    dy, dproj, ds_glu, g["b_glu"] = _ssm_post_bwd(dos[0], s["y"], s["z_ssm"], wb["w_glu"][L], p["b_glu"][L], dproj,
                                                  col["z_ssm"], f"ssm_post_bwd_{tag}")
    wgrad("w_glu", s["a_glu"], ds_glu)
    dvec = p["ssm_d"][L].reshape(1, C)
    dproj, dbm, dct, dar, dai, g["ssm_d"] = _ssm_bwd(dy, s["uz"], s["xr"], s["xi"], s["bmat"], s["cmat"], *s["amat"], dvec,
                                                     dproj, f"ssm_scan_bwd_{tag}")
    G, P = p["ssm_lambda_re"].shape[1:]
    Hg = SSM_GROUP
    half = dbm.shape[2] // 2
    dbbr = _block_diag_part(dbm[:, :, :half], Hg, P).reshape(G, Hg, P).transpose(1, 0, 2)
    dbbi = _block_diag_part(dbm[:, :, half:], Hg, P).reshape(G, Hg, P).transpose(1, 0, 2)
    g["ssm_c_re"] = _block_diag_part(dct[:, :, :half], Hg, P).reshape(G, Hg, P)
    g["ssm_c_im"] = -_block_diag_part(dct[:, :, half:], Hg, P).reshape(G, Hg, P)
    glre, glim, gdt, gbr, gbi = _ssm_disc_bwd(*s["disc_in"], dar.reshape(G, P), dai.reshape(G, P), dbbr, dbbi,
                                              name=f"ssm_disc_bwd_{tag}")
    g["ssm_lambda_re"], g["ssm_lambda_im"], g["ssm_log_dt"] = glre, glim, gdt.reshape(G)
    g["ssm_b_re"] = gbr.transpose(1, 2, 0)
    g["ssm_b_im"] = gbi.transpose(1, 2, 0)

    dproj, do_g, dvec_g = _attn_mix_bwd(dos[1], s["os"], s["lses"], s["z_attn"], dproj, col["z_attn"], f"attn_mix_bwd_{tag}")
    rest, dbias = [], []
    for gi in range(len(ATTN_CONFIGS)):
        dq_g, dproj, dv_g, db_g = _attn_group_bwd(s["qkv"], do_g[gi], dvec_g[gi], s["lses"][gi], biasm, gi, dproj, col["k"],
                                                  f"attn_bwd_g{gi}_{tag}", nu=_attn_units_for(gi, L))
        gw = dq_g.shape[1]
        rest += [(dq_g, col["q"] + gi * gw), (dv_g, col["v"] + gi * gw)]
        dbias.append(db_g)
    dbias = jnp.stack(dbias)

    dproj, dz_mem, dk_mem, dv_mem = _mem_attn_bwd(dos[2], s["qz_mem"], s["kv"], dproj, col["q_mem"], f"mem_attn_bwd_{tag}")
    rest.append((dz_mem, col["z_mem"]))
    for piece, at in rest:
        dproj = lax.dynamic_update_slice(dproj, piece.astype(BF16), (0, at))
    dkv = jnp.concatenate([dk_mem, dv_mem], axis=1)
    wgrad("w_mem_kv", s["mn"], dkv)
    dmn = _matmul(dkv, wb["w_mem_kv"], mode="nt", name=f"d_mn_{tag}", b_lead=L)
    _, g["mem_norm_g"] = _rmsnorm_bwd(mem, p["mem_norm_g"][L], dmn, None, f"mem_norm_bwd_{tag}")

    dh = _matmul(dproj, wb["w_in"], mode="nt", name=f"d_h_{tag}", b_lead=L, tm=512, tn=1024)
    wgrad("w_in", s["h"], dproj, tn=2304, tk=1024)
    dx_in, g["norm_g"] = _rmsnorm_bwd(s["x"], p["norm_g"][L], dh, dx, f"norm_bwd_{tag}")
    return dx_in, g, dbias


def _bucket_onehot(gi):
    buckets, bands = _band_tables()
    hit = (buckets[gi].reshape(1, -1) == jnp.arange(NUM_BUCKETS)[:, None]) & bands[gi].reshape(1, -1)
    return hit.astype(BF16)


def _bias_tables(rel_bias, name):
    _, bands = _band_tables()
    out = []
    for gi in range(len(ATTN_CONFIGS)):
        tab = rel_bias[:, gi * HEADS_PER_GROUP:(gi + 1) * HEADS_PER_GROUP].T
        flat = _matmul(tab, _bucket_onehot(gi), mode="nn", name=f"{name}_{gi}", split_a=3, tn=4096)
        out.append(jnp.where(bands[gi][None], flat.reshape(HEADS_PER_GROUP, ATTN_BLOCK, 2 * ATTN_BLOCK), NEG_INF))
    return jnp.stack(out)


def _rel_bias_grad(dbias_sum, name):
    cols = []
    for gi in range(len(ATTN_CONFIGS)):
        flat = dbias_sum[gi].reshape(HEADS_PER_GROUP, -1)
        cols.append(_matmul(flat, _bucket_onehot(gi), mode="nt", name=f"{name}_{gi}", split_a=2, tk=4096).T)
    return jnp.concatenate(cols, axis=1)


def _local_step(x, mem, target, p, wb):
    depth = p["norm_g"].shape[0]
    biasm = _bias_tables(p["rel_bias"], "bias_table")
    saved = []
    for L in range(depth):
        x, s = _layer_fwd(x, mem, p, wb, L, biasm)
        saved.append(s)
    loss_vec, dx, dgf = _loss_head(x, p["final_norm_g"], target, "loss_head")
    grads = {"final_norm_g": dgf.reshape(-1)}
    per_layer = [None] * depth
    dbias_sum = 0.0
    stacked = {}
    for L in reversed(range(depth)):
        dx, per_layer[L], dbias = _layer_bwd(dx, mem, p, wb, L, saved[L], biasm, stacked)
        stacked = {n: per_layer[L][n] for n, _ in BIG}
        dbias_sum = dbias_sum + dbias
    grads.update(stacked)
    for n in per_layer[0]:
        if n not in stacked:
            grads[n] = jnp.stack([per_layer[L][n].reshape(p[n].shape[1:]) for L in range(depth)])
    grads["rel_bias"] = _rel_bias_grad(dbias_sum, "d_rel_bias")
    return jnp.sum(loss_vec), dx, grads


def _chip_coords(j):
    return j // 2, j % 2


def _place_shard(shard, ax, chip, name):
    _, a, b = shard.shape
    ra = _pick(a, 256, 16)
    full = (2, a * N_CHIPS, b) if ax == 1 else (2, a, b * N_CHIPS)
    per = a // ra

    def body(j_ref, s_ref, o_ref):
        o_ref[...] = s_ref[...].astype(BF16)

    out_idx = (lambda l, i, j: (l, j[0] * per + i, 0)) if ax == 1 else (lambda l, i, j: (l, i, j[0]))
    return pl.pallas_call(
        body, name=name,
        grid_spec=pltpu.PrefetchScalarGridSpec(
            num_scalar_prefetch=1, grid=(2, per),
            in_specs=[pl.BlockSpec((None, ra, b), lambda l, i, j: (l, i, 0))],
            out_specs=pl.BlockSpec((None, ra, b), out_idx)),
        out_shape=jax.ShapeDtypeStruct(full, BF16), compiler_params=_params("parallel", "parallel"),
    )(chip, shard)


def _gather_shards(fulls, axes, name):
    n = len(fulls)
    widths = [a.shape[ax] // N_CHIPS for a, ax in zip(fulls, axes)]
    aligns = [LANES if ax == 2 else 16 for ax in axes]

    def body(*refs):
        outs = refs[n:2 * n]
        send_sems, recv_sems, fsend_sems, frecv_sems = refs[2 * n:]
        x, y, c = lax.axis_index("x"), lax.axis_index("y"), lax.axis_index("c")
        mine = 2 * x + y
        sibling = (x, y, 1 - c)

        def window(t, layer, j):
            start = pl.ds(pl.multiple_of(j * widths[t], aligns[t]), widths[t])
            return outs[t].at[(layer, start, slice(None)) if axes[t] == 1 else (layer, slice(None), start)]

        def over_ici(t, j, block):
            return pltpu.make_async_remote_copy(
                src_ref=window(t, c, mine), dst_ref=window(t, c, block), send_sem=send_sems.at[t, j],
                recv_sem=recv_sems.at[t, block], device_id=(*_chip_coords(j), c), device_id_type=MESH)

        def over_d2d(t, j, layer):
            return pltpu.make_async_remote_copy(
                src_ref=window(t, layer, j), dst_ref=window(t, layer, j), send_sem=fsend_sems.at[t, j],
                recv_sem=frecv_sems.at[t, j], device_id=sibling, device_id_type=MESH)

        for t in range(n):
            for j in range(N_CHIPS):
                @pl.when(j != mine)
                def _():
                    over_ici(t, j, mine).start()
        for t in range(n):
            for j in range(N_CHIPS):
                @pl.when(j != mine)
                def _():
                    over_ici(t, j, j).wait_recv()
                    over_d2d(t, j, c).start()
        for t in range(n):
            for j in range(N_CHIPS):
                @pl.when(j != mine)
                def _():
                    over_ici(t, j, mine).wait_send()
                    over_d2d(t, j, c).wait_send()
                    over_d2d(t, j, 1 - c).wait_recv()

    sem = pltpu.SemaphoreType.DMA
    return pl.pallas_call(
        body, name=name, in_specs=[HBM] * n, out_specs=[HBM] * n,
        out_shape=[jax.ShapeDtypeStruct(a.shape, a.dtype) for a in fulls],
        input_output_aliases={t: t for t in range(n)},
        scratch_shapes=[sem((n, N_CHIPS)), sem((n, N_CHIPS)), sem((n, N_CHIPS)), sem((n, N_CHIPS))],
    )(*fulls)


def _scatter_slices(arrays, axes, name):
    n = len(arrays)

    def piece(a, ax):
        if ax is None:
            return a.shape, None
        w = a.shape[ax] // N_CHIPS
        return a.shape[:ax] + (w,) + a.shape[ax + 1:], w

    shapes = [piece(a, ax) for a, ax in zip(arrays, axes)]

    def body(*refs):
        ins, outs = refs[:n], refs[n:2 * n]
        send_sems, recv_sems, loc_sems = refs[2 * n:]
        x, y, c = lax.axis_index("x"), lax.axis_index("y"), lax.axis_index("c")
        mine = 2 * x + y

        def src(t, j):
            ax, w = axes[t], shapes[t][1]
            if ax is None:
                return ins[t]
            idx = tuple(pl.ds(j * w, w) if d == ax else slice(None) for d in range(len(arrays[t].shape)))
            return ins[t].at[idx]

        for t in range(n):
            for j in range(N_CHIPS):
                @pl.when(j == mine)
                def _():
                    pltpu.make_async_copy(src(t, j), outs[t].at[j], loc_sems.at[t]).start()

                @pl.when(j != mine)
                def _():
                    pltpu.make_async_remote_copy(
                        src_ref=src(t, j), dst_ref=outs[t].at[mine], send_sem=send_sems.at[t, j], recv_sem=recv_sems.at[t, mine],
                        device_id=(*_chip_coords(j), c), device_id_type=MESH).start()
        for t in range(n):
            for j in range(N_CHIPS):
                @pl.when(j == mine)
                def _():
                    pltpu.make_async_copy(src(t, j), outs[t].at[j], loc_sems.at[t]).wait()

                @pl.when(j != mine)
                def _():
                    cp = pltpu.make_async_remote_copy(
                        src_ref=src(t, j), dst_ref=outs[t].at[j], send_sem=send_sems.at[t, j], recv_sem=recv_sems.at[t, j],
                        device_id=(*_chip_coords(j), c), device_id_type=MESH)
                    cp.wait_send()
                    cp.wait_recv()

    return pl.pallas_call(
        body, name=name, in_specs=[HBM] * n, out_specs=[HBM] * n,
        out_shape=[jax.ShapeDtypeStruct((N_CHIPS,) + sh, a.dtype) for a, (sh, _) in zip(arrays, shapes)],
        scratch_shapes=[pltpu.SemaphoreType.DMA((n, N_CHIPS)), pltpu.SemaphoreType.DMA((n, N_CHIPS)), pltpu.SemaphoreType.DMA((n,))],
    )(*arrays)


def _swap_layers(stacked, name):
    n = len(stacked)

    def body(*refs):
        ins, outs = refs[:n], refs[n:2 * n]
        send_sems, recv_sems = refs[2 * n:]
        c = lax.axis_index("c")
        peer = (lax.axis_index("x"), lax.axis_index("y"), 1 - c)
        cps = [pltpu.make_async_remote_copy(src_ref=ins[t].at[1 - c], dst_ref=outs[t], send_sem=send_sems.at[t],
                                            recv_sem=recv_sems.at[t], device_id=peer, device_id_type=MESH) for t in range(n)]
        for cp in cps:
            cp.start()
        for cp in cps:
            cp.wait_send()
            cp.wait_recv()

    return pl.pallas_call(
        body, name=name, in_specs=[HBM] * n, out_specs=[HBM] * n,
        out_shape=[jax.ShapeDtypeStruct(a.shape[1:], a.dtype) for a in stacked],
        scratch_shapes=[pltpu.SemaphoreType.DMA((n,)), pltpu.SemaphoreType.DMA((n,))],
    )(*stacked)


def _merge_layers(stacked, name):
    n = len(stacked)

    def body(*refs):
        outs = refs[n:2 * n]
        send_sems, recv_sems = refs[2 * n:]
        c = lax.axis_index("c")
        peer = (lax.axis_index("x"), lax.axis_index("y"), 1 - c)
        for t in range(n):
            pltpu.make_async_remote_copy(src_ref=outs[t].at[c], dst_ref=outs[t].at[c], send_sem=send_sems.at[t],
                                         recv_sem=recv_sems.at[t], device_id=peer, device_id_type=MESH).start()
        for t in range(n):
            cp = pltpu.make_async_remote_copy(src_ref=outs[t].at[c], dst_ref=outs[t].at[1 - c], send_sem=send_sems.at[t],
                                              recv_sem=recv_sems.at[t], device_id=peer, device_id_type=MESH)
            cp.wait_send()
            cp.wait_recv()

    sem = pltpu.SemaphoreType.DMA
    return pl.pallas_call(
        body, name=name, in_specs=[HBM] * n, out_specs=[HBM] * n,
        out_shape=[jax.ShapeDtypeStruct(a.shape, a.dtype) for a in stacked],
        input_output_aliases={t: t for t in range(n)}, scratch_shapes=[sem((n,)), sem((n,))],
    )(*stacked)


def _pair_sum(stacked, landed, core, name):
    _, K, N = stacked.shape
    tr = _pick(K, max(16, (1 << 19) // N // 16 * 16), 16)

    def body(c_ref, s_ref, l_ref, o_ref):
        o_ref[...] = (s_ref[...].astype(F32) + l_ref[...].astype(F32)).astype(o_ref.dtype)

    return pl.pallas_call(
        body, name=name,
        grid_spec=pltpu.PrefetchScalarGridSpec(
            num_scalar_prefetch=1, grid=(K // tr,),
            in_specs=[pl.BlockSpec((None, tr, N), lambda i, c: (c[0], i, 0)), pl.BlockSpec((tr, N), lambda i, c: (i, 0))],
            out_specs=pl.BlockSpec((tr, N), lambda i, c: (i, 0))),
        out_shape=jax.ShapeDtypeStruct((K, N), stacked.dtype), compiler_params=_params("parallel"),
    )(core, stacked, landed)


def _sum_chips(landed, core, name):
    _, R, C = landed.shape
    tr = _pick(R, max(SUBLANES, (1 << 19) // C // 16 * 16), 16)

    def body(c_ref, l_ref, o_ref):
        acc = l_ref[0].astype(F32) + l_ref[1].astype(F32)
        acc = acc + l_ref[2].astype(F32)
        o_ref[...] = acc + l_ref[3].astype(F32)

    return pl.pallas_call(
        body, name=name,
        grid_spec=pltpu.PrefetchScalarGridSpec(
            num_scalar_prefetch=1, grid=(R // tr,),
            in_specs=[pl.BlockSpec((N_CHIPS, tr, C), lambda i, c: (0, i, 0))],
            out_specs=pl.BlockSpec((None, tr, C), lambda i, c: (c[0], i, 0))),
        out_shape=jax.ShapeDtypeStruct((2, R, C), F32), compiler_params=_params("parallel"),
    )(core, landed)


def _adamw_math(w_ref, g_ref, m_ref, v_ref, d_ref, nm_ref, nv_ref):
    c1 = 1.0 / (1.0 - ADAM_B1 ** ADAM_STEP)
    c2 = 1.0 / (1.0 - ADAM_B2 ** ADAM_STEP)
    g = g_ref[...]
    nm = ADAM_B1 * m_ref[...] + (1.0 - ADAM_B1) * g
    nv = ADAM_B2 * v_ref[...] + (1.0 - ADAM_B2) * (g * g)
    nm_ref[...] = nm
    nv_ref[...] = nv
    d_ref[...] = -ADAM_LR * ((nm * c1) / (jnp.sqrt(nv * c2) + ADAM_EPS) + ADAM_WD * w_ref[...])


def _adamw_whole(w, g, m, v, name):
    shape = w.shape
    view = (-1,) + shape[-2:] if w.ndim >= 2 else (1, 1, -1)

    def body(*refs):
        _adamw_math(*refs)

    res = pl.pallas_call(body, name=name, out_shape=[jax.ShapeDtypeStruct(w.reshape(view).shape, F32)] * 3,
                         compiler_params=pltpu.CompilerParams(vmem_limit_bytes=VMEM_LIMIT_BYTES))(
        *(a.reshape(view) for a in (w, g, m, v)))
    return [r.reshape(shape) for r in res]


def _adamw(w, g, m, v, name):
    R, C = w.shape
    tr = _pick(R, max(SUBLANES, (1 << 18) // C // 8 * 8), SUBLANES)

    def body(*refs):
        _adamw_math(*refs)

    blk = pl.BlockSpec((tr, C), lambda i: (i, 0))
    return pl.pallas_call(
        body, name=name, grid=(R // tr,), in_specs=[blk] * 4, out_specs=[blk] * 3,
        out_shape=[jax.ShapeDtypeStruct((R, C), F32)] * 3, compiler_params=_params("parallel"),
    )(w, g, m, v)


def _pack_small(d, prefix=""):
    flat = jnp.concatenate([d[prefix + n].astype(F32).reshape(-1) for n in SMALL])
    pad = (-flat.shape[0]) % (2 * 16 * LANES)
    return jnp.pad(flat, (0, pad)).reshape(-1, LANES)


def _unpack_small(packed, shapes):
    flat = packed.reshape(-1)
    out, off = {}, 0
    for n in SMALL:
        size = int(np.prod(shapes[n]))
        out[n] = flat[off:off + size].reshape(shapes[n])
        off += size
    return out


def kernel(*args):
    p = dict(zip(INPUTS, args))
    x, mem, target = p["x"][0], p["mem"][0], p["loss_target"][0]

    names = [n for n, _ in BIG] + ["small"]
    core = lax.axis_index("c").astype(jnp.int32).reshape(1)
    chip = (2 * lax.axis_index("x") + lax.axis_index("y")).astype(jnp.int32).reshape(1)
    placed = [_place_shard(p[n], ax, chip, f"place_{n}") for n, ax in BIG]
    wb = dict(zip(names, _gather_shards(placed, [ax for _, ax in BIG], "gather_weights")))

    loss_part, dx, grads = _local_step(x, mem, target, p, wb)
    loss = lax.psum(loss_part, ("x", "y", "c"))

    stacked = [grads[n] for n, _ in BIG] + [_pack_small(grads).reshape(2, -1, LANES)]
    theirs = _swap_layers(stacked, "swap_layers")
    pair = [_pair_sum(s, o, core, f"pair_sum_{n}") for n, s, o in zip(names, stacked, theirs)]
    landed = _scatter_slices(pair, [ax - 1 for _, ax in BIG] + [None], "scatter_grads")
    reduced = [_sum_chips(ld.reshape(N_CHIPS, -1, ld.shape[-1]), core, f"sum_chips_{n}") for n, ld in zip(names, landed)]
    total = _merge_layers(reduced, "merge_layers")

    out = {}
    for (n, _), g in zip(BIG, total):
        sh = p[n].shape
        two_d = lambda a: a.reshape(-1, sh[-1])
        res = (g,) + tuple(_adamw(two_d(p[n]), two_d(g), two_d(p["m_" + n]), two_d(p["v_" + n]), f"adamw_{n}"))
        for key, r in zip(("grad_", "delta_", "new_m_", "new_v_"), res):
            out[key + n] = r.reshape(sh)
    for n, g in _unpack_small(total[-1], {n: p[n].shape for n in SMALL}).items():
        res = (g,) + tuple(_adamw_whole(p[n], g, p["m_" + n], p["v_" + n], f"adamw_{n}"))
        for key, r in zip(("grad_", "delta_", "new_m_", "new_v_"), res):
            out[key + n] = r

    result = [loss, dx.reshape(p["x"].shape)]
    for key in ("grad_", "delta_", "new_m_", "new_v_"):
        result += [out[key + n] for n in WEIGHTS]
    return tuple(result)
```

```python
import math

import jax
import jax.numpy as jnp
import numpy as np
from jax import lax
from jax.experimental import pallas as pl
from jax.experimental.pallas import tpu as pltpu

F32 = jnp.float32
BF16 = jnp.bfloat16
MESH = pl.DeviceIdType.MESH
HBM = pl.BlockSpec(memory_space=pltpu.HBM)

EPS = 1e-6
SSM_GROUP = 16
SSM_STATE = 64
ATTN_HEAD_DIM = 64
HEADS_PER_GROUP = 4
ATTN_CONFIGS = ((128, 1), (512, 4), (2048, 16))
ATTN_BLOCK = 128
NUM_BUCKETS = 32
REL_MAX_DISTANCE = 2048
NEG_INF = -1e30
MEM_HEADS = 4
ADAM_LR = 0.001
ADAM_B1 = 0.9
ADAM_B2 = 0.999
ADAM_EPS = 1e-08
ADAM_WD = 0.01
ADAM_STEP = 10

LANES = 128
SUBLANES = 8
VMEM_LIMIT_BYTES = 48 * 1024 * 1024
SSM_BLOCK_CH = 128

N_CHIPS = 4
BIG = (("w_in", 2), ("w_glu", 1), ("w_mem_kv", 1), ("w_br_ssm", 2), ("w_br_attn", 2), ("w_br_mem", 2), ("w_out", 1))
SMALL = ("norm_g", "mem_norm_g", "b_gate", "ssm_lambda_re", "ssm_lambda_im", "ssm_log_dt", "ssm_b_re", "ssm_b_im",
         "ssm_c_re", "ssm_c_im", "ssm_d", "b_glu", "rel_bias", "final_norm_g")
WEIGHTS = ("norm_g", "mem_norm_g", "w_in", "b_gate", "ssm_lambda_re", "ssm_lambda_im", "ssm_log_dt", "ssm_b_re",
           "ssm_b_im", "ssm_c_re", "ssm_c_im", "ssm_d", "w_glu", "b_glu", "w_mem_kv", "w_br_ssm", "w_br_attn",
           "w_br_mem", "w_out", "rel_bias", "final_norm_g")
INPUTS = ("x", "mem") + WEIGHTS + ("loss_target",) + tuple("m_" + n for n in WEIGHTS) + tuple("v_" + n for n in WEIGHTS)


def _params(*sem):
    return pltpu.CompilerParams(dimension_semantics=sem, vmem_limit_bytes=VMEM_LIMIT_BYTES)


def _pick(dim, pref, align):
    if dim <= pref:
        return dim
    t = pref - pref % align
    while t >= align:
        if dim % t == 0:
            return t
        t -= align
    return dim


def _sigmoid(v):
    return 0.5 * jnp.tanh(0.5 * v) + 0.5


def _silu_and_grad(z):
    s = _sigmoid(z)
    return z * s, s * (1.0 + z * (1.0 - s))


_GELU_C = math.sqrt(2.0 / math.pi)


def _gelu_and_grad(y):
    inner = _GELU_C * (y + 0.044715 * y * y * y)
    t = jnp.tanh(inner)
    g = 0.5 * y * (1.0 + t)
    dg = 0.5 * (1.0 + t) + 0.5 * y * (1.0 - t * t) * _GELU_C * (1.0 + 3.0 * 0.044715 * y * y)
    return g, dg


def _dot(a, b, dims):
    return lax.dot_general(a, b, (dims, ((), ())), preferred_element_type=F32)


NN = ((1,), (0,))
NT = ((1,), (1,))
TN = ((0,), (0,))


def _matmul(a, b, *, mode, name, out_dtype=F32, add=None, split_a=1, tm=1024, tn=768, tk=2304,
            a_lead=None, b_lead=None, b_off=0, n_cols=None, stack=None):
    ashape = a.shape if a_lead is None else a.shape[1:]
    K, M = ashape if mode == "tn" else ashape[::-1]
    bshape = b.shape if b_lead is None else b.shape[1:]
    N = n_cols or (bshape[0] if mode == "nt" else bshape[1])
    if mode != "tn" and M >= 4 * tm:
        tm = 2 * tm
    tm = _pick(M, tm, LANES if mode == "tn" else SUBLANES)
    tn = _pick(math.gcd(N, b_off) if b_off else N, tn, LANES)
    tk = _pick(K, tk, LANES)
    nk = K // tk
    joff = b_off // tn
    dims = {"nn": NN, "nt": NT, "tn": TN}[mode]
    has_add = add is not None
    has_prev = stack is not None and stack[2] is not None

    def body(*refs):
        a_ref, b_ref = refs[:2]
        add_ref = refs[2] if has_add else None
        o_ref = refs[-2] if nk > 1 else refs[-1]
        k = pl.program_id(2)
        bv = b_ref[...].astype(BF16)
        if split_a > 1:
            rest = a_ref[...].astype(F32)
            part = 0.0
            for _ in range(split_a):
                piece = rest.astype(BF16)
                part = part + _dot(piece, bv, dims)
                rest = rest - piece.astype(F32)
        else:
            part = _dot(a_ref[...].astype(BF16), bv, dims)

        def finish(r):
            if has_add:
                r = r + add_ref[...]
            o_ref[...] = r.astype(out_dtype)

        if nk == 1:
            finish(part)
            return
        acc_ref = refs[-1]

        @pl.when(k == 0)
        def _():
            acc_ref[...] = part

        @pl.when((k > 0) & (k < nk - 1))
        def _():
            acc_ref[...] += part

        @pl.when(k == nk - 1)
        def _():
            finish(acc_ref[...] + part)

    alead = () if a_lead is None else (a_lead,)
    alead_blk = () if a_lead is None else (None,)
    if mode == "tn":
        a_spec = pl.BlockSpec(alead_blk + (tk, tm), lambda i, j, k: alead + (k, i))
    else:
        a_spec = pl.BlockSpec(alead_blk + (tm, tk), lambda i, j, k: alead + (i, k))
    lead = () if b_lead is None else (b_lead,)
    lead_blk = () if b_lead is None else (None,)
    if mode == "nt":
        b_spec = pl.BlockSpec(lead_blk + (tn, tk), lambda i, j, k: lead + (j + joff, k))
    else:
        b_spec = pl.BlockSpec(lead_blk + (tk, tn), lambda i, j, k: lead + (k, j + joff))
    in_specs = [a_spec, b_spec]
    args = [a, b]
    if has_add:
        in_specs.append(pl.BlockSpec((tm, tn), lambda i, j, k: (i, j)))
        args.append(add)
    aliases = {}
    if stack is None:
        out_spec = pl.BlockSpec((tm, tn), lambda i, j, k: (i, j))
        out_shape = jax.ShapeDtypeStruct((M, N), out_dtype)
    else:
        layer, depth, prev = stack
        out_spec = pl.BlockSpec((None, tm, tn), lambda i, j, k: (layer, i, j))
        out_shape = jax.ShapeDtypeStruct((depth, M, N), out_dtype)
        if has_prev:
            in_specs.append(pl.BlockSpec(memory_space=pl.ANY))
            args.append(prev)
            aliases = {len(args) - 1: 0}
    return pl.pallas_call(
        body, name=name, grid=(M // tm, N // tn, nk), in_specs=in_specs, out_specs=out_spec, out_shape=out_shape,
        scratch_shapes=[pltpu.VMEM((tm, tn), F32)] if nk > 1 else [], input_output_aliases=aliases,
        compiler_params=_params("parallel", "parallel", "arbitrary"),
    )(*args)


def _rmsnorm(x, g, name):
    T, D = x.shape
    tm = _pick(T, 512, SUBLANES)

    def body(x_ref, g_ref, h_ref):
        xv = x_ref[...]
        r = lax.rsqrt(jnp.mean(xv * xv, axis=-1, keepdims=True) + EPS)
        h_ref[...] = (xv * r * g_ref[...]).astype(BF16)

    return pl.pallas_call(
        body, name=name, grid=(T // tm,),
        in_specs=[pl.BlockSpec((tm, D), lambda i: (i, 0)), pl.BlockSpec((1, D), lambda i: (0, 0))],
        out_specs=pl.BlockSpec((tm, D), lambda i: (i, 0)),
        out_shape=jax.ShapeDtypeStruct((T, D), BF16), compiler_params=_params("parallel"),
    )(x, g.reshape(1, D))


def _rmsnorm_bwd(x, g, dh, dres, name):
    T, D = x.shape
    tm = _pick(T, 512, SUBLANES)
    with_res = dres is not None

    def body(*refs):
        if with_res:
            x_ref, g_ref, dh_ref, dres_ref, dx_ref, dg_ref = refs
        else:
            x_ref, g_ref, dh_ref, dx_ref, dg_ref = refs
        xv = x_ref[...]
        dhv = dh_ref[...]
        r = lax.rsqrt(jnp.mean(xv * xv, axis=-1, keepdims=True) + EPS)
        dyg = dhv * g_ref[...]
        c = jnp.mean(dyg * xv, axis=-1, keepdims=True)
        dx = r * dyg - xv * (r * r * r) * c
        if with_res:
            dx = dx + dres_ref[...]
        dx_ref[...] = dx

        @pl.when(pl.program_id(0) == 0)
        def _():
            dg_ref[...] = jnp.zeros_like(dg_ref)

        dg_ref[...] += jnp.sum(dhv * xv * r, axis=0, keepdims=True)

    row = pl.BlockSpec((tm, D), lambda i: (i, 0))
    vec = pl.BlockSpec((1, D), lambda i: (0, 0))
    ins = [x, g.reshape(1, D), dh] + ([dres] if with_res else [])
    return pl.pallas_call(
        body, name=name, grid=(T // tm,), in_specs=[row, vec, row] + ([row] if with_res else []),
        out_specs=[row, vec],
        out_shape=[jax.ShapeDtypeStruct((T, D), F32), jax.ShapeDtypeStruct((1, D), F32)],
        compiler_params=_params("arbitrary"),
    )(*ins)


def _loss_head(x, g, target, name):
    T, D = x.shape
    tm = _pick(T, 512, SUBLANES)

    def body(x_ref, g_ref, t_ref, loss_ref, dx_ref, dg_ref):
        xv = x_ref[...]
        gv = g_ref[...]
        r = lax.rsqrt(jnp.mean(xv * xv, axis=-1, keepdims=True) + EPS)
        e = xv * r * gv - t_ref[...]
        dy = e * (1.0 / D)
        dyg = dy * gv
        c = jnp.mean(dyg * xv, axis=-1, keepdims=True)
        dx_ref[...] = r * dyg - xv * (r * r * r) * c

        @pl.when(pl.program_id(0) == 0)
        def _():
            loss_ref[...] = jnp.zeros_like(loss_ref)
            dg_ref[...] = jnp.zeros_like(dg_ref)

        loss_ref[...] += jnp.sum(e * e, axis=0, keepdims=True) * (0.5 / D)
        dg_ref[...] += jnp.sum(dy * xv * r, axis=0, keepdims=True)

    row = pl.BlockSpec((tm, D), lambda i: (i, 0))
    vec = pl.BlockSpec((1, D), lambda i: (0, 0))
    return pl.pallas_call(
        body, name=name, grid=(T // tm,), in_specs=[row, vec, row], out_specs=[vec, row, vec],
        out_shape=[jax.ShapeDtypeStruct((1, D), F32), jax.ShapeDtypeStruct((T, D), F32), jax.ShapeDtypeStruct((1, D), F32)],
        compiler_params=_params("arbitrary"),
    )(x, g.reshape(1, D), target)


def _ssm_disc_math(lre, lim, logdt, br, bi):
    dt = jnp.exp(logdt)
    mag = jnp.exp(lre * dt)
    ar = mag * jnp.cos(lim * dt)
    ai = mag * jnp.sin(lim * dt)
    den = lre * lre + lim * lim
    nr = ar - 1.0
    fr = (nr * lre + ai * lim) / den
    fi = (ai * lre - nr * lim) / den
    return ar, ai, fr[None] * br - fi[None] * bi, fr[None] * bi + fi[None] * br


def _ssm_disc(lre, lim, logdt, br, bi, name):
    def body(lre_ref, lim_ref, dt_ref, br_ref, bi_ref, ar_ref, ai_ref, bbr_ref, bbi_ref):
        ar, ai, bbr, bbi = _ssm_disc_math(lre_ref[...], lim_ref[...], dt_ref[...], br_ref[...], bi_ref[...])
        ar_ref[...] = ar
        ai_ref[...] = ai
        bbr_ref[...] = bbr
        bbi_ref[...] = bbi

    sd = jax.ShapeDtypeStruct
    return pl.pallas_call(
        body, name=name, out_shape=[sd(lre.shape, F32), sd(lre.shape, F32), sd(br.shape, F32), sd(br.shape, F32)],
    )(lre, lim, logdt, br, bi)


def _ssm_disc_bwd(lre, lim, logdt, br, bi, dar, dai, dbbr, dbbi, name):
    def body(lre_ref, lim_ref, dt_ref, br_ref, bi_ref, dar_ref, dai_ref, dbbr_ref, dbbi_ref,
             glre_ref, glim_ref, gdt_ref, gbr_ref, gbi_ref):
        _, vjp = jax.vjp(_ssm_disc_math, lre_ref[...], lim_ref[...], dt_ref[...], br_ref[...], bi_ref[...])
        glre, glim, gdt, gbr, gbi = vjp((dar_ref[...], dai_ref[...], dbbr_ref[...], dbbi_ref[...]))
        glre_ref[...] = glre
        glim_ref[...] = glim
        gdt_ref[...] = gdt
        gbr_ref[...] = gbr
        gbi_ref[...] = gbi

    sd = jax.ShapeDtypeStruct
    return pl.pallas_call(
        body, name=name,
        out_shape=[sd(lre.shape, F32), sd(lre.shape, F32), sd(logdt.shape, F32), sd(br.shape, F32), sd(br.shape, F32)],
    )(lre, lim, logdt, br, bi, dar, dai, dbbr, dbbi)


SSM_STEPS_FWD = 256
SSM_STEPS_BWD = 256


def _ssm_tiles(ref, v, off, steps, n):
    return [ref[v, pl.ds(off + j, steps, stride=SUBLANES), :] for j in range(n)]


def _ssm_fwd(uz, bmat, cmat, art, ait, dvec, name, steps=SSM_STEPS_FWD):
    T = uz.shape[0]
    nblk, cb, width = bmat.shape
    C = nblk * cb
    half = width // 2
    nt = half // LANES
    npair = nblk // 2
    kc = min(steps, T)
    nchunk = T // kc

    def body(u_ref, b_ref, c_ref, ar_ref, ai_ref, d_ref, y_ref, xr_ref, xi_ref, sr_ref, si_ref):
        @pl.when(pl.program_id(0) == 0)
        def _():
            sr_ref[...] = jnp.zeros_like(sr_ref)
            si_ref[...] = jnp.zeros_like(si_ref)

        uv = u_ref[...]
        for b in range(nblk):
            bu = _dot(uv[:, b * cb:(b + 1) * cb].astype(BF16), b_ref[b], NN)
            v, off = b // 2, nt * (b % 2)
            for j in range(nt):
                xr_ref[v, pl.ds(off + j, kc, stride=SUBLANES), :] = bu[:, j * LANES:(j + 1) * LANES]
                xi_ref[v, pl.ds(off + j, kc, stride=SUBLANES), :] = bu[:, half + j * LANES:half + (j + 1) * LANES]
        ars = [ar_ref[v] for v in range(npair)]
        ais = [ai_ref[v] for v in range(npair)]

        def step(k, carry):
            row = pl.ds(k * SUBLANES, SUBLANES)
            out = []
            for v in range(npair):
                xr, xi = carry[2 * v], carry[2 * v + 1]
                nr = ars[v] * xr - ais[v] * xi + xr_ref[v, row, :]
                ni = ars[v] * xi + ais[v] * xr + xi_ref[v, row, :]
                xr_ref[v, row, :] = nr
                xi_ref[v, row, :] = ni
                out += [nr, ni]
            return tuple(out)

        fin = tuple(ref[v] for v in range(npair) for ref in (sr_ref, si_ref))
        for k in range(kc):
            fin = step(k, fin)
        for v in range(npair):
            sr_ref[v] = fin[2 * v]
            si_ref[v] = fin[2 * v + 1]
        for b in range(nblk):
            v, off = b // 2, nt * (b % 2)
            xb = jnp.concatenate(_ssm_tiles(xr_ref, v, off, kc, nt) + _ssm_tiles(xi_ref, v, off, kc, nt), axis=1)
            cols = slice(b * cb, (b + 1) * cb)
            y_ref[:, cols] = _dot(xb.astype(BF16), c_ref[b], NN) + d_ref[:, cols] * uv[:, cols]

    whole = lambda a: pl.BlockSpec(a.shape, lambda c: (0,) * a.ndim)
    st = pl.BlockSpec((npair, kc * SUBLANES, LANES), lambda c: (0, c, 0))
    sd = jax.ShapeDtypeStruct
    return pl.pallas_call(
        body, name=name, grid=(nchunk,),
        in_specs=[pl.BlockSpec((kc, C), lambda c: (c, 0)), whole(bmat), whole(cmat), whole(art), whole(ait), whole(dvec)],
        out_specs=[pl.BlockSpec((kc, C), lambda c: (c, 0)), st, st],
        out_shape=[sd((T, C), F32), sd((npair, T * SUBLANES, LANES), F32), sd((npair, T * SUBLANES, LANES), F32)],
        scratch_shapes=[pltpu.VMEM((npair, SUBLANES, LANES), F32), pltpu.VMEM((npair, SUBLANES, LANES), F32)],
        compiler_params=_params("arbitrary"),
    )(uz, bmat, cmat, art, ait, dvec)


def _ssm_bwd(dy, uz, xr, xi, bmat, cmat, art, ait, dvec, dproj, name):
    T = uz.shape[0]
    nblk, cb, width = bmat.shape
    C = nblk * cb
    half = width // 2
    nt = half // LANES
    npair = nblk // 2
    kc = min(SSM_STEPS_BWD, T)
    nchunk = T // kc

    def body(dy_ref, u_ref, xr_ref, xi_ref, xpr_ref, xpi_ref, b_ref, c_ref, ar_ref, ai_ref, d_ref, _,
             du_ref, db_ref, dc_ref, dar_ref, dai_ref, dd_ref, gr_ref, gi_ref, sr_ref, si_ref):
        c = pl.program_id(0)

        @pl.when(c == 0)
        def _():
            for ref in (sr_ref, si_ref, db_ref, dc_ref, dar_ref, dai_ref, dd_ref):
                ref[...] = jnp.zeros_like(ref)

        dyv = dy_ref[...]
        uv = u_ref[...]
        for b in range(nblk):
            dx = _dot(dyv[:, b * cb:(b + 1) * cb].astype(BF16), c_ref[b], NT)
            v, off = b // 2, nt * (b % 2)
            for j in range(nt):
                gr_ref[v, pl.ds(off + j, kc, stride=SUBLANES), :] = dx[:, j * LANES:(j + 1) * LANES]
                gi_ref[v, pl.ds(off + j, kc, stride=SUBLANES), :] = dx[:, half + j * LANES:half + (j + 1) * LANES]
        ars = [ar_ref[v] for v in range(npair)]
        ais = [ai_ref[v] for v in range(npair)]

        def pair_update(v, gr, gi, row):
            nr = ars[v] * gr + ais[v] * gi + gr_ref[v, row, :]
            ni = ars[v] * gi - ais[v] * gr + gi_ref[v, row, :]
            gr_ref[v, row, :] = nr
            gi_ref[v, row, :] = ni
            return nr, ni

        def step(i, carry):
            k = kc - 1 - i
            row = pl.ds(k * SUBLANES, SUBLANES)
            prow = pl.ds((k - 1) * SUBLANES, SUBLANES)
            out = []
            for v in range(npair):
                gr, gi, sr, si = carry[4 * v:4 * v + 4]
                nr, ni = pair_update(v, gr, gi, row)
                pr, pi = xr_ref[v, prow, :], xi_ref[v, prow, :]
                out += [nr, ni, sr + pr * nr + pi * ni, si + pr * ni - pi * nr]
            return tuple(out)

        mid = tuple(ref[v] for v in range(npair) for ref in (sr_ref, si_ref, dar_ref, dai_ref))
        for i in range(kc - 1):
            mid = step(i, mid)
        live = (c < nchunk - 1).astype(F32)
        row0 = pl.ds(0, SUBLANES)
        for v in range(npair):
            gr, gi, sr, si = mid[4 * v:4 * v + 4]
            nr, ni = pair_update(v, gr, gi, row0)
            pr, pi = xpr_ref[v] * live, xpi_ref[v] * live
            sr_ref[v] = nr
            si_ref[v] = ni
            dar_ref[v] = sr + pr * nr + pi * ni
            dai_ref[v] = si + pr * ni - pi * nr
        for b in range(nblk):
            v, off = b // 2, nt * (b % 2)
            cols = slice(b * cb, (b + 1) * cb)
            gb = jnp.concatenate(_ssm_tiles(gr_ref, v, off, kc, nt) + _ssm_tiles(gi_ref, v, off, kc, nt), axis=1).astype(BF16)
            xb = jnp.concatenate(_ssm_tiles(xr_ref, v, off, kc, nt) + _ssm_tiles(xi_ref, v, off, kc, nt), axis=1).astype(BF16)
            du_ref[:, cols] = (_dot(gb, b_ref[b], NT) + dyv[:, cols] * d_ref[:, cols]).astype(BF16)
            db_ref[b] += _dot(uv[:, cols].astype(BF16), gb, TN)
            dc_ref[b] += _dot(dyv[:, cols].astype(BF16), xb, TN)
        dd_ref[...] += jnp.sum(dyv * uv, axis=0, keepdims=True)

    whole = lambda a: pl.BlockSpec(a.shape, lambda c: (0,) * a.ndim)
    rev = lambda c: (nchunk - 1 - c, 0)
    st = pl.BlockSpec((npair, kc * SUBLANES, LANES), lambda c: (0, nchunk - 1 - c, 0))
    stp = pl.BlockSpec((npair, SUBLANES, LANES), lambda c: (0, jnp.maximum((nchunk - 1 - c) * kc - 1, 0), 0))
    acc = lambda shape: pl.BlockSpec(shape, lambda c: (0,) * len(shape))
    sd = jax.ShapeDtypeStruct
    pair_shape = (npair, SUBLANES, LANES)
    return pl.pallas_call(
        body, name=name, grid=(nchunk,),
        in_specs=[pl.BlockSpec((kc, C), rev), pl.BlockSpec((kc, C), rev), st, st, stp, stp, whole(bmat), whole(cmat),
                  whole(art), whole(ait), whole(dvec), pl.BlockSpec(memory_space=pl.ANY)],
        out_specs=[pl.BlockSpec((kc, C), rev), acc(bmat.shape), acc(bmat.shape), acc(pair_shape), acc(pair_shape), acc((1, C))],
        out_shape=[sd(dproj.shape, BF16), sd(bmat.shape, F32), sd(bmat.shape, F32), sd(pair_shape, F32), sd(pair_shape, F32),
                   sd((1, C), F32)],
        scratch_shapes=[pltpu.VMEM((npair, kc * SUBLANES, LANES), F32), pltpu.VMEM((npair, kc * SUBLANES, LANES), F32),
                        pltpu.VMEM(pair_shape, F32), pltpu.VMEM(pair_shape, F32)],
        input_output_aliases={11: 0}, compiler_params=_params("arbitrary"),
    )(dy, uz, xr, xi, xr, xi, bmat, cmat, art, ait, dvec, dproj)


def _ssm_post(y, z, w_glu, b_glu, name):
    T, C = y.shape
    tm = _pick(T, 512, SUBLANES)

    def body(y_ref, z_ref, w_ref, b_ref, o_ref, a_ref):
        a, _ = _gelu_and_grad(y_ref[...])
        ab = a.astype(BF16)
        sg = _sigmoid(_dot(ab, w_ref[...], NN) + b_ref[...])
        sz, _ = _silu_and_grad(z_ref[...].astype(F32))
        o_ref[...] = (a * sg * sz).astype(BF16)
        a_ref[...] = ab

    row = pl.BlockSpec((tm, C), lambda i: (i, 0))
    return pl.pallas_call(
        body, name=name, grid=(T // tm,),
        in_specs=[row, row, pl.BlockSpec((C, C), lambda i: (0, 0)), pl.BlockSpec((1, C), lambda i: (0, 0))],
        out_specs=[row, row], out_shape=[jax.ShapeDtypeStruct((T, C), BF16)] * 2, compiler_params=_params("parallel"),
    )(y, z, w_glu, b_glu.reshape(1, C))


def _ssm_post_bwd(do, y, z, w_glu, b_glu, dproj, col, name):
    T, C = y.shape
    tm = _pick(T, 512, SUBLANES)

    def body(do_ref, y_ref, z_ref, w_ref, b_ref, _, dy_ref, dz_ref, ds_ref, db_ref):
        dov = do_ref[...]
        a, da_dy = _gelu_and_grad(y_ref[...])
        sg = _sigmoid(_dot(a.astype(BF16), w_ref[...], NN) + b_ref[...])
        sz, dsz = _silu_and_grad(z_ref[...].astype(F32))
        yg = a * sg
        dz_ref[...] = (dov * yg * dsz).astype(BF16)
        dyg = dov * sz
        ds = dyg * a * sg * (1.0 - sg)
        dsb = ds.astype(BF16)
        ds_ref[...] = dsb
        da = dyg * sg + _dot(dsb, w_ref[...], NT)
        dy_ref[...] = da * da_dy

        @pl.when(pl.program_id(0) == 0)
        def _():
            db_ref[...] = jnp.zeros_like(db_ref)

        db_ref[...] += jnp.sum(ds, axis=0, keepdims=True)

    row = pl.BlockSpec((tm, C), lambda i: (i, 0))
    vec = pl.BlockSpec((1, C), lambda i: (0, 0))
    sd = jax.ShapeDtypeStruct
    return pl.pallas_call(
        body, name=name, grid=(T // tm,),
        in_specs=[row, row, row, pl.BlockSpec((C, C), lambda i: (0, 0)), vec, pl.BlockSpec(memory_space=pl.ANY)],
        out_specs=[row, pl.BlockSpec((tm, C), lambda i: (i, col // C)), row, vec],
        out_shape=[sd((T, C), F32), sd(dproj.shape, BF16), sd((T, C), BF16), sd((1, C), F32)],
        input_output_aliases={5: 1}, compiler_params=_params("arbitrary"),
    )(do, y, z, w_glu, b_glu.reshape(1, C), dproj)


def _rel_bucket(dist):
    n = jnp.maximum(dist, 0)
    max_exact = NUM_BUCKETS // 2
    n_f = jnp.maximum(n, 1).astype(F32)
    large = max_exact + (jnp.log(n_f / max_exact) / math.log(REL_MAX_DISTANCE / max_exact)
                         * (NUM_BUCKETS - max_exact)).astype(jnp.int32)
    large = jnp.minimum(large, NUM_BUCKETS - 1)
    return jnp.where(n < max_exact, n, large)


def _band_tables():
    qi = jnp.arange(ATTN_BLOCK)[:, None]
    kj = jnp.arange(2 * ATTN_BLOCK)[None, :]
    delta = ATTN_BLOCK + qi - kj
    buckets, bands = [], []
    for window, dilation in ATTN_CONFIGS:
        bands.append((delta >= 0) & (delta <= window // dilation))
        buckets.append(_rel_bucket(jnp.maximum(delta, 0) * dilation))
    return jnp.stack(buckets), jnp.stack(bands)


ATTN_UNITS = 4


def _attn_units_for(g, forward):
    if ATTN_CONFIGS[g][1] >= 16:
        return ATTN_UNITS
    return 4 * ATTN_UNITS if forward else 2 * ATTN_UNITS


def _attn_tile(T, r, nu):
    nq = max(1, nu // r)
    rows = ATTN_BLOCK * r * nq
    return nq, rows, T // rows


def _attn_units(r, nq, chunk, nu):
    if r >= nu:
        return [(chunk * nu + i, None) for i in range(nu)]
    units = []
    for j in range(nq):
        for s in range(r):
            units.append((ATTN_BLOCK * j * r + s, ATTN_BLOCK * (j - 1) * r + s if j else None))
    return units


def _rows(start, r):
    return pl.ds(start, ATTN_BLOCK, stride=r) if r > 1 else pl.ds(start, ATTN_BLOCK)


def _attn_group_fwd(qkv, biasm, g, name, nu=ATTN_UNITS):
    T = qkv.shape[0]
    r = ATTN_CONFIGS[g][1]
    B, hd = ATTN_BLOCK, ATTN_HEAD_DIM
    nq, rows, ntiles = _attn_tile(T, r, nu)
    nchunks = max(1, r // nu)
    last_prev = B * (nq - 1) * r
    scale = hd ** -0.5
    tiles_per_tensor = 3 * HEADS_PER_GROUP * hd // LANES

    def body(q_ref, kc_ref, kp_ref, vc_ref, vp_ref, bias_ref, o_ref, lse_ref, s_ref, p_ref):
        n = pl.program_id(1)
        lane = lax.broadcasted_iota(jnp.int32, (1, LANES), 1)
        col = lax.broadcasted_iota(jnp.int32, (1, 2 * B), 1)
        masks = [lane < hd, lane >= hd]
        first_pen = jnp.where((col < B) & (n == 0), NEG_INF, 0.0)

        def chunk_body(chunk):
            units = _attn_units(r, nq, chunk, nu)

            def keys(cur_ref, prev_ref, cs, ps):
                prev = prev_ref[_rows(last_prev + (cs if r >= nu else cs % r), r), :] if ps is None else cur_ref[_rows(ps, r), :]
                return jnp.concatenate([prev, cur_ref[_rows(cs, r), :]], axis=0).astype(BF16)

            for u, (cs, ps) in enumerate(units):
                qv = q_ref[_rows(cs, r), :]
                kw = keys(kc_ref, kp_ref, cs, ps)
                for hh in range(2):
                    s_ref[2 * u + hh] = _dot(jnp.where(masks[hh], qv, 0.0).astype(BF16), kw, NT)
            for u, (cs, ps) in enumerate(units):
                lses = []
                for hh in range(2):
                    s = s_ref[2 * u + hh] * scale + bias_ref[hh]
                    if ps is None:
                        s = s + first_pen
                    m = jnp.max(s, axis=-1, keepdims=True)
                    p = jnp.exp(s - m)
                    l = jnp.sum(p, axis=-1, keepdims=True)
                    p_ref[2 * u + hh] = (p * (1.0 / l)).astype(BF16)
                    lses.append(m + jnp.log(l))
                lse_ref[_rows(cs, r), :] = jnp.where(masks[0], lses[0], lses[1])
            for u, (cs, ps) in enumerate(units):
                vw = keys(vc_ref, vp_ref, cs, ps)
                o_ref[_rows(cs, r), :] = (_dot(p_ref[2 * u], jnp.where(masks[0], vw, 0), NN)
                                          + _dot(p_ref[2 * u + 1], jnp.where(masks[1], vw, 0), NN))

        if nchunks == 1:
            chunk_body(0)
        else:
            pl.loop(0, nchunks)(chunk_body)

    def cur(t):
        return pl.BlockSpec((rows, LANES), lambda hf, n: (n, t * tiles_per_tensor + 2 * g + hf))

    def prev(t):
        return pl.BlockSpec((rows, LANES), lambda hf, n: (jnp.maximum(n - 1, 0), t * tiles_per_tensor + 2 * g + hf))

    out = pl.BlockSpec((rows, LANES), lambda hf, n: (n, hf))
    sd = jax.ShapeDtypeStruct((T, 2 * LANES), F32)
    return pl.pallas_call(
        body, name=name, grid=(2, ntiles),
        in_specs=[cur(0), cur(1), prev(1), cur(2), prev(2), pl.BlockSpec((None, 2, B, 2 * B), lambda hf, n: (g, hf, 0, 0))],
        out_specs=[out, out], out_shape=[sd, sd],
        scratch_shapes=[pltpu.VMEM((2 * nu, B, 2 * B), F32), pltpu.VMEM((2 * nu, B, 2 * B), BF16)],
        compiler_params=_params("parallel", "parallel"),
    )(qkv, qkv, qkv, qkv, qkv, biasm)


def _attn_group_bwd(qkv, do, dvec, lse, biasm, g, dproj, dk_col, name, nu=ATTN_UNITS):
    T = qkv.shape[0]
    r = ATTN_CONFIGS[g][1]
    B, hd = ATTN_BLOCK, ATTN_HEAD_DIM
    nq, rows, ntiles = _attn_tile(T, r, nu)
    nchunks = max(1, r // nu)
    last_prev = B * (nq - 1) * r
    scale = hd ** -0.5
    tiles_per_tensor = 3 * HEADS_PER_GROUP * hd // LANES

    def body(q_ref, kc_ref, kp_ref, vc_ref, vp_ref, do_ref, dv_ref, lse_ref, bias_ref, _,
             dq_ref, dk_ref, dvo_ref, dbias_ref, ck_ref, cv_ref, ak_ref, av_ref, s_ref, dp_ref, p_ref, ds_ref):
        n = pl.program_id(1)
        lane = lax.broadcasted_iota(jnp.int32, (1, LANES), 1)
        col = lax.broadcasted_iota(jnp.int32, (1, 2 * B), 1)
        masks = [lane < hd, lane >= hd]
        first_pen = jnp.where((col < B) & (n == 0), NEG_INF, 0.0)

        @pl.when(n == 0)
        def _():
            dbias_ref[...] = jnp.zeros_like(dbias_ref)
            ck_ref[...] = jnp.zeros_like(ck_ref)
            cv_ref[...] = jnp.zeros_like(cv_ref)

        def chunk_body(chunk):
            units = _attn_units(r, nq, chunk, nu)

            def prev_rows(cs):
                return _rows(last_prev + (cs if r >= nu else cs % r), r)

            def keys(cur_ref, prev_ref, cs, ps):
                prev = prev_ref[prev_rows(cs), :] if ps is None else cur_ref[_rows(ps, r), :]
                return jnp.concatenate([prev, cur_ref[_rows(cs, r), :]], axis=0).astype(BF16)

            for u, (cs, ps) in enumerate(units):
                qv = q_ref[_rows(cs, r), :]
                dov = do_ref[_rows(cs, r), :]
                kw = keys(kc_ref, kp_ref, cs, ps)
                vw = keys(vc_ref, vp_ref, cs, ps)
                for hh in range(2):
                    s_ref[2 * u + hh] = _dot(jnp.where(masks[hh], qv, 0.0).astype(BF16), kw, NT)
                    dp_ref[2 * u + hh] = _dot(jnp.where(masks[hh], dov, 0.0).astype(BF16), vw, NT)
            for u, (cs, ps) in enumerate(units):
                lse_t = lse_ref[_rows(cs, r), :]
                dv_t = dv_ref[_rows(cs, r), :]
                for hh in range(2):
                    lo = hh * hd
                    s = s_ref[2 * u + hh] * scale + bias_ref[hh]
                    if ps is None:
                        s = s + first_pen
                    p = jnp.exp(s - lse_t[:, lo:lo + 1])
                    ds = p * (dp_ref[2 * u + hh] + dv_t[:, lo:lo + 1])
                    dbias_ref[hh] += ds
                    p_ref[2 * u + hh] = p.astype(BF16)
                    ds_ref[2 * u + hh] = ds.astype(BF16)
            for u, (cs, ps) in enumerate(units):
                qv = q_ref[_rows(cs, r), :]
                dov = do_ref[_rows(cs, r), :]
                kw = keys(kc_ref, kp_ref, cs, ps)
                dq, dkw, dvw = 0.0, 0.0, 0.0
                for hh in range(2):
                    dsb = ds_ref[2 * u + hh]
                    dq = dq + _dot(dsb, jnp.where(masks[hh], kw, 0), NN)
                    dkw = dkw + _dot(dsb, jnp.where(masks[hh], qv, 0.0).astype(BF16), TN)
                    dvw = dvw + _dot(p_ref[2 * u + hh], jnp.where(masks[hh], dov, 0.0).astype(BF16), TN)
                dq_ref[_rows(cs, r), :] = dq * scale
                ak_ref[_rows(cs, r), :] = dkw[B:] * scale
                av_ref[_rows(cs, r), :] = dvw[B:]
                if ps is None:
                    ck_ref[prev_rows(cs), :] += dkw[:B] * scale
                    cv_ref[prev_rows(cs), :] += dvw[:B]
                else:
                    ak_ref[_rows(ps, r), :] += dkw[:B] * scale
                    av_ref[_rows(ps, r), :] += dvw[:B]

        @pl.when(n < ntiles)
        def _():
            for chunk in range(nchunks):
                chunk_body(chunk)

        dk_ref[...] = ck_ref[...].astype(BF16)
        dvo_ref[...] = cv_ref[...].astype(BF16)
        ck_ref[...] = ak_ref[...]
        cv_ref[...] = av_ref[...]

    last = ntiles - 1

    def cur(t):
        return pl.BlockSpec((rows, LANES), lambda hf, n: (jnp.minimum(n, last), t * tiles_per_tensor + 2 * g + hf))

    def prev(t):
        return pl.BlockSpec((rows, LANES), lambda hf, n: (jnp.clip(n - 1, 0, last), t * tiles_per_tensor + 2 * g + hf))

    nat = pl.BlockSpec((rows, LANES), lambda hf, n: (jnp.minimum(n, last), hf))
    nat_prev = pl.BlockSpec((rows, LANES), lambda hf, n: (jnp.clip(n - 1, 0, last), hf))
    tab = pl.BlockSpec((None, 2, B, 2 * B), lambda hf, n: (g, hf, 0, 0))
    dtab = pl.BlockSpec((2, B, 2 * B), lambda hf, n: (hf, 0, 0))
    sd = jax.ShapeDtypeStruct
    vm = pltpu.VMEM
    dk_tile = dk_col // LANES + 2 * g
    dk_spec = pl.BlockSpec((rows, LANES), lambda hf, n: (jnp.clip(n - 1, 0, last), dk_tile + hf))
    return pl.pallas_call(
        body, name=name, grid=(2, ntiles + 1),
        in_specs=[cur(0), cur(1), prev(1), cur(2), prev(2), nat, nat, nat, tab, pl.BlockSpec(memory_space=pl.ANY)],
        out_specs=[nat, dk_spec, nat_prev, dtab],
        out_shape=[sd((T, 2 * LANES), F32), sd(dproj.shape, BF16), sd((T, 2 * LANES), BF16),
                   sd((HEADS_PER_GROUP, B, 2 * B), F32)],
        scratch_shapes=[vm((rows, LANES), F32), vm((rows, LANES), F32), vm((rows, LANES), F32), vm((rows, LANES), F32),
                        vm((2 * nu, B, 2 * B), F32), vm((2 * nu, B, 2 * B), F32),
                        vm((2 * nu, B, 2 * B), BF16), vm((2 * nu, B, 2 * B), BF16)],
        input_output_aliases={9: 1}, compiler_params=_params("parallel", "arbitrary"),
    )(qkv, qkv, qkv, qkv, qkv, do, dvec, lse, biasm, dproj)


def _attn_mix(os, lses, z, name):
    T, gw = os[0].shape
    C = z.shape[1]
    tm = _pick(T, 512, SUBLANES)

    def body(o0_ref, o1_ref, o2_ref, l0_ref, l1_ref, l2_ref, z_ref, out_ref):
        ls = [l0_ref[...], l1_ref[...], l2_ref[...]]
        mx = jnp.maximum(jnp.maximum(ls[0], ls[1]), ls[2])
        es = [jnp.exp(l - mx) for l in ls]
        inv = 1.0 / (es[0] + es[1] + es[2])
        for i, o_ref in enumerate((o0_ref, o1_ref, o2_ref)):
            sz, _ = _silu_and_grad(z_ref[:, i * gw:(i + 1) * gw].astype(F32))
            out_ref[:, i * gw:(i + 1) * gw] = (o_ref[...] * (es[i] * inv) * sz).astype(BF16)

    row = pl.BlockSpec((tm, C), lambda i: (i, 0))
    grp = pl.BlockSpec((tm, gw), lambda i: (i, 0))
    return pl.pallas_call(
        body, name=name, grid=(T // tm,), in_specs=[grp] * 6 + [row], out_specs=row,
        out_shape=jax.ShapeDtypeStruct((T, C), BF16), compiler_params=_params("parallel"),
    )(*os, *lses, z)


def _attn_mix_bwd(dout, os, lses, z, dproj, col, name):
    T, gw = os[0].shape
    C = z.shape[1]
    tm = _pick(T, 512, SUBLANES)
    head_of = np.arange(gw) // ATTN_HEAD_DIM
    ones = jnp.asarray(head_of[:, None] == head_of[None, :], BF16)

    def body(dout_ref, o0_ref, o1_ref, o2_ref, l0_ref, l1_ref, l2_ref, z_ref, ones_ref, _,
             dz_ref, do0_ref, do1_ref, do2_ref, dv0_ref, dv1_ref, dv2_ref):
        ls = [l0_ref[...], l1_ref[...], l2_ref[...]]
        mx = jnp.maximum(jnp.maximum(ls[0], ls[1]), ls[2])
        es = [jnp.exp(l - mx) for l in ls]
        inv = 1.0 / (es[0] + es[1] + es[2])
        alphas, ebar = [], 0.0
        for i, (o_ref, do_ref) in enumerate(((o0_ref, do0_ref), (o1_ref, do1_ref), (o2_ref, do2_ref))):
            sl = slice(i * gw, (i + 1) * gw)
            alpha = es[i] * inv
            ov = o_ref[...]
            dv = dout_ref[:, sl]
            sz, dsz = _silu_and_grad(z_ref[:, sl].astype(F32))
            dz_ref[:, sl] = (dv * ov * alpha * dsz).astype(BF16)
            da = dv * sz
            do_ref[...] = da * alpha
            t = da * ov
            t1 = t.astype(BF16)
            r1 = t - t1.astype(F32)
            t2 = r1.astype(BF16)
            t3 = (r1 - t2.astype(F32)).astype(BF16)
            e = _dot(t1, ones_ref[...], NN) + _dot(t2, ones_ref[...], NN) + _dot(t3, ones_ref[...], NN)
            ebar = ebar + alpha * e
            alphas.append(alpha)
        for alpha, dv_ref in zip(alphas, (dv0_ref, dv1_ref, dv2_ref)):
            dv_ref[...] = -alpha * ebar

    row = pl.BlockSpec((tm, C), lambda i: (i, 0))
    grp = pl.BlockSpec((tm, gw), lambda i: (i, 0))
    sd = jax.ShapeDtypeStruct
    res = pl.pallas_call(
        body, name=name, grid=(T // tm,),
        in_specs=[row] + [grp] * 6 + [row, pl.BlockSpec((gw, gw), lambda i: (0, 0)), pl.BlockSpec(memory_space=pl.ANY)],
        out_specs=[pl.BlockSpec((tm, C), lambda i: (i, col // C))] + [grp] * 6,
        out_shape=[sd(dproj.shape, BF16)] + [sd((T, gw), F32)] * 6, input_output_aliases={9: 0},
        compiler_params=_params("parallel"),
    )(dout, *os, *lses, z, ones, dproj)
    return res[0], res[1:4], res[4:7]


def _mem_attn(qz, kv, name):
    T = qz.shape[0]
    dm = qz.shape[1] // 2
    M = kv.shape[0]
    hd = dm // MEM_HEADS
    scale = hd ** -0.5
    tm = _pick(T, 512, SUBLANES)

    def body(q_ref, z_ref, k_ref, v_ref, o_ref, s_ref, p_ref):
        heads = [slice(h * hd, (h + 1) * hd) for h in range(MEM_HEADS)]
        for h, sl in enumerate(heads):
            s_ref[h] = _dot(q_ref[:, sl].astype(BF16), k_ref[:, sl], NT)
        for h, sl in enumerate(heads):
            s = s_ref[h] * scale
            p = jnp.exp(s - jnp.max(s, axis=-1, keepdims=True))
            p_ref[h] = (p * (1.0 / jnp.sum(p, axis=-1, keepdims=True))).astype(BF16)
        for h, sl in enumerate(heads):
            sz, _ = _silu_and_grad(z_ref[:, sl].astype(F32))
            o_ref[:, sl] = (_dot(p_ref[h], v_ref[:, sl], NN) * sz).astype(BF16)

    return pl.pallas_call(
        body, name=name, grid=(T // tm,),
        in_specs=[pl.BlockSpec((tm, dm), lambda i: (i, 0)), pl.BlockSpec((tm, dm), lambda i: (i, 1)),
                  pl.BlockSpec((M, dm), lambda i: (0, 0)), pl.BlockSpec((M, dm), lambda i: (0, 1))],
        out_specs=pl.BlockSpec((tm, dm), lambda i: (i, 0)),
        out_shape=jax.ShapeDtypeStruct((T, dm), BF16),
        scratch_shapes=[pltpu.VMEM((MEM_HEADS, tm, M), F32), pltpu.VMEM((MEM_HEADS, tm, M), BF16)],
        compiler_params=_params("parallel"),
    )(qz, qz, kv, kv)


def _mem_attn_bwd(do, qz, kv, dproj, col, name):
    T = qz.shape[0]
    dm = qz.shape[1] // 2
    M = kv.shape[0]
    hd = dm // MEM_HEADS
    scale = hd ** -0.5
    tm = _pick(T, 512, SUBLANES)

    def body(do_ref, q_ref, z_ref, k_ref, v_ref, _, dq_ref, dz_ref, dk_ref, dv_ref, s_ref, dp_ref, p_ref, ds_ref, dob_ref):
        @pl.when(pl.program_id(0) == 0)
        def _():
            dk_ref[...] = jnp.zeros_like(dk_ref)
            dv_ref[...] = jnp.zeros_like(dv_ref)

        heads = [slice(h * hd, (h + 1) * hd) for h in range(MEM_HEADS)]
        for h, sl in enumerate(heads):
            sz, _ = _silu_and_grad(z_ref[:, sl].astype(F32))
            dob = (do_ref[:, sl] * sz).astype(BF16)
            dob_ref[:, sl] = dob
            s_ref[h] = _dot(q_ref[:, sl].astype(BF16), k_ref[:, sl], NT)
            dp_ref[h] = _dot(dob, v_ref[:, sl], NT)
        for h, sl in enumerate(heads):
            s = s_ref[h] * scale
            p = jnp.exp(s - jnp.max(s, axis=-1, keepdims=True))
            pn = p * (1.0 / jnp.sum(p, axis=-1, keepdims=True))
            dp = dp_ref[h]
            p_ref[h] = pn.astype(BF16)
            ds_ref[h] = (pn * (dp - jnp.sum(dp * pn, axis=-1, keepdims=True))).astype(BF16)
        for h, sl in enumerate(heads):
            _, dsz = _silu_and_grad(z_ref[:, sl].astype(F32))
            dz_ref[:, sl] = (do_ref[:, sl] * _dot(p_ref[h], v_ref[:, sl], NN) * dsz).astype(BF16)
            dq_ref[:, sl] = (_dot(ds_ref[h], k_ref[:, sl], NN) * scale).astype(BF16)
            dk_ref[:, sl] += _dot(ds_ref[h], q_ref[:, sl].astype(BF16), TN) * scale
            dv_ref[:, sl] += _dot(p_ref[h], dob_ref[:, sl], TN)

    rowq = pl.BlockSpec((tm, dm), lambda i: (i, 0))
    rowz = pl.BlockSpec((tm, dm), lambda i: (i, 1))
    kb = pl.BlockSpec((M, dm), lambda i: (0, 0))
    vb = pl.BlockSpec((M, dm), lambda i: (0, 1))
    sd = jax.ShapeDtypeStruct
    dq, dz, dk, dv = pl.pallas_call(
        body, name=name, grid=(T // tm,), in_specs=[rowq, rowq, rowz, kb, vb, pl.BlockSpec(memory_space=pl.ANY)],
        out_specs=[pl.BlockSpec((tm, dm), lambda i: (i, col // dm)), rowq, kb, kb],
        out_shape=[sd(dproj.shape, BF16), sd((T, dm), BF16), sd((M, dm), F32), sd((M, dm), F32)],
        scratch_shapes=[pltpu.VMEM((MEM_HEADS, tm, M), F32), pltpu.VMEM((MEM_HEADS, tm, M), F32),
                        pltpu.VMEM((MEM_HEADS, tm, M), BF16), pltpu.VMEM((MEM_HEADS, tm, M), BF16), pltpu.VMEM((tm, dm), BF16)],
        input_output_aliases={5: 0}, compiler_params=_params("arbitrary"),
    )(do, qz, qz, kv, kv, dproj)
    return dq, dz, dk, dv


def _merge(os, ws, L, logits, b_gate, name):
    T = os[0].shape[0]
    D = ws[0].shape[2]
    tm = _pick(T, 512, SUBLANES)

    def body(o0_ref, o1_ref, o2_ref, w0_ref, w1_ref, w2_ref, l_ref, b_ref, m_ref, p_ref):
        acc = 0.0
        for i, (o_ref, w_ref) in enumerate(((o0_ref, w0_ref), (o1_ref, w1_ref), (o2_ref, w2_ref))):
            sl = slice(i * D, (i + 1) * D)
            bp = _dot(o_ref[...], w_ref[...], NN)
            p_ref[i] = bp.astype(BF16)
            acc = acc + _sigmoid(l_ref[:, sl].astype(F32) + b_ref[:, sl]) * bp
        m_ref[...] = acc.astype(BF16)

    return pl.pallas_call(
        body, name=name, grid=(T // tm,),
        in_specs=[pl.BlockSpec((tm, o.shape[1]), lambda i: (i, 0)) for o in os]
        + [pl.BlockSpec((None,) + w.shape[1:], lambda i: (L, 0, 0)) for w in ws]
        + [pl.BlockSpec((tm, 3 * D), lambda i: (i, 0)), pl.BlockSpec((1, 3 * D), lambda i: (0, 0))],
        out_specs=[pl.BlockSpec((tm, D), lambda i: (i, 0)), pl.BlockSpec((3, tm, D), lambda i: (0, i, 0))],
        out_shape=[jax.ShapeDtypeStruct((T, D), BF16), jax.ShapeDtypeStruct((3, T, D), BF16)],
        compiler_params=_params("parallel"),
    )(*os, *ws, logits, b_gate.reshape(1, 3 * D))


def _merge_bwd(dmerged, bps, logits, b_gate, dproj_cols, dl_off, name):
    _, T, D = bps.shape
    tm = _pick(T, 2048, SUBLANES)
    cw = _pick(math.gcd(dl_off, D), 512, LANES)
    per = D // cw

    def body(dm_ref, p_ref, l_ref, b_ref, dl_ref, d_ref, db_ref):
        @pl.when(pl.program_id(1) == 0)
        def _():
            db_ref[...] = jnp.zeros_like(db_ref)

        dmv = dm_ref[...]
        gt = _sigmoid(l_ref[...].astype(F32) + b_ref[...])
        d_ref[...] = (dmv * gt).astype(BF16)
        dl = dmv * p_ref[...].astype(F32) * gt * (1.0 - gt)
        dl_ref[...] = dl.astype(BF16)
        db_ref[...] += jnp.sum(dl, axis=0, keepdims=True)

    stacked = pl.BlockSpec((None, tm, cw), lambda j, i: (j // per, i, j % per))
    sd = jax.ShapeDtypeStruct
    return pl.pallas_call(
        body, name=name, grid=(3 * per, T // tm),
        in_specs=[pl.BlockSpec((tm, cw), lambda j, i: (i, j % per)), stacked, pl.BlockSpec((tm, cw), lambda j, i: (i, j)),
                  pl.BlockSpec((1, cw), lambda j, i: (0, j))],
        out_specs=[pl.BlockSpec((tm, cw), lambda j, i: (i, dl_off // cw + j)), stacked, pl.BlockSpec((1, cw), lambda j, i: (0, j))],
        out_shape=[sd((T, dproj_cols), BF16), sd((3, T, D), BF16), sd((1, 3 * D), F32)],
        compiler_params=_params("parallel", "arbitrary"),
    )(dmerged, bps, logits, b_gate.reshape(1, 3 * D))


def _block_diag(w):
    nblk, ng, a, b = w.shape
    eye = jnp.eye(ng, dtype=w.dtype)
    return (w[:, :, :, None, :] * eye[None, :, None, :, None]).reshape(nblk, ng * a, ng * b)


def _block_diag_part(m, a, b):
    nblk = m.shape[0]
    ng = m.shape[1] // a
    m5 = m.reshape(nblk, ng, a, ng, b)
    eye = jnp.eye(ng, dtype=m.dtype)
    return jnp.sum(m5 * eye[None, :, None, :, None], axis=3)


def _ssm_matrices(p, L, tag):
    G, P = p["ssm_lambda_re"].shape[1:]
    Hg = SSM_GROUP
    gpb = SSM_BLOCK_CH // Hg
    nblk = G // gpb
    br = p["ssm_b_re"][L].transpose(2, 0, 1)
    bi = p["ssm_b_im"][L].transpose(2, 0, 1)
    disc_in = (p["ssm_lambda_re"][L], p["ssm_lambda_im"][L], p["ssm_log_dt"][L].reshape(G, 1), br, bi)
    ar, ai, bbr, bbi = _ssm_disc(*disc_in, name=f"ssm_disc_{tag}")
    amat = (ar.reshape(nblk // 2, SUBLANES, LANES), ai.reshape(nblk // 2, SUBLANES, LANES))
    bbr_g = bbr.transpose(1, 0, 2).reshape(nblk, gpb, Hg, P)
    bbi_g = bbi.transpose(1, 0, 2).reshape(nblk, gpb, Hg, P)
    bmat = jnp.concatenate([_block_diag(bbr_g), _block_diag(bbi_g)], axis=2).astype(BF16)
    cre = p["ssm_c_re"][L].reshape(nblk, gpb, Hg, P).transpose(0, 1, 3, 2)
    cim = p["ssm_c_im"][L].reshape(nblk, gpb, Hg, P).transpose(0, 1, 3, 2)
    cmat = jnp.concatenate([_block_diag(cre), -_block_diag(cim)], axis=1).astype(BF16)
    return disc_in, amat, bmat, cmat


def _layer_fwd(x, mem, p, wb, L, biasm):
    T, D = x.shape
    C = p["ssm_d"].shape[1]
    dm = wb["w_br_mem"].shape[1]
    tag = f"l{L}"
    s = {"x": x}
    h = _rmsnorm(x, p["norm_g"][L], f"norm_{tag}")
    offs = [int(o) for o in np.cumsum([0, C, C, 3 * 768, 768, 2 * dm, 3 * D])]
    names = ("uz", "z_ssm", "qkv", "z_attn", "qz_mem", "logits")
    dts = (F32, BF16, F32, BF16, BF16, BF16)
    for i, (nm, dt) in enumerate(zip(names, dts)):
        wide_bf16 = offs[i + 1] - offs[i] >= 1024 and dt == BF16
        tiles = dict(tm=2048) if wide_bf16 else {}
        s[nm] = _matmul(h, wb["w_in"], mode="nn", name=f"in_{nm}_{tag}", out_dtype=dt, b_lead=L, b_off=offs[i],
                        n_cols=offs[i + 1] - offs[i], **tiles)
    s["h"] = h

    disc_in, amat, bmat, cmat = _ssm_matrices(p, L, tag)
    dvec = p["ssm_d"][L].reshape(1, C)
    y, xr, xi = _ssm_fwd(s["uz"], bmat, cmat, *amat, dvec, f"ssm_scan_{tag}")
    o_ssm, a_glu = _ssm_post(y, s["z_ssm"], wb["w_glu"][L], p["b_glu"][L], f"ssm_post_{tag}")
    s.update(disc_in=disc_in, amat=amat, bmat=bmat, cmat=cmat, xr=xr, xi=xi, y=y, a_glu=a_glu, o_ssm=o_ssm)

    groups = [_attn_group_fwd(s["qkv"], biasm, g, f"attn_g{g}_{tag}", nu=_attn_units_for(g, True))
              for g in range(len(ATTN_CONFIGS))]
    os, lses = [o for o, _ in groups], [l for _, l in groups]
    o_attn = _attn_mix(os, lses, s["z_attn"], f"attn_mix_{tag}")
    s.update(os=os, lses=lses, o_attn=o_attn)

    mn = _rmsnorm(mem, p["mem_norm_g"][L], f"mem_norm_{tag}")
    kv = _matmul(mn, wb["w_mem_kv"], mode="nn", name=f"mem_kv_{tag}", out_dtype=BF16, b_lead=L)
    o_mem = _mem_attn(s["qz_mem"], kv, f"mem_attn_{tag}")
    s.update(mn=mn, kv=kv, o_mem=o_mem)

    merged, bps = _merge([o_ssm, o_attn, o_mem], [wb["w_br_ssm"], wb["w_br_attn"], wb["w_br_mem"]], L, s["logits"],
                         p["b_gate"][L], f"merge_{tag}")
    s.update(bps=bps, merged=merged)
    x_new = _matmul(merged, wb["w_out"], mode="nn", name=f"out_{tag}", add=x, b_lead=L)
    return x_new, s


def _layer_bwd(dx, mem, p, wb, L, s, biasm, gprev):
    T, D = dx.shape
    C = p["ssm_d"].shape[1]
    depth = p["norm_g"].shape[0]
    tag = f"l{L}"
    g = {}

    def wgrad(n, a, b, **tiles):
        g[n] = _matmul(a, b, mode="tn", name=f"d{n}_{tag}", out_dtype=BF16, stack=(L, depth, gprev.get(n)), **tiles)

    dmerged = _matmul(dx, wb["w_out"], mode="nt", name=f"d_merged_{tag}", b_lead=L)
    wgrad("w_out", s["merged"], dx)
    dm = s["qz_mem"].shape[1] // 2
    col = dict(zip(("u", "z_ssm", "q", "k", "v", "z_attn", "q_mem", "z_mem", "logits", "end"),
                   (int(o) for o in np.cumsum([0, C, C, 768, 768, 768, 768, dm, dm, 3 * D]))))
    dproj, dbps, g["b_gate"] = _merge_bwd(dmerged, s["bps"], s["logits"], p["b_gate"][L], col["end"], col["logits"],
                                          f"merge_bwd_{tag}")
    dos = []
    for i, (o, n) in enumerate(((s["o_ssm"], "w_br_ssm"), (s["o_attn"], "w_br_attn"), (s["o_mem"], "w_br_mem"))):
        dos.append(_matmul(dbps, wb[n], mode="nt", name=f"d_o_{n}_{tag}", a_lead=i, b_lead=L))
        g[n] = _matmul(o, dbps, mode="tn", name=f"d{n}_{tag}", out_dtype=BF16, b_lead=i, stack=(L, depth, gprev.get(n)))

    dy, dproj, ds_glu, g["b_glu"] = _ssm_post_bwd(dos[0], s["y"], s["z_ssm"], wb["w_glu"][L], p["b_glu"][L], dproj,
                                                  col["z_ssm"], f"ssm_post_bwd_{tag}")
    wgrad("w_glu", s["a_glu"], ds_glu)
    dvec = p["ssm_d"][L].reshape(1, C)
    dproj, dbm, dct, dar, dai, g["ssm_d"] = _ssm_bwd(dy, s["uz"], s["xr"], s["xi"], s["bmat"], s["cmat"], *s["amat"], dvec,
                                                     dproj, f"ssm_scan_bwd_{tag}")
    G, P = p["ssm_lambda_re"].shape[1:]
    Hg = SSM_GROUP
    half = dbm.shape[2] // 2
    dbbr = _block_diag_part(dbm[:, :, :half], Hg, P).reshape(G, Hg, P).transpose(1, 0, 2)
    dbbi = _block_diag_part(dbm[:, :, half:], Hg, P).reshape(G, Hg, P).transpose(1, 0, 2)
    g["ssm_c_re"] = _block_diag_part(dct[:, :, :half], Hg, P).reshape(G, Hg, P)
    g["ssm_c_im"] = -_block_diag_part(dct[:, :, half:], Hg, P).reshape(G, Hg, P)
    glre, glim, gdt, gbr, gbi = _ssm_disc_bwd(*s["disc_in"], dar.reshape(G, P), dai.reshape(G, P), dbbr, dbbi,
                                              name=f"ssm_disc_bwd_{tag}")
    g["ssm_lambda_re"], g["ssm_lambda_im"], g["ssm_log_dt"] = glre, glim, gdt.reshape(G)
    g["ssm_b_re"] = gbr.transpose(1, 2, 0)
    g["ssm_b_im"] = gbi.transpose(1, 2, 0)

    dproj, do_g, dvec_g = _attn_mix_bwd(dos[1], s["os"], s["lses"], s["z_attn"], dproj, col["z_attn"], f"attn_mix_bwd_{tag}")
    rest, dbias = [], []
    for gi in range(len(ATTN_CONFIGS)):
        dq_g, dproj, dv_g, db_g = _attn_group_bwd(s["qkv"], do_g[gi], dvec_g[gi], s["lses"][gi], biasm, gi, dproj, col["k"],
                                                  f"attn_bwd_g{gi}_{tag}", nu=_attn_units_for(gi, False))
        gw = dq_g.shape[1]
        rest += [(dq_g, col["q"] + gi * gw), (dv_g, col["v"] + gi * gw)]
        dbias.append(db_g)
    dbias = jnp.stack(dbias)

    dproj, dz_mem, dk_mem, dv_mem = _mem_attn_bwd(dos[2], s["qz_mem"], s["kv"], dproj, col["q_mem"], f"mem_attn_bwd_{tag}")
    rest.append((dz_mem, col["z_mem"]))
    for piece, at in rest:
        dproj = lax.dynamic_update_slice(dproj, piece.astype(BF16), (0, at))
    dkv = jnp.concatenate([dk_mem, dv_mem], axis=1)
    wgrad("w_mem_kv", s["mn"], dkv)
    dmn = _matmul(dkv, wb["w_mem_kv"], mode="nt", name=f"d_mn_{tag}", b_lead=L)
    _, g["mem_norm_g"] = _rmsnorm_bwd(mem, p["mem_norm_g"][L], dmn, None, f"mem_norm_bwd_{tag}")

    dh = _matmul(dproj, wb["w_in"], mode="nt", name=f"d_h_{tag}", b_lead=L, tm=512, tn=1024)
    wgrad("w_in", s["h"], dproj, tn=2304, tk=1024)
    dx_in, g["norm_g"] = _rmsnorm_bwd(s["x"], p["norm_g"][L], dh, dx, f"norm_bwd_{tag}")
    return dx_in, g, dbias


def _bucket_onehot(gi):
    buckets, bands = _band_tables()
    hit = (buckets[gi].reshape(1, -1) == jnp.arange(NUM_BUCKETS)[:, None]) & bands[gi].reshape(1, -1)
    return hit.astype(BF16)


def _bias_tables(rel_bias, name):
    _, bands = _band_tables()
    out = []
    for gi in range(len(ATTN_CONFIGS)):
        tab = rel_bias[:, gi * HEADS_PER_GROUP:(gi + 1) * HEADS_PER_GROUP].T
        flat = _matmul(tab, _bucket_onehot(gi), mode="nn", name=f"{name}_{gi}", split_a=3, tn=4096)
        out.append(jnp.where(bands[gi][None], flat.reshape(HEADS_PER_GROUP, ATTN_BLOCK, 2 * ATTN_BLOCK), NEG_INF))
    return jnp.stack(out)


def _rel_bias_grad(dbias_sum, name):
    cols = []
    for gi in range(len(ATTN_CONFIGS)):
        flat = dbias_sum[gi].reshape(HEADS_PER_GROUP, -1)
        cols.append(_matmul(flat, _bucket_onehot(gi), mode="nt", name=f"{name}_{gi}", split_a=2, tk=4096).T)
    return jnp.concatenate(cols, axis=1)


def _local_step(x, mem, target, p, wb):
    depth = p["norm_g"].shape[0]
    biasm = _bias_tables(p["rel_bias"], "bias_table")
    saved = []
    for L in range(depth):
        x, s = _layer_fwd(x, mem, p, wb, L, biasm)
        saved.append(s)
    loss_vec, dx, dgf = _loss_head(x, p["final_norm_g"], target, "loss_head")
    grads = {"final_norm_g": dgf.reshape(-1)}
    per_layer = [None] * depth
    dbias_sum = 0.0
    stacked = {}
    for L in reversed(range(depth)):
        dx, per_layer[L], dbias = _layer_bwd(dx, mem, p, wb, L, saved[L], biasm, stacked)
        stacked = {n: per_layer[L][n] for n, _ in BIG}
        dbias_sum = dbias_sum + dbias
    grads.update(stacked)
    for n in per_layer[0]:
        if n not in stacked:
            grads[n] = jnp.stack([per_layer[L][n].reshape(p[n].shape[1:]) for L in range(depth)])
    grads["rel_bias"] = _rel_bias_grad(dbias_sum, "d_rel_bias")
    return jnp.sum(loss_vec), dx, grads


def _chip_coords(j):
    return j // 2, j % 2


def _place_shard(shard, ax, chip, name):
    _, a, b = shard.shape
    ra = _pick(a, 256, 16)
    full = (2, a * N_CHIPS, b) if ax == 1 else (2, a, b * N_CHIPS)
    per = a // ra

    def body(j_ref, s_ref, o_ref):
        o_ref[...] = s_ref[...].astype(BF16)

    out_idx = (lambda l, i, j: (l, j[0] * per + i, 0)) if ax == 1 else (lambda l, i, j: (l, i, j[0]))
    return pl.pallas_call(
        body, name=name,
        grid_spec=pltpu.PrefetchScalarGridSpec(
            num_scalar_prefetch=1, grid=(2, per),
            in_specs=[pl.BlockSpec((None, ra, b), lambda l, i, j: (l, i, 0))],
            out_specs=pl.BlockSpec((None, ra, b), out_idx)),
        out_shape=jax.ShapeDtypeStruct(full, BF16), compiler_params=_params("parallel", "parallel"),
    )(chip, shard)


def _gather_shards(fulls, axes, name):
    n = len(fulls)
    widths = [a.shape[ax] // N_CHIPS for a, ax in zip(fulls, axes)]
    aligns = [LANES if ax == 2 else 16 for ax in axes]

    def body(*refs):
        outs = refs[n:2 * n]
        send_sems, recv_sems, fsend_sems, frecv_sems = refs[2 * n:]
        x, y, c = lax.axis_index("x"), lax.axis_index("y"), lax.axis_index("c")
        mine = 2 * x + y
        sibling = (x, y, 1 - c)

        def window(t, layer, j):
            start = pl.ds(pl.multiple_of(j * widths[t], aligns[t]), widths[t])
            return outs[t].at[(layer, start, slice(None)) if axes[t] == 1 else (layer, slice(None), start)]

        def over_ici(t, j, block):
            return pltpu.make_async_remote_copy(
                src_ref=window(t, c, mine), dst_ref=window(t, c, block), send_sem=send_sems.at[t, j],
                recv_sem=recv_sems.at[t, block], device_id=(*_chip_coords(j), c), device_id_type=MESH)

        def over_d2d(t, j, layer):
            return pltpu.make_async_remote_copy(
                src_ref=window(t, layer, j), dst_ref=window(t, layer, j), send_sem=fsend_sems.at[t, j],
                recv_sem=frecv_sems.at[t, j], device_id=sibling, device_id_type=MESH)

        for t in range(n):
            for j in range(N_CHIPS):
                @pl.when(j != mine)
                def _():
                    over_ici(t, j, mine).start()
        for t in range(n):
            for j in range(N_CHIPS):
                @pl.when(j != mine)
                def _():
                    over_ici(t, j, j).wait_recv()
                    over_d2d(t, j, c).start()
        for t in range(n):
            for j in range(N_CHIPS):
                @pl.when(j != mine)
                def _():
                    over_ici(t, j, mine).wait_send()
                    over_d2d(t, j, c).wait_send()
                    over_d2d(t, j, 1 - c).wait_recv()

    sem = pltpu.SemaphoreType.DMA
    return pl.pallas_call(
        body, name=name, in_specs=[HBM] * n, out_specs=[HBM] * n,
        out_shape=[jax.ShapeDtypeStruct(a.shape, a.dtype) for a in fulls],
        input_output_aliases={t: t for t in range(n)},
        scratch_shapes=[sem((n, N_CHIPS)), sem((n, N_CHIPS)), sem((n, N_CHIPS)), sem((n, N_CHIPS))],
    )(*fulls)


def _scatter_slices(arrays, axes, name):
    n = len(arrays)

    def piece(a, ax):
        if ax is None:
            return a.shape, None
        w = a.shape[ax] // N_CHIPS
        return a.shape[:ax] + (w,) + a.shape[ax + 1:], w

    shapes = [piece(a, ax) for a, ax in zip(arrays, axes)]

    def body(*refs):
        ins, outs = refs[:n], refs[n:2 * n]
        send_sems, recv_sems, loc_sems = refs[2 * n:]
        x, y, c = lax.axis_index("x"), lax.axis_index("y"), lax.axis_index("c")
        mine = 2 * x + y

        def src(t, j):
            ax, w = axes[t], shapes[t][1]
            if ax is None:
                return ins[t]
            idx = tuple(pl.ds(j * w, w) if d == ax else slice(None) for d in range(len(arrays[t].shape)))
            return ins[t].at[idx]

        for t in range(n):
            for j in range(N_CHIPS):
                @pl.when(j == mine)
                def _():
                    pltpu.make_async_copy(src(t, j), outs[t].at[j], loc_sems.at[t]).start()

                @pl.when(j != mine)
                def _():
                    pltpu.make_async_remote_copy(
                        src_ref=src(t, j), dst_ref=outs[t].at[mine], send_sem=send_sems.at[t, j], recv_sem=recv_sems.at[t, mine],
                        device_id=(*_chip_coords(j), c), device_id_type=MESH).start()
        for t in range(n):
            for j in range(N_CHIPS):
                @pl.when(j == mine)
                def _():
                    pltpu.make_async_copy(src(t, j), outs[t].at[j], loc_sems.at[t]).wait()

                @pl.when(j != mine)
                def _():
                    cp = pltpu.make_async_remote_copy(
                        src_ref=src(t, j), dst_ref=outs[t].at[j], send_sem=send_sems.at[t, j], recv_sem=recv_sems.at[t, j],
                        device_id=(*_chip_coords(j), c), device_id_type=MESH)
                    cp.wait_send()
                    cp.wait_recv()

    return pl.pallas_call(
        body, name=name, in_specs=[HBM] * n, out_specs=[HBM] * n,
        out_shape=[jax.ShapeDtypeStruct((N_CHIPS,) + sh, a.dtype) for a, (sh, _) in zip(arrays, shapes)],
        scratch_shapes=[pltpu.SemaphoreType.DMA((n, N_CHIPS)), pltpu.SemaphoreType.DMA((n, N_CHIPS)), pltpu.SemaphoreType.DMA((n,))],
    )(*arrays)


def _swap_layers(stacked, name):
    n = len(stacked)

    def body(*refs):
        ins, outs = refs[:n], refs[n:2 * n]
        send_sems, recv_sems = refs[2 * n:]
        c = lax.axis_index("c")
        peer = (lax.axis_index("x"), lax.axis_index("y"), 1 - c)
        cps = [pltpu.make_async_remote_copy(src_ref=ins[t].at[1 - c], dst_ref=outs[t], send_sem=send_sems.at[t],
                                            recv_sem=recv_sems.at[t], device_id=peer, device_id_type=MESH) for t in range(n)]
        for cp in cps:
            cp.start()
        for cp in cps:
            cp.wait_send()
            cp.wait_recv()

    return pl.pallas_call(
        body, name=name, in_specs=[HBM] * n, out_specs=[HBM] * n,
        out_shape=[jax.ShapeDtypeStruct(a.shape[1:], a.dtype) for a in stacked],
        scratch_shapes=[pltpu.SemaphoreType.DMA((n,)), pltpu.SemaphoreType.DMA((n,))],
    )(*stacked)


def _merge_layers(stacked, name):
    n = len(stacked)

    def body(*refs):
        outs = refs[n:2 * n]
        send_sems, recv_sems = refs[2 * n:]
        c = lax.axis_index("c")
        peer = (lax.axis_index("x"), lax.axis_index("y"), 1 - c)
        for t in range(n):
            pltpu.make_async_remote_copy(src_ref=outs[t].at[c], dst_ref=outs[t].at[c], send_sem=send_sems.at[t],
                                         recv_sem=recv_sems.at[t], device_id=peer, device_id_type=MESH).start()
        for t in range(n):
            cp = pltpu.make_async_remote_copy(src_ref=outs[t].at[c], dst_ref=outs[t].at[1 - c], send_sem=send_sems.at[t],
                                              recv_sem=recv_sems.at[t], device_id=peer, device_id_type=MESH)
            cp.wait_send()
            cp.wait_recv()

    sem = pltpu.SemaphoreType.DMA
    return pl.pallas_call(
        body, name=name, in_specs=[HBM] * n, out_specs=[HBM] * n,
        out_shape=[jax.ShapeDtypeStruct(a.shape, a.dtype) for a in stacked],
        input_output_aliases={t: t for t in range(n)}, scratch_shapes=[sem((n,)), sem((n,))],
    )(*stacked)


def _pair_sum(stacked, landed, core, name):
    _, K, N = stacked.shape
    tr = _pick(K, max(16, (1 << 19) // N // 16 * 16), 16)

    def body(c_ref, s_ref, l_ref, o_ref):
        o_ref[...] = (s_ref[...].astype(F32) + l_ref[...].astype(F32)).astype(o_ref.dtype)

    return pl.pallas_call(
        body, name=name,
        grid_spec=pltpu.PrefetchScalarGridSpec(
            num_scalar_prefetch=1, grid=(K // tr,),
            in_specs=[pl.BlockSpec((None, tr, N), lambda i, c: (c[0], i, 0)), pl.BlockSpec((tr, N), lambda i, c: (i, 0))],
            out_specs=pl.BlockSpec((tr, N), lambda i, c: (i, 0))),
        out_shape=jax.ShapeDtypeStruct((K, N), stacked.dtype), compiler_params=_params("parallel"),
    )(core, stacked, landed)


def _sum_chips(landed, core, name):
    _, R, C = landed.shape
    tr = _pick(R, max(SUBLANES, (1 << 19) // C // 16 * 16), 16)

    def body(c_ref, l_ref, o_ref):
        acc = l_ref[0].astype(F32) + l_ref[1].astype(F32)
        acc = acc + l_ref[2].astype(F32)
        o_ref[...] = acc + l_ref[3].astype(F32)

    return pl.pallas_call(
        body, name=name,
        grid_spec=pltpu.PrefetchScalarGridSpec(
            num_scalar_prefetch=1, grid=(R // tr,),
            in_specs=[pl.BlockSpec((N_CHIPS, tr, C), lambda i, c: (0, i, 0))],
            out_specs=pl.BlockSpec((None, tr, C), lambda i, c: (c[0], i, 0))),
        out_shape=jax.ShapeDtypeStruct((2, R, C), F32), compiler_params=_params("parallel"),
    )(core, landed)


def _adamw_math(w_ref, g_ref, m_ref, v_ref, d_ref, nm_ref, nv_ref):
    c1 = 1.0 / (1.0 - ADAM_B1 ** ADAM_STEP)
    c2 = 1.0 / (1.0 - ADAM_B2 ** ADAM_STEP)
    g = g_ref[...]
    nm = ADAM_B1 * m_ref[...] + (1.0 - ADAM_B1) * g
    nv = ADAM_B2 * v_ref[...] + (1.0 - ADAM_B2) * (g * g)
    nm_ref[...] = nm
    nv_ref[...] = nv
    d_ref[...] = -ADAM_LR * ((nm * c1) / (jnp.sqrt(nv * c2) + ADAM_EPS) + ADAM_WD * w_ref[...])


def _adamw_whole(w, g, m, v, name):
    shape = w.shape
    view = (-1,) + shape[-2:] if w.ndim >= 2 else (1, 1, -1)

    def body(*refs):
        _adamw_math(*refs)

    res = pl.pallas_call(body, name=name, out_shape=[jax.ShapeDtypeStruct(w.reshape(view).shape, F32)] * 3,
                         compiler_params=pltpu.CompilerParams(vmem_limit_bytes=VMEM_LIMIT_BYTES))(
        *(a.reshape(view) for a in (w, g, m, v)))
    return [r.reshape(shape) for r in res]


def _adamw(w, g, m, v, name):
    R, C = w.shape
    tr = _pick(R, max(SUBLANES, (1 << 18) // C // 8 * 8), SUBLANES)

    def body(*refs):
        _adamw_math(*refs)

    blk = pl.BlockSpec((tr, C), lambda i: (i, 0))
    return pl.pallas_call(
        body, name=name, grid=(R // tr,), in_specs=[blk] * 4, out_specs=[blk] * 3,
        out_shape=[jax.ShapeDtypeStruct((R, C), F32)] * 3, compiler_params=_params("parallel"),
    )(w, g, m, v)


def _pack_small(d, prefix=""):
    flat = jnp.concatenate([d[prefix + n].astype(F32).reshape(-1) for n in SMALL])
    pad = (-flat.shape[0]) % (2 * 16 * LANES)
    return jnp.pad(flat, (0, pad)).reshape(-1, LANES)


def _unpack_small(packed, shapes):
    flat = packed.reshape(-1)
    out, off = {}, 0
    for n in SMALL:
        size = int(np.prod(shapes[n]))
        out[n] = flat[off:off + size].reshape(shapes[n])
        off += size
    return out


def kernel(*args):
    p = dict(zip(INPUTS, args))
    x, mem, target = p["x"][0], p["mem"][0], p["loss_target"][0]

    names = [n for n, _ in BIG] + ["small"]
    core = lax.axis_index("c").astype(jnp.int32).reshape(1)
    chip = (2 * lax.axis_index("x") + lax.axis_index("y")).astype(jnp.int32).reshape(1)
    placed = [_place_shard(p[n], ax, chip, f"place_{n}") for n, ax in BIG]
    wb = dict(zip(names, _gather_shards(placed, [ax for _, ax in BIG], "gather_weights")))

    loss_part, dx, grads = _local_step(x, mem, target, p, wb)
    loss = lax.psum(loss_part, ("x", "y", "c"))

    stacked = [grads[n] for n, _ in BIG] + [_pack_small(grads).reshape(2, -1, LANES)]
    theirs = _swap_layers(stacked, "swap_layers")
    pair = [_pair_sum(s, o, core, f"pair_sum_{n}") for n, s, o in zip(names, stacked, theirs)]
    landed = _scatter_slices(pair, [ax - 1 for _, ax in BIG] + [None], "scatter_grads")
    reduced = [_sum_chips(ld.reshape(N_CHIPS, -1, ld.shape[-1]), core, f"sum_chips_{n}") for n, ld in zip(names, landed)]
    total = _merge_layers(reduced, "merge_layers")

    out = {}
    for (n, _), g in zip(BIG, total):
        sh = p[n].shape
        two_d = lambda a: a.reshape(-1, sh[-1])
        res = (g,) + tuple(_adamw(two_d(p[n]), two_d(g), two_d(p["m_" + n]), two_d(p["v_" + n]), f"adamw_{n}"))
        for key, r in zip(("grad_", "delta_", "new_m_", "new_v_"), res):
            out[key + n] = r.reshape(sh)
    for n, g in _unpack_small(total[-1], {n: p[n].shape for n in SMALL}).items():
        res = (g,) + tuple(_adamw_whole(p[n], g, p["m_" + n], p["v_" + n], f"adamw_{n}"))
        for key, r in zip(("grad_", "delta_", "new_m_", "new_v_"), res):
            out[key + n] = r

    result = [loss, dx.reshape(p["x"].shape)]
    for key in ("grad_", "delta_", "new_m_", "new_v_"):
        result += [out[key + n] for n in WEIGHTS]
    return tuple(result)
```

```python
import math

import jax
import jax.numpy as jnp
import numpy as np
from jax import lax
from jax.experimental import pallas as pl
from jax.experimental.pallas import tpu as pltpu

F32 = jnp.float32
BF16 = jnp.bfloat16
MESH = pl.DeviceIdType.MESH
HBM = pl.BlockSpec(memory_space=pltpu.HBM)

EPS = 1e-6
SSM_GROUP = 16
SSM_STATE = 64
ATTN_HEAD_DIM = 64
HEADS_PER_GROUP = 4
ATTN_CONFIGS = ((128, 1), (512, 4), (2048, 16))
ATTN_BLOCK = 128
NUM_BUCKETS = 32
REL_MAX_DISTANCE = 2048
NEG_INF = -1e30
MEM_HEADS = 4
ADAM_LR = 0.001
ADAM_B1 = 0.9
ADAM_B2 = 0.999
ADAM_EPS = 1e-08
ADAM_WD = 0.01
ADAM_STEP = 10

LANES = 128
SUBLANES = 8
VMEM_LIMIT_BYTES = 48 * 1024 * 1024
SSM_BLOCK_CH = 128

N_CHIPS = 4
BIG = (("w_in", 2), ("w_glu", 1), ("w_mem_kv", 1), ("w_br_ssm", 2), ("w_br_attn", 2), ("w_br_mem", 2), ("w_out", 1))
SMALL = ("norm_g", "mem_norm_g", "b_gate", "ssm_lambda_re", "ssm_lambda_im", "ssm_log_dt", "ssm_b_re", "ssm_b_im",
         "ssm_c_re", "ssm_c_im", "ssm_d", "b_glu", "rel_bias", "final_norm_g")
WEIGHTS = ("norm_g", "mem_norm_g", "w_in", "b_gate", "ssm_lambda_re", "ssm_lambda_im", "ssm_log_dt", "ssm_b_re",
           "ssm_b_im", "ssm_c_re", "ssm_c_im", "ssm_d", "w_glu", "b_glu", "w_mem_kv", "w_br_ssm", "w_br_attn",
           "w_br_mem", "w_out", "rel_bias", "final_norm_g")
INPUTS = ("x", "mem") + WEIGHTS + ("loss_target",) + tuple("m_" + n for n in WEIGHTS) + tuple("v_" + n for n in WEIGHTS)


def _params(*sem):
    return pltpu.CompilerParams(dimension_semantics=sem, vmem_limit_bytes=VMEM_LIMIT_BYTES)


def _pick(dim, pref, align):
    if dim <= pref:
        return dim
    t = pref - pref % align
    while t >= align:
        if dim % t == 0:
            return t
        t -= align
    return dim


def _sigmoid(v):
    return 0.5 * jnp.tanh(0.5 * v) + 0.5


def _silu_and_grad(z):
    s = _sigmoid(z)
    return z * s, s * (1.0 + z * (1.0 - s))


_GELU_C = math.sqrt(2.0 / math.pi)


def _gelu_and_grad(y):
    inner = _GELU_C * (y + 0.044715 * y * y * y)
    t = jnp.tanh(inner)
    g = 0.5 * y * (1.0 + t)
    dg = 0.5 * (1.0 + t) + 0.5 * y * (1.0 - t * t) * _GELU_C * (1.0 + 3.0 * 0.044715 * y * y)
    return g, dg


def _dot(a, b, dims):
    return lax.dot_general(a, b, (dims, ((), ())), preferred_element_type=F32)


NN = ((1,), (0,))
NT = ((1,), (1,))
TN = ((0,), (0,))


def _matmul(a, b, *, mode, name, out_dtype=F32, add=None, split_a=1, tm=1024, tn=768, tk=2304,
            a_lead=None, b_lead=None, b_off=0, n_cols=None, stack=None):
    ashape = a.shape if a_lead is None else a.shape[1:]
    K, M = ashape if mode == "tn" else ashape[::-1]
    bshape = b.shape if b_lead is None else b.shape[1:]
    N = n_cols or (bshape[0] if mode == "nt" else bshape[1])
    if mode != "tn" and M >= 4 * tm:
        tm = 2 * tm
    tm = _pick(M, tm, LANES if mode == "tn" else SUBLANES)
    tn = _pick(math.gcd(N, b_off) if b_off else N, tn, LANES)
    tk = _pick(K, tk, LANES)
    nk = K // tk
    joff = b_off // tn
    dims = {"nn": NN, "nt": NT, "tn": TN}[mode]
    has_add = add is not None
    has_prev = stack is not None and stack[2] is not None

    def body(*refs):
        a_ref, b_ref = refs[:2]
        add_ref = refs[2] if has_add else None
        o_ref = refs[-2] if nk > 1 else refs[-1]
        k = pl.program_id(2)
        bv = b_ref[...].astype(BF16)
        if split_a > 1:
            rest = a_ref[...].astype(F32)
            part = 0.0
            for _ in range(split_a):
                piece = rest.astype(BF16)
                part = part + _dot(piece, bv, dims)
                rest = rest - piece.astype(F32)
        else:
            part = _dot(a_ref[...].astype(BF16), bv, dims)

        def finish(r):
            if has_add:
                r = r + add_ref[...]
            o_ref[...] = r.astype(out_dtype)

        if nk == 1:
            finish(part)
            return
        acc_ref = refs[-1]

        @pl.when(k == 0)
        def _():
            acc_ref[...] = part

        @pl.when((k > 0) & (k < nk - 1))
        def _():
            acc_ref[...] += part

        @pl.when(k == nk - 1)
        def _():
            finish(acc_ref[...] + part)

    alead = () if a_lead is None else (a_lead,)
    alead_blk = () if a_lead is None else (None,)
    if mode == "tn":
        a_spec = pl.BlockSpec(alead_blk + (tk, tm), lambda i, j, k: alead + (k, i))
    else:
        a_spec = pl.BlockSpec(alead_blk + (tm, tk), lambda i, j, k: alead + (i, k))
    lead = () if b_lead is None else (b_lead,)
    lead_blk = () if b_lead is None else (None,)
    if mode == "nt":
        b_spec = pl.BlockSpec(lead_blk + (tn, tk), lambda i, j, k: lead + (j + joff, k))
    else:
        b_spec = pl.BlockSpec(lead_blk + (tk, tn), lambda i, j, k: lead + (k, j + joff))
    in_specs = [a_spec, b_spec]
    args = [a, b]
    if has_add:
        in_specs.append(pl.BlockSpec((tm, tn), lambda i, j, k: (i, j)))
        args.append(add)
    aliases = {}
    if stack is None:
        out_spec = pl.BlockSpec((tm, tn), lambda i, j, k: (i, j))
        out_shape = jax.ShapeDtypeStruct((M, N), out_dtype)
    else:
        layer, depth, prev = stack
        out_spec = pl.BlockSpec((None, tm, tn), lambda i, j, k: (layer, i, j))
        out_shape = jax.ShapeDtypeStruct((depth, M, N), out_dtype)
        if has_prev:
            in_specs.append(pl.BlockSpec(memory_space=pl.ANY))
            args.append(prev)
            aliases = {len(args) - 1: 0}
    return pl.pallas_call(
        body, name=name, grid=(M // tm, N // tn, nk), in_specs=in_specs, out_specs=out_spec, out_shape=out_shape,
        scratch_shapes=[pltpu.VMEM((tm, tn), F32)] if nk > 1 else [], input_output_aliases=aliases,
        compiler_params=_params("parallel", "parallel", "arbitrary"),
    )(*args)


def _rmsnorm(x, g, name):
    T, D = x.shape
    tm = _pick(T, 512, SUBLANES)

    def body(x_ref, g_ref, h_ref):
        xv = x_ref[...]
        r = lax.rsqrt(jnp.mean(xv * xv, axis=-1, keepdims=True) + EPS)
        h_ref[...] = (xv * r * g_ref[...]).astype(BF16)

    return pl.pallas_call(
        body, name=name, grid=(T // tm,),
        in_specs=[pl.BlockSpec((tm, D), lambda i: (i, 0)), pl.BlockSpec((1, D), lambda i: (0, 0))],
        out_specs=pl.BlockSpec((tm, D), lambda i: (i, 0)),
        out_shape=jax.ShapeDtypeStruct((T, D), BF16), compiler_params=_params("parallel"),
    )(x, g.reshape(1, D))


def _rmsnorm_bwd(x, g, dh, dres, name):
    T, D = x.shape
    tm = _pick(T, 512, SUBLANES)
    with_res = dres is not None

    def body(*refs):
        if with_res:
            x_ref, g_ref, dh_ref, dres_ref, dx_ref, dg_ref = refs
        else:
            x_ref, g_ref, dh_ref, dx_ref, dg_ref = refs
        xv = x_ref[...]
        dhv = dh_ref[...]
        r = lax.rsqrt(jnp.mean(xv * xv, axis=-1, keepdims=True) + EPS)
        dyg = dhv * g_ref[...]
        c = jnp.mean(dyg * xv, axis=-1, keepdims=True)
        dx = r * dyg - xv * (r * r * r) * c
        if with_res:
            dx = dx + dres_ref[...]
        dx_ref[...] = dx

        @pl.when(pl.program_id(0) == 0)
        def _():
            dg_ref[...] = jnp.zeros_like(dg_ref)

        dg_ref[...] += jnp.sum(dhv * xv * r, axis=0, keepdims=True)

    row = pl.BlockSpec((tm, D), lambda i: (i, 0))
    vec = pl.BlockSpec((1, D), lambda i: (0, 0))
    ins = [x, g.reshape(1, D), dh] + ([dres] if with_res else [])
    return pl.pallas_call(
        body, name=name, grid=(T // tm,), in_specs=[row, vec, row] + ([row] if with_res else []),
        out_specs=[row, vec],
        out_shape=[jax.ShapeDtypeStruct((T, D), F32), jax.ShapeDtypeStruct((1, D), F32)],
        compiler_params=_params("arbitrary"),
    )(*ins)


def _loss_head(x, g, target, name):
    T, D = x.shape
    tm = _pick(T, 512, SUBLANES)

    def body(x_ref, g_ref, t_ref, loss_ref, dx_ref, dg_ref):
        xv = x_ref[...]
        gv = g_ref[...]
        r = lax.rsqrt(jnp.mean(xv * xv, axis=-1, keepdims=True) + EPS)
        e = xv * r * gv - t_ref[...]
        dy = e * (1.0 / D)
        dyg = dy * gv
        c = jnp.mean(dyg * xv, axis=-1, keepdims=True)
        dx_ref[...] = r * dyg - xv * (r * r * r) * c

        @pl.when(pl.program_id(0) == 0)
        def _():
            loss_ref[...] = jnp.zeros_like(loss_ref)
            dg_ref[...] = jnp.zeros_like(dg_ref)

        loss_ref[...] += jnp.sum(e * e, axis=0, keepdims=True) * (0.5 / D)
        dg_ref[...] += jnp.sum(dy * xv * r, axis=0, keepdims=True)

    row = pl.BlockSpec((tm, D), lambda i: (i, 0))
    vec = pl.BlockSpec((1, D), lambda i: (0, 0))
    return pl.pallas_call(
        body, name=name, grid=(T // tm,), in_specs=[row, vec, row], out_specs=[vec, row, vec],
        out_shape=[jax.ShapeDtypeStruct((1, D), F32), jax.ShapeDtypeStruct((T, D), F32), jax.ShapeDtypeStruct((1, D), F32)],
        compiler_params=_params("arbitrary"),
    )(x, g.reshape(1, D), target)


def _ssm_disc_math(lre, lim, logdt, br, bi):
    dt = jnp.exp(logdt)
    mag = jnp.exp(lre * dt)
    ar = mag * jnp.cos(lim * dt)
    ai = mag * jnp.sin(lim * dt)
    den = lre * lre + lim * lim
    nr = ar - 1.0
    fr = (nr * lre + ai * lim) / den
    fi = (ai * lre - nr * lim) / den
    return ar, ai, fr[None] * br - fi[None] * bi, fr[None] * bi + fi[None] * br


def _ssm_disc(lre, lim, logdt, br, bi, name):
    def body(lre_ref, lim_ref, dt_ref, br_ref, bi_ref, ar_ref, ai_ref, bbr_ref, bbi_ref):
        ar, ai, bbr, bbi = _ssm_disc_math(lre_ref[...], lim_ref[...], dt_ref[...], br_ref[...], bi_ref[...])
        ar_ref[...] = ar
        ai_ref[...] = ai
        bbr_ref[...] = bbr
        bbi_ref[...] = bbi

    sd = jax.ShapeDtypeStruct
    return pl.pallas_call(
        body, name=name, out_shape=[sd(lre.shape, F32), sd(lre.shape, F32), sd(br.shape, F32), sd(br.shape, F32)],
    )(lre, lim, logdt, br, bi)


def _ssm_disc_bwd(lre, lim, logdt, br, bi, dar, dai, dbbr, dbbi, name):
    def body(lre_ref, lim_ref, dt_ref, br_ref, bi_ref, dar_ref, dai_ref, dbbr_ref, dbbi_ref,
             glre_ref, glim_ref, gdt_ref, gbr_ref, gbi_ref):
        _, vjp = jax.vjp(_ssm_disc_math, lre_ref[...], lim_ref[...], dt_ref[...], br_ref[...], bi_ref[...])
        glre, glim, gdt, gbr, gbi = vjp((dar_ref[...], dai_ref[...], dbbr_ref[...], dbbi_ref[...]))
        glre_ref[...] = glre
        glim_ref[...] = glim
        gdt_ref[...] = gdt
        gbr_ref[...] = gbr
        gbi_ref[...] = gbi

    sd = jax.ShapeDtypeStruct
    return pl.pallas_call(
        body, name=name,
        out_shape=[sd(lre.shape, F32), sd(lre.shape, F32), sd(logdt.shape, F32), sd(br.shape, F32), sd(br.shape, F32)],
    )(lre, lim, logdt, br, bi, dar, dai, dbbr, dbbi)


SSM_STEPS_FWD = 256
SSM_STEPS_BWD = 256


def _ssm_tiles(ref, v, off, steps, n):
    return [ref[v, pl.ds(off + j, steps, stride=SUBLANES), :] for j in range(n)]


def _ssm_fwd(uz, bmat, cmat, art, ait, dvec, name, steps=SSM_STEPS_FWD):
    T = uz.shape[0]
    nblk, cb, width = bmat.shape
    C = nblk * cb
    half = width // 2
    nt = half // LANES
    npair = nblk // 2
    kc = min(steps, T)
    nchunk = T // kc

    def body(u_ref, b_ref, c_ref, ar_ref, ai_ref, d_ref, y_ref, xr_ref, xi_ref, sr_ref, si_ref):
        @pl.when(pl.program_id(0) == 0)
        def _():
            sr_ref[...] = jnp.zeros_like(sr_ref)
            si_ref[...] = jnp.zeros_like(si_ref)

        uv = u_ref[...]
        for b in range(nblk):
            bu = _dot(uv[:, b * cb:(b + 1) * cb].astype(BF16), b_ref[b], NN)
            v, off = b // 2, nt * (b % 2)
            for j in range(nt):
                xr_ref[v, pl.ds(off + j, kc, stride=SUBLANES), :] = bu[:, j * LANES:(j + 1) * LANES]
                xi_ref[v, pl.ds(off + j, kc, stride=SUBLANES), :] = bu[:, half + j * LANES:half + (j + 1) * LANES]
        ars = [ar_ref[v] for v in range(npair)]
        ais = [ai_ref[v] for v in range(npair)]

        def step(k, carry):
            row = pl.ds(k * SUBLANES, SUBLANES)
            out = []
            for v in range(npair):
                xr, xi = carry[2 * v], carry[2 * v + 1]
                nr = ars[v] * xr - ais[v] * xi + xr_ref[v, row, :]
                ni = ars[v] * xi + ais[v] * xr + xi_ref[v, row, :]
                xr_ref[v, row, :] = nr
                xi_ref[v, row, :] = ni
                out += [nr, ni]
            return tuple(out)

        fin = tuple(ref[v] for v in range(npair) for ref in (sr_ref, si_ref))
        for k in range(kc):
            fin = step(k, fin)
        for v in range(npair):
            sr_ref[v] = fin[2 * v]
            si_ref[v] = fin[2 * v + 1]
        for b in range(nblk):
            v, off = b // 2, nt * (b % 2)
            xb = jnp.concatenate(_ssm_tiles(xr_ref, v, off, kc, nt) + _ssm_tiles(xi_ref, v, off, kc, nt), axis=1)
            cols = slice(b * cb, (b + 1) * cb)
            y_ref[:, cols] = _dot(xb.astype(BF16), c_ref[b], NN) + d_ref[:, cols] * uv[:, cols]

    whole = lambda a: pl.BlockSpec(a.shape, lambda c: (0,) * a.ndim)
    st = pl.BlockSpec((npair, kc * SUBLANES, LANES), lambda c: (0, c, 0))
    sd = jax.ShapeDtypeStruct
    return pl.pallas_call(
        body, name=name, grid=(nchunk,),
        in_specs=[pl.BlockSpec((kc, C), lambda c: (c, 0)), whole(bmat), whole(cmat), whole(art), whole(ait), whole(dvec)],
        out_specs=[pl.BlockSpec((kc, C), lambda c: (c, 0)), st, st],
        out_shape=[sd((T, C), F32), sd((npair, T * SUBLANES, LANES), F32), sd((npair, T * SUBLANES, LANES), F32)],
        scratch_shapes=[pltpu.VMEM((npair, SUBLANES, LANES), F32), pltpu.VMEM((npair, SUBLANES, LANES), F32)],
        compiler_params=_params("arbitrary"),
    )(uz, bmat, cmat, art, ait, dvec)


def _ssm_bwd(dy, uz, xr, xi, bmat, cmat, art, ait, dvec, dproj, name):
    T = uz.shape[0]
    nblk, cb, width = bmat.shape
    C = nblk * cb
    half = width // 2
    nt = half // LANES
    npair = nblk // 2
    kc = min(SSM_STEPS_BWD, T)
    nchunk = T // kc

    def body(dy_ref, u_ref, xr_ref, xi_ref, xpr_ref, xpi_ref, b_ref, c_ref, ar_ref, ai_ref, d_ref, _,
             du_ref, db_ref, dc_ref, dar_ref, dai_ref, dd_ref, gr_ref, gi_ref, sr_ref, si_ref):
        c = pl.program_id(0)

        @pl.when(c == 0)
        def _():
            for ref in (sr_ref, si_ref, db_ref, dc_ref, dar_ref, dai_ref, dd_ref):
                ref[...] = jnp.zeros_like(ref)

        dyv = dy_ref[...]
        uv = u_ref[...]
        for b in range(nblk):
            dx = _dot(dyv[:, b * cb:(b + 1) * cb].astype(BF16), c_ref[b], NT)
            v, off = b // 2, nt * (b % 2)
            for j in range(nt):
                gr_ref[v, pl.ds(off + j, kc, stride=SUBLANES), :] = dx[:, j * LANES:(j + 1) * LANES]
                gi_ref[v, pl.ds(off + j, kc, stride=SUBLANES), :] = dx[:, half + j * LANES:half + (j + 1) * LANES]
        ars = [ar_ref[v] for v in range(npair)]
        ais = [ai_ref[v] for v in range(npair)]

        def pair_update(v, gr, gi, row):
            nr = ars[v] * gr + ais[v] * gi + gr_ref[v, row, :]
            ni = ars[v] * gi - ais[v] * gr + gi_ref[v, row, :]
            gr_ref[v, row, :] = nr
            gi_ref[v, row, :] = ni
            return nr, ni

        def step(i, carry):
            k = kc - 1 - i
            row = pl.ds(k * SUBLANES, SUBLANES)
            prow = pl.ds((k - 1) * SUBLANES, SUBLANES)
            out = []
            for v in range(npair):
                gr, gi, sr, si = carry[4 * v:4 * v + 4]
                nr, ni = pair_update(v, gr, gi, row)
                pr, pi = xr_ref[v, prow, :], xi_ref[v, prow, :]
                out += [nr, ni, sr + pr * nr + pi * ni, si + pr * ni - pi * nr]
            return tuple(out)

        mid = tuple(ref[v] for v in range(npair) for ref in (sr_ref, si_ref, dar_ref, dai_ref))
        for i in range(kc - 1):
            mid = step(i, mid)
        live = (c < nchunk - 1).astype(F32)
        row0 = pl.ds(0, SUBLANES)
        for v in range(npair):
            gr, gi, sr, si = mid[4 * v:4 * v + 4]
            nr, ni = pair_update(v, gr, gi, row0)
            pr, pi = xpr_ref[v] * live, xpi_ref[v] * live
            sr_ref[v] = nr
            si_ref[v] = ni
            dar_ref[v] = sr + pr * nr + pi * ni
            dai_ref[v] = si + pr * ni - pi * nr
        for b in range(nblk):
            v, off = b // 2, nt * (b % 2)
            cols = slice(b * cb, (b + 1) * cb)
            gb = jnp.concatenate(_ssm_tiles(gr_ref, v, off, kc, nt) + _ssm_tiles(gi_ref, v, off, kc, nt), axis=1).astype(BF16)
            xb = jnp.concatenate(_ssm_tiles(xr_ref, v, off, kc, nt) + _ssm_tiles(xi_ref, v, off, kc, nt), axis=1).astype(BF16)
            du_ref[:, cols] = (_dot(gb, b_ref[b], NT) + dyv[:, cols] * d_ref[:, cols]).astype(BF16)
            db_ref[b] += _dot(uv[:, cols].astype(BF16), gb, TN)
            dc_ref[b] += _dot(dyv[:, cols].astype(BF16), xb, TN)
        dd_ref[...] += jnp.sum(dyv * uv, axis=0, keepdims=True)

    whole = lambda a: pl.BlockSpec(a.shape, lambda c: (0,) * a.ndim)
    rev = lambda c: (nchunk - 1 - c, 0)
    st = pl.BlockSpec((npair, kc * SUBLANES, LANES), lambda c: (0, nchunk - 1 - c, 0))
    stp = pl.BlockSpec((npair, SUBLANES, LANES), lambda c: (0, jnp.maximum((nchunk - 1 - c) * kc - 1, 0), 0))
    acc = lambda shape: pl.BlockSpec(shape, lambda c: (0,) * len(shape))
    sd = jax.ShapeDtypeStruct
    pair_shape = (npair, SUBLANES, LANES)
    return pl.pallas_call(
        body, name=name, grid=(nchunk,),
        in_specs=[pl.BlockSpec((kc, C), rev), pl.BlockSpec((kc, C), rev), st, st, stp, stp, whole(bmat), whole(cmat),
                  whole(art), whole(ait), whole(dvec), pl.BlockSpec(memory_space=pl.ANY)],
        out_specs=[pl.BlockSpec((kc, C), rev), acc(bmat.shape), acc(bmat.shape), acc(pair_shape), acc(pair_shape), acc((1, C))],
        out_shape=[sd(dproj.shape, BF16), sd(bmat.shape, F32), sd(bmat.shape, F32), sd(pair_shape, F32), sd(pair_shape, F32),
                   sd((1, C), F32)],
        scratch_shapes=[pltpu.VMEM((npair, kc * SUBLANES, LANES), F32), pltpu.VMEM((npair, kc * SUBLANES, LANES), F32),
                        pltpu.VMEM(pair_shape, F32), pltpu.VMEM(pair_shape, F32)],
        input_output_aliases={11: 0}, compiler_params=_params("arbitrary"),
    )(dy, uz, xr, xi, xr, xi, bmat, cmat, art, ait, dvec, dproj)


def _ssm_post(y, z, w_glu, b_glu, name):
    T, C = y.shape
    tm = _pick(T, 512, SUBLANES)

    def body(y_ref, z_ref, w_ref, b_ref, o_ref, a_ref):
        a, _ = _gelu_and_grad(y_ref[...])
        ab = a.astype(BF16)
        sg = _sigmoid(_dot(ab, w_ref[...], NN) + b_ref[...])
        sz, _ = _silu_and_grad(z_ref[...].astype(F32))
        o_ref[...] = (a * sg * sz).astype(BF16)
        a_ref[...] = ab

    row = pl.BlockSpec((tm, C), lambda i: (i, 0))
    return pl.pallas_call(
        body, name=name, grid=(T // tm,),
        in_specs=[row, row, pl.BlockSpec((C, C), lambda i: (0, 0)), pl.BlockSpec((1, C), lambda i: (0, 0))],
        out_specs=[row, row], out_shape=[jax.ShapeDtypeStruct((T, C), BF16)] * 2, compiler_params=_params("parallel"),
    )(y, z, w_glu, b_glu.reshape(1, C))


def _ssm_post_bwd(do, y, z, w_glu, b_glu, dproj, col, name):
    T, C = y.shape
    tm = _pick(T, 512, SUBLANES)

    def body(do_ref, y_ref, z_ref, w_ref, b_ref, _, dy_ref, dz_ref, ds_ref, db_ref):
        dov = do_ref[...]
        a, da_dy = _gelu_and_grad(y_ref[...])
        sg = _sigmoid(_dot(a.astype(BF16), w_ref[...], NN) + b_ref[...])
        sz, dsz = _silu_and_grad(z_ref[...].astype(F32))
        yg = a * sg
        dz_ref[...] = (dov * yg * dsz).astype(BF16)
        dyg = dov * sz
        ds = dyg * a * sg * (1.0 - sg)
        dsb = ds.astype(BF16)
        ds_ref[...] = dsb
        da = dyg * sg + _dot(dsb, w_ref[...], NT)
        dy_ref[...] = da * da_dy

        @pl.when(pl.program_id(0) == 0)
        def _():
            db_ref[...] = jnp.zeros_like(db_ref)

        db_ref[...] += jnp.sum(ds, axis=0, keepdims=True)

    row = pl.BlockSpec((tm, C), lambda i: (i, 0))
    vec = pl.BlockSpec((1, C), lambda i: (0, 0))
    sd = jax.ShapeDtypeStruct
    return pl.pallas_call(
        body, name=name, grid=(T // tm,),
        in_specs=[row, row, row, pl.BlockSpec((C, C), lambda i: (0, 0)), vec, pl.BlockSpec(memory_space=pl.ANY)],
        out_specs=[row, pl.BlockSpec((tm, C), lambda i: (i, col // C)), row, vec],
        out_shape=[sd((T, C), F32), sd(dproj.shape, BF16), sd((T, C), BF16), sd((1, C), F32)],
        input_output_aliases={5: 1}, compiler_params=_params("arbitrary"),
    )(do, y, z, w_glu, b_glu.reshape(1, C), dproj)


def _rel_bucket(dist):
    n = jnp.maximum(dist, 0)
    max_exact = NUM_BUCKETS // 2
    n_f = jnp.maximum(n, 1).astype(F32)
    large = max_exact + (jnp.log(n_f / max_exact) / math.log(REL_MAX_DISTANCE / max_exact)
                         * (NUM_BUCKETS - max_exact)).astype(jnp.int32)
    large = jnp.minimum(large, NUM_BUCKETS - 1)
    return jnp.where(n < max_exact, n, large)


def _band_tables():
    qi = jnp.arange(ATTN_BLOCK)[:, None]
    kj = jnp.arange(2 * ATTN_BLOCK)[None, :]
    delta = ATTN_BLOCK + qi - kj
    buckets, bands = [], []
    for window, dilation in ATTN_CONFIGS:
        bands.append((delta >= 0) & (delta <= window // dilation))
        buckets.append(_rel_bucket(jnp.maximum(delta, 0) * dilation))
    return jnp.stack(buckets), jnp.stack(bands)


ATTN_UNITS = 4


def _attn_units_for(g, forward):
    if ATTN_CONFIGS[g][1] >= 16:
        return ATTN_UNITS
    return 4 * ATTN_UNITS if forward else 2 * ATTN_UNITS


def _attn_tile(T, r, nu):
    nq = max(1, nu // r)
    rows = ATTN_BLOCK * r * nq
    return nq, rows, T // rows


def _attn_units(r, nq, chunk, nu):
    if r >= nu:
        return [(chunk * nu + i, None) for i in range(nu)]
    units = []
    for j in range(nq):
        for s in range(r):
            units.append((ATTN_BLOCK * j * r + s, ATTN_BLOCK * (j - 1) * r + s if j else None))
    return units


def _rows(start, r):
    return pl.ds(start, ATTN_BLOCK, stride=r) if r > 1 else pl.ds(start, ATTN_BLOCK)


def _attn_group_fwd(qkv, biasm, g, name, nu=ATTN_UNITS):
    T = qkv.shape[0]
    r = ATTN_CONFIGS[g][1]
    B, hd = ATTN_BLOCK, ATTN_HEAD_DIM
    nq, rows, ntiles = _attn_tile(T, r, nu)
    nchunks = max(1, r // nu)
    last_prev = B * (nq - 1) * r
    scale = hd ** -0.5
    tiles_per_tensor = 3 * HEADS_PER_GROUP * hd // LANES

    def body(q_ref, kc_ref, kp_ref, vc_ref, vp_ref, bias_ref, o_ref, lse_ref, s_ref, p_ref):
        n = pl.program_id(1)
        lane = lax.broadcasted_iota(jnp.int32, (1, LANES), 1)
        col = lax.broadcasted_iota(jnp.int32, (1, 2 * B), 1)
        masks = [lane < hd, lane >= hd]
        first_pen = jnp.where((col < B) & (n == 0), NEG_INF, 0.0)

        def chunk_body(chunk):
            units = _attn_units(r, nq, chunk, nu)

            def keys(cur_ref, prev_ref, cs, ps):
                prev = prev_ref[_rows(last_prev + (cs if r >= nu else cs % r), r), :] if ps is None else cur_ref[_rows(ps, r), :]
                return jnp.concatenate([prev, cur_ref[_rows(cs, r), :]], axis=0).astype(BF16)

            for u, (cs, ps) in enumerate(units):
                qv = q_ref[_rows(cs, r), :]
                kw = keys(kc_ref, kp_ref, cs, ps)
                for hh in range(2):
                    s_ref[2 * u + hh] = _dot(jnp.where(masks[hh], qv, 0.0).astype(BF16), kw, NT)
            for u, (cs, ps) in enumerate(units):
                lses = []
                for hh in range(2):
                    s = s_ref[2 * u + hh] * scale + bias_ref[hh]
                    if ps is None:
                        s = s + first_pen
                    m = jnp.max(s, axis=-1, keepdims=True)
                    p = jnp.exp(s - m)
                    l = jnp.sum(p, axis=-1, keepdims=True)
                    p_ref[2 * u + hh] = (p * (1.0 / l)).astype(BF16)
                    lses.append(m + jnp.log(l))
                lse_ref[_rows(cs, r), :] = jnp.where(masks[0], lses[0], lses[1])
            for u, (cs, ps) in enumerate(units):
                vw = keys(vc_ref, vp_ref, cs, ps)
                o_ref[_rows(cs, r), :] = (_dot(p_ref[2 * u], jnp.where(masks[0], vw, 0), NN)
                                          + _dot(p_ref[2 * u + 1], jnp.where(masks[1], vw, 0), NN))

        for chunk in range(nchunks):
            chunk_body(chunk)

    def cur(t):
        return pl.BlockSpec((rows, LANES), lambda hf, n: (n, t * tiles_per_tensor + 2 * g + hf))

    def prev(t):
        return pl.BlockSpec((rows, LANES), lambda hf, n: (jnp.maximum(n - 1, 0), t * tiles_per_tensor + 2 * g + hf))

    out = pl.BlockSpec((rows, LANES), lambda hf, n: (n, hf))
    sd = jax.ShapeDtypeStruct((T, 2 * LANES), F32)
    return pl.pallas_call(
        body, name=name, grid=(2, ntiles),
        in_specs=[cur(0), cur(1), prev(1), cur(2), prev(2), pl.BlockSpec((None, 2, B, 2 * B), lambda hf, n: (g, hf, 0, 0))],
        out_specs=[out, out], out_shape=[sd, sd],
        scratch_shapes=[pltpu.VMEM((2 * nu, B, 2 * B), F32), pltpu.VMEM((2 * nu, B, 2 * B), BF16)],
        compiler_params=_params("parallel", "parallel"),
    )(qkv, qkv, qkv, qkv, qkv, biasm)


def _attn_group_bwd(qkv, do, dvec, lse, biasm, g, dproj, dk_col, name, nu=ATTN_UNITS):
    T = qkv.shape[0]
    r = ATTN_CONFIGS[g][1]
    B, hd = ATTN_BLOCK, ATTN_HEAD_DIM
    nq, rows, ntiles = _attn_tile(T, r, nu)
    nchunks = max(1, r // nu)
    last_prev = B * (nq - 1) * r
    scale = hd ** -0.5
    tiles_per_tensor = 3 * HEADS_PER_GROUP * hd // LANES
    stash = r > SUBLANES

    def body(q_ref, kc_ref, kp_ref, vc_ref, vp_ref, do_ref, dv_ref, lse_ref, bias_ref, _,
             dq_ref, dk_ref, dvo_ref, dbias_ref, ck_ref, cv_ref, ak_ref, av_ref, s_ref, dp_ref, p_ref, ds_ref,
             qd_ref, dod_ref, kwd_ref):
        n = pl.program_id(1)
        lane = lax.broadcasted_iota(jnp.int32, (1, LANES), 1)
        col = lax.broadcasted_iota(jnp.int32, (1, 2 * B), 1)
        masks = [lane < hd, lane >= hd]
        first_pen = jnp.where((col < B) & (n == 0), NEG_INF, 0.0)

        @pl.when(n == 0)
        def _():
            dbias_ref[...] = jnp.zeros_like(dbias_ref)
            ck_ref[...] = jnp.zeros_like(ck_ref)
            cv_ref[...] = jnp.zeros_like(cv_ref)

        def chunk_body(chunk):
            units = _attn_units(r, nq, chunk, nu)

            def prev_rows(cs):
                return _rows(last_prev + (cs if r >= nu else cs % r), r)

            def keys(cur_ref, prev_ref, cs, ps):
                prev = prev_ref[prev_rows(cs), :] if ps is None else cur_ref[_rows(ps, r), :]
                return jnp.concatenate([prev, cur_ref[_rows(cs, r), :]], axis=0).astype(BF16)

            for u, (cs, ps) in enumerate(units):
                qv = q_ref[_rows(cs, r), :]
                dov = do_ref[_rows(cs, r), :]
                kw = keys(kc_ref, kp_ref, cs, ps)
                vw = keys(vc_ref, vp_ref, cs, ps)
                if stash:
                    qd_ref[u] = qv.astype(BF16)
                    dod_ref[u] = dov.astype(BF16)
                    kwd_ref[u] = kw
                for hh in range(2):
                    s_ref[2 * u + hh] = _dot(jnp.where(masks[hh], qv, 0.0).astype(BF16), kw, NT)
                    dp_ref[2 * u + hh] = _dot(jnp.where(masks[hh], dov, 0.0).astype(BF16), vw, NT)
            for u, (cs, ps) in enumerate(units):
                lse_t = lse_ref[_rows(cs, r), :]
                dv_t = dv_ref[_rows(cs, r), :]
                for hh in range(2):
                    lo = hh * hd
                    s = s_ref[2 * u + hh] * scale + bias_ref[hh]
                    if ps is None:
                        s = s + first_pen
                    p = jnp.exp(s - lse_t[:, lo:lo + 1])
                    ds = p * (dp_ref[2 * u + hh] + dv_t[:, lo:lo + 1])
                    dbias_ref[hh] += ds
                    p_ref[2 * u + hh] = p.astype(BF16)
                    ds_ref[2 * u + hh] = ds.astype(BF16)
            for u, (cs, ps) in enumerate(units):
                if stash:
                    qv, dov, kw = qd_ref[u], dod_ref[u], kwd_ref[u]
                else:
                    qv = q_ref[_rows(cs, r), :].astype(BF16)
                    dov = do_ref[_rows(cs, r), :].astype(BF16)
                    kw = keys(kc_ref, kp_ref, cs, ps)
                dq, dkw, dvw = 0.0, 0.0, 0.0
                for hh in range(2):
                    dsb = ds_ref[2 * u + hh]
                    dq = dq + _dot(dsb, jnp.where(masks[hh], kw, 0), NN)
                    dkw = dkw + _dot(dsb, jnp.where(masks[hh], qv, 0), TN)
                    dvw = dvw + _dot(p_ref[2 * u + hh], jnp.where(masks[hh], dov, 0), TN)
                dq_ref[_rows(cs, r), :] = dq * scale
                ak_ref[_rows(cs, r), :] = dkw[B:] * scale
                av_ref[_rows(cs, r), :] = dvw[B:]
                if ps is None:
                    ck_ref[prev_rows(cs), :] += dkw[:B] * scale
                    cv_ref[prev_rows(cs), :] += dvw[:B]
                else:
                    ak_ref[_rows(ps, r), :] += dkw[:B] * scale
                    av_ref[_rows(ps, r), :] += dvw[:B]

        @pl.when(n < ntiles)
        def _():
            for chunk in range(nchunks):
                chunk_body(chunk)

        dk_ref[...] = ck_ref[...].astype(BF16)
        dvo_ref[...] = cv_ref[...].astype(BF16)
        ck_ref[...] = ak_ref[...]
        cv_ref[...] = av_ref[...]

    last = ntiles - 1

    def cur(t):
        return pl.BlockSpec((rows, LANES), lambda hf, n: (jnp.minimum(n, last), t * tiles_per_tensor + 2 * g + hf))

    def prev(t):
        return pl.BlockSpec((rows, LANES), lambda hf, n: (jnp.clip(n - 1, 0, last), t * tiles_per_tensor + 2 * g + hf))

    nat = pl.BlockSpec((rows, LANES), lambda hf, n: (jnp.minimum(n, last), hf))
    nat_prev = pl.BlockSpec((rows, LANES), lambda hf, n: (jnp.clip(n - 1, 0, last), hf))
    tab = pl.BlockSpec((None, 2, B, 2 * B), lambda hf, n: (g, hf, 0, 0))
    dtab = pl.BlockSpec((2, B, 2 * B), lambda hf, n: (hf, 0, 0))
    sd = jax.ShapeDtypeStruct
    vm = pltpu.VMEM
    dk_tile = dk_col // LANES + 2 * g
    dk_spec = pl.BlockSpec((rows, LANES), lambda hf, n: (jnp.clip(n - 1, 0, last), dk_tile + hf))
    return pl.pallas_call(
        body, name=name, grid=(2, ntiles + 1),
        in_specs=[cur(0), cur(1), prev(1), cur(2), prev(2), nat, nat, nat, tab, pl.BlockSpec(memory_space=pl.ANY)],
        out_specs=[nat, dk_spec, nat_prev, dtab],
        out_shape=[sd((T, 2 * LANES), F32), sd(dproj.shape, BF16), sd((T, 2 * LANES), BF16),
                   sd((HEADS_PER_GROUP, B, 2 * B), F32)],
        scratch_shapes=[vm((rows, LANES), F32), vm((rows, LANES), F32), vm((rows, LANES), F32), vm((rows, LANES), F32),
                        vm((2 * nu, B, 2 * B), F32), vm((2 * nu, B, 2 * B), F32),
                        vm((2 * nu, B, 2 * B), BF16), vm((2 * nu, B, 2 * B), BF16),
                        vm((nu, B, LANES), BF16), vm((nu, B, LANES), BF16), vm((nu, 2 * B, LANES), BF16)],
        input_output_aliases={9: 1}, compiler_params=_params("parallel", "arbitrary"),
    )(qkv, qkv, qkv, qkv, qkv, do, dvec, lse, biasm, dproj)


def _attn_mix(os, lses, z, name):
    T, gw = os[0].shape
    C = z.shape[1]
    tm = _pick(T, 512, SUBLANES)

    def body(o0_ref, o1_ref, o2_ref, l0_ref, l1_ref, l2_ref, z_ref, out_ref):
        ls = [l0_ref[...], l1_ref[...], l2_ref[...]]
        mx = jnp.maximum(jnp.maximum(ls[0], ls[1]), ls[2])
        es = [jnp.exp(l - mx) for l in ls]
        inv = 1.0 / (es[0] + es[1] + es[2])
        for i, o_ref in enumerate((o0_ref, o1_ref, o2_ref)):
            sz, _ = _silu_and_grad(z_ref[:, i * gw:(i + 1) * gw].astype(F32))
            out_ref[:, i * gw:(i + 1) * gw] = (o_ref[...] * (es[i] * inv) * sz).astype(BF16)

    row = pl.BlockSpec((tm, C), lambda i: (i, 0))
    grp = pl.BlockSpec((tm, gw), lambda i: (i, 0))
    return pl.pallas_call(
        body, name=name, grid=(T // tm,), in_specs=[grp] * 6 + [row], out_specs=row,
        out_shape=jax.ShapeDtypeStruct((T, C), BF16), compiler_params=_params("parallel"),
    )(*os, *lses, z)


def _attn_mix_bwd(dout, os, lses, z, dproj, col, name):
    T, gw = os[0].shape
    C = z.shape[1]
    tm = _pick(T, 512, SUBLANES)
    head_of = np.arange(gw) // ATTN_HEAD_DIM
    ones = jnp.asarray(head_of[:, None] == head_of[None, :], BF16)

    def body(dout_ref, o0_ref, o1_ref, o2_ref, l0_ref, l1_ref, l2_ref, z_ref, ones_ref, _,
             dz_ref, do0_ref, do1_ref, do2_ref, dv0_ref, dv1_ref, dv2_ref):
        ls = [l0_ref[...], l1_ref[...], l2_ref[...]]
        mx = jnp.maximum(jnp.maximum(ls[0], ls[1]), ls[2])
        es = [jnp.exp(l - mx) for l in ls]
        inv = 1.0 / (es[0] + es[1] + es[2])
        alphas, ebar = [], 0.0
        for i, (o_ref, do_ref) in enumerate(((o0_ref, do0_ref), (o1_ref, do1_ref), (o2_ref, do2_ref))):
            sl = slice(i * gw, (i + 1) * gw)
            alpha = es[i] * inv
            ov = o_ref[...]
            dv = dout_ref[:, sl]
            sz, dsz = _silu_and_grad(z_ref[:, sl].astype(F32))
            dz_ref[:, sl] = (dv * ov * alpha * dsz).astype(BF16)
            da = dv * sz
            do_ref[...] = da * alpha
            t = da * ov
            t1 = t.astype(BF16)
            r1 = t - t1.astype(F32)
            t2 = r1.astype(BF16)
            t3 = (r1 - t2.astype(F32)).astype(BF16)
            e = _dot(t1, ones_ref[...], NN) + _dot(t2, ones_ref[...], NN) + _dot(t3, ones_ref[...], NN)
            ebar = ebar + alpha * e
            alphas.append(alpha)
        for alpha, dv_ref in zip(alphas, (dv0_ref, dv1_ref, dv2_ref)):
            dv_ref[...] = -alpha * ebar

    row = pl.BlockSpec((tm, C), lambda i: (i, 0))
    grp = pl.BlockSpec((tm, gw), lambda i: (i, 0))
    sd = jax.ShapeDtypeStruct
    res = pl.pallas_call(
        body, name=name, grid=(T // tm,),
        in_specs=[row] + [grp] * 6 + [row, pl.BlockSpec((gw, gw), lambda i: (0, 0)), pl.BlockSpec(memory_space=pl.ANY)],
        out_specs=[pl.BlockSpec((tm, C), lambda i: (i, col // C))] + [grp] * 6,
        out_shape=[sd(dproj.shape, BF16)] + [sd((T, gw), F32)] * 6, input_output_aliases={9: 0},
        compiler_params=_params("parallel"),
    )(dout, *os, *lses, z, ones, dproj)
    return res[0], res[1:4], res[4:7]


def _mem_attn(qz, kv, name):
    T = qz.shape[0]
    dm = qz.shape[1] // 2
    M = kv.shape[0]
    hd = dm // MEM_HEADS
    scale = hd ** -0.5
    tm = _pick(T, 512, SUBLANES)

    def body(q_ref, z_ref, k_ref, v_ref, o_ref, s_ref, p_ref):
        heads = [slice(h * hd, (h + 1) * hd) for h in range(MEM_HEADS)]
        for h, sl in enumerate(heads):
            s_ref[h] = _dot(q_ref[:, sl].astype(BF16), k_ref[:, sl], NT)
        for h, sl in enumerate(heads):
            s = s_ref[h] * scale
            p = jnp.exp(s - jnp.max(s, axis=-1, keepdims=True))
            p_ref[h] = (p * (1.0 / jnp.sum(p, axis=-1, keepdims=True))).astype(BF16)
        for h, sl in enumerate(heads):
            sz, _ = _silu_and_grad(z_ref[:, sl].astype(F32))
            o_ref[:, sl] = (_dot(p_ref[h], v_ref[:, sl], NN) * sz).astype(BF16)

    return pl.pallas_call(
        body, name=name, grid=(T // tm,),
        in_specs=[pl.BlockSpec((tm, dm), lambda i: (i, 0)), pl.BlockSpec((tm, dm), lambda i: (i, 1)),
                  pl.BlockSpec((M, dm), lambda i: (0, 0)), pl.BlockSpec((M, dm), lambda i: (0, 1))],
        out_specs=pl.BlockSpec((tm, dm), lambda i: (i, 0)),
        out_shape=jax.ShapeDtypeStruct((T, dm), BF16),
        scratch_shapes=[pltpu.VMEM((MEM_HEADS, tm, M), F32), pltpu.VMEM((MEM_HEADS, tm, M), BF16)],
        compiler_params=_params("parallel"),
    )(qz, qz, kv, kv)


def _mem_attn_bwd(do, qz, kv, dproj, col, name):
    T = qz.shape[0]
    dm = qz.shape[1] // 2
    M = kv.shape[0]
    hd = dm // MEM_HEADS
    scale = hd ** -0.5
    tm = _pick(T, 512, SUBLANES)

    def body(do_ref, q_ref, z_ref, k_ref, v_ref, _, dq_ref, dz_ref, dk_ref, dv_ref, s_ref, dp_ref, p_ref, ds_ref, dob_ref):
        @pl.when(pl.program_id(0) == 0)
        def _():
            dk_ref[...] = jnp.zeros_like(dk_ref)
            dv_ref[...] = jnp.zeros_like(dv_ref)

        heads = [slice(h * hd, (h + 1) * hd) for h in range(MEM_HEADS)]
        for h, sl in enumerate(heads):
            sz, _ = _silu_and_grad(z_ref[:, sl].astype(F32))
            dob = (do_ref[:, sl] * sz).astype(BF16)
            dob_ref[:, sl] = dob
            s_ref[h] = _dot(q_ref[:, sl].astype(BF16), k_ref[:, sl], NT)
            dp_ref[h] = _dot(dob, v_ref[:, sl], NT)
        for h, sl in enumerate(heads):
            s = s_ref[h] * scale
            p = jnp.exp(s - jnp.max(s, axis=-1, keepdims=True))
            pn = p * (1.0 / jnp.sum(p, axis=-1, keepdims=True))
            dp = dp_ref[h]
            p_ref[h] = pn.astype(BF16)
            ds_ref[h] = (pn * (dp - jnp.sum(dp * pn, axis=-1, keepdims=True))).astype(BF16)
        for h, sl in enumerate(heads):
            _, dsz = _silu_and_grad(z_ref[:, sl].astype(F32))
            dz_ref[:, sl] = (do_ref[:, sl] * _dot(p_ref[h], v_ref[:, sl], NN) * dsz).astype(BF16)
            dq_ref[:, sl] = (_dot(ds_ref[h], k_ref[:, sl], NN) * scale).astype(BF16)
            dk_ref[:, sl] += _dot(ds_ref[h], q_ref[:, sl].astype(BF16), TN) * scale
            dv_ref[:, sl] += _dot(p_ref[h], dob_ref[:, sl], TN)

    rowq = pl.BlockSpec((tm, dm), lambda i: (i, 0))
    rowz = pl.BlockSpec((tm, dm), lambda i: (i, 1))
    kb = pl.BlockSpec((M, dm), lambda i: (0, 0))
    vb = pl.BlockSpec((M, dm), lambda i: (0, 1))
    sd = jax.ShapeDtypeStruct
    dq, dz, dk, dv = pl.pallas_call(
        body, name=name, grid=(T // tm,), in_specs=[rowq, rowq, rowz, kb, vb, pl.BlockSpec(memory_space=pl.ANY)],
        out_specs=[pl.BlockSpec((tm, dm), lambda i: (i, col // dm)), rowq, kb, kb],
        out_shape=[sd(dproj.shape, BF16), sd((T, dm), BF16), sd((M, dm), F32), sd((M, dm), F32)],
        scratch_shapes=[pltpu.VMEM((MEM_HEADS, tm, M), F32), pltpu.VMEM((MEM_HEADS, tm, M), F32),
                        pltpu.VMEM((MEM_HEADS, tm, M), BF16), pltpu.VMEM((MEM_HEADS, tm, M), BF16), pltpu.VMEM((tm, dm), BF16)],
        input_output_aliases={5: 0}, compiler_params=_params("arbitrary"),
    )(do, qz, qz, kv, kv, dproj)
    return dq, dz, dk, dv


def _merge(os, ws, L, logits, b_gate, name):
    T = os[0].shape[0]
    D = ws[0].shape[2]
    tm = _pick(T, 512, SUBLANES)

    def body(o0_ref, o1_ref, o2_ref, w0_ref, w1_ref, w2_ref, l_ref, b_ref, m_ref, p_ref):
        acc = 0.0
        for i, (o_ref, w_ref) in enumerate(((o0_ref, w0_ref), (o1_ref, w1_ref), (o2_ref, w2_ref))):
            sl = slice(i * D, (i + 1) * D)
            bp = _dot(o_ref[...], w_ref[...], NN)
            p_ref[i] = bp.astype(BF16)
            acc = acc + _sigmoid(l_ref[:, sl].astype(F32) + b_ref[:, sl]) * bp
        m_ref[...] = acc.astype(BF16)

    return pl.pallas_call(
        body, name=name, grid=(T // tm,),
        in_specs=[pl.BlockSpec((tm, o.shape[1]), lambda i: (i, 0)) for o in os]
        + [pl.BlockSpec((None,) + w.shape[1:], lambda i: (L, 0, 0)) for w in ws]
        + [pl.BlockSpec((tm, 3 * D), lambda i: (i, 0)), pl.BlockSpec((1, 3 * D), lambda i: (0, 0))],
        out_specs=[pl.BlockSpec((tm, D), lambda i: (i, 0)), pl.BlockSpec((3, tm, D), lambda i: (0, i, 0))],
        out_shape=[jax.ShapeDtypeStruct((T, D), BF16), jax.ShapeDtypeStruct((3, T, D), BF16)],
        compiler_params=_params("parallel"),
    )(*os, *ws, logits, b_gate.reshape(1, 3 * D))


def _merge_bwd(dmerged, bps, logits, b_gate, dproj_cols, dl_off, name):
    _, T, D = bps.shape
    tm = _pick(T, 2048, SUBLANES)
    cw = _pick(math.gcd(dl_off, D), 512, LANES)
    per = D // cw

    def body(dm_ref, p_ref, l_ref, b_ref, dl_ref, d_ref, db_ref):
        @pl.when(pl.program_id(1) == 0)
        def _():
            db_ref[...] = jnp.zeros_like(db_ref)

        dmv = dm_ref[...]
        gt = _sigmoid(l_ref[...].astype(F32) + b_ref[...])
        d_ref[...] = (dmv * gt).astype(BF16)
        dl = dmv * p_ref[...].astype(F32) * gt * (1.0 - gt)
        dl_ref[...] = dl.astype(BF16)
        db_ref[...] += jnp.sum(dl, axis=0, keepdims=True)

    stacked = pl.BlockSpec((None, tm, cw), lambda j, i: (j // per, i, j % per))
    sd = jax.ShapeDtypeStruct
    return pl.pallas_call(
        body, name=name, grid=(3 * per, T // tm),
        in_specs=[pl.BlockSpec((tm, cw), lambda j, i: (i, j % per)), stacked, pl.BlockSpec((tm, cw), lambda j, i: (i, j)),
                  pl.BlockSpec((1, cw), lambda j, i: (0, j))],
        out_specs=[pl.BlockSpec((tm, cw), lambda j, i: (i, dl_off // cw + j)), stacked, pl.BlockSpec((1, cw), lambda j, i: (0, j))],
        out_shape=[sd((T, dproj_cols), BF16), sd((3, T, D), BF16), sd((1, 3 * D), F32)],
        compiler_params=_params("parallel", "arbitrary"),
    )(dmerged, bps, logits, b_gate.reshape(1, 3 * D))


def _block_diag(w):
    nblk, ng, a, b = w.shape
    eye = jnp.eye(ng, dtype=w.dtype)
    return (w[:, :, :, None, :] * eye[None, :, None, :, None]).reshape(nblk, ng * a, ng * b)


def _block_diag_part(m, a, b):
    nblk = m.shape[0]
    ng = m.shape[1] // a
    m5 = m.reshape(nblk, ng, a, ng, b)
    eye = jnp.eye(ng, dtype=m.dtype)
    return jnp.sum(m5 * eye[None, :, None, :, None], axis=3)


def _ssm_matrices(p, L, tag):
    G, P = p["ssm_lambda_re"].shape[1:]
    Hg = SSM_GROUP
    gpb = SSM_BLOCK_CH // Hg
    nblk = G // gpb
    br = p["ssm_b_re"][L].transpose(2, 0, 1)
    bi = p["ssm_b_im"][L].transpose(2, 0, 1)
    disc_in = (p["ssm_lambda_re"][L], p["ssm_lambda_im"][L], p["ssm_log_dt"][L].reshape(G, 1), br, bi)
    ar, ai, bbr, bbi = _ssm_disc(*disc_in, name=f"ssm_disc_{tag}")
    amat = (ar.reshape(nblk // 2, SUBLANES, LANES), ai.reshape(nblk // 2, SUBLANES, LANES))
    bbr_g = bbr.transpose(1, 0, 2).reshape(nblk, gpb, Hg, P)
    bbi_g = bbi.transpose(1, 0, 2).reshape(nblk, gpb, Hg, P)
    bmat = jnp.concatenate([_block_diag(bbr_g), _block_diag(bbi_g)], axis=2).astype(BF16)
    cre = p["ssm_c_re"][L].reshape(nblk, gpb, Hg, P).transpose(0, 1, 3, 2)
    cim = p["ssm_c_im"][L].reshape(nblk, gpb, Hg, P).transpose(0, 1, 3, 2)
    cmat = jnp.concatenate([_block_diag(cre), -_block_diag(cim)], axis=1).astype(BF16)
    return disc_in, amat, bmat, cmat


def _layer_fwd(x, mem, p, wb, L, biasm):
    T, D = x.shape
    C = p["ssm_d"].shape[1]
    dm = wb["w_br_mem"].shape[1]
    tag = f"l{L}"
    s = {"x": x}
    h = _rmsnorm(x, p["norm_g"][L], f"norm_{tag}")
    offs = [int(o) for o in np.cumsum([0, C, C, 3 * 768, 768, 2 * dm, 3 * D])]
    names = ("uz", "z_ssm", "qkv", "z_attn", "qz_mem", "logits")
    dts = (F32, BF16, F32, BF16, BF16, BF16)
    for i, (nm, dt) in enumerate(zip(names, dts)):
        wide_bf16 = offs[i + 1] - offs[i] >= 1024 and dt == BF16
        tiles = dict(tm=2048) if wide_bf16 else {}
        s[nm] = _matmul(h, wb["w_in"], mode="nn", name=f"in_{nm}_{tag}", out_dtype=dt, b_lead=L, b_off=offs[i],
                        n_cols=offs[i + 1] - offs[i], **tiles)
    s["h"] = h

    disc_in, amat, bmat, cmat = _ssm_matrices(p, L, tag)
    dvec = p["ssm_d"][L].reshape(1, C)
    y, xr, xi = _ssm_fwd(s["uz"], bmat, cmat, *amat, dvec, f"ssm_scan_{tag}")
    o_ssm, a_glu = _ssm_post(y, s["z_ssm"], wb["w_glu"][L], p["b_glu"][L], f"ssm_post_{tag}")
    s.update(disc_in=disc_in, amat=amat, bmat=bmat, cmat=cmat, xr=xr, xi=xi, y=y, a_glu=a_glu, o_ssm=o_ssm)

    groups = [_attn_group_fwd(s["qkv"], biasm, g, f"attn_g{g}_{tag}", nu=_attn_units_for(g, True))
              for g in range(len(ATTN_CONFIGS))]
    os, lses = [o for o, _ in groups], [l for _, l in groups]
    o_attn = _attn_mix(os, lses, s["z_attn"], f"attn_mix_{tag}")
    s.update(os=os, lses=lses, o_attn=o_attn)

    mn = _rmsnorm(mem, p["mem_norm_g"][L], f"mem_norm_{tag}")
    kv = _matmul(mn, wb["w_mem_kv"], mode="nn", name=f"mem_kv_{tag}", out_dtype=BF16, b_lead=L)
    o_mem = _mem_attn(s["qz_mem"], kv, f"mem_attn_{tag}")
    s.update(mn=mn, kv=kv, o_mem=o_mem)

    merged, bps = _merge([o_ssm, o_attn, o_mem], [wb["w_br_ssm"], wb["w_br_attn"], wb["w_br_mem"]], L, s["logits"],
                         p["b_gate"][L], f"merge_{tag}")
    s.update(bps=bps, merged=merged)
    x_new = _matmul(merged, wb["w_out"], mode="nn", name=f"out_{tag}", add=x, b_lead=L)
    return x_new, s


def _layer_bwd(dx, mem, p, wb, L, s, biasm, gprev):
    T, D = dx.shape
    C = p["ssm_d"].shape[1]
    depth = p["norm_g"].shape[0]
    tag = f"l{L}"
    g = {}

    def wgrad(n, a, b, **tiles):
        g[n] = _matmul(a, b, mode="tn", name=f"d{n}_{tag}", out_dtype=BF16, stack=(L, depth, gprev.get(n)), **tiles)

    dmerged = _matmul(dx, wb["w_out"], mode="nt", name=f"d_merged_{tag}", b_lead=L)
    wgrad("w_out", s["merged"], dx)
    dm = s["qz_mem"].shape[1] // 2
    col = dict(zip(("u", "z_ssm", "q", "k", "v", "z_attn", "q_mem", "z_mem", "logits", "end"),
                   (int(o) for o in np.cumsum([0, C, C, 768, 768, 768, 768, dm, dm, 3 * D]))))
    dproj, dbps, g["b_gate"] = _merge_bwd(dmerged, s["bps"], s["logits"], p["b_gate"][L], col["end"], col["logits"],
                                          f"merge_bwd_{tag}")
    dos = []
    for i, (o, n) in enumerate(((s["o_ssm"], "w_br_ssm"), (s["o_attn"], "w_br_attn"), (s["o_mem"], "w_br_mem"))):
        dos.append(_matmul(dbps, wb[n], mode="nt", name=f"d_o_{n}_{tag}", a_lead=i, b_lead=L))
        g[n] = _matmul(o, dbps, mode="tn", name=f"d{n}_{tag}", out_dtype=BF16, b_lead=i, stack=(L, depth, gprev.get(n)))

    dy, dproj, ds_glu, g["b_glu"] = _ssm_post_bwd(dos[0], s["y"], s["z_ssm"], wb["w_glu"][L], p["b_glu"][L], dproj,
                                                  col["z_ssm"], f"ssm_post_bwd_{tag}")
    wgrad("w_glu", s["a_glu"], ds_glu)
    dvec = p["ssm_d"][L].reshape(1, C)
    dproj, dbm, dct, dar, dai, g["ssm_d"] = _ssm_bwd(dy, s["uz"], s["xr"], s["xi"], s["bmat"], s["cmat"], *s["amat"], dvec,
                                                     dproj, f"ssm_scan_bwd_{tag}")
    G, P = p["ssm_lambda_re"].shape[1:]
    Hg = SSM_GROUP
    half = dbm.shape[2] // 2
    dbbr = _block_diag_part(dbm[:, :, :half], Hg, P).reshape(G, Hg, P).transpose(1, 0, 2)
    dbbi = _block_diag_part(dbm[:, :, half:], Hg, P).reshape(G, Hg, P).transpose(1, 0, 2)
    g["ssm_c_re"] = _block_diag_part(dct[:, :, :half], Hg, P).reshape(G, Hg, P)
    g["ssm_c_im"] = -_block_diag_part(dct[:, :, half:], Hg, P).reshape(G, Hg, P)
    glre, glim, gdt, gbr, gbi = _ssm_disc_bwd(*s["disc_in"], dar.reshape(G, P), dai.reshape(G, P), dbbr, dbbi,
                                              name=f"ssm_disc_bwd_{tag}")
    g["ssm_lambda_re"], g["ssm_lambda_im"], g["ssm_log_dt"] = glre, glim, gdt.reshape(G)
    g["ssm_b_re"] = gbr.transpose(1, 2, 0)
    g["ssm_b_im"] = gbi.transpose(1, 2, 0)

    dproj, do_g, dvec_g = _attn_mix_bwd(dos[1], s["os"], s["lses"], s["z_attn"], dproj, col["z_attn"], f"attn_mix_bwd_{tag}")
    rest, dbias = [], []
    for gi in range(len(ATTN_CONFIGS)):
        dq_g, dproj, dv_g, db_g = _attn_group_bwd(s["qkv"], do_g[gi], dvec_g[gi], s["lses"][gi], biasm, gi, dproj, col["k"],
                                                  f"attn_bwd_g{gi}_{tag}", nu=_attn_units_for(gi, False))
        gw = dq_g.shape[1]
        rest += [(dq_g, col["q"] + gi * gw), (dv_g, col["v"] + gi * gw)]
        dbias.append(db_g)
    dbias = jnp.stack(dbias)

    dproj, dz_mem, dk_mem, dv_mem = _mem_attn_bwd(dos[2], s["qz_mem"], s["kv"], dproj, col["q_mem"], f"mem_attn_bwd_{tag}")
    rest.append((dz_mem, col["z_mem"]))
    for piece, at in rest:
        dproj = lax.dynamic_update_slice(dproj, piece.astype(BF16), (0, at))
    dkv = jnp.concatenate([dk_mem, dv_mem], axis=1)
    wgrad("w_mem_kv", s["mn"], dkv)
    dmn = _matmul(dkv, wb["w_mem_kv"], mode="nt", name=f"d_mn_{tag}", b_lead=L)
    _, g["mem_norm_g"] = _rmsnorm_bwd(mem, p["mem_norm_g"][L], dmn, None, f"mem_norm_bwd_{tag}")

    dh = _matmul(dproj, wb["w_in"], mode="nt", name=f"d_h_{tag}", b_lead=L, tm=512, tn=1024)
    wgrad("w_in", s["h"], dproj, tn=2304, tk=1024)
    dx_in, g["norm_g"] = _rmsnorm_bwd(s["x"], p["norm_g"][L], dh, dx, f"norm_bwd_{tag}")
    return dx_in, g, dbias


def _bucket_onehot(gi):
    buckets, bands = _band_tables()
    hit = (buckets[gi].reshape(1, -1) == jnp.arange(NUM_BUCKETS)[:, None]) & bands[gi].reshape(1, -1)
    return hit.astype(BF16)


def _bias_tables(rel_bias, name):
    _, bands = _band_tables()
    out = []
    for gi in range(len(ATTN_CONFIGS)):
        tab = rel_bias[:, gi * HEADS_PER_GROUP:(gi + 1) * HEADS_PER_GROUP].T
        flat = _matmul(tab, _bucket_onehot(gi), mode="nn", name=f"{name}_{gi}", split_a=3, tn=4096)
        out.append(jnp.where(bands[gi][None], flat.reshape(HEADS_PER_GROUP, ATTN_BLOCK, 2 * ATTN_BLOCK), NEG_INF))
    return jnp.stack(out)


def _rel_bias_grad(dbias_sum, name):
    cols = []
    for gi in range(len(ATTN_CONFIGS)):
        flat = dbias_sum[gi].reshape(HEADS_PER_GROUP, -1)
        cols.append(_matmul(flat, _bucket_onehot(gi), mode="nt", name=f"{name}_{gi}", split_a=2, tk=4096).T)
    return jnp.concatenate(cols, axis=1)


def _local_step(x, mem, target, p, wb):
    depth = p["norm_g"].shape[0]
    biasm = _bias_tables(p["rel_bias"], "bias_table")
    saved = []
    for L in range(depth):
        x, s = _layer_fwd(x, mem, p, wb, L, biasm)
        saved.append(s)
    loss_vec, dx, dgf = _loss_head(x, p["final_norm_g"], target, "loss_head")
    grads = {"final_norm_g": dgf.reshape(-1)}
    per_layer = [None] * depth
    dbias_sum = 0.0
    stacked = {}
    for L in reversed(range(depth)):
        dx, per_layer[L], dbias = _layer_bwd(dx, mem, p, wb, L, saved[L], biasm, stacked)
        stacked = {n: per_layer[L][n] for n, _ in BIG}
        dbias_sum = dbias_sum + dbias
    grads.update(stacked)
    for n in per_layer[0]:
        if n not in stacked:
            grads[n] = jnp.stack([per_layer[L][n].reshape(p[n].shape[1:]) for L in range(depth)])
    grads["rel_bias"] = _rel_bias_grad(dbias_sum, "d_rel_bias")
    return jnp.sum(loss_vec), dx, grads


def _chip_coords(j):
    return j // 2, j % 2


def _place_shard(shard, ax, chip, name):
    _, a, b = shard.shape
    ra = _pick(a, 256, 16)
    full = (2, a * N_CHIPS, b) if ax == 1 else (2, a, b * N_CHIPS)
    per = a // ra

    def body(j_ref, s_ref, o_ref):
        o_ref[...] = s_ref[...].astype(BF16)

    out_idx = (lambda l, i, j: (l, j[0] * per + i, 0)) if ax == 1 else (lambda l, i, j: (l, i, j[0]))
    return pl.pallas_call(
        body, name=name,
        grid_spec=pltpu.PrefetchScalarGridSpec(
            num_scalar_prefetch=1, grid=(2, per),
            in_specs=[pl.BlockSpec((None, ra, b), lambda l, i, j: (l, i, 0))],
            out_specs=pl.BlockSpec((None, ra, b), out_idx)),
        out_shape=jax.ShapeDtypeStruct(full, BF16), compiler_params=_params("parallel", "parallel"),
    )(chip, shard)


def _gather_shards(fulls, axes, name):
    n = len(fulls)
    widths = [a.shape[ax] // N_CHIPS for a, ax in zip(fulls, axes)]
    aligns = [LANES if ax == 2 else 16 for ax in axes]

    def body(*refs):
        outs = refs[n:2 * n]
        send_sems, recv_sems, fsend_sems, frecv_sems = refs[2 * n:]
        x, y, c = lax.axis_index("x"), lax.axis_index("y"), lax.axis_index("c")
        mine = 2 * x + y
        sibling = (x, y, 1 - c)

        def window(t, layer, j):
            start = pl.ds(pl.multiple_of(j * widths[t], aligns[t]), widths[t])
            return outs[t].at[(layer, start, slice(None)) if axes[t] == 1 else (layer, slice(None), start)]

        def over_ici(t, j, block):
            return pltpu.make_async_remote_copy(
                src_ref=window(t, c, mine), dst_ref=window(t, c, block), send_sem=send_sems.at[t, j],
                recv_sem=recv_sems.at[t, block], device_id=(*_chip_coords(j), c), device_id_type=MESH)

        def over_d2d(t, j, layer):
            return pltpu.make_async_remote_copy(
                src_ref=window(t, layer, j), dst_ref=window(t, layer, j), send_sem=fsend_sems.at[t, j],
                recv_sem=frecv_sems.at[t, j], device_id=sibling, device_id_type=MESH)

        for t in range(n):
            for j in range(N_CHIPS):
                @pl.when(j != mine)
                def _():
                    over_ici(t, j, mine).start()
        for t in range(n):
            for j in range(N_CHIPS):
                @pl.when(j != mine)
                def _():
                    over_ici(t, j, j).wait_recv()
                    over_d2d(t, j, c).start()
        for t in range(n):
            for j in range(N_CHIPS):
                @pl.when(j != mine)
                def _():
                    over_ici(t, j, mine).wait_send()
                    over_d2d(t, j, c).wait_send()
                    over_d2d(t, j, 1 - c).wait_recv()

    sem = pltpu.SemaphoreType.DMA
    return pl.pallas_call(
        body, name=name, in_specs=[HBM] * n, out_specs=[HBM] * n,
        out_shape=[jax.ShapeDtypeStruct(a.shape, a.dtype) for a in fulls],
        input_output_aliases={t: t for t in range(n)},
        scratch_shapes=[sem((n, N_CHIPS)), sem((n, N_CHIPS)), sem((n, N_CHIPS)), sem((n, N_CHIPS))],
    )(*fulls)


def _scatter_slices(arrays, axes, name):
    n = len(arrays)

    def piece(a, ax):
        if ax is None:
            return a.shape, None
        w = a.shape[ax] // N_CHIPS
        return a.shape[:ax] + (w,) + a.shape[ax + 1:], w

    shapes = [piece(a, ax) for a, ax in zip(arrays, axes)]

    def body(*refs):
        ins, outs = refs[:n], refs[n:2 * n]
        send_sems, recv_sems, loc_sems = refs[2 * n:]
        x, y, c = lax.axis_index("x"), lax.axis_index("y"), lax.axis_index("c")
        mine = 2 * x + y

        def src(t, j):
            ax, w = axes[t], shapes[t][1]
            if ax is None:
                return ins[t]
            idx = tuple(pl.ds(j * w, w) if d == ax else slice(None) for d in range(len(arrays[t].shape)))
            return ins[t].at[idx]

        for t in range(n):
            for j in range(N_CHIPS):
                @pl.when(j == mine)
                def _():
                    pltpu.make_async_copy(src(t, j), outs[t].at[j], loc_sems.at[t]).start()

                @pl.when(j != mine)
                def _():
                    pltpu.make_async_remote_copy(
                        src_ref=src(t, j), dst_ref=outs[t].at[mine], send_sem=send_sems.at[t, j], recv_sem=recv_sems.at[t, mine],
                        device_id=(*_chip_coords(j), c), device_id_type=MESH).start()
        for t in range(n):
            for j in range(N_CHIPS):
                @pl.when(j == mine)
                def _():
                    pltpu.make_async_copy(src(t, j), outs[t].at[j], loc_sems.at[t]).wait()

                @pl.when(j != mine)
                def _():
                    cp = pltpu.make_async_remote_copy(
                        src_ref=src(t, j), dst_ref=outs[t].at[j], send_sem=send_sems.at[t, j], recv_sem=recv_sems.at[t, j],
                        device_id=(*_chip_coords(j), c), device_id_type=MESH)
                    cp.wait_send()
                    cp.wait_recv()

    return pl.pallas_call(
        body, name=name, in_specs=[HBM] * n, out_specs=[HBM] * n,
        out_shape=[jax.ShapeDtypeStruct((N_CHIPS,) + sh, a.dtype) for a, (sh, _) in zip(arrays, shapes)],
        scratch_shapes=[pltpu.SemaphoreType.DMA((n, N_CHIPS)), pltpu.SemaphoreType.DMA((n, N_CHIPS)), pltpu.SemaphoreType.DMA((n,))],
    )(*arrays)


def _swap_layers(stacked, name):
    n = len(stacked)

    def body(*refs):
        ins, outs = refs[:n], refs[n:2 * n]
        send_sems, recv_sems = refs[2 * n:]
        c = lax.axis_index("c")
        peer = (lax.axis_index("x"), lax.axis_index("y"), 1 - c)
        cps = [pltpu.make_async_remote_copy(src_ref=ins[t].at[1 - c], dst_ref=outs[t], send_sem=send_sems.at[t],
                                            recv_sem=recv_sems.at[t], device_id=peer, device_id_type=MESH) for t in range(n)]
        for cp in cps:
            cp.start()
        for cp in cps:
            cp.wait_send()
            cp.wait_recv()

    return pl.pallas_call(
        body, name=name, in_specs=[HBM] * n, out_specs=[HBM] * n,
        out_shape=[jax.ShapeDtypeStruct(a.shape[1:], a.dtype) for a in stacked],
        scratch_shapes=[pltpu.SemaphoreType.DMA((n,)), pltpu.SemaphoreType.DMA((n,))],
    )(*stacked)


def _merge_layers(stacked, name):
    n = len(stacked)

    def body(*refs):
        outs = refs[n:2 * n]
        send_sems, recv_sems = refs[2 * n:]
        c = lax.axis_index("c")
        peer = (lax.axis_index("x"), lax.axis_index("y"), 1 - c)
        for t in range(n):
            pltpu.make_async_remote_copy(src_ref=outs[t].at[c], dst_ref=outs[t].at[c], send_sem=send_sems.at[t],
                                         recv_sem=recv_sems.at[t], device_id=peer, device_id_type=MESH).start()
        for t in range(n):
            cp = pltpu.make_async_remote_copy(src_ref=outs[t].at[c], dst_ref=outs[t].at[1 - c], send_sem=send_sems.at[t],
                                              recv_sem=recv_sems.at[t], device_id=peer, device_id_type=MESH)
            cp.wait_send()
            cp.wait_recv()

    sem = pltpu.SemaphoreType.DMA
    return pl.pallas_call(
        body, name=name, in_specs=[HBM] * n, out_specs=[HBM] * n,
        out_shape=[jax.ShapeDtypeStruct(a.shape, a.dtype) for a in stacked],
        input_output_aliases={t: t for t in range(n)}, scratch_shapes=[sem((n,)), sem((n,))],
    )(*stacked)


def _pair_sum(stacked, landed, core, name):
    _, K, N = stacked.shape
    tr = _pick(K, max(16, (1 << 19) // N // 16 * 16), 16)

    def body(c_ref, s_ref, l_ref, o_ref):
        o_ref[...] = (s_ref[...].astype(F32) + l_ref[...].astype(F32)).astype(o_ref.dtype)

    return pl.pallas_call(
        body, name=name,
        grid_spec=pltpu.PrefetchScalarGridSpec(
            num_scalar_prefetch=1, grid=(K // tr,),
            in_specs=[pl.BlockSpec((None, tr, N), lambda i, c: (c[0], i, 0)), pl.BlockSpec((tr, N), lambda i, c: (i, 0))],
            out_specs=pl.BlockSpec((tr, N), lambda i, c: (i, 0))),
        out_shape=jax.ShapeDtypeStruct((K, N), stacked.dtype), compiler_params=_params("parallel"),
    )(core, stacked, landed)


def _sum_chips(landed, core, name):
    _, R, C = landed.shape
    tr = _pick(R, max(SUBLANES, (1 << 19) // C // 16 * 16), 16)

    def body(c_ref, l_ref, o_ref):
        acc = l_ref[0].astype(F32) + l_ref[1].astype(F32)
        acc = acc + l_ref[2].astype(F32)
        o_ref[...] = acc + l_ref[3].astype(F32)

    return pl.pallas_call(
        body, name=name,
        grid_spec=pltpu.PrefetchScalarGridSpec(
            num_scalar_prefetch=1, grid=(R // tr,),
            in_specs=[pl.BlockSpec((N_CHIPS, tr, C), lambda i, c: (0, i, 0))],
            out_specs=pl.BlockSpec((None, tr, C), lambda i, c: (c[0], i, 0))),
        out_shape=jax.ShapeDtypeStruct((2, R, C), F32), compiler_params=_params("parallel"),
    )(core, landed)


def _adamw_math(w_ref, g_ref, m_ref, v_ref, d_ref, nm_ref, nv_ref):
    c1 = 1.0 / (1.0 - ADAM_B1 ** ADAM_STEP)
    c2 = 1.0 / (1.0 - ADAM_B2 ** ADAM_STEP)
    g = g_ref[...]
    nm = ADAM_B1 * m_ref[...] + (1.0 - ADAM_B1) * g
    nv = ADAM_B2 * v_ref[...] + (1.0 - ADAM_B2) * (g * g)
    nm_ref[...] = nm
    nv_ref[...] = nv
    d_ref[...] = -ADAM_LR * ((nm * c1) / (jnp.sqrt(nv * c2) + ADAM_EPS) + ADAM_WD * w_ref[...])


def _adamw_whole(w, g, m, v, name):
    shape = w.shape
    view = (-1,) + shape[-2:] if w.ndim >= 2 else (1, 1, -1)

    def body(*refs):
        _adamw_math(*refs)

    res = pl.pallas_call(body, name=name, out_shape=[jax.ShapeDtypeStruct(w.reshape(view).shape, F32)] * 3,
                         compiler_params=pltpu.CompilerParams(vmem_limit_bytes=VMEM_LIMIT_BYTES))(
        *(a.reshape(view) for a in (w, g, m, v)))
    return [r.reshape(shape) for r in res]


def _adamw(w, g, m, v, name):
    R, C = w.shape
    tr = _pick(R, max(SUBLANES, (1 << 18) // C // 8 * 8), SUBLANES)

    def body(*refs):
        _adamw_math(*refs)

    blk = pl.BlockSpec((tr, C), lambda i: (i, 0))
    return pl.pallas_call(
        body, name=name, grid=(R // tr,), in_specs=[blk] * 4, out_specs=[blk] * 3,
        out_shape=[jax.ShapeDtypeStruct((R, C), F32)] * 3, compiler_params=_params("parallel"),
    )(w, g, m, v)


def _pack_small(d, prefix=""):
    flat = jnp.concatenate([d[prefix + n].astype(F32).reshape(-1) for n in SMALL])
    pad = (-flat.shape[0]) % (2 * 16 * LANES)
    return jnp.pad(flat, (0, pad)).reshape(-1, LANES)


def _unpack_small(packed, shapes):
    flat = packed.reshape(-1)
    out, off = {}, 0
    for n in SMALL:
        size = int(np.prod(shapes[n]))
        out[n] = flat[off:off + size].reshape(shapes[n])
        off += size
    return out


def kernel(*args):
    p = dict(zip(INPUTS, args))
    x, mem, target = p["x"][0], p["mem"][0], p["loss_target"][0]

    names = [n for n, _ in BIG] + ["small"]
    core = lax.axis_index("c").astype(jnp.int32).reshape(1)
    chip = (2 * lax.axis_index("x") + lax.axis_index("y")).astype(jnp.int32).reshape(1)
    placed = [_place_shard(p[n], ax, chip, f"place_{n}") for n, ax in BIG]
    wb = dict(zip(names, _gather_shards(placed, [ax for _, ax in BIG], "gather_weights")))

    loss_part, dx, grads = _local_step(x, mem, target, p, wb)
    loss = lax.psum(loss_part, ("x", "y", "c"))

    stacked = [grads[n] for n, _ in BIG] + [_pack_small(grads).reshape(2, -1, LANES)]
    theirs = _swap_layers(stacked, "swap_layers")
    pair = [_pair_sum(s, o, core, f"pair_sum_{n}") for n, s, o in zip(names, stacked, theirs)]
    landed = _scatter_slices(pair, [ax - 1 for _, ax in BIG] + [None], "scatter_grads")
    reduced = [_sum_chips(ld.reshape(N_CHIPS, -1, ld.shape[-1]), core, f"sum_chips_{n}") for n, ld in zip(names, landed)]
    total = _merge_layers(reduced, "merge_layers")

    out = {}
    for (n, _), g in zip(BIG, total):
        sh = p[n].shape
        two_d = lambda a: a.reshape(-1, sh[-1])
        res = (g,) + tuple(_adamw(two_d(p[n]), two_d(g), two_d(p["m_" + n]), two_d(p["v_" + n]), f"adamw_{n}"))
        for key, r in zip(("grad_", "delta_", "new_m_", "new_v_"), res):
            out[key + n] = r.reshape(sh)
    for n, g in _unpack_small(total[-1], {n: p[n].shape for n in SMALL}).items():
        res = (g,) + tuple(_adamw_whole(p[n], g, p["m_" + n], p["v_" + n], f"adamw_{n}"))
        for key, r in zip(("grad_", "delta_", "new_m_", "new_v_"), res):
            out[key + n] = r

    result = [loss, dx.reshape(p["x"].shape)]
    for key in ("grad_", "delta_", "new_m_", "new_v_"):
        result += [out[key + n] for n in WEIGHTS]
    return tuple(result)
```

```python
import math

import jax
import jax.numpy as jnp
import numpy as np
from jax import lax
from jax.experimental import pallas as pl
from jax.experimental.pallas import tpu as pltpu

F32 = jnp.float32
BF16 = jnp.bfloat16
MESH = pl.DeviceIdType.MESH
HBM = pl.BlockSpec(memory_space=pltpu.HBM)

EPS = 1e-6
SSM_GROUP = 16
SSM_STATE = 64
ATTN_HEAD_DIM = 64
HEADS_PER_GROUP = 4
ATTN_CONFIGS = ((128, 1), (512, 4), (2048, 16))
ATTN_BLOCK = 128
NUM_BUCKETS = 32
REL_MAX_DISTANCE = 2048
NEG_INF = -1e30
MEM_HEADS = 4
ADAM_LR = 0.001
ADAM_B1 = 0.9
ADAM_B2 = 0.999
ADAM_EPS = 1e-08
ADAM_WD = 0.01
ADAM_STEP = 10

LANES = 128
SUBLANES = 8
VMEM_LIMIT_BYTES = 48 * 1024 * 1024
SSM_BLOCK_CH = 128

N_CHIPS = 4
BIG = (("w_in", 2), ("w_glu", 1), ("w_mem_kv", 1), ("w_br_ssm", 2), ("w_br_attn", 2), ("w_br_mem", 2), ("w_out", 1))
SMALL = ("norm_g", "mem_norm_g", "b_gate", "ssm_lambda_re", "ssm_lambda_im", "ssm_log_dt", "ssm_b_re", "ssm_b_im",
         "ssm_c_re", "ssm_c_im", "ssm_d", "b_glu", "rel_bias", "final_norm_g")
WEIGHTS = ("norm_g", "mem_norm_g", "w_in", "b_gate", "ssm_lambda_re", "ssm_lambda_im", "ssm_log_dt", "ssm_b_re",
           "ssm_b_im", "ssm_c_re", "ssm_c_im", "ssm_d", "w_glu", "b_glu", "w_mem_kv", "w_br_ssm", "w_br_attn",
           "w_br_mem", "w_out", "rel_bias", "final_norm_g")
INPUTS = ("x", "mem") + WEIGHTS + ("loss_target",) + tuple("m_" + n for n in WEIGHTS) + tuple("v_" + n for n in WEIGHTS)


def _params(*sem):
    return pltpu.CompilerParams(dimension_semantics=sem, vmem_limit_bytes=VMEM_LIMIT_BYTES)


def _pick(dim, pref, align):
    if dim <= pref:
        return dim
    t = pref - pref % align
    while t >= align:
        if dim % t == 0:
            return t
        t -= align
    return dim


def _sigmoid(v):
    return 0.5 * jnp.tanh(0.5 * v) + 0.5


def _silu_and_grad(z):
    s = _sigmoid(z)
    return z * s, s * (1.0 + z * (1.0 - s))


_GELU_C = math.sqrt(2.0 / math.pi)


def _gelu_and_grad(y):
    inner = _GELU_C * (y + 0.044715 * y * y * y)
    t = jnp.tanh(inner)
    g = 0.5 * y * (1.0 + t)
    dg = 0.5 * (1.0 + t) + 0.5 * y * (1.0 - t * t) * _GELU_C * (1.0 + 3.0 * 0.044715 * y * y)
    return g, dg


def _dot(a, b, dims):
    return lax.dot_general(a, b, (dims, ((), ())), preferred_element_type=F32)


NN = ((1,), (0,))
NT = ((1,), (1,))
TN = ((0,), (0,))


def _matmul(a, b, *, mode, name, out_dtype=F32, add=None, split_a=1, tm=1024, tn=768, tk=2304,
            a_lead=None, b_lead=None, b_off=0, n_cols=None, stack=None):
    ashape = a.shape if a_lead is None else a.shape[1:]
    K, M = ashape if mode == "tn" else ashape[::-1]
    bshape = b.shape if b_lead is None else b.shape[1:]
    N = n_cols or (bshape[0] if mode == "nt" else bshape[1])
    if mode != "tn" and M >= 4 * tm:
        tm = 2 * tm
    tm = _pick(M, tm, LANES if mode == "tn" else SUBLANES)
    tn = _pick(math.gcd(N, b_off) if b_off else N, tn, LANES)
    tk = _pick(K, tk, LANES)
    nk = K // tk
    joff = b_off // tn
    dims = {"nn": NN, "nt": NT, "tn": TN}[mode]
    has_add = add is not None
    has_prev = stack is not None and stack[2] is not None

    def body(*refs):
        a_ref, b_ref = refs[:2]
        add_ref = refs[2] if has_add else None
        o_ref = refs[-2] if nk > 1 else refs[-1]
        k = pl.program_id(2)
        bv = b_ref[...].astype(BF16)
        if split_a > 1:
            rest = a_ref[...].astype(F32)
            part = 0.0
            for _ in range(split_a):
                piece = rest.astype(BF16)
                part = part + _dot(piece, bv, dims)
                rest = rest - piece.astype(F32)
        else:
            part = _dot(a_ref[...].astype(BF16), bv, dims)

        def finish(r):
            if has_add:
                r = r + add_ref[...]
            o_ref[...] = r.astype(out_dtype)

        if nk == 1:
            finish(part)
            return
        acc_ref = refs[-1]

        @pl.when(k == 0)
        def _():
            acc_ref[...] = part

        @pl.when((k > 0) & (k < nk - 1))
        def _():
            acc_ref[...] += part

        @pl.when(k == nk - 1)
        def _():
            finish(acc_ref[...] + part)

    alead = () if a_lead is None else (a_lead,)
    alead_blk = () if a_lead is None else (None,)
    if mode == "tn":
        a_spec = pl.BlockSpec(alead_blk + (tk, tm), lambda i, j, k: alead + (k, i))
    else:
        a_spec = pl.BlockSpec(alead_blk + (tm, tk), lambda i, j, k: alead + (i, k))
    lead = () if b_lead is None else (b_lead,)
    lead_blk = () if b_lead is None else (None,)
    if mode == "nt":
        b_spec = pl.BlockSpec(lead_blk + (tn, tk), lambda i, j, k: lead + (j + joff, k))
    else:
        b_spec = pl.BlockSpec(lead_blk + (tk, tn), lambda i, j, k: lead + (k, j + joff))
    in_specs = [a_spec, b_spec]
    args = [a, b]
    if has_add:
        in_specs.append(pl.BlockSpec((tm, tn), lambda i, j, k: (i, j)))
        args.append(add)
    aliases = {}
    if stack is None:
        out_spec = pl.BlockSpec((tm, tn), lambda i, j, k: (i, j))
        out_shape = jax.ShapeDtypeStruct((M, N), out_dtype)
    else:
        layer, depth, prev = stack
        out_spec = pl.BlockSpec((None, tm, tn), lambda i, j, k: (layer, i, j))
        out_shape = jax.ShapeDtypeStruct((depth, M, N), out_dtype)
        if has_prev:
            in_specs.append(pl.BlockSpec(memory_space=pl.ANY))
            args.append(prev)
            aliases = {len(args) - 1: 0}
    return pl.pallas_call(
        body, name=name, grid=(M // tm, N // tn, nk), in_specs=in_specs, out_specs=out_spec, out_shape=out_shape,
        scratch_shapes=[pltpu.VMEM((tm, tn), F32)] if nk > 1 else [], input_output_aliases=aliases,
        compiler_params=_params("parallel", "parallel", "arbitrary"),
    )(*args)


def _rmsnorm(x, g, name):
    T, D = x.shape
    tm = _pick(T, 512, SUBLANES)

    def body(x_ref, g_ref, h_ref):
        xv = x_ref[...]
        r = lax.rsqrt(jnp.mean(xv * xv, axis=-1, keepdims=True) + EPS)
        h_ref[...] = (xv * r * g_ref[...]).astype(BF16)

    return pl.pallas_call(
        body, name=name, grid=(T // tm,),
        in_specs=[pl.BlockSpec((tm, D), lambda i: (i, 0)), pl.BlockSpec((1, D), lambda i: (0, 0))],
        out_specs=pl.BlockSpec((tm, D), lambda i: (i, 0)),
        out_shape=jax.ShapeDtypeStruct((T, D), BF16), compiler_params=_params("parallel"),
    )(x, g.reshape(1, D))


def _rmsnorm_bwd(x, g, dh, dres, name):
    T, D = x.shape
    tm = _pick(T, 512, SUBLANES)
    with_res = dres is not None

    def body(*refs):
        if with_res:
            x_ref, g_ref, dh_ref, dres_ref, dx_ref, dg_ref = refs
        else:
            x_ref, g_ref, dh_ref, dx_ref, dg_ref = refs
        xv = x_ref[...]
        dhv = dh_ref[...]
        r = lax.rsqrt(jnp.mean(xv * xv, axis=-1, keepdims=True) + EPS)
        dyg = dhv * g_ref[...]
        c = jnp.mean(dyg * xv, axis=-1, keepdims=True)
        dx = r * dyg - xv * (r * r * r) * c
        if with_res:
            dx = dx + dres_ref[...]
        dx_ref[...] = dx

        @pl.when(pl.program_id(0) == 0)
        def _():
            dg_ref[...] = jnp.zeros_like(dg_ref)

        dg_ref[...] += jnp.sum(dhv * xv * r, axis=0, keepdims=True)

    row = pl.BlockSpec((tm, D), lambda i: (i, 0))
    vec = pl.BlockSpec((1, D), lambda i: (0, 0))
    ins = [x, g.reshape(1, D), dh] + ([dres] if with_res else [])
    return pl.pallas_call(
        body, name=name, grid=(T // tm,), in_specs=[row, vec, row] + ([row] if with_res else []),
        out_specs=[row, vec],
        out_shape=[jax.ShapeDtypeStruct((T, D), F32), jax.ShapeDtypeStruct((1, D), F32)],
        compiler_params=_params("arbitrary"),
    )(*ins)


def _loss_head(x, g, target, name):
    T, D = x.shape
    tm = _pick(T, 512, SUBLANES)

    def body(x_ref, g_ref, t_ref, loss_ref, dx_ref, dg_ref):
        xv = x_ref[...]
        gv = g_ref[...]
        r = lax.rsqrt(jnp.mean(xv * xv, axis=-1, keepdims=True) + EPS)
        e = xv * r * gv - t_ref[...]
        dy = e * (1.0 / D)
        dyg = dy * gv
        c = jnp.mean(dyg * xv, axis=-1, keepdims=True)
        dx_ref[...] = r * dyg - xv * (r * r * r) * c

        @pl.when(pl.program_id(0) == 0)
        def _():
            loss_ref[...] = jnp.zeros_like(loss_ref)
            dg_ref[...] = jnp.zeros_like(dg_ref)

        loss_ref[...] += jnp.sum(e * e, axis=0, keepdims=True) * (0.5 / D)
        dg_ref[...] += jnp.sum(dy * xv * r, axis=0, keepdims=True)

    row = pl.BlockSpec((tm, D), lambda i: (i, 0))
    vec = pl.BlockSpec((1, D), lambda i: (0, 0))
    return pl.pallas_call(
        body, name=name, grid=(T // tm,), in_specs=[row, vec, row], out_specs=[vec, row, vec],
        out_shape=[jax.ShapeDtypeStruct((1, D), F32), jax.ShapeDtypeStruct((T, D), F32), jax.ShapeDtypeStruct((1, D), F32)],
        compiler_params=_params("arbitrary"),
    )(x, g.reshape(1, D), target)


def _ssm_disc_math(lre, lim, logdt, br, bi):
    dt = jnp.exp(logdt)
    mag = jnp.exp(lre * dt)
    ar = mag * jnp.cos(lim * dt)
    ai = mag * jnp.sin(lim * dt)
    den = lre * lre + lim * lim
    nr = ar - 1.0
    fr = (nr * lre + ai * lim) / den
    fi = (ai * lre - nr * lim) / den
    return ar, ai, fr[None] * br - fi[None] * bi, fr[None] * bi + fi[None] * br


def _ssm_disc(lre, lim, logdt, br, bi, name):
    def body(lre_ref, lim_ref, dt_ref, br_ref, bi_ref, ar_ref, ai_ref, bbr_ref, bbi_ref):
        ar, ai, bbr, bbi = _ssm_disc_math(lre_ref[...], lim_ref[...], dt_ref[...], br_ref[...], bi_ref[...])
        ar_ref[...] = ar
        ai_ref[...] = ai
        bbr_ref[...] = bbr
        bbi_ref[...] = bbi

    sd = jax.ShapeDtypeStruct
    return pl.pallas_call(
        body, name=name, out_shape=[sd(lre.shape, F32), sd(lre.shape, F32), sd(br.shape, F32), sd(br.shape, F32)],
    )(lre, lim, logdt, br, bi)


def _ssm_disc_bwd(lre, lim, logdt, br, bi, dar, dai, dbbr, dbbi, name):
    def body(lre_ref, lim_ref, dt_ref, br_ref, bi_ref, dar_ref, dai_ref, dbbr_ref, dbbi_ref,
             glre_ref, glim_ref, gdt_ref, gbr_ref, gbi_ref):
        _, vjp = jax.vjp(_ssm_disc_math, lre_ref[...], lim_ref[...], dt_ref[...], br_ref[...], bi_ref[...])
        glre, glim, gdt, gbr, gbi = vjp((dar_ref[...], dai_ref[...], dbbr_ref[...], dbbi_ref[...]))
        glre_ref[...] = glre
        glim_ref[...] = glim
        gdt_ref[...] = gdt
        gbr_ref[...] = gbr
        gbi_ref[...] = gbi

    sd = jax.ShapeDtypeStruct
    return pl.pallas_call(
        body, name=name,
        out_shape=[sd(lre.shape, F32), sd(lre.shape, F32), sd(logdt.shape, F32), sd(br.shape, F32), sd(br.shape, F32)],
    )(lre, lim, logdt, br, bi, dar, dai, dbbr, dbbi)


SSM_STEPS_FWD = 256
SSM_STEPS_BWD = 256


def _ssm_tiles(ref, v, off, steps, n):
    return [ref[v, pl.ds(off + j, steps, stride=SUBLANES), :] for j in range(n)]


def _ssm_fwd(uz, bmat, cmat, art, ait, dvec, name, steps=SSM_STEPS_FWD):
    T = uz.shape[0]
    nblk, cb, width = bmat.shape
    C = nblk * cb
    half = width // 2
    nt = half // LANES
    npair = nblk // 2
    kc = min(steps, T)
    nchunk = T // kc

    def body(u_ref, b_ref, c_ref, ar_ref, ai_ref, d_ref, y_ref, xr_ref, xi_ref, sr_ref, si_ref):
        @pl.when(pl.program_id(0) == 0)
        def _():
            sr_ref[...] = jnp.zeros_like(sr_ref)
            si_ref[...] = jnp.zeros_like(si_ref)

        uv = u_ref[...]
        for b in range(nblk):
            bu = _dot(uv[:, b * cb:(b + 1) * cb].astype(BF16), b_ref[b], NN)
            v, off = b // 2, nt * (b % 2)
            for j in range(nt):
                xr_ref[v, pl.ds(off + j, kc, stride=SUBLANES), :] = bu[:, j * LANES:(j + 1) * LANES]
                xi_ref[v, pl.ds(off + j, kc, stride=SUBLANES), :] = bu[:, half + j * LANES:half + (j + 1) * LANES]
        ars = [ar_ref[v] for v in range(npair)]
        ais = [ai_ref[v] for v in range(npair)]

        def step(k, carry):
            row = pl.ds(k * SUBLANES, SUBLANES)
            out = []
            for v in range(npair):
                xr, xi = carry[2 * v], carry[2 * v + 1]
                nr = ars[v] * xr - ais[v] * xi + xr_ref[v, row, :]
                ni = ars[v] * xi + ais[v] * xr + xi_ref[v, row, :]
                xr_ref[v, row, :] = nr
                xi_ref[v, row, :] = ni
                out += [nr, ni]
            return tuple(out)

        fin = tuple(ref[v] for v in range(npair) for ref in (sr_ref, si_ref))
        for k in range(kc):
            fin = step(k, fin)
        for v in range(npair):
            sr_ref[v] = fin[2 * v]
            si_ref[v] = fin[2 * v + 1]
        for b in range(nblk):
            v, off = b // 2, nt * (b % 2)
            xb = jnp.concatenate(_ssm_tiles(xr_ref, v, off, kc, nt) + _ssm_tiles(xi_ref, v, off, kc, nt), axis=1)
            cols = slice(b * cb, (b + 1) * cb)
            y_ref[:, cols] = _dot(xb.astype(BF16), c_ref[b], NN) + d_ref[:, cols] * uv[:, cols]

    whole = lambda a: pl.BlockSpec(a.shape, lambda c: (0,) * a.ndim)
    st = pl.BlockSpec((npair, kc * SUBLANES, LANES), lambda c: (0, c, 0))
    sd = jax.ShapeDtypeStruct
    return pl.pallas_call(
        body, name=name, grid=(nchunk,),
        in_specs=[pl.BlockSpec((kc, C), lambda c: (c, 0)), whole(bmat), whole(cmat), whole(art), whole(ait), whole(dvec)],
        out_specs=[pl.BlockSpec((kc, C), lambda c: (c, 0)), st, st],
        out_shape=[sd((T, C), F32), sd((npair, T * SUBLANES, LANES), F32), sd((npair, T * SUBLANES, LANES), F32)],
        scratch_shapes=[pltpu.VMEM((npair, SUBLANES, LANES), F32), pltpu.VMEM((npair, SUBLANES, LANES), F32)],
        compiler_params=_params("arbitrary"),
    )(uz, bmat, cmat, art, ait, dvec)


def _ssm_bwd(dy, uz, xr, xi, bmat, cmat, art, ait, dvec, dproj, name):
    T = uz.shape[0]
    nblk, cb, width = bmat.shape
    C = nblk * cb
    half = width // 2
    nt = half // LANES
    npair = nblk // 2
    kc = min(SSM_STEPS_BWD, T)
    nchunk = T // kc

    def body(dy_ref, u_ref, xr_ref, xi_ref, xpr_ref, xpi_ref, b_ref, c_ref, ar_ref, ai_ref, d_ref, _,
             du_ref, db_ref, dc_ref, dar_ref, dai_ref, dd_ref, gr_ref, gi_ref, sr_ref, si_ref):
        c = pl.program_id(0)

        @pl.when(c == 0)
        def _():
            for ref in (sr_ref, si_ref, db_ref, dc_ref, dar_ref, dai_ref, dd_ref):
                ref[...] = jnp.zeros_like(ref)

        dyv = dy_ref[...]
        uv = u_ref[...]
        for b in range(nblk):
            dx = _dot(dyv[:, b * cb:(b + 1) * cb].astype(BF16), c_ref[b], NT)
            v, off = b // 2, nt * (b % 2)
            for j in range(nt):
                gr_ref[v, pl.ds(off + j, kc, stride=SUBLANES), :] = dx[:, j * LANES:(j + 1) * LANES]
                gi_ref[v, pl.ds(off + j, kc, stride=SUBLANES), :] = dx[:, half + j * LANES:half + (j + 1) * LANES]
        ars = [ar_ref[v] for v in range(npair)]
        ais = [ai_ref[v] for v in range(npair)]

        def pair_update(v, gr, gi, row):
            nr = ars[v] * gr + ais[v] * gi + gr_ref[v, row, :]
            ni = ars[v] * gi - ais[v] * gr + gi_ref[v, row, :]
            gr_ref[v, row, :] = nr
            gi_ref[v, row, :] = ni
            return nr, ni

        def step(i, carry):
            k = kc - 1 - i
            row = pl.ds(k * SUBLANES, SUBLANES)
            prow = pl.ds((k - 1) * SUBLANES, SUBLANES)
            out = []
            for v in range(npair):
                gr, gi, sr, si = carry[4 * v:4 * v + 4]
                nr, ni = pair_update(v, gr, gi, row)
                pr, pi = xr_ref[v, prow, :], xi_ref[v, prow, :]
                out += [nr, ni, sr + pr * nr + pi * ni, si + pr * ni - pi * nr]
            return tuple(out)

        mid = tuple(ref[v] for v in range(npair) for ref in (sr_ref, si_ref, dar_ref, dai_ref))
        for i in range(kc - 1):
            mid = step(i, mid)
        live = (c < nchunk - 1).astype(F32)
        row0 = pl.ds(0, SUBLANES)
        for v in range(npair):
            gr, gi, sr, si = mid[4 * v:4 * v + 4]
            nr, ni = pair_update(v, gr, gi, row0)
            pr, pi = xpr_ref[v] * live, xpi_ref[v] * live
            sr_ref[v] = nr
            si_ref[v] = ni
            dar_ref[v] = sr + pr * nr + pi * ni
            dai_ref[v] = si + pr * ni - pi * nr
        for b in range(nblk):
            v, off = b // 2, nt * (b % 2)
            cols = slice(b * cb, (b + 1) * cb)
            gb = jnp.concatenate(_ssm_tiles(gr_ref, v, off, kc, nt) + _ssm_tiles(gi_ref, v, off, kc, nt), axis=1).astype(BF16)
            xb = jnp.concatenate(_ssm_tiles(xr_ref, v, off, kc, nt) + _ssm_tiles(xi_ref, v, off, kc, nt), axis=1).astype(BF16)
            du_ref[:, cols] = (_dot(gb, b_ref[b], NT) + dyv[:, cols] * d_ref[:, cols]).astype(BF16)
            db_ref[b] += _dot(uv[:, cols].astype(BF16), gb, TN)
            dc_ref[b] += _dot(dyv[:, cols].astype(BF16), xb, TN)
        dd_ref[...] += jnp.sum(dyv * uv, axis=0, keepdims=True)

    whole = lambda a: pl.BlockSpec(a.shape, lambda c: (0,) * a.ndim)
    rev = lambda c: (nchunk - 1 - c, 0)
    st = pl.BlockSpec((npair, kc * SUBLANES, LANES), lambda c: (0, nchunk - 1 - c, 0))
    stp = pl.BlockSpec((npair, SUBLANES, LANES), lambda c: (0, jnp.maximum((nchunk - 1 - c) * kc - 1, 0), 0))
    acc = lambda shape: pl.BlockSpec(shape, lambda c: (0,) * len(shape))
    sd = jax.ShapeDtypeStruct
    pair_shape = (npair, SUBLANES, LANES)
    return pl.pallas_call(
        body, name=name, grid=(nchunk,),
        in_specs=[pl.BlockSpec((kc, C), rev), pl.BlockSpec((kc, C), rev), st, st, stp, stp, whole(bmat), whole(cmat),
                  whole(art), whole(ait), whole(dvec), pl.BlockSpec(memory_space=pl.ANY)],
        out_specs=[pl.BlockSpec((kc, C), rev), acc(bmat.shape), acc(bmat.shape), acc(pair_shape), acc(pair_shape), acc((1, C))],
        out_shape=[sd(dproj.shape, BF16), sd(bmat.shape, F32), sd(bmat.shape, F32), sd(pair_shape, F32), sd(pair_shape, F32),
                   sd((1, C), F32)],
        scratch_shapes=[pltpu.VMEM((npair, kc * SUBLANES, LANES), F32), pltpu.VMEM((npair, kc * SUBLANES, LANES), F32),
                        pltpu.VMEM(pair_shape, F32), pltpu.VMEM(pair_shape, F32)],
        input_output_aliases={11: 0}, compiler_params=_params("arbitrary"),
    )(dy, uz, xr, xi, xr, xi, bmat, cmat, art, ait, dvec, dproj)


def _ssm_post(y, z, w_glu, b_glu, name):
    T, C = y.shape
    tm = _pick(T, 512, SUBLANES)

    def body(y_ref, z_ref, w_ref, b_ref, o_ref, a_ref):
        a, _ = _gelu_and_grad(y_ref[...])
        ab = a.astype(BF16)
        sg = _sigmoid(_dot(ab, w_ref[...], NN) + b_ref[...])
        sz, _ = _silu_and_grad(z_ref[...].astype(F32))
        o_ref[...] = (a * sg * sz).astype(BF16)
        a_ref[...] = ab

    row = pl.BlockSpec((tm, C), lambda i: (i, 0))
    return pl.pallas_call(
        body, name=name, grid=(T // tm,),
        in_specs=[row, row, pl.BlockSpec((C, C), lambda i: (0, 0)), pl.BlockSpec((1, C), lambda i: (0, 0))],
        out_specs=[row, row], out_shape=[jax.ShapeDtypeStruct((T, C), BF16)] * 2, compiler_params=_params("parallel"),
    )(y, z, w_glu, b_glu.reshape(1, C))


def _ssm_post_bwd(do, y, z, w_glu, b_glu, dproj, col, name):
    T, C = y.shape
    tm = _pick(T, 512, SUBLANES)

    def body(do_ref, y_ref, z_ref, w_ref, b_ref, _, dy_ref, dz_ref, ds_ref, db_ref):
        dov = do_ref[...]
        a, da_dy = _gelu_and_grad(y_ref[...])
        sg = _sigmoid(_dot(a.astype(BF16), w_ref[...], NN) + b_ref[...])
        sz, dsz = _silu_and_grad(z_ref[...].astype(F32))
        yg = a * sg
        dz_ref[...] = (dov * yg * dsz).astype(BF16)
        dyg = dov * sz
        ds = dyg * a * sg * (1.0 - sg)
        dsb = ds.astype(BF16)
        ds_ref[...] = dsb
        da = dyg * sg + _dot(dsb, w_ref[...], NT)
        dy_ref[...] = da * da_dy

        @pl.when(pl.program_id(0) == 0)
        def _():
            db_ref[...] = jnp.zeros_like(db_ref)

        db_ref[...] += jnp.sum(ds, axis=0, keepdims=True)

    row = pl.BlockSpec((tm, C), lambda i: (i, 0))
    vec = pl.BlockSpec((1, C), lambda i: (0, 0))
    sd = jax.ShapeDtypeStruct
    return pl.pallas_call(
        body, name=name, grid=(T // tm,),
        in_specs=[row, row, row, pl.BlockSpec((C, C), lambda i: (0, 0)), vec, pl.BlockSpec(memory_space=pl.ANY)],
        out_specs=[row, pl.BlockSpec((tm, C), lambda i: (i, col // C)), row, vec],
        out_shape=[sd((T, C), F32), sd(dproj.shape, BF16), sd((T, C), BF16), sd((1, C), F32)],
        input_output_aliases={5: 1}, compiler_params=_params("arbitrary"),
    )(do, y, z, w_glu, b_glu.reshape(1, C), dproj)


def _rel_bucket(dist):
    n = jnp.maximum(dist, 0)
    max_exact = NUM_BUCKETS // 2
    n_f = jnp.maximum(n, 1).astype(F32)
    large = max_exact + (jnp.log(n_f / max_exact) / math.log(REL_MAX_DISTANCE / max_exact)
                         * (NUM_BUCKETS - max_exact)).astype(jnp.int32)
    large = jnp.minimum(large, NUM_BUCKETS - 1)
    return jnp.where(n < max_exact, n, large)


def _band_tables():
    qi = jnp.arange(ATTN_BLOCK)[:, None]
    kj = jnp.arange(2 * ATTN_BLOCK)[None, :]
    delta = ATTN_BLOCK + qi - kj
    buckets, bands = [], []
    for window, dilation in ATTN_CONFIGS:
        bands.append((delta >= 0) & (delta <= window // dilation))
        buckets.append(_rel_bucket(jnp.maximum(delta, 0) * dilation))
    return jnp.stack(buckets), jnp.stack(bands)


ATTN_UNITS = 4


def _attn_units_for(g, forward):
    if ATTN_CONFIGS[g][1] >= 16:
        return ATTN_UNITS
    return 4 * ATTN_UNITS if forward else 2 * ATTN_UNITS


def _attn_tile(T, r, nu):
    nq = max(1, nu // r)
    rows = ATTN_BLOCK * r * nq
    return nq, rows, T // rows


def _attn_units(r, nq, chunk, nu):
    if r >= nu:
        return [(chunk * nu + i, None) for i in range(nu)]
    units = []
    for j in range(nq):
        for s in range(r):
            units.append((ATTN_BLOCK * j * r + s, ATTN_BLOCK * (j - 1) * r + s if j else None))
    return units


def _rows(start, r):
    return pl.ds(start, ATTN_BLOCK, stride=r) if r > 1 else pl.ds(start, ATTN_BLOCK)


def _attn_group_fwd(qkv, biasm, g, name, nu=ATTN_UNITS):
    T = qkv.shape[0]
    r = ATTN_CONFIGS[g][1]
    B, hd = ATTN_BLOCK, ATTN_HEAD_DIM
    nq, rows, ntiles = _attn_tile(T, r, nu)
    nchunks = max(1, r // nu)
    last_prev = B * (nq - 1) * r
    scale = hd ** -0.5
    tiles_per_tensor = 3 * HEADS_PER_GROUP * hd // LANES

    def body(q_ref, kc_ref, kp_ref, vc_ref, vp_ref, bias_ref, o_ref, lse_ref, s_ref, p_ref):
        n = pl.program_id(1)
        lane = lax.broadcasted_iota(jnp.int32, (1, LANES), 1)
        col = lax.broadcasted_iota(jnp.int32, (1, 2 * B), 1)
        masks = [lane < hd, lane >= hd]
        first_pen = jnp.where((col < B) & (n == 0), NEG_INF, 0.0)

        def chunk_body(chunk):
            units = _attn_units(r, nq, chunk, nu)

            def keys(cur_ref, prev_ref, cs, ps):
                prev = prev_ref[_rows(last_prev + (cs if r >= nu else cs % r), r), :] if ps is None else cur_ref[_rows(ps, r), :]
                return jnp.concatenate([prev, cur_ref[_rows(cs, r), :]], axis=0).astype(BF16)

            for u, (cs, ps) in enumerate(units):
                qv = q_ref[_rows(cs, r), :]
                kw = keys(kc_ref, kp_ref, cs, ps)
                for hh in range(2):
                    s_ref[2 * u + hh] = _dot(jnp.where(masks[hh], qv, 0.0).astype(BF16), kw, NT)
            for u, (cs, ps) in enumerate(units):
                lses = []
                for hh in range(2):
                    s = s_ref[2 * u + hh] * scale + bias_ref[hh]
                    if ps is None:
                        s = s + first_pen
                    m = jnp.max(s, axis=-1, keepdims=True)
                    p = jnp.exp(s - m)
                    l = jnp.sum(p, axis=-1, keepdims=True)
                    p_ref[2 * u + hh] = (p * (1.0 / l)).astype(BF16)
                    lses.append(m + jnp.log(l))
                lse_ref[_rows(cs, r), :] = jnp.where(masks[0], lses[0], lses[1])
            for u, (cs, ps) in enumerate(units):
                vw = keys(vc_ref, vp_ref, cs, ps)
                o_ref[_rows(cs, r), :] = (_dot(p_ref[2 * u], jnp.where(masks[0], vw, 0), NN)
                                          + _dot(p_ref[2 * u + 1], jnp.where(masks[1], vw, 0), NN))

        for chunk in range(nchunks):
            chunk_body(chunk)

    def cur(t):
        return pl.BlockSpec((rows, LANES), lambda hf, n: (n, t * tiles_per_tensor + 2 * g + hf))

    def prev(t):
        return pl.BlockSpec((rows, LANES), lambda hf, n: (jnp.maximum(n - 1, 0), t * tiles_per_tensor + 2 * g + hf))

    out = pl.BlockSpec((rows, LANES), lambda hf, n: (n, hf))
    sd = jax.ShapeDtypeStruct((T, 2 * LANES), F32)
    return pl.pallas_call(
        body, name=name, grid=(2, ntiles),
        in_specs=[cur(0), cur(1), prev(1), cur(2), prev(2), pl.BlockSpec((None, 2, B, 2 * B), lambda hf, n: (g, hf, 0, 0))],
        out_specs=[out, out], out_shape=[sd, sd],
        scratch_shapes=[pltpu.VMEM((2 * nu, B, 2 * B), F32), pltpu.VMEM((2 * nu, B, 2 * B), BF16)],
        compiler_params=_params("parallel", "parallel"),
    )(qkv, qkv, qkv, qkv, qkv, biasm)


def _attn_group_bwd(qkv, do, dvec, lse, biasm, g, dproj, dk_col, name, nu=ATTN_UNITS):
    T = qkv.shape[0]
    r = ATTN_CONFIGS[g][1]
    B, hd = ATTN_BLOCK, ATTN_HEAD_DIM
    nq, rows, ntiles = _attn_tile(T, r, nu)
    nchunks = max(1, r // nu)
    last_prev = B * (nq - 1) * r
    scale = hd ** -0.5
    tiles_per_tensor = 3 * HEADS_PER_GROUP * hd // LANES
    stash = r > SUBLANES

    def body(q_ref, kc_ref, kp_ref, vc_ref, vp_ref, do_ref, dv_ref, lse_ref, bias_ref, _,
             dq_ref, dk_ref, dvo_ref, dbias_ref, ck_ref, cv_ref, ak_ref, av_ref, s_ref, dp_ref, p_ref, ds_ref,
             qd_ref, dod_ref, kwd_ref):
        n = pl.program_id(1)
        lane = lax.broadcasted_iota(jnp.int32, (1, LANES), 1)
        col = lax.broadcasted_iota(jnp.int32, (1, 2 * B), 1)
        masks = [lane < hd, lane >= hd]
        first_pen = jnp.where((col < B) & (n == 0), NEG_INF, 0.0)

        @pl.when(n == 0)
        def _():
            dbias_ref[...] = jnp.zeros_like(dbias_ref)
            ck_ref[...] = jnp.zeros_like(ck_ref)
            cv_ref[...] = jnp.zeros_like(cv_ref)

        def chunk_body(chunk):
            units = _attn_units(r, nq, chunk, nu)

            def prev_rows(cs):
                return _rows(last_prev + (cs if r >= nu else cs % r), r)

            def keys(cur_ref, prev_ref, cs, ps):
                prev = prev_ref[prev_rows(cs), :] if ps is None else cur_ref[_rows(ps, r), :]
                return jnp.concatenate([prev, cur_ref[_rows(cs, r), :]], axis=0).astype(BF16)

            for u, (cs, ps) in enumerate(units):
                qv = q_ref[_rows(cs, r), :]
                dov = do_ref[_rows(cs, r), :]
                kw = keys(kc_ref, kp_ref, cs, ps)
                vw = keys(vc_ref, vp_ref, cs, ps)
                if stash:
                    qd_ref[u] = qv.astype(BF16)
                    dod_ref[u] = dov.astype(BF16)
                    kwd_ref[u] = kw
                for hh in range(2):
                    s_ref[2 * u + hh] = _dot(jnp.where(masks[hh], qv, 0.0).astype(BF16), kw, NT)
                    dp_ref[2 * u + hh] = _dot(jnp.where(masks[hh], dov, 0.0).astype(BF16), vw, NT)
            for u, (cs, ps) in enumerate(units):
                lse_t = lse_ref[_rows(cs, r), :]
                dv_t = dv_ref[_rows(cs, r), :]
                for hh in range(2):
                    lo = hh * hd
                    s = s_ref[2 * u + hh] * scale + bias_ref[hh]
                    if ps is None:
                        s = s + first_pen
                    p = jnp.exp(s - lse_t[:, lo:lo + 1])
                    ds = p * (dp_ref[2 * u + hh] + dv_t[:, lo:lo + 1])
                    dbias_ref[hh] += ds
                    p_ref[2 * u + hh] = p.astype(BF16)
                    ds_ref[2 * u + hh] = ds.astype(BF16)
            for u, (cs, ps) in enumerate(units):
                if stash:
                    qv, dov, kw = qd_ref[u], dod_ref[u], kwd_ref[u]
                else:
                    qv = q_ref[_rows(cs, r), :].astype(BF16)
                    dov = do_ref[_rows(cs, r), :].astype(BF16)
                    kw = keys(kc_ref, kp_ref, cs, ps)
                dq, dkw, dvw = 0.0, 0.0, 0.0
                for hh in range(2):
                    dsb = ds_ref[2 * u + hh]
                    dq = dq + _dot(dsb, jnp.where(masks[hh], kw, 0), NN)
                    dkw = dkw + _dot(dsb, jnp.where(masks[hh], qv, 0), TN)
                    dvw = dvw + _dot(p_ref[2 * u + hh], jnp.where(masks[hh], dov, 0), TN)
                dq_ref[_rows(cs, r), :] = dq * scale
                ak_ref[_rows(cs, r), :] = dkw[B:] * scale
                av_ref[_rows(cs, r), :] = dvw[B:]
                if ps is None:
                    ck_ref[prev_rows(cs), :] += dkw[:B] * scale
                    cv_ref[prev_rows(cs), :] += dvw[:B]
                else:
                    ak_ref[_rows(ps, r), :] += dkw[:B] * scale
                    av_ref[_rows(ps, r), :] += dvw[:B]

        @pl.when(n < ntiles)
        def _():
            for chunk in range(nchunks):
                chunk_body(chunk)

        dk_ref[...] = ck_ref[...].astype(BF16)
        dvo_ref[...] = cv_ref[...].astype(BF16)
        ck_ref[...] = ak_ref[...]
        cv_ref[...] = av_ref[...]

    last = ntiles - 1

    def cur(t):
        return pl.BlockSpec((rows, LANES), lambda hf, n: (jnp.minimum(n, last), t * tiles_per_tensor + 2 * g + hf))

    def prev(t):
        return pl.BlockSpec((rows, LANES), lambda hf, n: (jnp.clip(n - 1, 0, last), t * tiles_per_tensor + 2 * g + hf))

    nat = pl.BlockSpec((rows, LANES), lambda hf, n: (jnp.minimum(n, last), hf))
    nat_prev = pl.BlockSpec((rows, LANES), lambda hf, n: (jnp.clip(n - 1, 0, last), hf))
    tab = pl.BlockSpec((None, 2, B, 2 * B), lambda hf, n: (g, hf, 0, 0))
    dtab = pl.BlockSpec((2, B, 2 * B), lambda hf, n: (hf, 0, 0))
    sd = jax.ShapeDtypeStruct
    vm = pltpu.VMEM
    dk_tile = dk_col // LANES + 2 * g
    dk_spec = pl.BlockSpec((rows, LANES), lambda hf, n: (jnp.clip(n - 1, 0, last), dk_tile + hf))
    return pl.pallas_call(
        body, name=name, grid=(2, ntiles + 1),
        in_specs=[cur(0), cur(1), prev(1), cur(2), prev(2), nat, nat, nat, tab, pl.BlockSpec(memory_space=pl.ANY)],
        out_specs=[nat, dk_spec, nat_prev, dtab],
        out_shape=[sd((T, 2 * LANES), F32), sd(dproj.shape, BF16), sd((T, 2 * LANES), BF16),
                   sd((HEADS_PER_GROUP, B, 2 * B), F32)],
        scratch_shapes=[vm((rows, LANES), F32), vm((rows, LANES), F32), vm((rows, LANES), F32), vm((rows, LANES), F32),
                        vm((2 * nu, B, 2 * B), F32), vm((2 * nu, B, 2 * B), F32),
                        vm((2 * nu, B, 2 * B), BF16), vm((2 * nu, B, 2 * B), BF16),
                        vm((nu, B, LANES), BF16), vm((nu, B, LANES), BF16), vm((nu, 2 * B, LANES), BF16)],
        input_output_aliases={9: 1}, compiler_params=_params("parallel", "arbitrary"),
    )(qkv, qkv, qkv, qkv, qkv, do, dvec, lse, biasm, dproj)


def _attn_mix(os, lses, z, name):
    T, gw = os[0].shape
    C = z.shape[1]
    tm = _pick(T, 512, SUBLANES)

    def body(o0_ref, o1_ref, o2_ref, l0_ref, l1_ref, l2_ref, z_ref, out_ref):
        ls = [l0_ref[...], l1_ref[...], l2_ref[...]]
        mx = jnp.maximum(jnp.maximum(ls[0], ls[1]), ls[2])
        es = [jnp.exp(l - mx) for l in ls]
        inv = 1.0 / (es[0] + es[1] + es[2])
        for i, o_ref in enumerate((o0_ref, o1_ref, o2_ref)):
            sz, _ = _silu_and_grad(z_ref[:, i * gw:(i + 1) * gw].astype(F32))
            out_ref[:, i * gw:(i + 1) * gw] = (o_ref[...] * (es[i] * inv) * sz).astype(BF16)

    row = pl.BlockSpec((tm, C), lambda i: (i, 0))
    grp = pl.BlockSpec((tm, gw), lambda i: (i, 0))
    return pl.pallas_call(
        body, name=name, grid=(T // tm,), in_specs=[grp] * 6 + [row], out_specs=row,
        out_shape=jax.ShapeDtypeStruct((T, C), BF16), compiler_params=_params("parallel"),
    )(*os, *lses, z)


def _attn_mix_bwd(dout, os, lses, z, dproj, col, name):
    T, gw = os[0].shape
    C = z.shape[1]
    tm = _pick(T, 512, SUBLANES)
    head_of = np.arange(gw) // ATTN_HEAD_DIM
    ones = jnp.asarray(head_of[:, None] == head_of[None, :], BF16)

    def body(dout_ref, o0_ref, o1_ref, o2_ref, l0_ref, l1_ref, l2_ref, z_ref, ones_ref, _,
             dz_ref, do0_ref, do1_ref, do2_ref, dv0_ref, dv1_ref, dv2_ref):
        ls = [l0_ref[...], l1_ref[...], l2_ref[...]]
        mx = jnp.maximum(jnp.maximum(ls[0], ls[1]), ls[2])
        es = [jnp.exp(l - mx) for l in ls]
        inv = 1.0 / (es[0] + es[1] + es[2])
        alphas, ebar = [], 0.0
        for i, (o_ref, do_ref) in enumerate(((o0_ref, do0_ref), (o1_ref, do1_ref), (o2_ref, do2_ref))):
            sl = slice(i * gw, (i + 1) * gw)
            alpha = es[i] * inv
            ov = o_ref[...]
            dv = dout_ref[:, sl]
            sz, dsz = _silu_and_grad(z_ref[:, sl].astype(F32))
            dz_ref[:, sl] = (dv * ov * alpha * dsz).astype(BF16)
            da = dv * sz
            do_ref[...] = da * alpha
            t = da * ov
            t1 = t.astype(BF16)
            r1 = t - t1.astype(F32)
            t2 = r1.astype(BF16)
            t3 = (r1 - t2.astype(F32)).astype(BF16)
            e = _dot(t1, ones_ref[...], NN) + _dot(t2, ones_ref[...], NN) + _dot(t3, ones_ref[...], NN)
            ebar = ebar + alpha * e
            alphas.append(alpha)
        for alpha, dv_ref in zip(alphas, (dv0_ref, dv1_ref, dv2_ref)):
            dv_ref[...] = -alpha * ebar

    row = pl.BlockSpec((tm, C), lambda i: (i, 0))
    grp = pl.BlockSpec((tm, gw), lambda i: (i, 0))
    sd = jax.ShapeDtypeStruct
    res = pl.pallas_call(
        body, name=name, grid=(T // tm,),
        in_specs=[row] + [grp] * 6 + [row, pl.BlockSpec((gw, gw), lambda i: (0, 0)), pl.BlockSpec(memory_space=pl.ANY)],
        out_specs=[pl.BlockSpec((tm, C), lambda i: (i, col // C))] + [grp] * 6,
        out_shape=[sd(dproj.shape, BF16)] + [sd((T, gw), F32)] * 6, input_output_aliases={9: 0},
        compiler_params=_params("parallel"),
    )(dout, *os, *lses, z, ones, dproj)
    return res[0], res[1:4], res[4:7]


def _mem_attn(qz, kv, name):
    T = qz.shape[0]
    dm = qz.shape[1] // 2
    M = kv.shape[0]
    hd = dm // MEM_HEADS
    scale = hd ** -0.5
    tm = _pick(T, 512, SUBLANES)

    def body(q_ref, z_ref, k_ref, v_ref, o_ref, s_ref, p_ref):
        heads = [slice(h * hd, (h + 1) * hd) for h in range(MEM_HEADS)]
        for h, sl in enumerate(heads):
            s_ref[h] = _dot(q_ref[:, sl].astype(BF16), k_ref[:, sl], NT)
        for h, sl in enumerate(heads):
            s = s_ref[h] * scale
            p = jnp.exp(s - jnp.max(s, axis=-1, keepdims=True))
            p_ref[h] = (p * (1.0 / jnp.sum(p, axis=-1, keepdims=True))).astype(BF16)
        for h, sl in enumerate(heads):
            sz, _ = _silu_and_grad(z_ref[:, sl].astype(F32))
            o_ref[:, sl] = (_dot(p_ref[h], v_ref[:, sl], NN) * sz).astype(BF16)

    return pl.pallas_call(
        body, name=name, grid=(T // tm,),
        in_specs=[pl.BlockSpec((tm, dm), lambda i: (i, 0)), pl.BlockSpec((tm, dm), lambda i: (i, 1)),
                  pl.BlockSpec((M, dm), lambda i: (0, 0)), pl.BlockSpec((M, dm), lambda i: (0, 1))],
        out_specs=pl.BlockSpec((tm, dm), lambda i: (i, 0)),
        out_shape=jax.ShapeDtypeStruct((T, dm), BF16),
        scratch_shapes=[pltpu.VMEM((MEM_HEADS, tm, M), F32), pltpu.VMEM((MEM_HEADS, tm, M), BF16)],
        compiler_params=_params("parallel"),
    )(qz, qz, kv, kv)


def _mem_attn_bwd(do, qz, kv, dproj, col, name):
    T = qz.shape[0]
    dm = qz.shape[1] // 2
    M = kv.shape[0]
    hd = dm // MEM_HEADS
    scale = hd ** -0.5
    tm = _pick(T, 512, SUBLANES)

    def body(do_ref, q_ref, z_ref, k_ref, v_ref, _, dq_ref, dz_ref, dk_ref, dv_ref, s_ref, dp_ref, p_ref, ds_ref, dob_ref):
        @pl.when(pl.program_id(0) == 0)
        def _():
            dk_ref[...] = jnp.zeros_like(dk_ref)
            dv_ref[...] = jnp.zeros_like(dv_ref)

        heads = [slice(h * hd, (h + 1) * hd) for h in range(MEM_HEADS)]
        for h, sl in enumerate(heads):
            sz, _ = _silu_and_grad(z_ref[:, sl].astype(F32))
            dob = (do_ref[:, sl] * sz).astype(BF16)
            dob_ref[:, sl] = dob
            s_ref[h] = _dot(q_ref[:, sl].astype(BF16), k_ref[:, sl], NT)
            dp_ref[h] = _dot(dob, v_ref[:, sl], NT)
        for h, sl in enumerate(heads):
            s = s_ref[h] * scale
            p = jnp.exp(s - jnp.max(s, axis=-1, keepdims=True))
            pn = p * (1.0 / jnp.sum(p, axis=-1, keepdims=True))
            dp = dp_ref[h]
            p_ref[h] = pn.astype(BF16)
            ds_ref[h] = (pn * (dp - jnp.sum(dp * pn, axis=-1, keepdims=True))).astype(BF16)
        for h, sl in enumerate(heads):
            _, dsz = _silu_and_grad(z_ref[:, sl].astype(F32))
            dz_ref[:, sl] = (do_ref[:, sl] * _dot(p_ref[h], v_ref[:, sl], NN) * dsz).astype(BF16)
            dq_ref[:, sl] = (_dot(ds_ref[h], k_ref[:, sl], NN) * scale).astype(BF16)
            dk_ref[:, sl] += _dot(ds_ref[h], q_ref[:, sl].astype(BF16), TN) * scale
            dv_ref[:, sl] += _dot(p_ref[h], dob_ref[:, sl], TN)

    rowq = pl.BlockSpec((tm, dm), lambda i: (i, 0))
    rowz = pl.BlockSpec((tm, dm), lambda i: (i, 1))
    kb = pl.BlockSpec((M, dm), lambda i: (0, 0))
    vb = pl.BlockSpec((M, dm), lambda i: (0, 1))
    sd = jax.ShapeDtypeStruct
    dq, dz, dk, dv = pl.pallas_call(
        body, name=name, grid=(T // tm,), in_specs=[rowq, rowq, rowz, kb, vb, pl.BlockSpec(memory_space=pl.ANY)],
        out_specs=[pl.BlockSpec((tm, dm), lambda i: (i, col // dm)), rowq, kb, kb],
        out_shape=[sd(dproj.shape, BF16), sd((T, dm), BF16), sd((M, dm), F32), sd((M, dm), F32)],
        scratch_shapes=[pltpu.VMEM((MEM_HEADS, tm, M), F32), pltpu.VMEM((MEM_HEADS, tm, M), F32),
                        pltpu.VMEM((MEM_HEADS, tm, M), BF16), pltpu.VMEM((MEM_HEADS, tm, M), BF16), pltpu.VMEM((tm, dm), BF16)],
        input_output_aliases={5: 0}, compiler_params=_params("arbitrary"),
    )(do, qz, qz, kv, kv, dproj)
    return dq, dz, dk, dv


def _merge(os, ws, L, logits, b_gate, name):
    T = os[0].shape[0]
    D = ws[0].shape[2]
    tm = _pick(T, 512, SUBLANES)

    def body(o0_ref, o1_ref, o2_ref, w0_ref, w1_ref, w2_ref, l_ref, b_ref, m_ref, p_ref):
        acc = 0.0
        for i, (o_ref, w_ref) in enumerate(((o0_ref, w0_ref), (o1_ref, w1_ref), (o2_ref, w2_ref))):
            sl = slice(i * D, (i + 1) * D)
            bp = _dot(o_ref[...], w_ref[...], NN)
            p_ref[i] = bp.astype(BF16)
            acc = acc + _sigmoid(l_ref[:, sl].astype(F32) + b_ref[:, sl]) * bp
        m_ref[...] = acc.astype(BF16)

    return pl.pallas_call(
        body, name=name, grid=(T // tm,),
        in_specs=[pl.BlockSpec((tm, o.shape[1]), lambda i: (i, 0)) for o in os]
        + [pl.BlockSpec((None,) + w.shape[1:], lambda i: (L, 0, 0)) for w in ws]
        + [pl.BlockSpec((tm, 3 * D), lambda i: (i, 0)), pl.BlockSpec((1, 3 * D), lambda i: (0, 0))],
        out_specs=[pl.BlockSpec((tm, D), lambda i: (i, 0)), pl.BlockSpec((3, tm, D), lambda i: (0, i, 0))],
        out_shape=[jax.ShapeDtypeStruct((T, D), BF16), jax.ShapeDtypeStruct((3, T, D), BF16)],
        compiler_params=_params("parallel"),
    )(*os, *ws, logits, b_gate.reshape(1, 3 * D))


def _merge_bwd(dmerged, bps, logits, b_gate, dproj_cols, dl_off, name):
    _, T, D = bps.shape
    tm = _pick(T, 2048, SUBLANES)
    cw = _pick(math.gcd(dl_off, D), 512, LANES)
    per = D // cw

    def body(dm_ref, p_ref, l_ref, b_ref, dl_ref, d_ref, db_ref):
        @pl.when(pl.program_id(1) == 0)
        def _():
            db_ref[...] = jnp.zeros_like(db_ref)

        dmv = dm_ref[...]
        gt = _sigmoid(l_ref[...].astype(F32) + b_ref[...])
        d_ref[...] = (dmv * gt).astype(BF16)
        dl = dmv * p_ref[...].astype(F32) * gt * (1.0 - gt)
        dl_ref[...] = dl.astype(BF16)
        db_ref[...] += jnp.sum(dl, axis=0, keepdims=True)

    stacked = pl.BlockSpec((None, tm, cw), lambda j, i: (j // per, i, j % per))
    sd = jax.ShapeDtypeStruct
    return pl.pallas_call(
        body, name=name, grid=(3 * per, T // tm),
        in_specs=[pl.BlockSpec((tm, cw), lambda j, i: (i, j % per)), stacked, pl.BlockSpec((tm, cw), lambda j, i: (i, j)),
                  pl.BlockSpec((1, cw), lambda j, i: (0, j))],
        out_specs=[pl.BlockSpec((tm, cw), lambda j, i: (i, dl_off // cw + j)), stacked, pl.BlockSpec((1, cw), lambda j, i: (0, j))],
        out_shape=[sd((T, dproj_cols), BF16), sd((3, T, D), BF16), sd((1, 3 * D), F32)],
        compiler_params=_params("parallel", "arbitrary"),
    )(dmerged, bps, logits, b_gate.reshape(1, 3 * D))


def _block_diag(w):
    nblk, ng, a, b = w.shape
    eye = jnp.eye(ng, dtype=w.dtype)
    return (w[:, :, :, None, :] * eye[None, :, None, :, None]).reshape(nblk, ng * a, ng * b)


def _block_diag_part(m, a, b):
    nblk = m.shape[0]
    ng = m.shape[1] // a
    m5 = m.reshape(nblk, ng, a, ng, b)
    eye = jnp.eye(ng, dtype=m.dtype)
    return jnp.sum(m5 * eye[None, :, None, :, None], axis=3)


def _ssm_matrices(p, L, tag):
    G, P = p["ssm_lambda_re"].shape[1:]
    Hg = SSM_GROUP
    gpb = SSM_BLOCK_CH // Hg
    nblk = G // gpb
    br = p["ssm_b_re"][L].transpose(2, 0, 1)
    bi = p["ssm_b_im"][L].transpose(2, 0, 1)
    disc_in = (p["ssm_lambda_re"][L], p["ssm_lambda_im"][L], p["ssm_log_dt"][L].reshape(G, 1), br, bi)
    ar, ai, bbr, bbi = _ssm_disc(*disc_in, name=f"ssm_disc_{tag}")
    amat = (ar.reshape(nblk // 2, SUBLANES, LANES), ai.reshape(nblk // 2, SUBLANES, LANES))
    bbr_g = bbr.transpose(1, 0, 2).reshape(nblk, gpb, Hg, P)
    bbi_g = bbi.transpose(1, 0, 2).reshape(nblk, gpb, Hg, P)
    bmat = jnp.concatenate([_block_diag(bbr_g), _block_diag(bbi_g)], axis=2).astype(BF16)
    cre = p["ssm_c_re"][L].reshape(nblk, gpb, Hg, P).transpose(0, 1, 3, 2)
    cim = p["ssm_c_im"][L].reshape(nblk, gpb, Hg, P).transpose(0, 1, 3, 2)
    cmat = jnp.concatenate([_block_diag(cre), -_block_diag(cim)], axis=1).astype(BF16)
    return disc_in, amat, bmat, cmat


def _layer_fwd(x, mem, p, wb, L, biasm):
    T, D = x.shape
    C = p["ssm_d"].shape[1]
    dm = wb["w_br_mem"].shape[1]
    tag = f"l{L}"
    s = {"x": x}
    h = _rmsnorm(x, p["norm_g"][L], f"norm_{tag}")
    offs = [int(o) for o in np.cumsum([0, C, C, 3 * 768, 768, 2 * dm, 3 * D])]
    names = ("uz", "z_ssm", "qkv", "z_attn", "qz_mem", "logits")
    dts = (F32, BF16, F32, BF16, BF16, BF16)
    for i, (nm, dt) in enumerate(zip(names, dts)):
        wide_bf16 = offs[i + 1] - offs[i] >= 1024 and dt == BF16
        tiles = dict(tm=2048) if wide_bf16 else {}
        s[nm] = _matmul(h, wb["w_in"], mode="nn", name=f"in_{nm}_{tag}", out_dtype=dt, b_lead=L, b_off=offs[i],
                        n_cols=offs[i + 1] - offs[i], **tiles)
    s["h"] = h

    disc_in, amat, bmat, cmat = _ssm_matrices(p, L, tag)
    dvec = p["ssm_d"][L].reshape(1, C)
    y, xr, xi = _ssm_fwd(s["uz"], bmat, cmat, *amat, dvec, f"ssm_scan_{tag}")
    o_ssm, a_glu = _ssm_post(y, s["z_ssm"], wb["w_glu"][L], p["b_glu"][L], f"ssm_post_{tag}")
    s.update(disc_in=disc_in, amat=amat, bmat=bmat, cmat=cmat, xr=xr, xi=xi, y=y, a_glu=a_glu, o_ssm=o_ssm)

    groups = [_attn_group_fwd(s["qkv"], biasm, g, f"attn_g{g}_{tag}", nu=_attn_units_for(g, True))
              for g in range(len(ATTN_CONFIGS))]
    os, lses = [o for o, _ in groups], [l for _, l in groups]
    o_attn = _attn_mix(os, lses, s["z_attn"], f"attn_mix_{tag}")
    s.update(os=os, lses=lses, o_attn=o_attn)

    mn = _rmsnorm(mem, p["mem_norm_g"][L], f"mem_norm_{tag}")
    kv = _matmul(mn, wb["w_mem_kv"], mode="nn", name=f"mem_kv_{tag}", out_dtype=BF16, b_lead=L)
    o_mem = _mem_attn(s["qz_mem"], kv, f"mem_attn_{tag}")
    s.update(mn=mn, kv=kv, o_mem=o_mem)

    merged, bps = _merge([o_ssm, o_attn, o_mem], [wb["w_br_ssm"], wb["w_br_attn"], wb["w_br_mem"]], L, s["logits"],
                         p["b_gate"][L], f"merge_{tag}")
    s.update(bps=bps, merged=merged)
    x_new = _matmul(merged, wb["w_out"], mode="nn", name=f"out_{tag}", add=x, b_lead=L)
    return x_new, s


def _layer_bwd(dx, mem, p, wb, L, s, biasm, gprev):
    T, D = dx.shape
    C = p["ssm_d"].shape[1]
    depth = p["norm_g"].shape[0]
    tag = f"l{L}"
    g = {}

    def wgrad(n, a, b, **tiles):
        g[n] = _matmul(a, b, mode="tn", name=f"d{n}_{tag}", out_dtype=BF16, stack=(L, depth, gprev.get(n)), **tiles)

    dmerged = _matmul(dx, wb["w_out"], mode="nt", name=f"d_merged_{tag}", b_lead=L)
    wgrad("w_out", s["merged"], dx)
    dm = s["qz_mem"].shape[1] // 2
    col = dict(zip(("u", "z_ssm", "q", "k", "v", "z_attn", "q_mem", "z_mem", "logits", "end"),
                   (int(o) for o in np.cumsum([0, C, C, 768, 768, 768, 768, dm, dm, 3 * D]))))
    dproj, dbps, g["b_gate"] = _merge_bwd(dmerged, s["bps"], s["logits"], p["b_gate"][L], col["end"], col["logits"],
                                          f"merge_bwd_{tag}")
    dos = []
    for i, (o, n) in enumerate(((s["o_ssm"], "w_br_ssm"), (s["o_attn"], "w_br_attn"), (s["o_mem"], "w_br_mem"))):
        dos.append(_matmul(dbps, wb[n], mode="nt", name=f"d_o_{n}_{tag}", a_lead=i, b_lead=L))
        g[n] = _matmul(o, dbps, mode="tn", name=f"d{n}_{tag}", out_dtype=BF16, b_lead=i, stack=(L, depth, gprev.get(n)))

    dy, dproj, ds_glu, g["b_glu"] = _ssm_post_bwd(dos[0], s["y"], s["z_ssm"], wb["w_glu"][L], p["b_glu"][L], dproj,
                                                  col["z_ssm"], f"ssm_post_bwd_{tag}")
    wgrad("w_glu", s["a_glu"], ds_glu)
    dvec = p["ssm_d"][L].reshape(1, C)
    dproj, dbm, dct, dar, dai, g["ssm_d"] = _ssm_bwd(dy, s["uz"], s["xr"], s["xi"], s["bmat"], s["cmat"], *s["amat"], dvec,
                                                     dproj, f"ssm_scan_bwd_{tag}")
    G, P = p["ssm_lambda_re"].shape[1:]
    Hg = SSM_GROUP
    half = dbm.shape[2] // 2
    dbbr = _block_diag_part(dbm[:, :, :half], Hg, P).reshape(G, Hg, P).transpose(1, 0, 2)
    dbbi = _block_diag_part(dbm[:, :, half:], Hg, P).reshape(G, Hg, P).transpose(1, 0, 2)
    g["ssm_c_re"] = _block_diag_part(dct[:, :, :half], Hg, P).reshape(G, Hg, P)
    g["ssm_c_im"] = -_block_diag_part(dct[:, :, half:], Hg, P).reshape(G, Hg, P)
    glre, glim, gdt, gbr, gbi = _ssm_disc_bwd(*s["disc_in"], dar.reshape(G, P), dai.reshape(G, P), dbbr, dbbi,
                                              name=f"ssm_disc_bwd_{tag}")
    g["ssm_lambda_re"], g["ssm_lambda_im"], g["ssm_log_dt"] = glre, glim, gdt.reshape(G)
    g["ssm_b_re"] = gbr.transpose(1, 2, 0)
    g["ssm_b_im"] = gbi.transpose(1, 2, 0)

    dproj, do_g, dvec_g = _attn_mix_bwd(dos[1], s["os"], s["lses"], s["z_attn"], dproj, col["z_attn"], f"attn_mix_bwd_{tag}")
    rest, dbias = [], []
    for gi in range(len(ATTN_CONFIGS)):
        dq_g, dproj, dv_g, db_g = _attn_group_bwd(s["qkv"], do_g[gi], dvec_g[gi], s["lses"][gi], biasm, gi, dproj, col["k"],
                                                  f"attn_bwd_g{gi}_{tag}", nu=_attn_units_for(gi, False))
        gw = dq_g.shape[1]
        rest += [(dq_g, col["q"] + gi * gw), (dv_g, col["v"] + gi * gw)]
        dbias.append(db_g)
    dbias = jnp.stack(dbias)

    dproj, dz_mem, dk_mem, dv_mem = _mem_attn_bwd(dos[2], s["qz_mem"], s["kv"], dproj, col["q_mem"], f"mem_attn_bwd_{tag}")
    rest.append((dz_mem, col["z_mem"]))
    for piece, at in rest:
        dproj = lax.dynamic_update_slice(dproj, piece.astype(BF16), (0, at))
    dkv = jnp.concatenate([dk_mem, dv_mem], axis=1)
    wgrad("w_mem_kv", s["mn"], dkv)
    dmn = _matmul(dkv, wb["w_mem_kv"], mode="nt", name=f"d_mn_{tag}", b_lead=L)
    _, g["mem_norm_g"] = _rmsnorm_bwd(mem, p["mem_norm_g"][L], dmn, None, f"mem_norm_bwd_{tag}")

    dh = _matmul(dproj, wb["w_in"], mode="nt", name=f"d_h_{tag}", b_lead=L, tm=512, tn=1024)
    wgrad("w_in", s["h"], dproj, tn=2304, tk=1024)
    dx_in, g["norm_g"] = _rmsnorm_bwd(s["x"], p["norm_g"][L], dh, dx, f"norm_bwd_{tag}")
    return dx_in, g, dbias


def _bucket_onehot(gi):
    buckets, bands = _band_tables()
    hit = (buckets[gi].reshape(1, -1) == jnp.arange(NUM_BUCKETS)[:, None]) & bands[gi].reshape(1, -1)
    return hit.astype(BF16)


def _bias_tables(rel_bias, name):
    _, bands = _band_tables()
    out = []
    for gi in range(len(ATTN_CONFIGS)):
        tab = rel_bias[:, gi * HEADS_PER_GROUP:(gi + 1) * HEADS_PER_GROUP].T
        flat = _matmul(tab, _bucket_onehot(gi), mode="nn", name=f"{name}_{gi}", split_a=3, tn=4096)
        out.append(jnp.where(bands[gi][None], flat.reshape(HEADS_PER_GROUP, ATTN_BLOCK, 2 * ATTN_BLOCK), NEG_INF))
    return jnp.stack(out)


def _rel_bias_grad(dbias_sum, name):
    cols = []
    for gi in range(len(ATTN_CONFIGS)):
        flat = dbias_sum[gi].reshape(HEADS_PER_GROUP, -1)
        cols.append(_matmul(flat, _bucket_onehot(gi), mode="nt", name=f"{name}_{gi}", split_a=2, tk=4096).T)
    return jnp.concatenate(cols, axis=1)


def _local_step(x, mem, target, p, wb):
    depth = p["norm_g"].shape[0]
    biasm = _bias_tables(p["rel_bias"], "bias_table")
    saved = []
    for L in range(depth):
        x, s = _layer_fwd(x, mem, p, wb, L, biasm)
        saved.append(s)
    loss_vec, dx, dgf = _loss_head(x, p["final_norm_g"], target, "loss_head")
    grads = {"final_norm_g": dgf.reshape(-1)}
    per_layer = [None] * depth
    dbias_sum = 0.0
    stacked = {}
    for L in reversed(range(depth)):
        dx, per_layer[L], dbias = _layer_bwd(dx, mem, p, wb, L, saved[L], biasm, stacked)
        stacked = {n: per_layer[L][n] for n, _ in BIG}
        dbias_sum = dbias_sum + dbias
    grads.update(stacked)
    for n in per_layer[0]:
        if n not in stacked:
            grads[n] = jnp.stack([per_layer[L][n].reshape(p[n].shape[1:]) for L in range(depth)])
    grads["rel_bias"] = _rel_bias_grad(dbias_sum, "d_rel_bias")
    return jnp.sum(loss_vec), dx, grads


def _chip_coords(j):
    return j // 2, j % 2


def _place_shard(shard, ax, chip, name):
    _, a, b = shard.shape
    ra = _pick(a, 256, 16)
    full = (2, a * N_CHIPS, b) if ax == 1 else (2, a, b * N_CHIPS)
    per = a // ra

    def body(j_ref, s_ref, o_ref):
        o_ref[...] = s_ref[...].astype(BF16)

    out_idx = (lambda l, i, j: (l, j[0] * per + i, 0)) if ax == 1 else (lambda l, i, j: (l, i, j[0]))
    return pl.pallas_call(
        body, name=name,
        grid_spec=pltpu.PrefetchScalarGridSpec(
            num_scalar_prefetch=1, grid=(2, per),
            in_specs=[pl.BlockSpec((None, ra, b), lambda l, i, j: (l, i, 0))],
            out_specs=pl.BlockSpec((None, ra, b), out_idx)),
        out_shape=jax.ShapeDtypeStruct(full, BF16), compiler_params=_params("parallel", "parallel"),
    )(chip, shard)


def _gather_shards(fulls, axes, name):
    n = len(fulls)
    widths = [a.shape[ax] // N_CHIPS for a, ax in zip(fulls, axes)]
    aligns = [LANES if ax == 2 else 16 for ax in axes]

    def body(*refs):
        outs = refs[n:2 * n]
        send_sems, recv_sems, fsend_sems, frecv_sems = refs[2 * n:]
        x, y, c = lax.axis_index("x"), lax.axis_index("y"), lax.axis_index("c")
        mine = 2 * x + y
        sibling = (x, y, 1 - c)

        def window(t, layer, j):
            start = pl.ds(pl.multiple_of(j * widths[t], aligns[t]), widths[t])
            return outs[t].at[(layer, start, slice(None)) if axes[t] == 1 else (layer, slice(None), start)]

        def over_ici(t, j, block):
            return pltpu.make_async_remote_copy(
                src_ref=window(t, c, mine), dst_ref=window(t, c, block), send_sem=send_sems.at[t, j],
                recv_sem=recv_sems.at[t, block], device_id=(*_chip_coords(j), c), device_id_type=MESH)

        def over_d2d(t, j, layer):
            return pltpu.make_async_remote_copy(
                src_ref=window(t, layer, j), dst_ref=window(t, layer, j), send_sem=fsend_sems.at[t, j],
                recv_sem=frecv_sems.at[t, j], device_id=sibling, device_id_type=MESH)

        for t in range(n):
            for j in range(N_CHIPS):
                @pl.when(j != mine)
                def _():
                    over_ici(t, j, mine).start()
        for t in range(n):
            for j in range(N_CHIPS):
                @pl.when(j != mine)
                def _():
                    over_ici(t, j, j).wait_recv()
                    over_d2d(t, j, c).start()
        for t in range(n):
            for j in range(N_CHIPS):
                @pl.when(j != mine)
                def _():
                    over_ici(t, j, mine).wait_send()
                    over_d2d(t, j, c).wait_send()
                    over_d2d(t, j, 1 - c).wait_recv()

    sem = pltpu.SemaphoreType.DMA
    return pl.pallas_call(
        body, name=name, in_specs=[HBM] * n, out_specs=[HBM] * n,
        out_shape=[jax.ShapeDtypeStruct(a.shape, a.dtype) for a in fulls],
        input_output_aliases={t: t for t in range(n)},
        scratch_shapes=[sem((n, N_CHIPS)), sem((n, N_CHIPS)), sem((n, N_CHIPS)), sem((n, N_CHIPS))],
    )(*fulls)


def _scatter_slices(arrays, axes, name):
    n = len(arrays)

    def piece(a, ax):
        if ax is None:
            return a.shape, None
        w = a.shape[ax] // N_CHIPS
        return a.shape[:ax] + (w,) + a.shape[ax + 1:], w

    shapes = [piece(a, ax) for a, ax in zip(arrays, axes)]

    def body(*refs):
        ins, outs = refs[:n], refs[n:2 * n]
        send_sems, recv_sems, loc_sems = refs[2 * n:]
        x, y, c = lax.axis_index("x"), lax.axis_index("y"), lax.axis_index("c")
        mine = 2 * x + y

        def src(t, j):
            ax, w = axes[t], shapes[t][1]
            if ax is None:
                return ins[t]
            idx = tuple(pl.ds(j * w, w) if d == ax else slice(None) for d in range(len(arrays[t].shape)))
            return ins[t].at[idx]

        for t in range(n):
            for j in range(N_CHIPS):
                @pl.when(j == mine)
                def _():
                    pltpu.make_async_copy(src(t, j), outs[t].at[j], loc_sems.at[t]).start()

                @pl.when(j != mine)
                def _():
                    pltpu.make_async_remote_copy(
                        src_ref=src(t, j), dst_ref=outs[t].at[mine], send_sem=send_sems.at[t, j], recv_sem=recv_sems.at[t, mine],
                        device_id=(*_chip_coords(j), c), device_id_type=MESH).start()
        for t in range(n):
            for j in range(N_CHIPS):
                @pl.when(j == mine)
                def _():
                    pltpu.make_async_copy(src(t, j), outs[t].at[j], loc_sems.at[t]).wait()

                @pl.when(j != mine)
                def _():
                    cp = pltpu.make_async_remote_copy(
                        src_ref=src(t, j), dst_ref=outs[t].at[j], send_sem=send_sems.at[t, j], recv_sem=recv_sems.at[t, j],
                        device_id=(*_chip_coords(j), c), device_id_type=MESH)
                    cp.wait_send()
                    cp.wait_recv()

    return pl.pallas_call(
        body, name=name, in_specs=[HBM] * n, out_specs=[HBM] * n,
        out_shape=[jax.ShapeDtypeStruct((N_CHIPS,) + sh, a.dtype) for a, (sh, _) in zip(arrays, shapes)],
        scratch_shapes=[pltpu.SemaphoreType.DMA((n, N_CHIPS)), pltpu.SemaphoreType.DMA((n, N_CHIPS)), pltpu.SemaphoreType.DMA((n,))],
    )(*arrays)


def _swap_layers(stacked, name):
    n = len(stacked)

    def body(*refs):
        ins, outs = refs[:n], refs[n:2 * n]
        send_sems, recv_sems = refs[2 * n:]
        c = lax.axis_index("c")
        peer = (lax.axis_index("x"), lax.axis_index("y"), 1 - c)
        cps = [pltpu.make_async_remote_copy(src_ref=ins[t].at[1 - c], dst_ref=outs[t], send_sem=send_sems.at[t],
                                            recv_sem=recv_sems.at[t], device_id=peer, device_id_type=MESH) for t in range(n)]
        for cp in cps:
            cp.start()
        for cp in cps:
            cp.wait_send()
            cp.wait_recv()

    return pl.pallas_call(
        body, name=name, in_specs=[HBM] * n, out_specs=[HBM] * n,
        out_shape=[jax.ShapeDtypeStruct(a.shape[1:], a.dtype) for a in stacked],
        scratch_shapes=[pltpu.SemaphoreType.DMA((n,)), pltpu.SemaphoreType.DMA((n,))],
    )(*stacked)


def _merge_layers(stacked, name):
    n = len(stacked)

    def body(*refs):
        outs = refs[n:2 * n]
        send_sems, recv_sems = refs[2 * n:]
        c = lax.axis_index("c")
        peer = (lax.axis_index("x"), lax.axis_index("y"), 1 - c)
        for t in range(n):
            pltpu.make_async_remote_copy(src_ref=outs[t].at[c], dst_ref=outs[t].at[c], send_sem=send_sems.at[t],
                                         recv_sem=recv_sems.at[t], device_id=peer, device_id_type=MESH).start()
        for t in range(n):
            cp = pltpu.make_async_remote_copy(src_ref=outs[t].at[c], dst_ref=outs[t].at[1 - c], send_sem=send_sems.at[t],
                                              recv_sem=recv_sems.at[t], device_id=peer, device_id_type=MESH)
            cp.wait_send()
            cp.wait_recv()

    sem = pltpu.SemaphoreType.DMA
    return pl.pallas_call(
        body, name=name, in_specs=[HBM] * n, out_specs=[HBM] * n,
        out_shape=[jax.ShapeDtypeStruct(a.shape, a.dtype) for a in stacked],
        input_output_aliases={t: t for t in range(n)}, scratch_shapes=[sem((n,)), sem((n,))],
    )(*stacked)


def _pair_sum(stacked, landed, core, name):
    _, K, N = stacked.shape
    tr = _pick(K, max(16, (1 << 19) // N // 16 * 16), 16)

    def body(c_ref, s_ref, l_ref, o_ref):
        o_ref[...] = (s_ref[...].astype(F32) + l_ref[...].astype(F32)).astype(o_ref.dtype)

    return pl.pallas_call(
        body, name=name,
        grid_spec=pltpu.PrefetchScalarGridSpec(
            num_scalar_prefetch=1, grid=(K // tr,),
            in_specs=[pl.BlockSpec((None, tr, N), lambda i, c: (c[0], i, 0)), pl.BlockSpec((tr, N), lambda i, c: (i, 0))],
            out_specs=pl.BlockSpec((tr, N), lambda i, c: (i, 0))),
        out_shape=jax.ShapeDtypeStruct((K, N), stacked.dtype), compiler_params=_params("parallel"),
    )(core, stacked, landed)


def _sum_chips(landed, core, name):
    _, R, C = landed.shape
    tr = _pick(R, max(SUBLANES, (1 << 19) // C // 16 * 16), 16)

    def body(c_ref, l_ref, o_ref):
        acc = l_ref[0].astype(F32) + l_ref[1].astype(F32)
        acc = acc + l_ref[2].astype(F32)
        o_ref[...] = acc + l_ref[3].astype(F32)

    return pl.pallas_call(
        body, name=name,
        grid_spec=pltpu.PrefetchScalarGridSpec(
            num_scalar_prefetch=1, grid=(R // tr,),
            in_specs=[pl.BlockSpec((N_CHIPS, tr, C), lambda i, c: (0, i, 0))],
            out_specs=pl.BlockSpec((None, tr, C), lambda i, c: (c[0], i, 0))),
        out_shape=jax.ShapeDtypeStruct((2, R, C), F32), compiler_params=_params("parallel"),
    )(core, landed)


def _adamw_math(w_ref, g_ref, m_ref, v_ref, d_ref, nm_ref, nv_ref):
    c1 = 1.0 / (1.0 - ADAM_B1 ** ADAM_STEP)
    c2 = 1.0 / (1.0 - ADAM_B2 ** ADAM_STEP)
    g = g_ref[...]
    nm = ADAM_B1 * m_ref[...] + (1.0 - ADAM_B1) * g
    nv = ADAM_B2 * v_ref[...] + (1.0 - ADAM_B2) * (g * g)
    nm_ref[...] = nm
    nv_ref[...] = nv
    d_ref[...] = -ADAM_LR * ((nm * c1) / (jnp.sqrt(nv * c2) + ADAM_EPS) + ADAM_WD * w_ref[...])


def _adamw_whole(w, g, m, v, name):
    shape = w.shape
    view = (-1,) + shape[-2:] if w.ndim >= 2 else (1, 1, -1)

    def body(*refs):
        _adamw_math(*refs)

    res = pl.pallas_call(body, name=name, out_shape=[jax.ShapeDtypeStruct(w.reshape(view).shape, F32)] * 3,
                         compiler_params=pltpu.CompilerParams(vmem_limit_bytes=VMEM_LIMIT_BYTES))(
        *(a.reshape(view) for a in (w, g, m, v)))
    return [r.reshape(shape) for r in res]


def _adamw(w, g, m, v, name):
    R, C = w.shape
    tr = _pick(R, max(SUBLANES, (1 << 18) // C // 8 * 8), SUBLANES)

    def body(w_ref, g_ref, m_ref, v_ref, go_ref, d_ref, nm_ref, nv_ref):
        go_ref[...] = g_ref[...]
        _adamw_math(w_ref, g_ref, m_ref, v_ref, d_ref, nm_ref, nv_ref)

    blk = pl.BlockSpec((tr, C), lambda i: (i, 0))
    return pl.pallas_call(
        body, name=name, grid=(R // tr,), in_specs=[blk] * 4, out_specs=[blk] * 4,
        out_shape=[jax.ShapeDtypeStruct((R, C), F32)] * 4, compiler_params=_params("parallel"),
    )(w, g, m, v)


def _pack_small(d, prefix=""):
    flat = jnp.concatenate([d[prefix + n].astype(F32).reshape(-1) for n in SMALL])
    pad = (-flat.shape[0]) % (2 * 16 * LANES)
    return jnp.pad(flat, (0, pad)).reshape(-1, LANES)


def _unpack_small(packed, shapes):
    flat = packed.reshape(-1)
    out, off = {}, 0
    for n in SMALL:
        size = int(np.prod(shapes[n]))
        out[n] = flat[off:off + size].reshape(shapes[n])
        off += size
    return out


def kernel(*args):
    p = dict(zip(INPUTS, args))
    x, mem, target = p["x"][0], p["mem"][0], p["loss_target"][0]

    names = [n for n, _ in BIG] + ["small"]
    core = lax.axis_index("c").astype(jnp.int32).reshape(1)
    chip = (2 * lax.axis_index("x") + lax.axis_index("y")).astype(jnp.int32).reshape(1)
    placed = [_place_shard(p[n], ax, chip, f"place_{n}") for n, ax in BIG]
    wb = dict(zip(names, _gather_shards(placed, [ax for _, ax in BIG], "gather_weights")))

    loss_part, dx, grads = _local_step(x, mem, target, p, wb)
    loss = lax.psum(loss_part, ("x", "y", "c"))

    stacked = [grads[n] for n, _ in BIG] + [_pack_small(grads).reshape(2, -1, LANES)]
    theirs = _swap_layers(stacked, "swap_layers")
    pair = [_pair_sum(s, o, core, f"pair_sum_{n}") for n, s, o in zip(names, stacked, theirs)]
    landed = _scatter_slices(pair, [ax - 1 for _, ax in BIG] + [None], "scatter_grads")
    reduced = [_sum_chips(ld.reshape(N_CHIPS, -1, ld.shape[-1]), core, f"sum_chips_{n}") for n, ld in zip(names, landed)]
    total = _merge_layers(reduced, "merge_layers")

    out = {}
    for (n, _), g in zip(BIG, total):
        sh = p[n].shape
        two_d = lambda a: a.reshape(-1, sh[-1])
        res = _adamw(two_d(p[n]), two_d(g), two_d(p["m_" + n]), two_d(p["v_" + n]), f"adamw_{n}")
        for key, r in zip(("grad_", "delta_", "new_m_", "new_v_"), res):
            out[key + n] = r.reshape(sh)
    for n, g in _unpack_small(total[-1], {n: p[n].shape for n in SMALL}).items():
        res = (g,) + tuple(_adamw_whole(p[n], g, p["m_" + n], p["v_" + n], f"adamw_{n}"))
        for key, r in zip(("grad_", "delta_", "new_m_", "new_v_"), res):
            out[key + n] = r

    result = [loss, dx.reshape(p["x"].shape)]
    for key in ("grad_", "delta_", "new_m_", "new_v_"):
        result += [out[key + n] for n in WEIGHTS]
    return tuple(result)
```
